```python
import math
import jax, jax.numpy as jnp
from jax import lax
import numpy as np

D_MODEL = 1024
BATCH = 16
SEQ = 256
DEPTH = 2
DEC_BATCH = 2
DEC_SEQ = 4096
PAST_LEN = 256

GRID_W = 64
HY_CH = D_MODEL // 2
FILT_BANDS = 16
FILT_EMB = 1 + 2 * FILT_BANDS
FILT_ORDER = 64
FAST_DECAY_PCT = 0.3
SLOW_DECAY_PCT = 1.5
DECAY_TARGET = 1e-2
MAX_DECAY = math.log(DECAY_TARGET) / FAST_DECAY_PCT
MIN_DECAY = math.log(DECAY_TARGET) / SLOW_DECAY_PCT
SHORT_CONV = 3
MLA_HEADS = 4
QK_NOPE = 128
QK_ROPE = 64
V_HEAD = 128
Q_LORA = D_MODEL // 4
KV_LORA = D_MODEL // 8
ROPE_THETA = 10000.0
Q_BLOCK = 128
FN_GROUPS = 8
FN_GROUP_CH = D_MODEL // FN_GROUPS
D_FF = 4 * D_MODEL
ALPHA = (2 * DEPTH) ** 0.25
BETA = (8 * DEPTH) ** -0.25
LN_EPS = 1e-5
RMS_EPS = 1e-6
W_IN0 = 3 * HY_CH + Q_LORA + KV_LORA + QK_ROPE
MIX_OUT0 = HY_CH + MLA_HEADS * V_HEAD

kernel_name = 'hyena_mla_fnet_diffusion_step'


def layer_norm_plain(x):
    xf = x.astype(jnp.float32)
    mu = jnp.mean(xf, axis=-1, keepdims=True)
    var = jnp.mean(jnp.square(xf - mu), axis=-1, keepdims=True)
    return (xf - mu) * lax.rsqrt(var + LN_EPS)


def layer_norm(x, g, b):
    y = layer_norm_plain(x) * g.astype(jnp.float32) + b.astype(jnp.float32)
    return y.astype(x.dtype)


def rms_norm(x, g):
    xf = x.astype(jnp.float32)
    y = xf * lax.rsqrt(jnp.mean(jnp.square(xf), axis=-1, keepdims=True) + RMS_EPS)
    return (y * g.astype(jnp.float32)).astype(x.dtype)


def modulation(cond, w, b):
    m = jax.nn.silu(cond) @ w + b
    return jnp.split(m[:, None, :], 6, axis=-1)


def adaln(x, shift, scale):
    return (layer_norm_plain(x) * (1.0 + scale.astype(jnp.float32)) + shift.astype(jnp.float32)).astype(x.dtype)


def post_residual(x, out, gate, g, b):
    return layer_norm(ALPHA * x + gate * out, g, b)


def short_conv3(x, w, b):
    xp = jnp.pad(x, ((0, 0), (1, 1), (0, 0)))
    return xp[:, :-2] * w[0] + xp[:, 1:-1] * w[1] + xp[:, 2:] * w[2] + b


def hyena_filters(L, w1, b1, freq, w2, b2, w3):
    f32 = jnp.float32
    t = jnp.linspace(0.0, 1.0, L, dtype=f32)[:, None]
    w_ang = 2.0 * math.pi * jnp.arange(L, dtype=f32) / L
    bands = jnp.linspace(1e-4, FILT_BANDS - 1, FILT_BANDS, dtype=f32)
    ang = w_ang[:, None] * bands[None, :]
    z = jnp.concatenate([t, jnp.cos(ang), -jnp.sin(ang)], axis=-1)
    fr = freq.astype(f32)
    h = jnp.sin(fr * (z @ w1.astype(f32) + b1.astype(f32)))
    h = jnp.sin(fr * (h @ w2.astype(f32) + b2.astype(f32)))
    h = h @ w3.astype(f32)
    deltas = jnp.abs(jnp.linspace(MIN_DECAY, MAX_DECAY, HY_CH, dtype=f32))
    decay = jnp.exp(-t * deltas[None, :])
    h = h.reshape(L, 2, HY_CH) * decay[:, None, :]
    h = h / jnp.sum(jnp.abs(h), axis=(0, 1), keepdims=True)
    return h[:, 0], h[:, 1]


def long_conv_bidir(u, h_fwd, h_bwd, skip):
    L = u.shape[1]
    uf = u.astype(jnp.float32)
    k = jnp.concatenate([h_fwd, jnp.zeros((1, HY_CH), jnp.float32), h_bwd[1:][::-1]], axis=0)
    U = jnp.fft.rfft(uf, n=2 * L, axis=1)
    K = jnp.fft.rfft(k, n=2 * L, axis=0)
    y = jnp.fft.irfft(U * K[None], n=2 * L, axis=1)[:, :L]
    return y + uf * skip.astype(jnp.float32)


def hyena_mixer(p, conv_w, conv_b, hf_w1, hf_b1, hf_freq, hf_w2, hf_b2, hf_w3, hf_skip):
    L = p.shape[1]
    p = short_conv3(p, conv_w, conv_b)
    x0, x1, v = jnp.split(p, 3, axis=-1)
    h_fwd, h_bwd = hyena_filters(L, hf_w1, hf_b1, hf_freq, hf_w2, hf_b2, hf_w3)
    v = long_conv_bidir(v * x1, h_fwd, h_bwd, hf_skip).astype(p.dtype)
    return v * x0


def grid_rope_tables(L):
    rows = L // GRID_W
    f32 = jnp.float32
    row = jnp.repeat(jnp.arange(rows, dtype=f32), GRID_W)
    col = jnp.tile(jnp.arange(GRID_W, dtype=f32), rows)
    half = QK_ROPE // 2
    inv = 1.0 / (ROPE_THETA ** (jnp.arange(0, half, 2, dtype=f32) / half))
    ar = row[:, None] * inv[None, :]
    ac = col[:, None] * inv[None, :]
    ang = jnp.stack([ar, ar, ac, ac], axis=1).reshape(L, QK_ROPE)
    return jnp.cos(ang), jnp.sin(ang)


def apply_rope(x, cos, sin):
    xs = x.reshape(x.shape[:-1] + (2, 2, QK_ROPE // 4))
    x1, x2 = xs[..., 0, :], xs[..., 1, :]
    rot = jnp.stack([-x2, x1], axis=-2).reshape(x.shape)
    return (x * cos + rot * sin).astype(x.dtype)


def block_attention(q, k, v):
    b, lq, h, dk = q.shape
    nb = lq // Q_BLOCK
    scale = 1.0 / math.sqrt(dk)
    qb = q.reshape(b, nb, Q_BLOCK, h, dk).transpose(1, 0, 2, 3, 4)

    def attend(q_blk):
        s = jnp.einsum('bqhd,bkhd->bhqk', q_blk, k).astype(jnp.float32) * scale
        p = jax.nn.softmax(s, axis=-1)
        return jnp.einsum('bhqk,bkhd->bqhd', p.astype(v.dtype), v)

    ob = lax.map(attend, qb)
    return ob.transpose(1, 0, 2, 3, 4).reshape(b, lq, h, v.shape[-1])


def mla_query(q_c, q_norm, q_up):
    b, l, _ = q_c.shape
    q = (rms_norm(q_c, q_norm) @ q_up).reshape(b, l, MLA_HEADS, QK_NOPE + QK_ROPE)
    return q[..., :QK_NOPE], q[..., QK_NOPE:]


def mla_attend(q_nope, q_pe, kv_n, k_pe, kv_up):
    b, lk, _ = kv_n.shape
    kv = (kv_n @ kv_up).reshape(b, lk, MLA_HEADS, QK_NOPE + V_HEAD)
    k_nope, v = kv[..., :QK_NOPE], kv[..., QK_NOPE:]
    k_rope = jnp.broadcast_to(k_pe[:, :, None, :], (b, lk, MLA_HEADS, QK_ROPE)).astype(k_nope.dtype)
    k = jnp.concatenate([k_nope, k_rope], axis=-1)
    q = jnp.concatenate([q_nope, q_pe.astype(q_nope.dtype)], axis=-1)
    o = block_attention(q, k, v)
    return o.reshape(o.shape[0], o.shape[1], MLA_HEADS * V_HEAD)


def ab_front(x, mods, w_in, hyena_p):
    h = adaln(x, mods[0], mods[1])
    z = h @ w_in
    hy, q_c, kv_c, k_pe = jnp.split(z, [3 * HY_CH, 3 * HY_CH + Q_LORA, 3 * HY_CH + Q_LORA + KV_LORA], axis=-1)
    y_hy = hyena_mixer(hy, *hyena_p)
    return y_hy, q_c, kv_c, k_pe


def layer_ab_context(x, mods, w_in, hyena_p, mla_p, w_out, g, b):
    q_norm, q_up, kv_norm, kv_up = mla_p
    y_hy, q_c, kv_c, k_pe = ab_front(x, mods, w_in, hyena_p)
    q_nope, q_pe = mla_query(q_c, q_norm, q_up)
    kv_n = rms_norm(kv_c, kv_norm)
    y_mla = mla_attend(q_nope, q_pe, kv_n, k_pe, kv_up)
    out = jnp.concatenate([y_hy, y_mla.astype(y_hy.dtype)], axis=-1) @ w_out
    return post_residual(x, out, mods[2], g, b), kv_n, k_pe


def layer_ab_latent(x, mods, ctx_ckv, ctx_krope, w_in, hyena_p, mla_p, w_out, g, b):
    q_norm, q_up, kv_norm, kv_up = mla_p
    L = x.shape[1]
    y_hy, q_c, kv_c, k_pe = ab_front(x, mods, w_in, hyena_p)
    cos, sin = grid_rope_tables(L)
    q_nope, q_pe = mla_query(q_c, q_norm, q_up)
    q_pe = apply_rope(q_pe, cos[None, :, None, :], sin[None, :, None, :])
    k_pe = apply_rope(k_pe, cos[None], sin[None])
    kv_n = rms_norm(kv_c, kv_norm)
    kv_all = jnp.concatenate([kv_n, ctx_ckv.astype(kv_n.dtype)], axis=1)
    kpe_all = jnp.concatenate([k_pe, ctx_krope.astype(k_pe.dtype)], axis=1)
    y_mla = mla_attend(q_nope, q_pe, kv_all, kpe_all, kv_up)
    out = jnp.concatenate([y_hy, y_mla.astype(y_hy.dtype)], axis=-1) @ w_out
    return post_residual(x, out, mods[2], g, b)


def fourier_mix(h):
    b, l, d = h.shape
    hg = h.astype(jnp.float32).reshape(b, l, FN_GROUPS, FN_GROUP_CH)
    y = jnp.fft.fft2(hg, axes=(1, 3), norm='ortho').real
    return y.reshape(b, l, d).astype(h.dtype)


def layer_c(x, mods, w_out, g, b):
    h = adaln(x, mods[0], mods[1])
    out = fourier_mix(h) @ w_out
    return post_residual(x, out, mods[2], g, b)


def channel_mixer(x, mods, w1, w2, g, b):
    h = adaln(x, mods[3], mods[4])
    out = jnp.square(jax.nn.relu(h @ w1)) @ w2
    return post_residual(x, out, mods[5], g, b)


def setup_inputs(seed: int = 0) -> dict:
    key = jax.random.key(seed)
    ks = iter(jax.random.split(key, 48))
    f32 = jnp.float32

    def nrm(shape, scale):
        return jax.random.normal(next(ks), shape, f32) * scale

    def gain(n):
        return 1.0 + nrm((n,), 0.02)

    d = D_MODEL
    inp = {}
    inp['x_prompt'] = nrm((BATCH, SEQ, d), 1.0)
    inp['x_sample'] = nrm((DEC_BATCH, DEC_SEQ, d), 1.0)
    inp['cache_l0_ckv'] = nrm((DEC_BATCH, PAST_LEN, KV_LORA), 1.0)
    inp['cache_l0_krope'] = nrm((DEC_BATCH, PAST_LEN, QK_ROPE), 1.0)
    inp['c'] = nrm((DEC_BATCH, d), 1.0)
    inp['c_ctx'] = nrm((d,), 1.0)
    inp['l0_ada_w'] = nrm((d, 6 * d), d ** -0.5)
    inp['l0_ada_b'] = nrm((6 * d,), 0.02)
    inp['l0_w_in'] = nrm((d, W_IN0), d ** -0.5)
    inp['l0_conv_w'] = nrm((SHORT_CONV, 3 * HY_CH), SHORT_CONV ** -0.5)
    inp['l0_conv_b'] = nrm((3 * HY_CH,), 0.02)
    inp['l0_hf_w1'] = nrm((FILT_EMB, FILT_ORDER), FILT_EMB ** -0.5)
    inp['l0_hf_b1'] = nrm((FILT_ORDER,), 0.1)
    inp['l0_hf_freq'] = 1.0 + nrm((FILT_ORDER,), 0.1)
    inp['l0_hf_w2'] = nrm((FILT_ORDER, FILT_ORDER), FILT_ORDER ** -0.5)
    inp['l0_hf_b2'] = nrm((FILT_ORDER,), 0.1)
    inp['l0_hf_w3'] = nrm((FILT_ORDER, 2 * HY_CH), FILT_ORDER ** -0.5)
    inp['l0_hf_skip'] = nrm((HY_CH,), 0.5)
    inp['l0_q_norm'] = gain(Q_LORA)
    inp['l0_q_up'] = nrm((Q_LORA, MLA_HEADS * (QK_NOPE + QK_ROPE)), Q_LORA ** -0.5)
    inp['l0_kv_norm'] = gain(KV_LORA)
    inp['l0_kv_up'] = nrm((KV_LORA, MLA_HEADS * (QK_NOPE + V_HEAD)), KV_LORA ** -0.5)
    inp['l0_w_out'] = nrm((MIX_OUT0, d), BETA * MIX_OUT0 ** -0.5)
    inp['l0_ln1_g'] = gain(d)
    inp['l0_ln1_b'] = nrm((d,), 0.02)
    inp['l0_mlp_w1'] = nrm((d, D_FF), d ** -0.5)
    inp['l0_mlp_w2'] = nrm((D_FF, d), BETA * D_FF ** -0.5)
    inp['l0_ln2_g'] = gain(d)
    inp['l0_ln2_b'] = nrm((d,), 0.02)
    inp['l1_ada_w'] = nrm((d, 6 * d), d ** -0.5)
    inp['l1_ada_b'] = nrm((6 * d,), 0.02)
    inp['l1_w_out'] = nrm((d, d), BETA * d ** -0.5)
    inp['l1_ln1_g'] = gain(d)
    inp['l1_ln1_b'] = nrm((d,), 0.02)
    inp['l1_mlp_w1'] = nrm((d, D_FF), d ** -0.5)
    inp['l1_mlp_w2'] = nrm((D_FF, d), BETA * D_FF ** -0.5)
    inp['l1_ln2_g'] = gain(d)
    inp['l1_ln2_b'] = nrm((d,), 0.02)
    return inp


def reference(x_prompt, x_sample, cache_l0_ckv, cache_l0_krope, c, c_ctx,
              l0_ada_w, l0_ada_b, l0_w_in, l0_conv_w, l0_conv_b,
              l0_hf_w1, l0_hf_b1, l0_hf_freq, l0_hf_w2, l0_hf_b2, l0_hf_w3, l0_hf_skip,
              l0_q_norm, l0_q_up, l0_kv_norm, l0_kv_up, l0_w_out,
              l0_ln1_g, l0_ln1_b, l0_mlp_w1, l0_mlp_w2, l0_ln2_g, l0_ln2_b,
              l1_ada_w, l1_ada_b, l1_w_out, l1_ln1_g, l1_ln1_b,
              l1_mlp_w1, l1_mlp_w2, l1_ln2_g, l1_ln2_b):
    hyena_p = (l0_conv_w, l0_conv_b, l0_hf_w1, l0_hf_b1, l0_hf_freq, l0_hf_w2, l0_hf_b2, l0_hf_w3, l0_hf_skip)
    mla_p = (l0_q_norm, l0_q_up, l0_kv_norm, l0_kv_up)
    ada_w = (l0_ada_w, l1_ada_w)
    ada_b = (l0_ada_b, l1_ada_b)
    ln1 = ((l0_ln1_g, l0_ln1_b), (l1_ln1_g, l1_ln1_b))
    mlp = ((l0_mlp_w1, l0_mlp_w2), (l1_mlp_w1, l1_mlp_w2))
    ln2 = ((l0_ln2_g, l0_ln2_b), (l1_ln2_g, l1_ln2_b))

    xc, xs = x_prompt, x_sample
    ctx_ckv = None
    ctx_krope = None
    for l in range(DEPTH):
        mc = modulation(c_ctx[None, :], ada_w[l], ada_b[l])
        ms = modulation(c, ada_w[l], ada_b[l])
        g1, b1 = ln1[l]
        if l % 2 == 0:
            xc, ctx_ckv, ctx_krope = layer_ab_context(xc, mc, l0_w_in, hyena_p, mla_p, l0_w_out, g1, b1)
            xs = layer_ab_latent(xs, ms, cache_l0_ckv, cache_l0_krope, l0_w_in, hyena_p, mla_p, l0_w_out, g1, b1)
        else:
            xc = layer_c(xc, mc, l1_w_out, g1, b1)
            xs = layer_c(xs, ms, l1_w_out, g1, b1)
        w1, w2 = mlp[l]
        g2, b2 = ln2[l]
        xc = channel_mixer(xc, mc, w1, w2, g2, b2)
        xs = channel_mixer(xs, ms, w1, w2, g2, b2)
    return (xc, xs, ctx_ckv, ctx_krope)
```

```python
import functools
import math

import numpy as np
import jax
import jax.numpy as jnp
from jax import lax
from jax.experimental import pallas as pl
from jax.experimental.pallas import tpu as pltpu

F32 = jnp.float32
BF16 = jnp.bfloat16
HI = lax.Precision.HIGHEST

D_MODEL = 1024
DEPTH = 2
GRID_W = 64
HY_CH = 512
FILT_BANDS = 16
FILT_ORDER = 64
FAST_DECAY_PCT = 0.3
SLOW_DECAY_PCT = 1.5
DECAY_TARGET = 1e-2
MAX_DECAY = math.log(DECAY_TARGET) / FAST_DECAY_PCT
MIN_DECAY = math.log(DECAY_TARGET) / SLOW_DECAY_PCT
MLA_HEADS = 4
QK_NOPE = 128
QK_ROPE = 64
V_HEAD = 128
Q_LORA = 256
KV_LORA = 128
ROPE_THETA = 10000.0
FN_GROUP_CH = 128
D_FF = 4096
ALPHA = (2 * DEPTH) ** 0.25
LN_EPS = 1e-5
RMS_EPS = 1e-6

LANE = 128
ROW_TILE = 256
QK_PAD = 256
VMEM_LIMIT = 56 * 1024 * 1024


def _cparams(sem):
    return pltpu.CompilerParams(dimension_semantics=sem, vmem_limit_bytes=VMEM_LIMIT)


def _ln_plain(x):
    mu = jnp.mean(x, axis=-1, keepdims=True)
    xc = x - mu
    var = jnp.mean(xc * xc, axis=-1, keepdims=True)
    return xc * lax.rsqrt(var + LN_EPS)


def _rms(x, g):
    return x * lax.rsqrt(jnp.mean(x * x, axis=-1, keepdims=True) + RMS_EPS) * g


def _bdot(a, b):
    return jnp.dot(a.astype(BF16), b, preferred_element_type=F32)


def _mod_kernel(c_ref, w_ref, b_ref, o_ref):
    c = c_ref[...]
    s = c / (1.0 + jnp.exp(-c))
    o_ref[...] = jnp.dot(s, w_ref[...], precision=HI, preferred_element_type=F32) + b_ref[...]


def _modulation(cond8, w, b):
    n = w.shape[1]
    tn = 1536
    out = pl.pallas_call(
        _mod_kernel,
        out_shape=jax.ShapeDtypeStruct((8, n), F32),
        grid=(n // tn,),
        in_specs=[pl.BlockSpec((8, D_MODEL), lambda j: (0, 0)),
                  pl.BlockSpec((D_MODEL, tn), lambda j: (0, j)),
                  pl.BlockSpec((1, tn), lambda j: (0, j))],
        out_specs=pl.BlockSpec((8, tn), lambda j: (0, j)),
        compiler_params=_cparams(("arbitrary",)),
        name="modulation",
    )(cond8, w, b.reshape(1, n))
    return out.reshape(8, 1, n)


def _mod_spec(mod_base, tiles_per_mod):
    return pl.BlockSpec((None, 1, 6 * D_MODEL), lambda i: (mod_base + i // tiles_per_mod, 0, 0))


def _const_spec(shape):
    nd = len(shape)
    return pl.BlockSpec(shape, lambda i: (0,) * nd)


def _front_kernel(x_ref, m_ref, win_ref, qn_ref, qup_ref, kvn_ref, kvup_ref, cos_ref, sin_ref,
                  hy_ref, q_ref, k_ref, v_ref, kvn_out_ref, kpe_ref):
    m = m_ref[...]
    h = _ln_plain(x_ref[...]) * (1.0 + m[:, D_MODEL:2 * D_MODEL]) + m[:, 0:D_MODEL]
    z = _bdot(h, win_ref[...])
    hy_ref[...] = z[:, :3 * HY_CH]
    q_c = z[:, 1536:1792]
    kv_c = z[:, 1792:1920]
    cos = cos_ref[...]
    sin = sin_ref[...]
    kpe = z[:, 1920:2048] * cos + z[:, 2048:2176] * sin
    kpe_ref[...] = kpe[:, :QK_ROPE]
    kpe_b = kpe.astype(BF16)
    q = _bdot(_rms(q_c, qn_ref[...]), qup_ref[...]) * (1.0 / math.sqrt(QK_NOPE + QK_ROPE))
    kvn = _rms(kv_c, kvn_ref[...])
    kvn_out_ref[...] = kvn
    kv = _bdot(kvn, kvup_ref[...])
    for hd in range(MLA_HEADS):
        a = hd * LANE
        q_pe = (q[:, 512 + a:512 + a + LANE] * cos + q[:, 1024 + a:1024 + a + LANE] * sin).astype(BF16)
        q_ref[hd] = jnp.concatenate([q[:, a:a + LANE].astype(BF16), q_pe], axis=-1)
        k_ref[hd] = jnp.concatenate([kv[:, 2 * a:2 * a + LANE].astype(BF16), kpe_b], axis=-1)
        v_ref[hd] = kv[:, 2 * a + LANE:2 * a + 2 * LANE].astype(BF16)


def _front(x, mods, mod_base, tiles_per_mod, w, cos, sin, tiles_per_seq):
    t = x.shape[0]
    tm = ROW_TILE
    win, qn, qup, kvn, kvup = w
    if tiles_per_seq == 1:
        tab_spec = pl.BlockSpec((tm, LANE), lambda i: (0, 0))
    else:
        tab_spec = pl.BlockSpec((tm, LANE), lambda i: (i % tiles_per_seq, 0))
    return pl.pallas_call(
        _front_kernel,
        out_shape=(jax.ShapeDtypeStruct((t, 3 * HY_CH), F32),
                   jax.ShapeDtypeStruct((MLA_HEADS, t, QK_PAD), BF16),
                   jax.ShapeDtypeStruct((MLA_HEADS, t, QK_PAD), BF16),
                   jax.ShapeDtypeStruct((MLA_HEADS, t, V_HEAD), BF16),
                   jax.ShapeDtypeStruct((t, KV_LORA), F32),
                   jax.ShapeDtypeStruct((t, QK_ROPE), F32)),
        grid=(t // tm,),
        in_specs=[pl.BlockSpec((tm, D_MODEL), lambda i: (i, 0)),
                  _mod_spec(mod_base, tiles_per_mod),
                  _const_spec(win.shape), _const_spec(qn.shape), _const_spec(qup.shape),
                  _const_spec(kvn.shape), _const_spec(kvup.shape),
                  tab_spec, tab_spec],
        out_specs=(pl.BlockSpec((tm, 3 * HY_CH), lambda i: (i, 0)),
                   pl.BlockSpec((MLA_HEADS, tm, QK_PAD), lambda i: (0, i, 0)),
                   pl.BlockSpec((MLA_HEADS, tm, QK_PAD), lambda i: (0, i, 0)),
                   pl.BlockSpec((MLA_HEADS, tm, V_HEAD), lambda i: (0, i, 0)),
                   pl.BlockSpec((tm, KV_LORA), lambda i: (i, 0)),
                   pl.BlockSpec((tm, QK_ROPE), lambda i: (i, 0))),
        compiler_params=_cparams(("arbitrary",)),
        name="l0_front",
    )(x, mods, win, qn, qup, kvn, kvup, cos, sin)


def _cache_kv_kernel(ckv_ref, kr_ref, kvup_ref, k_ref, v_ref):
    kv = _bdot(ckv_ref[...], kvup_ref[...])
    kr = kr_ref[...].astype(BF16)
    for hd in range(MLA_HEADS):
        a = 2 * hd * LANE
        k_ref[hd] = jnp.concatenate([kv[:, a:a + LANE].astype(BF16), kr], axis=-1)
        v_ref[hd] = kv[:, a + LANE:a + 2 * LANE].astype(BF16)


def _cache_kv(ckv, krope_pad, kvup):
    t = ckv.shape[0]
    return pl.pallas_call(
        _cache_kv_kernel,
        out_shape=(jax.ShapeDtypeStruct((MLA_HEADS, t, QK_PAD), BF16),
                   jax.ShapeDtypeStruct((MLA_HEADS, t, V_HEAD), BF16)),
        name="l0_cache_kv",
    )(ckv, krope_pad, kvup)


def _attn_kernel(q_ref, k_ref, v_ref, o_ref):
    s = lax.dot_general(q_ref[...], k_ref[...], (((1,), (1,)), ((), ())),
                        preferred_element_type=F32)
    m = jnp.max(s, axis=-1, keepdims=True)
    p = jnp.exp(s - m)
    l = jnp.sum(p, axis=-1, keepdims=True)
    o = jnp.dot(p.astype(BF16), v_ref[...], preferred_element_type=F32)
    o_ref[...] = (o / l).astype(o_ref.dtype)


def _attention(q, k, v, nb, lq, tq):
    lk = k.shape[2]
    nq = lq // tq
    return pl.pallas_call(
        _attn_kernel,
        out_shape=jax.ShapeDtypeStruct((nb * lq, MLA_HEADS * V_HEAD), BF16),
        grid=(nb, MLA_HEADS, nq),
        in_specs=[pl.BlockSpec((None, tq, QK_PAD), lambda b, h, i: (h, b * nq + i, 0)),
                  pl.BlockSpec((None, None, lk, QK_PAD), lambda b, h, i: (h, b, 0, 0)),
                  pl.BlockSpec((None, None, lk, V_HEAD), lambda b, h, i: (h, b, 0, 0))],
        out_specs=pl.BlockSpec((tq, V_HEAD), lambda b, h, i: (b * nq + i, h)),
        compiler_params=_cparams(("arbitrary", "arbitrary", "arbitrary")),
        name="l0_attention",
    )(q, k, v)


def _conv_gate_kernel(tiles_per_seq, hy_ref, prev_ref, next_ref, w_ref, b_ref, skip_ref,
                      u_ref, e_ref, x0_ref):
    i = pl.program_id(0)
    x = hy_ref[...]
    tm = x.shape[0]
    pos = i % tiles_per_seq
    prev_row = jnp.where(pos == 0, 0.0, prev_ref[7:8, :])
    next_row = jnp.where(pos == tiles_per_seq - 1, 0.0, next_ref[0:1, :])
    rows = lax.broadcasted_iota(jnp.int32, x.shape, 0)
    xm1 = jnp.where(rows == 0, prev_row, pltpu.roll(x, 1, 0))
    xp1 = jnp.where(rows == tm - 1, next_row, pltpu.roll(x, tm - 1, 0))
    w = w_ref[...]
    p = xm1 * w[0:1, :] + x * w[1:2, :] + xp1 * w[2:3, :] + b_ref[...]
    u = p[:, 2 * HY_CH:] * p[:, HY_CH:2 * HY_CH]
    u_ref[...] = u.astype(BF16)
    e_ref[...] = u * skip_ref[...]
    x0_ref[...] = p[:, :HY_CH]


def _conv_gate(hy, conv_w, conv_b, skip, tiles_per_seq):
    t = hy.shape[0]
    tm = ROW_TILE
    r8 = tm // 8
    n8 = t // 8
    return pl.pallas_call(
        functools.partial(_conv_gate_kernel, tiles_per_seq),
        out_shape=(jax.ShapeDtypeStruct((t, HY_CH), BF16),
                   jax.ShapeDtypeStruct((t, HY_CH), F32),
                   jax.ShapeDtypeStruct((t, HY_CH), F32)),
        grid=(t // tm,),
        in_specs=[pl.BlockSpec((tm, 3 * HY_CH), lambda i: (i, 0)),
                  pl.BlockSpec((8, 3 * HY_CH), lambda i: (jnp.maximum(i * r8 - 1, 0), 0)),
                  pl.BlockSpec((8, 3 * HY_CH), lambda i: (jnp.minimum((i + 1) * r8, n8 - 1), 0)),
                  _const_spec(conv_w.shape), _const_spec(conv_b.shape), _const_spec(skip.shape)],
        out_specs=(pl.BlockSpec((tm, HY_CH), lambda i: (i, 0)),
                   pl.BlockSpec((tm, HY_CH), lambda i: (i, 0)),
                   pl.BlockSpec((tm, HY_CH), lambda i: (i, 0))),
        compiler_params=_cparams(("arbitrary",)),
        name="l0_conv_gate",
    )(hy, hy, hy, conv_w, conv_b, skip)


def _filter_kernel(z_ref, w1_ref, b1_ref, fr_ref, w2_ref, b2_ref, w3_ref, dl_ref, h_ref, norm_ref):
    p = pl.program_id(0)
    i = pl.program_id(1)
    z = z_ref[...]
    tl = z.shape[0]
    fr = fr_ref[...]
    h = jnp.sin(fr * (jnp.dot(z, w1_ref[...], precision=HI, preferred_element_type=F32) + b1_ref[...]))
    h = jnp.sin(fr * (jnp.dot(h, w2_ref[...], precision=HI, preferred_element_type=F32) + b2_ref[...]))
    h = jnp.dot(h, w3_ref[...], precision=HI, preferred_element_type=F32)
    decay = jnp.exp(-(z[:, 0:1] * dl_ref[...]))
    hf = h[:, :HY_CH] * decay
    hb = h[:, HY_CH:] * decay

    @pl.when(jnp.logical_and(p == 0, i == 0))
    def _():
        norm_ref[...] = jnp.zeros_like(norm_ref)

    @pl.when(p == 0)
    def _():
        norm_ref[...] += jnp.sum(jnp.abs(hf) + jnp.abs(hb), axis=0, keepdims=True)

    @pl.when(p == 1)
    def _():
        nrm = norm_ref[...]
        rows = lax.broadcasted_iota(jnp.int32, hb.shape, 0) + i * tl
        h_ref[0] = (hf / nrm).astype(BF16)
        h_ref[1] = jnp.where(rows == 0, 0.0, hb / nrm).astype(BF16)


def _filter_embedding(L):
    t = np.linspace(0.0, 1.0, L)[:, None]
    w_ang = 2.0 * np.pi * np.arange(L) / L
    bands = np.linspace(1e-4, FILT_BANDS - 1, FILT_BANDS)
    ang = w_ang[:, None] * bands[None, :]
    z = np.zeros((L, LANE), np.float64)
    z[:, 0:1] = t
    z[:, 1:1 + FILT_BANDS] = np.cos(ang)
    z[:, 1 + FILT_BANDS:1 + 2 * FILT_BANDS] = -np.sin(ang)
    return jnp.asarray(z, F32)


def _filters(L, w1p, b1, fr, w2, b2, w3):
    tl = min(L, 512)
    z = _filter_embedding(L)
    deltas = jnp.asarray(np.abs(np.linspace(MIN_DECAY, MAX_DECAY, HY_CH))[None, :], F32)
    c2 = lambda shape: pl.BlockSpec(shape, lambda p, i: (0, 0))
    return pl.pallas_call(
        _filter_kernel,
        out_shape=jax.ShapeDtypeStruct((2, L, HY_CH), BF16),
        grid=(2, L // tl),
        in_specs=[pl.BlockSpec((tl, LANE), lambda p, i: (i, 0)),
                  c2(w1p.shape), c2(b1.shape), c2(fr.shape), c2(w2.shape), c2(b2.shape),
                  c2(w3.shape), c2(deltas.shape)],
        out_specs=pl.BlockSpec((2, tl, HY_CH), lambda p, i: (0, i * p, 0)),
        scratch_shapes=[pltpu.VMEM((1, HY_CH), F32)],
        compiler_params=_cparams(("arbitrary", "arbitrary")),
        name="l0_hyena_filters",
    )(z, w1p, b1, fr, w2, b2, w3, deltas)


def _dft_tables(kind, L, ti):
    ni = L // ti
    i = np.arange(ti, dtype=np.int64)[:, None]
    big = (np.arange(ni, dtype=np.int64) * ti)[:, None]
    c = np.arange(L, dtype=np.int64)[None, :]
    if kind == "hy_fwd":
        period = 4 * L
        base_idx = (2 * i + 1) * c
        r_idx = 2 * big * c
        scale = 1.0
    elif kind == "hy_inv":
        period = 4 * L
        base_idx = (2 * c + 1) * i
        r_idx = (2 * c + 1) * big
        scale = 1.0 / L
    else:
        period = L
        base_idx = i * c
        r_idx = big * c
        scale = 1.0 / math.sqrt(L * FN_GROUP_CH)
    ab = 2.0 * np.pi * (base_idx % period) / period
    ar = 2.0 * np.pi * (r_idx % period) / period
    return (jnp.asarray(np.cos(ab), F32), jnp.asarray(np.sin(ab), F32),
            jnp.asarray(scale * np.cos(ar), F32).reshape(ni, 1, L),
            jnp.asarray(scale * np.sin(ar), F32).reshape(ni, 1, L))


def _dft_kernel(mode, nb, n_x, *refs):
    bc_ref, bs_ref, rc_ref, rs_ref = refs[:4]
    x_refs = refs[4:4 + n_x]
    rest = refs[4 + n_x:]
    p_ref, q_ref = rest[-2], rest[-1]
    j = pl.program_id(2)
    nj = pl.num_programs(2)
    tj = x_refs[0].shape[1]
    if bc_ref.shape[1] == tj:
        bc, bs, rc, rs = bc_ref[...], bs_ref[...], rc_ref[...], rs_ref[...]
    else:
        off = pl.multiple_of(j * tj, tj)
        bc, bs = bc_ref[:, pl.ds(off, tj)], bs_ref[:, pl.ds(off, tj)]
        rc, rs = rc_ref[:, pl.ds(off, tj)], rs_ref[:, pl.ds(off, tj)]
    tc = (bc * rc - bs * rs).astype(BF16)
    ts = (bs * rc + bc * rs).astype(BF16)
    x1_ref = x_refs[0]
    x2_ref = x_refs[-1]

    pq = [(jnp.dot(tc, x1_ref[b], preferred_element_type=F32),
           jnp.dot(ts, x2_ref[b], preferred_element_type=F32)) for b in range(nb)]

    @pl.when(j == 0)
    def _():
        for b in range(nb):
            p_ref[b] = pq[b][0]
            q_ref[b] = pq[b][1]

    @pl.when(j > 0)
    def _():
        for b in range(nb):
            p_ref[b] += pq[b][0]
            q_ref[b] += pq[b][1]

    @pl.when(j == nj - 1)
    def _():
        if mode == "filt":
            kre_ref, kim_ref = rest[0], rest[1]
            kre_ref[...] = p_ref[0] + p_ref[1]
            kim_ref[...] = q_ref[1] - q_ref[0]
        elif mode == "fwdk":
            kre, kim = rest[0][...], rest[1][...]
            yre_ref, yim_ref = rest[2], rest[3]
            for b in range(nb):
                pp, qq = p_ref[b], q_ref[b]
                yre_ref[b] = (pp * kre + qq * kim).astype(BF16)
                yim_ref[b] = (pp * kim - qq * kre).astype(BF16)
        elif mode == "inv":
            e_ref, x0_ref, o_ref = rest[0], rest[1], rest[2]
            for b in range(nb):
                o_ref[b] = ((p_ref[b] - q_ref[b] + e_ref[b]) * x0_ref[b]).astype(BF16)
        else:
            o_ref = rest[0]
            for b in range(nb):
                o_ref[b] = (p_ref[b] - q_ref[b]).astype(BF16)


def _dft(mode, kind, xs, extras, nb):
    B, L, C = xs[0].shape
    ti = min(L, 256)
    tj = min(L, 512)
    bc, bs, rc, rs = _dft_tables(kind, L, ti)
    grid = (B // nb, L // ti, L // tj)
    x_spec = pl.BlockSpec((nb, tj, C), lambda g, i, j: (g, j, 0))
    row_spec = lambda c, dt=None: pl.BlockSpec((nb, ti, c), lambda g, i, j: (g, i, 0))
    in_specs = [pl.BlockSpec((ti, L), lambda g, i, j: (0, 0)),
                pl.BlockSpec((ti, L), lambda g, i, j: (0, 0)),
                pl.BlockSpec((None, 1, L), lambda g, i, j: (i, 0, 0)),
                pl.BlockSpec((None, 1, L), lambda g, i, j: (i, 0, 0))] + [x_spec] * len(xs)
    if mode == "filt":
        out_shape = (jax.ShapeDtypeStruct((L, HY_CH), F32),) * 2
        out_specs = (pl.BlockSpec((ti, HY_CH), lambda g, i, j: (i, 0)),) * 2
    elif mode == "fwdk":
        in_specs += [pl.BlockSpec((ti, HY_CH), lambda g, i, j: (i, 0))] * 2
        out_shape = (jax.ShapeDtypeStruct((B, L, C), BF16),) * 2
        out_specs = (row_spec(C),) * 2
    elif mode == "inv":
        in_specs += [row_spec(C)] * 2
        out_shape = jax.ShapeDtypeStruct((B, L, C), BF16)
        out_specs = row_spec(C)
    else:
        out_shape = jax.ShapeDtypeStruct((B, L, C), BF16)
        out_specs = row_spec(C)
    return pl.pallas_call(
        functools.partial(_dft_kernel, mode, nb, len(xs)),
        out_shape=out_shape,
        grid=grid,
        in_specs=in_specs,
        out_specs=out_specs,
        scratch_shapes=[pltpu.VMEM((nb, ti, C), F32), pltpu.VMEM((nb, ti, C), F32)],
        compiler_params=_cparams(("arbitrary", "arbitrary", "arbitrary")),
        name="dft_" + mode,
    )(bc, bs, rc, rs, *xs, *extras)


def _fnet_front_kernel(x_ref, m_ref, cs_ref, zc_ref, zs_ref):
    m = m_ref[...]
    h = (_ln_plain(x_ref[...]) * (1.0 + m[:, D_MODEL:2 * D_MODEL]) + m[:, 0:D_MODEL]).astype(BF16)
    cs = cs_ref[...]
    for g in range(D_MODEL // FN_GROUP_CH):
        a = g * FN_GROUP_CH
        z = jnp.dot(h[:, a:a + FN_GROUP_CH], cs, preferred_element_type=F32)
        zc_ref[:, a:a + FN_GROUP_CH] = z[:, :FN_GROUP_CH].astype(BF16)
        zs_ref[:, a:a + FN_GROUP_CH] = z[:, FN_GROUP_CH:].astype(BF16)


def _fnet_front(x, mods, mod_base, tiles_per_mod):
    t = x.shape[0]
    tm = ROW_TILE
    g = FN_GROUP_CH
    jk = (np.arange(g, dtype=np.int64)[:, None] * np.arange(g, dtype=np.int64)[None, :]) % g
    ang = 2.0 * np.pi * jk / g
    cs = jnp.asarray(np.concatenate([np.cos(ang), np.sin(ang)], axis=1), F32).astype(BF16)
    return pl.pallas_call(
        _fnet_front_kernel,
        out_shape=(jax.ShapeDtypeStruct((t, D_MODEL), BF16),) * 2,
        grid=(t // tm,),
        in_specs=[pl.BlockSpec((tm, D_MODEL), lambda i: (i, 0)),
                  _mod_spec(mod_base, tiles_per_mod),
                  _const_spec(cs.shape)],
        out_specs=(pl.BlockSpec((tm, D_MODEL), lambda i: (i, 0)),) * 2,
        compiler_params=_cparams(("arbitrary",)),
        name="l1_fnet_front",
    )(x, mods, cs)


def _post_kernel(n_a, *refs):
    x_ref, m_ref = refs[0], refs[1]
    a_refs = refs[2:2 + n_a]
    wo_refs = refs[2 + n_a:2 + 2 * n_a]
    g1_ref, b1_ref, w1_ref, w2_ref, g2_ref, b2_ref, o_ref = refs[2 + 2 * n_a:]
    m = m_ref[...]
    d = D_MODEL
    out = jnp.dot(a_refs[0][...], wo_refs[0][...], preferred_element_type=F32)
    for a_ref, wo_ref in zip(a_refs[1:], wo_refs[1:]):
        out += jnp.dot(a_ref[...], wo_ref[...], preferred_element_type=F32)
    x1 = _ln_plain(ALPHA * x_ref[...] + m[:, 2 * d:3 * d] * out) * g1_ref[...] + b1_ref[...]
    h = (_ln_plain(x1) * (1.0 + m[:, 4 * d:5 * d]) + m[:, 3 * d:4 * d]).astype(BF16)
    acc = jnp.zeros_like(x1)
    for c in range(D_FF // d):
        hc = jnp.maximum(jnp.dot(h, w1_ref[:, c * d:(c + 1) * d], preferred_element_type=F32), 0.0)
        acc += jnp.dot((hc * hc).astype(BF16), w2_ref[c * d:(c + 1) * d, :], preferred_element_type=F32)
    o_ref[...] = _ln_plain(ALPHA * x1 + m[:, 5 * d:6 * d] * acc) * g2_ref[...] + b2_ref[...]


def _post(x, mods, mod_base, tiles_per_mod, a_list, wo_list, g1, b1, w1, w2, g2, b2):
    t = x.shape[0]
    tm = ROW_TILE
    row = lambda c: pl.BlockSpec((tm, c), lambda i: (i, 0))
    in_specs = ([row(D_MODEL), _mod_spec(mod_base, tiles_per_mod)]
                + [row(a.shape[1]) for a in a_list]
                + [_const_spec(w.shape) for w in wo_list]
                + [_const_spec(v.shape) for v in (g1, b1, w1, w2, g2, b2)])
    return pl.pallas_call(
        functools.partial(_post_kernel, len(a_list)),
        out_shape=jax.ShapeDtypeStruct((t, D_MODEL), F32),
        grid=(t // tm,),
        in_specs=in_specs,
        out_specs=row(D_MODEL),
        compiler_params=_cparams(("arbitrary",)),
        name="post_mlp",
    )(x, mods, *a_list, *wo_list, g1, b1, w1, w2, g2, b2)


def _rot_cols(w):
    parts = []
    for seg in range(2):
        o = seg * 32
        parts += [-w[:, o + 16:o + 32], w[:, o:o + 16]]
    return jnp.concatenate(parts, axis=1)


def _pad_cols(w, n):
    return jnp.pad(w, ((0, 0), (0, n - w.shape[1])))


def _rope_tables(L):
    rows = L // GRID_W
    row = np.repeat(np.arange(rows, dtype=np.float64), GRID_W)
    col = np.tile(np.arange(GRID_W, dtype=np.float64), rows)
    half = QK_ROPE // 2
    inv = 1.0 / (ROPE_THETA ** (np.arange(0, half, 2, dtype=np.float64) / half))
    ar = row[:, None] * inv[None, :]
    ac = col[:, None] * inv[None, :]
    ang = np.concatenate([ar, ar, ac, ac], axis=1)
    cos = np.concatenate([np.cos(ang), np.ones_like(ang)], axis=1)
    sin = np.concatenate([np.sin(ang), np.zeros_like(ang)], axis=1)
    return jnp.asarray(cos, F32), jnp.asarray(sin, F32)


def kernel(x_prompt, x_sample, cache_l0_ckv, cache_l0_krope, c, c_ctx, l0_ada_w, l0_ada_b, l0_w_in, l0_conv_w, l0_conv_b, l0_hf_w1, l0_hf_b1, l0_hf_freq, l0_hf_w2, l0_hf_b2, l0_hf_w3, l0_hf_skip, l0_q_norm, l0_q_up, l0_kv_norm, l0_kv_up, l0_w_out, l0_ln1_g, l0_ln1_b, l0_mlp_w1, l0_mlp_w2, l0_ln2_g, l0_ln2_b, l1_ada_w, l1_ada_b, l1_w_out, l1_ln1_g, l1_ln1_b, l1_mlp_w1, l1_mlp_w2, l1_ln2_g, l1_ln2_b):
    nbc, lc, d = x_prompt.shape
    nbs, ls, _ = x_sample.shape
    past = cache_l0_ckv.shape[1]
    tm = ROW_TILE
    row1 = lambda v: v.reshape(1, -1)

    cond8 = jnp.concatenate([c_ctx[None, :], c, jnp.zeros((8 - 1 - nbs, d), F32)], axis=0)
    mods0 = _modulation(cond8, l0_ada_w, l0_ada_b)
    mods1 = _modulation(cond8, l1_ada_w, l1_ada_b)

    kpe_w = l0_w_in[:, 1920:1984]
    win = jnp.concatenate([l0_w_in[:, :1920], _pad_cols(kpe_w, LANE), _pad_cols(_rot_cols(kpe_w), LANE)],
                          axis=1).astype(BF16)
    dh = QK_NOPE + QK_ROPE
    q_nope = [l0_q_up[:, h * dh:h * dh + QK_NOPE] for h in range(MLA_HEADS)]
    q_pe = [l0_q_up[:, h * dh + QK_NOPE:(h + 1) * dh] for h in range(MLA_HEADS)]
    qup = jnp.concatenate(q_nope + [_pad_cols(w, LANE) for w in q_pe]
                          + [_pad_cols(_rot_cols(w), LANE) for w in q_pe], axis=1).astype(BF16)
    kvup = l0_kv_up.astype(BF16)
    front_w = (win, row1(l0_q_norm), qup, row1(l0_kv_norm), kvup)
    w1p = jnp.pad(l0_hf_w1, ((0, LANE - l0_hf_w1.shape[0]), (0, 0)))
    filt_w = (w1p, row1(l0_hf_b1), row1(l0_hf_freq), l0_hf_w2, row1(l0_hf_b2), l0_hf_w3)
    wo0 = l0_w_out.astype(BF16)
    conv_b = row1(l0_conv_b)
    skip = row1(l0_hf_skip)

    xc = x_prompt.reshape(nbc * lc, d)
    xs = x_sample.reshape(nbs * ls, d)
    groups = (
        dict(x=xc, nb=nbc, L=lc, mod_base=0, tiles_per_mod=nbc * lc // tm, dft_nb=4, tq=lc),
        dict(x=xs, nb=nbs, L=ls, mod_base=1, tiles_per_mod=ls // tm, dft_nb=nbs, tq=256),
    )
    ones_tab = (jnp.concatenate([jnp.ones((tm, LANE), F32)], axis=0), jnp.zeros((tm, LANE), F32))

    outs = []
    ctx_ckv = ctx_krope = None
    for gi, g in enumerate(groups):
        nb, L = g["nb"], g["L"]
        tiles_per_seq = L // tm
        latent = gi == 1
        cos, sin = _rope_tables(L) if latent else ones_tab
        hy, q, k, v, kvn, kpe = _front(g["x"], mods0, g["mod_base"], g["tiles_per_mod"], front_w,
                                       cos, sin, tiles_per_seq if latent else 1)
        k = k.reshape(MLA_HEADS, nb, L, QK_PAD)
        v = v.reshape(MLA_HEADS, nb, L, V_HEAD)
        if latent:
            kc, vc = _cache_kv(cache_l0_ckv.reshape(nbs * past, KV_LORA),
                               _pad_cols(cache_l0_krope.reshape(nbs * past, QK_ROPE), LANE), kvup)
            k = jnp.concatenate([k, kc.reshape(MLA_HEADS, nb, past, QK_PAD)], axis=2)
            v = jnp.concatenate([v, vc.reshape(MLA_HEADS, nb, past, V_HEAD)], axis=2)
        else:
            ctx_ckv = kvn.reshape(nb, L, KV_LORA)
            ctx_krope = kpe.reshape(nb, L, QK_ROPE)
        y_mla = _attention(q, k, v, nb, L, g["tq"])

        u, e, x0 = _conv_gate(hy, l0_conv_w, conv_b, skip, tiles_per_seq)
        hfilt = _filters(L, *filt_w)
        kre, kim = _dft("filt", "hy_fwd", [hfilt], [], 2)
        sh = (nb, L, HY_CH)
        yre, yim = _dft("fwdk", "hy_fwd", [u.reshape(sh)], [kre, kim], g["dft_nb"])
        y_hy = _dft("inv", "hy_inv", [yre, yim], [e.reshape(sh), x0.reshape(sh)], g["dft_nb"])
        y_hy = y_hy.reshape(nb * L, HY_CH)

        x1 = _post(g["x"], mods0, g["mod_base"], g["tiles_per_mod"], [y_hy, y_mla],
                   [wo0[:HY_CH], wo0[HY_CH:]], row1(l0_ln1_g), row1(l0_ln1_b),
                   l0_mlp_w1.astype(BF16), l0_mlp_w2.astype(BF16), row1(l0_ln2_g), row1(l0_ln2_b))

        zc, zs = _fnet_front(x1, mods1, g["mod_base"], g["tiles_per_mod"])
        sh = (nb, L, d)
        yf = _dft("fnet", "fnet", [zc.reshape(sh), zs.reshape(sh)], [], min(g["dft_nb"], 2))
        x2 = _post(x1, mods1, g["mod_base"], g["tiles_per_mod"], [yf.reshape(nb * L, d)],
                   [l1_w_out.astype(BF16)], row1(l1_ln1_g), row1(l1_ln1_b),
                   l1_mlp_w1.astype(BF16), l1_mlp_w2.astype(BF16), row1(l1_ln2_g), row1(l1_ln2_b))
        outs.append(x2.reshape(nb, L, d))

    return (outs[0], outs[1], ctx_ckv, ctx_krope)
```

```python
import functools
import math

import numpy as np
import jax
import jax.numpy as jnp
from jax import lax
from jax.experimental import pallas as pl
from jax.experimental.pallas import tpu as pltpu

F32 = jnp.float32
BF16 = jnp.bfloat16
HI = lax.Precision.HIGHEST

D_MODEL = 1024
DEPTH = 2
GRID_W = 64
HY_CH = 512
FILT_BANDS = 16
FILT_ORDER = 64
FAST_DECAY_PCT = 0.3
SLOW_DECAY_PCT = 1.5
DECAY_TARGET = 1e-2
MAX_DECAY = math.log(DECAY_TARGET) / FAST_DECAY_PCT
MIN_DECAY = math.log(DECAY_TARGET) / SLOW_DECAY_PCT
MLA_HEADS = 4
QK_NOPE = 128
QK_ROPE = 64
V_HEAD = 128
Q_LORA = 256
KV_LORA = 128
ROPE_THETA = 10000.0
FN_GROUP_CH = 128
D_FF = 4096
ALPHA = (2 * DEPTH) ** 0.25
LN_EPS = 1e-5
RMS_EPS = 1e-6

LANE = 128
ROW_TILE = 256
QK_PAD = 256
VMEM_LIMIT = 56 * 1024 * 1024


def _cparams(sem):
    return pltpu.CompilerParams(dimension_semantics=sem, vmem_limit_bytes=VMEM_LIMIT)


def _ln_plain(x):
    mu = jnp.mean(x, axis=-1, keepdims=True)
    xc = x - mu
    var = jnp.mean(xc * xc, axis=-1, keepdims=True)
    return xc * lax.rsqrt(var + LN_EPS)


def _rms(x, g):
    return x * lax.rsqrt(jnp.mean(x * x, axis=-1, keepdims=True) + RMS_EPS) * g


def _bdot(a, b):
    return jnp.dot(a.astype(BF16), b, preferred_element_type=F32)


def _mod_kernel(c_ref, w_ref, b_ref, o_ref):
    c = c_ref[...]
    s = c / (1.0 + jnp.exp(-c))
    o_ref[...] = jnp.dot(s, w_ref[...], precision=HI, preferred_element_type=F32) + b_ref[...]


def _modulation(cond8, w, b):
    n = w.shape[1]
    tn = 1536
    out = pl.pallas_call(
        _mod_kernel,
        out_shape=jax.ShapeDtypeStruct((8, n), F32),
        grid=(n // tn,),
        in_specs=[pl.BlockSpec((8, D_MODEL), lambda j: (0, 0)),
                  pl.BlockSpec((D_MODEL, tn), lambda j: (0, j)),
                  pl.BlockSpec((1, tn), lambda j: (0, j))],
        out_specs=pl.BlockSpec((8, tn), lambda j: (0, j)),
        compiler_params=_cparams(("arbitrary",)),
        name="modulation",
    )(cond8, w, b.reshape(1, n))
    return out.reshape(8, 1, n)


def _mod_spec(mod_base, tiles_per_mod):
    return pl.BlockSpec((None, 1, 6 * D_MODEL), lambda i: (mod_base + i // tiles_per_mod, 0, 0))


def _const_spec(shape):
    nd = len(shape)
    return pl.BlockSpec(shape, lambda i: (0,) * nd)


def _front_kernel(x_ref, m_ref, win_ref, qn_ref, qup_ref, kvn_ref, kvup_ref, cos_ref, sin_ref,
                  hy_ref, q_ref, k_ref, v_ref, kvn_out_ref, kpe_ref):
    m = m_ref[...]
    h = _ln_plain(x_ref[...]) * (1.0 + m[:, D_MODEL:2 * D_MODEL]) + m[:, 0:D_MODEL]
    z = _bdot(h, win_ref[...])
    hy_ref[...] = z[:, :3 * HY_CH]
    q_c = z[:, 1536:1792]
    kv_c = z[:, 1792:1920]
    cos = cos_ref[...]
    sin = sin_ref[...]
    kpe = z[:, 1920:2048] * cos + z[:, 2048:2176] * sin
    kpe_ref[...] = kpe[:, :QK_ROPE]
    kpe_b = kpe.astype(BF16)
    q = _bdot(_rms(q_c, qn_ref[...]), qup_ref[...]) * (1.0 / math.sqrt(QK_NOPE + QK_ROPE))
    kvn = _rms(kv_c, kvn_ref[...])
    kvn_out_ref[...] = kvn
    kv = _bdot(kvn, kvup_ref[...])
    for hd in range(MLA_HEADS):
        a = hd * LANE
        q_pe = (q[:, 512 + a:512 + a + LANE] * cos + q[:, 1024 + a:1024 + a + LANE] * sin).astype(BF16)
        q_ref[hd] = jnp.concatenate([q[:, a:a + LANE].astype(BF16), q_pe], axis=-1)
        k_ref[hd] = jnp.concatenate([kv[:, 2 * a:2 * a + LANE].astype(BF16), kpe_b], axis=-1)
        v_ref[hd] = kv[:, 2 * a + LANE:2 * a + 2 * LANE].astype(BF16)


def _front(x, mods, mod_base, tiles_per_mod, w, cos, sin, tiles_per_seq):
    t = x.shape[0]
    tm = ROW_TILE
    win, qn, qup, kvn, kvup = w
    if tiles_per_seq == 1:
        tab_spec = pl.BlockSpec((tm, LANE), lambda i: (0, 0))
    else:
        tab_spec = pl.BlockSpec((tm, LANE), lambda i: (i % tiles_per_seq, 0))
    return pl.pallas_call(
        _front_kernel,
        out_shape=(jax.ShapeDtypeStruct((t, 3 * HY_CH), F32),
                   jax.ShapeDtypeStruct((MLA_HEADS, t, QK_PAD), BF16),
                   jax.ShapeDtypeStruct((MLA_HEADS, t, QK_PAD), BF16),
                   jax.ShapeDtypeStruct((MLA_HEADS, t, V_HEAD), BF16),
                   jax.ShapeDtypeStruct((t, KV_LORA), F32),
                   jax.ShapeDtypeStruct((t, QK_ROPE), F32)),
        grid=(t // tm,),
        in_specs=[pl.BlockSpec((tm, D_MODEL), lambda i: (i, 0)),
                  _mod_spec(mod_base, tiles_per_mod),
                  _const_spec(win.shape), _const_spec(qn.shape), _const_spec(qup.shape),
                  _const_spec(kvn.shape), _const_spec(kvup.shape),
                  tab_spec, tab_spec],
        out_specs=(pl.BlockSpec((tm, 3 * HY_CH), lambda i: (i, 0)),
                   pl.BlockSpec((MLA_HEADS, tm, QK_PAD), lambda i: (0, i, 0)),
                   pl.BlockSpec((MLA_HEADS, tm, QK_PAD), lambda i: (0, i, 0)),
                   pl.BlockSpec((MLA_HEADS, tm, V_HEAD), lambda i: (0, i, 0)),
                   pl.BlockSpec((tm, KV_LORA), lambda i: (i, 0)),
                   pl.BlockSpec((tm, QK_ROPE), lambda i: (i, 0))),
        compiler_params=_cparams(("arbitrary",)),
        name="l0_front",
    )(x, mods, win, qn, qup, kvn, kvup, cos, sin)


def _cache_kv_kernel(ckv_ref, kr_ref, kvup_ref, k_ref, v_ref):
    kv = _bdot(ckv_ref[...], kvup_ref[...])
    kr = kr_ref[...].astype(BF16)
    for hd in range(MLA_HEADS):
        a = 2 * hd * LANE
        k_ref[hd] = jnp.concatenate([kv[:, a:a + LANE].astype(BF16), kr], axis=-1)
        v_ref[hd] = kv[:, a + LANE:a + 2 * LANE].astype(BF16)


def _cache_kv(ckv, krope_pad, kvup):
    t = ckv.shape[0]
    return pl.pallas_call(
        _cache_kv_kernel,
        out_shape=(jax.ShapeDtypeStruct((MLA_HEADS, t, QK_PAD), BF16),
                   jax.ShapeDtypeStruct((MLA_HEADS, t, V_HEAD), BF16)),
        name="l0_cache_kv",
    )(ckv, krope_pad, kvup)


def _attn_kernel(q_ref, k_ref, v_ref, o_ref):
    s = lax.dot_general(q_ref[...], k_ref[...], (((1,), (1,)), ((), ())),
                        preferred_element_type=F32)
    m = jnp.max(s, axis=-1, keepdims=True)
    p = jnp.exp(s - m)
    l = jnp.sum(p, axis=-1, keepdims=True)
    o = jnp.dot(p.astype(BF16), v_ref[...], preferred_element_type=F32)
    o_ref[...] = (o / l).astype(o_ref.dtype)


def _attention(q, k, v, nb, lq, tq):
    lk = k.shape[2]
    nq = lq // tq
    return pl.pallas_call(
        _attn_kernel,
        out_shape=jax.ShapeDtypeStruct((nb * lq, MLA_HEADS * V_HEAD), BF16),
        grid=(nb, MLA_HEADS, nq),
        in_specs=[pl.BlockSpec((None, tq, QK_PAD), lambda b, h, i: (h, b * nq + i, 0)),
                  pl.BlockSpec((None, None, lk, QK_PAD), lambda b, h, i: (h, b, 0, 0)),
                  pl.BlockSpec((None, None, lk, V_HEAD), lambda b, h, i: (h, b, 0, 0))],
        out_specs=pl.BlockSpec((tq, V_HEAD), lambda b, h, i: (b * nq + i, h)),
        compiler_params=_cparams(("arbitrary", "arbitrary", "arbitrary")),
        name="l0_attention",
    )(q, k, v)


def _conv_gate_kernel(tiles_per_seq, hy_ref, prev_ref, next_ref, w_ref, b_ref, skip_ref,
                      u_ref, e_ref, x0_ref):
    i = pl.program_id(0)
    x = hy_ref[...]
    tm = x.shape[0]
    pos = i % tiles_per_seq
    prev_row = jnp.where(pos == 0, 0.0, prev_ref[7:8, :])
    next_row = jnp.where(pos == tiles_per_seq - 1, 0.0, next_ref[0:1, :])
    rows = lax.broadcasted_iota(jnp.int32, x.shape, 0)
    xm1 = jnp.where(rows == 0, prev_row, pltpu.roll(x, 1, 0))
    xp1 = jnp.where(rows == tm - 1, next_row, pltpu.roll(x, tm - 1, 0))
    w = w_ref[...]
    p = xm1 * w[0:1, :] + x * w[1:2, :] + xp1 * w[2:3, :] + b_ref[...]
    u = p[:, 2 * HY_CH:] * p[:, HY_CH:2 * HY_CH]
    u_ref[...] = u.astype(BF16)
    e_ref[...] = u * skip_ref[...]
    x0_ref[...] = p[:, :HY_CH]


def _conv_gate(hy, conv_w, conv_b, skip, tiles_per_seq):
    t = hy.shape[0]
    tm = ROW_TILE
    r8 = tm // 8
    n8 = t // 8
    return pl.pallas_call(
        functools.partial(_conv_gate_kernel, tiles_per_seq),
        out_shape=(jax.ShapeDtypeStruct((t, HY_CH), BF16),
                   jax.ShapeDtypeStruct((t, HY_CH), F32),
                   jax.ShapeDtypeStruct((t, HY_CH), F32)),
        grid=(t // tm,),
        in_specs=[pl.BlockSpec((tm, 3 * HY_CH), lambda i: (i, 0)),
                  pl.BlockSpec((8, 3 * HY_CH), lambda i: (jnp.maximum(i * r8 - 1, 0), 0)),
                  pl.BlockSpec((8, 3 * HY_CH), lambda i: (jnp.minimum((i + 1) * r8, n8 - 1), 0)),
                  _const_spec(conv_w.shape), _const_spec(conv_b.shape), _const_spec(skip.shape)],
        out_specs=(pl.BlockSpec((tm, HY_CH), lambda i: (i, 0)),
                   pl.BlockSpec((tm, HY_CH), lambda i: (i, 0)),
                   pl.BlockSpec((tm, HY_CH), lambda i: (i, 0))),
        compiler_params=_cparams(("arbitrary",)),
        name="l0_conv_gate",
    )(hy, hy, hy, conv_w, conv_b, skip)


def _filter_kernel(z_ref, w1_ref, b1_ref, fr_ref, w2_ref, b2_ref, w3_ref, dl_ref, h_ref, norm_ref):
    i = pl.program_id(0)
    z = z_ref[...]
    tl = z.shape[0]
    fr = fr_ref[...]
    h = jnp.sin(fr * (jnp.dot(z, w1_ref[...], precision=HI, preferred_element_type=F32) + b1_ref[...]))
    h = jnp.sin(fr * (jnp.dot(h, w2_ref[...], precision=HI, preferred_element_type=F32) + b2_ref[...]))
    h = _bdot(h, w3_ref[...])
    decay = jnp.exp(-(z[:, 0:1] * dl_ref[...]))
    hf = h[:, :HY_CH] * decay
    hb = h[:, HY_CH:] * decay
    part = jnp.sum(jnp.abs(hf) + jnp.abs(hb), axis=0, keepdims=True)

    @pl.when(i == 0)
    def _():
        norm_ref[...] = part

    @pl.when(i > 0)
    def _():
        norm_ref[...] += part

    rows = lax.broadcasted_iota(jnp.int32, hb.shape, 0) + i * tl
    h_ref[0] = hf.astype(BF16)
    h_ref[1] = jnp.where(rows == 0, 0.0, hb).astype(BF16)


def _filter_embedding(L):
    t = np.linspace(0.0, 1.0, L)[:, None]
    w_ang = 2.0 * np.pi * np.arange(L) / L
    bands = np.linspace(1e-4, FILT_BANDS - 1, FILT_BANDS)
    ang = w_ang[:, None] * bands[None, :]
    z = np.zeros((L, LANE), np.float64)
    z[:, 0:1] = t
    z[:, 1:1 + FILT_BANDS] = np.cos(ang)
    z[:, 1 + FILT_BANDS:1 + 2 * FILT_BANDS] = -np.sin(ang)
    return jnp.asarray(z, F32)


def _filters(L, w1p, b1, fr, w2, b2, w3):
    tl = min(L, 512)
    z = _filter_embedding(L)
    deltas = jnp.asarray(np.abs(np.linspace(MIN_DECAY, MAX_DECAY, HY_CH))[None, :], F32)
    return pl.pallas_call(
        _filter_kernel,
        out_shape=(jax.ShapeDtypeStruct((2, L, HY_CH), BF16), jax.ShapeDtypeStruct((1, HY_CH), F32)),
        grid=(L // tl,),
        in_specs=[pl.BlockSpec((tl, LANE), lambda i: (i, 0)),
                  _const_spec(w1p.shape), _const_spec(b1.shape), _const_spec(fr.shape),
                  _const_spec(w2.shape), _const_spec(b2.shape), _const_spec(w3.shape),
                  _const_spec(deltas.shape)],
        out_specs=(pl.BlockSpec((2, tl, HY_CH), lambda i: (0, i, 0)),
                   pl.BlockSpec((1, HY_CH), lambda i: (0, 0))),
        compiler_params=_cparams(("arbitrary",)),
        name="l0_hyena_filters",
    )(z, w1p, b1, fr, w2, b2, w3, deltas)


def _dft_tables(kind, L, ti):
    ni = L // ti
    i = np.arange(ti, dtype=np.int64)[:, None]
    big = (np.arange(ni, dtype=np.int64) * ti)[:, None]
    c = np.arange(L, dtype=np.int64)[None, :]
    if kind == "hy_fwd":
        period = 4 * L
        base_idx = (2 * i + 1) * c
        r_idx = 2 * big * c
        scale = 1.0
    elif kind == "hy_inv":
        period = 4 * L
        base_idx = (2 * c + 1) * i
        r_idx = (2 * c + 1) * big
        scale = 1.0 / L
    else:
        period = L
        base_idx = i * c
        r_idx = big * c
        scale = 1.0 / math.sqrt(L * FN_GROUP_CH)
    ab = 2.0 * np.pi * (base_idx % period) / period
    ar = 2.0 * np.pi * (r_idx % period) / period
    return (jnp.asarray(np.cos(ab), F32), jnp.asarray(np.sin(ab), F32),
            jnp.asarray(scale * np.cos(ar), F32).reshape(ni, 1, L),
            jnp.asarray(scale * np.sin(ar), F32).reshape(ni, 1, L))


def _dft_kernel(mode, nb, n_x, *refs):
    bc_ref, bs_ref, rc_ref, rs_ref = refs[:4]
    x_refs = refs[4:4 + n_x]
    rest = refs[4 + n_x:]
    p_ref, q_ref = rest[-2], rest[-1]
    j = pl.program_id(2)
    nj = pl.num_programs(2)
    tj = x_refs[0].shape[1]
    if bc_ref.shape[1] == tj:
        bc, bs, rc, rs = bc_ref[...], bs_ref[...], rc_ref[...], rs_ref[...]
    else:
        off = pl.multiple_of(j * tj, tj)
        bc, bs = bc_ref[:, pl.ds(off, tj)], bs_ref[:, pl.ds(off, tj)]
        rc, rs = rc_ref[:, pl.ds(off, tj)], rs_ref[:, pl.ds(off, tj)]
    tc = (bc * rc - bs * rs).astype(BF16)
    ts = (bs * rc + bc * rs).astype(BF16)
    x1_ref = x_refs[0]
    x2_ref = x_refs[-1]

    pq = [(jnp.dot(tc, x1_ref[b], preferred_element_type=F32),
           jnp.dot(ts, x2_ref[b], preferred_element_type=F32)) for b in range(nb)]

    @pl.when(j == 0)
    def _():
        for b in range(nb):
            p_ref[b] = pq[b][0]
            q_ref[b] = pq[b][1]

    @pl.when(j > 0)
    def _():
        for b in range(nb):
            p_ref[b] += pq[b][0]
            q_ref[b] += pq[b][1]

    @pl.when(j == nj - 1)
    def _():
        if mode == "filt":
            nrm = rest[0][...]
            kre_ref, kim_ref = rest[1], rest[2]
            kre_ref[...] = (p_ref[0] + p_ref[1]) / nrm
            kim_ref[...] = (q_ref[1] - q_ref[0]) / nrm
        elif mode == "fwdk":
            kre, kim = rest[0][...], rest[1][...]
            yre_ref, yim_ref = rest[2], rest[3]
            for b in range(nb):
                pp, qq = p_ref[b], q_ref[b]
                yre_ref[b] = (pp * kre + qq * kim).astype(BF16)
                yim_ref[b] = (pp * kim - qq * kre).astype(BF16)
        elif mode == "inv":
            e_ref, x0_ref, o_ref = rest[0], rest[1], rest[2]
            for b in range(nb):
                o_ref[b] = ((p_ref[b] - q_ref[b] + e_ref[b]) * x0_ref[b]).astype(BF16)
        else:
            o_ref = rest[0]
            for b in range(nb):
                o_ref[b] = (p_ref[b] - q_ref[b]).astype(BF16)


def _dft(mode, kind, xs, extras, nb):
    B, L, C = xs[0].shape
    ti = min(L, 256)
    tj = min(L, 512)
    bc, bs, rc, rs = _dft_tables(kind, L, ti)
    grid = (B // nb, L // ti, L // tj)
    x_spec = pl.BlockSpec((nb, tj, C), lambda g, i, j: (g, j, 0))
    row_spec = lambda c, dt=None: pl.BlockSpec((nb, ti, c), lambda g, i, j: (g, i, 0))
    in_specs = [pl.BlockSpec((ti, L), lambda g, i, j: (0, 0)),
                pl.BlockSpec((ti, L), lambda g, i, j: (0, 0)),
                pl.BlockSpec((None, 1, L), lambda g, i, j: (i, 0, 0)),
                pl.BlockSpec((None, 1, L), lambda g, i, j: (i, 0, 0))] + [x_spec] * len(xs)
    if mode == "filt":
        in_specs += [pl.BlockSpec((1, HY_CH), lambda g, i, j: (0, 0))]
        out_shape = (jax.ShapeDtypeStruct((L, HY_CH), F32),) * 2
        out_specs = (pl.BlockSpec((ti, HY_CH), lambda g, i, j: (i, 0)),) * 2
    elif mode == "fwdk":
        in_specs += [pl.BlockSpec((ti, HY_CH), lambda g, i, j: (i, 0))] * 2
        out_shape = (jax.ShapeDtypeStruct((B, L, C), BF16),) * 2
        out_specs = (row_spec(C),) * 2
    elif mode == "inv":
        in_specs += [row_spec(C)] * 2
        out_shape = jax.ShapeDtypeStruct((B, L, C), BF16)
        out_specs = row_spec(C)
    else:
        out_shape = jax.ShapeDtypeStruct((B, L, C), BF16)
        out_specs = row_spec(C)
    return pl.pallas_call(
        functools.partial(_dft_kernel, mode, nb, len(xs)),
        out_shape=out_shape,
        grid=grid,
        in_specs=in_specs,
        out_specs=out_specs,
        scratch_shapes=[pltpu.VMEM((nb, ti, C), F32), pltpu.VMEM((nb, ti, C), F32)],
        compiler_params=_cparams(("arbitrary", "arbitrary", "arbitrary")),
        name="dft_" + mode,
    )(bc, bs, rc, rs, *xs, *extras)


FFT_R = 64
FFT_KF = 8


def _colmm_kernel(n_extra, g_ref, x_ref, *rest):
    y = jnp.dot(g_ref[...], x_ref[...], preferred_element_type=F32)
    if n_extra:
        e_ref, x0_ref, o_ref = rest
        y = (y + e_ref[...]) * x0_ref[...]
    else:
        o_ref, = rest
    o_ref[...] = y.astype(o_ref.dtype)


def _colmm(g, x, extras, name):
    nbx, k, n = x.shape
    m = g.shape[0]
    tn = 4096
    col = lambda r: pl.BlockSpec((None, r, tn), lambda b, j: (b, 0, j))
    return pl.pallas_call(
        functools.partial(_colmm_kernel, len(extras)),
        out_shape=jax.ShapeDtypeStruct((nbx, m, n), BF16),
        grid=(nbx, n // tn),
        in_specs=[pl.BlockSpec((m, k), lambda b, j: (0, 0)), col(k)] + [col(m)] * len(extras),
        out_specs=col(m),
        compiler_params=_cparams(("arbitrary", "arbitrary")),
        name=name,
    )(g, x, *extras)


def _hy2_tables():
    L = FFT_R * FFT_R
    n2 = 2 * L
    f1 = np.arange(2 * FFT_R, dtype=np.int64)
    s1 = np.arange(FFT_R, dtype=np.int64)
    th = np.pi * (((2 * f1[:, None] + 1) * s1[None, :]) % (4 * FFT_R)) / (2 * FFT_R)
    ga = np.concatenate([np.cos(th), -np.sin(th)], axis=0)
    ma = np.concatenate([np.cos(th).T, -np.sin(th).T], axis=1) / L
    f2 = np.arange(FFT_R // 2, dtype=np.int64)
    s2 = np.arange(FFT_R, dtype=np.int64)
    idx = ((n2 // FFT_R) * 2 * f2[None, :, None] * s2[None, None, :]
           + (2 * f1[:, None, None] + 1) * s2[None, None, :]) % (2 * n2)
    al = np.pi * idx / n2
    c, s = np.cos(al), np.sin(al)
    nmat = np.concatenate([np.concatenate([c, s], axis=2), np.concatenate([-s, c], axis=2)], axis=1)
    ct, st = np.transpose(c, (0, 2, 1)), np.transpose(s, (0, 2, 1))
    mmat = np.concatenate([np.concatenate([ct, -st], axis=2), np.concatenate([st, ct], axis=2)], axis=1)
    bf = lambda a: jnp.asarray(a, F32).astype(BF16)
    return bf(ga), bf(nmat), bf(mmat), bf(ma)


def _hy_mid_kernel(a_ref, n_ref, m_ref, k_ref, e_ref):
    half = FFT_R // 2
    for j in range(FFT_KF):
        s = jnp.concatenate([a_ref[0, j], a_ref[1, j]], axis=0)
        t = jnp.dot(n_ref[j], s, preferred_element_type=F32)
        tr, ti = t[:half], t[half:]
        kr, ki = k_ref[j, 0], k_ref[j, 1]
        y = jnp.concatenate([tr * kr - ti * ki, tr * ki + ti * kr], axis=0).astype(BF16)
        d = jnp.dot(m_ref[j], y, preferred_element_type=F32)
        e_ref[0, j] = d[:FFT_R].astype(BF16)
        e_ref[1, j] = d[FFT_R:].astype(BF16)


def _hy_mid(a, nmat, mmat, khat):
    nb = a.shape[0]
    c = a.shape[-1]
    nf1 = 2 * FFT_R
    blk = pl.BlockSpec((None, 2, FFT_KF, FFT_R, c), lambda i, b: (b, 0, i, 0, 0))
    return pl.pallas_call(
        _hy_mid_kernel,
        out_shape=jax.ShapeDtypeStruct(a.shape, BF16),
        grid=(nf1 // FFT_KF, nb),
        in_specs=[blk,
                  pl.BlockSpec((FFT_KF, FFT_R, 2 * FFT_R), lambda i, b: (i, 0, 0)),
                  pl.BlockSpec((FFT_KF, 2 * FFT_R, FFT_R), lambda i, b: (i, 0, 0)),
                  pl.BlockSpec((FFT_KF, 2, FFT_R // 2, c), lambda i, b: (i, 0, 0, 0))],
        out_specs=blk,
        compiler_params=_cparams(("arbitrary", "arbitrary")),
        name="l0_hyena_mid",
    )(a, nmat, mmat, khat)


def _hy_kfilt_kernel(a_ref, n_ref, nrm_ref, k_ref):
    half = FFT_R // 2
    nrm = nrm_ref[...]
    for j in range(FFT_KF):
        tf = jnp.dot(n_ref[j], jnp.concatenate([a_ref[0, 0, j], a_ref[0, 1, j]], axis=0),
                     preferred_element_type=F32)
        tb = jnp.dot(n_ref[j], jnp.concatenate([a_ref[1, 0, j], a_ref[1, 1, j]], axis=0),
                     preferred_element_type=F32)
        k_ref[j, 0] = (tf[:half] + tb[:half]) / nrm
        k_ref[j, 1] = (tf[half:] - tb[half:]) / nrm


def _hy_kfilt(a, nmat, nrm):
    c = a.shape[-1]
    nf1 = 2 * FFT_R
    return pl.pallas_call(
        _hy_kfilt_kernel,
        out_shape=jax.ShapeDtypeStruct((nf1, 2, FFT_R // 2, c), F32),
        grid=(nf1 // FFT_KF,),
        in_specs=[pl.BlockSpec((2, 2, FFT_KF, FFT_R, c), lambda i: (0, 0, i, 0, 0)),
                  pl.BlockSpec((FFT_KF, FFT_R, 2 * FFT_R), lambda i: (i, 0, 0)),
                  _const_spec(nrm.shape)],
        out_specs=pl.BlockSpec((FFT_KF, 2, FFT_R // 2, c), lambda i: (i, 0, 0, 0)),
        compiler_params=_cparams(("arbitrary",)),
        name="l0_hyena_kfilt",
    )(a, nmat, nrm)


def _hyena_long(u, e, x0, hfilt, nrm, nb):
    L = FFT_R * FFT_R
    c = u.shape[-1]
    ga, nmat, mmat, ma = _hy2_tables()
    ah = _colmm(ga, hfilt.reshape(2, FFT_R, FFT_R * c), [], "l0_hyena_fwd_a")
    khat = _hy_kfilt(ah.reshape(2, 2, 2 * FFT_R, FFT_R, c), nmat, nrm)
    a = _colmm(ga, u.reshape(nb, FFT_R, FFT_R * c), [], "l0_hyena_fwd_a")
    ee = _hy_mid(a.reshape(nb, 2, 2 * FFT_R, FFT_R, c), nmat, mmat, khat)
    y = _colmm(ma, ee.reshape(nb, 4 * FFT_R, FFT_R * c),
               [e.reshape(nb, FFT_R, FFT_R * c), x0.reshape(nb, FFT_R, FFT_R * c)], "l0_hyena_inv_a")
    return y.reshape(nb * L, c)


def _fn2_tables():
    L = FFT_R * FFT_R
    r = np.arange(FFT_R, dtype=np.int64)
    idx = (FFT_R * r[None, :, None] * r[None, None, :] + r[None, :, None] * r[:, None, None]) % L
    gm = 2.0 * np.pi * idx / L
    c, s = np.cos(gm), np.sin(gm)
    g1 = np.concatenate([np.concatenate([c, -s], axis=2), np.concatenate([-s, -c], axis=2)], axis=1)
    dl = 2.0 * np.pi * ((r[:, None] * r[None, :]) % FFT_R) / FFT_R
    g2 = np.concatenate([np.cos(dl), np.sin(dl)], axis=1) / math.sqrt(L * FN_GROUP_CH)
    bf = lambda a: jnp.asarray(a, F32).astype(BF16)
    return bf(g1), bf(g2)


def _fnet_s1_kernel(x_ref, m_ref, cs_ref, g1_ref, o_ref, zc_ref, zs_ref):
    xs = jnp.concatenate([x_ref[:, j, :] for j in range(FFT_KF)], axis=0)
    m = m_ref[...]
    h = (_ln_plain(xs) * (1.0 + m[:, D_MODEL:2 * D_MODEL]) + m[:, 0:D_MODEL]).astype(BF16)
    cs = cs_ref[...]
    for g in range(D_MODEL // FN_GROUP_CH):
        a = g * FN_GROUP_CH
        z = jnp.dot(h[:, a:a + FN_GROUP_CH], cs, preferred_element_type=F32)
        zc_ref[:, a:a + FN_GROUP_CH] = z[:, :FN_GROUP_CH].astype(BF16)
        zs_ref[:, a:a + FN_GROUP_CH] = z[:, FN_GROUP_CH:].astype(BF16)
    for j in range(FFT_KF):
        r0 = j * FFT_R
        s = jnp.concatenate([zc_ref[r0:r0 + FFT_R, :], zs_ref[r0:r0 + FFT_R, :]], axis=0)
        b = jnp.dot(g1_ref[j], s, preferred_element_type=F32)
        o_ref[0, j] = b[:FFT_R].astype(BF16)
        o_ref[1, j] = b[FFT_R:].astype(BF16)


def _fnet_long(x, mods, mod_base, nb):
    L = FFT_R * FFT_R
    d = D_MODEL
    g1, g2 = _fn2_tables()
    cs = _group_dft_table()
    bsh = (nb, 2, FFT_R, FFT_R, d)
    bb = pl.pallas_call(
        _fnet_s1_kernel,
        out_shape=jax.ShapeDtypeStruct(bsh, BF16),
        grid=(nb, FFT_R // FFT_KF),
        in_specs=[pl.BlockSpec((None, FFT_R, FFT_KF, d), lambda b, k: (b, 0, k, 0)),
                  pl.BlockSpec((None, 1, 6 * d), lambda b, k: (mod_base + b, 0, 0)),
                  pl.BlockSpec(cs.shape, lambda b, k: (0, 0)),
                  pl.BlockSpec((FFT_KF, 2 * FFT_R, 2 * FFT_R), lambda b, k: (k, 0, 0))],
        out_specs=pl.BlockSpec((None, 2, FFT_KF, FFT_R, d), lambda b, k: (b, 0, k, 0, 0)),
        scratch_shapes=[pltpu.VMEM((FFT_KF * FFT_R, d), BF16), pltpu.VMEM((FFT_KF * FFT_R, d), BF16)],
        compiler_params=_cparams(("arbitrary", "arbitrary")),
        name="l1_fnet_stage1",
    )(x.reshape(nb, FFT_R, FFT_R, d), mods, cs, g1)
    y = _colmm(g2, bb.reshape(nb, 2 * FFT_R, FFT_R * d), [], "l1_fnet_stage2")
    return y.reshape(nb * L, d)


def _group_dft_table():
    g = FN_GROUP_CH
    jk = (np.arange(g, dtype=np.int64)[:, None] * np.arange(g, dtype=np.int64)[None, :]) % g
    ang = 2.0 * np.pi * jk / g
    return jnp.asarray(np.concatenate([np.cos(ang), np.sin(ang)], axis=1), F32).astype(BF16)


def _fnet_front_kernel(x_ref, m_ref, cs_ref, zc_ref, zs_ref):
    m = m_ref[...]
    h = (_ln_plain(x_ref[...]) * (1.0 + m[:, D_MODEL:2 * D_MODEL]) + m[:, 0:D_MODEL]).astype(BF16)
    cs = cs_ref[...]
    for g in range(D_MODEL // FN_GROUP_CH):
        a = g * FN_GROUP_CH
        z = jnp.dot(h[:, a:a + FN_GROUP_CH], cs, preferred_element_type=F32)
        zc_ref[:, a:a + FN_GROUP_CH] = z[:, :FN_GROUP_CH].astype(BF16)
        zs_ref[:, a:a + FN_GROUP_CH] = z[:, FN_GROUP_CH:].astype(BF16)


def _fnet_front(x, mods, mod_base, tiles_per_mod):
    t = x.shape[0]
    tm = ROW_TILE
    cs = _group_dft_table()
    return pl.pallas_call(
        _fnet_front_kernel,
        out_shape=(jax.ShapeDtypeStruct((t, D_MODEL), BF16),) * 2,
        grid=(t // tm,),
        in_specs=[pl.BlockSpec((tm, D_MODEL), lambda i: (i, 0)),
                  _mod_spec(mod_base, tiles_per_mod),
                  _const_spec(cs.shape)],
        out_specs=(pl.BlockSpec((tm, D_MODEL), lambda i: (i, 0)),) * 2,
        compiler_params=_cparams(("arbitrary",)),
        name="l1_fnet_front",
    )(x, mods, cs)


def _post_kernel(n_a, *refs):
    x_ref, m_ref = refs[0], refs[1]
    a_refs = refs[2:2 + n_a]
    wo_refs = refs[2 + n_a:2 + 2 * n_a]
    g1_ref, b1_ref, w1_ref, w2_ref, g2_ref, b2_ref, o_ref = refs[2 + 2 * n_a:]
    m = m_ref[...]
    d = D_MODEL
    out = jnp.dot(a_refs[0][...], wo_refs[0][...], preferred_element_type=F32)
    for a_ref, wo_ref in zip(a_refs[1:], wo_refs[1:]):
        out += jnp.dot(a_ref[...], wo_ref[...], preferred_element_type=F32)
    x1 = _ln_plain(ALPHA * x_ref[...] + m[:, 2 * d:3 * d] * out) * g1_ref[...] + b1_ref[...]
    h = (_ln_plain(x1) * (1.0 + m[:, 4 * d:5 * d]) + m[:, 3 * d:4 * d]).astype(BF16)
    acc = jnp.zeros_like(x1)
    for c in range(D_FF // d):
        hc = jnp.maximum(jnp.dot(h, w1_ref[:, c * d:(c + 1) * d], preferred_element_type=F32), 0.0)
        acc += jnp.dot((hc * hc).astype(BF16), w2_ref[c * d:(c + 1) * d, :], preferred_element_type=F32)
    o_ref[...] = _ln_plain(ALPHA * x1 + m[:, 5 * d:6 * d] * acc) * g2_ref[...] + b2_ref[...]


def _post(x, mods, mod_base, tiles_per_mod, a_list, wo_list, g1, b1, w1, w2, g2, b2):
    t = x.shape[0]
    tm = ROW_TILE
    row = lambda c: pl.BlockSpec((tm, c), lambda i: (i, 0))
    in_specs = ([row(D_MODEL), _mod_spec(mod_base, tiles_per_mod)]
                + [row(a.shape[1]) for a in a_list]
                + [_const_spec(w.shape) for w in wo_list]
                + [_const_spec(v.shape) for v in (g1, b1, w1, w2, g2, b2)])
    return pl.pallas_call(
        functools.partial(_post_kernel, len(a_list)),
        out_shape=jax.ShapeDtypeStruct((t, D_MODEL), F32),
        grid=(t // tm,),
        in_specs=in_specs,
        out_specs=row(D_MODEL),
        compiler_params=_cparams(("arbitrary",)),
        name="post_mlp",
    )(x, mods, *a_list, *wo_list, g1, b1, w1, w2, g2, b2)


def _rot_cols(w):
    parts = []
    for seg in range(2):
        o = seg * 32
        parts += [-w[:, o + 16:o + 32], w[:, o:o + 16]]
    return jnp.concatenate(parts, axis=1)


def _pad_cols(w, n):
    return jnp.pad(w, ((0, 0), (0, n - w.shape[1])))


def _rope_tables(L):
    rows = L // GRID_W
    row = np.repeat(np.arange(rows, dtype=np.float64), GRID_W)
    col = np.tile(np.arange(GRID_W, dtype=np.float64), rows)
    half = QK_ROPE // 2
    inv = 1.0 / (ROPE_THETA ** (np.arange(0, half, 2, dtype=np.float64) / half))
    ar = row[:, None] * inv[None, :]
    ac = col[:, None] * inv[None, :]
    ang = np.concatenate([ar, ar, ac, ac], axis=1)
    cos = np.concatenate([np.cos(ang), np.ones_like(ang)], axis=1)
    sin = np.concatenate([np.sin(ang), np.zeros_like(ang)], axis=1)
    return jnp.asarray(cos, F32), jnp.asarray(sin, F32)


def kernel(x_prompt, x_sample, cache_l0_ckv, cache_l0_krope, c, c_ctx, l0_ada_w, l0_ada_b, l0_w_in, l0_conv_w, l0_conv_b, l0_hf_w1, l0_hf_b1, l0_hf_freq, l0_hf_w2, l0_hf_b2, l0_hf_w3, l0_hf_skip, l0_q_norm, l0_q_up, l0_kv_norm, l0_kv_up, l0_w_out, l0_ln1_g, l0_ln1_b, l0_mlp_w1, l0_mlp_w2, l0_ln2_g, l0_ln2_b, l1_ada_w, l1_ada_b, l1_w_out, l1_ln1_g, l1_ln1_b, l1_mlp_w1, l1_mlp_w2, l1_ln2_g, l1_ln2_b):
    nbc, lc, d = x_prompt.shape
    nbs, ls, _ = x_sample.shape
    past = cache_l0_ckv.shape[1]
    tm = ROW_TILE
    row1 = lambda v: v.reshape(1, -1)

    cond8 = jnp.concatenate([c_ctx[None, :], c, jnp.zeros((8 - 1 - nbs, d), F32)], axis=0)
    mods0 = _modulation(cond8, l0_ada_w, l0_ada_b)
    mods1 = _modulation(cond8, l1_ada_w, l1_ada_b)

    kpe_w = l0_w_in[:, 1920:1984]
    win = jnp.concatenate([l0_w_in[:, :1920], _pad_cols(kpe_w, LANE), _pad_cols(_rot_cols(kpe_w), LANE)],
                          axis=1).astype(BF16)
    dh = QK_NOPE + QK_ROPE
    q_nope = [l0_q_up[:, h * dh:h * dh + QK_NOPE] for h in range(MLA_HEADS)]
    q_pe = [l0_q_up[:, h * dh + QK_NOPE:(h + 1) * dh] for h in range(MLA_HEADS)]
    qup = jnp.concatenate(q_nope + [_pad_cols(w, LANE) for w in q_pe]
                          + [_pad_cols(_rot_cols(w), LANE) for w in q_pe], axis=1).astype(BF16)
    kvup = l0_kv_up.astype(BF16)
    front_w = (win, row1(l0_q_norm), qup, row1(l0_kv_norm), kvup)
    w1p = jnp.pad(l0_hf_w1, ((0, LANE - l0_hf_w1.shape[0]), (0, 0)))
    filt_w = (w1p, row1(l0_hf_b1), row1(l0_hf_freq), l0_hf_w2, row1(l0_hf_b2), l0_hf_w3.astype(BF16))
    wo0 = l0_w_out.astype(BF16)
    conv_b = row1(l0_conv_b)
    skip = row1(l0_hf_skip)

    xc = x_prompt.reshape(nbc * lc, d)
    xs = x_sample.reshape(nbs * ls, d)
    groups = (
        dict(x=xc, nb=nbc, L=lc, mod_base=0, tiles_per_mod=nbc * lc // tm, dft_nb=4, tq=lc),
        dict(x=xs, nb=nbs, L=ls, mod_base=1, tiles_per_mod=ls // tm, dft_nb=nbs, tq=256),
    )
    ones_tab = (jnp.concatenate([jnp.ones((tm, LANE), F32)], axis=0), jnp.zeros((tm, LANE), F32))

    outs = []
    ctx_ckv = ctx_krope = None
    for gi, g in enumerate(groups):
        nb, L = g["nb"], g["L"]
        tiles_per_seq = L // tm
        latent = gi == 1
        cos, sin = _rope_tables(L) if latent else ones_tab
        hy, q, k, v, kvn, kpe = _front(g["x"], mods0, g["mod_base"], g["tiles_per_mod"], front_w,
                                       cos, sin, tiles_per_seq if latent else 1)
        k = k.reshape(MLA_HEADS, nb, L, QK_PAD)
        v = v.reshape(MLA_HEADS, nb, L, V_HEAD)
        if latent:
            kc, vc = _cache_kv(cache_l0_ckv.reshape(nbs * past, KV_LORA),
                               _pad_cols(cache_l0_krope.reshape(nbs * past, QK_ROPE), LANE), kvup)
            k = jnp.concatenate([k, kc.reshape(MLA_HEADS, nb, past, QK_PAD)], axis=2)
            v = jnp.concatenate([v, vc.reshape(MLA_HEADS, nb, past, V_HEAD)], axis=2)
        else:
            ctx_ckv = kvn.reshape(nb, L, KV_LORA)
            ctx_krope = kpe.reshape(nb, L, QK_ROPE)
        y_mla = _attention(q, k, v, nb, L, g["tq"])

        u, e, x0 = _conv_gate(hy, l0_conv_w, conv_b, skip, tiles_per_seq)
        hfilt, hnorm = _filters(L, *filt_w)
        if L == FFT_R * FFT_R:
            y_hy = _hyena_long(u, e, x0, hfilt, hnorm, nb)
        else:
            kre, kim = _dft("filt", "hy_fwd", [hfilt], [hnorm], 2)
            sh = (nb, L, HY_CH)
            yre, yim = _dft("fwdk", "hy_fwd", [u.reshape(sh)], [kre, kim], g["dft_nb"])
            y_hy = _dft("inv", "hy_inv", [yre, yim], [e.reshape(sh), x0.reshape(sh)], g["dft_nb"])
            y_hy = y_hy.reshape(nb * L, HY_CH)

        x1 = _post(g["x"], mods0, g["mod_base"], g["tiles_per_mod"], [y_hy, y_mla],
                   [wo0[:HY_CH], wo0[HY_CH:]], row1(l0_ln1_g), row1(l0_ln1_b),
                   l0_mlp_w1.astype(BF16), l0_mlp_w2.astype(BF16), row1(l0_ln2_g), row1(l0_ln2_b))

        if L == FFT_R * FFT_R:
            yf = _fnet_long(x1, mods1, g["mod_base"], nb)
        else:
            zc, zs = _fnet_front(x1, mods1, g["mod_base"], g["tiles_per_mod"])
            sh = (nb, L, d)
            yf = _dft("fnet", "fnet", [zc.reshape(sh), zs.reshape(sh)], [], min(g["dft_nb"], 2))
            yf = yf.reshape(nb * L, d)
        x2 = _post(x1, mods1, g["mod_base"], g["tiles_per_mod"], [yf],
                   [l1_w_out.astype(BF16)], row1(l1_ln1_g), row1(l1_ln1_b),
                   l1_mlp_w1.astype(BF16), l1_mlp_w2.astype(BF16), row1(l1_ln2_g), row1(l1_ln2_b))
        outs.append(x2.reshape(nb, L, d))

    return (outs[0], outs[1], ctx_ckv, ctx_krope)
```

```python
import functools
import math

import numpy as np
import jax
import jax.numpy as jnp
from jax import lax
from jax.experimental import pallas as pl
from jax.experimental.pallas import tpu as pltpu

F32 = jnp.float32
BF16 = jnp.bfloat16
HI = lax.Precision.HIGHEST

D_MODEL = 1024
DEPTH = 2
GRID_W = 64
HY_CH = 512
FILT_BANDS = 16
FILT_ORDER = 64
FAST_DECAY_PCT = 0.3
SLOW_DECAY_PCT = 1.5
DECAY_TARGET = 1e-2
MAX_DECAY = math.log(DECAY_TARGET) / FAST_DECAY_PCT
MIN_DECAY = math.log(DECAY_TARGET) / SLOW_DECAY_PCT
MLA_HEADS = 4
QK_NOPE = 128
QK_ROPE = 64
V_HEAD = 128
Q_LORA = 256
KV_LORA = 128
ROPE_THETA = 10000.0
FN_GROUP_CH = 128
D_FF = 4096
ALPHA = (2 * DEPTH) ** 0.25
LN_EPS = 1e-5
RMS_EPS = 1e-6

LANE = 128
ROW_TILE = 256
QK_PAD = 256
VMEM_LIMIT = 56 * 1024 * 1024


def _cparams(sem):
    return pltpu.CompilerParams(dimension_semantics=sem, vmem_limit_bytes=VMEM_LIMIT)


def _ln_plain(x):
    mu = jnp.mean(x, axis=-1, keepdims=True)
    xc = x - mu
    var = jnp.mean(xc * xc, axis=-1, keepdims=True)
    return xc * lax.rsqrt(var + LN_EPS)


def _rms(x, g):
    return x * lax.rsqrt(jnp.mean(x * x, axis=-1, keepdims=True) + RMS_EPS) * g


def _bdot(a, b):
    return jnp.dot(a.astype(BF16), b, preferred_element_type=F32)


def _mod_kernel(c_ref, w_ref, b_ref, o_ref):
    c = c_ref[...]
    s = c / (1.0 + jnp.exp(-c))
    o_ref[...] = jnp.dot(s, w_ref[...], precision=HI, preferred_element_type=F32) + b_ref[...]


def _modulation(cond8, w, b):
    n = w.shape[1]
    tn = 1536
    out = pl.pallas_call(
        _mod_kernel,
        out_shape=jax.ShapeDtypeStruct((8, n), F32),
        grid=(n // tn,),
        in_specs=[pl.BlockSpec((8, D_MODEL), lambda j: (0, 0)),
                  pl.BlockSpec((D_MODEL, tn), lambda j: (0, j)),
                  pl.BlockSpec((1, tn), lambda j: (0, j))],
        out_specs=pl.BlockSpec((8, tn), lambda j: (0, j)),
        compiler_params=_cparams(("arbitrary",)),
        name="modulation",
    )(cond8, w, b.reshape(1, n))
    return out.reshape(8, 1, n)


def _mod_spec(mod_base, tiles_per_mod):
    return pl.BlockSpec((None, 1, 6 * D_MODEL), lambda i: (mod_base + i // tiles_per_mod, 0, 0))


def _const_spec(shape):
    nd = len(shape)
    return pl.BlockSpec(shape, lambda i: (0,) * nd)


def _front_kernel(x_ref, m_ref, win_ref, qn_ref, qup_ref, kvn_ref, kvup_ref, cos_ref, sin_ref,
                  hy_ref, q_ref, k_ref, v_ref, kvn_out_ref, kpe_ref):
    m = m_ref[...]
    h = _ln_plain(x_ref[...]) * (1.0 + m[:, D_MODEL:2 * D_MODEL]) + m[:, 0:D_MODEL]
    z = _bdot(h, win_ref[...])
    hy_ref[...] = z[:, :3 * HY_CH]
    q_c = z[:, 1536:1792]
    kv_c = z[:, 1792:1920]
    cos = cos_ref[...]
    sin = sin_ref[...]
    kpe = z[:, 1920:2048] * cos + z[:, 2048:2176] * sin
    kpe_ref[...] = kpe[:, :QK_ROPE]
    kpe_b = kpe.astype(BF16)
    q = _bdot(_rms(q_c, qn_ref[...]), qup_ref[...]) * (1.0 / math.sqrt(QK_NOPE + QK_ROPE))
    kvn = _rms(kv_c, kvn_ref[...])
    kvn_out_ref[...] = kvn
    kv = _bdot(kvn, kvup_ref[...])
    for hd in range(MLA_HEADS):
        a = hd * LANE
        q_pe = (q[:, 512 + a:512 + a + LANE] * cos + q[:, 1024 + a:1024 + a + LANE] * sin).astype(BF16)
        q_ref[hd] = jnp.concatenate([q[:, a:a + LANE].astype(BF16), q_pe], axis=-1)
        k_ref[hd] = jnp.concatenate([kv[:, 2 * a:2 * a + LANE].astype(BF16), kpe_b], axis=-1)
        v_ref[hd] = kv[:, 2 * a + LANE:2 * a + 2 * LANE].astype(BF16)


def _front(x, mods, mod_base, tiles_per_mod, w, cos, sin, tiles_per_seq):
    t = x.shape[0]
    tm = ROW_TILE
    win, qn, qup, kvn, kvup = w
    if tiles_per_seq == 1:
        tab_spec = pl.BlockSpec((tm, LANE), lambda i: (0, 0))
    else:
        tab_spec = pl.BlockSpec((tm, LANE), lambda i: (i % tiles_per_seq, 0))
    return pl.pallas_call(
        _front_kernel,
        out_shape=(jax.ShapeDtypeStruct((t, 3 * HY_CH), F32),
                   jax.ShapeDtypeStruct((MLA_HEADS, t, QK_PAD), BF16),
                   jax.ShapeDtypeStruct((MLA_HEADS, t, QK_PAD), BF16),
                   jax.ShapeDtypeStruct((MLA_HEADS, t, V_HEAD), BF16),
                   jax.ShapeDtypeStruct((t, KV_LORA), F32),
                   jax.ShapeDtypeStruct((t, QK_ROPE), F32)),
        grid=(t // tm,),
        in_specs=[pl.BlockSpec((tm, D_MODEL), lambda i: (i, 0)),
                  _mod_spec(mod_base, tiles_per_mod),
                  _const_spec(win.shape), _const_spec(qn.shape), _const_spec(qup.shape),
                  _const_spec(kvn.shape), _const_spec(kvup.shape),
                  tab_spec, tab_spec],
        out_specs=(pl.BlockSpec((tm, 3 * HY_CH), lambda i: (i, 0)),
                   pl.BlockSpec((MLA_HEADS, tm, QK_PAD), lambda i: (0, i, 0)),
                   pl.BlockSpec((MLA_HEADS, tm, QK_PAD), lambda i: (0, i, 0)),
                   pl.BlockSpec((MLA_HEADS, tm, V_HEAD), lambda i: (0, i, 0)),
                   pl.BlockSpec((tm, KV_LORA), lambda i: (i, 0)),
                   pl.BlockSpec((tm, QK_ROPE), lambda i: (i, 0))),
        compiler_params=_cparams(("arbitrary",)),
        name="l0_front",
    )(x, mods, win, qn, qup, kvn, kvup, cos, sin)


def _cache_kv_kernel(ckv_ref, kr_ref, kvup_ref, k_ref, v_ref):
    kv = _bdot(ckv_ref[...], kvup_ref[...])
    kr = kr_ref[...].astype(BF16)
    for hd in range(MLA_HEADS):
        a = 2 * hd * LANE
        k_ref[hd] = jnp.concatenate([kv[:, a:a + LANE].astype(BF16), kr], axis=-1)
        v_ref[hd] = kv[:, a + LANE:a + 2 * LANE].astype(BF16)


def _cache_kv(ckv, krope_pad, kvup):
    t = ckv.shape[0]
    return pl.pallas_call(
        _cache_kv_kernel,
        out_shape=(jax.ShapeDtypeStruct((MLA_HEADS, t, QK_PAD), BF16),
                   jax.ShapeDtypeStruct((MLA_HEADS, t, V_HEAD), BF16)),
        name="l0_cache_kv",
    )(ckv, krope_pad, kvup)


def _attn_kernel(q_ref, k_ref, v_ref, o_ref):
    s = lax.dot_general(q_ref[...], k_ref[...], (((1,), (1,)), ((), ())),
                        preferred_element_type=F32)
    m = jnp.max(s, axis=-1, keepdims=True)
    p = jnp.exp(s - m)
    l = jnp.sum(p, axis=-1, keepdims=True)
    o = jnp.dot(p.astype(BF16), v_ref[...], preferred_element_type=F32)
    o_ref[...] = (o / l).astype(o_ref.dtype)


def _attention(q, k, v, nb, lq, tq):
    lk = k.shape[2]
    nq = lq // tq
    return pl.pallas_call(
        _attn_kernel,
        out_shape=jax.ShapeDtypeStruct((nb * lq, MLA_HEADS * V_HEAD), BF16),
        grid=(nb, MLA_HEADS, nq),
        in_specs=[pl.BlockSpec((None, tq, QK_PAD), lambda b, h, i: (h, b * nq + i, 0)),
                  pl.BlockSpec((None, None, lk, QK_PAD), lambda b, h, i: (h, b, 0, 0)),
                  pl.BlockSpec((None, None, lk, V_HEAD), lambda b, h, i: (h, b, 0, 0))],
        out_specs=pl.BlockSpec((tq, V_HEAD), lambda b, h, i: (b * nq + i, h)),
        compiler_params=_cparams(("arbitrary", "arbitrary", "arbitrary")),
        name="l0_attention",
    )(q, k, v)


def _conv_gate_kernel(tiles_per_seq, hy_ref, prev_ref, next_ref, w_ref, b_ref, skip_ref,
                      u_ref, e_ref, x0_ref):
    i = pl.program_id(0)
    x = hy_ref[...]
    tm = x.shape[0]
    pos = i % tiles_per_seq
    prev_row = jnp.where(pos == 0, 0.0, prev_ref[7:8, :])
    next_row = jnp.where(pos == tiles_per_seq - 1, 0.0, next_ref[0:1, :])
    rows = lax.broadcasted_iota(jnp.int32, x.shape, 0)
    xm1 = jnp.where(rows == 0, prev_row, pltpu.roll(x, 1, 0))
    xp1 = jnp.where(rows == tm - 1, next_row, pltpu.roll(x, tm - 1, 0))
    w = w_ref[...]
    p = xm1 * w[0:1, :] + x * w[1:2, :] + xp1 * w[2:3, :] + b_ref[...]
    u = p[:, 2 * HY_CH:] * p[:, HY_CH:2 * HY_CH]
    u_ref[...] = u.astype(u_ref.dtype)
    e_ref[...] = u * skip_ref[...]
    x0_ref[...] = p[:, :HY_CH]


def _conv_gate(hy, conv_w, conv_b, skip, tiles_per_seq, u_dtype):
    t = hy.shape[0]
    tm = ROW_TILE
    r8 = tm // 8
    n8 = t // 8
    return pl.pallas_call(
        functools.partial(_conv_gate_kernel, tiles_per_seq),
        out_shape=(jax.ShapeDtypeStruct((t, HY_CH), u_dtype),
                   jax.ShapeDtypeStruct((t, HY_CH), F32),
                   jax.ShapeDtypeStruct((t, HY_CH), F32)),
        grid=(t // tm,),
        in_specs=[pl.BlockSpec((tm, 3 * HY_CH), lambda i: (i, 0)),
                  pl.BlockSpec((8, 3 * HY_CH), lambda i: (jnp.maximum(i * r8 - 1, 0), 0)),
                  pl.BlockSpec((8, 3 * HY_CH), lambda i: (jnp.minimum((i + 1) * r8, n8 - 1), 0)),
                  _const_spec(conv_w.shape), _const_spec(conv_b.shape), _const_spec(skip.shape)],
        out_specs=(pl.BlockSpec((tm, HY_CH), lambda i: (i, 0)),
                   pl.BlockSpec((tm, HY_CH), lambda i: (i, 0)),
                   pl.BlockSpec((tm, HY_CH), lambda i: (i, 0))),
        compiler_params=_cparams(("arbitrary",)),
        name="l0_conv_gate",
    )(hy, hy, hy, conv_w, conv_b, skip)


def _filter_kernel(z_ref, w1_ref, b1_ref, fr_ref, w2_ref, b2_ref, w3_ref, dl_ref, h_ref, norm_ref):
    i = pl.program_id(0)
    z = z_ref[...]
    tl = z.shape[0]
    fr = fr_ref[...]
    h = jnp.sin(fr * (jnp.dot(z, w1_ref[...], precision=HI, preferred_element_type=F32) + b1_ref[...]))
    h = jnp.sin(fr * (jnp.dot(h, w2_ref[...], precision=HI, preferred_element_type=F32) + b2_ref[...]))
    h = _bdot(h, w3_ref[...])
    decay = jnp.exp(-(z[:, 0:1] * dl_ref[...]))
    hf = h[:, :HY_CH] * decay
    hb = h[:, HY_CH:] * decay
    part = jnp.sum(jnp.abs(hf) + jnp.abs(hb), axis=0, keepdims=True)

    @pl.when(i == 0)
    def _():
        norm_ref[...] = part

    @pl.when(i > 0)
    def _():
        norm_ref[...] += part

    rows = lax.broadcasted_iota(jnp.int32, hb.shape, 0) + i * tl
    h_ref[0] = hf.astype(h_ref.dtype)
    h_ref[1] = jnp.where(rows == 0, 0.0, hb).astype(h_ref.dtype)


def _filter_embedding(L):
    t = np.linspace(0.0, 1.0, L)[:, None]
    w_ang = 2.0 * np.pi * np.arange(L) / L
    bands = np.linspace(1e-4, FILT_BANDS - 1, FILT_BANDS)
    ang = w_ang[:, None] * bands[None, :]
    z = np.zeros((L, LANE), np.float64)
    z[:, 0:1] = t
    z[:, 1:1 + FILT_BANDS] = np.cos(ang)
    z[:, 1 + FILT_BANDS:1 + 2 * FILT_BANDS] = -np.sin(ang)
    return jnp.asarray(z, F32)


def _filters(L, h_dtype, w1p, b1, fr, w2, b2, w3):
    tl = min(L, 512)
    z = _filter_embedding(L)
    deltas = jnp.asarray(np.abs(np.linspace(MIN_DECAY, MAX_DECAY, HY_CH))[None, :], F32)
    return pl.pallas_call(
        _filter_kernel,
        out_shape=(jax.ShapeDtypeStruct((2, L, HY_CH), h_dtype), jax.ShapeDtypeStruct((1, HY_CH), F32)),
        grid=(L // tl,),
        in_specs=[pl.BlockSpec((tl, LANE), lambda i: (i, 0)),
                  _const_spec(w1p.shape), _const_spec(b1.shape), _const_spec(fr.shape),
                  _const_spec(w2.shape), _const_spec(b2.shape), _const_spec(w3.shape),
                  _const_spec(deltas.shape)],
        out_specs=(pl.BlockSpec((2, tl, HY_CH), lambda i: (0, i, 0)),
                   pl.BlockSpec((1, HY_CH), lambda i: (0, 0))),
        compiler_params=_cparams(("arbitrary",)),
        name="l0_hyena_filters",
    )(z, w1p, b1, fr, w2, b2, w3, deltas)


def _dft_tables(kind, L, ti):
    ni = L // ti
    i = np.arange(ti, dtype=np.int64)[:, None]
    big = (np.arange(ni, dtype=np.int64) * ti)[:, None]
    c = np.arange(L, dtype=np.int64)[None, :]
    if kind == "hy_fwd":
        period = 4 * L
        base_idx = (2 * i + 1) * c
        r_idx = 2 * big * c
        scale = 1.0
    elif kind == "hy_inv":
        period = 4 * L
        base_idx = (2 * c + 1) * i
        r_idx = (2 * c + 1) * big
        scale = 1.0 / L
    else:
        period = L
        base_idx = i * c
        r_idx = big * c
        scale = 1.0 / math.sqrt(L * FN_GROUP_CH)
    ab = 2.0 * np.pi * (base_idx % period) / period
    ar = 2.0 * np.pi * (r_idx % period) / period
    return (jnp.asarray(np.cos(ab), F32), jnp.asarray(np.sin(ab), F32),
            jnp.asarray(scale * np.cos(ar), F32).reshape(ni, 1, L),
            jnp.asarray(scale * np.sin(ar), F32).reshape(ni, 1, L))


def _dft_kernel(mode, nb, n_x, *refs):
    bc_ref, bs_ref, rc_ref, rs_ref = refs[:4]
    x_refs = refs[4:4 + n_x]
    rest = refs[4 + n_x:]
    p_ref, q_ref = rest[-2], rest[-1]
    j = pl.program_id(2)
    nj = pl.num_programs(2)
    tj = x_refs[0].shape[1]
    if bc_ref.shape[1] == tj:
        bc, bs, rc, rs = bc_ref[...], bs_ref[...], rc_ref[...], rs_ref[...]
    else:
        off = pl.multiple_of(j * tj, tj)
        bc, bs = bc_ref[:, pl.ds(off, tj)], bs_ref[:, pl.ds(off, tj)]
        rc, rs = rc_ref[:, pl.ds(off, tj)], rs_ref[:, pl.ds(off, tj)]
    tc = (bc * rc - bs * rs).astype(BF16)
    ts = (bs * rc + bc * rs).astype(BF16)
    x1_ref = x_refs[0]
    x2_ref = x_refs[-1]

    pq = [(jnp.dot(tc, x1_ref[b], preferred_element_type=F32),
           jnp.dot(ts, x2_ref[b], preferred_element_type=F32)) for b in range(nb)]

    @pl.when(j == 0)
    def _():
        for b in range(nb):
            p_ref[b] = pq[b][0]
            q_ref[b] = pq[b][1]

    @pl.when(j > 0)
    def _():
        for b in range(nb):
            p_ref[b] += pq[b][0]
            q_ref[b] += pq[b][1]

    @pl.when(j == nj - 1)
    def _():
        if mode == "filt":
            nrm = rest[0][...]
            kre_ref, kim_ref = rest[1], rest[2]
            kre_ref[...] = (p_ref[0] + p_ref[1]) / nrm
            kim_ref[...] = (q_ref[1] - q_ref[0]) / nrm
        elif mode == "fwdk":
            kre, kim = rest[0][...], rest[1][...]
            yre_ref, yim_ref = rest[2], rest[3]
            for b in range(nb):
                pp, qq = p_ref[b], q_ref[b]
                yre_ref[b] = (pp * kre + qq * kim).astype(BF16)
                yim_ref[b] = (pp * kim - qq * kre).astype(BF16)
        elif mode == "inv":
            e_ref, x0_ref, o_ref = rest[0], rest[1], rest[2]
            for b in range(nb):
                o_ref[b] = ((p_ref[b] - q_ref[b] + e_ref[b]) * x0_ref[b]).astype(BF16)
        else:
            o_ref = rest[0]
            for b in range(nb):
                o_ref[b] = (p_ref[b] - q_ref[b]).astype(BF16)


def _dft(mode, kind, xs, extras, nb):
    B, L, C = xs[0].shape
    ti = min(L, 256)
    tj = min(L, 512)
    bc, bs, rc, rs = _dft_tables(kind, L, ti)
    grid = (B // nb, L // ti, L // tj)
    x_spec = pl.BlockSpec((nb, tj, C), lambda g, i, j: (g, j, 0))
    row_spec = lambda c, dt=None: pl.BlockSpec((nb, ti, c), lambda g, i, j: (g, i, 0))
    in_specs = [pl.BlockSpec((ti, L), lambda g, i, j: (0, 0)),
                pl.BlockSpec((ti, L), lambda g, i, j: (0, 0)),
                pl.BlockSpec((None, 1, L), lambda g, i, j: (i, 0, 0)),
                pl.BlockSpec((None, 1, L), lambda g, i, j: (i, 0, 0))] + [x_spec] * len(xs)
    if mode == "filt":
        in_specs += [pl.BlockSpec((1, HY_CH), lambda g, i, j: (0, 0))]
        out_shape = (jax.ShapeDtypeStruct((L, HY_CH), F32),) * 2
        out_specs = (pl.BlockSpec((ti, HY_CH), lambda g, i, j: (i, 0)),) * 2
    elif mode == "fwdk":
        in_specs += [pl.BlockSpec((ti, HY_CH), lambda g, i, j: (i, 0))] * 2
        out_shape = (jax.ShapeDtypeStruct((B, L, C), BF16),) * 2
        out_specs = (row_spec(C),) * 2
    elif mode == "inv":
        in_specs += [row_spec(C)] * 2
        out_shape = jax.ShapeDtypeStruct((B, L, C), BF16)
        out_specs = row_spec(C)
    else:
        out_shape = jax.ShapeDtypeStruct((B, L, C), BF16)
        out_specs = row_spec(C)
    return pl.pallas_call(
        functools.partial(_dft_kernel, mode, nb, len(xs)),
        out_shape=out_shape,
        grid=grid,
        in_specs=in_specs,
        out_specs=out_specs,
        scratch_shapes=[pltpu.VMEM((nb, ti, C), F32), pltpu.VMEM((nb, ti, C), F32)],
        compiler_params=_cparams(("arbitrary", "arbitrary", "arbitrary")),
        name="dft_" + mode,
    )(bc, bs, rc, rs, *xs, *extras)


FFT_R = 64
FFT_KF = 8


def _pack_pairs(x):
    return pltpu.bitcast(x.astype(BF16), jnp.uint32)


def _unpack_pairs(w):
    return pltpu.bitcast(w, BF16)


def _lead_in_kernel(g_ref, x_ref, o_ref):
    g = g_ref[...]
    for j in range(FFT_KF):
        y = jnp.dot(g, x_ref[:, j, :].astype(BF16), preferred_element_type=F32)
        o_ref[:, j, :] = _pack_pairs(y)


def _lead_in(g, x, name):
    nbx, _, _, c = x.shape
    m2 = g.shape[0] // 2
    return pl.pallas_call(
        _lead_in_kernel,
        out_shape=jax.ShapeDtypeStruct((nbx, m2, FFT_R, c), jnp.uint32),
        grid=(nbx, FFT_R // FFT_KF),
        in_specs=[pl.BlockSpec(g.shape, lambda b, k: (0, 0)),
                  pl.BlockSpec((None, FFT_R, FFT_KF, c), lambda b, k: (b, 0, k, 0))],
        out_specs=pl.BlockSpec((None, m2, FFT_KF, c), lambda b, k: (b, 0, k, 0)),
        compiler_params=_cparams(("arbitrary", "arbitrary")),
        name=name,
    )(g, x)


def _lead_out_kernel(n_extra, g_ref, w_ref, *rest):
    g = g_ref[...]
    o_ref = rest[-1]
    for j in range(FFT_KF):
        y = jnp.dot(g, _unpack_pairs(w_ref[:, j, :]), preferred_element_type=F32)
        if n_extra:
            y = (y + rest[0][:, j, :]) * rest[1][:, j, :]
        o_ref[:, j, :] = y


def _lead_out(g, w, extras, name):
    nb, k2, _, c = w.shape
    blk = lambda r: pl.BlockSpec((None, r, FFT_KF, c), lambda b, k: (b, 0, k, 0))
    return pl.pallas_call(
        functools.partial(_lead_out_kernel, len(extras)),
        out_shape=jax.ShapeDtypeStruct((nb, FFT_R, FFT_R, c), F32),
        grid=(nb, FFT_R // FFT_KF),
        in_specs=[pl.BlockSpec(g.shape, lambda b, k: (0, 0)), blk(k2)] + [blk(FFT_R)] * len(extras),
        out_specs=blk(FFT_R),
        compiler_params=_cparams(("arbitrary", "arbitrary")),
        name=name,
    )(g, w, *extras)


def _interleave(a, b, axis):
    st = np.stack([a, b], axis=axis + 1)
    shape = list(a.shape)
    shape[axis] *= 2
    return st.reshape(shape)


def _hy2_tables():
    L = FFT_R * FFT_R
    n2 = 2 * L
    f1 = np.arange(2 * FFT_R, dtype=np.int64)
    s1 = np.arange(FFT_R, dtype=np.int64)
    th = np.pi * (((2 * f1[:, None] + 1) * s1[None, :]) % (4 * FFT_R)) / (2 * FFT_R)
    ga = _interleave(np.cos(th), -np.sin(th), 0)
    ma = _interleave(np.cos(th).T, -np.sin(th).T, 1) / L
    f2 = np.arange(FFT_R // 2, dtype=np.int64)
    s2 = np.arange(FFT_R, dtype=np.int64)
    idx = ((n2 // FFT_R) * 2 * f2[None, :, None] * s2[None, None, :]
           + (2 * f1[:, None, None] + 1) * s2[None, None, :]) % (2 * n2)
    al = np.pi * idx / n2
    c, s = np.cos(al), np.sin(al)
    nmat = np.concatenate([_interleave(c, s, 2), _interleave(-s, c, 2)], axis=1)
    ct, st = np.transpose(c, (0, 2, 1)), np.transpose(s, (0, 2, 1))
    mmat = _interleave(np.concatenate([ct, -st], axis=2), np.concatenate([st, ct], axis=2), 1)
    bf = lambda a: jnp.asarray(a, F32).astype(BF16)
    return bf(ga), bf(nmat), bf(mmat), bf(ma)


def _hy_mid_kernel(a_ref, n_ref, m_ref, k_ref, e_ref):
    half = FFT_R // 2
    for j in range(FFT_KF):
        t = jnp.dot(n_ref[j], _unpack_pairs(a_ref[j]), preferred_element_type=F32)
        tr, ti = t[:half], t[half:]
        kr, ki = k_ref[j, 0], k_ref[j, 1]
        y = jnp.concatenate([tr * kr - ti * ki, tr * ki + ti * kr], axis=0).astype(BF16)
        e_ref[j] = _pack_pairs(jnp.dot(m_ref[j], y, preferred_element_type=F32))


def _hy_mid(a, nmat, mmat, khat):
    nb = a.shape[0]
    c = a.shape[-1]
    nf1 = 2 * FFT_R
    blk = pl.BlockSpec((None, FFT_KF, FFT_R, c), lambda i, b: (b, i, 0, 0))
    return pl.pallas_call(
        _hy_mid_kernel,
        out_shape=jax.ShapeDtypeStruct(a.shape, jnp.uint32),
        grid=(nf1 // FFT_KF, nb),
        in_specs=[blk,
                  pl.BlockSpec((FFT_KF, FFT_R, 2 * FFT_R), lambda i, b: (i, 0, 0)),
                  pl.BlockSpec((FFT_KF, 2 * FFT_R, FFT_R), lambda i, b: (i, 0, 0)),
                  pl.BlockSpec((FFT_KF, 2, FFT_R // 2, c), lambda i, b: (i, 0, 0, 0))],
        out_specs=blk,
        compiler_params=_cparams(("arbitrary", "arbitrary")),
        name="l0_hyena_mid",
    )(a, nmat, mmat, khat)


def _hy_kfilt_kernel(a_ref, n_ref, nrm_ref, k_ref):
    half = FFT_R // 2
    nrm = nrm_ref[...]
    for j in range(FFT_KF):
        tf = jnp.dot(n_ref[j], _unpack_pairs(a_ref[0, j]), preferred_element_type=F32)
        tb = jnp.dot(n_ref[j], _unpack_pairs(a_ref[1, j]), preferred_element_type=F32)
        k_ref[j, 0] = (tf[:half] + tb[:half]) / nrm
        k_ref[j, 1] = (tf[half:] - tb[half:]) / nrm


def _hy_kfilt(a, nmat, nrm):
    c = a.shape[-1]
    nf1 = 2 * FFT_R
    return pl.pallas_call(
        _hy_kfilt_kernel,
        out_shape=jax.ShapeDtypeStruct((nf1, 2, FFT_R // 2, c), F32),
        grid=(nf1 // FFT_KF,),
        in_specs=[pl.BlockSpec((2, FFT_KF, FFT_R, c), lambda i: (0, i, 0, 0)),
                  pl.BlockSpec((FFT_KF, FFT_R, 2 * FFT_R), lambda i: (i, 0, 0)),
                  _const_spec(nrm.shape)],
        out_specs=pl.BlockSpec((FFT_KF, 2, FFT_R // 2, c), lambda i: (i, 0, 0, 0)),
        compiler_params=_cparams(("arbitrary",)),
        name="l0_hyena_kfilt",
    )(a, nmat, nrm)


def _hyena_long(u, e, x0, hfilt, nrm, nb):
    L = FFT_R * FFT_R
    c = u.shape[-1]
    v4 = lambda a, n: a.reshape(n, FFT_R, FFT_R, c)
    ga, nmat, mmat, ma = _hy2_tables()
    khat = _hy_kfilt(_lead_in(ga, v4(hfilt, 2), "l0_hyena_fwd_a"), nmat, nrm)
    ee = _hy_mid(_lead_in(ga, v4(u, nb), "l0_hyena_fwd_a"), nmat, mmat, khat)
    y = _lead_out(ma, ee, [v4(e, nb), v4(x0, nb)], "l0_hyena_inv_a")
    return y.reshape(nb * L, c)


def _fn2_tables():
    L = FFT_R * FFT_R
    r = np.arange(FFT_R, dtype=np.int64)
    idx = (FFT_R * r[None, :, None] * r[None, None, :] + r[None, :, None] * r[:, None, None]) % L
    gm = 2.0 * np.pi * idx / L
    c, s = np.cos(gm), np.sin(gm)
    g1 = _interleave(np.concatenate([c, -s], axis=2), np.concatenate([-s, -c], axis=2), 1)
    dl = 2.0 * np.pi * ((r[:, None] * r[None, :]) % FFT_R) / FFT_R
    g2 = _interleave(np.cos(dl), np.sin(dl), 1) / math.sqrt(L * FN_GROUP_CH)
    bf = lambda a: jnp.asarray(a, F32).astype(BF16)
    return bf(g1), bf(g2)


def _fnet_s1_kernel(x_ref, m_ref, cs_ref, g1_ref, o_ref, zc_ref, zs_ref):
    xs = jnp.concatenate([x_ref[:, j, :] for j in range(FFT_KF)], axis=0)
    m = m_ref[...]
    h = (_ln_plain(xs) * (1.0 + m[:, D_MODEL:2 * D_MODEL]) + m[:, 0:D_MODEL]).astype(BF16)
    cs = cs_ref[...]
    for g in range(D_MODEL // FN_GROUP_CH):
        a = g * FN_GROUP_CH
        z = jnp.dot(h[:, a:a + FN_GROUP_CH], cs, preferred_element_type=F32)
        zc_ref[:, a:a + FN_GROUP_CH] = z[:, :FN_GROUP_CH].astype(BF16)
        zs_ref[:, a:a + FN_GROUP_CH] = z[:, FN_GROUP_CH:].astype(BF16)
    for j in range(FFT_KF):
        r0 = j * FFT_R
        s = jnp.concatenate([zc_ref[r0:r0 + FFT_R, :], zs_ref[r0:r0 + FFT_R, :]], axis=0)
        o_ref[j] = _pack_pairs(jnp.dot(g1_ref[j], s, preferred_element_type=F32))


def _fnet_long(x, mods, mod_base, nb):
    L = FFT_R * FFT_R
    d = D_MODEL
    g1, g2 = _fn2_tables()
    cs = _group_dft_table()
    bb = pl.pallas_call(
        _fnet_s1_kernel,
        out_shape=jax.ShapeDtypeStruct((nb, FFT_R, FFT_R, d), jnp.uint32),
        grid=(nb, FFT_R // FFT_KF),
        in_specs=[pl.BlockSpec((None, FFT_R, FFT_KF, d), lambda b, k: (b, 0, k, 0)),
                  pl.BlockSpec((None, 1, 6 * d), lambda b, k: (mod_base + b, 0, 0)),
                  pl.BlockSpec(cs.shape, lambda b, k: (0, 0)),
                  pl.BlockSpec((FFT_KF, 2 * FFT_R, 2 * FFT_R), lambda b, k: (k, 0, 0))],
        out_specs=pl.BlockSpec((None, FFT_KF, FFT_R, d), lambda b, k: (b, k, 0, 0)),
        scratch_shapes=[pltpu.VMEM((FFT_KF * FFT_R, d), BF16), pltpu.VMEM((FFT_KF * FFT_R, d), BF16)],
        compiler_params=_cparams(("arbitrary", "arbitrary")),
        name="l1_fnet_stage1",
    )(x.reshape(nb, FFT_R, FFT_R, d), mods, cs, g1)
    y = _lead_out(g2, bb, [], "l1_fnet_stage2")
    return y.reshape(nb * L, d)


def _group_dft_table():
    g = FN_GROUP_CH
    jk = (np.arange(g, dtype=np.int64)[:, None] * np.arange(g, dtype=np.int64)[None, :]) % g
    ang = 2.0 * np.pi * jk / g
    return jnp.asarray(np.concatenate([np.cos(ang), np.sin(ang)], axis=1), F32).astype(BF16)


def _fnet_front_kernel(x_ref, m_ref, cs_ref, zc_ref, zs_ref):
    m = m_ref[...]
    h = (_ln_plain(x_ref[...]) * (1.0 + m[:, D_MODEL:2 * D_MODEL]) + m[:, 0:D_MODEL]).astype(BF16)
    cs = cs_ref[...]
    for g in range(D_MODEL // FN_GROUP_CH):
        a = g * FN_GROUP_CH
        z = jnp.dot(h[:, a:a + FN_GROUP_CH], cs, preferred_element_type=F32)
        zc_ref[:, a:a + FN_GROUP_CH] = z[:, :FN_GROUP_CH].astype(BF16)
        zs_ref[:, a:a + FN_GROUP_CH] = z[:, FN_GROUP_CH:].astype(BF16)


def _fnet_front(x, mods, mod_base, tiles_per_mod):
    t = x.shape[0]
    tm = ROW_TILE
    cs = _group_dft_table()
    return pl.pallas_call(
        _fnet_front_kernel,
        out_shape=(jax.ShapeDtypeStruct((t, D_MODEL), BF16),) * 2,
        grid=(t // tm,),
        in_specs=[pl.BlockSpec((tm, D_MODEL), lambda i: (i, 0)),
                  _mod_spec(mod_base, tiles_per_mod),
                  _const_spec(cs.shape)],
        out_specs=(pl.BlockSpec((tm, D_MODEL), lambda i: (i, 0)),) * 2,
        compiler_params=_cparams(("arbitrary",)),
        name="l1_fnet_front",
    )(x, mods, cs)


def _post_kernel(n_a, *refs):
    x_ref, m_ref = refs[0], refs[1]
    a_refs = refs[2:2 + n_a]
    wo_refs = refs[2 + n_a:2 + 2 * n_a]
    g1_ref, b1_ref, w1_ref, w2_ref, g2_ref, b2_ref, o_ref = refs[2 + 2 * n_a:]
    m = m_ref[...]
    d = D_MODEL
    out = _bdot(a_refs[0][...], wo_refs[0][...])
    for a_ref, wo_ref in zip(a_refs[1:], wo_refs[1:]):
        out += _bdot(a_ref[...], wo_ref[...])
    x1 = _ln_plain(ALPHA * x_ref[...] + m[:, 2 * d:3 * d] * out) * g1_ref[...] + b1_ref[...]
    h = (_ln_plain(x1) * (1.0 + m[:, 4 * d:5 * d]) + m[:, 3 * d:4 * d]).astype(BF16)
    acc = jnp.zeros_like(x1)
    for c in range(D_FF // d):
        hc = jnp.maximum(jnp.dot(h, w1_ref[:, c * d:(c + 1) * d], preferred_element_type=F32), 0.0)
        acc += jnp.dot((hc * hc).astype(BF16), w2_ref[c * d:(c + 1) * d, :], preferred_element_type=F32)
    o_ref[...] = _ln_plain(ALPHA * x1 + m[:, 5 * d:6 * d] * acc) * g2_ref[...] + b2_ref[...]


def _post(x, mods, mod_base, tiles_per_mod, a_list, wo_list, g1, b1, w1, w2, g2, b2):
    t = x.shape[0]
    tm = ROW_TILE
    row = lambda c: pl.BlockSpec((tm, c), lambda i: (i, 0))
    in_specs = ([row(D_MODEL), _mod_spec(mod_base, tiles_per_mod)]
                + [row(a.shape[1]) for a in a_list]
                + [_const_spec(w.shape) for w in wo_list]
                + [_const_spec(v.shape) for v in (g1, b1, w1, w2, g2, b2)])
    return pl.pallas_call(
        functools.partial(_post_kernel, len(a_list)),
        out_shape=jax.ShapeDtypeStruct((t, D_MODEL), F32),
        grid=(t // tm,),
        in_specs=in_specs,
        out_specs=row(D_MODEL),
        compiler_params=_cparams(("arbitrary",)),
        name="post_mlp",
    )(x, mods, *a_list, *wo_list, g1, b1, w1, w2, g2, b2)


def _rot_cols(w):
    parts = []
    for seg in range(2):
        o = seg * 32
        parts += [-w[:, o + 16:o + 32], w[:, o:o + 16]]
    return jnp.concatenate(parts, axis=1)


def _pad_cols(w, n):
    return jnp.pad(w, ((0, 0), (0, n - w.shape[1])))


def _rope_tables(L):
    rows = L // GRID_W
    row = np.repeat(np.arange(rows, dtype=np.float64), GRID_W)
    col = np.tile(np.arange(GRID_W, dtype=np.float64), rows)
    half = QK_ROPE // 2
    inv = 1.0 / (ROPE_THETA ** (np.arange(0, half, 2, dtype=np.float64) / half))
    ar = row[:, None] * inv[None, :]
    ac = col[:, None] * inv[None, :]
    ang = np.concatenate([ar, ar, ac, ac], axis=1)
    cos = np.concatenate([np.cos(ang), np.ones_like(ang)], axis=1)
    sin = np.concatenate([np.sin(ang), np.zeros_like(ang)], axis=1)
    return jnp.asarray(cos, F32), jnp.asarray(sin, F32)


def kernel(x_prompt, x_sample, cache_l0_ckv, cache_l0_krope, c, c_ctx, l0_ada_w, l0_ada_b, l0_w_in, l0_conv_w, l0_conv_b, l0_hf_w1, l0_hf_b1, l0_hf_freq, l0_hf_w2, l0_hf_b2, l0_hf_w3, l0_hf_skip, l0_q_norm, l0_q_up, l0_kv_norm, l0_kv_up, l0_w_out, l0_ln1_g, l0_ln1_b, l0_mlp_w1, l0_mlp_w2, l0_ln2_g, l0_ln2_b, l1_ada_w, l1_ada_b, l1_w_out, l1_ln1_g, l1_ln1_b, l1_mlp_w1, l1_mlp_w2, l1_ln2_g, l1_ln2_b):
    nbc, lc, d = x_prompt.shape
    nbs, ls, _ = x_sample.shape
    past = cache_l0_ckv.shape[1]
    tm = ROW_TILE
    row1 = lambda v: v.reshape(1, -1)

    cond8 = jnp.concatenate([c_ctx[None, :], c, jnp.zeros((8 - 1 - nbs, d), F32)], axis=0)
    mods0 = _modulation(cond8, l0_ada_w, l0_ada_b)
    mods1 = _modulation(cond8, l1_ada_w, l1_ada_b)

    kpe_w = l0_w_in[:, 1920:1984]
    win = jnp.concatenate([l0_w_in[:, :1920], _pad_cols(kpe_w, LANE), _pad_cols(_rot_cols(kpe_w), LANE)],
                          axis=1).astype(BF16)
    dh = QK_NOPE + QK_ROPE
    q_nope = [l0_q_up[:, h * dh:h * dh + QK_NOPE] for h in range(MLA_HEADS)]
    q_pe = [l0_q_up[:, h * dh + QK_NOPE:(h + 1) * dh] for h in range(MLA_HEADS)]
    qup = jnp.concatenate(q_nope + [_pad_cols(w, LANE) for w in q_pe]
                          + [_pad_cols(_rot_cols(w), LANE) for w in q_pe], axis=1).astype(BF16)
    kvup = l0_kv_up.astype(BF16)
    front_w = (win, row1(l0_q_norm), qup, row1(l0_kv_norm), kvup)
    w1p = jnp.pad(l0_hf_w1, ((0, LANE - l0_hf_w1.shape[0]), (0, 0)))
    filt_w = (w1p, row1(l0_hf_b1), row1(l0_hf_freq), l0_hf_w2, row1(l0_hf_b2), l0_hf_w3.astype(BF16))
    wo0 = l0_w_out.astype(BF16)
    conv_b = row1(l0_conv_b)
    skip = row1(l0_hf_skip)

    xc = x_prompt.reshape(nbc * lc, d)
    xs = x_sample.reshape(nbs * ls, d)
    groups = (
        dict(x=xc, nb=nbc, L=lc, mod_base=0, tiles_per_mod=nbc * lc // tm, dft_nb=4, tq=lc),
        dict(x=xs, nb=nbs, L=ls, mod_base=1, tiles_per_mod=ls // tm, dft_nb=nbs, tq=256),
    )
    ones_tab = (jnp.concatenate([jnp.ones((tm, LANE), F32)], axis=0), jnp.zeros((tm, LANE), F32))

    outs = []
    ctx_ckv = ctx_krope = None
    for gi, g in enumerate(groups):
        nb, L = g["nb"], g["L"]
        tiles_per_seq = L // tm
        latent = gi == 1
        cos, sin = _rope_tables(L) if latent else ones_tab
        hy, q, k, v, kvn, kpe = _front(g["x"], mods0, g["mod_base"], g["tiles_per_mod"], front_w,
                                       cos, sin, tiles_per_seq if latent else 1)
        k = k.reshape(MLA_HEADS, nb, L, QK_PAD)
        v = v.reshape(MLA_HEADS, nb, L, V_HEAD)
        if latent:
            kc, vc = _cache_kv(cache_l0_ckv.reshape(nbs * past, KV_LORA),
                               _pad_cols(cache_l0_krope.reshape(nbs * past, QK_ROPE), LANE), kvup)
            k = jnp.concatenate([k, kc.reshape(MLA_HEADS, nb, past, QK_PAD)], axis=2)
            v = jnp.concatenate([v, vc.reshape(MLA_HEADS, nb, past, V_HEAD)], axis=2)
        else:
            ctx_ckv = kvn.reshape(nb, L, KV_LORA)
            ctx_krope = kpe.reshape(nb, L, QK_ROPE)
        y_mla = _attention(q, k, v, nb, L, g["tq"])

        two_stage = L == FFT_R * FFT_R
        io_dtype = F32 if two_stage else BF16
        u, e, x0 = _conv_gate(hy, l0_conv_w, conv_b, skip, tiles_per_seq, io_dtype)
        hfilt, hnorm = _filters(L, io_dtype, *filt_w)
        if two_stage:
            y_hy = _hyena_long(u, e, x0, hfilt, hnorm, nb)
        else:
            kre, kim = _dft("filt", "hy_fwd", [hfilt], [hnorm], 2)
            sh = (nb, L, HY_CH)
            yre, yim = _dft("fwdk", "hy_fwd", [u.reshape(sh)], [kre, kim], g["dft_nb"])
            y_hy = _dft("inv", "hy_inv", [yre, yim], [e.reshape(sh), x0.reshape(sh)], g["dft_nb"])
            y_hy = y_hy.reshape(nb * L, HY_CH)

        x1 = _post(g["x"], mods0, g["mod_base"], g["tiles_per_mod"], [y_hy, y_mla],
                   [wo0[:HY_CH], wo0[HY_CH:]], row1(l0_ln1_g), row1(l0_ln1_b),
                   l0_mlp_w1.astype(BF16), l0_mlp_w2.astype(BF16), row1(l0_ln2_g), row1(l0_ln2_b))

        if L == FFT_R * FFT_R:
            yf = _fnet_long(x1, mods1, g["mod_base"], nb)
        else:
            zc, zs = _fnet_front(x1, mods1, g["mod_base"], g["tiles_per_mod"])
            sh = (nb, L, d)
            yf = _dft("fnet", "fnet", [zc.reshape(sh), zs.reshape(sh)], [], min(g["dft_nb"], 2))
            yf = yf.reshape(nb * L, d)
        x2 = _post(x1, mods1, g["mod_base"], g["tiles_per_mod"], [yf],
                   [l1_w_out.astype(BF16)], row1(l1_ln1_g), row1(l1_ln1_b),
                   l1_mlp_w1.astype(BF16), l1_mlp_w2.astype(BF16), row1(l1_ln2_g), row1(l1_ln2_b))
        outs.append(x2.reshape(nb, L, d))

    return (outs[0], outs[1], ctx_ckv, ctx_krope)
```

```python
import functools
import math

import numpy as np
import jax
import jax.numpy as jnp
from jax import lax
from jax.experimental import pallas as pl
from jax.experimental.pallas import tpu as pltpu

F32 = jnp.float32
BF16 = jnp.bfloat16
HI = lax.Precision.HIGHEST

D_MODEL = 1024
DEPTH = 2
GRID_W = 64
HY_CH = 512
FILT_BANDS = 16
FILT_ORDER = 64
FAST_DECAY_PCT = 0.3
SLOW_DECAY_PCT = 1.5
DECAY_TARGET = 1e-2
MAX_DECAY = math.log(DECAY_TARGET) / FAST_DECAY_PCT
MIN_DECAY = math.log(DECAY_TARGET) / SLOW_DECAY_PCT
MLA_HEADS = 4
QK_NOPE = 128
QK_ROPE = 64
V_HEAD = 128
Q_LORA = 256
KV_LORA = 128
ROPE_THETA = 10000.0
FN_GROUP_CH = 128
D_FF = 4096
ALPHA = (2 * DEPTH) ** 0.25
LN_EPS = 1e-5
RMS_EPS = 1e-6

LANE = 128
ROW_TILE = 256
QK_PAD = 256
VMEM_LIMIT = 56 * 1024 * 1024


def _cparams(sem):
    return pltpu.CompilerParams(dimension_semantics=sem, vmem_limit_bytes=VMEM_LIMIT)


def _ln_plain(x):
    mu = jnp.mean(x, axis=-1, keepdims=True)
    xc = x - mu
    var = jnp.mean(xc * xc, axis=-1, keepdims=True)
    return xc * lax.rsqrt(var + LN_EPS)


def _rms(x, g):
    return x * lax.rsqrt(jnp.mean(x * x, axis=-1, keepdims=True) + RMS_EPS) * g


def _bdot(a, b):
    return jnp.dot(a.astype(BF16), b, preferred_element_type=F32)


def _mod_kernel(c_ref, w_ref, b_ref, o_ref):
    c = c_ref[...]
    s = c / (1.0 + jnp.exp(-c))
    o_ref[...] = jnp.dot(s, w_ref[...], precision=HI, preferred_element_type=F32) + b_ref[...]


def _modulation(cond8, w, b):
    n = w.shape[1]
    tn = 1536
    out = pl.pallas_call(
        _mod_kernel,
        out_shape=jax.ShapeDtypeStruct((8, n), F32),
        grid=(n // tn,),
        in_specs=[pl.BlockSpec((8, D_MODEL), lambda j: (0, 0)),
                  pl.BlockSpec((D_MODEL, tn), lambda j: (0, j)),
                  pl.BlockSpec((1, tn), lambda j: (0, j))],
        out_specs=pl.BlockSpec((8, tn), lambda j: (0, j)),
        compiler_params=_cparams(("arbitrary",)),
        name="modulation",
    )(cond8, w, b.reshape(1, n))
    return out.reshape(8, 1, n)


def _mod_spec(mod_base, tiles_per_mod):
    return pl.BlockSpec((None, 1, 6 * D_MODEL), lambda i: (mod_base + i // tiles_per_mod, 0, 0))


def _const_spec(shape):
    nd = len(shape)
    return pl.BlockSpec(shape, lambda i: (0,) * nd)


def _front_kernel(x_ref, m_ref, win_ref, qn_ref, qup_ref, kvn_ref, kvup_ref, cos_ref, sin_ref,
                  hy_ref, q_ref, k_ref, v_ref, kvn_out_ref, kpe_ref):
    m = m_ref[...]
    h = _ln_plain(x_ref[...]) * (1.0 + m[:, D_MODEL:2 * D_MODEL]) + m[:, 0:D_MODEL]
    z = _bdot(h, win_ref[...])
    hy_ref[...] = z[:, :3 * HY_CH]
    q_c = z[:, 1536:1792]
    kv_c = z[:, 1792:1920]
    cos = cos_ref[...]
    sin = sin_ref[...]
    kpe = z[:, 1920:2048] * cos + z[:, 2048:2176] * sin
    kpe_ref[...] = kpe[:, :QK_ROPE]
    kpe_b = kpe.astype(BF16)
    q = _bdot(_rms(q_c, qn_ref[...]), qup_ref[...]) * (1.0 / math.sqrt(QK_NOPE + QK_ROPE))
    kvn = _rms(kv_c, kvn_ref[...])
    kvn_out_ref[...] = kvn
    kv = _bdot(kvn, kvup_ref[...])
    for hd in range(MLA_HEADS):
        a = hd * LANE
        q_pe = (q[:, 512 + a:512 + a + LANE] * cos + q[:, 1024 + a:1024 + a + LANE] * sin).astype(BF16)
        q_ref[hd] = jnp.concatenate([q[:, a:a + LANE].astype(BF16), q_pe], axis=-1)
        k_ref[hd] = jnp.concatenate([kv[:, 2 * a:2 * a + LANE].astype(BF16), kpe_b], axis=-1)
        v_ref[hd] = jnp.transpose(kv[:, 2 * a + LANE:2 * a + 2 * LANE]).astype(BF16)


def _front(x, mods, mod_base, tiles_per_mod, w, cos, sin, tiles_per_seq):
    t = x.shape[0]
    tm = ROW_TILE
    win, qn, qup, kvn, kvup = w
    if tiles_per_seq == 1:
        tab_spec = pl.BlockSpec((tm, LANE), lambda i: (0, 0))
    else:
        tab_spec = pl.BlockSpec((tm, LANE), lambda i: (i % tiles_per_seq, 0))
    return pl.pallas_call(
        _front_kernel,
        out_shape=(jax.ShapeDtypeStruct((t, 3 * HY_CH), F32),
                   jax.ShapeDtypeStruct((MLA_HEADS, t, QK_PAD), BF16),
                   jax.ShapeDtypeStruct((MLA_HEADS, t, QK_PAD), BF16),
                   jax.ShapeDtypeStruct((MLA_HEADS, V_HEAD, t), BF16),
                   jax.ShapeDtypeStruct((t, KV_LORA), F32),
                   jax.ShapeDtypeStruct((t, QK_ROPE), F32)),
        grid=(t // tm,),
        in_specs=[pl.BlockSpec((tm, D_MODEL), lambda i: (i, 0)),
                  _mod_spec(mod_base, tiles_per_mod),
                  _const_spec(win.shape), _const_spec(qn.shape), _const_spec(qup.shape),
                  _const_spec(kvn.shape), _const_spec(kvup.shape),
                  tab_spec, tab_spec],
        out_specs=(pl.BlockSpec((tm, 3 * HY_CH), lambda i: (i, 0)),
                   pl.BlockSpec((MLA_HEADS, tm, QK_PAD), lambda i: (0, i, 0)),
                   pl.BlockSpec((MLA_HEADS, tm, QK_PAD), lambda i: (0, i, 0)),
                   pl.BlockSpec((MLA_HEADS, V_HEAD, tm), lambda i: (0, 0, i)),
                   pl.BlockSpec((tm, KV_LORA), lambda i: (i, 0)),
                   pl.BlockSpec((tm, QK_ROPE), lambda i: (i, 0))),
        compiler_params=_cparams(("arbitrary",)),
        name="l0_front",
    )(x, mods, win, qn, qup, kvn, kvup, cos, sin)


def _cache_kv_kernel(ckv_ref, kr_ref, kvup_ref, k_ref, v_ref):
    kv = _bdot(ckv_ref[...], kvup_ref[...])
    kr = kr_ref[...].astype(BF16)
    for hd in range(MLA_HEADS):
        a = 2 * hd * LANE
        k_ref[hd] = jnp.concatenate([kv[:, a:a + LANE].astype(BF16), kr], axis=-1)
        v_ref[hd] = jnp.transpose(kv[:, a + LANE:a + 2 * LANE]).astype(BF16)


def _cache_kv(ckv, krope_pad, kvup):
    t = ckv.shape[0]
    return pl.pallas_call(
        _cache_kv_kernel,
        out_shape=(jax.ShapeDtypeStruct((MLA_HEADS, t, QK_PAD), BF16),
                   jax.ShapeDtypeStruct((MLA_HEADS, V_HEAD, t), BF16)),
        name="l0_cache_kv",
    )(ckv, krope_pad, kvup)


def _col_reduce(x, op):
    rows, n = x.shape
    for g in (32, 8):
        if rows % (8 * g) == 0 and rows > 8 * g:
            x = op(x.reshape(rows // (8 * g), 8 * g, n), axis=0)
            rows = 8 * g
    return op(x, axis=0, keepdims=True)


def _attn_kernel(n_kv, q_ref, *refs):
    k_refs, vt_refs = refs[:n_kv], refs[n_kv:2 * n_kv]
    o_ref, s_even, s_odd = refs[2 * n_kv:]
    i = pl.program_id(0)

    @pl.when(i == 0)
    def _():
        s_odd[...] = jnp.zeros_like(s_odd)

    def step(s_write, s_read):
        q = q_ref[...]
        nt = (((1,), (1,)), ((), ()))
        r0 = 0
        for k_ref in k_refs:
            lk = k_ref.shape[0]
            s_write[r0:r0 + lk, :] = lax.dot_general(k_ref[...], q, nt, preferred_element_type=F32)
            r0 += lk
        s = s_read[...]
        m = _col_reduce(s, jnp.max)
        p = jnp.exp(s - m)
        l = _col_reduce(p, jnp.sum)
        pb = p.astype(BF16)
        acc = None
        r0 = 0
        for vt_ref in vt_refs:
            lk = vt_ref.shape[1]
            pv = jnp.dot(vt_ref[...], pb[r0:r0 + lk, :], preferred_element_type=F32)
            acc = pv if acc is None else acc + pv
            r0 += lk
        o_ref[...] = jnp.transpose(acc / l).astype(o_ref.dtype)

    pl.when(i % 2 == 0)(lambda: step(s_even, s_odd))
    pl.when(i % 2 == 1)(lambda: step(s_odd, s_even))


def _attention(q, k, vt, extra, nb, lq, tq):
    nq = lq // tq
    n_tiles = nb * MLA_HEADS * nq

    def where(t):
        bh = t // nq
        return bh // MLA_HEADS, bh % MLA_HEADS, t % nq

    def score_side(fn):
        return lambda i: fn(*where(jnp.minimum(i, n_tiles - 1)))

    def value_side(fn):
        return lambda i: fn(*where(jnp.maximum(i - 1, 0)))

    ks, vts = [k], [vt]
    if extra is not None:
        ks.append(extra[0])
        vts.append(extra[1])
    in_specs = [pl.BlockSpec((None, tq, QK_PAD), score_side(lambda b, h, j: (h, b * nq + j, 0)))]
    in_specs += [pl.BlockSpec((None, a.shape[1] // nb, QK_PAD), score_side(lambda b, h, j: (h, b, 0))) for a in ks]
    in_specs += [pl.BlockSpec((None, V_HEAD, a.shape[2] // nb), value_side(lambda b, h, j: (h, 0, b))) for a in vts]
    lk_total = sum(a.shape[1] // nb for a in ks)
    return pl.pallas_call(
        functools.partial(_attn_kernel, len(ks)),
        out_shape=jax.ShapeDtypeStruct((nb * lq, MLA_HEADS * V_HEAD), BF16),
        grid=(n_tiles + 1,),
        in_specs=in_specs,
        out_specs=pl.BlockSpec((tq, V_HEAD), value_side(lambda b, h, j: (b * nq + j, h))),
        scratch_shapes=[pltpu.VMEM((lk_total, tq), F32), pltpu.VMEM((lk_total, tq), F32)],
        compiler_params=_cparams(("arbitrary",)),
        name="l0_attention",
    )(q, *ks, *vts)


def _conv_gate_kernel(tiles_per_seq, hy_ref, prev_ref, next_ref, w_ref, b_ref, skip_ref,
                      u_ref, e_ref, x0_ref):
    i = pl.program_id(0)
    x = hy_ref[...]
    tm = x.shape[0]
    pos = i % tiles_per_seq
    prev_row = jnp.where(pos == 0, 0.0, prev_ref[7:8, :])
    next_row = jnp.where(pos == tiles_per_seq - 1, 0.0, next_ref[0:1, :])
    rows = lax.broadcasted_iota(jnp.int32, x.shape, 0)
    xm1 = jnp.where(rows == 0, prev_row, pltpu.roll(x, 1, 0))
    xp1 = jnp.where(rows == tm - 1, next_row, pltpu.roll(x, tm - 1, 0))
    w = w_ref[...]
    p = xm1 * w[0:1, :] + x * w[1:2, :] + xp1 * w[2:3, :] + b_ref[...]
    u = p[:, 2 * HY_CH:] * p[:, HY_CH:2 * HY_CH]
    u_ref[...] = u.astype(u_ref.dtype)
    e_ref[...] = u * skip_ref[...]
    x0_ref[...] = p[:, :HY_CH]


def _conv_gate(hy, conv_w, conv_b, skip, tiles_per_seq, u_dtype):
    t = hy.shape[0]
    tm = ROW_TILE
    r8 = tm // 8
    n8 = t // 8
    return pl.pallas_call(
        functools.partial(_conv_gate_kernel, tiles_per_seq),
        out_shape=(jax.ShapeDtypeStruct((t, HY_CH), u_dtype),
                   jax.ShapeDtypeStruct((t, HY_CH), F32),
                   jax.ShapeDtypeStruct((t, HY_CH), F32)),
        grid=(t // tm,),
        in_specs=[pl.BlockSpec((tm, 3 * HY_CH), lambda i: (i, 0)),
                  pl.BlockSpec((8, 3 * HY_CH), lambda i: (jnp.maximum(i * r8 - 1, 0), 0)),
                  pl.BlockSpec((8, 3 * HY_CH), lambda i: (jnp.minimum((i + 1) * r8, n8 - 1), 0)),
                  _const_spec(conv_w.shape), _const_spec(conv_b.shape), _const_spec(skip.shape)],
        out_specs=(pl.BlockSpec((tm, HY_CH), lambda i: (i, 0)),
                   pl.BlockSpec((tm, HY_CH), lambda i: (i, 0)),
                   pl.BlockSpec((tm, HY_CH), lambda i: (i, 0))),
        compiler_params=_cparams(("arbitrary",)),
        name="l0_conv_gate",
    )(hy, hy, hy, conv_w, conv_b, skip)


def _filter_kernel(z_ref, w1_ref, b1_ref, fr_ref, w2_ref, b2_ref, w3_ref, dl_ref, h_ref, norm_ref):
    i = pl.program_id(0)
    z = z_ref[...]
    tl = z.shape[0]
    fr = fr_ref[...]
    h = jnp.sin(fr * (jnp.dot(z, w1_ref[...], precision=HI, preferred_element_type=F32) + b1_ref[...]))
    h = jnp.sin(fr * (jnp.dot(h, w2_ref[...], precision=HI, preferred_element_type=F32) + b2_ref[...]))
    h = _bdot(h, w3_ref[...])
    decay = jnp.exp(-(z[:, 0:1] * dl_ref[...]))
    hf = h[:, :HY_CH] * decay
    hb = h[:, HY_CH:] * decay
    part = jnp.sum(jnp.abs(hf) + jnp.abs(hb), axis=0, keepdims=True)

    @pl.when(i == 0)
    def _():
        norm_ref[...] = part

    @pl.when(i > 0)
    def _():
        norm_ref[...] += part

    rows = lax.broadcasted_iota(jnp.int32, hb.shape, 0) + i * tl
    h_ref[0] = hf.astype(h_ref.dtype)
    h_ref[1] = jnp.where(rows == 0, 0.0, hb).astype(h_ref.dtype)


def _filter_embedding(L):
    t = np.linspace(0.0, 1.0, L)[:, None]
    w_ang = 2.0 * np.pi * np.arange(L) / L
    bands = np.linspace(1e-4, FILT_BANDS - 1, FILT_BANDS)
    ang = w_ang[:, None] * bands[None, :]
    z = np.zeros((L, LANE), np.float64)
    z[:, 0:1] = t
    z[:, 1:1 + FILT_BANDS] = np.cos(ang)
    z[:, 1 + FILT_BANDS:1 + 2 * FILT_BANDS] = -np.sin(ang)
    return jnp.asarray(z, F32)


def _filters(L, h_dtype, w1p, b1, fr, w2, b2, w3):
    tl = min(L, 512)
    z = _filter_embedding(L)
    deltas = jnp.asarray(np.abs(np.linspace(MIN_DECAY, MAX_DECAY, HY_CH))[None, :], F32)
    return pl.pallas_call(
        _filter_kernel,
        out_shape=(jax.ShapeDtypeStruct((2, L, HY_CH), h_dtype), jax.ShapeDtypeStruct((1, HY_CH), F32)),
        grid=(L // tl,),
        in_specs=[pl.BlockSpec((tl, LANE), lambda i: (i, 0)),
                  _const_spec(w1p.shape), _const_spec(b1.shape), _const_spec(fr.shape),
                  _const_spec(w2.shape), _const_spec(b2.shape), _const_spec(w3.shape),
                  _const_spec(deltas.shape)],
        out_specs=(pl.BlockSpec((2, tl, HY_CH), lambda i: (0, i, 0)),
                   pl.BlockSpec((1, HY_CH), lambda i: (0, 0))),
        compiler_params=_cparams(("arbitrary",)),
        name="l0_hyena_filters",
    )(z, w1p, b1, fr, w2, b2, w3, deltas)


def _dft_tables(kind, L, ti):
    ni = L // ti
    i = np.arange(ti, dtype=np.int64)[:, None]
    big = (np.arange(ni, dtype=np.int64) * ti)[:, None]
    c = np.arange(L, dtype=np.int64)[None, :]
    if kind == "hy_fwd":
        period = 4 * L
        base_idx = (2 * i + 1) * c
        r_idx = 2 * big * c
        scale = 1.0
    elif kind == "hy_inv":
        period = 4 * L
        base_idx = (2 * c + 1) * i
        r_idx = (2 * c + 1) * big
        scale = 1.0 / L
    else:
        period = L
        base_idx = i * c
        r_idx = big * c
        scale = 1.0 / math.sqrt(L * FN_GROUP_CH)
    ab = 2.0 * np.pi * (base_idx % period) / period
    ar = 2.0 * np.pi * (r_idx % period) / period
    return (jnp.asarray(np.cos(ab), F32), jnp.asarray(np.sin(ab), F32),
            jnp.asarray(scale * np.cos(ar), F32).reshape(ni, 1, L),
            jnp.asarray(scale * np.sin(ar), F32).reshape(ni, 1, L))


def _dft_kernel(mode, nb, n_x, *refs):
    bc_ref, bs_ref, rc_ref, rs_ref = refs[:4]
    x_refs = refs[4:4 + n_x]
    rest = refs[4 + n_x:]
    p_ref, q_ref = rest[-2], rest[-1]
    j = pl.program_id(2)
    nj = pl.num_programs(2)
    tj = x_refs[0].shape[1]
    if bc_ref.shape[1] == tj:
        bc, bs, rc, rs = bc_ref[...], bs_ref[...], rc_ref[...], rs_ref[...]
    else:
        off = pl.multiple_of(j * tj, tj)
        bc, bs = bc_ref[:, pl.ds(off, tj)], bs_ref[:, pl.ds(off, tj)]
        rc, rs = rc_ref[:, pl.ds(off, tj)], rs_ref[:, pl.ds(off, tj)]
    tc = (bc * rc - bs * rs).astype(BF16)
    ts = (bs * rc + bc * rs).astype(BF16)
    x1_ref = x_refs[0]
    x2_ref = x_refs[-1]

    pq = [(jnp.dot(tc, x1_ref[b], preferred_element_type=F32),
           jnp.dot(ts, x2_ref[b], preferred_element_type=F32)) for b in range(nb)]

    @pl.when(j == 0)
    def _():
        for b in range(nb):
            p_ref[b] = pq[b][0]
            q_ref[b] = pq[b][1]

    @pl.when(j > 0)
    def _():
        for b in range(nb):
            p_ref[b] += pq[b][0]
            q_ref[b] += pq[b][1]

    @pl.when(j == nj - 1)
    def _():
        if mode == "filt":
            nrm = rest[0][...]
            kre_ref, kim_ref = rest[1], rest[2]
            kre_ref[...] = (p_ref[0] + p_ref[1]) / nrm
            kim_ref[...] = (q_ref[1] - q_ref[0]) / nrm
        elif mode == "fwdk":
            kre, kim = rest[0][...], rest[1][...]
            yre_ref, yim_ref = rest[2], rest[3]
            for b in range(nb):
                pp, qq = p_ref[b], q_ref[b]
                yre_ref[b] = (pp * kre + qq * kim).astype(BF16)
                yim_ref[b] = (pp * kim - qq * kre).astype(BF16)
        elif mode == "inv":
            e_ref, x0_ref, o_ref = rest[0], rest[1], rest[2]
            for b in range(nb):
                o_ref[b] = ((p_ref[b] - q_ref[b] + e_ref[b]) * x0_ref[b]).astype(BF16)
        else:
            o_ref = rest[0]
            for b in range(nb):
                o_ref[b] = (p_ref[b] - q_ref[b]).astype(BF16)


def _dft(mode, kind, xs, extras, nb):
    B, L, C = xs[0].shape
    ti = min(L, 256)
    tj = min(L, 512)
    bc, bs, rc, rs = _dft_tables(kind, L, ti)
    grid = (B // nb, L // ti, L // tj)
    x_spec = pl.BlockSpec((nb, tj, C), lambda g, i, j: (g, j, 0))
    row_spec = lambda c, dt=None: pl.BlockSpec((nb, ti, c), lambda g, i, j: (g, i, 0))
    in_specs = [pl.BlockSpec((ti, L), lambda g, i, j: (0, 0)),
                pl.BlockSpec((ti, L), lambda g, i, j: (0, 0)),
                pl.BlockSpec((None, 1, L), lambda g, i, j: (i, 0, 0)),
                pl.BlockSpec((None, 1, L), lambda g, i, j: (i, 0, 0))] + [x_spec] * len(xs)
    if mode == "filt":
        in_specs += [pl.BlockSpec((1, HY_CH), lambda g, i, j: (0, 0))]
        out_shape = (jax.ShapeDtypeStruct((L, HY_CH), F32),) * 2
        out_specs = (pl.BlockSpec((ti, HY_CH), lambda g, i, j: (i, 0)),) * 2
    elif mode == "fwdk":
        in_specs += [pl.BlockSpec((ti, HY_CH), lambda g, i, j: (i, 0))] * 2
        out_shape = (jax.ShapeDtypeStruct((B, L, C), BF16),) * 2
        out_specs = (row_spec(C),) * 2
    elif mode == "inv":
        in_specs += [row_spec(C)] * 2
        out_shape = jax.ShapeDtypeStruct((B, L, C), BF16)
        out_specs = row_spec(C)
    else:
        out_shape = jax.ShapeDtypeStruct((B, L, C), BF16)
        out_specs = row_spec(C)
    return pl.pallas_call(
        functools.partial(_dft_kernel, mode, nb, len(xs)),
        out_shape=out_shape,
        grid=grid,
        in_specs=in_specs,
        out_specs=out_specs,
        scratch_shapes=[pltpu.VMEM((nb, ti, C), F32), pltpu.VMEM((nb, ti, C), F32)],
        compiler_params=_cparams(("arbitrary", "arbitrary", "arbitrary")),
        name="dft_" + mode,
    )(bc, bs, rc, rs, *xs, *extras)


FFT_R = 64
FFT_KF = 8


def _pack_pairs(x):
    return pltpu.bitcast(x.astype(BF16), jnp.uint32)


def _unpack_pairs(w):
    return pltpu.bitcast(w, BF16)


def _lead_in_kernel(g_ref, x_ref, o_ref):
    g = g_ref[...]
    for j in range(FFT_KF):
        y = jnp.dot(g, x_ref[:, j, :].astype(BF16), preferred_element_type=F32)
        o_ref[:, j, :] = _pack_pairs(y)


def _lead_in(g, x, name):
    nbx, _, _, c = x.shape
    m2 = g.shape[0] // 2
    return pl.pallas_call(
        _lead_in_kernel,
        out_shape=jax.ShapeDtypeStruct((nbx, m2, FFT_R, c), jnp.uint32),
        grid=(nbx, FFT_R // FFT_KF),
        in_specs=[pl.BlockSpec(g.shape, lambda b, k: (0, 0)),
                  pl.BlockSpec((None, FFT_R, FFT_KF, c), lambda b, k: (b, 0, k, 0))],
        out_specs=pl.BlockSpec((None, m2, FFT_KF, c), lambda b, k: (b, 0, k, 0)),
        compiler_params=_cparams(("arbitrary", "arbitrary")),
        name=name,
    )(g, x)


def _lead_out_kernel(n_extra, g_ref, w_ref, *rest):
    g = g_ref[...]
    o_ref = rest[-1]
    for j in range(FFT_KF):
        y = jnp.dot(g, _unpack_pairs(w_ref[:, j, :]), preferred_element_type=F32)
        if n_extra:
            y = (y + rest[0][:, j, :]) * rest[1][:, j, :]
        o_ref[:, j, :] = y


def _lead_out(g, w, extras, name):
    nb, k2, _, c = w.shape
    blk = lambda r: pl.BlockSpec((None, r, FFT_KF, c), lambda b, k: (b, 0, k, 0))
    return pl.pallas_call(
        functools.partial(_lead_out_kernel, len(extras)),
        out_shape=jax.ShapeDtypeStruct((nb, FFT_R, FFT_R, c), F32),
        grid=(nb, FFT_R // FFT_KF),
        in_specs=[pl.BlockSpec(g.shape, lambda b, k: (0, 0)), blk(k2)] + [blk(FFT_R)] * len(extras),
        out_specs=blk(FFT_R),
        compiler_params=_cparams(("arbitrary", "arbitrary")),
        name=name,
    )(g, w, *extras)


def _interleave(a, b, axis):
    st = np.stack([a, b], axis=axis + 1)
    shape = list(a.shape)
    shape[axis] *= 2
    return st.reshape(shape)


def _hy2_tables():
    L = FFT_R * FFT_R
    n2 = 2 * L
    f1 = np.arange(2 * FFT_R, dtype=np.int64)
    s1 = np.arange(FFT_R, dtype=np.int64)
    th = np.pi * (((2 * f1[:, None] + 1) * s1[None, :]) % (4 * FFT_R)) / (2 * FFT_R)
    ga = _interleave(np.cos(th), -np.sin(th), 0)
    ma = _interleave(np.cos(th).T, -np.sin(th).T, 1) / L
    f2 = np.arange(FFT_R // 2, dtype=np.int64)
    s2 = np.arange(FFT_R, dtype=np.int64)
    idx = ((n2 // FFT_R) * 2 * f2[None, :, None] * s2[None, None, :]
           + (2 * f1[:, None, None] + 1) * s2[None, None, :]) % (2 * n2)
    al = np.pi * idx / n2
    c, s = np.cos(al), np.sin(al)
    nmat = np.concatenate([_interleave(c, s, 2), _interleave(-s, c, 2)], axis=1)
    ct, st = np.transpose(c, (0, 2, 1)), np.transpose(s, (0, 2, 1))
    mmat = _interleave(np.concatenate([ct, -st], axis=2), np.concatenate([st, ct], axis=2), 1)
    bf = lambda a: jnp.asarray(a, F32).astype(BF16)
    return bf(ga), bf(nmat), bf(mmat), bf(ma)


def _hy_mid_kernel(a_ref, n_ref, m_ref, k_ref, e_ref):
    half = FFT_R // 2
    for j in range(FFT_KF):
        t = jnp.dot(n_ref[j], _unpack_pairs(a_ref[j]), preferred_element_type=F32)
        tr, ti = t[:half], t[half:]
        kr, ki = k_ref[j, 0], k_ref[j, 1]
        y = jnp.concatenate([tr * kr - ti * ki, tr * ki + ti * kr], axis=0).astype(BF16)
        e_ref[j] = _pack_pairs(jnp.dot(m_ref[j], y, preferred_element_type=F32))


def _hy_mid(a, nmat, mmat, khat):
    nb = a.shape[0]
    c = a.shape[-1]
    nf1 = 2 * FFT_R
    blk = pl.BlockSpec((None, FFT_KF, FFT_R, c), lambda i, b: (b, i, 0, 0))
    return pl.pallas_call(
        _hy_mid_kernel,
        out_shape=jax.ShapeDtypeStruct(a.shape, jnp.uint32),
        grid=(nf1 // FFT_KF, nb),
        in_specs=[blk,
                  pl.BlockSpec((FFT_KF, FFT_R, 2 * FFT_R), lambda i, b: (i, 0, 0)),
                  pl.BlockSpec((FFT_KF, 2 * FFT_R, FFT_R), lambda i, b: (i, 0, 0)),
                  pl.BlockSpec((FFT_KF, 2, FFT_R // 2, c), lambda i, b: (i, 0, 0, 0))],
        out_specs=blk,
        compiler_params=_cparams(("arbitrary", "arbitrary")),
        name="l0_hyena_mid",
    )(a, nmat, mmat, khat)


def _hy_kfilt_kernel(a_ref, n_ref, nrm_ref, k_ref):
    half = FFT_R // 2
    nrm = nrm_ref[...]
    for j in range(FFT_KF):
        tf = jnp.dot(n_ref[j], _unpack_pairs(a_ref[0, j]), preferred_element_type=F32)
        tb = jnp.dot(n_ref[j], _unpack_pairs(a_ref[1, j]), preferred_element_type=F32)
        k_ref[j, 0] = (tf[:half] + tb[:half]) / nrm
        k_ref[j, 1] = (tf[half:] - tb[half:]) / nrm


def _hy_kfilt(a, nmat, nrm):
    c = a.shape[-1]
    nf1 = 2 * FFT_R
    return pl.pallas_call(
        _hy_kfilt_kernel,
        out_shape=jax.ShapeDtypeStruct((nf1, 2, FFT_R // 2, c), F32),
        grid=(nf1 // FFT_KF,),
        in_specs=[pl.BlockSpec((2, FFT_KF, FFT_R, c), lambda i: (0, i, 0, 0)),
                  pl.BlockSpec((FFT_KF, FFT_R, 2 * FFT_R), lambda i: (i, 0, 0)),
                  _const_spec(nrm.shape)],
        out_specs=pl.BlockSpec((FFT_KF, 2, FFT_R // 2, c), lambda i: (i, 0, 0, 0)),
        compiler_params=_cparams(("arbitrary",)),
        name="l0_hyena_kfilt",
    )(a, nmat, nrm)


def _hyena_long(u, e, x0, hfilt, nrm, nb):
    L = FFT_R * FFT_R
    c = u.shape[-1]
    v4 = lambda a, n: a.reshape(n, FFT_R, FFT_R, c)
    ga, nmat, mmat, ma = _hy2_tables()
    khat = _hy_kfilt(_lead_in(ga, v4(hfilt, 2), "l0_hyena_fwd_a"), nmat, nrm)
    ee = _hy_mid(_lead_in(ga, v4(u, nb), "l0_hyena_fwd_a"), nmat, mmat, khat)
    y = _lead_out(ma, ee, [v4(e, nb), v4(x0, nb)], "l0_hyena_inv_a")
    return y.reshape(nb * L, c)


def _fn2_tables():
    L = FFT_R * FFT_R
    r = np.arange(FFT_R, dtype=np.int64)
    idx = (FFT_R * r[None, :, None] * r[None, None, :] + r[None, :, None] * r[:, None, None]) % L
    gm = 2.0 * np.pi * idx / L
    c, s = np.cos(gm), np.sin(gm)
    g1 = _interleave(np.concatenate([c, -s], axis=2), np.concatenate([-s, -c], axis=2), 1)
    dl = 2.0 * np.pi * ((r[:, None] * r[None, :]) % FFT_R) / FFT_R
    g2 = _interleave(np.cos(dl), np.sin(dl), 1) / math.sqrt(L * FN_GROUP_CH)
    bf = lambda a: jnp.asarray(a, F32).astype(BF16)
    return bf(g1), bf(g2)


def _fnet_s1_kernel(x_ref, m_ref, cs_ref, g1_ref, o_ref, zc_ref, zs_ref):
    xs = jnp.concatenate([x_ref[:, j, :] for j in range(FFT_KF)], axis=0)
    m = m_ref[...]
    h = (_ln_plain(xs) * (1.0 + m[:, D_MODEL:2 * D_MODEL]) + m[:, 0:D_MODEL]).astype(BF16)
    cs = cs_ref[...]
    for g in range(D_MODEL // FN_GROUP_CH):
        a = g * FN_GROUP_CH
        z = jnp.dot(h[:, a:a + FN_GROUP_CH], cs, preferred_element_type=F32)
        zc_ref[:, a:a + FN_GROUP_CH] = z[:, :FN_GROUP_CH].astype(BF16)
        zs_ref[:, a:a + FN_GROUP_CH] = z[:, FN_GROUP_CH:].astype(BF16)
    for j in range(FFT_KF):
        r0 = j * FFT_R
        s = jnp.concatenate([zc_ref[r0:r0 + FFT_R, :], zs_ref[r0:r0 + FFT_R, :]], axis=0)
        o_ref[j] = _pack_pairs(jnp.dot(g1_ref[j], s, preferred_element_type=F32))


def _fnet_long(x, mods, mod_base, nb):
    L = FFT_R * FFT_R
    d = D_MODEL
    g1, g2 = _fn2_tables()
    cs = _group_dft_table()
    bb = pl.pallas_call(
        _fnet_s1_kernel,
        out_shape=jax.ShapeDtypeStruct((nb, FFT_R, FFT_R, d), jnp.uint32),
        grid=(nb, FFT_R // FFT_KF),
        in_specs=[pl.BlockSpec((None, FFT_R, FFT_KF, d), lambda b, k: (b, 0, k, 0)),
                  pl.BlockSpec((None, 1, 6 * d), lambda b, k: (mod_base + b, 0, 0)),
                  pl.BlockSpec(cs.shape, lambda b, k: (0, 0)),
                  pl.BlockSpec((FFT_KF, 2 * FFT_R, 2 * FFT_R), lambda b, k: (k, 0, 0))],
        out_specs=pl.BlockSpec((None, FFT_KF, FFT_R, d), lambda b, k: (b, k, 0, 0)),
        scratch_shapes=[pltpu.VMEM((FFT_KF * FFT_R, d), BF16), pltpu.VMEM((FFT_KF * FFT_R, d), BF16)],
        compiler_params=_cparams(("arbitrary", "arbitrary")),
        name="l1_fnet_stage1",
    )(x.reshape(nb, FFT_R, FFT_R, d), mods, cs, g1)
    y = _lead_out(g2, bb, [], "l1_fnet_stage2")
    return y.reshape(nb * L, d)


def _group_dft_table():
    g = FN_GROUP_CH
    jk = (np.arange(g, dtype=np.int64)[:, None] * np.arange(g, dtype=np.int64)[None, :]) % g
    ang = 2.0 * np.pi * jk / g
    return jnp.asarray(np.concatenate([np.cos(ang), np.sin(ang)], axis=1), F32).astype(BF16)


def _fnet_front_kernel(x_ref, m_ref, cs_ref, zc_ref, zs_ref):
    m = m_ref[...]
    h = (_ln_plain(x_ref[...]) * (1.0 + m[:, D_MODEL:2 * D_MODEL]) + m[:, 0:D_MODEL]).astype(BF16)
    cs = cs_ref[...]
    for g in range(D_MODEL // FN_GROUP_CH):
        a = g * FN_GROUP_CH
        z = jnp.dot(h[:, a:a + FN_GROUP_CH], cs, preferred_element_type=F32)
        zc_ref[:, a:a + FN_GROUP_CH] = z[:, :FN_GROUP_CH].astype(BF16)
        zs_ref[:, a:a + FN_GROUP_CH] = z[:, FN_GROUP_CH:].astype(BF16)


def _fnet_front(x, mods, mod_base, tiles_per_mod):
    t = x.shape[0]
    tm = ROW_TILE
    cs = _group_dft_table()
    return pl.pallas_call(
        _fnet_front_kernel,
        out_shape=(jax.ShapeDtypeStruct((t, D_MODEL), BF16),) * 2,
        grid=(t // tm,),
        in_specs=[pl.BlockSpec((tm, D_MODEL), lambda i: (i, 0)),
                  _mod_spec(mod_base, tiles_per_mod),
                  _const_spec(cs.shape)],
        out_specs=(pl.BlockSpec((tm, D_MODEL), lambda i: (i, 0)),) * 2,
        compiler_params=_cparams(("arbitrary",)),
        name="l1_fnet_front",
    )(x, mods, cs)


def _post_kernel(n_a, *refs):
    x_ref, m_ref = refs[0], refs[1]
    a_refs = refs[2:2 + n_a]
    wo_refs = refs[2 + n_a:2 + 2 * n_a]
    g1_ref, b1_ref, w1_ref, w2_ref, g2_ref, b2_ref, o_ref = refs[2 + 2 * n_a:]
    m = m_ref[...]
    d = D_MODEL
    out = _bdot(a_refs[0][...], wo_refs[0][...])
    for a_ref, wo_ref in zip(a_refs[1:], wo_refs[1:]):
        out += _bdot(a_ref[...], wo_ref[...])
    x1 = _ln_plain(ALPHA * x_ref[...] + m[:, 2 * d:3 * d] * out) * g1_ref[...] + b1_ref[...]
    h = (_ln_plain(x1) * (1.0 + m[:, 4 * d:5 * d]) + m[:, 3 * d:4 * d]).astype(BF16)
    acc = jnp.zeros_like(x1)
    for c in range(D_FF // d):
        hc = jnp.maximum(jnp.dot(h, w1_ref[:, c * d:(c + 1) * d], preferred_element_type=F32), 0.0)
        acc += jnp.dot((hc * hc).astype(BF16), w2_ref[c * d:(c + 1) * d, :], preferred_element_type=F32)
    o_ref[...] = _ln_plain(ALPHA * x1 + m[:, 5 * d:6 * d] * acc) * g2_ref[...] + b2_ref[...]


def _post(x, mods, mod_base, tiles_per_mod, a_list, wo_list, g1, b1, w1, w2, g2, b2):
    t = x.shape[0]
    tm = ROW_TILE
    row = lambda c: pl.BlockSpec((tm, c), lambda i: (i, 0))
    in_specs = ([row(D_MODEL), _mod_spec(mod_base, tiles_per_mod)]
                + [row(a.shape[1]) for a in a_list]
                + [_const_spec(w.shape) for w in wo_list]
                + [_const_spec(v.shape) for v in (g1, b1, w1, w2, g2, b2)])
    return pl.pallas_call(
        functools.partial(_post_kernel, len(a_list)),
        out_shape=jax.ShapeDtypeStruct((t, D_MODEL), F32),
        grid=(t // tm,),
        in_specs=in_specs,
        out_specs=row(D_MODEL),
        compiler_params=_cparams(("arbitrary",)),
        name="post_mlp",
    )(x, mods, *a_list, *wo_list, g1, b1, w1, w2, g2, b2)


def _rot_cols(w):
    parts = []
    for seg in range(2):
        o = seg * 32
        parts += [-w[:, o + 16:o + 32], w[:, o:o + 16]]
    return jnp.concatenate(parts, axis=1)


def _pad_cols(w, n):
    return jnp.pad(w, ((0, 0), (0, n - w.shape[1])))


def _rope_tables(L):
    rows = L // GRID_W
    row = np.repeat(np.arange(rows, dtype=np.float64), GRID_W)
    col = np.tile(np.arange(GRID_W, dtype=np.float64), rows)
    half = QK_ROPE // 2
    inv = 1.0 / (ROPE_THETA ** (np.arange(0, half, 2, dtype=np.float64) / half))
    ar = row[:, None] * inv[None, :]
    ac = col[:, None] * inv[None, :]
    ang = np.concatenate([ar, ar, ac, ac], axis=1)
    cos = np.concatenate([np.cos(ang), np.ones_like(ang)], axis=1)
    sin = np.concatenate([np.sin(ang), np.zeros_like(ang)], axis=1)
    return jnp.asarray(cos, F32), jnp.asarray(sin, F32)


def kernel(x_prompt, x_sample, cache_l0_ckv, cache_l0_krope, c, c_ctx, l0_ada_w, l0_ada_b, l0_w_in, l0_conv_w, l0_conv_b, l0_hf_w1, l0_hf_b1, l0_hf_freq, l0_hf_w2, l0_hf_b2, l0_hf_w3, l0_hf_skip, l0_q_norm, l0_q_up, l0_kv_norm, l0_kv_up, l0_w_out, l0_ln1_g, l0_ln1_b, l0_mlp_w1, l0_mlp_w2, l0_ln2_g, l0_ln2_b, l1_ada_w, l1_ada_b, l1_w_out, l1_ln1_g, l1_ln1_b, l1_mlp_w1, l1_mlp_w2, l1_ln2_g, l1_ln2_b):
    nbc, lc, d = x_prompt.shape
    nbs, ls, _ = x_sample.shape
    past = cache_l0_ckv.shape[1]
    tm = ROW_TILE
    row1 = lambda v: v.reshape(1, -1)

    cond8 = jnp.concatenate([c_ctx[None, :], c, jnp.zeros((8 - 1 - nbs, d), F32)], axis=0)
    mods0 = _modulation(cond8, l0_ada_w, l0_ada_b)
    mods1 = _modulation(cond8, l1_ada_w, l1_ada_b)

    kpe_w = l0_w_in[:, 1920:1984]
    win = jnp.concatenate([l0_w_in[:, :1920], _pad_cols(kpe_w, LANE), _pad_cols(_rot_cols(kpe_w), LANE)],
                          axis=1).astype(BF16)
    dh = QK_NOPE + QK_ROPE
    q_nope = [l0_q_up[:, h * dh:h * dh + QK_NOPE] for h in range(MLA_HEADS)]
    q_pe = [l0_q_up[:, h * dh + QK_NOPE:(h + 1) * dh] for h in range(MLA_HEADS)]
    qup = jnp.concatenate(q_nope + [_pad_cols(w, LANE) for w in q_pe]
                          + [_pad_cols(_rot_cols(w), LANE) for w in q_pe], axis=1).astype(BF16)
    kvup = l0_kv_up.astype(BF16)
    front_w = (win, row1(l0_q_norm), qup, row1(l0_kv_norm), kvup)
    w1p = jnp.pad(l0_hf_w1, ((0, LANE - l0_hf_w1.shape[0]), (0, 0)))
    filt_w = (w1p, row1(l0_hf_b1), row1(l0_hf_freq), l0_hf_w2, row1(l0_hf_b2), l0_hf_w3.astype(BF16))
    wo0 = l0_w_out.astype(BF16)
    conv_b = row1(l0_conv_b)
    skip = row1(l0_hf_skip)

    xc = x_prompt.reshape(nbc * lc, d)
    xs = x_sample.reshape(nbs * ls, d)
    groups = (
        dict(x=xc, nb=nbc, L=lc, mod_base=0, tiles_per_mod=nbc * lc // tm, dft_nb=4, tq=lc),
        dict(x=xs, nb=nbs, L=ls, mod_base=1, tiles_per_mod=ls // tm, dft_nb=nbs, tq=512),
    )
    ones_tab = (jnp.concatenate([jnp.ones((tm, LANE), F32)], axis=0), jnp.zeros((tm, LANE), F32))

    outs = []
    ctx_ckv = ctx_krope = None
    for gi, g in enumerate(groups):
        nb, L = g["nb"], g["L"]
        tiles_per_seq = L // tm
        latent = gi == 1
        cos, sin = _rope_tables(L) if latent else ones_tab
        hy, q, k, vt, kvn, kpe = _front(g["x"], mods0, g["mod_base"], g["tiles_per_mod"], front_w,
                                       cos, sin, tiles_per_seq if latent else 1)
        if latent:
            extra = _cache_kv(cache_l0_ckv.reshape(nbs * past, KV_LORA),
                              _pad_cols(cache_l0_krope.reshape(nbs * past, QK_ROPE), LANE), kvup)
        else:
            extra = None
            ctx_ckv = kvn.reshape(nb, L, KV_LORA)
            ctx_krope = kpe.reshape(nb, L, QK_ROPE)
        y_mla = _attention(q, k, vt, extra, nb, L, g["tq"])

        two_stage = L == FFT_R * FFT_R
        io_dtype = F32 if two_stage else BF16
        u, e, x0 = _conv_gate(hy, l0_conv_w, conv_b, skip, tiles_per_seq, io_dtype)
        hfilt, hnorm = _filters(L, io_dtype, *filt_w)
        if two_stage:
            y_hy = _hyena_long(u, e, x0, hfilt, hnorm, nb)
        else:
            kre, kim = _dft("filt", "hy_fwd", [hfilt], [hnorm], 2)
            sh = (nb, L, HY_CH)
            yre, yim = _dft("fwdk", "hy_fwd", [u.reshape(sh)], [kre, kim], g["dft_nb"])
            y_hy = _dft("inv", "hy_inv", [yre, yim], [e.reshape(sh), x0.reshape(sh)], g["dft_nb"])
            y_hy = y_hy.reshape(nb * L, HY_CH)

        x1 = _post(g["x"], mods0, g["mod_base"], g["tiles_per_mod"], [y_hy, y_mla],
                   [wo0[:HY_CH], wo0[HY_CH:]], row1(l0_ln1_g), row1(l0_ln1_b),
                   l0_mlp_w1.astype(BF16), l0_mlp_w2.astype(BF16), row1(l0_ln2_g), row1(l0_ln2_b))

        if L == FFT_R * FFT_R:
            yf = _fnet_long(x1, mods1, g["mod_base"], nb)
        else:
            zc, zs = _fnet_front(x1, mods1, g["mod_base"], g["tiles_per_mod"])
            sh = (nb, L, d)
            yf = _dft("fnet", "fnet", [zc.reshape(sh), zs.reshape(sh)], [], min(g["dft_nb"], 2))
            yf = yf.reshape(nb * L, d)
        x2 = _post(x1, mods1, g["mod_base"], g["tiles_per_mod"], [yf],
                   [l1_w_out.astype(BF16)], row1(l1_ln1_g), row1(l1_ln1_b),
                   l1_mlp_w1.astype(BF16), l1_mlp_w2.astype(BF16), row1(l1_ln2_g), row1(l1_ln2_b))
        outs.append(x2.reshape(nb, L, d))

    return (outs[0], outs[1], ctx_ckv, ctx_krope)
```

```python
import functools
import math

import numpy as np
import jax
import jax.numpy as jnp
from jax import lax
from jax.experimental import pallas as pl
from jax.experimental.pallas import tpu as pltpu

F32 = jnp.float32
BF16 = jnp.bfloat16
HI = lax.Precision.HIGHEST

D_MODEL = 1024
DEPTH = 2
GRID_W = 64
HY_CH = 512
FILT_BANDS = 16
FILT_ORDER = 64
FAST_DECAY_PCT = 0.3
SLOW_DECAY_PCT = 1.5
DECAY_TARGET = 1e-2
MAX_DECAY = math.log(DECAY_TARGET) / FAST_DECAY_PCT
MIN_DECAY = math.log(DECAY_TARGET) / SLOW_DECAY_PCT
MLA_HEADS = 4
QK_NOPE = 128
QK_ROPE = 64
V_HEAD = 128
Q_LORA = 256
KV_LORA = 128
ROPE_THETA = 10000.0
FN_GROUP_CH = 128
D_FF = 4096
ALPHA = (2 * DEPTH) ** 0.25
LN_EPS = 1e-5
RMS_EPS = 1e-6

LANE = 128
ROW_TILE = 256
POST_TILE = 512
QK_PAD = 256
VMEM_LIMIT = 56 * 1024 * 1024


def _cparams(sem):
    return pltpu.CompilerParams(dimension_semantics=sem, vmem_limit_bytes=VMEM_LIMIT)


def _ln_plain(x):
    mu = jnp.mean(x, axis=-1, keepdims=True)
    xc = x - mu
    var = jnp.mean(xc * xc, axis=-1, keepdims=True)
    return xc * lax.rsqrt(var + LN_EPS)


def _rms(x, g):
    return x * lax.rsqrt(jnp.mean(x * x, axis=-1, keepdims=True) + RMS_EPS) * g


def _bdot(a, b):
    return jnp.dot(a.astype(BF16), b, preferred_element_type=F32)


def _mod_kernel(c_ref, w_ref, b_ref, o_ref):
    c = c_ref[...]
    s = c / (1.0 + jnp.exp(-c))
    o_ref[...] = jnp.dot(s, w_ref[...], precision=HI, preferred_element_type=F32) + b_ref[...]


def _modulation(cond8, w, b):
    n = w.shape[1]
    tn = 1536
    out = pl.pallas_call(
        _mod_kernel,
        out_shape=jax.ShapeDtypeStruct((8, n), F32),
        grid=(n // tn,),
        in_specs=[pl.BlockSpec((8, D_MODEL), lambda j: (0, 0)),
                  pl.BlockSpec((D_MODEL, tn), lambda j: (0, j)),
                  pl.BlockSpec((1, tn), lambda j: (0, j))],
        out_specs=pl.BlockSpec((8, tn), lambda j: (0, j)),
        compiler_params=_cparams(("arbitrary",)),
        name="modulation",
    )(cond8, w, b.reshape(1, n))
    return out.reshape(8, 1, n)


def _mod_spec(mod_base, tiles_per_mod):
    return pl.BlockSpec((None, 1, 6 * D_MODEL), lambda i: (mod_base + i // tiles_per_mod, 0, 0))


def _const_spec(shape):
    nd = len(shape)
    return pl.BlockSpec(shape, lambda i: (0,) * nd)


def _front_kernel(x_ref, m_ref, win_ref, qn_ref, qup_ref, kvn_ref, kvup_ref, cos_ref, sin_ref,
                  hy_ref, q_ref, k_ref, v_ref, kvn_out_ref, kpe_ref):
    m = m_ref[...]
    h = _ln_plain(x_ref[...]) * (1.0 + m[:, D_MODEL:2 * D_MODEL]) + m[:, 0:D_MODEL]
    z = _bdot(h, win_ref[...])
    hy_ref[...] = z[:, :3 * HY_CH]
    q_c = z[:, 1536:1792]
    kv_c = z[:, 1792:1920]
    cos = cos_ref[...]
    sin = sin_ref[...]
    kpe = z[:, 1920:2048] * cos + z[:, 2048:2176] * sin
    kpe_ref[...] = kpe[:, :QK_ROPE]
    kpe_b = kpe.astype(BF16)
    q = _bdot(_rms(q_c, qn_ref[...]), qup_ref[...]) * (1.0 / math.sqrt(QK_NOPE + QK_ROPE))
    kvn = _rms(kv_c, kvn_ref[...])
    kvn_out_ref[...] = kvn
    kv = _bdot(kvn, kvup_ref[...])
    for hd in range(MLA_HEADS):
        a = hd * LANE
        q_pe = (q[:, 512 + a:512 + a + LANE] * cos + q[:, 1024 + a:1024 + a + LANE] * sin).astype(BF16)
        q_ref[hd] = jnp.concatenate([q[:, a:a + LANE].astype(BF16), q_pe], axis=-1)
        k_ref[hd] = jnp.concatenate([kv[:, 2 * a:2 * a + LANE].astype(BF16), kpe_b], axis=-1)
        v_ref[hd] = jnp.transpose(kv[:, 2 * a + LANE:2 * a + 2 * LANE]).astype(BF16)


def _front(x, mods, mod_base, tiles_per_mod, w, cos, sin, tiles_per_seq):
    t = x.shape[0]
    tm = ROW_TILE
    win, qn, qup, kvn, kvup = w
    if tiles_per_seq == 1:
        tab_spec = pl.BlockSpec((tm, LANE), lambda i: (0, 0))
    else:
        tab_spec = pl.BlockSpec((tm, LANE), lambda i: (i % tiles_per_seq, 0))
    return pl.pallas_call(
        _front_kernel,
        out_shape=(jax.ShapeDtypeStruct((t, 3 * HY_CH), F32),
                   jax.ShapeDtypeStruct((MLA_HEADS, t, QK_PAD), BF16),
                   jax.ShapeDtypeStruct((MLA_HEADS, t, QK_PAD), BF16),
                   jax.ShapeDtypeStruct((MLA_HEADS, V_HEAD, t), BF16),
                   jax.ShapeDtypeStruct((t, KV_LORA), F32),
                   jax.ShapeDtypeStruct((t, QK_ROPE), F32)),
        grid=(t // tm,),
        in_specs=[pl.BlockSpec((tm, D_MODEL), lambda i: (i, 0)),
                  _mod_spec(mod_base, tiles_per_mod),
                  _const_spec(win.shape), _const_spec(qn.shape), _const_spec(qup.shape),
                  _const_spec(kvn.shape), _const_spec(kvup.shape),
                  tab_spec, tab_spec],
        out_specs=(pl.BlockSpec((tm, 3 * HY_CH), lambda i: (i, 0)),
                   pl.BlockSpec((MLA_HEADS, tm, QK_PAD), lambda i: (0, i, 0)),
                   pl.BlockSpec((MLA_HEADS, tm, QK_PAD), lambda i: (0, i, 0)),
                   pl.BlockSpec((MLA_HEADS, V_HEAD, tm), lambda i: (0, 0, i)),
                   pl.BlockSpec((tm, KV_LORA), lambda i: (i, 0)),
                   pl.BlockSpec((tm, QK_ROPE), lambda i: (i, 0))),
        compiler_params=_cparams(("arbitrary",)),
        name="l0_front",
    )(x, mods, win, qn, qup, kvn, kvup, cos, sin)


def _cache_kv_kernel(ckv_ref, kr_ref, kvup_ref, k_ref, v_ref):
    kv = _bdot(ckv_ref[...], kvup_ref[...])
    kr = kr_ref[...].astype(BF16)
    for hd in range(MLA_HEADS):
        a = 2 * hd * LANE
        k_ref[hd] = jnp.concatenate([kv[:, a:a + LANE].astype(BF16), kr], axis=-1)
        v_ref[hd] = jnp.transpose(kv[:, a + LANE:a + 2 * LANE]).astype(BF16)


def _cache_kv(ckv, krope_pad, kvup):
    t = ckv.shape[0]
    return pl.pallas_call(
        _cache_kv_kernel,
        out_shape=(jax.ShapeDtypeStruct((MLA_HEADS, t, QK_PAD), BF16),
                   jax.ShapeDtypeStruct((MLA_HEADS, V_HEAD, t), BF16)),
        name="l0_cache_kv",
    )(ckv, krope_pad, kvup)


def _col_reduce(x, op):
    rows, n = x.shape
    for g in (32, 8):
        if rows % (8 * g) == 0 and rows > 8 * g:
            x = op(x.reshape(rows // (8 * g), 8 * g, n), axis=0)
            rows = 8 * g
    return op(x, axis=0, keepdims=True)


def _attn_kernel(n_kv, q_ref, *refs):
    k_refs, vt_refs = refs[:n_kv], refs[n_kv:2 * n_kv]
    o_ref, s_even, s_odd = refs[2 * n_kv:]
    i = pl.program_id(0)

    @pl.when(i == 0)
    def _():
        s_odd[...] = jnp.zeros_like(s_odd)

    def step(s_write, s_read):
        q = q_ref[...]
        nt = (((1,), (1,)), ((), ()))
        r0 = 0
        for k_ref in k_refs:
            lk = k_ref.shape[0]
            s_write[r0:r0 + lk, :] = lax.dot_general(k_ref[...], q, nt, preferred_element_type=F32)
            r0 += lk
        s = s_read[...]
        m = _col_reduce(s, jnp.max)
        p = jnp.exp(s - m)
        l = _col_reduce(p, jnp.sum)
        pb = p.astype(BF16)
        acc = None
        r0 = 0
        for vt_ref in vt_refs:
            lk = vt_ref.shape[1]
            pv = jnp.dot(vt_ref[...], pb[r0:r0 + lk, :], preferred_element_type=F32)
            acc = pv if acc is None else acc + pv
            r0 += lk
        o_ref[...] = jnp.transpose(acc / l).astype(o_ref.dtype)

    pl.when(i % 2 == 0)(lambda: step(s_even, s_odd))
    pl.when(i % 2 == 1)(lambda: step(s_odd, s_even))


def _attention(q, k, vt, extra, nb, lq, tq):
    nq = lq // tq
    n_tiles = nb * MLA_HEADS * nq

    def where(t):
        bh = t // nq
        return bh // MLA_HEADS, bh % MLA_HEADS, t % nq

    def score_side(fn):
        return lambda i: fn(*where(jnp.minimum(i, n_tiles - 1)))

    def value_side(fn):
        return lambda i: fn(*where(jnp.maximum(i - 1, 0)))

    ks, vts = [k], [vt]
    if extra is not None:
        ks.append(extra[0])
        vts.append(extra[1])
    in_specs = [pl.BlockSpec((None, tq, QK_PAD), score_side(lambda b, h, j: (h, b * nq + j, 0)))]
    in_specs += [pl.BlockSpec((None, a.shape[1] // nb, QK_PAD), score_side(lambda b, h, j: (h, b, 0))) for a in ks]
    in_specs += [pl.BlockSpec((None, V_HEAD, a.shape[2] // nb), value_side(lambda b, h, j: (h, 0, b))) for a in vts]
    lk_total = sum(a.shape[1] // nb for a in ks)
    return pl.pallas_call(
        functools.partial(_attn_kernel, len(ks)),
        out_shape=jax.ShapeDtypeStruct((nb * lq, MLA_HEADS * V_HEAD), BF16),
        grid=(n_tiles + 1,),
        in_specs=in_specs,
        out_specs=pl.BlockSpec((tq, V_HEAD), value_side(lambda b, h, j: (b * nq + j, h))),
        scratch_shapes=[pltpu.VMEM((lk_total, tq), F32), pltpu.VMEM((lk_total, tq), F32)],
        compiler_params=_cparams(("arbitrary",)),
        name="l0_attention",
    )(q, *ks, *vts)


def _conv_gate_kernel(tiles_per_seq, hy_ref, prev_ref, next_ref, w_ref, b_ref, skip_ref,
                      u_ref, e_ref, x0_ref):
    i = pl.program_id(0)
    x = hy_ref[...]
    tm = x.shape[0]
    pos = i % tiles_per_seq
    prev_row = jnp.where(pos == 0, 0.0, prev_ref[7:8, :])
    next_row = jnp.where(pos == tiles_per_seq - 1, 0.0, next_ref[0:1, :])
    rows = lax.broadcasted_iota(jnp.int32, x.shape, 0)
    xm1 = jnp.where(rows == 0, prev_row, pltpu.roll(x, 1, 0))
    xp1 = jnp.where(rows == tm - 1, next_row, pltpu.roll(x, tm - 1, 0))
    w = w_ref[...]
    p = xm1 * w[0:1, :] + x * w[1:2, :] + xp1 * w[2:3, :] + b_ref[...]
    u = p[:, 2 * HY_CH:] * p[:, HY_CH:2 * HY_CH]
    u_ref[...] = u.astype(u_ref.dtype)
    e_ref[...] = u * skip_ref[...]
    x0_ref[...] = p[:, :HY_CH]


def _conv_gate(hy, conv_w, conv_b, skip, tiles_per_seq, u_dtype):
    t = hy.shape[0]
    tm = ROW_TILE
    r8 = tm // 8
    n8 = t // 8
    return pl.pallas_call(
        functools.partial(_conv_gate_kernel, tiles_per_seq),
        out_shape=(jax.ShapeDtypeStruct((t, HY_CH), u_dtype),
                   jax.ShapeDtypeStruct((t, HY_CH), F32),
                   jax.ShapeDtypeStruct((t, HY_CH), F32)),
        grid=(t // tm,),
        in_specs=[pl.BlockSpec((tm, 3 * HY_CH), lambda i: (i, 0)),
                  pl.BlockSpec((8, 3 * HY_CH), lambda i: (jnp.maximum(i * r8 - 1, 0), 0)),
                  pl.BlockSpec((8, 3 * HY_CH), lambda i: (jnp.minimum((i + 1) * r8, n8 - 1), 0)),
                  _const_spec(conv_w.shape), _const_spec(conv_b.shape), _const_spec(skip.shape)],
        out_specs=(pl.BlockSpec((tm, HY_CH), lambda i: (i, 0)),
                   pl.BlockSpec((tm, HY_CH), lambda i: (i, 0)),
                   pl.BlockSpec((tm, HY_CH), lambda i: (i, 0))),
        compiler_params=_cparams(("arbitrary",)),
        name="l0_conv_gate",
    )(hy, hy, hy, conv_w, conv_b, skip)


def _filter_kernel(z_ref, w1_ref, b1_ref, fr_ref, w2_ref, b2_ref, w3_ref, dl_ref, h_ref, norm_ref):
    i = pl.program_id(0)
    z = z_ref[...]
    tl = z.shape[0]
    fr = fr_ref[...]
    h = jnp.sin(fr * (jnp.dot(z, w1_ref[...], precision=HI, preferred_element_type=F32) + b1_ref[...]))
    h = jnp.sin(fr * (jnp.dot(h, w2_ref[...], precision=HI, preferred_element_type=F32) + b2_ref[...]))
    h = _bdot(h, w3_ref[...])
    decay = jnp.exp(-(z[:, 0:1] * dl_ref[...]))
    hf = h[:, :HY_CH] * decay
    hb = h[:, HY_CH:] * decay
    part = jnp.sum(jnp.abs(hf) + jnp.abs(hb), axis=0, keepdims=True)

    @pl.when(i == 0)
    def _():
        norm_ref[...] = part

    @pl.when(i > 0)
    def _():
        norm_ref[...] += part

    rows = lax.broadcasted_iota(jnp.int32, hb.shape, 0) + i * tl
    h_ref[0] = hf.astype(h_ref.dtype)
    h_ref[1] = jnp.where(rows == 0, 0.0, hb).astype(h_ref.dtype)


def _filter_embedding(L):
    t = np.linspace(0.0, 1.0, L)[:, None]
    w_ang = 2.0 * np.pi * np.arange(L) / L
    bands = np.linspace(1e-4, FILT_BANDS - 1, FILT_BANDS)
    ang = w_ang[:, None] * bands[None, :]
    z = np.zeros((L, LANE), np.float64)
    z[:, 0:1] = t
    z[:, 1:1 + FILT_BANDS] = np.cos(ang)
    z[:, 1 + FILT_BANDS:1 + 2 * FILT_BANDS] = -np.sin(ang)
    return jnp.asarray(z, F32)


def _filters(L, h_dtype, w1p, b1, fr, w2, b2, w3):
    tl = min(L, 512)
    z = _filter_embedding(L)
    deltas = jnp.asarray(np.abs(np.linspace(MIN_DECAY, MAX_DECAY, HY_CH))[None, :], F32)
    return pl.pallas_call(
        _filter_kernel,
        out_shape=(jax.ShapeDtypeStruct((2, L, HY_CH), h_dtype), jax.ShapeDtypeStruct((1, HY_CH), F32)),
        grid=(L // tl,),
        in_specs=[pl.BlockSpec((tl, LANE), lambda i: (i, 0)),
                  _const_spec(w1p.shape), _const_spec(b1.shape), _const_spec(fr.shape),
                  _const_spec(w2.shape), _const_spec(b2.shape), _const_spec(w3.shape),
                  _const_spec(deltas.shape)],
        out_specs=(pl.BlockSpec((2, tl, HY_CH), lambda i: (0, i, 0)),
                   pl.BlockSpec((1, HY_CH), lambda i: (0, 0))),
        compiler_params=_cparams(("arbitrary",)),
        name="l0_hyena_filters",
    )(z, w1p, b1, fr, w2, b2, w3, deltas)


def _dft_tables(kind, L, ti):
    ni = L // ti
    i = np.arange(ti, dtype=np.int64)[:, None]
    big = (np.arange(ni, dtype=np.int64) * ti)[:, None]
    c = np.arange(L, dtype=np.int64)[None, :]
    if kind == "hy_fwd":
        period = 4 * L
        base_idx = (2 * i + 1) * c
        r_idx = 2 * big * c
        scale = 1.0
    elif kind == "hy_inv":
        period = 4 * L
        base_idx = (2 * c + 1) * i
        r_idx = (2 * c + 1) * big
        scale = 1.0 / L
    else:
        period = L
        base_idx = i * c
        r_idx = big * c
        scale = 1.0 / math.sqrt(L * FN_GROUP_CH)
    ab = 2.0 * np.pi * (base_idx % period) / period
    ar = 2.0 * np.pi * (r_idx % period) / period
    return (jnp.asarray(np.cos(ab), F32), jnp.asarray(np.sin(ab), F32),
            jnp.asarray(scale * np.cos(ar), F32).reshape(ni, 1, L),
            jnp.asarray(scale * np.sin(ar), F32).reshape(ni, 1, L))


def _dft_kernel(mode, nb, n_x, *refs):
    bc_ref, bs_ref, rc_ref, rs_ref = refs[:4]
    x_refs = refs[4:4 + n_x]
    rest = refs[4 + n_x:]
    p_ref, q_ref = rest[-2], rest[-1]
    j = pl.program_id(2)
    nj = pl.num_programs(2)
    tj = x_refs[0].shape[1]
    if bc_ref.shape[1] == tj:
        bc, bs, rc, rs = bc_ref[...], bs_ref[...], rc_ref[...], rs_ref[...]
    else:
        off = pl.multiple_of(j * tj, tj)
        bc, bs = bc_ref[:, pl.ds(off, tj)], bs_ref[:, pl.ds(off, tj)]
        rc, rs = rc_ref[:, pl.ds(off, tj)], rs_ref[:, pl.ds(off, tj)]
    tc = (bc * rc - bs * rs).astype(BF16)
    ts = (bs * rc + bc * rs).astype(BF16)
    x1_ref = x_refs[0]
    x2_ref = x_refs[-1]

    pq = [(jnp.dot(tc, x1_ref[b], preferred_element_type=F32),
           jnp.dot(ts, x2_ref[b], preferred_element_type=F32)) for b in range(nb)]

    @pl.when(j == 0)
    def _():
        for b in range(nb):
            p_ref[b] = pq[b][0]
            q_ref[b] = pq[b][1]

    @pl.when(j > 0)
    def _():
        for b in range(nb):
            p_ref[b] += pq[b][0]
            q_ref[b] += pq[b][1]

    @pl.when(j == nj - 1)
    def _():
        if mode == "filt":
            nrm = rest[0][...]
            kre_ref, kim_ref = rest[1], rest[2]
            kre_ref[...] = (p_ref[0] + p_ref[1]) / nrm
            kim_ref[...] = (q_ref[1] - q_ref[0]) / nrm
        elif mode == "fwdk":
            kre, kim = rest[0][...], rest[1][...]
            yre_ref, yim_ref = rest[2], rest[3]
            for b in range(nb):
                pp, qq = p_ref[b], q_ref[b]
                yre_ref[b] = (pp * kre + qq * kim).astype(BF16)
                yim_ref[b] = (pp * kim - qq * kre).astype(BF16)
        elif mode == "inv":
            e_ref, x0_ref, o_ref = rest[0], rest[1], rest[2]
            for b in range(nb):
                o_ref[b] = ((p_ref[b] - q_ref[b] + e_ref[b]) * x0_ref[b]).astype(BF16)
        else:
            o_ref = rest[0]
            for b in range(nb):
                o_ref[b] = (p_ref[b] - q_ref[b]).astype(BF16)


def _dft(mode, kind, xs, extras, nb):
    B, L, C = xs[0].shape
    ti = min(L, 256)
    tj = min(L, 512)
    bc, bs, rc, rs = _dft_tables(kind, L, ti)
    grid = (B // nb, L // ti, L // tj)
    x_spec = pl.BlockSpec((nb, tj, C), lambda g, i, j: (g, j, 0))
    row_spec = lambda c, dt=None: pl.BlockSpec((nb, ti, c), lambda g, i, j: (g, i, 0))
    in_specs = [pl.BlockSpec((ti, L), lambda g, i, j: (0, 0)),
                pl.BlockSpec((ti, L), lambda g, i, j: (0, 0)),
                pl.BlockSpec((None, 1, L), lambda g, i, j: (i, 0, 0)),
                pl.BlockSpec((None, 1, L), lambda g, i, j: (i, 0, 0))] + [x_spec] * len(xs)
    if mode == "filt":
        in_specs += [pl.BlockSpec((1, HY_CH), lambda g, i, j: (0, 0))]
        out_shape = (jax.ShapeDtypeStruct((L, HY_CH), F32),) * 2
        out_specs = (pl.BlockSpec((ti, HY_CH), lambda g, i, j: (i, 0)),) * 2
    elif mode == "fwdk":
        in_specs += [pl.BlockSpec((ti, HY_CH), lambda g, i, j: (i, 0))] * 2
        out_shape = (jax.ShapeDtypeStruct((B, L, C), BF16),) * 2
        out_specs = (row_spec(C),) * 2
    elif mode == "inv":
        in_specs += [row_spec(C)] * 2
        out_shape = jax.ShapeDtypeStruct((B, L, C), BF16)
        out_specs = row_spec(C)
    else:
        out_shape = jax.ShapeDtypeStruct((B, L, C), BF16)
        out_specs = row_spec(C)
    return pl.pallas_call(
        functools.partial(_dft_kernel, mode, nb, len(xs)),
        out_shape=out_shape,
        grid=grid,
        in_specs=in_specs,
        out_specs=out_specs,
        scratch_shapes=[pltpu.VMEM((nb, ti, C), F32), pltpu.VMEM((nb, ti, C), F32)],
        compiler_params=_cparams(("arbitrary", "arbitrary", "arbitrary")),
        name="dft_" + mode,
    )(bc, bs, rc, rs, *xs, *extras)


FFT_R = 64
FFT_KF = 8


def _pack_pairs(x):
    return pltpu.bitcast(x.astype(BF16), jnp.uint32)


def _unpack_pairs(w):
    return pltpu.bitcast(w, BF16)


def _lead_in_kernel(g_ref, x_ref, o_ref):
    g = g_ref[...]
    xt = pltpu.einshape("sjc->jsc", x_ref[...])
    ys = [_pack_pairs(jnp.dot(g, xt[j].astype(BF16), preferred_element_type=F32)) for j in range(FFT_KF)]
    o_ref[...] = pltpu.einshape("jfc->fjc", jnp.stack(ys, axis=0))


def _lead_in(g, x, name):
    nbx, _, _, c = x.shape
    m2 = g.shape[0] // 2
    return pl.pallas_call(
        _lead_in_kernel,
        out_shape=jax.ShapeDtypeStruct((nbx, m2, FFT_R, c), jnp.uint32),
        grid=(nbx, FFT_R // FFT_KF),
        in_specs=[pl.BlockSpec(g.shape, lambda b, k: (0, 0)),
                  pl.BlockSpec((None, FFT_R, FFT_KF, c), lambda b, k: (b, 0, k, 0))],
        out_specs=pl.BlockSpec((None, m2, FFT_KF, c), lambda b, k: (b, 0, k, 0)),
        compiler_params=_cparams(("arbitrary", "arbitrary")),
        name=name,
    )(g, x)


def _lead_out_kernel(n_extra, g_ref, w_ref, *rest):
    g = g_ref[...]
    o_ref = rest[-1]
    wt = pltpu.einshape("kjc->jkc", w_ref[...])
    ys = [jnp.dot(g, _unpack_pairs(wt[j]), preferred_element_type=F32) for j in range(FFT_KF)]
    y = pltpu.einshape("jtc->tjc", jnp.stack(ys, axis=0))
    if n_extra:
        y = (y + rest[0][...]) * rest[1][...]
    o_ref[...] = y


def _lead_out(g, w, extras, name):
    nb, k2, _, c = w.shape
    blk = lambda r: pl.BlockSpec((None, r, FFT_KF, c), lambda b, k: (b, 0, k, 0))
    return pl.pallas_call(
        functools.partial(_lead_out_kernel, len(extras)),
        out_shape=jax.ShapeDtypeStruct((nb, FFT_R, FFT_R, c), F32),
        grid=(nb, FFT_R // FFT_KF),
        in_specs=[pl.BlockSpec(g.shape, lambda b, k: (0, 0)), blk(k2)] + [blk(FFT_R)] * len(extras),
        out_specs=blk(FFT_R),
        compiler_params=_cparams(("arbitrary", "arbitrary")),
        name=name,
    )(g, w, *extras)


def _interleave(a, b, axis):
    st = np.stack([a, b], axis=axis + 1)
    shape = list(a.shape)
    shape[axis] *= 2
    return st.reshape(shape)


def _hy2_tables():
    L = FFT_R * FFT_R
    n2 = 2 * L
    f1 = np.arange(2 * FFT_R, dtype=np.int64)
    s1 = np.arange(FFT_R, dtype=np.int64)
    th = np.pi * (((2 * f1[:, None] + 1) * s1[None, :]) % (4 * FFT_R)) / (2 * FFT_R)
    ga = _interleave(np.cos(th), -np.sin(th), 0)
    ma = _interleave(np.cos(th).T, -np.sin(th).T, 1) / L
    f2 = np.arange(FFT_R // 2, dtype=np.int64)
    s2 = np.arange(FFT_R, dtype=np.int64)
    idx = ((n2 // FFT_R) * 2 * f2[None, :, None] * s2[None, None, :]
           + (2 * f1[:, None, None] + 1) * s2[None, None, :]) % (2 * n2)
    al = np.pi * idx / n2
    c, s = np.cos(al), np.sin(al)
    nmat = np.concatenate([_interleave(c, s, 2), _interleave(-s, c, 2)], axis=1)
    ct, st = np.transpose(c, (0, 2, 1)), np.transpose(s, (0, 2, 1))
    mmat = _interleave(np.concatenate([ct, -st], axis=2), np.concatenate([st, ct], axis=2), 1)
    bf = lambda a: jnp.asarray(a, F32).astype(BF16)
    return bf(ga), bf(nmat), bf(mmat), bf(ma)


def _hy_mid_kernel(a_ref, n_ref, m_ref, k_ref, e_ref):
    half = FFT_R // 2
    for j in range(FFT_KF):
        t = jnp.dot(n_ref[j], _unpack_pairs(a_ref[j]), preferred_element_type=F32)
        tr, ti = t[:half], t[half:]
        kr, ki = k_ref[j, 0], k_ref[j, 1]
        y = jnp.concatenate([tr * kr - ti * ki, tr * ki + ti * kr], axis=0).astype(BF16)
        e_ref[j] = _pack_pairs(jnp.dot(m_ref[j], y, preferred_element_type=F32))


def _hy_mid(a, nmat, mmat, khat):
    nb = a.shape[0]
    c = a.shape[-1]
    nf1 = 2 * FFT_R
    blk = pl.BlockSpec((None, FFT_KF, FFT_R, c), lambda i, b: (b, i, 0, 0))
    return pl.pallas_call(
        _hy_mid_kernel,
        out_shape=jax.ShapeDtypeStruct(a.shape, jnp.uint32),
        grid=(nf1 // FFT_KF, nb),
        in_specs=[blk,
                  pl.BlockSpec((FFT_KF, FFT_R, 2 * FFT_R), lambda i, b: (i, 0, 0)),
                  pl.BlockSpec((FFT_KF, 2 * FFT_R, FFT_R), lambda i, b: (i, 0, 0)),
                  pl.BlockSpec((FFT_KF, 2, FFT_R // 2, c), lambda i, b: (i, 0, 0, 0))],
        out_specs=blk,
        compiler_params=_cparams(("arbitrary", "arbitrary")),
        name="l0_hyena_mid",
    )(a, nmat, mmat, khat)


def _hy_kfilt_kernel(a_ref, n_ref, nrm_ref, k_ref):
    half = FFT_R // 2
    nrm = nrm_ref[...]
    for j in range(FFT_KF):
        tf = jnp.dot(n_ref[j], _unpack_pairs(a_ref[0, j]), preferred_element_type=F32)
        tb = jnp.dot(n_ref[j], _unpack_pairs(a_ref[1, j]), preferred_element_type=F32)
        k_ref[j, 0] = (tf[:half] + tb[:half]) / nrm
        k_ref[j, 1] = (tf[half:] - tb[half:]) / nrm


def _hy_kfilt(a, nmat, nrm):
    c = a.shape[-1]
    nf1 = 2 * FFT_R
    return pl.pallas_call(
        _hy_kfilt_kernel,
        out_shape=jax.ShapeDtypeStruct((nf1, 2, FFT_R // 2, c), F32),
        grid=(nf1 // FFT_KF,),
        in_specs=[pl.BlockSpec((2, FFT_KF, FFT_R, c), lambda i: (0, i, 0, 0)),
                  pl.BlockSpec((FFT_KF, FFT_R, 2 * FFT_R), lambda i: (i, 0, 0)),
                  _const_spec(nrm.shape)],
        out_specs=pl.BlockSpec((FFT_KF, 2, FFT_R // 2, c), lambda i: (i, 0, 0, 0)),
        compiler_params=_cparams(("arbitrary",)),
        name="l0_hyena_kfilt",
    )(a, nmat, nrm)


def _hyena_long(u, e, x0, hfilt, nrm, nb):
    L = FFT_R * FFT_R
    c = u.shape[-1]
    v4 = lambda a, n: a.reshape(n, FFT_R, FFT_R, c)
    ga, nmat, mmat, ma = _hy2_tables()
    khat = _hy_kfilt(_lead_in(ga, v4(hfilt, 2), "l0_hyena_fwd_a"), nmat, nrm)
    ee = _hy_mid(_lead_in(ga, v4(u, nb), "l0_hyena_fwd_a"), nmat, mmat, khat)
    y = _lead_out(ma, ee, [v4(e, nb), v4(x0, nb)], "l0_hyena_inv_a")
    return y.reshape(nb * L, c)


def _fn2_tables():
    L = FFT_R * FFT_R
    r = np.arange(FFT_R, dtype=np.int64)
    idx = (FFT_R * r[None, :, None] * r[None, None, :] + r[None, :, None] * r[:, None, None]) % L
    gm = 2.0 * np.pi * idx / L
    c, s = np.cos(gm), np.sin(gm)
    g1 = _interleave(np.concatenate([c, -s], axis=2), np.concatenate([-s, -c], axis=2), 1)
    dl = 2.0 * np.pi * ((r[:, None] * r[None, :]) % FFT_R) / FFT_R
    g2 = _interleave(np.cos(dl), np.sin(dl), 1) / math.sqrt(L * FN_GROUP_CH)
    bf = lambda a: jnp.asarray(a, F32).astype(BF16)
    return bf(g1), bf(g2)


def _fnet_s1_kernel(x_ref, m_ref, cs_ref, g1_ref, o_ref, zc_ref, zs_ref):
    xs = pltpu.einshape("ajc->jac", x_ref[...]).reshape(FFT_KF * FFT_R, D_MODEL)
    m = m_ref[...]
    h = (_ln_plain(xs) * (1.0 + m[:, D_MODEL:2 * D_MODEL]) + m[:, 0:D_MODEL]).astype(BF16)
    cs = cs_ref[...]
    for g in range(D_MODEL // FN_GROUP_CH):
        a = g * FN_GROUP_CH
        z = jnp.dot(h[:, a:a + FN_GROUP_CH], cs, preferred_element_type=F32)
        zc_ref[:, a:a + FN_GROUP_CH] = z[:, :FN_GROUP_CH].astype(BF16)
        zs_ref[:, a:a + FN_GROUP_CH] = z[:, FN_GROUP_CH:].astype(BF16)
    for j in range(FFT_KF):
        r0 = j * FFT_R
        s = jnp.concatenate([zc_ref[r0:r0 + FFT_R, :], zs_ref[r0:r0 + FFT_R, :]], axis=0)
        o_ref[j] = _pack_pairs(jnp.dot(g1_ref[j], s, preferred_element_type=F32))


def _fnet_long(x, mods, mod_base, nb):
    L = FFT_R * FFT_R
    d = D_MODEL
    g1, g2 = _fn2_tables()
    cs = _group_dft_table()
    bb = pl.pallas_call(
        _fnet_s1_kernel,
        out_shape=jax.ShapeDtypeStruct((nb, FFT_R, FFT_R, d), jnp.uint32),
        grid=(nb, FFT_R // FFT_KF),
        in_specs=[pl.BlockSpec((None, FFT_R, FFT_KF, d), lambda b, k: (b, 0, k, 0)),
                  pl.BlockSpec((None, 1, 6 * d), lambda b, k: (mod_base + b, 0, 0)),
                  pl.BlockSpec(cs.shape, lambda b, k: (0, 0)),
                  pl.BlockSpec((FFT_KF, 2 * FFT_R, 2 * FFT_R), lambda b, k: (k, 0, 0))],
        out_specs=pl.BlockSpec((None, FFT_KF, FFT_R, d), lambda b, k: (b, k, 0, 0)),
        scratch_shapes=[pltpu.VMEM((FFT_KF * FFT_R, d), BF16), pltpu.VMEM((FFT_KF * FFT_R, d), BF16)],
        compiler_params=_cparams(("arbitrary", "arbitrary")),
        name="l1_fnet_stage1",
    )(x.reshape(nb, FFT_R, FFT_R, d), mods, cs, g1)
    y = _lead_out(g2, bb, [], "l1_fnet_stage2")
    return y.reshape(nb * L, d)


def _group_dft_table():
    g = FN_GROUP_CH
    jk = (np.arange(g, dtype=np.int64)[:, None] * np.arange(g, dtype=np.int64)[None, :]) % g
    ang = 2.0 * np.pi * jk / g
    return jnp.asarray(np.concatenate([np.cos(ang), np.sin(ang)], axis=1), F32).astype(BF16)


def _fnet_front_kernel(x_ref, m_ref, cs_ref, zc_ref, zs_ref):
    m = m_ref[...]
    h = (_ln_plain(x_ref[...]) * (1.0 + m[:, D_MODEL:2 * D_MODEL]) + m[:, 0:D_MODEL]).astype(BF16)
    cs = cs_ref[...]
    for g in range(D_MODEL // FN_GROUP_CH):
        a = g * FN_GROUP_CH
        z = jnp.dot(h[:, a:a + FN_GROUP_CH], cs, preferred_element_type=F32)
        zc_ref[:, a:a + FN_GROUP_CH] = z[:, :FN_GROUP_CH].astype(BF16)
        zs_ref[:, a:a + FN_GROUP_CH] = z[:, FN_GROUP_CH:].astype(BF16)


def _fnet_front(x, mods, mod_base, tiles_per_mod):
    t = x.shape[0]
    tm = ROW_TILE
    cs = _group_dft_table()
    return pl.pallas_call(
        _fnet_front_kernel,
        out_shape=(jax.ShapeDtypeStruct((t, D_MODEL), BF16),) * 2,
        grid=(t // tm,),
        in_specs=[pl.BlockSpec((tm, D_MODEL), lambda i: (i, 0)),
                  _mod_spec(mod_base, tiles_per_mod),
                  _const_spec(cs.shape)],
        out_specs=(pl.BlockSpec((tm, D_MODEL), lambda i: (i, 0)),) * 2,
        compiler_params=_cparams(("arbitrary",)),
        name="l1_fnet_front",
    )(x, mods, cs)


def _post_kernel(n_a, *refs):
    x_ref, m_ref = refs[0], refs[1]
    a_refs = refs[2:2 + n_a]
    wo_refs = refs[2 + n_a:2 + 2 * n_a]
    g1_ref, b1_ref, w1_ref, w2_ref, g2_ref, b2_ref, o_ref = refs[2 + 2 * n_a:]
    m = m_ref[...]
    d = D_MODEL
    out = _bdot(a_refs[0][...], wo_refs[0][...])
    for a_ref, wo_ref in zip(a_refs[1:], wo_refs[1:]):
        out += _bdot(a_ref[...], wo_ref[...])
    x1 = _ln_plain(ALPHA * x_ref[...] + m[:, 2 * d:3 * d] * out) * g1_ref[...] + b1_ref[...]
    h = (_ln_plain(x1) * (1.0 + m[:, 4 * d:5 * d]) + m[:, 3 * d:4 * d]).astype(BF16)
    acc = jnp.zeros_like(x1)
    for c in range(D_FF // d):
        hc = jnp.maximum(jnp.dot(h, w1_ref[:, c * d:(c + 1) * d], preferred_element_type=F32), 0.0)
        acc += jnp.dot((hc * hc).astype(BF16), w2_ref[c * d:(c + 1) * d, :], preferred_element_type=F32)
    o_ref[...] = _ln_plain(ALPHA * x1 + m[:, 5 * d:6 * d] * acc) * g2_ref[...] + b2_ref[...]


def _post(x, mods, mod_base, tiles_per_mod, a_list, wo_list, g1, b1, w1, w2, g2, b2):
    t = x.shape[0]
    tm = POST_TILE
    row = lambda c: pl.BlockSpec((tm, c), lambda i: (i, 0))
    once = lambda v: pl.BlockSpec(v.shape, lambda i: (0,) * v.ndim, pipeline_mode=pl.Buffered(1))
    in_specs = ([row(D_MODEL), _mod_spec(mod_base, tiles_per_mod * ROW_TILE // tm)]
                + [row(a.shape[1]) for a in a_list]
                + [once(w) for w in wo_list]
                + [once(v) for v in (g1, b1, w1, w2, g2, b2)])
    return pl.pallas_call(
        functools.partial(_post_kernel, len(a_list)),
        out_shape=jax.ShapeDtypeStruct((t, D_MODEL), F32),
        grid=(t // tm,),
        in_specs=in_specs,
        out_specs=row(D_MODEL),
        compiler_params=_cparams(("arbitrary",)),
        name="post_mlp",
    )(x, mods, *a_list, *wo_list, g1, b1, w1, w2, g2, b2)


def _rot_cols(w):
    parts = []
    for seg in range(2):
        o = seg * 32
        parts += [-w[:, o + 16:o + 32], w[:, o:o + 16]]
    return jnp.concatenate(parts, axis=1)


def _pad_cols(w, n):
    return jnp.pad(w, ((0, 0), (0, n - w.shape[1])))


def _rope_tables(L):
    rows = L // GRID_W
    row = np.repeat(np.arange(rows, dtype=np.float64), GRID_W)
    col = np.tile(np.arange(GRID_W, dtype=np.float64), rows)
    half = QK_ROPE // 2
    inv = 1.0 / (ROPE_THETA ** (np.arange(0, half, 2, dtype=np.float64) / half))
    ar = row[:, None] * inv[None, :]
    ac = col[:, None] * inv[None, :]
    ang = np.concatenate([ar, ar, ac, ac], axis=1)
    cos = np.concatenate([np.cos(ang), np.ones_like(ang)], axis=1)
    sin = np.concatenate([np.sin(ang), np.zeros_like(ang)], axis=1)
    return jnp.asarray(cos, F32), jnp.asarray(sin, F32)


def kernel(x_prompt, x_sample, cache_l0_ckv, cache_l0_krope, c, c_ctx, l0_ada_w, l0_ada_b, l0_w_in, l0_conv_w, l0_conv_b, l0_hf_w1, l0_hf_b1, l0_hf_freq, l0_hf_w2, l0_hf_b2, l0_hf_w3, l0_hf_skip, l0_q_norm, l0_q_up, l0_kv_norm, l0_kv_up, l0_w_out, l0_ln1_g, l0_ln1_b, l0_mlp_w1, l0_mlp_w2, l0_ln2_g, l0_ln2_b, l1_ada_w, l1_ada_b, l1_w_out, l1_ln1_g, l1_ln1_b, l1_mlp_w1, l1_mlp_w2, l1_ln2_g, l1_ln2_b):
    nbc, lc, d = x_prompt.shape
    nbs, ls, _ = x_sample.shape
    past = cache_l0_ckv.shape[1]
    tm = ROW_TILE
    row1 = lambda v: v.reshape(1, -1)

    cond8 = jnp.concatenate([c_ctx[None, :], c, jnp.zeros((8 - 1 - nbs, d), F32)], axis=0)
    mods0 = _modulation(cond8, l0_ada_w, l0_ada_b)
    mods1 = _modulation(cond8, l1_ada_w, l1_ada_b)

    kpe_w = l0_w_in[:, 1920:1984]
    win = jnp.concatenate([l0_w_in[:, :1920], _pad_cols(kpe_w, LANE), _pad_cols(_rot_cols(kpe_w), LANE)],
                          axis=1).astype(BF16)
    dh = QK_NOPE + QK_ROPE
    q_nope = [l0_q_up[:, h * dh:h * dh + QK_NOPE] for h in range(MLA_HEADS)]
    q_pe = [l0_q_up[:, h * dh + QK_NOPE:(h + 1) * dh] for h in range(MLA_HEADS)]
    qup = jnp.concatenate(q_nope + [_pad_cols(w, LANE) for w in q_pe]
                          + [_pad_cols(_rot_cols(w), LANE) for w in q_pe], axis=1).astype(BF16)
    kvup = l0_kv_up.astype(BF16)
    front_w = (win, row1(l0_q_norm), qup, row1(l0_kv_norm), kvup)
    w1p = jnp.pad(l0_hf_w1, ((0, LANE - l0_hf_w1.shape[0]), (0, 0)))
    filt_w = (w1p, row1(l0_hf_b1), row1(l0_hf_freq), l0_hf_w2, row1(l0_hf_b2), l0_hf_w3.astype(BF16))
    wo0 = l0_w_out.astype(BF16)
    conv_b = row1(l0_conv_b)
    skip = row1(l0_hf_skip)

    xc = x_prompt.reshape(nbc * lc, d)
    xs = x_sample.reshape(nbs * ls, d)
    groups = (
        dict(x=xc, nb=nbc, L=lc, mod_base=0, tiles_per_mod=nbc * lc // tm, dft_nb=4, tq=lc),
        dict(x=xs, nb=nbs, L=ls, mod_base=1, tiles_per_mod=ls // tm, dft_nb=nbs, tq=512),
    )
    ones_tab = (jnp.concatenate([jnp.ones((tm, LANE), F32)], axis=0), jnp.zeros((tm, LANE), F32))

    outs = []
    ctx_ckv = ctx_krope = None
    for gi, g in enumerate(groups):
        nb, L = g["nb"], g["L"]
        tiles_per_seq = L // tm
        latent = gi == 1
        cos, sin = _rope_tables(L) if latent else ones_tab
        hy, q, k, vt, kvn, kpe = _front(g["x"], mods0, g["mod_base"], g["tiles_per_mod"], front_w,
                                       cos, sin, tiles_per_seq if latent else 1)
        if latent:
            extra = _cache_kv(cache_l0_ckv.reshape(nbs * past, KV_LORA),
                              _pad_cols(cache_l0_krope.reshape(nbs * past, QK_ROPE), LANE), kvup)
        else:
            extra = None
            ctx_ckv = kvn.reshape(nb, L, KV_LORA)
            ctx_krope = kpe.reshape(nb, L, QK_ROPE)
        y_mla = _attention(q, k, vt, extra, nb, L, g["tq"])

        two_stage = L == FFT_R * FFT_R
        io_dtype = F32 if two_stage else BF16
        u, e, x0 = _conv_gate(hy, l0_conv_w, conv_b, skip, tiles_per_seq, io_dtype)
        hfilt, hnorm = _filters(L, io_dtype, *filt_w)
        if two_stage:
            y_hy = _hyena_long(u, e, x0, hfilt, hnorm, nb)
        else:
            kre, kim = _dft("filt", "hy_fwd", [hfilt], [hnorm], 2)
            sh = (nb, L, HY_CH)
            yre, yim = _dft("fwdk", "hy_fwd", [u.reshape(sh)], [kre, kim], g["dft_nb"])
            y_hy = _dft("inv", "hy_inv", [yre, yim], [e.reshape(sh), x0.reshape(sh)], g["dft_nb"])
            y_hy = y_hy.reshape(nb * L, HY_CH)

        x1 = _post(g["x"], mods0, g["mod_base"], g["tiles_per_mod"], [y_hy, y_mla],
                   [wo0[:HY_CH], wo0[HY_CH:]], row1(l0_ln1_g), row1(l0_ln1_b),
                   l0_mlp_w1.astype(BF16), l0_mlp_w2.astype(BF16), row1(l0_ln2_g), row1(l0_ln2_b))

        if L == FFT_R * FFT_R:
            yf = _fnet_long(x1, mods1, g["mod_base"], nb)
        else:
            zc, zs = _fnet_front(x1, mods1, g["mod_base"], g["tiles_per_mod"])
            sh = (nb, L, d)
            yf = _dft("fnet", "fnet", [zc.reshape(sh), zs.reshape(sh)], [], min(g["dft_nb"], 2))
            yf = yf.reshape(nb * L, d)
        x2 = _post(x1, mods1, g["mod_base"], g["tiles_per_mod"], [yf],
                   [l1_w_out.astype(BF16)], row1(l1_ln1_g), row1(l1_ln1_b),
                   l1_mlp_w1.astype(BF16), l1_mlp_w2.astype(BF16), row1(l1_ln2_g), row1(l1_ln2_b))
        outs.append(x2.reshape(nb, L, d))

    return (outs[0], outs[1], ctx_ckv, ctx_krope)
```

```python
import functools
import math

import numpy as np
import jax
import jax.numpy as jnp
from jax import lax
from jax.experimental import pallas as pl
from jax.experimental.pallas import tpu as pltpu

F32 = jnp.float32
BF16 = jnp.bfloat16
HI = lax.Precision.HIGHEST

D_MODEL = 1024
DEPTH = 2
GRID_W = 64
HY_CH = 512
FILT_BANDS = 16
FILT_ORDER = 64
FAST_DECAY_PCT = 0.3
SLOW_DECAY_PCT = 1.5
DECAY_TARGET = 1e-2
MAX_DECAY = math.log(DECAY_TARGET) / FAST_DECAY_PCT
MIN_DECAY = math.log(DECAY_TARGET) / SLOW_DECAY_PCT
MLA_HEADS = 4
QK_NOPE = 128
QK_ROPE = 64
V_HEAD = 128
Q_LORA = 256
KV_LORA = 128
ROPE_THETA = 10000.0
FN_GROUP_CH = 128
D_FF = 4096
ALPHA = (2 * DEPTH) ** 0.25
LN_EPS = 1e-5
RMS_EPS = 1e-6

LANE = 128
ROW_TILE = 256
POST_TILE = 512
QK_PAD = 256
VT_ROWS = V_HEAD + 16
LOG2E = 1.4426950408889634
VMEM_LIMIT = 56 * 1024 * 1024


def _cparams(sem):
    return pltpu.CompilerParams(dimension_semantics=sem, vmem_limit_bytes=VMEM_LIMIT)


def _ln_plain(x):
    mu = jnp.mean(x, axis=-1, keepdims=True)
    xc = x - mu
    var = jnp.mean(xc * xc, axis=-1, keepdims=True)
    return xc * lax.rsqrt(var + LN_EPS)


def _rms(x, g):
    return x * lax.rsqrt(jnp.mean(x * x, axis=-1, keepdims=True) + RMS_EPS) * g


def _bdot(a, b):
    return jnp.dot(a.astype(BF16), b, preferred_element_type=F32)


def _vt_rows(v):
    ones = jnp.ones((VT_ROWS - V_HEAD, v.shape[0]), BF16)
    return jnp.concatenate([jnp.transpose(v).astype(BF16), ones], axis=0)


def _mod_kernel(c_ref, w_ref, b_ref, o_ref):
    c = c_ref[...]
    s = c / (1.0 + jnp.exp(-c))
    o_ref[...] = jnp.dot(s, w_ref[...], precision=HI, preferred_element_type=F32) + b_ref[...]


def _modulation(cond8, w, b):
    n = w.shape[1]
    tn = 1536
    out = pl.pallas_call(
        _mod_kernel,
        out_shape=jax.ShapeDtypeStruct((8, n), F32),
        grid=(n // tn,),
        in_specs=[pl.BlockSpec((8, D_MODEL), lambda j: (0, 0)),
                  pl.BlockSpec((D_MODEL, tn), lambda j: (0, j)),
                  pl.BlockSpec((1, tn), lambda j: (0, j))],
        out_specs=pl.BlockSpec((8, tn), lambda j: (0, j)),
        compiler_params=_cparams(("arbitrary",)),
        name="modulation",
    )(cond8, w, b.reshape(1, n))
    return out.reshape(8, 1, n)


def _mod_spec(mod_base, tiles_per_mod):
    return pl.BlockSpec((None, 1, 6 * D_MODEL), lambda i: (mod_base + i // tiles_per_mod, 0, 0))


def _const_spec(shape):
    nd = len(shape)
    return pl.BlockSpec(shape, lambda i: (0,) * nd)


HALO = 8


def _front_kernel(tiles_per_seq, x_ref, xp_ref, xn_ref, m_ref, win_ref, qn_ref, qup_ref, kvn_ref, kvup_ref,
                  cos_ref, sin_ref, cw_ref, cb_ref, skip_ref,
                  u_ref, e_ref, x0_ref, q_ref, k_ref, v_ref, kvn_out_ref, kpe_ref):
    i = pl.program_id(0)
    m = m_ref[...]
    tm = x_ref.shape[0]
    nh = 3 * HY_CH
    xe = jnp.concatenate([xp_ref[...], x_ref[...], xn_ref[...]], axis=0)
    he = _ln_plain(xe) * (1.0 + m[:, D_MODEL:2 * D_MODEL]) + m[:, 0:D_MODEL]

    zh = _bdot(he, win_ref[:, :nh])
    pos = i % tiles_per_seq
    rows = lax.broadcasted_iota(jnp.int32, (tm + 2 * HALO, 1), 0)
    inside = jnp.logical_and(jnp.logical_or(rows >= HALO, pos != 0),
                             jnp.logical_or(rows < tm + HALO, pos != tiles_per_seq - 1))
    zh = jnp.where(inside, zh, 0.0)
    cw = cw_ref[...]
    pz = (pltpu.roll(zh, 1, 0) * cw[0:1, :] + zh * cw[1:2, :]
          + pltpu.roll(zh, tm + 2 * HALO - 1, 0) * cw[2:3, :])[HALO:HALO + tm] + cb_ref[...]
    u = pz[:, 2 * HY_CH:] * pz[:, HY_CH:2 * HY_CH]
    u_ref[...] = u.astype(u_ref.dtype)
    e_ref[...] = u * skip_ref[...]
    x0_ref[...] = pz[:, :HY_CH]

    z = _bdot(he[HALO:HALO + tm], win_ref[:, nh:])
    q_c = z[:, 0:256]
    kv_c = z[:, 256:384]
    cos = cos_ref[...]
    sin = sin_ref[...]
    kpe = z[:, 384:512] * cos + z[:, 512:640] * sin
    kpe_ref[...] = kpe[:, :QK_ROPE]
    kpe_b = kpe.astype(BF16)
    q = _bdot(_rms(q_c, qn_ref[...]), qup_ref[...]) * (LOG2E / math.sqrt(QK_NOPE + QK_ROPE))
    kvn = _rms(kv_c, kvn_ref[...])
    kvn_out_ref[...] = kvn
    kv = _bdot(kvn, kvup_ref[...])
    for hd in range(MLA_HEADS):
        a = hd * LANE
        q_pe = (q[:, 512 + a:512 + a + LANE] * cos + q[:, 1024 + a:1024 + a + LANE] * sin).astype(BF16)
        q_ref[hd] = jnp.concatenate([q[:, a:a + LANE].astype(BF16), q_pe], axis=-1)
        k_ref[hd] = jnp.concatenate([kv[:, 2 * a:2 * a + LANE].astype(BF16), kpe_b], axis=-1)
        v_ref[hd] = _vt_rows(kv[:, 2 * a + LANE:2 * a + 2 * LANE])


def _front(x, mods, mod_base, tiles_per_mod, w, cos, sin, rope, tiles_per_seq, u_dtype):
    t = x.shape[0]
    tm = ROW_TILE
    win, qn, qup, kvn, kvup, conv_w, conv_b, skip = w
    if rope:
        tab_spec = pl.BlockSpec((tm, LANE), lambda i: (i % tiles_per_seq, 0))
    else:
        tab_spec = pl.BlockSpec((tm, LANE), lambda i: (0, 0))
    r8 = tm // HALO
    n8 = t // HALO
    hy_out = lambda dt: jax.ShapeDtypeStruct((t, HY_CH), dt)
    hy_spec = pl.BlockSpec((tm, HY_CH), lambda i: (i, 0))
    return pl.pallas_call(
        functools.partial(_front_kernel, tiles_per_seq),
        out_shape=(hy_out(u_dtype), hy_out(F32), hy_out(F32),
                   jax.ShapeDtypeStruct((MLA_HEADS, t, QK_PAD), BF16),
                   jax.ShapeDtypeStruct((MLA_HEADS, t, QK_PAD), BF16),
                   jax.ShapeDtypeStruct((MLA_HEADS, VT_ROWS, t), BF16),
                   jax.ShapeDtypeStruct((t, KV_LORA), F32),
                   jax.ShapeDtypeStruct((t, QK_ROPE), F32)),
        grid=(t // tm,),
        in_specs=[pl.BlockSpec((tm, D_MODEL), lambda i: (i, 0)),
                  pl.BlockSpec((HALO, D_MODEL), lambda i: (jnp.maximum(i * r8 - 1, 0), 0)),
                  pl.BlockSpec((HALO, D_MODEL), lambda i: (jnp.minimum((i + 1) * r8, n8 - 1), 0)),
                  _mod_spec(mod_base, tiles_per_mod),
                  _const_spec(win.shape), _const_spec(qn.shape), _const_spec(qup.shape),
                  _const_spec(kvn.shape), _const_spec(kvup.shape),
                  tab_spec, tab_spec,
                  _const_spec(conv_w.shape), _const_spec(conv_b.shape), _const_spec(skip.shape)],
        out_specs=(hy_spec, hy_spec, hy_spec,
                   pl.BlockSpec((MLA_HEADS, tm, QK_PAD), lambda i: (0, i, 0)),
                   pl.BlockSpec((MLA_HEADS, tm, QK_PAD), lambda i: (0, i, 0)),
                   pl.BlockSpec((MLA_HEADS, VT_ROWS, tm), lambda i: (0, 0, i)),
                   pl.BlockSpec((tm, KV_LORA), lambda i: (i, 0)),
                   pl.BlockSpec((tm, QK_ROPE), lambda i: (i, 0))),
        compiler_params=_cparams(("arbitrary",)),
        name="l0_front",
    )(x, x, x, mods, win, qn, qup, kvn, kvup, cos, sin, conv_w, conv_b, skip)


def _cache_kv_kernel(ckv_ref, kr_ref, kvup_ref, k_ref, v_ref):
    kv = _bdot(ckv_ref[...], kvup_ref[...])
    kr = kr_ref[...].astype(BF16)
    for hd in range(MLA_HEADS):
        a = 2 * hd * LANE
        k_ref[hd] = jnp.concatenate([kv[:, a:a + LANE].astype(BF16), kr], axis=-1)
        v_ref[hd] = _vt_rows(kv[:, a + LANE:a + 2 * LANE])


def _cache_kv(ckv, krope_pad, kvup):
    t = ckv.shape[0]
    return pl.pallas_call(
        _cache_kv_kernel,
        out_shape=(jax.ShapeDtypeStruct((MLA_HEADS, t, QK_PAD), BF16),
                   jax.ShapeDtypeStruct((MLA_HEADS, VT_ROWS, t), BF16)),
        name="l0_cache_kv",
    )(ckv, krope_pad, kvup)


def _col_reduce(x, op):
    rows, n = x.shape
    for g in (32, 8):
        if rows % (8 * g) == 0 and rows > 8 * g:
            x = op(x.reshape(rows // (8 * g), 8 * g, n), axis=0)
            rows = 8 * g
    return op(x, axis=0, keepdims=True)


def _attn_kernel(n_kv, q_ref, *refs):
    k_refs, vt_refs = refs[:n_kv], refs[n_kv:2 * n_kv]
    o_ref, s_even, s_odd = refs[2 * n_kv:]
    i = pl.program_id(0)

    @pl.when(i == 0)
    def _():
        s_odd[...] = jnp.zeros_like(s_odd)

    def step(s_write, s_read):
        q = q_ref[...]
        nt = (((1,), (1,)), ((), ()))
        r0 = 0
        for k_ref in k_refs:
            lk = k_ref.shape[0]
            s_write[r0:r0 + lk, :] = lax.dot_general(k_ref[...], q, nt, preferred_element_type=F32)
            r0 += lk
        s = s_read[...]
        m = _col_reduce(s, jnp.max)
        pb = jnp.exp2(s - m).astype(BF16)
        acc = None
        r0 = 0
        for vt_ref in vt_refs:
            lk = vt_ref.shape[1]
            pv = jnp.dot(vt_ref[...], pb[r0:r0 + lk, :], preferred_element_type=F32)
            acc = pv if acc is None else acc + pv
            r0 += lk
        o_ref[...] = jnp.transpose(acc[:V_HEAD] / acc[V_HEAD:V_HEAD + 1]).astype(o_ref.dtype)

    pl.when(i % 2 == 0)(lambda: step(s_even, s_odd))
    pl.when(i % 2 == 1)(lambda: step(s_odd, s_even))


def _attention(q, k, vt, extra, nb, lq, tq):
    nq = lq // tq
    n_tiles = nb * MLA_HEADS * nq

    def where(t):
        bh = t // nq
        return bh // MLA_HEADS, bh % MLA_HEADS, t % nq

    def score_side(fn):
        return lambda i: fn(*where(jnp.minimum(i, n_tiles - 1)))

    def value_side(fn):
        return lambda i: fn(*where(jnp.maximum(i - 1, 0)))

    ks, vts = [k], [vt]
    if extra is not None:
        ks.append(extra[0])
        vts.append(extra[1])
    in_specs = [pl.BlockSpec((None, tq, QK_PAD), score_side(lambda b, h, j: (h, b * nq + j, 0)))]
    in_specs += [pl.BlockSpec((None, a.shape[1] // nb, QK_PAD), score_side(lambda b, h, j: (h, b, 0))) for a in ks]
    in_specs += [pl.BlockSpec((None, VT_ROWS, a.shape[2] // nb), value_side(lambda b, h, j: (h, 0, b))) for a in vts]
    lk_total = sum(a.shape[1] // nb for a in ks)
    return pl.pallas_call(
        functools.partial(_attn_kernel, len(ks)),
        out_shape=jax.ShapeDtypeStruct((nb * lq, MLA_HEADS * V_HEAD), BF16),
        grid=(n_tiles + 1,),
        in_specs=in_specs,
        out_specs=pl.BlockSpec((tq, V_HEAD), value_side(lambda b, h, j: (b * nq + j, h))),
        scratch_shapes=[pltpu.VMEM((lk_total, tq), F32), pltpu.VMEM((lk_total, tq), F32)],
        compiler_params=_cparams(("arbitrary",)),
        name="l0_attention",
    )(q, *ks, *vts)


def _filter_kernel(z_ref, w1_ref, b1_ref, fr_ref, w2_ref, b2_ref, w3_ref, dl_ref, h_ref, norm_ref):
    i = pl.program_id(0)
    z = z_ref[...]
    tl = z.shape[0]
    fr = fr_ref[...]
    h = jnp.sin(fr * (jnp.dot(z, w1_ref[...], precision=HI, preferred_element_type=F32) + b1_ref[...]))
    h = jnp.sin(fr * (jnp.dot(h, w2_ref[...], precision=HI, preferred_element_type=F32) + b2_ref[...]))
    h = _bdot(h, w3_ref[...])
    decay = jnp.exp(-(z[:, 0:1] * dl_ref[...]))
    hf = h[:, :HY_CH] * decay
    hb = h[:, HY_CH:] * decay
    part = jnp.sum(jnp.abs(hf) + jnp.abs(hb), axis=0, keepdims=True)

    @pl.when(i == 0)
    def _():
        norm_ref[...] = part

    @pl.when(i > 0)
    def _():
        norm_ref[...] += part

    rows = lax.broadcasted_iota(jnp.int32, hb.shape, 0) + i * tl
    h_ref[0] = hf.astype(h_ref.dtype)
    h_ref[1] = jnp.where(rows == 0, 0.0, hb).astype(h_ref.dtype)


def _filter_embedding(L):
    t = np.linspace(0.0, 1.0, L)[:, None]
    w_ang = 2.0 * np.pi * np.arange(L) / L
    bands = np.linspace(1e-4, FILT_BANDS - 1, FILT_BANDS)
    ang = w_ang[:, None] * bands[None, :]
    z = np.zeros((L, LANE), np.float64)
    z[:, 0:1] = t
    z[:, 1:1 + FILT_BANDS] = np.cos(ang)
    z[:, 1 + FILT_BANDS:1 + 2 * FILT_BANDS] = -np.sin(ang)
    return jnp.asarray(z, F32)


def _filters(L, h_dtype, w1p, b1, fr, w2, b2, w3):
    tl = min(L, 512)
    z = _filter_embedding(L)
    deltas = jnp.asarray(np.abs(np.linspace(MIN_DECAY, MAX_DECAY, HY_CH))[None, :], F32)
    return pl.pallas_call(
        _filter_kernel,
        out_shape=(jax.ShapeDtypeStruct((2, L, HY_CH), h_dtype), jax.ShapeDtypeStruct((1, HY_CH), F32)),
        grid=(L // tl,),
        in_specs=[pl.BlockSpec((tl, LANE), lambda i: (i, 0)),
                  _const_spec(w1p.shape), _const_spec(b1.shape), _const_spec(fr.shape),
                  _const_spec(w2.shape), _const_spec(b2.shape), _const_spec(w3.shape),
                  _const_spec(deltas.shape)],
        out_specs=(pl.BlockSpec((2, tl, HY_CH), lambda i: (0, i, 0)),
                   pl.BlockSpec((1, HY_CH), lambda i: (0, 0))),
        compiler_params=_cparams(("arbitrary",)),
        name="l0_hyena_filters",
    )(z, w1p, b1, fr, w2, b2, w3, deltas)


def _dft_tables(kind, L, ti):
    ni = L // ti
    i = np.arange(ti, dtype=np.int64)[:, None]
    big = (np.arange(ni, dtype=np.int64) * ti)[:, None]
    c = np.arange(L, dtype=np.int64)[None, :]
    if kind == "hy_fwd":
        period = 4 * L
        base_idx = (2 * i + 1) * c
        r_idx = 2 * big * c
        scale = 1.0
    elif kind == "hy_inv":
        period = 4 * L
        base_idx = (2 * c + 1) * i
        r_idx = (2 * c + 1) * big
        scale = 1.0 / L
    else:
        period = L
        base_idx = i * c
        r_idx = big * c
        scale = 1.0 / math.sqrt(L * FN_GROUP_CH)
    ab = 2.0 * np.pi * (base_idx % period) / period
    ar = 2.0 * np.pi * (r_idx % period) / period
    return (jnp.asarray(np.cos(ab), F32), jnp.asarray(np.sin(ab), F32),
            jnp.asarray(scale * np.cos(ar), F32).reshape(ni, 1, L),
            jnp.asarray(scale * np.sin(ar), F32).reshape(ni, 1, L))


def _dft_kernel(mode, nb, n_x, *refs):
    bc_ref, bs_ref, rc_ref, rs_ref = refs[:4]
    x_refs = refs[4:4 + n_x]
    rest = refs[4 + n_x:]
    p_ref, q_ref = rest[-2], rest[-1]
    j = pl.program_id(2)
    nj = pl.num_programs(2)
    tj = x_refs[0].shape[1]
    if bc_ref.shape[1] == tj:
        bc, bs, rc, rs = bc_ref[...], bs_ref[...], rc_ref[...], rs_ref[...]
    else:
        off = pl.multiple_of(j * tj, tj)
        bc, bs = bc_ref[:, pl.ds(off, tj)], bs_ref[:, pl.ds(off, tj)]
        rc, rs = rc_ref[:, pl.ds(off, tj)], rs_ref[:, pl.ds(off, tj)]
    tc = (bc * rc - bs * rs).astype(BF16)
    ts = (bs * rc + bc * rs).astype(BF16)
    x1_ref = x_refs[0]
    x2_ref = x_refs[-1]

    pq = [(jnp.dot(tc, x1_ref[b], preferred_element_type=F32),
           jnp.dot(ts, x2_ref[b], preferred_element_type=F32)) for b in range(nb)]

    @pl.when(j == 0)
    def _():
        for b in range(nb):
            p_ref[b] = pq[b][0]
            q_ref[b] = pq[b][1]

    @pl.when(j > 0)
    def _():
        for b in range(nb):
            p_ref[b] += pq[b][0]
            q_ref[b] += pq[b][1]

    @pl.when(j == nj - 1)
    def _():
        if mode == "filt":
            nrm = rest[0][...]
            kre_ref, kim_ref = rest[1], rest[2]
            kre_ref[...] = (p_ref[0] + p_ref[1]) / nrm
            kim_ref[...] = (q_ref[1] - q_ref[0]) / nrm
        elif mode == "fwdk":
            kre, kim = rest[0][...], rest[1][...]
            yre_ref, yim_ref = rest[2], rest[3]
            for b in range(nb):
                pp, qq = p_ref[b], q_ref[b]
                yre_ref[b] = (pp * kre + qq * kim).astype(BF16)
                yim_ref[b] = (pp * kim - qq * kre).astype(BF16)
        elif mode == "inv":
            e_ref, x0_ref, o_ref = rest[0], rest[1], rest[2]
            for b in range(nb):
                o_ref[b] = ((p_ref[b] - q_ref[b] + e_ref[b]) * x0_ref[b]).astype(BF16)
        else:
            o_ref = rest[0]
            for b in range(nb):
                o_ref[b] = (p_ref[b] - q_ref[b]).astype(BF16)


def _dft(mode, kind, xs, extras, nb):
    B, L, C = xs[0].shape
    ti = min(L, 256)
    tj = min(L, 512)
    bc, bs, rc, rs = _dft_tables(kind, L, ti)
    grid = (B // nb, L // ti, L // tj)
    x_spec = pl.BlockSpec((nb, tj, C), lambda g, i, j: (g, j, 0))
    row_spec = lambda c, dt=None: pl.BlockSpec((nb, ti, c), lambda g, i, j: (g, i, 0))
    in_specs = [pl.BlockSpec((ti, L), lambda g, i, j: (0, 0)),
                pl.BlockSpec((ti, L), lambda g, i, j: (0, 0)),
                pl.BlockSpec((None, 1, L), lambda g, i, j: (i, 0, 0)),
                pl.BlockSpec((None, 1, L), lambda g, i, j: (i, 0, 0))] + [x_spec] * len(xs)
    if mode == "filt":
        in_specs += [pl.BlockSpec((1, HY_CH), lambda g, i, j: (0, 0))]
        out_shape = (jax.ShapeDtypeStruct((L, HY_CH), F32),) * 2
        out_specs = (pl.BlockSpec((ti, HY_CH), lambda g, i, j: (i, 0)),) * 2
    elif mode == "fwdk":
        in_specs += [pl.BlockSpec((ti, HY_CH), lambda g, i, j: (i, 0))] * 2
        out_shape = (jax.ShapeDtypeStruct((B, L, C), BF16),) * 2
        out_specs = (row_spec(C),) * 2
    elif mode == "inv":
        in_specs += [row_spec(C)] * 2
        out_shape = jax.ShapeDtypeStruct((B, L, C), BF16)
        out_specs = row_spec(C)
    else:
        out_shape = jax.ShapeDtypeStruct((B, L, C), BF16)
        out_specs = row_spec(C)
    return pl.pallas_call(
        functools.partial(_dft_kernel, mode, nb, len(xs)),
        out_shape=out_shape,
        grid=grid,
        in_specs=in_specs,
        out_specs=out_specs,
        scratch_shapes=[pltpu.VMEM((nb, ti, C), F32), pltpu.VMEM((nb, ti, C), F32)],
        compiler_params=_cparams(("arbitrary", "arbitrary", "arbitrary")),
        name="dft_" + mode,
    )(bc, bs, rc, rs, *xs, *extras)


FFT_R = 64
FFT_KF = 8


def _pack_pairs(x):
    return pltpu.bitcast(x.astype(BF16), jnp.uint32)


def _unpack_pairs(w):
    return pltpu.bitcast(w, BF16)


def _lead_in_kernel(g_ref, x_ref, o_ref):
    g = g_ref[...]
    xt = jnp.swapaxes(x_ref[...], 0, 1)
    ys = [_pack_pairs(jnp.dot(g, xt[j].astype(BF16), preferred_element_type=F32)) for j in range(FFT_KF)]
    o_ref[...] = jnp.swapaxes(jnp.stack(ys, axis=0), 0, 1)


def _lead_in(g, x, name):
    nbx, _, _, c = x.shape
    m2 = g.shape[0] // 2
    return pl.pallas_call(
        _lead_in_kernel,
        out_shape=jax.ShapeDtypeStruct((nbx, m2, FFT_R, c), jnp.uint32),
        grid=(nbx, FFT_R // FFT_KF),
        in_specs=[pl.BlockSpec(g.shape, lambda b, k: (0, 0)),
                  pl.BlockSpec((None, FFT_R, FFT_KF, c), lambda b, k: (b, 0, k, 0))],
        out_specs=pl.BlockSpec((None, m2, FFT_KF, c), lambda b, k: (b, 0, k, 0)),
        compiler_params=_cparams(("arbitrary", "arbitrary")),
        name=name,
    )(g, x)


def _lead_out_kernel(n_extra, g_ref, w_ref, *rest):
    g = g_ref[...]
    o_ref = rest[-1]
    wt = jnp.swapaxes(w_ref[...], 0, 1)
    ys = [jnp.dot(g, _unpack_pairs(wt[j]), preferred_element_type=F32) for j in range(FFT_KF)]
    y = jnp.swapaxes(jnp.stack(ys, axis=0), 0, 1)
    if n_extra:
        y = (y + rest[0][...]) * rest[1][...]
    o_ref[...] = y


def _lead_out(g, w, extras, name):
    nb, k2, _, c = w.shape
    blk = lambda r: pl.BlockSpec((None, r, FFT_KF, c), lambda b, k: (b, 0, k, 0))
    return pl.pallas_call(
        functools.partial(_lead_out_kernel, len(extras)),
        out_shape=jax.ShapeDtypeStruct((nb, FFT_R, FFT_R, c), F32),
        grid=(nb, FFT_R // FFT_KF),
        in_specs=[pl.BlockSpec(g.shape, lambda b, k: (0, 0)), blk(k2)] + [blk(FFT_R)] * len(extras),
        out_specs=blk(FFT_R),
        compiler_params=_cparams(("arbitrary", "arbitrary")),
        name=name,
    )(g, w, *extras)


def _interleave(a, b, axis):
    st = np.stack([a, b], axis=axis + 1)
    shape = list(a.shape)
    shape[axis] *= 2
    return st.reshape(shape)


def _hy2_tables():
    L = FFT_R * FFT_R
    n2 = 2 * L
    f1 = np.arange(2 * FFT_R, dtype=np.int64)
    s1 = np.arange(FFT_R, dtype=np.int64)
    th = np.pi * (((2 * f1[:, None] + 1) * s1[None, :]) % (4 * FFT_R)) / (2 * FFT_R)
    ga = _interleave(np.cos(th), -np.sin(th), 0)
    ma = _interleave(np.cos(th).T, -np.sin(th).T, 1) / L
    f2 = np.arange(FFT_R // 2, dtype=np.int64)
    s2 = np.arange(FFT_R, dtype=np.int64)
    idx = ((n2 // FFT_R) * 2 * f2[None, :, None] * s2[None, None, :]
           + (2 * f1[:, None, None] + 1) * s2[None, None, :]) % (2 * n2)
    al = np.pi * idx / n2
    c, s = np.cos(al), np.sin(al)
    nmat = np.concatenate([_interleave(c, s, 2), _interleave(-s, c, 2)], axis=1)
    ct, st = np.transpose(c, (0, 2, 1)), np.transpose(s, (0, 2, 1))
    mmat = _interleave(np.concatenate([ct, -st], axis=2), np.concatenate([st, ct], axis=2), 1)
    bf = lambda a: jnp.asarray(a, F32).astype(BF16)
    return bf(ga), bf(nmat), bf(mmat), bf(ma)


def _hy_mid_kernel(a_ref, n_ref, m_ref, k_ref, e_ref):
    half = FFT_R // 2
    for j in range(FFT_KF):
        t = jnp.dot(n_ref[j], _unpack_pairs(a_ref[j]), preferred_element_type=F32)
        tr, ti = t[:half], t[half:]
        kr, ki = k_ref[j, 0], k_ref[j, 1]
        y = jnp.concatenate([tr * kr - ti * ki, tr * ki + ti * kr], axis=0).astype(BF16)
        e_ref[j] = _pack_pairs(jnp.dot(m_ref[j], y, preferred_element_type=F32))


def _hy_mid(a, nmat, mmat, khat):
    nb = a.shape[0]
    c = a.shape[-1]
    nf1 = 2 * FFT_R
    blk = pl.BlockSpec((None, FFT_KF, FFT_R, c), lambda i, b: (b, i, 0, 0))
    return pl.pallas_call(
        _hy_mid_kernel,
        out_shape=jax.ShapeDtypeStruct(a.shape, jnp.uint32),
        grid=(nf1 // FFT_KF, nb),
        in_specs=[blk,
                  pl.BlockSpec((FFT_KF, FFT_R, 2 * FFT_R), lambda i, b: (i, 0, 0)),
                  pl.BlockSpec((FFT_KF, 2 * FFT_R, FFT_R), lambda i, b: (i, 0, 0)),
                  pl.BlockSpec((FFT_KF, 2, FFT_R // 2, c), lambda i, b: (i, 0, 0, 0))],
        out_specs=blk,
        compiler_params=_cparams(("arbitrary", "arbitrary")),
        name="l0_hyena_mid",
    )(a, nmat, mmat, khat)


def _hy_kfilt_kernel(a_ref, n_ref, nrm_ref, k_ref):
    half = FFT_R // 2
    nrm = nrm_ref[...]
    for j in range(FFT_KF):
        tf = jnp.dot(n_ref[j], _unpack_pairs(a_ref[0, j]), preferred_element_type=F32)
        tb = jnp.dot(n_ref[j], _unpack_pairs(a_ref[1, j]), preferred_element_type=F32)
        k_ref[j, 0] = (tf[:half] + tb[:half]) / nrm
        k_ref[j, 1] = (tf[half:] - tb[half:]) / nrm


def _hy_kfilt(a, nmat, nrm):
    c = a.shape[-1]
    nf1 = 2 * FFT_R
    return pl.pallas_call(
        _hy_kfilt_kernel,
        out_shape=jax.ShapeDtypeStruct((nf1, 2, FFT_R // 2, c), F32),
        grid=(nf1 // FFT_KF,),
        in_specs=[pl.BlockSpec((2, FFT_KF, FFT_R, c), lambda i: (0, i, 0, 0)),
                  pl.BlockSpec((FFT_KF, FFT_R, 2 * FFT_R), lambda i: (i, 0, 0)),
                  _const_spec(nrm.shape)],
        out_specs=pl.BlockSpec((FFT_KF, 2, FFT_R // 2, c), lambda i: (i, 0, 0, 0)),
        compiler_params=_cparams(("arbitrary",)),
        name="l0_hyena_kfilt",
    )(a, nmat, nrm)


def _hyena_long(u, e, x0, hfilt, nrm, nb):
    L = FFT_R * FFT_R
    c = u.shape[-1]
    v4 = lambda a, n: a.reshape(n, FFT_R, FFT_R, c)
    ga, nmat, mmat, ma = _hy2_tables()
    khat = _hy_kfilt(_lead_in(ga, v4(hfilt, 2), "l0_hyena_fwd_a"), nmat, nrm)
    ee = _hy_mid(_lead_in(ga, v4(u, nb), "l0_hyena_fwd_a"), nmat, mmat, khat)
    y = _lead_out(ma, ee, [v4(e, nb), v4(x0, nb)], "l0_hyena_inv_a")
    return y.reshape(nb * L, c)


def _fn2_tables():
    L = FFT_R * FFT_R
    r = np.arange(FFT_R, dtype=np.int64)
    idx = (FFT_R * r[None, :, None] * r[None, None, :] + r[None, :, None] * r[:, None, None]) % L
    gm = 2.0 * np.pi * idx / L
    c, s = np.cos(gm), np.sin(gm)
    g1 = _interleave(np.concatenate([c, -s], axis=2), np.concatenate([-s, -c], axis=2), 1)
    dl = 2.0 * np.pi * ((r[:, None] * r[None, :]) % FFT_R) / FFT_R
    g2 = _interleave(np.cos(dl), np.sin(dl), 1) / math.sqrt(L * FN_GROUP_CH)
    bf = lambda a: jnp.asarray(a, F32).astype(BF16)
    return bf(g1), bf(g2)


def _fnet_s1_kernel(x_ref, m_ref, cs_ref, g1_ref, o_ref, zc_ref, zs_ref):
    xs = jnp.swapaxes(x_ref[...], 0, 1).reshape(FFT_KF * FFT_R, D_MODEL)
    m = m_ref[...]
    h = (_ln_plain(xs) * (1.0 + m[:, D_MODEL:2 * D_MODEL]) + m[:, 0:D_MODEL]).astype(BF16)
    cs = cs_ref[...]
    for g in range(D_MODEL // FN_GROUP_CH):
        a = g * FN_GROUP_CH
        z = jnp.dot(h[:, a:a + FN_GROUP_CH], cs, preferred_element_type=F32)
        zc_ref[:, a:a + FN_GROUP_CH] = z[:, :FN_GROUP_CH].astype(BF16)
        zs_ref[:, a:a + FN_GROUP_CH] = z[:, FN_GROUP_CH:].astype(BF16)
    for j in range(FFT_KF):
        r0 = j * FFT_R
        s = jnp.concatenate([zc_ref[r0:r0 + FFT_R, :], zs_ref[r0:r0 + FFT_R, :]], axis=0)
        o_ref[j] = _pack_pairs(jnp.dot(g1_ref[j], s, preferred_element_type=F32))


def _fnet_long(x, mods, mod_base, nb):
    L = FFT_R * FFT_R
    d = D_MODEL
    g1, g2 = _fn2_tables()
    cs = _group_dft_table()
    bb = pl.pallas_call(
        _fnet_s1_kernel,
        out_shape=jax.ShapeDtypeStruct((nb, FFT_R, FFT_R, d), jnp.uint32),
        grid=(nb, FFT_R // FFT_KF),
        in_specs=[pl.BlockSpec((None, FFT_R, FFT_KF, d), lambda b, k: (b, 0, k, 0)),
                  pl.BlockSpec((None, 1, 6 * d), lambda b, k: (mod_base + b, 0, 0)),
                  pl.BlockSpec(cs.shape, lambda b, k: (0, 0)),
                  pl.BlockSpec((FFT_KF, 2 * FFT_R, 2 * FFT_R), lambda b, k: (k, 0, 0))],
        out_specs=pl.BlockSpec((None, FFT_KF, FFT_R, d), lambda b, k: (b, k, 0, 0)),
        scratch_shapes=[pltpu.VMEM((FFT_KF * FFT_R, d), BF16), pltpu.VMEM((FFT_KF * FFT_R, d), BF16)],
        compiler_params=_cparams(("arbitrary", "arbitrary")),
        name="l1_fnet_stage1",
    )(x.reshape(nb, FFT_R, FFT_R, d), mods, cs, g1)
    y = _lead_out(g2, bb, [], "l1_fnet_stage2")
    return y.reshape(nb * L, d)


def _group_dft_table():
    g = FN_GROUP_CH
    jk = (np.arange(g, dtype=np.int64)[:, None] * np.arange(g, dtype=np.int64)[None, :]) % g
    ang = 2.0 * np.pi * jk / g
    return jnp.asarray(np.concatenate([np.cos(ang), np.sin(ang)], axis=1), F32).astype(BF16)


def _fnet_front_kernel(x_ref, m_ref, cs_ref, zc_ref, zs_ref):
    m = m_ref[...]
    h = (_ln_plain(x_ref[...]) * (1.0 + m[:, D_MODEL:2 * D_MODEL]) + m[:, 0:D_MODEL]).astype(BF16)
    cs = cs_ref[...]
    for g in range(D_MODEL // FN_GROUP_CH):
        a = g * FN_GROUP_CH
        z = jnp.dot(h[:, a:a + FN_GROUP_CH], cs, preferred_element_type=F32)
        zc_ref[:, a:a + FN_GROUP_CH] = z[:, :FN_GROUP_CH].astype(BF16)
        zs_ref[:, a:a + FN_GROUP_CH] = z[:, FN_GROUP_CH:].astype(BF16)


def _fnet_front(x, mods, mod_base, tiles_per_mod):
    t = x.shape[0]
    tm = ROW_TILE
    cs = _group_dft_table()
    return pl.pallas_call(
        _fnet_front_kernel,
        out_shape=(jax.ShapeDtypeStruct((t, D_MODEL), BF16),) * 2,
        grid=(t // tm,),
        in_specs=[pl.BlockSpec((tm, D_MODEL), lambda i: (i, 0)),
                  _mod_spec(mod_base, tiles_per_mod),
                  _const_spec(cs.shape)],
        out_specs=(pl.BlockSpec((tm, D_MODEL), lambda i: (i, 0)),) * 2,
        compiler_params=_cparams(("arbitrary",)),
        name="l1_fnet_front",
    )(x, mods, cs)


def _post_kernel(n_a, *refs):
    x_ref, m_ref = refs[0], refs[1]
    a_refs = refs[2:2 + n_a]
    wo_refs = refs[2 + n_a:2 + 2 * n_a]
    g1_ref, b1_ref, w1_ref, w2_ref, g2_ref, b2_ref, o_ref = refs[2 + 2 * n_a:]
    m = m_ref[...]
    d = D_MODEL
    out = _bdot(a_refs[0][...], wo_refs[0][...])
    for a_ref, wo_ref in zip(a_refs[1:], wo_refs[1:]):
        out += _bdot(a_ref[...], wo_ref[...])
    x1 = _ln_plain(ALPHA * x_ref[...] + m[:, 2 * d:3 * d] * out) * g1_ref[...] + b1_ref[...]
    h = (_ln_plain(x1) * (1.0 + m[:, 4 * d:5 * d]) + m[:, 3 * d:4 * d]).astype(BF16)
    acc = jnp.zeros_like(x1)
    for c in range(D_FF // d):
        hc = jnp.maximum(jnp.dot(h, w1_ref[:, c * d:(c + 1) * d], preferred_element_type=F32), 0.0)
        acc += jnp.dot((hc * hc).astype(BF16), w2_ref[c * d:(c + 1) * d, :], preferred_element_type=F32)
    o_ref[...] = _ln_plain(ALPHA * x1 + m[:, 5 * d:6 * d] * acc) * g2_ref[...] + b2_ref[...]


def _post(x, mods, mod_base, tiles_per_mod, a_list, wo_list, g1, b1, w1, w2, g2, b2):
    t = x.shape[0]
    tm = POST_TILE
    row = lambda c: pl.BlockSpec((tm, c), lambda i: (i, 0))
    once = lambda v: pl.BlockSpec(v.shape, lambda i: (0,) * v.ndim, pipeline_mode=pl.Buffered(1))
    in_specs = ([row(D_MODEL), _mod_spec(mod_base, tiles_per_mod * ROW_TILE // tm)]
                + [row(a.shape[1]) for a in a_list]
                + [once(w) for w in wo_list]
                + [once(v) for v in (g1, b1, w1, w2, g2, b2)])
    return pl.pallas_call(
        functools.partial(_post_kernel, len(a_list)),
        out_shape=jax.ShapeDtypeStruct((t, D_MODEL), F32),
        grid=(t // tm,),
        in_specs=in_specs,
        out_specs=row(D_MODEL),
        compiler_params=_cparams(("arbitrary",)),
        name="post_mlp",
    )(x, mods, *a_list, *wo_list, g1, b1, w1, w2, g2, b2)


def _rot_cols(w):
    parts = []
    for seg in range(2):
        o = seg * 32
        parts += [-w[:, o + 16:o + 32], w[:, o:o + 16]]
    return jnp.concatenate(parts, axis=1)


def _pad_cols(w, n):
    return jnp.pad(w, ((0, 0), (0, n - w.shape[1])))


def _rope_tables(L):
    rows = L // GRID_W
    row = np.repeat(np.arange(rows, dtype=np.float64), GRID_W)
    col = np.tile(np.arange(GRID_W, dtype=np.float64), rows)
    half = QK_ROPE // 2
    inv = 1.0 / (ROPE_THETA ** (np.arange(0, half, 2, dtype=np.float64) / half))
    ar = row[:, None] * inv[None, :]
    ac = col[:, None] * inv[None, :]
    ang = np.concatenate([ar, ar, ac, ac], axis=1)
    cos = np.concatenate([np.cos(ang), np.ones_like(ang)], axis=1)
    sin = np.concatenate([np.sin(ang), np.zeros_like(ang)], axis=1)
    return jnp.asarray(cos, F32), jnp.asarray(sin, F32)


def kernel(x_prompt, x_sample, cache_l0_ckv, cache_l0_krope, c, c_ctx, l0_ada_w, l0_ada_b, l0_w_in, l0_conv_w, l0_conv_b, l0_hf_w1, l0_hf_b1, l0_hf_freq, l0_hf_w2, l0_hf_b2, l0_hf_w3, l0_hf_skip, l0_q_norm, l0_q_up, l0_kv_norm, l0_kv_up, l0_w_out, l0_ln1_g, l0_ln1_b, l0_mlp_w1, l0_mlp_w2, l0_ln2_g, l0_ln2_b, l1_ada_w, l1_ada_b, l1_w_out, l1_ln1_g, l1_ln1_b, l1_mlp_w1, l1_mlp_w2, l1_ln2_g, l1_ln2_b):
    nbc, lc, d = x_prompt.shape
    nbs, ls, _ = x_sample.shape
    past = cache_l0_ckv.shape[1]
    tm = ROW_TILE
    row1 = lambda v: v.reshape(1, -1)

    cond8 = jnp.concatenate([c_ctx[None, :], c, jnp.zeros((8 - 1 - nbs, d), F32)], axis=0)
    mods0 = _modulation(cond8, l0_ada_w, l0_ada_b)
    mods1 = _modulation(cond8, l1_ada_w, l1_ada_b)

    kpe_w = l0_w_in[:, 1920:1984]
    win = jnp.concatenate([l0_w_in[:, :1920], _pad_cols(kpe_w, LANE), _pad_cols(_rot_cols(kpe_w), LANE)],
                          axis=1).astype(BF16)
    dh = QK_NOPE + QK_ROPE
    q_nope = [l0_q_up[:, h * dh:h * dh + QK_NOPE] for h in range(MLA_HEADS)]
    q_pe = [l0_q_up[:, h * dh + QK_NOPE:(h + 1) * dh] for h in range(MLA_HEADS)]
    qup = jnp.concatenate(q_nope + [_pad_cols(w, LANE) for w in q_pe]
                          + [_pad_cols(_rot_cols(w), LANE) for w in q_pe], axis=1).astype(BF16)
    kvup = l0_kv_up.astype(BF16)
    front_w = (win, row1(l0_q_norm), qup, row1(l0_kv_norm), kvup, l0_conv_w, row1(l0_conv_b), row1(l0_hf_skip))
    w1p = jnp.pad(l0_hf_w1, ((0, LANE - l0_hf_w1.shape[0]), (0, 0)))
    filt_w = (w1p, row1(l0_hf_b1), row1(l0_hf_freq), l0_hf_w2, row1(l0_hf_b2), l0_hf_w3.astype(BF16))
    wo0 = l0_w_out.astype(BF16)

    xc = x_prompt.reshape(nbc * lc, d)
    xs = x_sample.reshape(nbs * ls, d)
    groups = (
        dict(x=xc, nb=nbc, L=lc, mod_base=0, tiles_per_mod=nbc * lc // tm, dft_nb=4, tq=lc),
        dict(x=xs, nb=nbs, L=ls, mod_base=1, tiles_per_mod=ls // tm, dft_nb=nbs, tq=512),
    )
    ones_tab = (jnp.concatenate([jnp.ones((tm, LANE), F32)], axis=0), jnp.zeros((tm, LANE), F32))

    outs = []
    ctx_ckv = ctx_krope = None
    for gi, g in enumerate(groups):
        nb, L = g["nb"], g["L"]
        tiles_per_seq = L // tm
        latent = gi == 1
        cos, sin = _rope_tables(L) if latent else ones_tab
        two_stage = L == FFT_R * FFT_R
        io_dtype = F32 if two_stage else BF16
        u, e, x0, q, k, vt, kvn, kpe = _front(g["x"], mods0, g["mod_base"], g["tiles_per_mod"], front_w,
                                              cos, sin, latent, tiles_per_seq, io_dtype)
        if latent:
            extra = _cache_kv(cache_l0_ckv.reshape(nbs * past, KV_LORA),
                              _pad_cols(cache_l0_krope.reshape(nbs * past, QK_ROPE), LANE), kvup)
        else:
            extra = None
            ctx_ckv = kvn.reshape(nb, L, KV_LORA)
            ctx_krope = kpe.reshape(nb, L, QK_ROPE)
        y_mla = _attention(q, k, vt, extra, nb, L, g["tq"])

        hfilt, hnorm = _filters(L, io_dtype, *filt_w)
        if two_stage:
            y_hy = _hyena_long(u, e, x0, hfilt, hnorm, nb)
        else:
            kre, kim = _dft("filt", "hy_fwd", [hfilt], [hnorm], 2)
            sh = (nb, L, HY_CH)
            yre, yim = _dft("fwdk", "hy_fwd", [u.reshape(sh)], [kre, kim], g["dft_nb"])
            y_hy = _dft("inv", "hy_inv", [yre, yim], [e.reshape(sh), x0.reshape(sh)], g["dft_nb"])
            y_hy = y_hy.reshape(nb * L, HY_CH)

        x1 = _post(g["x"], mods0, g["mod_base"], g["tiles_per_mod"], [y_hy, y_mla],
                   [wo0[:HY_CH], wo0[HY_CH:]], row1(l0_ln1_g), row1(l0_ln1_b),
                   l0_mlp_w1.astype(BF16), l0_mlp_w2.astype(BF16), row1(l0_ln2_g), row1(l0_ln2_b))

        if L == FFT_R * FFT_R:
            yf = _fnet_long(x1, mods1, g["mod_base"], nb)
        else:
            zc, zs = _fnet_front(x1, mods1, g["mod_base"], g["tiles_per_mod"])
            sh = (nb, L, d)
            yf = _dft("fnet", "fnet", [zc.reshape(sh), zs.reshape(sh)], [], min(g["dft_nb"], 2))
            yf = yf.reshape(nb * L, d)
        x2 = _post(x1, mods1, g["mod_base"], g["tiles_per_mod"], [yf],
                   [l1_w_out.astype(BF16)], row1(l1_ln1_g), row1(l1_ln1_b),
                   l1_mlp_w1.astype(BF16), l1_mlp_w2.astype(BF16), row1(l1_ln2_g), row1(l1_ln2_b))
        outs.append(x2.reshape(nb, L, d))

    return (outs[0], outs[1], ctx_ckv, ctx_krope)
```

```python
import functools
import math

import numpy as np
import jax
import jax.numpy as jnp
from jax import lax
from jax.experimental import pallas as pl
from jax.experimental.pallas import tpu as pltpu

F32 = jnp.float32
BF16 = jnp.bfloat16
HI = lax.Precision.HIGHEST

D_MODEL = 1024
DEPTH = 2
GRID_W = 64
HY_CH = 512
FILT_BANDS = 16
FILT_ORDER = 64
FAST_DECAY_PCT = 0.3
SLOW_DECAY_PCT = 1.5
DECAY_TARGET = 1e-2
MAX_DECAY = math.log(DECAY_TARGET) / FAST_DECAY_PCT
MIN_DECAY = math.log(DECAY_TARGET) / SLOW_DECAY_PCT
MLA_HEADS = 4
QK_NOPE = 128
QK_ROPE = 64
V_HEAD = 128
Q_LORA = 256
KV_LORA = 128
ROPE_THETA = 10000.0
FN_GROUP_CH = 128
D_FF = 4096
ALPHA = (2 * DEPTH) ** 0.25
LN_EPS = 1e-5
RMS_EPS = 1e-6

LANE = 128
ROW_TILE = 256
POST_TILE = 512
POST_SPLIT = 2
QK_PAD = 256
VT_ROWS = V_HEAD + 16
LOG2E = 1.4426950408889634
VMEM_LIMIT = 56 * 1024 * 1024


def _cparams(sem):
    return pltpu.CompilerParams(dimension_semantics=sem, vmem_limit_bytes=VMEM_LIMIT)


def _ln_plain(x):
    mu = jnp.mean(x, axis=-1, keepdims=True)
    xc = x - mu
    var = jnp.mean(xc * xc, axis=-1, keepdims=True)
    return xc * lax.rsqrt(var + LN_EPS)


def _rms(x, g):
    return x * lax.rsqrt(jnp.mean(x * x, axis=-1, keepdims=True) + RMS_EPS) * g


def _bdot(a, b):
    return jnp.dot(a.astype(BF16), b, preferred_element_type=F32)


def _vt_rows(v):
    ones = jnp.ones((VT_ROWS - V_HEAD, v.shape[0]), BF16)
    return jnp.concatenate([jnp.transpose(v).astype(BF16), ones], axis=0)


def _mod_kernel(c_ref, w_ref, b_ref, o_ref):
    c = c_ref[...]
    s = c / (1.0 + jnp.exp(-c))
    o_ref[...] = jnp.dot(s, w_ref[...], precision=HI, preferred_element_type=F32) + b_ref[...]


def _modulation(cond8, w, b):
    n = w.shape[1]
    tn = 1536
    out = pl.pallas_call(
        _mod_kernel,
        out_shape=jax.ShapeDtypeStruct((8, n), F32),
        grid=(n // tn,),
        in_specs=[pl.BlockSpec((8, D_MODEL), lambda j: (0, 0)),
                  pl.BlockSpec((D_MODEL, tn), lambda j: (0, j)),
                  pl.BlockSpec((1, tn), lambda j: (0, j))],
        out_specs=pl.BlockSpec((8, tn), lambda j: (0, j)),
        compiler_params=_cparams(("arbitrary",)),
        name="modulation",
    )(cond8, w, b.reshape(1, n))
    return out.reshape(8, 1, n)


def _mod_spec(mod_base, tiles_per_mod):
    return pl.BlockSpec((None, 1, 6 * D_MODEL), lambda i: (mod_base + i // tiles_per_mod, 0, 0))


def _const_spec(shape):
    nd = len(shape)
    return pl.BlockSpec(shape, lambda i: (0,) * nd)


HALO = 8


def _front_kernel(tiles_per_seq, x_ref, xp_ref, xn_ref, m_ref, win_ref, qn_ref, qup_ref, kvn_ref, kvup_ref,
                  cos_ref, sin_ref, cw_ref, cb_ref, skip_ref,
                  u_ref, e_ref, x0_ref, q_ref, k_ref, v_ref, kvn_out_ref, kpe_ref):
    i = pl.program_id(0)
    m = m_ref[...]
    tm = x_ref.shape[0]
    nh = 3 * HY_CH
    xe = jnp.concatenate([xp_ref[...], x_ref[...], xn_ref[...]], axis=0)
    he = _ln_plain(xe) * (1.0 + m[:, D_MODEL:2 * D_MODEL]) + m[:, 0:D_MODEL]

    zh = _bdot(he, win_ref[:, :nh])
    pos = i % tiles_per_seq
    rows = lax.broadcasted_iota(jnp.int32, (tm + 2 * HALO, 1), 0)
    inside = jnp.logical_and(jnp.logical_or(rows >= HALO, pos != 0),
                             jnp.logical_or(rows < tm + HALO, pos != tiles_per_seq - 1))
    zh = jnp.where(inside, zh, 0.0)
    cw = cw_ref[...]
    pz = (pltpu.roll(zh, 1, 0) * cw[0:1, :] + zh * cw[1:2, :]
          + pltpu.roll(zh, tm + 2 * HALO - 1, 0) * cw[2:3, :])[HALO:HALO + tm] + cb_ref[...]
    u = pz[:, 2 * HY_CH:] * pz[:, HY_CH:2 * HY_CH]
    u_ref[...] = u.astype(u_ref.dtype)
    e_ref[...] = u * skip_ref[...]
    x0_ref[...] = pz[:, :HY_CH]

    z = _bdot(he[HALO:HALO + tm], win_ref[:, nh:])
    q_c = z[:, 0:256]
    kv_c = z[:, 256:384]
    cos = cos_ref[...]
    sin = sin_ref[...]
    kpe = z[:, 384:512] * cos + z[:, 512:640] * sin
    kpe_ref[...] = kpe[:, :QK_ROPE]
    kpe_b = kpe.astype(BF16)
    q = _bdot(_rms(q_c, qn_ref[...]), qup_ref[...]) * (LOG2E / math.sqrt(QK_NOPE + QK_ROPE))
    kvn = _rms(kv_c, kvn_ref[...])
    kvn_out_ref[...] = kvn
    kv = _bdot(kvn, kvup_ref[...])
    for hd in range(MLA_HEADS):
        a = hd * LANE
        q_pe = (q[:, 512 + a:512 + a + LANE] * cos + q[:, 1024 + a:1024 + a + LANE] * sin).astype(BF16)
        q_ref[hd] = jnp.concatenate([q[:, a:a + LANE].astype(BF16), q_pe], axis=-1)
        k_ref[hd] = jnp.concatenate([kv[:, 2 * a:2 * a + LANE].astype(BF16), kpe_b], axis=-1)
        v_ref[hd] = _vt_rows(kv[:, 2 * a + LANE:2 * a + 2 * LANE])


def _front(x, mods, mod_base, tiles_per_mod, w, cos, sin, rope, tiles_per_seq, u_dtype):
    t = x.shape[0]
    tm = ROW_TILE
    win, qn, qup, kvn, kvup, conv_w, conv_b, skip = w
    if rope:
        tab_spec = pl.BlockSpec((tm, LANE), lambda i: (i % tiles_per_seq, 0))
    else:
        tab_spec = pl.BlockSpec((tm, LANE), lambda i: (0, 0))
    r8 = tm // HALO
    n8 = t // HALO
    hy_out = lambda dt: jax.ShapeDtypeStruct((t, HY_CH), dt)
    hy_spec = pl.BlockSpec((tm, HY_CH), lambda i: (i, 0))
    return pl.pallas_call(
        functools.partial(_front_kernel, tiles_per_seq),
        out_shape=(hy_out(u_dtype), hy_out(F32), hy_out(F32),
                   jax.ShapeDtypeStruct((MLA_HEADS, t, QK_PAD), BF16),
                   jax.ShapeDtypeStruct((MLA_HEADS, t, QK_PAD), BF16),
                   jax.ShapeDtypeStruct((MLA_HEADS, VT_ROWS, t), BF16),
                   jax.ShapeDtypeStruct((t, KV_LORA), F32),
                   jax.ShapeDtypeStruct((t, QK_ROPE), F32)),
        grid=(t // tm,),
        in_specs=[pl.BlockSpec((tm, D_MODEL), lambda i: (i, 0)),
                  pl.BlockSpec((HALO, D_MODEL), lambda i: (jnp.maximum(i * r8 - 1, 0), 0)),
                  pl.BlockSpec((HALO, D_MODEL), lambda i: (jnp.minimum((i + 1) * r8, n8 - 1), 0)),
                  _mod_spec(mod_base, tiles_per_mod),
                  _const_spec(win.shape), _const_spec(qn.shape), _const_spec(qup.shape),
                  _const_spec(kvn.shape), _const_spec(kvup.shape),
                  tab_spec, tab_spec,
                  _const_spec(conv_w.shape), _const_spec(conv_b.shape), _const_spec(skip.shape)],
        out_specs=(hy_spec, hy_spec, hy_spec,
                   pl.BlockSpec((MLA_HEADS, tm, QK_PAD), lambda i: (0, i, 0)),
                   pl.BlockSpec((MLA_HEADS, tm, QK_PAD), lambda i: (0, i, 0)),
                   pl.BlockSpec((MLA_HEADS, VT_ROWS, tm), lambda i: (0, 0, i)),
                   pl.BlockSpec((tm, KV_LORA), lambda i: (i, 0)),
                   pl.BlockSpec((tm, QK_ROPE), lambda i: (i, 0))),
        compiler_params=_cparams(("arbitrary",)),
        name="l0_front",
    )(x, x, x, mods, win, qn, qup, kvn, kvup, cos, sin, conv_w, conv_b, skip)


def _cache_kv_kernel(ckv_ref, kr_ref, kvup_ref, k_ref, v_ref):
    kv = _bdot(ckv_ref[...], kvup_ref[...])
    kr = kr_ref[...].astype(BF16)
    for hd in range(MLA_HEADS):
        a = 2 * hd * LANE
        k_ref[hd] = jnp.concatenate([kv[:, a:a + LANE].astype(BF16), kr], axis=-1)
        v_ref[hd] = _vt_rows(kv[:, a + LANE:a + 2 * LANE])


def _cache_kv(ckv, krope_pad, kvup):
    t = ckv.shape[0]
    return pl.pallas_call(
        _cache_kv_kernel,
        out_shape=(jax.ShapeDtypeStruct((MLA_HEADS, t, QK_PAD), BF16),
                   jax.ShapeDtypeStruct((MLA_HEADS, VT_ROWS, t), BF16)),
        name="l0_cache_kv",
    )(ckv, krope_pad, kvup)


def _col_reduce(x, op):
    rows, n = x.shape
    for g in (32, 8):
        if rows % (8 * g) == 0 and rows > 8 * g:
            x = op(x.reshape(rows // (8 * g), 8 * g, n), axis=0)
            rows = 8 * g
    return op(x, axis=0, keepdims=True)


def _attn_kernel(n_kv, q_ref, *refs):
    k_refs, vt_refs = refs[:n_kv], refs[n_kv:2 * n_kv]
    o_ref, s_even, s_odd = refs[2 * n_kv:]
    i = pl.program_id(0)

    @pl.when(i == 0)
    def _():
        s_odd[...] = jnp.zeros_like(s_odd)

    def step(s_write, s_read):
        q = q_ref[...]
        nt = (((1,), (1,)), ((), ()))
        r0 = 0
        for k_ref in k_refs:
            lk = k_ref.shape[0]
            s_write[r0:r0 + lk, :] = lax.dot_general(k_ref[...], q, nt, preferred_element_type=F32)
            r0 += lk
        s = s_read[...]
        m = _col_reduce(s, jnp.max)
        pb = jnp.exp2(s - m).astype(BF16)
        acc = None
        r0 = 0
        for vt_ref in vt_refs:
            lk = vt_ref.shape[1]
            pv = jnp.dot(vt_ref[...], pb[r0:r0 + lk, :], preferred_element_type=F32)
            acc = pv if acc is None else acc + pv
            r0 += lk
        o_ref[...] = jnp.transpose(acc[:V_HEAD] / acc[V_HEAD:V_HEAD + 1]).astype(o_ref.dtype)

    pl.when(i % 2 == 0)(lambda: step(s_even, s_odd))
    pl.when(i % 2 == 1)(lambda: step(s_odd, s_even))


def _attention(q, k, vt, extra, nb, lq, tq):
    nq = lq // tq
    n_tiles = nb * MLA_HEADS * nq

    def where(t):
        bh = t // nq
        return bh // MLA_HEADS, bh % MLA_HEADS, t % nq

    def score_side(fn):
        return lambda i: fn(*where(jnp.minimum(i, n_tiles - 1)))

    def value_side(fn):
        return lambda i: fn(*where(jnp.maximum(i - 1, 0)))

    ks, vts = [k], [vt]
    if extra is not None:
        ks.append(extra[0])
        vts.append(extra[1])
    in_specs = [pl.BlockSpec((None, tq, QK_PAD), score_side(lambda b, h, j: (h, b * nq + j, 0)))]
    in_specs += [pl.BlockSpec((None, a.shape[1] // nb, QK_PAD), score_side(lambda b, h, j: (h, b, 0))) for a in ks]
    in_specs += [pl.BlockSpec((None, VT_ROWS, a.shape[2] // nb), value_side(lambda b, h, j: (h, 0, b))) for a in vts]
    lk_total = sum(a.shape[1] // nb for a in ks)
    return pl.pallas_call(
        functools.partial(_attn_kernel, len(ks)),
        out_shape=jax.ShapeDtypeStruct((nb * lq, MLA_HEADS * V_HEAD), BF16),
        grid=(n_tiles + 1,),
        in_specs=in_specs,
        out_specs=pl.BlockSpec((tq, V_HEAD), value_side(lambda b, h, j: (b * nq + j, h))),
        scratch_shapes=[pltpu.VMEM((lk_total, tq), F32), pltpu.VMEM((lk_total, tq), F32)],
        compiler_params=_cparams(("arbitrary",)),
        name="l0_attention",
    )(q, *ks, *vts)


def _filter_kernel(z_ref, w1_ref, b1_ref, fr_ref, w2_ref, b2_ref, w3_ref, dl_ref, h_ref, norm_ref):
    i = pl.program_id(0)
    z = z_ref[...]
    tl = z.shape[0]
    fr = fr_ref[...]
    h = jnp.sin(fr * (jnp.dot(z, w1_ref[...], precision=HI, preferred_element_type=F32) + b1_ref[...]))
    h = jnp.sin(fr * (jnp.dot(h, w2_ref[...], precision=HI, preferred_element_type=F32) + b2_ref[...]))
    h = _bdot(h, w3_ref[...])
    decay = jnp.exp(-(z[:, 0:1] * dl_ref[...]))
    hf = h[:, :HY_CH] * decay
    hb = h[:, HY_CH:] * decay
    part = jnp.sum(jnp.abs(hf) + jnp.abs(hb), axis=0, keepdims=True)

    @pl.when(i == 0)
    def _():
        norm_ref[...] = part

    @pl.when(i > 0)
    def _():
        norm_ref[...] += part

    rows = lax.broadcasted_iota(jnp.int32, hb.shape, 0) + i * tl
    h_ref[0] = hf.astype(h_ref.dtype)
    h_ref[1] = jnp.where(rows == 0, 0.0, hb).astype(h_ref.dtype)


def _filter_embedding(L):
    t = np.linspace(0.0, 1.0, L)[:, None]
    w_ang = 2.0 * np.pi * np.arange(L) / L
    bands = np.linspace(1e-4, FILT_BANDS - 1, FILT_BANDS)
    ang = w_ang[:, None] * bands[None, :]
    z = np.zeros((L, LANE), np.float64)
    z[:, 0:1] = t
    z[:, 1:1 + FILT_BANDS] = np.cos(ang)
    z[:, 1 + FILT_BANDS:1 + 2 * FILT_BANDS] = -np.sin(ang)
    return jnp.asarray(z, F32)


def _filters(L, h_dtype, w1p, b1, fr, w2, b2, w3):
    tl = min(L, 512)
    z = _filter_embedding(L)
    deltas = jnp.asarray(np.abs(np.linspace(MIN_DECAY, MAX_DECAY, HY_CH))[None, :], F32)
    return pl.pallas_call(
        _filter_kernel,
        out_shape=(jax.ShapeDtypeStruct((2, L, HY_CH), h_dtype), jax.ShapeDtypeStruct((1, HY_CH), F32)),
        grid=(L // tl,),
        in_specs=[pl.BlockSpec((tl, LANE), lambda i: (i, 0)),
                  _const_spec(w1p.shape), _const_spec(b1.shape), _const_spec(fr.shape),
                  _const_spec(w2.shape), _const_spec(b2.shape), _const_spec(w3.shape),
                  _const_spec(deltas.shape)],
        out_specs=(pl.BlockSpec((2, tl, HY_CH), lambda i: (0, i, 0)),
                   pl.BlockSpec((1, HY_CH), lambda i: (0, 0))),
        compiler_params=_cparams(("arbitrary",)),
        name="l0_hyena_filters",
    )(z, w1p, b1, fr, w2, b2, w3, deltas)


def _dft_tables(kind, L, ti):
    ni = L // ti
    i = np.arange(ti, dtype=np.int64)[:, None]
    big = (np.arange(ni, dtype=np.int64) * ti)[:, None]
    c = np.arange(L, dtype=np.int64)[None, :]
    if kind == "hy_fwd":
        period = 4 * L
        base_idx = (2 * i + 1) * c
        r_idx = 2 * big * c
        scale = 1.0
    elif kind == "hy_inv":
        period = 4 * L
        base_idx = (2 * c + 1) * i
        r_idx = (2 * c + 1) * big
        scale = 1.0 / L
    else:
        period = L
        base_idx = i * c
        r_idx = big * c
        scale = 1.0 / math.sqrt(L * FN_GROUP_CH)
    ab = 2.0 * np.pi * (base_idx % period) / period
    ar = 2.0 * np.pi * (r_idx % period) / period
    return (jnp.asarray(np.cos(ab), F32), jnp.asarray(np.sin(ab), F32),
            jnp.asarray(scale * np.cos(ar), F32).reshape(ni, 1, L),
            jnp.asarray(scale * np.sin(ar), F32).reshape(ni, 1, L))


def _dft_kernel(mode, nb, n_x, *refs):
    bc_ref, bs_ref, rc_ref, rs_ref = refs[:4]
    x_refs = refs[4:4 + n_x]
    rest = refs[4 + n_x:]
    p_ref, q_ref = rest[-2], rest[-1]
    j = pl.program_id(2)
    nj = pl.num_programs(2)
    tj = x_refs[0].shape[1]
    if bc_ref.shape[1] == tj:
        bc, bs, rc, rs = bc_ref[...], bs_ref[...], rc_ref[...], rs_ref[...]
    else:
        off = pl.multiple_of(j * tj, tj)
        bc, bs = bc_ref[:, pl.ds(off, tj)], bs_ref[:, pl.ds(off, tj)]
        rc, rs = rc_ref[:, pl.ds(off, tj)], rs_ref[:, pl.ds(off, tj)]
    tc = (bc * rc - bs * rs).astype(BF16)
    ts = (bs * rc + bc * rs).astype(BF16)
    x1_ref = x_refs[0]
    x2_ref = x_refs[-1]

    pq = [(jnp.dot(tc, x1_ref[b], preferred_element_type=F32),
           jnp.dot(ts, x2_ref[b], preferred_element_type=F32)) for b in range(nb)]

    @pl.when(j == 0)
    def _():
        for b in range(nb):
            p_ref[b] = pq[b][0]
            q_ref[b] = pq[b][1]

    @pl.when(j > 0)
    def _():
        for b in range(nb):
            p_ref[b] += pq[b][0]
            q_ref[b] += pq[b][1]

    @pl.when(j == nj - 1)
    def _():
        if mode == "filt":
            nrm = rest[0][...]
            kre_ref, kim_ref = rest[1], rest[2]
            kre_ref[...] = (p_ref[0] + p_ref[1]) / nrm
            kim_ref[...] = (q_ref[1] - q_ref[0]) / nrm
        elif mode == "fwdk":
            kre, kim = rest[0][...], rest[1][...]
            yre_ref, yim_ref = rest[2], rest[3]
            for b in range(nb):
                pp, qq = p_ref[b], q_ref[b]
                yre_ref[b] = (pp * kre + qq * kim).astype(BF16)
                yim_ref[b] = (pp * kim - qq * kre).astype(BF16)
        elif mode == "inv":
            e_ref, x0_ref, o_ref = rest[0], rest[1], rest[2]
            for b in range(nb):
                o_ref[b] = ((p_ref[b] - q_ref[b] + e_ref[b]) * x0_ref[b]).astype(BF16)
        else:
            o_ref = rest[0]
            for b in range(nb):
                o_ref[b] = (p_ref[b] - q_ref[b]).astype(BF16)


def _dft(mode, kind, xs, extras, nb):
    B, L, C = xs[0].shape
    ti = min(L, 256)
    tj = min(L, 512)
    bc, bs, rc, rs = _dft_tables(kind, L, ti)
    grid = (B // nb, L // ti, L // tj)
    x_spec = pl.BlockSpec((nb, tj, C), lambda g, i, j: (g, j, 0))
    row_spec = lambda c, dt=None: pl.BlockSpec((nb, ti, c), lambda g, i, j: (g, i, 0))
    in_specs = [pl.BlockSpec((ti, L), lambda g, i, j: (0, 0)),
                pl.BlockSpec((ti, L), lambda g, i, j: (0, 0)),
                pl.BlockSpec((None, 1, L), lambda g, i, j: (i, 0, 0)),
                pl.BlockSpec((None, 1, L), lambda g, i, j: (i, 0, 0))] + [x_spec] * len(xs)
    if mode == "filt":
        in_specs += [pl.BlockSpec((1, HY_CH), lambda g, i, j: (0, 0))]
        out_shape = (jax.ShapeDtypeStruct((L, HY_CH), F32),) * 2
        out_specs = (pl.BlockSpec((ti, HY_CH), lambda g, i, j: (i, 0)),) * 2
    elif mode == "fwdk":
        in_specs += [pl.BlockSpec((ti, HY_CH), lambda g, i, j: (i, 0))] * 2
        out_shape = (jax.ShapeDtypeStruct((B, L, C), BF16),) * 2
        out_specs = (row_spec(C),) * 2
    elif mode == "inv":
        in_specs += [row_spec(C)] * 2
        out_shape = jax.ShapeDtypeStruct((B, L, C), BF16)
        out_specs = row_spec(C)
    else:
        out_shape = jax.ShapeDtypeStruct((B, L, C), BF16)
        out_specs = row_spec(C)
    return pl.pallas_call(
        functools.partial(_dft_kernel, mode, nb, len(xs)),
        out_shape=out_shape,
        grid=grid,
        in_specs=in_specs,
        out_specs=out_specs,
        scratch_shapes=[pltpu.VMEM((nb, ti, C), F32), pltpu.VMEM((nb, ti, C), F32)],
        compiler_params=_cparams(("arbitrary", "arbitrary", "arbitrary")),
        name="dft_" + mode,
    )(bc, bs, rc, rs, *xs, *extras)


FFT_R = 64
FFT_KF = 8


def _pack_pairs(x):
    return pltpu.bitcast(x.astype(BF16), jnp.uint32)


def _unpack_pairs(w):
    return pltpu.bitcast(w, BF16)


def _lead_in_kernel(g_ref, x_ref, o_ref):
    g = g_ref[...]
    xt = jnp.swapaxes(x_ref[...], 0, 1)
    ys = [_pack_pairs(jnp.dot(g, xt[j].astype(BF16), preferred_element_type=F32)) for j in range(FFT_KF)]
    o_ref[...] = jnp.swapaxes(jnp.stack(ys, axis=0), 0, 1)


def _lead_in(g, x, name):
    nbx, _, _, c = x.shape
    m2 = g.shape[0] // 2
    return pl.pallas_call(
        _lead_in_kernel,
        out_shape=jax.ShapeDtypeStruct((nbx, m2, FFT_R, c), jnp.uint32),
        grid=(nbx, FFT_R // FFT_KF),
        in_specs=[pl.BlockSpec(g.shape, lambda b, k: (0, 0)),
                  pl.BlockSpec((None, FFT_R, FFT_KF, c), lambda b, k: (b, 0, k, 0))],
        out_specs=pl.BlockSpec((None, m2, FFT_KF, c), lambda b, k: (b, 0, k, 0)),
        compiler_params=_cparams(("arbitrary", "arbitrary")),
        name=name,
    )(g, x)


def _lead_out_kernel(n_extra, g_ref, w_ref, *rest):
    g = g_ref[...]
    o_ref = rest[-1]
    wt = jnp.swapaxes(w_ref[...], 0, 1)
    ys = [jnp.dot(g, _unpack_pairs(wt[j]), preferred_element_type=F32) for j in range(FFT_KF)]
    y = jnp.swapaxes(jnp.stack(ys, axis=0), 0, 1)
    if n_extra:
        y = (y + rest[0][...]) * rest[1][...]
    o_ref[...] = y


def _lead_out(g, w, extras, name):
    nb, k2, _, c = w.shape
    blk = lambda r: pl.BlockSpec((None, r, FFT_KF, c), lambda b, k: (b, 0, k, 0))
    return pl.pallas_call(
        functools.partial(_lead_out_kernel, len(extras)),
        out_shape=jax.ShapeDtypeStruct((nb, FFT_R, FFT_R, c), F32),
        grid=(nb, FFT_R // FFT_KF),
        in_specs=[pl.BlockSpec(g.shape, lambda b, k: (0, 0)), blk(k2)] + [blk(FFT_R)] * len(extras),
        out_specs=blk(FFT_R),
        compiler_params=_cparams(("arbitrary", "arbitrary")),
        name=name,
    )(g, w, *extras)


def _interleave(a, b, axis):
    st = np.stack([a, b], axis=axis + 1)
    shape = list(a.shape)
    shape[axis] *= 2
    return st.reshape(shape)


def _hy2_tables():
    L = FFT_R * FFT_R
    n2 = 2 * L
    f1 = np.arange(2 * FFT_R, dtype=np.int64)
    s1 = np.arange(FFT_R, dtype=np.int64)
    th = np.pi * (((2 * f1[:, None] + 1) * s1[None, :]) % (4 * FFT_R)) / (2 * FFT_R)
    ga = _interleave(np.cos(th), -np.sin(th), 0)
    ma = _interleave(np.cos(th).T, -np.sin(th).T, 1) / L
    f2 = np.arange(FFT_R // 2, dtype=np.int64)
    s2 = np.arange(FFT_R, dtype=np.int64)
    idx = ((n2 // FFT_R) * 2 * f2[None, :, None] * s2[None, None, :]
           + (2 * f1[:, None, None] + 1) * s2[None, None, :]) % (2 * n2)
    al = np.pi * idx / n2
    c, s = np.cos(al), np.sin(al)
    nmat = np.concatenate([_interleave(c, s, 2), _interleave(-s, c, 2)], axis=1)
    ct, st = np.transpose(c, (0, 2, 1)), np.transpose(s, (0, 2, 1))
    mmat = _interleave(np.concatenate([ct, -st], axis=2), np.concatenate([st, ct], axis=2), 1)
    bf = lambda a: jnp.asarray(a, F32).astype(BF16)
    return bf(ga), bf(nmat), bf(mmat), bf(ma)


def _hy_mid_kernel(a_ref, n_ref, m_ref, k_ref, e_ref):
    half = FFT_R // 2
    ts = [jnp.dot(n_ref[j], _unpack_pairs(a_ref[j]), preferred_element_type=F32)
          for j in range(FFT_KF)]
    ys = []
    for j in range(FFT_KF):
        tr, ti = ts[j][:half], ts[j][half:]
        kr, ki = k_ref[j, 0], k_ref[j, 1]
        ys.append(jnp.concatenate([tr * kr - ti * ki, tr * ki + ti * kr], axis=0).astype(BF16))
    for j in range(FFT_KF):
        e_ref[j] = _pack_pairs(jnp.dot(m_ref[j], ys[j], preferred_element_type=F32))


def _hy_mid(a, nmat, mmat, khat):
    nb = a.shape[0]
    c = a.shape[-1]
    nf1 = 2 * FFT_R
    blk = pl.BlockSpec((None, FFT_KF, FFT_R, c), lambda i, b: (b, i, 0, 0))
    return pl.pallas_call(
        _hy_mid_kernel,
        out_shape=jax.ShapeDtypeStruct(a.shape, jnp.uint32),
        grid=(nf1 // FFT_KF, nb),
        in_specs=[blk,
                  pl.BlockSpec((FFT_KF, FFT_R, 2 * FFT_R), lambda i, b: (i, 0, 0)),
                  pl.BlockSpec((FFT_KF, 2 * FFT_R, FFT_R), lambda i, b: (i, 0, 0)),
                  pl.BlockSpec((FFT_KF, 2, FFT_R // 2, c), lambda i, b: (i, 0, 0, 0))],
        out_specs=blk,
        compiler_params=_cparams(("arbitrary", "arbitrary")),
        name="l0_hyena_mid",
    )(a, nmat, mmat, khat)


def _hy_kfilt_kernel(a_ref, n_ref, nrm_ref, k_ref):
    half = FFT_R // 2
    nrm = nrm_ref[...]
    for j in range(FFT_KF):
        tf = jnp.dot(n_ref[j], _unpack_pairs(a_ref[0, j]), preferred_element_type=F32)
        tb = jnp.dot(n_ref[j], _unpack_pairs(a_ref[1, j]), preferred_element_type=F32)
        k_ref[j, 0] = (tf[:half] + tb[:half]) / nrm
        k_ref[j, 1] = (tf[half:] - tb[half:]) / nrm


def _hy_kfilt(a, nmat, nrm):
    c = a.shape[-1]
    nf1 = 2 * FFT_R
    return pl.pallas_call(
        _hy_kfilt_kernel,
        out_shape=jax.ShapeDtypeStruct((nf1, 2, FFT_R // 2, c), F32),
        grid=(nf1 // FFT_KF,),
        in_specs=[pl.BlockSpec((2, FFT_KF, FFT_R, c), lambda i: (0, i, 0, 0)),
                  pl.BlockSpec((FFT_KF, FFT_R, 2 * FFT_R), lambda i: (i, 0, 0)),
                  _const_spec(nrm.shape)],
        out_specs=pl.BlockSpec((FFT_KF, 2, FFT_R // 2, c), lambda i: (i, 0, 0, 0)),
        compiler_params=_cparams(("arbitrary",)),
        name="l0_hyena_kfilt",
    )(a, nmat, nrm)


def _hyena_long(u, e, x0, hfilt, nrm, nb):
    L = FFT_R * FFT_R
    c = u.shape[-1]
    v4 = lambda a, n: a.reshape(n, FFT_R, FFT_R, c)
    ga, nmat, mmat, ma = _hy2_tables()
    khat = _hy_kfilt(_lead_in(ga, v4(hfilt, 2), "l0_hyena_fwd_a"), nmat, nrm)
    ee = _hy_mid(_lead_in(ga, v4(u, nb), "l0_hyena_fwd_a"), nmat, mmat, khat)
    y = _lead_out(ma, ee, [v4(e, nb), v4(x0, nb)], "l0_hyena_inv_a")
    return y.reshape(nb * L, c)


def _fn2_tables():
    L = FFT_R * FFT_R
    r = np.arange(FFT_R, dtype=np.int64)
    idx = (FFT_R * r[None, :, None] * r[None, None, :] + r[None, :, None] * r[:, None, None]) % L
    gm = 2.0 * np.pi * idx / L
    c, s = np.cos(gm), np.sin(gm)
    g1 = _interleave(np.concatenate([c, -s], axis=2), np.concatenate([-s, -c], axis=2), 1)
    dl = 2.0 * np.pi * ((r[:, None] * r[None, :]) % FFT_R) / FFT_R
    g2 = _interleave(np.cos(dl), np.sin(dl), 1) / math.sqrt(L * FN_GROUP_CH)
    bf = lambda a: jnp.asarray(a, F32).astype(BF16)
    return bf(g1), bf(g2)


def _fnet_s1_kernel(x_ref, m_ref, cs_ref, g1_ref, o_ref, zc_ref, zs_ref):
    xs = jnp.swapaxes(x_ref[...], 0, 1).reshape(FFT_KF * FFT_R, D_MODEL)
    m = m_ref[...]
    h = (_ln_plain(xs) * (1.0 + m[:, D_MODEL:2 * D_MODEL]) + m[:, 0:D_MODEL]).astype(BF16)
    cs = cs_ref[...]
    for g in range(D_MODEL // FN_GROUP_CH):
        a = g * FN_GROUP_CH
        z = jnp.dot(h[:, a:a + FN_GROUP_CH], cs, preferred_element_type=F32)
        zc_ref[:, a:a + FN_GROUP_CH] = z[:, :FN_GROUP_CH].astype(BF16)
        zs_ref[:, a:a + FN_GROUP_CH] = z[:, FN_GROUP_CH:].astype(BF16)
    for j in range(FFT_KF):
        r0 = j * FFT_R
        s = jnp.concatenate([zc_ref[r0:r0 + FFT_R, :], zs_ref[r0:r0 + FFT_R, :]], axis=0)
        o_ref[j] = _pack_pairs(jnp.dot(g1_ref[j], s, preferred_element_type=F32))


def _fnet_long(x, mods, mod_base, nb):
    L = FFT_R * FFT_R
    d = D_MODEL
    g1, g2 = _fn2_tables()
    cs = _group_dft_table()
    bb = pl.pallas_call(
        _fnet_s1_kernel,
        out_shape=jax.ShapeDtypeStruct((nb, FFT_R, FFT_R, d), jnp.uint32),
        grid=(nb, FFT_R // FFT_KF),
        in_specs=[pl.BlockSpec((None, FFT_R, FFT_KF, d), lambda b, k: (b, 0, k, 0)),
                  pl.BlockSpec((None, 1, 6 * d), lambda b, k: (mod_base + b, 0, 0)),
                  pl.BlockSpec(cs.shape, lambda b, k: (0, 0)),
                  pl.BlockSpec((FFT_KF, 2 * FFT_R, 2 * FFT_R), lambda b, k: (k, 0, 0))],
        out_specs=pl.BlockSpec((None, FFT_KF, FFT_R, d), lambda b, k: (b, k, 0, 0)),
        scratch_shapes=[pltpu.VMEM((FFT_KF * FFT_R, d), BF16), pltpu.VMEM((FFT_KF * FFT_R, d), BF16)],
        compiler_params=_cparams(("arbitrary", "arbitrary")),
        name="l1_fnet_stage1",
    )(x.reshape(nb, FFT_R, FFT_R, d), mods, cs, g1)
    y = _lead_out(g2, bb, [], "l1_fnet_stage2")
    return y.reshape(nb * L, d)


def _group_dft_table():
    g = FN_GROUP_CH
    jk = (np.arange(g, dtype=np.int64)[:, None] * np.arange(g, dtype=np.int64)[None, :]) % g
    ang = 2.0 * np.pi * jk / g
    return jnp.asarray(np.concatenate([np.cos(ang), np.sin(ang)], axis=1), F32).astype(BF16)


def _fnet_front_kernel(x_ref, m_ref, cs_ref, zc_ref, zs_ref):
    m = m_ref[...]
    h = (_ln_plain(x_ref[...]) * (1.0 + m[:, D_MODEL:2 * D_MODEL]) + m[:, 0:D_MODEL]).astype(BF16)
    cs = cs_ref[...]
    for g in range(D_MODEL // FN_GROUP_CH):
        a = g * FN_GROUP_CH
        z = jnp.dot(h[:, a:a + FN_GROUP_CH], cs, preferred_element_type=F32)
        zc_ref[:, a:a + FN_GROUP_CH] = z[:, :FN_GROUP_CH].astype(BF16)
        zs_ref[:, a:a + FN_GROUP_CH] = z[:, FN_GROUP_CH:].astype(BF16)


def _fnet_front(x, mods, mod_base, tiles_per_mod):
    t = x.shape[0]
    tm = ROW_TILE
    cs = _group_dft_table()
    return pl.pallas_call(
        _fnet_front_kernel,
        out_shape=(jax.ShapeDtypeStruct((t, D_MODEL), BF16),) * 2,
        grid=(t // tm,),
        in_specs=[pl.BlockSpec((tm, D_MODEL), lambda i: (i, 0)),
                  _mod_spec(mod_base, tiles_per_mod),
                  _const_spec(cs.shape)],
        out_specs=(pl.BlockSpec((tm, D_MODEL), lambda i: (i, 0)),) * 2,
        compiler_params=_cparams(("arbitrary",)),
        name="l1_fnet_front",
    )(x, mods, cs)


def _post_kernel(n_a, *refs):
    x_ref, m_ref = refs[0], refs[1]
    a_refs = refs[2:2 + n_a]
    wo_refs = refs[2 + n_a:2 + 2 * n_a]
    g1_ref, b1_ref, w1_ref, w2_ref, g2_ref, b2_ref, o_ref = refs[2 + 2 * n_a:]
    m = m_ref[...]
    d = D_MODEL
    tm = x_ref.shape[0]
    halves = [(r, r + tm // POST_SPLIT) for r in range(0, tm, tm // POST_SPLIT)]
    outs = []
    for r0, r1 in halves:
        out = _bdot(a_refs[0][r0:r1, :], wo_refs[0][...])
        for a_ref, wo_ref in zip(a_refs[1:], wo_refs[1:]):
            out += _bdot(a_ref[r0:r1, :], wo_ref[...])
        outs.append(out)
    x1s, hs = [], []
    for (r0, r1), out in zip(halves, outs):
        x1 = _ln_plain(ALPHA * x_ref[r0:r1, :] + m[:, 2 * d:3 * d] * out) * g1_ref[...] + b1_ref[...]
        x1s.append(x1)
        hs.append((_ln_plain(x1) * (1.0 + m[:, 4 * d:5 * d]) + m[:, 3 * d:4 * d]).astype(BF16))
    accs = []
    n_c = D_FF // d

    def up(h, c):
        hc = jnp.maximum(jnp.dot(h, w1_ref[:, c * d:(c + 1) * d], preferred_element_type=F32), 0.0)
        return (hc * hc).astype(BF16)

    for h in hs:
        acc = None
        nxt = up(h, 0)
        for c in range(n_c):
            cur = nxt
            if c + 1 < n_c:
                nxt = up(h, c + 1)
            part = jnp.dot(cur, w2_ref[c * d:(c + 1) * d, :], preferred_element_type=F32)
            acc = part if acc is None else acc + part
        accs.append(acc)
    for (r0, r1), x1, acc in zip(halves, x1s, accs):
        o_ref[r0:r1, :] = _ln_plain(ALPHA * x1 + m[:, 5 * d:6 * d] * acc) * g2_ref[...] + b2_ref[...]


def _post(x, mods, mod_base, tiles_per_mod, a_list, wo_list, g1, b1, w1, w2, g2, b2):
    t = x.shape[0]
    tm = POST_TILE
    row = lambda c: pl.BlockSpec((tm, c), lambda i: (i, 0))
    once = lambda v: pl.BlockSpec(v.shape, lambda i: (0,) * v.ndim, pipeline_mode=pl.Buffered(1))
    in_specs = ([row(D_MODEL), _mod_spec(mod_base, tiles_per_mod * ROW_TILE // tm)]
                + [row(a.shape[1]) for a in a_list]
                + [once(w) for w in wo_list]
                + [once(v) for v in (g1, b1, w1, w2, g2, b2)])
    return pl.pallas_call(
        functools.partial(_post_kernel, len(a_list)),
        out_shape=jax.ShapeDtypeStruct((t, D_MODEL), F32),
        grid=(t // tm,),
        in_specs=in_specs,
        out_specs=row(D_MODEL),
        compiler_params=_cparams(("arbitrary",)),
        name="post_mlp",
    )(x, mods, *a_list, *wo_list, g1, b1, w1, w2, g2, b2)


def _rot_cols(w):
    parts = []
    for seg in range(2):
        o = seg * 32
        parts += [-w[:, o + 16:o + 32], w[:, o:o + 16]]
    return jnp.concatenate(parts, axis=1)


def _pad_cols(w, n):
    return jnp.pad(w, ((0, 0), (0, n - w.shape[1])))


def _rope_tables(L):
    rows = L // GRID_W
    row = np.repeat(np.arange(rows, dtype=np.float64), GRID_W)
    col = np.tile(np.arange(GRID_W, dtype=np.float64), rows)
    half = QK_ROPE // 2
    inv = 1.0 / (ROPE_THETA ** (np.arange(0, half, 2, dtype=np.float64) / half))
    ar = row[:, None] * inv[None, :]
    ac = col[:, None] * inv[None, :]
    ang = np.concatenate([ar, ar, ac, ac], axis=1)
    cos = np.concatenate([np.cos(ang), np.ones_like(ang)], axis=1)
    sin = np.concatenate([np.sin(ang), np.zeros_like(ang)], axis=1)
    return jnp.asarray(cos, F32), jnp.asarray(sin, F32)


def kernel(x_prompt, x_sample, cache_l0_ckv, cache_l0_krope, c, c_ctx, l0_ada_w, l0_ada_b, l0_w_in, l0_conv_w, l0_conv_b, l0_hf_w1, l0_hf_b1, l0_hf_freq, l0_hf_w2, l0_hf_b2, l0_hf_w3, l0_hf_skip, l0_q_norm, l0_q_up, l0_kv_norm, l0_kv_up, l0_w_out, l0_ln1_g, l0_ln1_b, l0_mlp_w1, l0_mlp_w2, l0_ln2_g, l0_ln2_b, l1_ada_w, l1_ada_b, l1_w_out, l1_ln1_g, l1_ln1_b, l1_mlp_w1, l1_mlp_w2, l1_ln2_g, l1_ln2_b):
    nbc, lc, d = x_prompt.shape
    nbs, ls, _ = x_sample.shape
    past = cache_l0_ckv.shape[1]
    tm = ROW_TILE
    row1 = lambda v: v.reshape(1, -1)

    cond8 = jnp.concatenate([c_ctx[None, :], c, jnp.zeros((8 - 1 - nbs, d), F32)], axis=0)
    mods0 = _modulation(cond8, l0_ada_w, l0_ada_b)
    mods1 = _modulation(cond8, l1_ada_w, l1_ada_b)

    kpe_w = l0_w_in[:, 1920:1984]
    win = jnp.concatenate([l0_w_in[:, :1920], _pad_cols(kpe_w, LANE), _pad_cols(_rot_cols(kpe_w), LANE)],
                          axis=1).astype(BF16)
    dh = QK_NOPE + QK_ROPE
    q_nope = [l0_q_up[:, h * dh:h * dh + QK_NOPE] for h in range(MLA_HEADS)]
    q_pe = [l0_q_up[:, h * dh + QK_NOPE:(h + 1) * dh] for h in range(MLA_HEADS)]
    qup = jnp.concatenate(q_nope + [_pad_cols(w, LANE) for w in q_pe]
                          + [_pad_cols(_rot_cols(w), LANE) for w in q_pe], axis=1).astype(BF16)
    kvup = l0_kv_up.astype(BF16)
    front_w = (win, row1(l0_q_norm), qup, row1(l0_kv_norm), kvup, l0_conv_w, row1(l0_conv_b), row1(l0_hf_skip))
    w1p = jnp.pad(l0_hf_w1, ((0, LANE - l0_hf_w1.shape[0]), (0, 0)))
    filt_w = (w1p, row1(l0_hf_b1), row1(l0_hf_freq), l0_hf_w2, row1(l0_hf_b2), l0_hf_w3.astype(BF16))
    wo0 = l0_w_out.astype(BF16)

    xc = x_prompt.reshape(nbc * lc, d)
    xs = x_sample.reshape(nbs * ls, d)
    groups = (
        dict(x=xc, nb=nbc, L=lc, mod_base=0, tiles_per_mod=nbc * lc // tm, dft_nb=4, tq=lc),
        dict(x=xs, nb=nbs, L=ls, mod_base=1, tiles_per_mod=ls // tm, dft_nb=nbs, tq=512),
    )
    ones_tab = (jnp.concatenate([jnp.ones((tm, LANE), F32)], axis=0), jnp.zeros((tm, LANE), F32))

    outs = []
    ctx_ckv = ctx_krope = None
    for gi, g in enumerate(groups):
        nb, L = g["nb"], g["L"]
        tiles_per_seq = L // tm
        latent = gi == 1
        cos, sin = _rope_tables(L) if latent else ones_tab
        two_stage = L == FFT_R * FFT_R
        io_dtype = F32 if two_stage else BF16
        u, e, x0, q, k, vt, kvn, kpe = _front(g["x"], mods0, g["mod_base"], g["tiles_per_mod"], front_w,
                                              cos, sin, latent, tiles_per_seq, io_dtype)
        if latent:
            extra = _cache_kv(cache_l0_ckv.reshape(nbs * past, KV_LORA),
                              _pad_cols(cache_l0_krope.reshape(nbs * past, QK_ROPE), LANE), kvup)
        else:
            extra = None
            ctx_ckv = kvn.reshape(nb, L, KV_LORA)
            ctx_krope = kpe.reshape(nb, L, QK_ROPE)
        y_mla = _attention(q, k, vt, extra, nb, L, g["tq"])

        hfilt, hnorm = _filters(L, io_dtype, *filt_w)
        if two_stage:
            y_hy = _hyena_long(u, e, x0, hfilt, hnorm, nb)
        else:
            kre, kim = _dft("filt", "hy_fwd", [hfilt], [hnorm], 2)
            sh = (nb, L, HY_CH)
            yre, yim = _dft("fwdk", "hy_fwd", [u.reshape(sh)], [kre, kim], g["dft_nb"])
            y_hy = _dft("inv", "hy_inv", [yre, yim], [e.reshape(sh), x0.reshape(sh)], g["dft_nb"])
            y_hy = y_hy.reshape(nb * L, HY_CH)

        x1 = _post(g["x"], mods0, g["mod_base"], g["tiles_per_mod"], [y_hy, y_mla],
                   [wo0[:HY_CH], wo0[HY_CH:]], row1(l0_ln1_g), row1(l0_ln1_b),
                   l0_mlp_w1.astype(BF16), l0_mlp_w2.astype(BF16), row1(l0_ln2_g), row1(l0_ln2_b))

        if L == FFT_R * FFT_R:
            yf = _fnet_long(x1, mods1, g["mod_base"], nb)
        else:
            zc, zs = _fnet_front(x1, mods1, g["mod_base"], g["tiles_per_mod"])
            sh = (nb, L, d)
            yf = _dft("fnet", "fnet", [zc.reshape(sh), zs.reshape(sh)], [], min(g["dft_nb"], 2))
            yf = yf.reshape(nb * L, d)
        x2 = _post(x1, mods1, g["mod_base"], g["tiles_per_mod"], [yf],
                   [l1_w_out.astype(BF16)], row1(l1_ln1_g), row1(l1_ln1_b),
                   l1_mlp_w1.astype(BF16), l1_mlp_w2.astype(BF16), row1(l1_ln2_g), row1(l1_ln2_b))
        outs.append(x2.reshape(nb, L, d))

    return (outs[0], outs[1], ctx_ckv, ctx_krope)
```

```python
import functools
import math

import numpy as np
import jax
import jax.numpy as jnp
from jax import lax
from jax.experimental import pallas as pl
from jax.experimental.pallas import tpu as pltpu

F32 = jnp.float32
BF16 = jnp.bfloat16
HI = lax.Precision.HIGHEST

D_MODEL = 1024
DEPTH = 2
GRID_W = 64
HY_CH = 512
FILT_BANDS = 16
FILT_ORDER = 64
FAST_DECAY_PCT = 0.3
SLOW_DECAY_PCT = 1.5
DECAY_TARGET = 1e-2
MAX_DECAY = math.log(DECAY_TARGET) / FAST_DECAY_PCT
MIN_DECAY = math.log(DECAY_TARGET) / SLOW_DECAY_PCT
MLA_HEADS = 4
QK_NOPE = 128
QK_ROPE = 64
V_HEAD = 128
Q_LORA = 256
KV_LORA = 128
ROPE_THETA = 10000.0
FN_GROUP_CH = 128
D_FF = 4096
ALPHA = (2 * DEPTH) ** 0.25
LN_EPS = 1e-5
RMS_EPS = 1e-6

LANE = 128
ROW_TILE = 256
POST_TILE = 512
POST_SPLIT = 2
QK_PAD = 256
VT_ROWS = V_HEAD + 16
LOG2E = 1.4426950408889634
VMEM_LIMIT = 56 * 1024 * 1024


def _cparams(sem):
    return pltpu.CompilerParams(dimension_semantics=sem, vmem_limit_bytes=VMEM_LIMIT)


def _ln_plain(x):
    mu = jnp.mean(x, axis=-1, keepdims=True)
    xc = x - mu
    var = jnp.mean(xc * xc, axis=-1, keepdims=True)
    return xc * lax.rsqrt(var + LN_EPS)


def _rms(x, g):
    return x * lax.rsqrt(jnp.mean(x * x, axis=-1, keepdims=True) + RMS_EPS) * g


def _bdot(a, b):
    return jnp.dot(a.astype(BF16), b, preferred_element_type=F32)


def _vt_rows(v):
    ones = jnp.ones((VT_ROWS - V_HEAD, v.shape[0]), BF16)
    return jnp.concatenate([jnp.transpose(v).astype(BF16), ones], axis=0)


MOD_STREAMS = 3


def _mod_kernel(c_ref, *refs):
    w_refs, b_ref, o_ref = refs[:MOD_STREAMS], refs[MOD_STREAMS], refs[MOD_STREAMS + 1]
    c = c_ref[...]
    s = c / (1.0 + jnp.exp(-c))
    tn = w_refs[0].shape[1]
    for k, w_ref in enumerate(w_refs):
        o_ref[:, k * tn:(k + 1) * tn] = (jnp.dot(s, w_ref[...], precision=HI, preferred_element_type=F32)
                                          + b_ref[:, k * tn:(k + 1) * tn])


def _modulation(cond8, w, b):
    n = w.shape[1]
    tn = 512
    step = MOD_STREAMS * tn
    w_specs = [pl.BlockSpec((D_MODEL, tn), functools.partial(lambda j, k: (0, MOD_STREAMS * j + k), k=k))
               for k in range(MOD_STREAMS)]
    out = pl.pallas_call(
        _mod_kernel,
        out_shape=jax.ShapeDtypeStruct((8, n), F32),
        grid=(n // step,),
        in_specs=[pl.BlockSpec((8, D_MODEL), lambda j: (0, 0))] + w_specs
                 + [pl.BlockSpec((1, step), lambda j: (0, j))],
        out_specs=pl.BlockSpec((8, step), lambda j: (0, j)),
        compiler_params=_cparams(("arbitrary",)),
        name="modulation",
    )(cond8, *([w] * MOD_STREAMS), b.reshape(1, n))
    return out.reshape(8, 1, n)


def _mod_spec(mod_base, tiles_per_mod):
    return pl.BlockSpec((None, 1, 6 * D_MODEL), lambda i: (mod_base + i // tiles_per_mod, 0, 0))


def _const_spec(shape):
    nd = len(shape)
    return pl.BlockSpec(shape, lambda i: (0,) * nd)


HALO = 8


def _front_kernel(tiles_per_seq, x_ref, xp_ref, xn_ref, m_ref, win_ref, qn_ref, qup_ref, kvn_ref, kvup_ref,
                  cos_ref, sin_ref, cw_ref, cb_ref, skip_ref,
                  u_ref, e_ref, x0_ref, q_ref, k_ref, v_ref, kvn_out_ref, kpe_ref):
    i = pl.program_id(0)
    m = m_ref[...]
    tm = x_ref.shape[0]
    nh = 3 * HY_CH
    xe = jnp.concatenate([xp_ref[...], x_ref[...], xn_ref[...]], axis=0)
    he = _ln_plain(xe) * (1.0 + m[:, D_MODEL:2 * D_MODEL]) + m[:, 0:D_MODEL]

    zh = _bdot(he, win_ref[:, :nh])
    pos = i % tiles_per_seq
    rows = lax.broadcasted_iota(jnp.int32, (tm + 2 * HALO, 1), 0)
    inside = jnp.logical_and(jnp.logical_or(rows >= HALO, pos != 0),
                             jnp.logical_or(rows < tm + HALO, pos != tiles_per_seq - 1))
    zh = jnp.where(inside, zh, 0.0)
    cw = cw_ref[...]
    pz = (pltpu.roll(zh, 1, 0) * cw[0:1, :] + zh * cw[1:2, :]
          + pltpu.roll(zh, tm + 2 * HALO - 1, 0) * cw[2:3, :])[HALO:HALO + tm] + cb_ref[...]
    u = pz[:, 2 * HY_CH:] * pz[:, HY_CH:2 * HY_CH]
    u_ref[...] = u.astype(u_ref.dtype)
    e_ref[...] = u * skip_ref[...]
    x0_ref[...] = pz[:, :HY_CH]

    z = _bdot(he[HALO:HALO + tm], win_ref[:, nh:])
    q_c = z[:, 0:256]
    kv_c = z[:, 256:384]
    cos = cos_ref[...]
    sin = sin_ref[...]
    kpe = z[:, 384:512] * cos + z[:, 512:640] * sin
    kpe_ref[...] = kpe[:, :QK_ROPE]
    kpe_b = kpe.astype(BF16)
    q = _bdot(_rms(q_c, qn_ref[...]), qup_ref[...]) * (LOG2E / math.sqrt(QK_NOPE + QK_ROPE))
    kvn = _rms(kv_c, kvn_ref[...])
    kvn_out_ref[...] = kvn
    kv = _bdot(kvn, kvup_ref[...])
    for hd in range(MLA_HEADS):
        a = hd * LANE
        q_pe = (q[:, 512 + a:512 + a + LANE] * cos + q[:, 1024 + a:1024 + a + LANE] * sin).astype(BF16)
        q_ref[hd] = jnp.concatenate([q[:, a:a + LANE].astype(BF16), q_pe], axis=-1)
        k_ref[hd] = jnp.concatenate([kv[:, 2 * a:2 * a + LANE].astype(BF16), kpe_b], axis=-1)
        v_ref[hd] = _vt_rows(kv[:, 2 * a + LANE:2 * a + 2 * LANE])


def _front(x, mods, mod_base, tiles_per_mod, w, cos, sin, rope, tiles_per_seq, u_dtype):
    t = x.shape[0]
    tm = ROW_TILE
    win, qn, qup, kvn, kvup, conv_w, conv_b, skip = w
    if rope:
        tab_spec = pl.BlockSpec((tm, LANE), lambda i: (i % tiles_per_seq, 0))
    else:
        tab_spec = pl.BlockSpec((tm, LANE), lambda i: (0, 0))
    r8 = tm // HALO
    n8 = t // HALO
    hy_out = lambda dt: jax.ShapeDtypeStruct((t, HY_CH), dt)
    hy_spec = pl.BlockSpec((tm, HY_CH), lambda i: (i, 0))
    return pl.pallas_call(
        functools.partial(_front_kernel, tiles_per_seq),
        out_shape=(hy_out(u_dtype), hy_out(F32), hy_out(F32),
                   jax.ShapeDtypeStruct((MLA_HEADS, t, QK_PAD), BF16),
                   jax.ShapeDtypeStruct((MLA_HEADS, t, QK_PAD), BF16),
                   jax.ShapeDtypeStruct((MLA_HEADS, VT_ROWS, t), BF16),
                   jax.ShapeDtypeStruct((t, KV_LORA), F32),
                   jax.ShapeDtypeStruct((t, QK_ROPE), F32)),
        grid=(t // tm,),
        in_specs=[pl.BlockSpec((tm, D_MODEL), lambda i: (i, 0)),
                  pl.BlockSpec((HALO, D_MODEL), lambda i: (jnp.maximum(i * r8 - 1, 0), 0)),
                  pl.BlockSpec((HALO, D_MODEL), lambda i: (jnp.minimum((i + 1) * r8, n8 - 1), 0)),
                  _mod_spec(mod_base, tiles_per_mod),
                  _const_spec(win.shape), _const_spec(qn.shape), _const_spec(qup.shape),
                  _const_spec(kvn.shape), _const_spec(kvup.shape),
                  tab_spec, tab_spec,
                  _const_spec(conv_w.shape), _const_spec(conv_b.shape), _const_spec(skip.shape)],
        out_specs=(hy_spec, hy_spec, hy_spec,
                   pl.BlockSpec((MLA_HEADS, tm, QK_PAD), lambda i: (0, i, 0)),
                   pl.BlockSpec((MLA_HEADS, tm, QK_PAD), lambda i: (0, i, 0)),
                   pl.BlockSpec((MLA_HEADS, VT_ROWS, tm), lambda i: (0, 0, i)),
                   pl.BlockSpec((tm, KV_LORA), lambda i: (i, 0)),
                   pl.BlockSpec((tm, QK_ROPE), lambda i: (i, 0))),
        compiler_params=_cparams(("arbitrary",)),
        name="l0_front",
    )(x, x, x, mods, win, qn, qup, kvn, kvup, cos, sin, conv_w, conv_b, skip)


def _cache_kv_kernel(ckv_ref, kr_ref, kvup_ref, k_ref, v_ref):
    kv = _bdot(ckv_ref[...], kvup_ref[...])
    kr = kr_ref[...].astype(BF16)
    for hd in range(MLA_HEADS):
        a = 2 * hd * LANE
        k_ref[hd] = jnp.concatenate([kv[:, a:a + LANE].astype(BF16), kr], axis=-1)
        v_ref[hd] = _vt_rows(kv[:, a + LANE:a + 2 * LANE])


def _cache_kv(ckv, krope_pad, kvup):
    t = ckv.shape[0]
    return pl.pallas_call(
        _cache_kv_kernel,
        out_shape=(jax.ShapeDtypeStruct((MLA_HEADS, t, QK_PAD), BF16),
                   jax.ShapeDtypeStruct((MLA_HEADS, VT_ROWS, t), BF16)),
        name="l0_cache_kv",
    )(ckv, krope_pad, kvup)


def _col_reduce(x, op):
    rows, n = x.shape
    for g in (32, 8):
        if rows % (8 * g) == 0 and rows > 8 * g:
            x = op(x.reshape(rows // (8 * g), 8 * g, n), axis=0)
            rows = 8 * g
    return op(x, axis=0, keepdims=True)


def _attn_kernel(n_kv, hps, q_ref, *refs):
    k_refs, vt_refs = refs[:n_kv], refs[n_kv:2 * n_kv]
    o_ref, s_even, s_odd = refs[2 * n_kv:]
    i = pl.program_id(0)

    @pl.when(i == 0)
    def _():
        s_odd[...] = jnp.zeros_like(s_odd)

    def step(s_write, s_read):
        nt = (((1,), (1,)), ((), ()))
        for h in range(hps):
            q = q_ref[h]
            r0 = 0
            for k_ref in k_refs:
                lk = k_ref.shape[1]
                s_write[h, r0:r0 + lk, :] = lax.dot_general(k_ref[h], q, nt, preferred_element_type=F32)
                r0 += lk
        for h in range(hps):
            s = s_read[h]
            m = _col_reduce(s, jnp.max)
            pb = jnp.exp2(s - m).astype(BF16)
            acc = None
            r0 = 0
            for vt_ref in vt_refs:
                lk = vt_ref.shape[2]
                pv = jnp.dot(vt_ref[h], pb[r0:r0 + lk, :], preferred_element_type=F32)
                acc = pv if acc is None else acc + pv
                r0 += lk
            o_ref[:, h * V_HEAD:(h + 1) * V_HEAD] = jnp.transpose(
                acc[:V_HEAD] / acc[V_HEAD:V_HEAD + 1]).astype(o_ref.dtype)

    pl.when(i % 2 == 0)(lambda: step(s_even, s_odd))
    pl.when(i % 2 == 1)(lambda: step(s_odd, s_even))


def _attention(q, k, vt, extra, nb, lq, tq, hps):
    nq = lq // tq
    ng = MLA_HEADS // hps
    n_tiles = nb * ng * nq

    def where(t):
        bh = t // nq
        return bh // ng, bh % ng, t % nq

    def score_side(fn):
        return lambda i: fn(*where(jnp.minimum(i, n_tiles - 1)))

    def value_side(fn):
        return lambda i: fn(*where(jnp.maximum(i - 1, 0)))

    ks, vts = [k], [vt]
    if extra is not None:
        ks.append(extra[0])
        vts.append(extra[1])
    in_specs = [pl.BlockSpec((hps, tq, QK_PAD), score_side(lambda b, h, j: (h, b * nq + j, 0)))]
    in_specs += [pl.BlockSpec((hps, a.shape[1] // nb, QK_PAD), score_side(lambda b, h, j: (h, b, 0))) for a in ks]
    in_specs += [pl.BlockSpec((hps, VT_ROWS, a.shape[2] // nb), value_side(lambda b, h, j: (h, 0, b))) for a in vts]
    lk_total = sum(a.shape[1] // nb for a in ks)
    return pl.pallas_call(
        functools.partial(_attn_kernel, len(ks), hps),
        out_shape=jax.ShapeDtypeStruct((nb * lq, MLA_HEADS * V_HEAD), BF16),
        grid=(n_tiles + 1,),
        in_specs=in_specs,
        out_specs=pl.BlockSpec((tq, hps * V_HEAD), value_side(lambda b, h, j: (b * nq + j, h))),
        scratch_shapes=[pltpu.VMEM((hps, lk_total, tq), F32), pltpu.VMEM((hps, lk_total, tq), F32)],
        compiler_params=_cparams(("arbitrary",)),
        name="l0_attention",
    )(q, *ks, *vts)


def _filter_kernel(z_ref, w1_ref, b1_ref, fr_ref, w2_ref, b2_ref, w3_ref, dl_ref, h_ref, norm_ref):
    i = pl.program_id(0)
    z = z_ref[...]
    tl = z.shape[0]
    fr = fr_ref[...]
    h = jnp.sin(fr * (jnp.dot(z, w1_ref[...], precision=HI, preferred_element_type=F32) + b1_ref[...]))
    h = jnp.sin(fr * (jnp.dot(h, w2_ref[...], precision=HI, preferred_element_type=F32) + b2_ref[...]))
    h = _bdot(h, w3_ref[...])
    decay = jnp.exp(-(z[:, 0:1] * dl_ref[...]))
    hf = h[:, :HY_CH] * decay
    hb = h[:, HY_CH:] * decay
    part = jnp.sum(jnp.abs(hf) + jnp.abs(hb), axis=0, keepdims=True)

    @pl.when(i == 0)
    def _():
        norm_ref[...] = part

    @pl.when(i > 0)
    def _():
        norm_ref[...] += part

    rows = lax.broadcasted_iota(jnp.int32, hb.shape, 0) + i * tl
    h_ref[0] = hf.astype(h_ref.dtype)
    h_ref[1] = jnp.where(rows == 0, 0.0, hb).astype(h_ref.dtype)


def _filter_embedding(L):
    t = np.linspace(0.0, 1.0, L)[:, None]
    w_ang = 2.0 * np.pi * np.arange(L) / L
    bands = np.linspace(1e-4, FILT_BANDS - 1, FILT_BANDS)
    ang = w_ang[:, None] * bands[None, :]
    z = np.zeros((L, LANE), np.float64)
    z[:, 0:1] = t
    z[:, 1:1 + FILT_BANDS] = np.cos(ang)
    z[:, 1 + FILT_BANDS:1 + 2 * FILT_BANDS] = -np.sin(ang)
    return jnp.asarray(z, F32)


def _filters(L, h_dtype, w1p, b1, fr, w2, b2, w3):
    tl = min(L, 512)
    z = _filter_embedding(L)
    deltas = jnp.asarray(np.abs(np.linspace(MIN_DECAY, MAX_DECAY, HY_CH))[None, :], F32)
    return pl.pallas_call(
        _filter_kernel,
        out_shape=(jax.ShapeDtypeStruct((2, L, HY_CH), h_dtype), jax.ShapeDtypeStruct((1, HY_CH), F32)),
        grid=(L // tl,),
        in_specs=[pl.BlockSpec((tl, LANE), lambda i: (i, 0)),
                  _const_spec(w1p.shape), _const_spec(b1.shape), _const_spec(fr.shape),
                  _const_spec(w2.shape), _const_spec(b2.shape), _const_spec(w3.shape),
                  _const_spec(deltas.shape)],
        out_specs=(pl.BlockSpec((2, tl, HY_CH), lambda i: (0, i, 0)),
                   pl.BlockSpec((1, HY_CH), lambda i: (0, 0))),
        compiler_params=_cparams(("arbitrary",)),
        name="l0_hyena_filters",
    )(z, w1p, b1, fr, w2, b2, w3, deltas)


def _dft_tables(kind, L, ti):
    ni = L // ti
    i = np.arange(ti, dtype=np.int64)[:, None]
    big = (np.arange(ni, dtype=np.int64) * ti)[:, None]
    c = np.arange(L, dtype=np.int64)[None, :]
    if kind == "hy_fwd":
        period = 4 * L
        base_idx = (2 * i + 1) * c
        r_idx = 2 * big * c
        scale = 1.0
    elif kind == "hy_inv":
        period = 4 * L
        base_idx = (2 * c + 1) * i
        r_idx = (2 * c + 1) * big
        scale = 1.0 / L
    else:
        period = L
        base_idx = i * c
        r_idx = big * c
        scale = 1.0 / math.sqrt(L * FN_GROUP_CH)
    ab = 2.0 * np.pi * (base_idx % period) / period
    ar = 2.0 * np.pi * (r_idx % period) / period
    return (jnp.asarray(np.cos(ab), F32), jnp.asarray(np.sin(ab), F32),
            jnp.asarray(scale * np.cos(ar), F32).reshape(ni, 1, L),
            jnp.asarray(scale * np.sin(ar), F32).reshape(ni, 1, L))


def _dft_kernel(mode, nb, n_x, *refs):
    bc_ref, bs_ref, rc_ref, rs_ref = refs[:4]
    x_refs = refs[4:4 + n_x]
    rest = refs[4 + n_x:]
    p_ref, q_ref = rest[-2], rest[-1]
    j = pl.program_id(2)
    nj = pl.num_programs(2)
    tj = x_refs[0].shape[1]
    if bc_ref.shape[1] == tj:
        bc, bs, rc, rs = bc_ref[...], bs_ref[...], rc_ref[...], rs_ref[...]
    else:
        off = pl.multiple_of(j * tj, tj)
        bc, bs = bc_ref[:, pl.ds(off, tj)], bs_ref[:, pl.ds(off, tj)]
        rc, rs = rc_ref[:, pl.ds(off, tj)], rs_ref[:, pl.ds(off, tj)]
    tc = (bc * rc - bs * rs).astype(BF16)
    ts = (bs * rc + bc * rs).astype(BF16)
    x1_ref = x_refs[0]
    x2_ref = x_refs[-1]

    pq = [(jnp.dot(tc, x1_ref[b], preferred_element_type=F32),
           jnp.dot(ts, x2_ref[b], preferred_element_type=F32)) for b in range(nb)]

    @pl.when(j == 0)
    def _():
        for b in range(nb):
            p_ref[b] = pq[b][0]
            q_ref[b] = pq[b][1]

    @pl.when(j > 0)
    def _():
        for b in range(nb):
            p_ref[b] += pq[b][0]
            q_ref[b] += pq[b][1]

    @pl.when(j == nj - 1)
    def _():
        if mode == "filt":
            nrm = rest[0][...]
            kre_ref, kim_ref = rest[1], rest[2]
            kre_ref[...] = (p_ref[0] + p_ref[1]) / nrm
            kim_ref[...] = (q_ref[1] - q_ref[0]) / nrm
        elif mode == "fwdk":
            kre, kim = rest[0][...], rest[1][...]
            yre_ref, yim_ref = rest[2], rest[3]
            for b in range(nb):
                pp, qq = p_ref[b], q_ref[b]
                yre_ref[b] = (pp * kre + qq * kim).astype(BF16)
                yim_ref[b] = (pp * kim - qq * kre).astype(BF16)
        elif mode == "inv":
            e_ref, x0_ref, o_ref = rest[0], rest[1], rest[2]
            for b in range(nb):
                o_ref[b] = ((p_ref[b] - q_ref[b] + e_ref[b]) * x0_ref[b]).astype(BF16)
        else:
            o_ref = rest[0]
            for b in range(nb):
                o_ref[b] = (p_ref[b] - q_ref[b]).astype(BF16)


def _dft(mode, kind, xs, extras, nb):
    B, L, C = xs[0].shape
    ti = min(L, 256)
    tj = min(L, 512)
    bc, bs, rc, rs = _dft_tables(kind, L, ti)
    grid = (B // nb, L // ti, L // tj)
    x_spec = pl.BlockSpec((nb, tj, C), lambda g, i, j: (g, j, 0))
    row_spec = lambda c, dt=None: pl.BlockSpec((nb, ti, c), lambda g, i, j: (g, i, 0))
    in_specs = [pl.BlockSpec((ti, L), lambda g, i, j: (0, 0)),
                pl.BlockSpec((ti, L), lambda g, i, j: (0, 0)),
                pl.BlockSpec((None, 1, L), lambda g, i, j: (i, 0, 0)),
                pl.BlockSpec((None, 1, L), lambda g, i, j: (i, 0, 0))] + [x_spec] * len(xs)
    if mode == "filt":
        in_specs += [pl.BlockSpec((1, HY_CH), lambda g, i, j: (0, 0))]
        out_shape = (jax.ShapeDtypeStruct((L, HY_CH), F32),) * 2
        out_specs = (pl.BlockSpec((ti, HY_CH), lambda g, i, j: (i, 0)),) * 2
    elif mode == "fwdk":
        in_specs += [pl.BlockSpec((ti, HY_CH), lambda g, i, j: (i, 0))] * 2
        out_shape = (jax.ShapeDtypeStruct((B, L, C), BF16),) * 2
        out_specs = (row_spec(C),) * 2
    elif mode == "inv":
        in_specs += [row_spec(C)] * 2
        out_shape = jax.ShapeDtypeStruct((B, L, C), BF16)
        out_specs = row_spec(C)
    else:
        out_shape = jax.ShapeDtypeStruct((B, L, C), BF16)
        out_specs = row_spec(C)
    return pl.pallas_call(
        functools.partial(_dft_kernel, mode, nb, len(xs)),
        out_shape=out_shape,
        grid=grid,
        in_specs=in_specs,
        out_specs=out_specs,
        scratch_shapes=[pltpu.VMEM((nb, ti, C), F32), pltpu.VMEM((nb, ti, C), F32)],
        compiler_params=_cparams(("arbitrary", "arbitrary", "arbitrary")),
        name="dft_" + mode,
    )(bc, bs, rc, rs, *xs, *extras)


FFT_R = 64
FFT_KF = 8


def _pack_pairs(x):
    return pltpu.bitcast(x.astype(BF16), jnp.uint32)


def _unpack_pairs(w):
    return pltpu.bitcast(w, BF16)


def _lead_in_kernel(g_ref, x_ref, o_ref):
    g = g_ref[...]
    xt = jnp.swapaxes(x_ref[...], 0, 1)
    ys = [_pack_pairs(jnp.dot(g, xt[j].astype(BF16), preferred_element_type=F32)) for j in range(FFT_KF)]
    o_ref[...] = jnp.swapaxes(jnp.stack(ys, axis=0), 0, 1)


def _lead_in(g, x, name):
    nbx, _, _, c = x.shape
    m2 = g.shape[0] // 2
    return pl.pallas_call(
        _lead_in_kernel,
        out_shape=jax.ShapeDtypeStruct((nbx, m2, FFT_R, c), jnp.uint32),
        grid=(nbx, FFT_R // FFT_KF),
        in_specs=[pl.BlockSpec(g.shape, lambda b, k: (0, 0)),
                  pl.BlockSpec((None, FFT_R, FFT_KF, c), lambda b, k: (b, 0, k, 0))],
        out_specs=pl.BlockSpec((None, m2, FFT_KF, c), lambda b, k: (b, 0, k, 0)),
        compiler_params=_cparams(("arbitrary", "arbitrary")),
        name=name,
    )(g, x)


def _lead_out_kernel(n_extra, g_ref, w_ref, *rest):
    g = g_ref[...]
    o_ref = rest[-1]
    wt = jnp.swapaxes(w_ref[...], 0, 1)
    ys = [jnp.dot(g, _unpack_pairs(wt[j]), preferred_element_type=F32) for j in range(FFT_KF)]
    y = jnp.swapaxes(jnp.stack(ys, axis=0), 0, 1)
    if n_extra:
        y = (y + rest[0][...]) * rest[1][...]
    o_ref[...] = y


def _lead_out(g, w, extras, name):
    nb, k2, _, c = w.shape
    blk = lambda r: pl.BlockSpec((None, r, FFT_KF, c), lambda b, k: (b, 0, k, 0))
    return pl.pallas_call(
        functools.partial(_lead_out_kernel, len(extras)),
        out_shape=jax.ShapeDtypeStruct((nb, FFT_R, FFT_R, c), F32),
        grid=(nb, FFT_R // FFT_KF),
        in_specs=[pl.BlockSpec(g.shape, lambda b, k: (0, 0)), blk(k2)] + [blk(FFT_R)] * len(extras),
        out_specs=blk(FFT_R),
        compiler_params=_cparams(("arbitrary", "arbitrary")),
        name=name,
    )(g, w, *extras)


def _interleave(a, b, axis):
    st = np.stack([a, b], axis=axis + 1)
    shape = list(a.shape)
    shape[axis] *= 2
    return st.reshape(shape)


def _hy2_tables():
    L = FFT_R * FFT_R
    n2 = 2 * L
    f1 = np.arange(2 * FFT_R, dtype=np.int64)
    s1 = np.arange(FFT_R, dtype=np.int64)
    th = np.pi * (((2 * f1[:, None] + 1) * s1[None, :]) % (4 * FFT_R)) / (2 * FFT_R)
    ga = _interleave(np.cos(th), -np.sin(th), 0)
    ma = _interleave(np.cos(th).T, -np.sin(th).T, 1) / L
    f2 = np.arange(FFT_R // 2, dtype=np.int64)
    s2 = np.arange(FFT_R, dtype=np.int64)
    idx = ((n2 // FFT_R) * 2 * f2[None, :, None] * s2[None, None, :]
           + (2 * f1[:, None, None] + 1) * s2[None, None, :]) % (2 * n2)
    al = np.pi * idx / n2
    c, s = np.cos(al), np.sin(al)
    nmat = np.concatenate([_interleave(c, s, 2), _interleave(-s, c, 2)], axis=1)
    ct, st = np.transpose(c, (0, 2, 1)), np.transpose(s, (0, 2, 1))
    mmat = _interleave(np.concatenate([ct, -st], axis=2), np.concatenate([st, ct], axis=2), 1)
    bf = lambda a: jnp.asarray(a, F32).astype(BF16)
    return bf(ga), bf(nmat), bf(mmat), bf(ma)


def _hy_mid_kernel(a_ref, n_ref, m_ref, k_ref, e_ref):
    half = FFT_R // 2
    ts = [jnp.dot(n_ref[j], _unpack_pairs(a_ref[j]), preferred_element_type=F32)
          for j in range(FFT_KF)]
    ys = []
    for j in range(FFT_KF):
        tr, ti = ts[j][:half], ts[j][half:]
        kr, ki = k_ref[j, 0], k_ref[j, 1]
        ys.append(jnp.concatenate([tr * kr - ti * ki, tr * ki + ti * kr], axis=0).astype(BF16))
    for j in range(FFT_KF):
        e_ref[j] = _pack_pairs(jnp.dot(m_ref[j], ys[j], preferred_element_type=F32))


def _hy_mid(a, nmat, mmat, khat):
    nb = a.shape[0]
    c = a.shape[-1]
    nf1 = 2 * FFT_R
    blk = pl.BlockSpec((None, FFT_KF, FFT_R, c), lambda i, b: (b, i, 0, 0))
    return pl.pallas_call(
        _hy_mid_kernel,
        out_shape=jax.ShapeDtypeStruct(a.shape, jnp.uint32),
        grid=(nf1 // FFT_KF, nb),
        in_specs=[blk,
                  pl.BlockSpec((FFT_KF, FFT_R, 2 * FFT_R), lambda i, b: (i, 0, 0)),
                  pl.BlockSpec((FFT_KF, 2 * FFT_R, FFT_R), lambda i, b: (i, 0, 0)),
                  pl.BlockSpec((FFT_KF, 2, FFT_R // 2, c), lambda i, b: (i, 0, 0, 0))],
        out_specs=blk,
        compiler_params=_cparams(("arbitrary", "arbitrary")),
        name="l0_hyena_mid",
    )(a, nmat, mmat, khat)


def _hy_kfilt_kernel(a_ref, n_ref, nrm_ref, k_ref):
    half = FFT_R // 2
    nrm = nrm_ref[...]
    for j in range(FFT_KF):
        tf = jnp.dot(n_ref[j], _unpack_pairs(a_ref[0, j]), preferred_element_type=F32)
        tb = jnp.dot(n_ref[j], _unpack_pairs(a_ref[1, j]), preferred_element_type=F32)
        k_ref[j, 0] = (tf[:half] + tb[:half]) / nrm
        k_ref[j, 1] = (tf[half:] - tb[half:]) / nrm


def _hy_kfilt(a, nmat, nrm):
    c = a.shape[-1]
    nf1 = 2 * FFT_R
    return pl.pallas_call(
        _hy_kfilt_kernel,
        out_shape=jax.ShapeDtypeStruct((nf1, 2, FFT_R // 2, c), F32),
        grid=(nf1 // FFT_KF,),
        in_specs=[pl.BlockSpec((2, FFT_KF, FFT_R, c), lambda i: (0, i, 0, 0)),
                  pl.BlockSpec((FFT_KF, FFT_R, 2 * FFT_R), lambda i: (i, 0, 0)),
                  _const_spec(nrm.shape)],
        out_specs=pl.BlockSpec((FFT_KF, 2, FFT_R // 2, c), lambda i: (i, 0, 0, 0)),
        compiler_params=_cparams(("arbitrary",)),
        name="l0_hyena_kfilt",
    )(a, nmat, nrm)


def _hyena_long(u, e, x0, hfilt, nrm, nb):
    L = FFT_R * FFT_R
    c = u.shape[-1]
    v4 = lambda a, n: a.reshape(n, FFT_R, FFT_R, c)
    ga, nmat, mmat, ma = _hy2_tables()
    khat = _hy_kfilt(_lead_in(ga, v4(hfilt, 2), "l0_hyena_fwd_a"), nmat, nrm)
    ee = _hy_mid(_lead_in(ga, v4(u, nb), "l0_hyena_fwd_a"), nmat, mmat, khat)
    y = _lead_out(ma, ee, [v4(e, nb), v4(x0, nb)], "l0_hyena_inv_a")
    return y.reshape(nb * L, c)


def _fn2_tables():
    L = FFT_R * FFT_R
    r = np.arange(FFT_R, dtype=np.int64)
    idx = (FFT_R * r[None, :, None] * r[None, None, :] + r[None, :, None] * r[:, None, None]) % L
    gm = 2.0 * np.pi * idx / L
    c, s = np.cos(gm), np.sin(gm)
    g1 = _interleave(np.concatenate([c, -s], axis=2), np.concatenate([-s, -c], axis=2), 1)
    dl = 2.0 * np.pi * ((r[:, None] * r[None, :]) % FFT_R) / FFT_R
    g2 = _interleave(np.cos(dl), np.sin(dl), 1) / math.sqrt(L * FN_GROUP_CH)
    bf = lambda a: jnp.asarray(a, F32).astype(BF16)
    return bf(g1), bf(g2)


def _fnet_s1_kernel(x_ref, m_ref, cs_ref, g1_ref, o_ref, zc_ref, zs_ref):
    xs = jnp.swapaxes(x_ref[...], 0, 1).reshape(FFT_KF * FFT_R, D_MODEL)
    m = m_ref[...]
    h = (_ln_plain(xs) * (1.0 + m[:, D_MODEL:2 * D_MODEL]) + m[:, 0:D_MODEL]).astype(BF16)
    cs = cs_ref[...]
    for g in range(D_MODEL // FN_GROUP_CH):
        a = g * FN_GROUP_CH
        z = jnp.dot(h[:, a:a + FN_GROUP_CH], cs, preferred_element_type=F32)
        zc_ref[:, a:a + FN_GROUP_CH] = z[:, :FN_GROUP_CH].astype(BF16)
        zs_ref[:, a:a + FN_GROUP_CH] = z[:, FN_GROUP_CH:].astype(BF16)
    for j in range(FFT_KF):
        r0 = j * FFT_R
        s = jnp.concatenate([zc_ref[r0:r0 + FFT_R, :], zs_ref[r0:r0 + FFT_R, :]], axis=0)
        o_ref[j] = _pack_pairs(jnp.dot(g1_ref[j], s, preferred_element_type=F32))


def _fnet_long(x, mods, mod_base, nb):
    L = FFT_R * FFT_R
    d = D_MODEL
    g1, g2 = _fn2_tables()
    cs = _group_dft_table()
    bb = pl.pallas_call(
        _fnet_s1_kernel,
        out_shape=jax.ShapeDtypeStruct((nb, FFT_R, FFT_R, d), jnp.uint32),
        grid=(nb, FFT_R // FFT_KF),
        in_specs=[pl.BlockSpec((None, FFT_R, FFT_KF, d), lambda b, k: (b, 0, k, 0)),
                  pl.BlockSpec((None, 1, 6 * d), lambda b, k: (mod_base + b, 0, 0)),
                  pl.BlockSpec(cs.shape, lambda b, k: (0, 0)),
                  pl.BlockSpec((FFT_KF, 2 * FFT_R, 2 * FFT_R), lambda b, k: (k, 0, 0))],
        out_specs=pl.BlockSpec((None, FFT_KF, FFT_R, d), lambda b, k: (b, k, 0, 0)),
        scratch_shapes=[pltpu.VMEM((FFT_KF * FFT_R, d), BF16), pltpu.VMEM((FFT_KF * FFT_R, d), BF16)],
        compiler_params=_cparams(("arbitrary", "arbitrary")),
        name="l1_fnet_stage1",
    )(x.reshape(nb, FFT_R, FFT_R, d), mods, cs, g1)
    y = _lead_out(g2, bb, [], "l1_fnet_stage2")
    return y.reshape(nb * L, d)


def _group_dft_table():
    g = FN_GROUP_CH
    jk = (np.arange(g, dtype=np.int64)[:, None] * np.arange(g, dtype=np.int64)[None, :]) % g
    ang = 2.0 * np.pi * jk / g
    return jnp.asarray(np.concatenate([np.cos(ang), np.sin(ang)], axis=1), F32).astype(BF16)


def _fnet_front_kernel(x_ref, m_ref, cs_ref, zc_ref, zs_ref):
    m = m_ref[...]
    h = (_ln_plain(x_ref[...]) * (1.0 + m[:, D_MODEL:2 * D_MODEL]) + m[:, 0:D_MODEL]).astype(BF16)
    cs = cs_ref[...]
    for g in range(D_MODEL // FN_GROUP_CH):
        a = g * FN_GROUP_CH
        z = jnp.dot(h[:, a:a + FN_GROUP_CH], cs, preferred_element_type=F32)
        zc_ref[:, a:a + FN_GROUP_CH] = z[:, :FN_GROUP_CH].astype(BF16)
        zs_ref[:, a:a + FN_GROUP_CH] = z[:, FN_GROUP_CH:].astype(BF16)


def _fnet_front(x, mods, mod_base, tiles_per_mod):
    t = x.shape[0]
    tm = ROW_TILE
    cs = _group_dft_table()
    return pl.pallas_call(
        _fnet_front_kernel,
        out_shape=(jax.ShapeDtypeStruct((t, D_MODEL), BF16),) * 2,
        grid=(t // tm,),
        in_specs=[pl.BlockSpec((tm, D_MODEL), lambda i: (i, 0)),
                  _mod_spec(mod_base, tiles_per_mod),
                  _const_spec(cs.shape)],
        out_specs=(pl.BlockSpec((tm, D_MODEL), lambda i: (i, 0)),) * 2,
        compiler_params=_cparams(("arbitrary",)),
        name="l1_fnet_front",
    )(x, mods, cs)


def _post_kernel(n_a, *refs):
    x_ref, m_ref = refs[0], refs[1]
    a_refs = refs[2:2 + n_a]
    wo_refs = refs[2 + n_a:2 + 2 * n_a]
    g1_ref, b1_ref, w1_ref, w2_ref, g2_ref, b2_ref, o_ref = refs[2 + 2 * n_a:]
    m = m_ref[...]
    d = D_MODEL
    tm = x_ref.shape[0]
    halves = [(r, r + tm // POST_SPLIT) for r in range(0, tm, tm // POST_SPLIT)]
    outs = []
    for r0, r1 in halves:
        out = _bdot(a_refs[0][r0:r1, :], wo_refs[0][...])
        for a_ref, wo_ref in zip(a_refs[1:], wo_refs[1:]):
            out += _bdot(a_ref[r0:r1, :], wo_ref[...])
        outs.append(out)
    x1s, hs = [], []
    for (r0, r1), out in zip(halves, outs):
        x1 = _ln_plain(ALPHA * x_ref[r0:r1, :] + m[:, 2 * d:3 * d] * out) * g1_ref[...] + b1_ref[...]
        x1s.append(x1)
        hs.append((_ln_plain(x1) * (1.0 + m[:, 4 * d:5 * d]) + m[:, 3 * d:4 * d]).astype(BF16))
    accs = []
    n_c = D_FF // d

    def up(h, c):
        hc = jnp.maximum(jnp.dot(h, w1_ref[:, c * d:(c + 1) * d], preferred_element_type=F32), 0.0)
        return (hc * hc).astype(BF16)

    for h in hs:
        acc = None
        nxt = up(h, 0)
        for c in range(n_c):
            cur = nxt
            if c + 1 < n_c:
                nxt = up(h, c + 1)
            part = jnp.dot(cur, w2_ref[c * d:(c + 1) * d, :], preferred_element_type=F32)
            acc = part if acc is None else acc + part
        accs.append(acc)
    for (r0, r1), x1, acc in zip(halves, x1s, accs):
        o_ref[r0:r1, :] = _ln_plain(ALPHA * x1 + m[:, 5 * d:6 * d] * acc) * g2_ref[...] + b2_ref[...]


def _post(x, mods, mod_base, tiles_per_mod, a_list, wo_list, g1, b1, w1, w2, g2, b2):
    t = x.shape[0]
    tm = POST_TILE
    row = lambda c: pl.BlockSpec((tm, c), lambda i: (i, 0))
    once = lambda v: pl.BlockSpec(v.shape, lambda i: (0,) * v.ndim, pipeline_mode=pl.Buffered(1))
    in_specs = ([row(D_MODEL), _mod_spec(mod_base, tiles_per_mod * ROW_TILE // tm)]
                + [row(a.shape[1]) for a in a_list]
                + [once(w) for w in wo_list]
                + [once(v) for v in (g1, b1, w1, w2, g2, b2)])
    return pl.pallas_call(
        functools.partial(_post_kernel, len(a_list)),
        out_shape=jax.ShapeDtypeStruct((t, D_MODEL), F32),
        grid=(t // tm,),
        in_specs=in_specs,
        out_specs=row(D_MODEL),
        compiler_params=_cparams(("arbitrary",)),
        name="post_mlp",
    )(x, mods, *a_list, *wo_list, g1, b1, w1, w2, g2, b2)


def _rot_cols(w):
    parts = []
    for seg in range(2):
        o = seg * 32
        parts += [-w[:, o + 16:o + 32], w[:, o:o + 16]]
    return jnp.concatenate(parts, axis=1)


def _pad_cols(w, n):
    return jnp.pad(w, ((0, 0), (0, n - w.shape[1])))


def _rope_tables(L):
    rows = L // GRID_W
    row = np.repeat(np.arange(rows, dtype=np.float64), GRID_W)
    col = np.tile(np.arange(GRID_W, dtype=np.float64), rows)
    half = QK_ROPE // 2
    inv = 1.0 / (ROPE_THETA ** (np.arange(0, half, 2, dtype=np.float64) / half))
    ar = row[:, None] * inv[None, :]
    ac = col[:, None] * inv[None, :]
    ang = np.concatenate([ar, ar, ac, ac], axis=1)
    cos = np.concatenate([np.cos(ang), np.ones_like(ang)], axis=1)
    sin = np.concatenate([np.sin(ang), np.zeros_like(ang)], axis=1)
    return jnp.asarray(cos, F32), jnp.asarray(sin, F32)


def kernel(x_prompt, x_sample, cache_l0_ckv, cache_l0_krope, c, c_ctx, l0_ada_w, l0_ada_b, l0_w_in, l0_conv_w, l0_conv_b, l0_hf_w1, l0_hf_b1, l0_hf_freq, l0_hf_w2, l0_hf_b2, l0_hf_w3, l0_hf_skip, l0_q_norm, l0_q_up, l0_kv_norm, l0_kv_up, l0_w_out, l0_ln1_g, l0_ln1_b, l0_mlp_w1, l0_mlp_w2, l0_ln2_g, l0_ln2_b, l1_ada_w, l1_ada_b, l1_w_out, l1_ln1_g, l1_ln1_b, l1_mlp_w1, l1_mlp_w2, l1_ln2_g, l1_ln2_b):
    nbc, lc, d = x_prompt.shape
    nbs, ls, _ = x_sample.shape
    past = cache_l0_ckv.shape[1]
    tm = ROW_TILE
    row1 = lambda v: v.reshape(1, -1)

    cond8 = jnp.concatenate([c_ctx[None, :], c, jnp.zeros((8 - 1 - nbs, d), F32)], axis=0)
    mods0 = _modulation(cond8, l0_ada_w, l0_ada_b)
    mods1 = _modulation(cond8, l1_ada_w, l1_ada_b)

    kpe_w = l0_w_in[:, 1920:1984]
    win = jnp.concatenate([l0_w_in[:, :1920], _pad_cols(kpe_w, LANE), _pad_cols(_rot_cols(kpe_w), LANE)],
                          axis=1).astype(BF16)
    dh = QK_NOPE + QK_ROPE
    q_nope = [l0_q_up[:, h * dh:h * dh + QK_NOPE] for h in range(MLA_HEADS)]
    q_pe = [l0_q_up[:, h * dh + QK_NOPE:(h + 1) * dh] for h in range(MLA_HEADS)]
    qup = jnp.concatenate(q_nope + [_pad_cols(w, LANE) for w in q_pe]
                          + [_pad_cols(_rot_cols(w), LANE) for w in q_pe], axis=1).astype(BF16)
    kvup = l0_kv_up.astype(BF16)
    front_w = (win, row1(l0_q_norm), qup, row1(l0_kv_norm), kvup, l0_conv_w, row1(l0_conv_b), row1(l0_hf_skip))
    w1p = jnp.pad(l0_hf_w1, ((0, LANE - l0_hf_w1.shape[0]), (0, 0)))
    filt_w = (w1p, row1(l0_hf_b1), row1(l0_hf_freq), l0_hf_w2, row1(l0_hf_b2), l0_hf_w3.astype(BF16))
    wo0 = l0_w_out.astype(BF16)

    xc = x_prompt.reshape(nbc * lc, d)
    xs = x_sample.reshape(nbs * ls, d)
    groups = (
        dict(x=xc, nb=nbc, L=lc, mod_base=0, tiles_per_mod=nbc * lc // tm, dft_nb=4, tq=lc, hps=MLA_HEADS),
        dict(x=xs, nb=nbs, L=ls, mod_base=1, tiles_per_mod=ls // tm, dft_nb=nbs, tq=512, hps=1),
    )
    ones_tab = (jnp.concatenate([jnp.ones((tm, LANE), F32)], axis=0), jnp.zeros((tm, LANE), F32))

    outs = []
    ctx_ckv = ctx_krope = None
    for gi, g in enumerate(groups):
        nb, L = g["nb"], g["L"]
        tiles_per_seq = L // tm
        latent = gi == 1
        cos, sin = _rope_tables(L) if latent else ones_tab
        two_stage = L == FFT_R * FFT_R
        io_dtype = F32 if two_stage else BF16
        u, e, x0, q, k, vt, kvn, kpe = _front(g["x"], mods0, g["mod_base"], g["tiles_per_mod"], front_w,
                                              cos, sin, latent, tiles_per_seq, io_dtype)
        if latent:
            extra = _cache_kv(cache_l0_ckv.reshape(nbs * past, KV_LORA),
                              _pad_cols(cache_l0_krope.reshape(nbs * past, QK_ROPE), LANE), kvup)
        else:
            extra = None
            ctx_ckv = kvn.reshape(nb, L, KV_LORA)
            ctx_krope = kpe.reshape(nb, L, QK_ROPE)
        y_mla = _attention(q, k, vt, extra, nb, L, g["tq"], g["hps"])

        hfilt, hnorm = _filters(L, io_dtype, *filt_w)
        if two_stage:
            y_hy = _hyena_long(u, e, x0, hfilt, hnorm, nb)
        else:
            kre, kim = _dft("filt", "hy_fwd", [hfilt], [hnorm], 2)
            sh = (nb, L, HY_CH)
            yre, yim = _dft("fwdk", "hy_fwd", [u.reshape(sh)], [kre, kim], g["dft_nb"])
            y_hy = _dft("inv", "hy_inv", [yre, yim], [e.reshape(sh), x0.reshape(sh)], g["dft_nb"])
            y_hy = y_hy.reshape(nb * L, HY_CH)

        x1 = _post(g["x"], mods0, g["mod_base"], g["tiles_per_mod"], [y_hy, y_mla],
                   [wo0[:HY_CH], wo0[HY_CH:]], row1(l0_ln1_g), row1(l0_ln1_b),
                   l0_mlp_w1.astype(BF16), l0_mlp_w2.astype(BF16), row1(l0_ln2_g), row1(l0_ln2_b))

        if L == FFT_R * FFT_R:
            yf = _fnet_long(x1, mods1, g["mod_base"], nb)
        else:
            zc, zs = _fnet_front(x1, mods1, g["mod_base"], g["tiles_per_mod"])
            sh = (nb, L, d)
            yf = _dft("fnet", "fnet", [zc.reshape(sh), zs.reshape(sh)], [], min(g["dft_nb"], 2))
            yf = yf.reshape(nb * L, d)
        x2 = _post(x1, mods1, g["mod_base"], g["tiles_per_mod"], [yf],
                   [l1_w_out.astype(BF16)], row1(l1_ln1_g), row1(l1_ln1_b),
                   l1_mlp_w1.astype(BF16), l1_mlp_w2.astype(BF16), row1(l1_ln2_g), row1(l1_ln2_b))
        outs.append(x2.reshape(nb, L, d))

    return (outs[0], outs[1], ctx_ckv, ctx_krope)
```

```python
import functools
import math

import numpy as np
import jax
import jax.numpy as jnp
from jax import lax
from jax.experimental import pallas as pl
from jax.experimental.pallas import tpu as pltpu

F32 = jnp.float32
BF16 = jnp.bfloat16
HI = lax.Precision.HIGHEST

D_MODEL = 1024
DEPTH = 2
GRID_W = 64
HY_CH = 512
FILT_BANDS = 16
FILT_ORDER = 64
FAST_DECAY_PCT = 0.3
SLOW_DECAY_PCT = 1.5
DECAY_TARGET = 1e-2
MAX_DECAY = math.log(DECAY_TARGET) / FAST_DECAY_PCT
MIN_DECAY = math.log(DECAY_TARGET) / SLOW_DECAY_PCT
MLA_HEADS = 4
QK_NOPE = 128
QK_ROPE = 64
V_HEAD = 128
Q_LORA = 256
KV_LORA = 128
ROPE_THETA = 10000.0
FN_GROUP_CH = 128
D_FF = 4096
ALPHA = (2 * DEPTH) ** 0.25
LN_EPS = 1e-5
RMS_EPS = 1e-6

LANE = 128
ROW_TILE = 256
POST_TILE = 512
POST_SPLIT = 2
QK_PAD = 256
VT_ROWS = V_HEAD + 16
LOG2E = 1.4426950408889634
VMEM_LIMIT = 56 * 1024 * 1024


def _cparams(sem):
    return pltpu.CompilerParams(dimension_semantics=sem, vmem_limit_bytes=VMEM_LIMIT)


def _ln_plain(x):
    mu = jnp.mean(x, axis=-1, keepdims=True)
    xc = x - mu
    var = jnp.mean(xc * xc, axis=-1, keepdims=True)
    return xc * lax.rsqrt(var + LN_EPS)


def _rms(x, g):
    return x * lax.rsqrt(jnp.mean(x * x, axis=-1, keepdims=True) + RMS_EPS) * g


def _bdot(a, b):
    return jnp.dot(a.astype(BF16), b, preferred_element_type=F32)


def _vt_rows(v):
    ones = jnp.ones((VT_ROWS - V_HEAD, v.shape[0]), BF16)
    return jnp.concatenate([jnp.transpose(v).astype(BF16), ones], axis=0)


MOD_STREAMS = 3


def _mod_kernel(c_ref, *refs):
    w_refs, b_ref, o_ref = refs[:MOD_STREAMS], refs[MOD_STREAMS], refs[MOD_STREAMS + 1]
    c = c_ref[...]
    s = c / (1.0 + jnp.exp(-c))
    s_hi = s.astype(BF16)
    s_lo = (s - s_hi.astype(F32)).astype(BF16)
    s2 = jnp.concatenate([s_hi, s_lo], axis=0)
    nr = s.shape[0]
    tn = w_refs[0].shape[1]
    for k, w_ref in enumerate(w_refs):
        w = w_ref[...]
        w_hi = w.astype(BF16)
        w_lo = (w - w_hi.astype(F32)).astype(BF16)
        r1 = jnp.dot(s2, w_hi, preferred_element_type=F32)
        r2 = jnp.dot(s_hi, w_lo, preferred_element_type=F32)
        o_ref[:, k * tn:(k + 1) * tn] = r1[:nr] + r1[nr:] + r2 + b_ref[:, k * tn:(k + 1) * tn]


def _modulation(cond8, w, b):
    n = w.shape[1]
    tn = 512
    step = MOD_STREAMS * tn
    w_specs = [pl.BlockSpec((D_MODEL, tn), functools.partial(lambda j, k: (0, MOD_STREAMS * j + k), k=k))
               for k in range(MOD_STREAMS)]
    out = pl.pallas_call(
        _mod_kernel,
        out_shape=jax.ShapeDtypeStruct((8, n), F32),
        grid=(n // step,),
        in_specs=[pl.BlockSpec((8, D_MODEL), lambda j: (0, 0))] + w_specs
                 + [pl.BlockSpec((1, step), lambda j: (0, j))],
        out_specs=pl.BlockSpec((8, step), lambda j: (0, j)),
        compiler_params=_cparams(("arbitrary",)),
        name="modulation",
    )(cond8, *([w] * MOD_STREAMS), b.reshape(1, n))
    return out.reshape(8, 1, n)


def _mod_spec(mod_base, tiles_per_mod):
    return pl.BlockSpec((None, 1, 6 * D_MODEL), lambda i: (mod_base + i // tiles_per_mod, 0, 0))


def _const_spec(shape):
    nd = len(shape)
    return pl.BlockSpec(shape, lambda i: (0,) * nd)


HALO = 8


def _front_kernel(tiles_per_seq, x_ref, xp_ref, xn_ref, m_ref, win_ref, qn_ref, qup_ref, kvn_ref, kvup_ref,
                  cos_ref, sin_ref, cw_ref, cb_ref, skip_ref,
                  u_ref, e_ref, x0_ref, q_ref, k_ref, v_ref, kvn_out_ref, kpe_ref):
    i = pl.program_id(0)
    m = m_ref[...]
    tm = x_ref.shape[0]
    nh = 3 * HY_CH
    xe = jnp.concatenate([xp_ref[...], x_ref[...], xn_ref[...]], axis=0)
    he = _ln_plain(xe) * (1.0 + m[:, D_MODEL:2 * D_MODEL]) + m[:, 0:D_MODEL]

    zh = _bdot(he, win_ref[:, :nh])
    pos = i % tiles_per_seq
    rows = lax.broadcasted_iota(jnp.int32, (tm + 2 * HALO, 1), 0)
    inside = jnp.logical_and(jnp.logical_or(rows >= HALO, pos != 0),
                             jnp.logical_or(rows < tm + HALO, pos != tiles_per_seq - 1))
    zh = jnp.where(inside, zh, 0.0)
    cw = cw_ref[...]
    pz = (pltpu.roll(zh, 1, 0) * cw[0:1, :] + zh * cw[1:2, :]
          + pltpu.roll(zh, tm + 2 * HALO - 1, 0) * cw[2:3, :])[HALO:HALO + tm] + cb_ref[...]
    u = pz[:, 2 * HY_CH:] * pz[:, HY_CH:2 * HY_CH]
    u_ref[...] = u.astype(u_ref.dtype)
    e_ref[...] = u * skip_ref[...]
    x0_ref[...] = pz[:, :HY_CH]

    z = _bdot(he[HALO:HALO + tm], win_ref[:, nh:])
    q_c = z[:, 0:256]
    kv_c = z[:, 256:384]
    cos = cos_ref[...]
    sin = sin_ref[...]
    kpe = z[:, 384:512] * cos + z[:, 512:640] * sin
    kpe_ref[...] = kpe[:, :QK_ROPE]
    kpe_b = kpe.astype(BF16)
    q = _bdot(_rms(q_c, qn_ref[...]), qup_ref[...]) * (LOG2E / math.sqrt(QK_NOPE + QK_ROPE))
    kvn = _rms(kv_c, kvn_ref[...])
    kvn_out_ref[...] = kvn
    kv = _bdot(kvn, kvup_ref[...])
    for hd in range(MLA_HEADS):
        a = hd * LANE
        q_pe = (q[:, 512 + a:512 + a + LANE] * cos + q[:, 1024 + a:1024 + a + LANE] * sin).astype(BF16)
        q_ref[hd] = jnp.concatenate([q[:, a:a + LANE].astype(BF16), q_pe], axis=-1)
        k_ref[hd] = jnp.concatenate([kv[:, 2 * a:2 * a + LANE].astype(BF16), kpe_b], axis=-1)
        v_ref[hd] = _vt_rows(kv[:, 2 * a + LANE:2 * a + 2 * LANE])


def _front(x, mods, mod_base, tiles_per_mod, w, cos, sin, rope, tiles_per_seq, u_dtype):
    t = x.shape[0]
    tm = ROW_TILE
    win, qn, qup, kvn, kvup, conv_w, conv_b, skip = w
    if rope:
        tab_spec = pl.BlockSpec((tm, LANE), lambda i: (i % tiles_per_seq, 0))
    else:
        tab_spec = pl.BlockSpec((tm, LANE), lambda i: (0, 0))
    r8 = tm // HALO
    n8 = t // HALO
    hy_out = lambda dt: jax.ShapeDtypeStruct((t, HY_CH), dt)
    hy_spec = pl.BlockSpec((tm, HY_CH), lambda i: (i, 0))
    return pl.pallas_call(
        functools.partial(_front_kernel, tiles_per_seq),
        out_shape=(hy_out(u_dtype), hy_out(F32), hy_out(F32),
                   jax.ShapeDtypeStruct((MLA_HEADS, t, QK_PAD), BF16),
                   jax.ShapeDtypeStruct((MLA_HEADS, t, QK_PAD), BF16),
                   jax.ShapeDtypeStruct((MLA_HEADS, VT_ROWS, t), BF16),
                   jax.ShapeDtypeStruct((t, KV_LORA), F32),
                   jax.ShapeDtypeStruct((t, QK_ROPE), F32)),
        grid=(t // tm,),
        in_specs=[pl.BlockSpec((tm, D_MODEL), lambda i: (i, 0)),
                  pl.BlockSpec((HALO, D_MODEL), lambda i: (jnp.maximum(i * r8 - 1, 0), 0)),
                  pl.BlockSpec((HALO, D_MODEL), lambda i: (jnp.minimum((i + 1) * r8, n8 - 1), 0)),
                  _mod_spec(mod_base, tiles_per_mod),
                  _const_spec(win.shape), _const_spec(qn.shape), _const_spec(qup.shape),
                  _const_spec(kvn.shape), _const_spec(kvup.shape),
                  tab_spec, tab_spec,
                  _const_spec(conv_w.shape), _const_spec(conv_b.shape), _const_spec(skip.shape)],
        out_specs=(hy_spec, hy_spec, hy_spec,
                   pl.BlockSpec((MLA_HEADS, tm, QK_PAD), lambda i: (0, i, 0)),
                   pl.BlockSpec((MLA_HEADS, tm, QK_PAD), lambda i: (0, i, 0)),
                   pl.BlockSpec((MLA_HEADS, VT_ROWS, tm), lambda i: (0, 0, i)),
                   pl.BlockSpec((tm, KV_LORA), lambda i: (i, 0)),
                   pl.BlockSpec((tm, QK_ROPE), lambda i: (i, 0))),
        compiler_params=_cparams(("arbitrary",)),
        name="l0_front",
    )(x, x, x, mods, win, qn, qup, kvn, kvup, cos, sin, conv_w, conv_b, skip)


def _cache_kv_kernel(ckv_ref, kr_ref, kvup_ref, k_ref, v_ref):
    kv = _bdot(ckv_ref[...], kvup_ref[...])
    kr = kr_ref[...].astype(BF16)
    for hd in range(MLA_HEADS):
        a = 2 * hd * LANE
        k_ref[hd] = jnp.concatenate([kv[:, a:a + LANE].astype(BF16), kr], axis=-1)
        v_ref[hd] = _vt_rows(kv[:, a + LANE:a + 2 * LANE])


def _cache_kv(ckv, krope_pad, kvup):
    t = ckv.shape[0]
    return pl.pallas_call(
        _cache_kv_kernel,
        out_shape=(jax.ShapeDtypeStruct((MLA_HEADS, t, QK_PAD), BF16),
                   jax.ShapeDtypeStruct((MLA_HEADS, VT_ROWS, t), BF16)),
        name="l0_cache_kv",
    )(ckv, krope_pad, kvup)


def _col_reduce(x, op):
    rows, n = x.shape
    for g in (32, 8):
        if rows % (8 * g) == 0 and rows > 8 * g:
            x = op(x.reshape(rows // (8 * g), 8 * g, n), axis=0)
            rows = 8 * g
    return op(x, axis=0, keepdims=True)


def _attn_kernel(n_kv, hps, q_ref, *refs):
    k_refs, vt_refs = refs[:n_kv], refs[n_kv:2 * n_kv]
    o_ref, s_even, s_odd = refs[2 * n_kv:]
    i = pl.program_id(0)

    @pl.when(i == 0)
    def _():
        s_odd[...] = jnp.zeros_like(s_odd)

    def step(s_write, s_read):
        nt = (((1,), (1,)), ((), ()))
        for h in range(hps):
            q = q_ref[h]
            r0 = 0
            for k_ref in k_refs:
                lk = k_ref.shape[1]
                s_write[h, r0:r0 + lk, :] = lax.dot_general(k_ref[h], q, nt, preferred_element_type=F32)
                r0 += lk
        for h in range(hps):
            s = s_read[h]
            m = _col_reduce(s, jnp.max)
            pb = jnp.exp2(s - m).astype(BF16)
            acc = None
            r0 = 0
            for vt_ref in vt_refs:
                lk = vt_ref.shape[2]
                pv = jnp.dot(vt_ref[h], pb[r0:r0 + lk, :], preferred_element_type=F32)
                acc = pv if acc is None else acc + pv
                r0 += lk
            o_ref[:, h * V_HEAD:(h + 1) * V_HEAD] = jnp.transpose(
                acc[:V_HEAD] / acc[V_HEAD:V_HEAD + 1]).astype(o_ref.dtype)

    pl.when(i % 2 == 0)(lambda: step(s_even, s_odd))
    pl.when(i % 2 == 1)(lambda: step(s_odd, s_even))


def _attention(q, k, vt, extra, nb, lq, tq, hps):
    nq = lq // tq
    ng = MLA_HEADS // hps
    n_tiles = nb * ng * nq

    def where(t):
        bh = t // nq
        return bh // ng, bh % ng, t % nq

    def score_side(fn):
        return lambda i: fn(*where(jnp.minimum(i, n_tiles - 1)))

    def value_side(fn):
        return lambda i: fn(*where(jnp.maximum(i - 1, 0)))

    ks, vts = [k], [vt]
    if extra is not None:
        ks.append(extra[0])
        vts.append(extra[1])
    in_specs = [pl.BlockSpec((hps, tq, QK_PAD), score_side(lambda b, h, j: (h, b * nq + j, 0)))]
    in_specs += [pl.BlockSpec((hps, a.shape[1] // nb, QK_PAD), score_side(lambda b, h, j: (h, b, 0))) for a in ks]
    in_specs += [pl.BlockSpec((hps, VT_ROWS, a.shape[2] // nb), value_side(lambda b, h, j: (h, 0, b))) for a in vts]
    lk_total = sum(a.shape[1] // nb for a in ks)
    return pl.pallas_call(
        functools.partial(_attn_kernel, len(ks), hps),
        out_shape=jax.ShapeDtypeStruct((nb * lq, MLA_HEADS * V_HEAD), BF16),
        grid=(n_tiles + 1,),
        in_specs=in_specs,
        out_specs=pl.BlockSpec((tq, hps * V_HEAD), value_side(lambda b, h, j: (b * nq + j, h))),
        scratch_shapes=[pltpu.VMEM((hps, lk_total, tq), F32), pltpu.VMEM((hps, lk_total, tq), F32)],
        compiler_params=_cparams(("arbitrary",)),
        name="l0_attention",
    )(q, *ks, *vts)


def _filter_kernel(z_ref, w1_ref, b1_ref, fr_ref, w2_ref, b2_ref, w3_ref, dl_ref, h_ref, norm_ref):
    i = pl.program_id(0)
    z = z_ref[...]
    tl = z.shape[0]
    fr = fr_ref[...]
    z2 = jnp.concatenate([z[:tl // 2], z[tl // 2:]], axis=1)
    h = jnp.sin(fr * (jnp.dot(z2, w1_ref[...], precision=HI, preferred_element_type=F32) + b1_ref[...]))
    h = jnp.sin(fr * (jnp.dot(h, w2_ref[...], precision=HI, preferred_element_type=F32) + b2_ref[...]))
    h = _bdot(h, w3_ref[...])
    h = jnp.concatenate([h[:, :2 * HY_CH], h[:, 2 * HY_CH:]], axis=0)
    decay = jnp.exp(-(z[:, 0:1] * dl_ref[...]))
    hf = h[:, :HY_CH] * decay
    hb = h[:, HY_CH:] * decay
    part = jnp.sum(jnp.abs(hf) + jnp.abs(hb), axis=0, keepdims=True)

    @pl.when(i == 0)
    def _():
        norm_ref[...] = part

    @pl.when(i > 0)
    def _():
        norm_ref[...] += part

    rows = lax.broadcasted_iota(jnp.int32, hb.shape, 0) + i * tl
    h_ref[0] = hf.astype(h_ref.dtype)
    h_ref[1] = jnp.where(rows == 0, 0.0, hb).astype(h_ref.dtype)


def _filter_embedding(L):
    t = np.linspace(0.0, 1.0, L)[:, None]
    w_ang = 2.0 * np.pi * np.arange(L) / L
    bands = np.linspace(1e-4, FILT_BANDS - 1, FILT_BANDS)
    ang = w_ang[:, None] * bands[None, :]
    z = np.zeros((L, LANE), np.float64)
    z[:, 0:1] = t
    z[:, 1:1 + FILT_BANDS] = np.cos(ang)
    z[:, 1 + FILT_BANDS:1 + 2 * FILT_BANDS] = -np.sin(ang)
    return jnp.asarray(z, F32)


def _filters(L, h_dtype, w1p, b1, fr, w2, b2, w3):
    tl = min(L, 512)
    z = _filter_embedding(L)
    deltas = jnp.asarray(np.abs(np.linspace(MIN_DECAY, MAX_DECAY, HY_CH))[None, :], F32)
    return pl.pallas_call(
        _filter_kernel,
        out_shape=(jax.ShapeDtypeStruct((2, L, HY_CH), h_dtype), jax.ShapeDtypeStruct((1, HY_CH), F32)),
        grid=(L // tl,),
        in_specs=[pl.BlockSpec((tl, LANE), lambda i: (i, 0)),
                  _const_spec(w1p.shape), _const_spec(b1.shape), _const_spec(fr.shape),
                  _const_spec(w2.shape), _const_spec(b2.shape), _const_spec(w3.shape),
                  _const_spec(deltas.shape)],
        out_specs=(pl.BlockSpec((2, tl, HY_CH), lambda i: (0, i, 0)),
                   pl.BlockSpec((1, HY_CH), lambda i: (0, 0))),
        compiler_params=_cparams(("arbitrary",)),
        name="l0_hyena_filters",
    )(z, w1p, b1, fr, w2, b2, w3, deltas)


def _dft_tables(kind, L, ti):
    ni = L // ti
    i = np.arange(ti, dtype=np.int64)[:, None]
    big = (np.arange(ni, dtype=np.int64) * ti)[:, None]
    c = np.arange(L, dtype=np.int64)[None, :]
    if kind == "hy_fwd":
        period = 4 * L
        base_idx = (2 * i + 1) * c
        r_idx = 2 * big * c
        scale = 1.0
    elif kind == "hy_inv":
        period = 4 * L
        base_idx = (2 * c + 1) * i
        r_idx = (2 * c + 1) * big
        scale = 1.0 / L
    else:
        period = L
        base_idx = i * c
        r_idx = big * c
        scale = 1.0 / math.sqrt(L * FN_GROUP_CH)
    ab = 2.0 * np.pi * (base_idx % period) / period
    ar = 2.0 * np.pi * (r_idx % period) / period
    return (jnp.asarray(np.cos(ab), F32), jnp.asarray(np.sin(ab), F32),
            jnp.asarray(scale * np.cos(ar), F32).reshape(ni, 1, L),
            jnp.asarray(scale * np.sin(ar), F32).reshape(ni, 1, L))


def _dft_kernel(mode, nb, n_x, *refs):
    bc_ref, bs_ref, rc_ref, rs_ref = refs[:4]
    x_refs = refs[4:4 + n_x]
    rest = refs[4 + n_x:]
    p_ref, q_ref = rest[-2], rest[-1]
    j = pl.program_id(2)
    nj = pl.num_programs(2)
    tj = x_refs[0].shape[1]
    if bc_ref.shape[1] == tj:
        bc, bs, rc, rs = bc_ref[...], bs_ref[...], rc_ref[...], rs_ref[...]
    else:
        off = pl.multiple_of(j * tj, tj)
        bc, bs = bc_ref[:, pl.ds(off, tj)], bs_ref[:, pl.ds(off, tj)]
        rc, rs = rc_ref[:, pl.ds(off, tj)], rs_ref[:, pl.ds(off, tj)]
    tc = (bc * rc - bs * rs).astype(BF16)
    ts = (bs * rc + bc * rs).astype(BF16)
    x1_ref = x_refs[0]
    x2_ref = x_refs[-1]

    pq = [(jnp.dot(tc, x1_ref[b], preferred_element_type=F32),
           jnp.dot(ts, x2_ref[b], preferred_element_type=F32)) for b in range(nb)]

    @pl.when(j == 0)
    def _():
        for b in range(nb):
            p_ref[b] = pq[b][0]
            q_ref[b] = pq[b][1]

    @pl.when(j > 0)
    def _():
        for b in range(nb):
            p_ref[b] += pq[b][0]
            q_ref[b] += pq[b][1]

    @pl.when(j == nj - 1)
    def _():
        if mode == "filt":
            nrm = rest[0][...]
            kre_ref, kim_ref = rest[1], rest[2]
            kre_ref[...] = (p_ref[0] + p_ref[1]) / nrm
            kim_ref[...] = (q_ref[1] - q_ref[0]) / nrm
        elif mode == "fwdk":
            kre, kim = rest[0][...], rest[1][...]
            yre_ref, yim_ref = rest[2], rest[3]
            for b in range(nb):
                pp, qq = p_ref[b], q_ref[b]
                yre_ref[b] = (pp * kre + qq * kim).astype(BF16)
                yim_ref[b] = (pp * kim - qq * kre).astype(BF16)
        elif mode == "inv":
            e_ref, x0_ref, o_ref = rest[0], rest[1], rest[2]
            for b in range(nb):
                o_ref[b] = ((p_ref[b] - q_ref[b] + e_ref[b]) * x0_ref[b]).astype(BF16)
        else:
            o_ref = rest[0]
            for b in range(nb):
                o_ref[b] = (p_ref[b] - q_ref[b]).astype(BF16)


def _dft(mode, kind, xs, extras, nb):
    B, L, C = xs[0].shape
    ti = min(L, 256)
    tj = min(L, 512)
    bc, bs, rc, rs = _dft_tables(kind, L, ti)
    grid = (B // nb, L // ti, L // tj)
    x_spec = pl.BlockSpec((nb, tj, C), lambda g, i, j: (g, j, 0))
    row_spec = lambda c, dt=None: pl.BlockSpec((nb, ti, c), lambda g, i, j: (g, i, 0))
    in_specs = [pl.BlockSpec((ti, L), lambda g, i, j: (0, 0)),
                pl.BlockSpec((ti, L), lambda g, i, j: (0, 0)),
                pl.BlockSpec((None, 1, L), lambda g, i, j: (i, 0, 0)),
                pl.BlockSpec((None, 1, L), lambda g, i, j: (i, 0, 0))] + [x_spec] * len(xs)
    if mode == "filt":
        in_specs += [pl.BlockSpec((1, HY_CH), lambda g, i, j: (0, 0))]
        out_shape = (jax.ShapeDtypeStruct((L, HY_CH), F32),) * 2
        out_specs = (pl.BlockSpec((ti, HY_CH), lambda g, i, j: (i, 0)),) * 2
    elif mode == "fwdk":
        in_specs += [pl.BlockSpec((ti, HY_CH), lambda g, i, j: (i, 0))] * 2
        out_shape = (jax.ShapeDtypeStruct((B, L, C), BF16),) * 2
        out_specs = (row_spec(C),) * 2
    elif mode == "inv":
        in_specs += [row_spec(C)] * 2
        out_shape = jax.ShapeDtypeStruct((B, L, C), BF16)
        out_specs = row_spec(C)
    else:
        out_shape = jax.ShapeDtypeStruct((B, L, C), BF16)
        out_specs = row_spec(C)
    return pl.pallas_call(
        functools.partial(_dft_kernel, mode, nb, len(xs)),
        out_shape=out_shape,
        grid=grid,
        in_specs=in_specs,
        out_specs=out_specs,
        scratch_shapes=[pltpu.VMEM((nb, ti, C), F32), pltpu.VMEM((nb, ti, C), F32)],
        compiler_params=_cparams(("arbitrary", "arbitrary", "arbitrary")),
        name="dft_" + mode,
    )(bc, bs, rc, rs, *xs, *extras)


FFT_R = 64
FFT_KF = 8


def _pack_pairs(x):
    return pltpu.bitcast(x.astype(BF16), jnp.uint32)


def _unpack_pairs(w):
    return pltpu.bitcast(w, BF16)


def _lead_in_kernel(g_ref, x_ref, o_ref):
    g = g_ref[...]
    xt = jnp.swapaxes(x_ref[...], 0, 1)
    ys = [_pack_pairs(jnp.dot(g, xt[j].astype(BF16), preferred_element_type=F32)) for j in range(FFT_KF)]
    o_ref[...] = jnp.swapaxes(jnp.stack(ys, axis=0), 0, 1)


def _lead_in(g, x, name):
    nbx, _, _, c = x.shape
    m2 = g.shape[0] // 2
    return pl.pallas_call(
        _lead_in_kernel,
        out_shape=jax.ShapeDtypeStruct((nbx, m2, FFT_R, c), jnp.uint32),
        grid=(nbx, FFT_R // FFT_KF),
        in_specs=[pl.BlockSpec(g.shape, lambda b, k: (0, 0)),
                  pl.BlockSpec((None, FFT_R, FFT_KF, c), lambda b, k: (b, 0, k, 0))],
        out_specs=pl.BlockSpec((None, m2, FFT_KF, c), lambda b, k: (b, 0, k, 0)),
        compiler_params=_cparams(("arbitrary", "arbitrary")),
        name=name,
    )(g, x)


def _lead_out_kernel(n_extra, g_ref, w_ref, *rest):
    g = g_ref[...]
    o_ref = rest[-1]
    wt = jnp.swapaxes(w_ref[...], 0, 1)
    ys = [jnp.dot(g, _unpack_pairs(wt[j]), preferred_element_type=F32) for j in range(FFT_KF)]
    y = jnp.swapaxes(jnp.stack(ys, axis=0), 0, 1)
    if n_extra:
        y = (y + rest[0][...]) * rest[1][...]
    o_ref[...] = y


def _lead_out(g, w, extras, name):
    nb, k2, _, c = w.shape
    blk = lambda r: pl.BlockSpec((None, r, FFT_KF, c), lambda b, k: (b, 0, k, 0))
    return pl.pallas_call(
        functools.partial(_lead_out_kernel, len(extras)),
        out_shape=jax.ShapeDtypeStruct((nb, FFT_R, FFT_R, c), F32),
        grid=(nb, FFT_R // FFT_KF),
        in_specs=[pl.BlockSpec(g.shape, lambda b, k: (0, 0)), blk(k2)] + [blk(FFT_R)] * len(extras),
        out_specs=blk(FFT_R),
        compiler_params=_cparams(("arbitrary", "arbitrary")),
        name=name,
    )(g, w, *extras)


def _interleave(a, b, axis):
    st = np.stack([a, b], axis=axis + 1)
    shape = list(a.shape)
    shape[axis] *= 2
    return st.reshape(shape)


def _hy2_tables():
    L = FFT_R * FFT_R
    n2 = 2 * L
    f1 = np.arange(2 * FFT_R, dtype=np.int64)
    s1 = np.arange(FFT_R, dtype=np.int64)
    th = np.pi * (((2 * f1[:, None] + 1) * s1[None, :]) % (4 * FFT_R)) / (2 * FFT_R)
    ga = _interleave(np.cos(th), -np.sin(th), 0)
    ma = _interleave(np.cos(th).T, -np.sin(th).T, 1) / L
    f2 = np.arange(FFT_R // 2, dtype=np.int64)
    s2 = np.arange(FFT_R, dtype=np.int64)
    idx = ((n2 // FFT_R) * 2 * f2[None, :, None] * s2[None, None, :]
           + (2 * f1[:, None, None] + 1) * s2[None, None, :]) % (2 * n2)
    al = np.pi * idx / n2
    c, s = np.cos(al), np.sin(al)
    nmat = np.concatenate([_interleave(c, s, 2), _interleave(-s, c, 2)], axis=1)
    ct, st = np.transpose(c, (0, 2, 1)), np.transpose(s, (0, 2, 1))
    mmat = _interleave(np.concatenate([ct, -st], axis=2), np.concatenate([st, ct], axis=2), 1)
    bf = lambda a: jnp.asarray(a, F32).astype(BF16)
    return bf(ga), bf(nmat), bf(mmat), bf(ma)


def _hy_mid_kernel(a_ref, n_ref, m_ref, k_ref, e_ref):
    half = FFT_R // 2
    ts = [jnp.dot(n_ref[j], _unpack_pairs(a_ref[j]), preferred_element_type=F32)
          for j in range(FFT_KF)]
    ys = []
    for j in range(FFT_KF):
        tr, ti = ts[j][:half], ts[j][half:]
        kr, ki = k_ref[j, 0], k_ref[j, 1]
        ys.append(jnp.concatenate([tr * kr - ti * ki, tr * ki + ti * kr], axis=0).astype(BF16))
    for j in range(FFT_KF):
        e_ref[j] = _pack_pairs(jnp.dot(m_ref[j], ys[j], preferred_element_type=F32))


def _hy_mid(a, nmat, mmat, khat):
    nb = a.shape[0]
    c = a.shape[-1]
    nf1 = 2 * FFT_R
    blk = pl.BlockSpec((None, FFT_KF, FFT_R, c), lambda i, b: (b, i, 0, 0))
    return pl.pallas_call(
        _hy_mid_kernel,
        out_shape=jax.ShapeDtypeStruct(a.shape, jnp.uint32),
        grid=(nf1 // FFT_KF, nb),
        in_specs=[blk,
                  pl.BlockSpec((FFT_KF, FFT_R, 2 * FFT_R), lambda i, b: (i, 0, 0)),
                  pl.BlockSpec((FFT_KF, 2 * FFT_R, FFT_R), lambda i, b: (i, 0, 0)),
                  pl.BlockSpec((FFT_KF, 2, FFT_R // 2, c), lambda i, b: (i, 0, 0, 0))],
        out_specs=blk,
        compiler_params=_cparams(("arbitrary", "arbitrary")),
        name="l0_hyena_mid",
    )(a, nmat, mmat, khat)


def _hy_kfilt_kernel(a_ref, n_ref, nrm_ref, k_ref):
    half = FFT_R // 2
    nrm = nrm_ref[...]
    for j in range(FFT_KF):
        tf = jnp.dot(n_ref[j], _unpack_pairs(a_ref[0, j]), preferred_element_type=F32)
        tb = jnp.dot(n_ref[j], _unpack_pairs(a_ref[1, j]), preferred_element_type=F32)
        k_ref[j, 0] = (tf[:half] + tb[:half]) / nrm
        k_ref[j, 1] = (tf[half:] - tb[half:]) / nrm


def _hy_kfilt(a, nmat, nrm):
    c = a.shape[-1]
    nf1 = 2 * FFT_R
    return pl.pallas_call(
        _hy_kfilt_kernel,
        out_shape=jax.ShapeDtypeStruct((nf1, 2, FFT_R // 2, c), F32),
        grid=(nf1 // FFT_KF,),
        in_specs=[pl.BlockSpec((2, FFT_KF, FFT_R, c), lambda i: (0, i, 0, 0)),
                  pl.BlockSpec((FFT_KF, FFT_R, 2 * FFT_R), lambda i: (i, 0, 0)),
                  _const_spec(nrm.shape)],
        out_specs=pl.BlockSpec((FFT_KF, 2, FFT_R // 2, c), lambda i: (i, 0, 0, 0)),
        compiler_params=_cparams(("arbitrary",)),
        name="l0_hyena_kfilt",
    )(a, nmat, nrm)


def _hyena_long(u, e, x0, hfilt, nrm, nb):
    L = FFT_R * FFT_R
    c = u.shape[-1]
    v4 = lambda a, n: a.reshape(n, FFT_R, FFT_R, c)
    ga, nmat, mmat, ma = _hy2_tables()
    khat = _hy_kfilt(_lead_in(ga, v4(hfilt, 2), "l0_hyena_fwd_a"), nmat, nrm)
    ee = _hy_mid(_lead_in(ga, v4(u, nb), "l0_hyena_fwd_a"), nmat, mmat, khat)
    y = _lead_out(ma, ee, [v4(e, nb), v4(x0, nb)], "l0_hyena_inv_a")
    return y.reshape(nb * L, c)


def _fn2_tables():
    L = FFT_R * FFT_R
    r = np.arange(FFT_R, dtype=np.int64)
    idx = (FFT_R * r[None, :, None] * r[None, None, :] + r[None, :, None] * r[:, None, None]) % L
    gm = 2.0 * np.pi * idx / L
    c, s = np.cos(gm), np.sin(gm)
    g1 = _interleave(np.concatenate([c, -s], axis=2), np.concatenate([-s, -c], axis=2), 1)
    dl = 2.0 * np.pi * ((r[:, None] * r[None, :]) % FFT_R) / FFT_R
    g2 = _interleave(np.cos(dl), np.sin(dl), 1) / math.sqrt(L * FN_GROUP_CH)
    bf = lambda a: jnp.asarray(a, F32).astype(BF16)
    return bf(g1), bf(g2)


def _fnet_s1_kernel(x_ref, m_ref, cs_ref, g1_ref, o_ref, zc_ref, zs_ref):
    xs = jnp.swapaxes(x_ref[...], 0, 1).reshape(FFT_KF * FFT_R, D_MODEL)
    m = m_ref[...]
    h = (_ln_plain(xs) * (1.0 + m[:, D_MODEL:2 * D_MODEL]) + m[:, 0:D_MODEL]).astype(BF16)
    cs = cs_ref[...]
    for g in range(D_MODEL // FN_GROUP_CH):
        a = g * FN_GROUP_CH
        z = jnp.dot(h[:, a:a + FN_GROUP_CH], cs, preferred_element_type=F32)
        zc_ref[:, a:a + FN_GROUP_CH] = z[:, :FN_GROUP_CH].astype(BF16)
        zs_ref[:, a:a + FN_GROUP_CH] = z[:, FN_GROUP_CH:].astype(BF16)
    for j in range(FFT_KF):
        r0 = j * FFT_R
        s = jnp.concatenate([zc_ref[r0:r0 + FFT_R, :], zs_ref[r0:r0 + FFT_R, :]], axis=0)
        o_ref[j] = _pack_pairs(jnp.dot(g1_ref[j], s, preferred_element_type=F32))


def _fnet_long(x, mods, mod_base, nb):
    L = FFT_R * FFT_R
    d = D_MODEL
    g1, g2 = _fn2_tables()
    cs = _group_dft_table()
    bb = pl.pallas_call(
        _fnet_s1_kernel,
        out_shape=jax.ShapeDtypeStruct((nb, FFT_R, FFT_R, d), jnp.uint32),
        grid=(nb, FFT_R // FFT_KF),
        in_specs=[pl.BlockSpec((None, FFT_R, FFT_KF, d), lambda b, k: (b, 0, k, 0)),
                  pl.BlockSpec((None, 1, 6 * d), lambda b, k: (mod_base + b, 0, 0)),
                  pl.BlockSpec(cs.shape, lambda b, k: (0, 0)),
                  pl.BlockSpec((FFT_KF, 2 * FFT_R, 2 * FFT_R), lambda b, k: (k, 0, 0))],
        out_specs=pl.BlockSpec((None, FFT_KF, FFT_R, d), lambda b, k: (b, k, 0, 0)),
        scratch_shapes=[pltpu.VMEM((FFT_KF * FFT_R, d), BF16), pltpu.VMEM((FFT_KF * FFT_R, d), BF16)],
        compiler_params=_cparams(("arbitrary", "arbitrary")),
        name="l1_fnet_stage1",
    )(x.reshape(nb, FFT_R, FFT_R, d), mods, cs, g1)
    y = _lead_out(g2, bb, [], "l1_fnet_stage2")
    return y.reshape(nb * L, d)


def _group_dft_table():
    g = FN_GROUP_CH
    jk = (np.arange(g, dtype=np.int64)[:, None] * np.arange(g, dtype=np.int64)[None, :]) % g
    ang = 2.0 * np.pi * jk / g
    return jnp.asarray(np.concatenate([np.cos(ang), np.sin(ang)], axis=1), F32).astype(BF16)


def _fnet_front_kernel(x_ref, m_ref, cs_ref, zc_ref, zs_ref):
    m = m_ref[...]
    h = (_ln_plain(x_ref[...]) * (1.0 + m[:, D_MODEL:2 * D_MODEL]) + m[:, 0:D_MODEL]).astype(BF16)
    cs = cs_ref[...]
    for g in range(D_MODEL // FN_GROUP_CH):
        a = g * FN_GROUP_CH
        z = jnp.dot(h[:, a:a + FN_GROUP_CH], cs, preferred_element_type=F32)
        zc_ref[:, a:a + FN_GROUP_CH] = z[:, :FN_GROUP_CH].astype(BF16)
        zs_ref[:, a:a + FN_GROUP_CH] = z[:, FN_GROUP_CH:].astype(BF16)


def _fnet_front(x, mods, mod_base, tiles_per_mod):
    t = x.shape[0]
    tm = ROW_TILE
    cs = _group_dft_table()
    return pl.pallas_call(
        _fnet_front_kernel,
        out_shape=(jax.ShapeDtypeStruct((t, D_MODEL), BF16),) * 2,
        grid=(t // tm,),
        in_specs=[pl.BlockSpec((tm, D_MODEL), lambda i: (i, 0)),
                  _mod_spec(mod_base, tiles_per_mod),
                  _const_spec(cs.shape)],
        out_specs=(pl.BlockSpec((tm, D_MODEL), lambda i: (i, 0)),) * 2,
        compiler_params=_cparams(("arbitrary",)),
        name="l1_fnet_front",
    )(x, mods, cs)


def _post_kernel(n_a, *refs):
    x_ref, m_ref = refs[0], refs[1]
    a_refs = refs[2:2 + n_a]
    wo_refs = refs[2 + n_a:2 + 2 * n_a]
    g1_ref, b1_ref, w1_ref, w2_ref, g2_ref, b2_ref, o_ref = refs[2 + 2 * n_a:]
    m = m_ref[...]
    d = D_MODEL
    tm = x_ref.shape[0]
    halves = [(r, r + tm // POST_SPLIT) for r in range(0, tm, tm // POST_SPLIT)]
    outs = []
    for r0, r1 in halves:
        out = _bdot(a_refs[0][r0:r1, :], wo_refs[0][...])
        for a_ref, wo_ref in zip(a_refs[1:], wo_refs[1:]):
            out += _bdot(a_ref[r0:r1, :], wo_ref[...])
        outs.append(out)
    x1s, hs = [], []
    for (r0, r1), out in zip(halves, outs):
        x1 = _ln_plain(ALPHA * x_ref[r0:r1, :] + m[:, 2 * d:3 * d] * out) * g1_ref[...] + b1_ref[...]
        x1s.append(x1)
        hs.append((_ln_plain(x1) * (1.0 + m[:, 4 * d:5 * d]) + m[:, 3 * d:4 * d]).astype(BF16))
    accs = []
    n_c = D_FF // d

    def up(h, c):
        hc = jnp.maximum(jnp.dot(h, w1_ref[:, c * d:(c + 1) * d], preferred_element_type=F32), 0.0)
        return (hc * hc).astype(BF16)

    for h in hs:
        acc = None
        nxt = up(h, 0)
        for c in range(n_c):
            cur = nxt
            if c + 1 < n_c:
                nxt = up(h, c + 1)
            part = jnp.dot(cur, w2_ref[c * d:(c + 1) * d, :], preferred_element_type=F32)
            acc = part if acc is None else acc + part
        accs.append(acc)
    for (r0, r1), x1, acc in zip(halves, x1s, accs):
        o_ref[r0:r1, :] = _ln_plain(ALPHA * x1 + m[:, 5 * d:6 * d] * acc) * g2_ref[...] + b2_ref[...]


def _post(x, mods, mod_base, tiles_per_mod, a_list, wo_list, g1, b1, w1, w2, g2, b2):
    t = x.shape[0]
    tm = POST_TILE
    row = lambda c: pl.BlockSpec((tm, c), lambda i: (i, 0))
    once = lambda v: pl.BlockSpec(v.shape, lambda i: (0,) * v.ndim, pipeline_mode=pl.Buffered(1))
    in_specs = ([row(D_MODEL), _mod_spec(mod_base, tiles_per_mod * ROW_TILE // tm)]
                + [row(a.shape[1]) for a in a_list]
                + [once(w) for w in wo_list]
                + [once(v) for v in (g1, b1, w1, w2, g2, b2)])
    return pl.pallas_call(
        functools.partial(_post_kernel, len(a_list)),
        out_shape=jax.ShapeDtypeStruct((t, D_MODEL), F32),
        grid=(t // tm,),
        in_specs=in_specs,
        out_specs=row(D_MODEL),
        compiler_params=_cparams(("arbitrary",)),
        name="post_mlp",
    )(x, mods, *a_list, *wo_list, g1, b1, w1, w2, g2, b2)


def _rot_cols(w):
    parts = []
    for seg in range(2):
        o = seg * 32
        parts += [-w[:, o + 16:o + 32], w[:, o:o + 16]]
    return jnp.concatenate(parts, axis=1)


def _pad_cols(w, n):
    return jnp.pad(w, ((0, 0), (0, n - w.shape[1])))


def _block_diag2(w):
    z = jnp.zeros_like(w)
    return jnp.concatenate([jnp.concatenate([w, z], axis=1), jnp.concatenate([z, w], axis=1)], axis=0)


def _rope_tables(L):
    rows = L // GRID_W
    row = np.repeat(np.arange(rows, dtype=np.float64), GRID_W)
    col = np.tile(np.arange(GRID_W, dtype=np.float64), rows)
    half = QK_ROPE // 2
    inv = 1.0 / (ROPE_THETA ** (np.arange(0, half, 2, dtype=np.float64) / half))
    ar = row[:, None] * inv[None, :]
    ac = col[:, None] * inv[None, :]
    ang = np.concatenate([ar, ar, ac, ac], axis=1)
    cos = np.concatenate([np.cos(ang), np.ones_like(ang)], axis=1)
    sin = np.concatenate([np.sin(ang), np.zeros_like(ang)], axis=1)
    return jnp.asarray(cos, F32), jnp.asarray(sin, F32)


def kernel(x_prompt, x_sample, cache_l0_ckv, cache_l0_krope, c, c_ctx, l0_ada_w, l0_ada_b, l0_w_in, l0_conv_w, l0_conv_b, l0_hf_w1, l0_hf_b1, l0_hf_freq, l0_hf_w2, l0_hf_b2, l0_hf_w3, l0_hf_skip, l0_q_norm, l0_q_up, l0_kv_norm, l0_kv_up, l0_w_out, l0_ln1_g, l0_ln1_b, l0_mlp_w1, l0_mlp_w2, l0_ln2_g, l0_ln2_b, l1_ada_w, l1_ada_b, l1_w_out, l1_ln1_g, l1_ln1_b, l1_mlp_w1, l1_mlp_w2, l1_ln2_g, l1_ln2_b):
    nbc, lc, d = x_prompt.shape
    nbs, ls, _ = x_sample.shape
    past = cache_l0_ckv.shape[1]
    tm = ROW_TILE
    row1 = lambda v: v.reshape(1, -1)

    cond8 = jnp.concatenate([c_ctx[None, :], c, jnp.zeros((8 - 1 - nbs, d), F32)], axis=0)
    mods0 = _modulation(cond8, l0_ada_w, l0_ada_b)
    mods1 = _modulation(cond8, l1_ada_w, l1_ada_b)

    kpe_w = l0_w_in[:, 1920:1984]
    win = jnp.concatenate([l0_w_in[:, :1920], _pad_cols(kpe_w, LANE), _pad_cols(_rot_cols(kpe_w), LANE)],
                          axis=1).astype(BF16)
    dh = QK_NOPE + QK_ROPE
    q_nope = [l0_q_up[:, h * dh:h * dh + QK_NOPE] for h in range(MLA_HEADS)]
    q_pe = [l0_q_up[:, h * dh + QK_NOPE:(h + 1) * dh] for h in range(MLA_HEADS)]
    qup = jnp.concatenate(q_nope + [_pad_cols(w, LANE) for w in q_pe]
                          + [_pad_cols(_rot_cols(w), LANE) for w in q_pe], axis=1).astype(BF16)
    kvup = l0_kv_up.astype(BF16)
    front_w = (win, row1(l0_q_norm), qup, row1(l0_kv_norm), kvup, l0_conv_w, row1(l0_conv_b), row1(l0_hf_skip))
    w1p = jnp.pad(l0_hf_w1, ((0, LANE - l0_hf_w1.shape[0]), (0, 0)))
    two = lambda v: jnp.tile(row1(v), (1, 2))
    filt_w = (_block_diag2(w1p), two(l0_hf_b1), two(l0_hf_freq), _block_diag2(l0_hf_w2), two(l0_hf_b2),
              _block_diag2(l0_hf_w3).astype(BF16))
    wo0 = l0_w_out.astype(BF16)

    xc = x_prompt.reshape(nbc * lc, d)
    xs = x_sample.reshape(nbs * ls, d)
    groups = (
        dict(x=xc, nb=nbc, L=lc, mod_base=0, tiles_per_mod=nbc * lc // tm, dft_nb=4, tq=lc, hps=MLA_HEADS),
        dict(x=xs, nb=nbs, L=ls, mod_base=1, tiles_per_mod=ls // tm, dft_nb=nbs, tq=512, hps=1),
    )
    ones_tab = (jnp.concatenate([jnp.ones((tm, LANE), F32)], axis=0), jnp.zeros((tm, LANE), F32))

    outs = []
    ctx_ckv = ctx_krope = None
    for gi, g in enumerate(groups):
        nb, L = g["nb"], g["L"]
        tiles_per_seq = L // tm
        latent = gi == 1
        cos, sin = _rope_tables(L) if latent else ones_tab
        two_stage = L == FFT_R * FFT_R
        io_dtype = F32 if two_stage else BF16
        u, e, x0, q, k, vt, kvn, kpe = _front(g["x"], mods0, g["mod_base"], g["tiles_per_mod"], front_w,
                                              cos, sin, latent, tiles_per_seq, io_dtype)
        if latent:
            extra = _cache_kv(cache_l0_ckv.reshape(nbs * past, KV_LORA),
                              _pad_cols(cache_l0_krope.reshape(nbs * past, QK_ROPE), LANE), kvup)
        else:
            extra = None
            ctx_ckv = kvn.reshape(nb, L, KV_LORA)
            ctx_krope = kpe.reshape(nb, L, QK_ROPE)
        y_mla = _attention(q, k, vt, extra, nb, L, g["tq"], g["hps"])

        hfilt, hnorm = _filters(L, io_dtype, *filt_w)
        if two_stage:
            y_hy = _hyena_long(u, e, x0, hfilt, hnorm, nb)
        else:
            kre, kim = _dft("filt", "hy_fwd", [hfilt], [hnorm], 2)
            sh = (nb, L, HY_CH)
            yre, yim = _dft("fwdk", "hy_fwd", [u.reshape(sh)], [kre, kim], g["dft_nb"])
            y_hy = _dft("inv", "hy_inv", [yre, yim], [e.reshape(sh), x0.reshape(sh)], g["dft_nb"])
            y_hy = y_hy.reshape(nb * L, HY_CH)

        x1 = _post(g["x"], mods0, g["mod_base"], g["tiles_per_mod"], [y_hy, y_mla],
                   [wo0[:HY_CH], wo0[HY_CH:]], row1(l0_ln1_g), row1(l0_ln1_b),
                   l0_mlp_w1.astype(BF16), l0_mlp_w2.astype(BF16), row1(l0_ln2_g), row1(l0_ln2_b))

        if L == FFT_R * FFT_R:
            yf = _fnet_long(x1, mods1, g["mod_base"], nb)
        else:
            zc, zs = _fnet_front(x1, mods1, g["mod_base"], g["tiles_per_mod"])
            sh = (nb, L, d)
            yf = _dft("fnet", "fnet", [zc.reshape(sh), zs.reshape(sh)], [], min(g["dft_nb"], 2))
            yf = yf.reshape(nb * L, d)
        x2 = _post(x1, mods1, g["mod_base"], g["tiles_per_mod"], [yf],
                   [l1_w_out.astype(BF16)], row1(l1_ln1_g), row1(l1_ln1_b),
                   l1_mlp_w1.astype(BF16), l1_mlp_w2.astype(BF16), row1(l1_ln2_g), row1(l1_ln2_b))
        outs.append(x2.reshape(nb, L, d))

    return (outs[0], outs[1], ctx_ckv, ctx_krope)
```

```python
import functools
import math

import numpy as np
import jax
import jax.numpy as jnp
from jax import lax
from jax.experimental import pallas as pl
from jax.experimental.pallas import tpu as pltpu

F32 = jnp.float32
BF16 = jnp.bfloat16
HI = lax.Precision.HIGHEST

D_MODEL = 1024
DEPTH = 2
GRID_W = 64
HY_CH = 512
FILT_BANDS = 16
FILT_ORDER = 64
FAST_DECAY_PCT = 0.3
SLOW_DECAY_PCT = 1.5
DECAY_TARGET = 1e-2
MAX_DECAY = math.log(DECAY_TARGET) / FAST_DECAY_PCT
MIN_DECAY = math.log(DECAY_TARGET) / SLOW_DECAY_PCT
MLA_HEADS = 4
QK_NOPE = 128
QK_ROPE = 64
V_HEAD = 128
Q_LORA = 256
KV_LORA = 128
ROPE_THETA = 10000.0
FN_GROUP_CH = 128
D_FF = 4096
ALPHA = (2 * DEPTH) ** 0.25
LN_EPS = 1e-5
RMS_EPS = 1e-6

LANE = 128
ROW_TILE = 256
POST_TILE = 512
POST_SPLIT = 2
QK_PAD = 256
VT_ROWS = V_HEAD + 16
LOG2E = 1.4426950408889634
VMEM_LIMIT = 56 * 1024 * 1024


def _cparams(sem):
    return pltpu.CompilerParams(dimension_semantics=sem, vmem_limit_bytes=VMEM_LIMIT)


def _ln_plain(x):
    mu = jnp.mean(x, axis=-1, keepdims=True)
    xc = x - mu
    var = jnp.mean(xc * xc, axis=-1, keepdims=True)
    return xc * lax.rsqrt(var + LN_EPS)


def _rms(x, g):
    return x * lax.rsqrt(jnp.mean(x * x, axis=-1, keepdims=True) + RMS_EPS) * g


def _bdot(a, b):
    return jnp.dot(a.astype(BF16), b, preferred_element_type=F32)


def _vt_rows(v):
    ones = jnp.ones((VT_ROWS - V_HEAD, v.shape[0]), BF16)
    return jnp.concatenate([jnp.transpose(v).astype(BF16), ones], axis=0)


MOD_STREAMS = 3


def _mod_kernel(c_ref, *refs):
    w_refs, b_ref, o_ref = refs[:MOD_STREAMS], refs[MOD_STREAMS], refs[MOD_STREAMS + 1]
    c = c_ref[...]
    s = c / (1.0 + jnp.exp(-c))
    s_hi = s.astype(BF16)
    s_lo = (s - s_hi.astype(F32)).astype(BF16)
    s2 = jnp.concatenate([s_hi, s_lo], axis=0)
    nr = s.shape[0]
    tn = w_refs[0].shape[1]
    for k, w_ref in enumerate(w_refs):
        w = w_ref[...]
        w_hi = w.astype(BF16)
        w_lo = (w - w_hi.astype(F32)).astype(BF16)
        r1 = jnp.dot(s2, w_hi, preferred_element_type=F32)
        r2 = jnp.dot(s_hi, w_lo, preferred_element_type=F32)
        o_ref[:, k * tn:(k + 1) * tn] = r1[:nr] + r1[nr:] + r2 + b_ref[:, k * tn:(k + 1) * tn]


def _modulation(cond8, w, b):
    n = w.shape[1]
    tn = 512
    step = MOD_STREAMS * tn
    w_specs = [pl.BlockSpec((D_MODEL, tn), functools.partial(lambda j, k: (0, MOD_STREAMS * j + k), k=k))
               for k in range(MOD_STREAMS)]
    out = pl.pallas_call(
        _mod_kernel,
        out_shape=jax.ShapeDtypeStruct((8, n), F32),
        grid=(n // step,),
        in_specs=[pl.BlockSpec((8, D_MODEL), lambda j: (0, 0))] + w_specs
                 + [pl.BlockSpec((1, step), lambda j: (0, j))],
        out_specs=pl.BlockSpec((8, step), lambda j: (0, j)),
        compiler_params=_cparams(("arbitrary",)),
        name="modulation",
    )(cond8, *([w] * MOD_STREAMS), b.reshape(1, n))
    return out.reshape(8, 1, n)


def _mod_spec(mod_base, tiles_per_mod):
    return pl.BlockSpec((None, 1, 6 * D_MODEL), lambda i: (mod_base + i // tiles_per_mod, 0, 0))


def _const_spec(shape):
    nd = len(shape)
    return pl.BlockSpec(shape, lambda i: (0,) * nd)


HALO = 8


def _front_kernel(tiles_per_seq, x_ref, xp_ref, xn_ref, m_ref, win_ref, qn_ref, qup_ref, kvn_ref, kvup_ref,
                  cos_ref, sin_ref, cw_ref, cb_ref, skip_ref,
                  u_ref, e_ref, x0_ref, q_ref, k_ref, v_ref, kvn_out_ref, kpe_ref):
    i = pl.program_id(0)
    m = m_ref[...]
    tm = x_ref.shape[0]
    nh = 3 * HY_CH
    xe = jnp.concatenate([xp_ref[...], x_ref[...], xn_ref[...]], axis=0)
    he = _ln_plain(xe) * (1.0 + m[:, D_MODEL:2 * D_MODEL]) + m[:, 0:D_MODEL]

    zh = _bdot(he, win_ref[:, :nh])
    pos = i % tiles_per_seq
    rows = lax.broadcasted_iota(jnp.int32, (tm + 2 * HALO, 1), 0)
    inside = jnp.logical_and(jnp.logical_or(rows >= HALO, pos != 0),
                             jnp.logical_or(rows < tm + HALO, pos != tiles_per_seq - 1))
    zh = jnp.where(inside, zh, 0.0)
    cw = cw_ref[...]
    pz = (pltpu.roll(zh, 1, 0) * cw[0:1, :] + zh * cw[1:2, :]
          + pltpu.roll(zh, tm + 2 * HALO - 1, 0) * cw[2:3, :])[HALO:HALO + tm] + cb_ref[...]
    u = pz[:, 2 * HY_CH:] * pz[:, HY_CH:2 * HY_CH]
    u_ref[...] = u.astype(u_ref.dtype)
    e_ref[...] = u * skip_ref[...]
    x0_ref[...] = pz[:, :HY_CH]

    z = _bdot(he[HALO:HALO + tm], win_ref[:, nh:])
    q_c = z[:, 0:256]
    kv_c = z[:, 256:384]
    cos = cos_ref[...]
    sin = sin_ref[...]
    kpe = z[:, 384:512] * cos + z[:, 512:640] * sin
    kpe_ref[...] = kpe[:, :QK_ROPE]
    kpe_b = kpe.astype(BF16)
    q = _bdot(_rms(q_c, qn_ref[...]), qup_ref[...]) * (LOG2E / math.sqrt(QK_NOPE + QK_ROPE))
    kvn = _rms(kv_c, kvn_ref[...])
    kvn_out_ref[...] = kvn
    kv = _bdot(kvn, kvup_ref[...])
    for hd in range(MLA_HEADS):
        a = hd * LANE
        q_pe = (q[:, 512 + a:512 + a + LANE] * cos + q[:, 1024 + a:1024 + a + LANE] * sin).astype(BF16)
        q_ref[hd] = jnp.concatenate([q[:, a:a + LANE].astype(BF16), q_pe], axis=-1)
        k_ref[hd] = jnp.concatenate([kv[:, 2 * a:2 * a + LANE].astype(BF16), kpe_b], axis=-1)
        v_ref[hd] = _vt_rows(kv[:, 2 * a + LANE:2 * a + 2 * LANE])


def _front(x, mods, mod_base, tiles_per_mod, w, cos, sin, rope, tiles_per_seq, u_dtype):
    t = x.shape[0]
    tm = ROW_TILE
    win, qn, qup, kvn, kvup, conv_w, conv_b, skip = w
    if rope:
        tab_spec = pl.BlockSpec((tm, LANE), lambda i: (i % tiles_per_seq, 0))
    else:
        tab_spec = pl.BlockSpec((tm, LANE), lambda i: (0, 0))
    r8 = tm // HALO
    n8 = t // HALO
    hy_out = lambda dt: jax.ShapeDtypeStruct((t, HY_CH), dt)
    hy_spec = pl.BlockSpec((tm, HY_CH), lambda i: (i, 0))
    return pl.pallas_call(
        functools.partial(_front_kernel, tiles_per_seq),
        out_shape=(hy_out(u_dtype), hy_out(F32), hy_out(F32),
                   jax.ShapeDtypeStruct((MLA_HEADS, t, QK_PAD), BF16),
                   jax.ShapeDtypeStruct((MLA_HEADS, t, QK_PAD), BF16),
                   jax.ShapeDtypeStruct((MLA_HEADS, VT_ROWS, t), BF16),
                   jax.ShapeDtypeStruct((t, KV_LORA), F32),
                   jax.ShapeDtypeStruct((t, QK_ROPE), F32)),
        grid=(t // tm,),
        in_specs=[pl.BlockSpec((tm, D_MODEL), lambda i: (i, 0)),
                  pl.BlockSpec((HALO, D_MODEL), lambda i: (jnp.maximum(i * r8 - 1, 0), 0)),
                  pl.BlockSpec((HALO, D_MODEL), lambda i: (jnp.minimum((i + 1) * r8, n8 - 1), 0)),
                  _mod_spec(mod_base, tiles_per_mod),
                  _const_spec(win.shape), _const_spec(qn.shape), _const_spec(qup.shape),
                  _const_spec(kvn.shape), _const_spec(kvup.shape),
                  tab_spec, tab_spec,
                  _const_spec(conv_w.shape), _const_spec(conv_b.shape), _const_spec(skip.shape)],
        out_specs=(hy_spec, hy_spec, hy_spec,
                   pl.BlockSpec((MLA_HEADS, tm, QK_PAD), lambda i: (0, i, 0)),
                   pl.BlockSpec((MLA_HEADS, tm, QK_PAD), lambda i: (0, i, 0)),
                   pl.BlockSpec((MLA_HEADS, VT_ROWS, tm), lambda i: (0, 0, i)),
                   pl.BlockSpec((tm, KV_LORA), lambda i: (i, 0)),
                   pl.BlockSpec((tm, QK_ROPE), lambda i: (i, 0))),
        compiler_params=_cparams(("arbitrary",)),
        name="l0_front",
    )(x, x, x, mods, win, qn, qup, kvn, kvup, cos, sin, conv_w, conv_b, skip)


def _cache_kv_kernel(ckv_ref, kr_ref, kvup_ref, k_ref, v_ref):
    kv = _bdot(ckv_ref[...], kvup_ref[...])
    kr = kr_ref[...].astype(BF16)
    for hd in range(MLA_HEADS):
        a = 2 * hd * LANE
        k_ref[hd] = jnp.concatenate([kv[:, a:a + LANE].astype(BF16), kr], axis=-1)
        v_ref[hd] = _vt_rows(kv[:, a + LANE:a + 2 * LANE])


def _cache_kv(ckv, krope_pad, kvup):
    t = ckv.shape[0]
    return pl.pallas_call(
        _cache_kv_kernel,
        out_shape=(jax.ShapeDtypeStruct((MLA_HEADS, t, QK_PAD), BF16),
                   jax.ShapeDtypeStruct((MLA_HEADS, VT_ROWS, t), BF16)),
        name="l0_cache_kv",
    )(ckv, krope_pad, kvup)


def _col_reduce(x, op):
    rows, n = x.shape
    for g in (32, 8):
        if rows % (8 * g) == 0 and rows > 8 * g:
            x = op(x.reshape(rows // (8 * g), 8 * g, n), axis=0)
            rows = 8 * g
    return op(x, axis=0, keepdims=True)


def _attn_kernel(n_kv, hps, q_ref, *refs):
    k_refs, vt_refs = refs[:n_kv], refs[n_kv:2 * n_kv]
    o_ref, s_even, s_odd = refs[2 * n_kv:]
    i = pl.program_id(0)

    @pl.when(i == 0)
    def _():
        s_odd[...] = jnp.zeros_like(s_odd)

    def step(s_write, s_read):
        nt = (((1,), (1,)), ((), ()))
        for h in range(hps):
            q = q_ref[h]
            r0 = 0
            for k_ref in k_refs:
                lk = k_ref.shape[1]
                s_write[h, r0:r0 + lk, :] = lax.dot_general(k_ref[h], q, nt, preferred_element_type=F32)
                r0 += lk
        for h in range(hps):
            s = s_read[h]
            m = _col_reduce(s, jnp.max)
            pb = jnp.exp2(s - m).astype(BF16)
            acc = None
            r0 = 0
            for vt_ref in vt_refs:
                lk = vt_ref.shape[2]
                pv = jnp.dot(vt_ref[h], pb[r0:r0 + lk, :], preferred_element_type=F32)
                acc = pv if acc is None else acc + pv
                r0 += lk
            o_ref[:, h * V_HEAD:(h + 1) * V_HEAD] = jnp.transpose(
                acc[:V_HEAD] / acc[V_HEAD:V_HEAD + 1]).astype(o_ref.dtype)

    pl.when(i % 2 == 0)(lambda: step(s_even, s_odd))
    pl.when(i % 2 == 1)(lambda: step(s_odd, s_even))


def _attention(q, k, vt, extra, nb, lq, tq, hps):
    nq = lq // tq
    ng = MLA_HEADS // hps
    n_tiles = nb * ng * nq

    def where(t):
        bh = t // nq
        return bh // ng, bh % ng, t % nq

    def score_side(fn):
        return lambda i: fn(*where(jnp.minimum(i, n_tiles - 1)))

    def value_side(fn):
        return lambda i: fn(*where(jnp.maximum(i - 1, 0)))

    ks, vts = [k], [vt]
    if extra is not None:
        ks.append(extra[0])
        vts.append(extra[1])
    in_specs = [pl.BlockSpec((hps, tq, QK_PAD), score_side(lambda b, h, j: (h, b * nq + j, 0)))]
    in_specs += [pl.BlockSpec((hps, a.shape[1] // nb, QK_PAD), score_side(lambda b, h, j: (h, b, 0))) for a in ks]
    in_specs += [pl.BlockSpec((hps, VT_ROWS, a.shape[2] // nb), value_side(lambda b, h, j: (h, 0, b))) for a in vts]
    lk_total = sum(a.shape[1] // nb for a in ks)
    return pl.pallas_call(
        functools.partial(_attn_kernel, len(ks), hps),
        out_shape=jax.ShapeDtypeStruct((nb * lq, MLA_HEADS * V_HEAD), BF16),
        grid=(n_tiles + 1,),
        in_specs=in_specs,
        out_specs=pl.BlockSpec((tq, hps * V_HEAD), value_side(lambda b, h, j: (b * nq + j, h))),
        scratch_shapes=[pltpu.VMEM((hps, lk_total, tq), F32), pltpu.VMEM((hps, lk_total, tq), F32)],
        compiler_params=_cparams(("arbitrary",)),
        name="l0_attention",
    )(q, *ks, *vts)


def _filter_kernel(z_ref, w1_ref, b1_ref, fr_ref, w2_ref, b2_ref, w3_ref, dl_ref, h_ref, norm_ref):
    i = pl.program_id(0)
    z = z_ref[...]
    tl = z.shape[0]
    fr = fr_ref[...]
    z2 = jnp.concatenate([z[:tl // 2], z[tl // 2:]], axis=1)
    h = jnp.sin(fr * (jnp.dot(z2, w1_ref[...], precision=HI, preferred_element_type=F32) + b1_ref[...]))
    h = jnp.sin(fr * (jnp.dot(h, w2_ref[...], precision=HI, preferred_element_type=F32) + b2_ref[...]))
    h = _bdot(h, w3_ref[...])
    h = jnp.concatenate([h[:, :2 * HY_CH], h[:, 2 * HY_CH:]], axis=0)
    decay = jnp.exp(-(z[:, 0:1] * dl_ref[...]))
    hf = h[:, :HY_CH] * decay
    hb = h[:, HY_CH:] * decay
    part = jnp.sum(jnp.abs(hf) + jnp.abs(hb), axis=0, keepdims=True)

    @pl.when(i == 0)
    def _():
        norm_ref[...] = part

    @pl.when(i > 0)
    def _():
        norm_ref[...] += part

    rows = lax.broadcasted_iota(jnp.int32, hb.shape, 0) + i * tl
    h_ref[0] = hf.astype(h_ref.dtype)
    h_ref[1] = jnp.where(rows == 0, 0.0, hb).astype(h_ref.dtype)


def _filter_embedding(L):
    t = np.linspace(0.0, 1.0, L)[:, None]
    w_ang = 2.0 * np.pi * np.arange(L) / L
    bands = np.linspace(1e-4, FILT_BANDS - 1, FILT_BANDS)
    ang = w_ang[:, None] * bands[None, :]
    z = np.zeros((L, LANE), np.float64)
    z[:, 0:1] = t
    z[:, 1:1 + FILT_BANDS] = np.cos(ang)
    z[:, 1 + FILT_BANDS:1 + 2 * FILT_BANDS] = -np.sin(ang)
    return jnp.asarray(z, F32)


def _filters(L, h_dtype, w1p, b1, fr, w2, b2, w3):
    tl = min(L, 512)
    z = _filter_embedding(L)
    deltas = jnp.asarray(np.abs(np.linspace(MIN_DECAY, MAX_DECAY, HY_CH))[None, :], F32)
    return pl.pallas_call(
        _filter_kernel,
        out_shape=(jax.ShapeDtypeStruct((2, L, HY_CH), h_dtype), jax.ShapeDtypeStruct((1, HY_CH), F32)),
        grid=(L // tl,),
        in_specs=[pl.BlockSpec((tl, LANE), lambda i: (i, 0)),
                  _const_spec(w1p.shape), _const_spec(b1.shape), _const_spec(fr.shape),
                  _const_spec(w2.shape), _const_spec(b2.shape), _const_spec(w3.shape),
                  _const_spec(deltas.shape)],
        out_specs=(pl.BlockSpec((2, tl, HY_CH), lambda i: (0, i, 0)),
                   pl.BlockSpec((1, HY_CH), lambda i: (0, 0))),
        compiler_params=_cparams(("arbitrary",)),
        name="l0_hyena_filters",
    )(z, w1p, b1, fr, w2, b2, w3, deltas)


def _dft_tables(kind, L, ti):
    ni = L // ti
    i = np.arange(ti, dtype=np.int64)[:, None]
    big = (np.arange(ni, dtype=np.int64) * ti)[:, None]
    c = np.arange(L, dtype=np.int64)[None, :]
    if kind == "hy_fwd":
        period = 4 * L
        base_idx = (2 * i + 1) * c
        r_idx = 2 * big * c
        scale = 1.0
    elif kind == "hy_inv":
        period = 4 * L
        base_idx = (2 * c + 1) * i
        r_idx = (2 * c + 1) * big
        scale = 1.0 / L
    else:
        period = L
        base_idx = i * c
        r_idx = big * c
        scale = 1.0 / math.sqrt(L * FN_GROUP_CH)
    ab = 2.0 * np.pi * (base_idx % period) / period
    ar = 2.0 * np.pi * (r_idx % period) / period
    return (jnp.asarray(np.cos(ab), F32), jnp.asarray(np.sin(ab), F32),
            jnp.asarray(scale * np.cos(ar), F32).reshape(ni, 1, L),
            jnp.asarray(scale * np.sin(ar), F32).reshape(ni, 1, L))


def _dft_kernel(mode, nb, n_x, *refs):
    bc_ref, bs_ref, rc_ref, rs_ref = refs[:4]
    x_refs = refs[4:4 + n_x]
    rest = refs[4 + n_x:]
    p_ref, q_ref = rest[-2], rest[-1]
    j = pl.program_id(2)
    nj = pl.num_programs(2)
    tj = x_refs[0].shape[1]
    if bc_ref.shape[1] == tj:
        bc, bs, rc, rs = bc_ref[...], bs_ref[...], rc_ref[...], rs_ref[...]
    else:
        off = pl.multiple_of(j * tj, tj)
        bc, bs = bc_ref[:, pl.ds(off, tj)], bs_ref[:, pl.ds(off, tj)]
        rc, rs = rc_ref[:, pl.ds(off, tj)], rs_ref[:, pl.ds(off, tj)]
    tc = (bc * rc - bs * rs).astype(BF16)
    ts = (bs * rc + bc * rs).astype(BF16)
    x1_ref = x_refs[0]
    x2_ref = x_refs[-1]

    pq = [(jnp.dot(tc, x1_ref[b], preferred_element_type=F32),
           jnp.dot(ts, x2_ref[b], preferred_element_type=F32)) for b in range(nb)]

    @pl.when(j == 0)
    def _():
        for b in range(nb):
            p_ref[b] = pq[b][0]
            q_ref[b] = pq[b][1]

    @pl.when(j > 0)
    def _():
        for b in range(nb):
            p_ref[b] += pq[b][0]
            q_ref[b] += pq[b][1]

    @pl.when(j == nj - 1)
    def _():
        if mode == "filt":
            nrm = rest[0][...]
            kre_ref, kim_ref = rest[1], rest[2]
            kre_ref[...] = (p_ref[0] + p_ref[1]) / nrm
            kim_ref[...] = (q_ref[1] - q_ref[0]) / nrm
        elif mode == "fwdk":
            kre, kim = rest[0][...], rest[1][...]
            yre_ref, yim_ref = rest[2], rest[3]
            for b in range(nb):
                pp, qq = p_ref[b], q_ref[b]
                yre_ref[b] = (pp * kre + qq * kim).astype(BF16)
                yim_ref[b] = (pp * kim - qq * kre).astype(BF16)
        elif mode == "inv":
            e_ref, x0_ref, o_ref = rest[0], rest[1], rest[2]
            for b in range(nb):
                o_ref[b] = ((p_ref[b] - q_ref[b] + e_ref[b]) * x0_ref[b]).astype(BF16)
        else:
            o_ref = rest[0]
            for b in range(nb):
                o_ref[b] = (p_ref[b] - q_ref[b]).astype(BF16)


def _dft(mode, kind, xs, extras, nb):
    B, L, C = xs[0].shape
    ti = min(L, 256)
    tj = min(L, 512)
    bc, bs, rc, rs = _dft_tables(kind, L, ti)
    grid = (B // nb, L // ti, L // tj)
    x_spec = pl.BlockSpec((nb, tj, C), lambda g, i, j: (g, j, 0))
    row_spec = lambda c, dt=None: pl.BlockSpec((nb, ti, c), lambda g, i, j: (g, i, 0))
    in_specs = [pl.BlockSpec((ti, L), lambda g, i, j: (0, 0)),
                pl.BlockSpec((ti, L), lambda g, i, j: (0, 0)),
                pl.BlockSpec((None, 1, L), lambda g, i, j: (i, 0, 0)),
                pl.BlockSpec((None, 1, L), lambda g, i, j: (i, 0, 0))] + [x_spec] * len(xs)
    if mode == "filt":
        in_specs += [pl.BlockSpec((1, HY_CH), lambda g, i, j: (0, 0))]
        out_shape = (jax.ShapeDtypeStruct((L, HY_CH), F32),) * 2
        out_specs = (pl.BlockSpec((ti, HY_CH), lambda g, i, j: (i, 0)),) * 2
    elif mode == "fwdk":
        in_specs += [pl.BlockSpec((ti, HY_CH), lambda g, i, j: (i, 0))] * 2
        out_shape = (jax.ShapeDtypeStruct((B, L, C), BF16),) * 2
        out_specs = (row_spec(C),) * 2
    elif mode == "inv":
        in_specs += [row_spec(C)] * 2
        out_shape = jax.ShapeDtypeStruct((B, L, C), BF16)
        out_specs = row_spec(C)
    else:
        out_shape = jax.ShapeDtypeStruct((B, L, C), BF16)
        out_specs = row_spec(C)
    return pl.pallas_call(
        functools.partial(_dft_kernel, mode, nb, len(xs)),
        out_shape=out_shape,
        grid=grid,
        in_specs=in_specs,
        out_specs=out_specs,
        scratch_shapes=[pltpu.VMEM((nb, ti, C), F32), pltpu.VMEM((nb, ti, C), F32)],
        compiler_params=_cparams(("arbitrary", "arbitrary", "arbitrary")),
        name="dft_" + mode,
    )(bc, bs, rc, rs, *xs, *extras)


FFT_R = 64
FFT_KF = 8


def _pack_pairs(x):
    return pltpu.bitcast(x.astype(BF16), jnp.uint32)


def _unpack_pairs(w):
    return pltpu.bitcast(w, BF16)


def _to_blocks(w):
    return jnp.swapaxes(w.reshape(FFT_R, FFT_KF, w.shape[-1]), 0, 1)


def _from_blocks(ws):
    c = ws[0].shape[-1]
    return jnp.swapaxes(jnp.stack(ws, axis=0), 0, 1).reshape(FFT_KF, FFT_KF, FFT_KF, c)


def _lead_in_kernel(g_ref, x_ref, o_ref):
    g = g_ref[...]
    xt = jnp.swapaxes(x_ref[...], 0, 1)
    for j in range(FFT_KF):
        o_ref[j] = _pack_pairs(jnp.dot(g, xt[j].astype(BF16), preferred_element_type=F32))


def _lead_in(g, x, name):
    nbx, _, _, c = x.shape
    m2 = g.shape[0] // 2
    return pl.pallas_call(
        _lead_in_kernel,
        out_shape=jax.ShapeDtypeStruct((nbx, FFT_KF, FFT_KF, m2, c), jnp.uint32),
        grid=(nbx, FFT_R // FFT_KF),
        in_specs=[pl.BlockSpec(g.shape, lambda b, k: (0, 0)),
                  pl.BlockSpec((None, FFT_R, FFT_KF, c), lambda b, k: (b, 0, k, 0))],
        out_specs=pl.BlockSpec((None, None, FFT_KF, m2, c), lambda b, k: (b, k, 0, 0, 0)),
        compiler_params=_cparams(("arbitrary", "arbitrary")),
        name=name,
    )(g, x)


def _lead_out_kernel(n_extra, g_ref, w_ref, *rest):
    g = g_ref[...]
    o_ref = rest[-1]
    ys = [jnp.dot(g, _unpack_pairs(w_ref[j]), preferred_element_type=F32) for j in range(FFT_KF)]
    y = jnp.swapaxes(jnp.stack(ys, axis=0), 0, 1)
    if n_extra:
        y = (y + rest[0][...]) * rest[1][...]
    o_ref[...] = y


def _lead_out(g, w, extras, name):
    nb, _, _, k2, c = w.shape
    blk = pl.BlockSpec((None, FFT_R, FFT_KF, c), lambda b, k: (b, 0, k, 0))
    return pl.pallas_call(
        functools.partial(_lead_out_kernel, len(extras)),
        out_shape=jax.ShapeDtypeStruct((nb, FFT_R, FFT_R, c), F32),
        grid=(nb, FFT_R // FFT_KF),
        in_specs=[pl.BlockSpec(g.shape, lambda b, k: (0, 0)),
                  pl.BlockSpec((None, None, FFT_KF, k2, c), lambda b, k: (b, k, 0, 0, 0))] + [blk] * len(extras),
        out_specs=blk,
        compiler_params=_cparams(("arbitrary", "arbitrary")),
        name=name,
    )(g, w, *extras)


def _interleave(a, b, axis):
    st = np.stack([a, b], axis=axis + 1)
    shape = list(a.shape)
    shape[axis] *= 2
    return st.reshape(shape)


def _hy2_tables():
    L = FFT_R * FFT_R
    n2 = 2 * L
    f1 = np.arange(2 * FFT_R, dtype=np.int64)
    s1 = np.arange(FFT_R, dtype=np.int64)
    th = np.pi * (((2 * f1[:, None] + 1) * s1[None, :]) % (4 * FFT_R)) / (2 * FFT_R)
    ga = _interleave(np.cos(th), -np.sin(th), 0)
    ma = _interleave(np.cos(th).T, -np.sin(th).T, 1) / L
    f2 = np.arange(FFT_R // 2, dtype=np.int64)
    s2 = np.arange(FFT_R, dtype=np.int64)
    idx = ((n2 // FFT_R) * 2 * f2[None, :, None] * s2[None, None, :]
           + (2 * f1[:, None, None] + 1) * s2[None, None, :]) % (2 * n2)
    al = np.pi * idx / n2
    c, s = np.cos(al), np.sin(al)
    nmat = np.concatenate([_interleave(c, s, 2), _interleave(-s, c, 2)], axis=1)
    ct, st = np.transpose(c, (0, 2, 1)), np.transpose(s, (0, 2, 1))
    mmat = _interleave(np.concatenate([ct, -st], axis=2), np.concatenate([st, ct], axis=2), 1)
    bf = lambda a: jnp.asarray(a, F32).astype(BF16)
    return bf(ga), bf(nmat), bf(mmat), bf(ma)


def _hy_mid_kernel(a_ref, n_ref, m_ref, k_ref, e_ref):
    half = FFT_R // 2
    a = _to_blocks(a_ref[...])
    ts = [jnp.dot(n_ref[j], _unpack_pairs(a[j]), preferred_element_type=F32)
          for j in range(FFT_KF)]
    ys = []
    for j in range(FFT_KF):
        tr, ti = ts[j][:half], ts[j][half:]
        kr, ki = k_ref[j, 0], k_ref[j, 1]
        ys.append(jnp.concatenate([tr * kr - ti * ki, tr * ki + ti * kr], axis=0).astype(BF16))
    e_ref[...] = _from_blocks([_pack_pairs(jnp.dot(m_ref[j], ys[j], preferred_element_type=F32))
                               for j in range(FFT_KF)])


def _hy_mid(a, nmat, mmat, khat):
    nb = a.shape[0]
    c = a.shape[-1]
    nf1 = 2 * FFT_R
    blk = pl.BlockSpec((None, FFT_KF, FFT_KF, FFT_KF, c), lambda i, b: (b, 0, 0, i, 0))
    return pl.pallas_call(
        _hy_mid_kernel,
        out_shape=jax.ShapeDtypeStruct(a.shape, jnp.uint32),
        grid=(nf1 // FFT_KF, nb),
        in_specs=[blk,
                  pl.BlockSpec((FFT_KF, FFT_R, 2 * FFT_R), lambda i, b: (i, 0, 0)),
                  pl.BlockSpec((FFT_KF, 2 * FFT_R, FFT_R), lambda i, b: (i, 0, 0)),
                  pl.BlockSpec((FFT_KF, 2, FFT_R // 2, c), lambda i, b: (i, 0, 0, 0))],
        out_specs=blk,
        compiler_params=_cparams(("arbitrary", "arbitrary")),
        name="l0_hyena_mid",
    )(a, nmat, mmat, khat)


def _hy_kfilt_kernel(a_ref, n_ref, nrm_ref, k_ref):
    half = FFT_R // 2
    nrm = nrm_ref[...]
    af, ab = _to_blocks(a_ref[0]), _to_blocks(a_ref[1])
    for j in range(FFT_KF):
        tf = jnp.dot(n_ref[j], _unpack_pairs(af[j]), preferred_element_type=F32)
        tb = jnp.dot(n_ref[j], _unpack_pairs(ab[j]), preferred_element_type=F32)
        k_ref[j, 0] = (tf[:half] + tb[:half]) / nrm
        k_ref[j, 1] = (tf[half:] - tb[half:]) / nrm


def _hy_kfilt(a, nmat, nrm):
    c = a.shape[-1]
    nf1 = 2 * FFT_R
    return pl.pallas_call(
        _hy_kfilt_kernel,
        out_shape=jax.ShapeDtypeStruct((nf1, 2, FFT_R // 2, c), F32),
        grid=(nf1 // FFT_KF,),
        in_specs=[pl.BlockSpec((2, FFT_KF, FFT_KF, FFT_KF, c), lambda i: (0, 0, 0, i, 0)),
                  pl.BlockSpec((FFT_KF, FFT_R, 2 * FFT_R), lambda i: (i, 0, 0)),
                  _const_spec(nrm.shape)],
        out_specs=pl.BlockSpec((FFT_KF, 2, FFT_R // 2, c), lambda i: (i, 0, 0, 0)),
        compiler_params=_cparams(("arbitrary",)),
        name="l0_hyena_kfilt",
    )(a, nmat, nrm)


def _hyena_long(u, e, x0, hfilt, nrm, nb):
    L = FFT_R * FFT_R
    c = u.shape[-1]
    v4 = lambda a, n: a.reshape(n, FFT_R, FFT_R, c)
    ga, nmat, mmat, ma = _hy2_tables()
    khat = _hy_kfilt(_lead_in(ga, v4(hfilt, 2), "l0_hyena_fwd_a"), nmat, nrm)
    ee = _hy_mid(_lead_in(ga, v4(u, nb), "l0_hyena_fwd_a"), nmat, mmat, khat)
    y = _lead_out(ma, ee, [v4(e, nb), v4(x0, nb)], "l0_hyena_inv_a")
    return y.reshape(nb * L, c)


def _fn2_tables():
    L = FFT_R * FFT_R
    r = np.arange(FFT_R, dtype=np.int64)
    idx = (FFT_R * r[None, :, None] * r[None, None, :] + r[None, :, None] * r[:, None, None]) % L
    gm = 2.0 * np.pi * idx / L
    c, s = np.cos(gm), np.sin(gm)
    g1 = _interleave(np.concatenate([c, -s], axis=2), np.concatenate([-s, -c], axis=2), 1)
    dl = 2.0 * np.pi * ((r[:, None] * r[None, :]) % FFT_R) / FFT_R
    g2 = _interleave(np.cos(dl), np.sin(dl), 1) / math.sqrt(L * FN_GROUP_CH)
    bf = lambda a: jnp.asarray(a, F32).astype(BF16)
    return bf(g1), bf(g2)


def _fnet_s1_kernel(x_ref, m_ref, cs_ref, g1_ref, o_ref, zc_ref, zs_ref):
    xs = jnp.swapaxes(x_ref[...], 0, 1).reshape(FFT_KF * FFT_R, D_MODEL)
    m = m_ref[...]
    h = (_ln_plain(xs) * (1.0 + m[:, D_MODEL:2 * D_MODEL]) + m[:, 0:D_MODEL]).astype(BF16)
    cs = cs_ref[...]
    for g in range(D_MODEL // FN_GROUP_CH):
        a = g * FN_GROUP_CH
        z = jnp.dot(h[:, a:a + FN_GROUP_CH], cs, preferred_element_type=F32)
        zc_ref[:, a:a + FN_GROUP_CH] = z[:, :FN_GROUP_CH].astype(BF16)
        zs_ref[:, a:a + FN_GROUP_CH] = z[:, FN_GROUP_CH:].astype(BF16)
    ws = []
    for j in range(FFT_KF):
        r0 = j * FFT_R
        s = jnp.concatenate([zc_ref[r0:r0 + FFT_R, :], zs_ref[r0:r0 + FFT_R, :]], axis=0)
        ws.append(_pack_pairs(jnp.dot(g1_ref[j], s, preferred_element_type=F32)))
    o_ref[...] = _from_blocks(ws)


def _fnet_long(x, mods, mod_base, nb):
    L = FFT_R * FFT_R
    d = D_MODEL
    g1, g2 = _fn2_tables()
    cs = _group_dft_table()
    bb = pl.pallas_call(
        _fnet_s1_kernel,
        out_shape=jax.ShapeDtypeStruct((nb, FFT_KF, FFT_KF, FFT_R, d), jnp.uint32),
        grid=(nb, FFT_R // FFT_KF),
        in_specs=[pl.BlockSpec((None, FFT_R, FFT_KF, d), lambda b, k: (b, 0, k, 0)),
                  pl.BlockSpec((None, 1, 6 * d), lambda b, k: (mod_base + b, 0, 0)),
                  pl.BlockSpec(cs.shape, lambda b, k: (0, 0)),
                  pl.BlockSpec((FFT_KF, 2 * FFT_R, 2 * FFT_R), lambda b, k: (k, 0, 0))],
        out_specs=pl.BlockSpec((None, FFT_KF, FFT_KF, FFT_KF, d), lambda b, k: (b, 0, 0, k, 0)),
        scratch_shapes=[pltpu.VMEM((FFT_KF * FFT_R, d), BF16), pltpu.VMEM((FFT_KF * FFT_R, d), BF16)],
        compiler_params=_cparams(("arbitrary", "arbitrary")),
        name="l1_fnet_stage1",
    )(x.reshape(nb, FFT_R, FFT_R, d), mods, cs, g1)
    y = _lead_out(g2, bb, [], "l1_fnet_stage2")
    return y.reshape(nb * L, d)


def _group_dft_table():
    g = FN_GROUP_CH
    jk = (np.arange(g, dtype=np.int64)[:, None] * np.arange(g, dtype=np.int64)[None, :]) % g
    ang = 2.0 * np.pi * jk / g
    return jnp.asarray(np.concatenate([np.cos(ang), np.sin(ang)], axis=1), F32).astype(BF16)


def _fnet_front_kernel(x_ref, m_ref, cs_ref, zc_ref, zs_ref):
    m = m_ref[...]
    h = (_ln_plain(x_ref[...]) * (1.0 + m[:, D_MODEL:2 * D_MODEL]) + m[:, 0:D_MODEL]).astype(BF16)
    cs = cs_ref[...]
    for g in range(D_MODEL // FN_GROUP_CH):
        a = g * FN_GROUP_CH
        z = jnp.dot(h[:, a:a + FN_GROUP_CH], cs, preferred_element_type=F32)
        zc_ref[:, a:a + FN_GROUP_CH] = z[:, :FN_GROUP_CH].astype(BF16)
        zs_ref[:, a:a + FN_GROUP_CH] = z[:, FN_GROUP_CH:].astype(BF16)


def _fnet_front(x, mods, mod_base, tiles_per_mod):
    t = x.shape[0]
    tm = ROW_TILE
    cs = _group_dft_table()
    return pl.pallas_call(
        _fnet_front_kernel,
        out_shape=(jax.ShapeDtypeStruct((t, D_MODEL), BF16),) * 2,
        grid=(t // tm,),
        in_specs=[pl.BlockSpec((tm, D_MODEL), lambda i: (i, 0)),
                  _mod_spec(mod_base, tiles_per_mod),
                  _const_spec(cs.shape)],
        out_specs=(pl.BlockSpec((tm, D_MODEL), lambda i: (i, 0)),) * 2,
        compiler_params=_cparams(("arbitrary",)),
        name="l1_fnet_front",
    )(x, mods, cs)


def _post_kernel(n_a, *refs):
    x_ref, m_ref = refs[0], refs[1]
    a_refs = refs[2:2 + n_a]
    wo_refs = refs[2 + n_a:2 + 2 * n_a]
    g1_ref, b1_ref, w1_ref, w2_ref, g2_ref, b2_ref, o_ref = refs[2 + 2 * n_a:]
    m = m_ref[...]
    d = D_MODEL
    tm = x_ref.shape[0]
    halves = [(r, r + tm // POST_SPLIT) for r in range(0, tm, tm // POST_SPLIT)]
    outs = []
    for r0, r1 in halves:
        out = _bdot(a_refs[0][r0:r1, :], wo_refs[0][...])
        for a_ref, wo_ref in zip(a_refs[1:], wo_refs[1:]):
            out += _bdot(a_ref[r0:r1, :], wo_ref[...])
        outs.append(out)
    x1s, hs = [], []
    for (r0, r1), out in zip(halves, outs):
        x1 = _ln_plain(ALPHA * x_ref[r0:r1, :] + m[:, 2 * d:3 * d] * out) * g1_ref[...] + b1_ref[...]
        x1s.append(x1)
        hs.append((_ln_plain(x1) * (1.0 + m[:, 4 * d:5 * d]) + m[:, 3 * d:4 * d]).astype(BF16))
    accs = []
    n_c = D_FF // d

    def up(h, c):
        hc = jnp.maximum(jnp.dot(h, w1_ref[:, c * d:(c + 1) * d], preferred_element_type=F32), 0.0)
        return (hc * hc).astype(BF16)

    for h in hs:
        acc = None
        nxt = up(h, 0)
        for c in range(n_c):
            cur = nxt
            if c + 1 < n_c:
                nxt = up(h, c + 1)
            part = jnp.dot(cur, w2_ref[c * d:(c + 1) * d, :], preferred_element_type=F32)
            acc = part if acc is None else acc + part
        accs.append(acc)
    for (r0, r1), x1, acc in zip(halves, x1s, accs):
        o_ref[r0:r1, :] = _ln_plain(ALPHA * x1 + m[:, 5 * d:6 * d] * acc) * g2_ref[...] + b2_ref[...]


def _post(x, mods, mod_base, tiles_per_mod, a_list, wo_list, g1, b1, w1, w2, g2, b2):
    t = x.shape[0]
    tm = POST_TILE
    row = lambda c: pl.BlockSpec((tm, c), lambda i: (i, 0))
    once = lambda v: pl.BlockSpec(v.shape, lambda i: (0,) * v.ndim, pipeline_mode=pl.Buffered(1))
    in_specs = ([row(D_MODEL), _mod_spec(mod_base, tiles_per_mod * ROW_TILE // tm)]
                + [row(a.shape[1]) for a in a_list]
                + [once(w) for w in wo_list]
                + [once(v) for v in (g1, b1, w1, w2, g2, b2)])
    return pl.pallas_call(
        functools.partial(_post_kernel, len(a_list)),
        out_shape=jax.ShapeDtypeStruct((t, D_MODEL), F32),
        grid=(t // tm,),
        in_specs=in_specs,
        out_specs=row(D_MODEL),
        compiler_params=_cparams(("arbitrary",)),
        name="post_mlp",
    )(x, mods, *a_list, *wo_list, g1, b1, w1, w2, g2, b2)


def _rot_cols(w):
    parts = []
    for seg in range(2):
        o = seg * 32
        parts += [-w[:, o + 16:o + 32], w[:, o:o + 16]]
    return jnp.concatenate(parts, axis=1)


def _pad_cols(w, n):
    return jnp.pad(w, ((0, 0), (0, n - w.shape[1])))


def _block_diag2(w):
    z = jnp.zeros_like(w)
    return jnp.concatenate([jnp.concatenate([w, z], axis=1), jnp.concatenate([z, w], axis=1)], axis=0)


def _rope_tables(L):
    rows = L // GRID_W
    row = np.repeat(np.arange(rows, dtype=np.float64), GRID_W)
    col = np.tile(np.arange(GRID_W, dtype=np.float64), rows)
    half = QK_ROPE // 2
    inv = 1.0 / (ROPE_THETA ** (np.arange(0, half, 2, dtype=np.float64) / half))
    ar = row[:, None] * inv[None, :]
    ac = col[:, None] * inv[None, :]
    ang = np.concatenate([ar, ar, ac, ac], axis=1)
    cos = np.concatenate([np.cos(ang), np.ones_like(ang)], axis=1)
    sin = np.concatenate([np.sin(ang), np.zeros_like(ang)], axis=1)
    return jnp.asarray(cos, F32), jnp.asarray(sin, F32)


def kernel(x_prompt, x_sample, cache_l0_ckv, cache_l0_krope, c, c_ctx, l0_ada_w, l0_ada_b, l0_w_in, l0_conv_w, l0_conv_b, l0_hf_w1, l0_hf_b1, l0_hf_freq, l0_hf_w2, l0_hf_b2, l0_hf_w3, l0_hf_skip, l0_q_norm, l0_q_up, l0_kv_norm, l0_kv_up, l0_w_out, l0_ln1_g, l0_ln1_b, l0_mlp_w1, l0_mlp_w2, l0_ln2_g, l0_ln2_b, l1_ada_w, l1_ada_b, l1_w_out, l1_ln1_g, l1_ln1_b, l1_mlp_w1, l1_mlp_w2, l1_ln2_g, l1_ln2_b):
    nbc, lc, d = x_prompt.shape
    nbs, ls, _ = x_sample.shape
    past = cache_l0_ckv.shape[1]
    tm = ROW_TILE
    row1 = lambda v: v.reshape(1, -1)

    cond8 = jnp.concatenate([c_ctx[None, :], c, jnp.zeros((8 - 1 - nbs, d), F32)], axis=0)
    mods0 = _modulation(cond8, l0_ada_w, l0_ada_b)
    mods1 = _modulation(cond8, l1_ada_w, l1_ada_b)

    kpe_w = l0_w_in[:, 1920:1984]
    win = jnp.concatenate([l0_w_in[:, :1920], _pad_cols(kpe_w, LANE), _pad_cols(_rot_cols(kpe_w), LANE)],
                          axis=1).astype(BF16)
    dh = QK_NOPE + QK_ROPE
    q_nope = [l0_q_up[:, h * dh:h * dh + QK_NOPE] for h in range(MLA_HEADS)]
    q_pe = [l0_q_up[:, h * dh + QK_NOPE:(h + 1) * dh] for h in range(MLA_HEADS)]
    qup = jnp.concatenate(q_nope + [_pad_cols(w, LANE) for w in q_pe]
                          + [_pad_cols(_rot_cols(w), LANE) for w in q_pe], axis=1).astype(BF16)
    kvup = l0_kv_up.astype(BF16)
    front_w = (win, row1(l0_q_norm), qup, row1(l0_kv_norm), kvup, l0_conv_w, row1(l0_conv_b), row1(l0_hf_skip))
    w1p = jnp.pad(l0_hf_w1, ((0, LANE - l0_hf_w1.shape[0]), (0, 0)))
    two = lambda v: jnp.tile(row1(v), (1, 2))
    filt_w = (_block_diag2(w1p), two(l0_hf_b1), two(l0_hf_freq), _block_diag2(l0_hf_w2), two(l0_hf_b2),
              _block_diag2(l0_hf_w3).astype(BF16))
    wo0 = l0_w_out.astype(BF16)

    xc = x_prompt.reshape(nbc * lc, d)
    xs = x_sample.reshape(nbs * ls, d)
    groups = (
        dict(x=xc, nb=nbc, L=lc, mod_base=0, tiles_per_mod=nbc * lc // tm, dft_nb=4, tq=lc, hps=MLA_HEADS),
        dict(x=xs, nb=nbs, L=ls, mod_base=1, tiles_per_mod=ls // tm, dft_nb=nbs, tq=512, hps=1),
    )
    ones_tab = (jnp.concatenate([jnp.ones((tm, LANE), F32)], axis=0), jnp.zeros((tm, LANE), F32))

    outs = []
    ctx_ckv = ctx_krope = None
    for gi, g in enumerate(groups):
        nb, L = g["nb"], g["L"]
        tiles_per_seq = L // tm
        latent = gi == 1
        cos, sin = _rope_tables(L) if latent else ones_tab
        two_stage = L == FFT_R * FFT_R
        io_dtype = F32 if two_stage else BF16
        u, e, x0, q, k, vt, kvn, kpe = _front(g["x"], mods0, g["mod_base"], g["tiles_per_mod"], front_w,
                                              cos, sin, latent, tiles_per_seq, io_dtype)
        if latent:
            extra = _cache_kv(cache_l0_ckv.reshape(nbs * past, KV_LORA),
                              _pad_cols(cache_l0_krope.reshape(nbs * past, QK_ROPE), LANE), kvup)
        else:
            extra = None
            ctx_ckv = kvn.reshape(nb, L, KV_LORA)
            ctx_krope = kpe.reshape(nb, L, QK_ROPE)
        y_mla = _attention(q, k, vt, extra, nb, L, g["tq"], g["hps"])

        hfilt, hnorm = _filters(L, io_dtype, *filt_w)
        if two_stage:
            y_hy = _hyena_long(u, e, x0, hfilt, hnorm, nb)
        else:
            kre, kim = _dft("filt", "hy_fwd", [hfilt], [hnorm], 2)
            sh = (nb, L, HY_CH)
            yre, yim = _dft("fwdk", "hy_fwd", [u.reshape(sh)], [kre, kim], g["dft_nb"])
            y_hy = _dft("inv", "hy_inv", [yre, yim], [e.reshape(sh), x0.reshape(sh)], g["dft_nb"])
            y_hy = y_hy.reshape(nb * L, HY_CH)

        x1 = _post(g["x"], mods0, g["mod_base"], g["tiles_per_mod"], [y_hy, y_mla],
                   [wo0[:HY_CH], wo0[HY_CH:]], row1(l0_ln1_g), row1(l0_ln1_b),
                   l0_mlp_w1.astype(BF16), l0_mlp_w2.astype(BF16), row1(l0_ln2_g), row1(l0_ln2_b))

        if L == FFT_R * FFT_R:
            yf = _fnet_long(x1, mods1, g["mod_base"], nb)
        else:
            zc, zs = _fnet_front(x1, mods1, g["mod_base"], g["tiles_per_mod"])
            sh = (nb, L, d)
            yf = _dft("fnet", "fnet", [zc.reshape(sh), zs.reshape(sh)], [], min(g["dft_nb"], 2))
            yf = yf.reshape(nb * L, d)
        x2 = _post(x1, mods1, g["mod_base"], g["tiles_per_mod"], [yf],
                   [l1_w_out.astype(BF16)], row1(l1_ln1_g), row1(l1_ln1_b),
                   l1_mlp_w1.astype(BF16), l1_mlp_w2.astype(BF16), row1(l1_ln2_g), row1(l1_ln2_b))
        outs.append(x2.reshape(nb, L, d))

    return (outs[0], outs[1], ctx_ckv, ctx_krope)
```

```python
import functools
import math

import numpy as np
import jax
import jax.numpy as jnp
from jax import lax
from jax.experimental import pallas as pl
from jax.experimental.pallas import tpu as pltpu

F32 = jnp.float32
BF16 = jnp.bfloat16
HI = lax.Precision.HIGHEST

D_MODEL = 1024
DEPTH = 2
GRID_W = 64
HY_CH = 512
FILT_BANDS = 16
FILT_ORDER = 64
FAST_DECAY_PCT = 0.3
SLOW_DECAY_PCT = 1.5
DECAY_TARGET = 1e-2
MAX_DECAY = math.log(DECAY_TARGET) / FAST_DECAY_PCT
MIN_DECAY = math.log(DECAY_TARGET) / SLOW_DECAY_PCT
MLA_HEADS = 4
QK_NOPE = 128
QK_ROPE = 64
V_HEAD = 128
Q_LORA = 256
KV_LORA = 128
ROPE_THETA = 10000.0
FN_GROUP_CH = 128
D_FF = 4096
ALPHA = (2 * DEPTH) ** 0.25
LN_EPS = 1e-5
RMS_EPS = 1e-6

LANE = 128
ROW_TILE = 256
POST_TILE = 512
POST_SPLIT = 2
QK_PAD = 256
VT_ROWS = V_HEAD + 16
LOG2E = 1.4426950408889634
VMEM_LIMIT = 56 * 1024 * 1024


def _cparams(sem):
    return pltpu.CompilerParams(dimension_semantics=sem, vmem_limit_bytes=VMEM_LIMIT)


def _ln_plain(x):
    mu = jnp.mean(x, axis=-1, keepdims=True)
    xc = x - mu
    var = jnp.mean(xc * xc, axis=-1, keepdims=True)
    return xc * lax.rsqrt(var + LN_EPS)


def _rms(x, g):
    return x * lax.rsqrt(jnp.mean(x * x, axis=-1, keepdims=True) + RMS_EPS) * g


def _bdot(a, b):
    return jnp.dot(a.astype(BF16), b, preferred_element_type=F32)


def _vt_rows(v):
    ones = jnp.ones((VT_ROWS - V_HEAD, v.shape[0]), BF16)
    return jnp.concatenate([jnp.transpose(v).astype(BF16), ones], axis=0)


MOD_STREAMS = 3


def _mod_kernel(c_ref, *refs):
    w_refs, b_ref, o_ref = refs[:MOD_STREAMS], refs[MOD_STREAMS], refs[MOD_STREAMS + 1]
    c = c_ref[...]
    s = c / (1.0 + jnp.exp(-c))
    s_hi = s.astype(BF16)
    s_lo = (s - s_hi.astype(F32)).astype(BF16)
    s2 = jnp.concatenate([s_hi, s_lo], axis=0)
    nr = s.shape[0]
    tn = w_refs[0].shape[1]
    for k, w_ref in enumerate(w_refs):
        w = w_ref[...]
        w_hi = w.astype(BF16)
        w_lo = (w - w_hi.astype(F32)).astype(BF16)
        r1 = jnp.dot(s2, w_hi, preferred_element_type=F32)
        r2 = jnp.dot(s_hi, w_lo, preferred_element_type=F32)
        o_ref[:, k * tn:(k + 1) * tn] = r1[:nr] + r1[nr:] + r2 + b_ref[:, k * tn:(k + 1) * tn]


def _modulation(cond8, w, b):
    n = w.shape[1]
    tn = 512
    step = MOD_STREAMS * tn
    w_specs = [pl.BlockSpec((D_MODEL, tn), functools.partial(lambda j, k: (0, MOD_STREAMS * j + k), k=k))
               for k in range(MOD_STREAMS)]
    out = pl.pallas_call(
        _mod_kernel,
        out_shape=jax.ShapeDtypeStruct((8, n), F32),
        grid=(n // step,),
        in_specs=[pl.BlockSpec((8, D_MODEL), lambda j: (0, 0))] + w_specs
                 + [pl.BlockSpec((1, step), lambda j: (0, j))],
        out_specs=pl.BlockSpec((8, step), lambda j: (0, j)),
        compiler_params=_cparams(("arbitrary",)),
        name="modulation",
    )(cond8, *([w] * MOD_STREAMS), b.reshape(1, n))
    return out.reshape(8, 1, n)


def _mod_spec(mod_base, tiles_per_mod):
    return pl.BlockSpec((None, 1, 6 * D_MODEL), lambda i: (mod_base + i // tiles_per_mod, 0, 0))


def _const_spec(shape):
    nd = len(shape)
    return pl.BlockSpec(shape, lambda i: (0,) * nd)


HALO = 8


def _front_kernel(tiles_per_seq, x_ref, xp_ref, xn_ref, m_ref, win_ref, qn_ref, qup_ref, kvn_ref, kvup_ref,
                  cos_ref, sin_ref, cw_ref, cb_ref, skip_ref,
                  u_ref, e_ref, x0_ref, q_ref, k_ref, v_ref, kvn_out_ref, kpe_ref):
    i = pl.program_id(0)
    m = m_ref[...]
    tm = x_ref.shape[0]
    nh = 3 * HY_CH
    xe = jnp.concatenate([xp_ref[...], x_ref[...], xn_ref[...]], axis=0)
    he = _ln_plain(xe) * (1.0 + m[:, D_MODEL:2 * D_MODEL]) + m[:, 0:D_MODEL]

    zh = _bdot(he, win_ref[:, :nh])
    pos = i % tiles_per_seq
    rows = lax.broadcasted_iota(jnp.int32, (tm + 2 * HALO, 1), 0)
    inside = jnp.logical_and(jnp.logical_or(rows >= HALO, pos != 0),
                             jnp.logical_or(rows < tm + HALO, pos != tiles_per_seq - 1))
    zh = jnp.where(inside, zh, 0.0)
    cw = cw_ref[...]
    pz = (pltpu.roll(zh, 1, 0) * cw[0:1, :] + zh * cw[1:2, :]
          + pltpu.roll(zh, tm + 2 * HALO - 1, 0) * cw[2:3, :])[HALO:HALO + tm] + cb_ref[...]
    u = pz[:, 2 * HY_CH:] * pz[:, HY_CH:2 * HY_CH]
    u_ref[...] = u.astype(u_ref.dtype)
    e_ref[...] = u * skip_ref[...]
    x0_ref[...] = pz[:, :HY_CH]

    z = _bdot(he[HALO:HALO + tm], win_ref[:, nh:])
    q_c = z[:, 0:256]
    kv_c = z[:, 256:384]
    cos = cos_ref[...]
    sin = sin_ref[...]
    kpe = z[:, 384:512] * cos + z[:, 512:640] * sin
    kpe_ref[...] = kpe[:, :QK_ROPE]
    kpe_b = kpe.astype(BF16)
    q = _bdot(_rms(q_c, qn_ref[...]), qup_ref[...]) * (LOG2E / math.sqrt(QK_NOPE + QK_ROPE))
    kvn = _rms(kv_c, kvn_ref[...])
    kvn_out_ref[...] = kvn
    kv = _bdot(kvn, kvup_ref[...])
    for hd in range(MLA_HEADS):
        a = hd * LANE
        q_pe = (q[:, 512 + a:512 + a + LANE] * cos + q[:, 1024 + a:1024 + a + LANE] * sin).astype(BF16)
        q_ref[hd] = jnp.concatenate([q[:, a:a + LANE].astype(BF16), q_pe], axis=-1)
        k_ref[hd] = jnp.concatenate([kv[:, 2 * a:2 * a + LANE].astype(BF16), kpe_b], axis=-1)
        v_ref[hd] = _vt_rows(kv[:, 2 * a + LANE:2 * a + 2 * LANE])


def _front(x, mods, mod_base, tiles_per_mod, w, cos, sin, rope, tiles_per_seq, u_dtype):
    t = x.shape[0]
    tm = ROW_TILE
    win, qn, qup, kvn, kvup, conv_w, conv_b, skip = w
    if rope:
        tab_spec = pl.BlockSpec((tm, LANE), lambda i: (i % tiles_per_seq, 0))
    else:
        tab_spec = pl.BlockSpec((tm, LANE), lambda i: (0, 0))
    r8 = tm // HALO
    n8 = t // HALO
    hy_out = lambda dt: jax.ShapeDtypeStruct((t, HY_CH), dt)
    hy_spec = pl.BlockSpec((tm, HY_CH), lambda i: (i, 0))
    return pl.pallas_call(
        functools.partial(_front_kernel, tiles_per_seq),
        out_shape=(hy_out(u_dtype), hy_out(F32), hy_out(F32),
                   jax.ShapeDtypeStruct((MLA_HEADS, t, QK_PAD), BF16),
                   jax.ShapeDtypeStruct((MLA_HEADS, t, QK_PAD), BF16),
                   jax.ShapeDtypeStruct((MLA_HEADS, VT_ROWS, t), BF16),
                   jax.ShapeDtypeStruct((t, KV_LORA), F32),
                   jax.ShapeDtypeStruct((t, QK_ROPE), F32)),
        grid=(t // tm,),
        in_specs=[pl.BlockSpec((tm, D_MODEL), lambda i: (i, 0)),
                  pl.BlockSpec((HALO, D_MODEL), lambda i: (jnp.maximum(i * r8 - 1, 0), 0)),
                  pl.BlockSpec((HALO, D_MODEL), lambda i: (jnp.minimum((i + 1) * r8, n8 - 1), 0)),
                  _mod_spec(mod_base, tiles_per_mod),
                  _const_spec(win.shape), _const_spec(qn.shape), _const_spec(qup.shape),
                  _const_spec(kvn.shape), _const_spec(kvup.shape),
                  tab_spec, tab_spec,
                  _const_spec(conv_w.shape), _const_spec(conv_b.shape), _const_spec(skip.shape)],
        out_specs=(hy_spec, hy_spec, hy_spec,
                   pl.BlockSpec((MLA_HEADS, tm, QK_PAD), lambda i: (0, i, 0)),
                   pl.BlockSpec((MLA_HEADS, tm, QK_PAD), lambda i: (0, i, 0)),
                   pl.BlockSpec((MLA_HEADS, VT_ROWS, tm), lambda i: (0, 0, i)),
                   pl.BlockSpec((tm, KV_LORA), lambda i: (i, 0)),
                   pl.BlockSpec((tm, QK_ROPE), lambda i: (i, 0))),
        compiler_params=_cparams(("arbitrary",)),
        name="l0_front",
    )(x, x, x, mods, win, qn, qup, kvn, kvup, cos, sin, conv_w, conv_b, skip)


def _cache_kv_kernel(ckv_ref, kr_ref, kvup_ref, k_ref, v_ref):
    kv = _bdot(ckv_ref[...], kvup_ref[...])
    kr = kr_ref[...].astype(BF16)
    for hd in range(MLA_HEADS):
        a = 2 * hd * LANE
        k_ref[hd] = jnp.concatenate([kv[:, a:a + LANE].astype(BF16), kr], axis=-1)
        v_ref[hd] = _vt_rows(kv[:, a + LANE:a + 2 * LANE])


def _cache_kv(ckv, krope_pad, kvup):
    t = ckv.shape[0]
    return pl.pallas_call(
        _cache_kv_kernel,
        out_shape=(jax.ShapeDtypeStruct((MLA_HEADS, t, QK_PAD), BF16),
                   jax.ShapeDtypeStruct((MLA_HEADS, VT_ROWS, t), BF16)),
        name="l0_cache_kv",
    )(ckv, krope_pad, kvup)


def _col_reduce(x, op):
    rows, n = x.shape
    for g in (32, 8):
        if rows % (8 * g) == 0 and rows > 8 * g:
            x = op(x.reshape(rows // (8 * g), 8 * g, n), axis=0)
            rows = 8 * g
    return op(x, axis=0, keepdims=True)


def _attn_kernel(n_kv, hps, q_ref, *refs):
    k_refs, vt_refs = refs[:n_kv], refs[n_kv:2 * n_kv]
    o_ref, s_even, s_odd = refs[2 * n_kv:]
    i = pl.program_id(0)

    @pl.when(i == 0)
    def _():
        s_odd[...] = jnp.zeros_like(s_odd)

    def step(s_write, s_read):
        nt = (((1,), (1,)), ((), ()))
        for h in range(hps):
            q = q_ref[h]
            r0 = 0
            for k_ref in k_refs:
                lk = k_ref.shape[1]
                s_write[h, r0:r0 + lk, :] = lax.dot_general(k_ref[h], q, nt, preferred_element_type=F32)
                r0 += lk
        for h in range(hps):
            s = s_read[h]
            m = _col_reduce(s, jnp.max)
            pb = jnp.exp2(s - m).astype(BF16)
            acc = None
            r0 = 0
            for vt_ref in vt_refs:
                lk = vt_ref.shape[2]
                pv = jnp.dot(vt_ref[h], pb[r0:r0 + lk, :], preferred_element_type=F32)
                acc = pv if acc is None else acc + pv
                r0 += lk
            o_ref[:, h * V_HEAD:(h + 1) * V_HEAD] = jnp.transpose(
                acc[:V_HEAD] / acc[V_HEAD:V_HEAD + 1]).astype(o_ref.dtype)

    pl.when(i % 2 == 0)(lambda: step(s_even, s_odd))
    pl.when(i % 2 == 1)(lambda: step(s_odd, s_even))


def _attention(q, k, vt, extra, nb, lq, tq, hps):
    nq = lq // tq
    ng = MLA_HEADS // hps
    n_tiles = nb * ng * nq

    def where(t):
        bh = t // nq
        return bh // ng, bh % ng, t % nq

    def score_side(fn):
        return lambda i: fn(*where(jnp.minimum(i, n_tiles - 1)))

    def value_side(fn):
        return lambda i: fn(*where(jnp.maximum(i - 1, 0)))

    ks, vts = [k], [vt]
    if extra is not None:
        ks.append(extra[0])
        vts.append(extra[1])
    in_specs = [pl.BlockSpec((hps, tq, QK_PAD), score_side(lambda b, h, j: (h, b * nq + j, 0)))]
    in_specs += [pl.BlockSpec((hps, a.shape[1] // nb, QK_PAD), score_side(lambda b, h, j: (h, b, 0))) for a in ks]
    in_specs += [pl.BlockSpec((hps, VT_ROWS, a.shape[2] // nb), value_side(lambda b, h, j: (h, 0, b))) for a in vts]
    lk_total = sum(a.shape[1] // nb for a in ks)
    return pl.pallas_call(
        functools.partial(_attn_kernel, len(ks), hps),
        out_shape=jax.ShapeDtypeStruct((nb * lq, MLA_HEADS * V_HEAD), BF16),
        grid=(n_tiles + 1,),
        in_specs=in_specs,
        out_specs=pl.BlockSpec((tq, hps * V_HEAD), value_side(lambda b, h, j: (b * nq + j, h))),
        scratch_shapes=[pltpu.VMEM((hps, lk_total, tq), F32), pltpu.VMEM((hps, lk_total, tq), F32)],
        compiler_params=_cparams(("arbitrary",)),
        name="l0_attention",
    )(q, *ks, *vts)


def _filter_kernel(z_ref, w1_ref, b1_ref, fr_ref, w2_ref, b2_ref, w3_ref, dl_ref, h_ref, norm_ref):
    i = pl.program_id(0)
    z = z_ref[...]
    tl = z.shape[0]
    fr = fr_ref[...]
    z2 = jnp.concatenate([z[:tl // 2], z[tl // 2:]], axis=1)
    h = jnp.sin(fr * (jnp.dot(z2, w1_ref[...], precision=HI, preferred_element_type=F32) + b1_ref[...]))
    h = jnp.sin(fr * (jnp.dot(h, w2_ref[...], precision=HI, preferred_element_type=F32) + b2_ref[...]))
    h = _bdot(h, w3_ref[...])
    h = jnp.concatenate([h[:, :2 * HY_CH], h[:, 2 * HY_CH:]], axis=0)
    decay = jnp.exp(-(z[:, 0:1] * dl_ref[...]))
    hf = h[:, :HY_CH] * decay
    hb = h[:, HY_CH:] * decay
    part = jnp.sum(jnp.abs(hf) + jnp.abs(hb), axis=0, keepdims=True)

    @pl.when(i == 0)
    def _():
        norm_ref[...] = part

    @pl.when(i > 0)
    def _():
        norm_ref[...] += part

    rows = lax.broadcasted_iota(jnp.int32, hb.shape, 0) + i * tl
    h_ref[0] = hf.astype(h_ref.dtype)
    h_ref[1] = jnp.where(rows == 0, 0.0, hb).astype(h_ref.dtype)


def _filter_embedding(L):
    t = np.linspace(0.0, 1.0, L)[:, None]
    w_ang = 2.0 * np.pi * np.arange(L) / L
    bands = np.linspace(1e-4, FILT_BANDS - 1, FILT_BANDS)
    ang = w_ang[:, None] * bands[None, :]
    z = np.zeros((L, LANE), np.float64)
    z[:, 0:1] = t
    z[:, 1:1 + FILT_BANDS] = np.cos(ang)
    z[:, 1 + FILT_BANDS:1 + 2 * FILT_BANDS] = -np.sin(ang)
    return jnp.asarray(z, F32)


def _filters(L, h_dtype, w1p, b1, fr, w2, b2, w3):
    tl = min(L, 512)
    z = _filter_embedding(L)
    deltas = jnp.asarray(np.abs(np.linspace(MIN_DECAY, MAX_DECAY, HY_CH))[None, :], F32)
    return pl.pallas_call(
        _filter_kernel,
        out_shape=(jax.ShapeDtypeStruct((2, L, HY_CH), h_dtype), jax.ShapeDtypeStruct((1, HY_CH), F32)),
        grid=(L // tl,),
        in_specs=[pl.BlockSpec((tl, LANE), lambda i: (i, 0)),
                  _const_spec(w1p.shape), _const_spec(b1.shape), _const_spec(fr.shape),
                  _const_spec(w2.shape), _const_spec(b2.shape), _const_spec(w3.shape),
                  _const_spec(deltas.shape)],
        out_specs=(pl.BlockSpec((2, tl, HY_CH), lambda i: (0, i, 0)),
                   pl.BlockSpec((1, HY_CH), lambda i: (0, 0))),
        compiler_params=_cparams(("arbitrary",)),
        name="l0_hyena_filters",
    )(z, w1p, b1, fr, w2, b2, w3, deltas)


def _dft_tables(kind, L, ti):
    ni = L // ti
    i = np.arange(ti, dtype=np.int64)[:, None]
    big = (np.arange(ni, dtype=np.int64) * ti)[:, None]
    c = np.arange(L, dtype=np.int64)[None, :]
    if kind == "hy_fwd":
        period = 4 * L
        base_idx = (2 * i + 1) * c
        r_idx = 2 * big * c
        scale = 1.0
    elif kind == "hy_inv":
        period = 4 * L
        base_idx = (2 * c + 1) * i
        r_idx = (2 * c + 1) * big
        scale = 1.0 / L
    else:
        period = L
        base_idx = i * c
        r_idx = big * c
        scale = 1.0 / math.sqrt(L * FN_GROUP_CH)
    ab = 2.0 * np.pi * (base_idx % period) / period
    ar = 2.0 * np.pi * (r_idx % period) / period
    return (jnp.asarray(np.cos(ab), F32), jnp.asarray(np.sin(ab), F32),
            jnp.asarray(scale * np.cos(ar), F32).reshape(ni, 1, L),
            jnp.asarray(scale * np.sin(ar), F32).reshape(ni, 1, L))


def _dft_kernel(mode, nb, n_x, *refs):
    bc_ref, bs_ref, rc_ref, rs_ref = refs[:4]
    x_refs = refs[4:4 + n_x]
    rest = refs[4 + n_x:]
    p_ref, q_ref = rest[-2], rest[-1]
    j = pl.program_id(2)
    nj = pl.num_programs(2)
    tj = x_refs[0].shape[1]
    if bc_ref.shape[1] == tj:
        bc, bs, rc, rs = bc_ref[...], bs_ref[...], rc_ref[...], rs_ref[...]
    else:
        off = pl.multiple_of(j * tj, tj)
        bc, bs = bc_ref[:, pl.ds(off, tj)], bs_ref[:, pl.ds(off, tj)]
        rc, rs = rc_ref[:, pl.ds(off, tj)], rs_ref[:, pl.ds(off, tj)]
    tc = (bc * rc - bs * rs).astype(BF16)
    ts = (bs * rc + bc * rs).astype(BF16)
    x1_ref = x_refs[0]
    x2_ref = x_refs[-1]

    pq = [(jnp.dot(tc, x1_ref[b], preferred_element_type=F32),
           jnp.dot(ts, x2_ref[b], preferred_element_type=F32)) for b in range(nb)]

    @pl.when(j == 0)
    def _():
        for b in range(nb):
            p_ref[b] = pq[b][0]
            q_ref[b] = pq[b][1]

    @pl.when(j > 0)
    def _():
        for b in range(nb):
            p_ref[b] += pq[b][0]
            q_ref[b] += pq[b][1]

    @pl.when(j == nj - 1)
    def _():
        if mode == "filt":
            nrm = rest[0][...]
            kre_ref, kim_ref = rest[1], rest[2]
            kre_ref[...] = (p_ref[0] + p_ref[1]) / nrm
            kim_ref[...] = (q_ref[1] - q_ref[0]) / nrm
        elif mode == "fwdk":
            kre, kim = rest[0][...], rest[1][...]
            yre_ref, yim_ref = rest[2], rest[3]
            for b in range(nb):
                pp, qq = p_ref[b], q_ref[b]
                yre_ref[b] = (pp * kre + qq * kim).astype(BF16)
                yim_ref[b] = (pp * kim - qq * kre).astype(BF16)
        elif mode == "inv":
            e_ref, x0_ref, o_ref = rest[0], rest[1], rest[2]
            for b in range(nb):
                o_ref[b] = ((p_ref[b] - q_ref[b] + e_ref[b]) * x0_ref[b]).astype(BF16)
        else:
            o_ref = rest[0]
            for b in range(nb):
                o_ref[b] = (p_ref[b] - q_ref[b]).astype(BF16)


def _dft(mode, kind, xs, extras, nb):
    B, L, C = xs[0].shape
    ti = min(L, 256)
    tj = min(L, 512)
    bc, bs, rc, rs = _dft_tables(kind, L, ti)
    grid = (B // nb, L // ti, L // tj)
    x_spec = pl.BlockSpec((nb, tj, C), lambda g, i, j: (g, j, 0))
    row_spec = lambda c, dt=None: pl.BlockSpec((nb, ti, c), lambda g, i, j: (g, i, 0))
    in_specs = [pl.BlockSpec((ti, L), lambda g, i, j: (0, 0)),
                pl.BlockSpec((ti, L), lambda g, i, j: (0, 0)),
                pl.BlockSpec((None, 1, L), lambda g, i, j: (i, 0, 0)),
                pl.BlockSpec((None, 1, L), lambda g, i, j: (i, 0, 0))] + [x_spec] * len(xs)
    if mode == "filt":
        in_specs += [pl.BlockSpec((1, HY_CH), lambda g, i, j: (0, 0))]
        out_shape = (jax.ShapeDtypeStruct((L, HY_CH), F32),) * 2
        out_specs = (pl.BlockSpec((ti, HY_CH), lambda g, i, j: (i, 0)),) * 2
    elif mode == "fwdk":
        in_specs += [pl.BlockSpec((ti, HY_CH), lambda g, i, j: (i, 0))] * 2
        out_shape = (jax.ShapeDtypeStruct((B, L, C), BF16),) * 2
        out_specs = (row_spec(C),) * 2
    elif mode == "inv":
        in_specs += [row_spec(C)] * 2
        out_shape = jax.ShapeDtypeStruct((B, L, C), BF16)
        out_specs = row_spec(C)
    else:
        out_shape = jax.ShapeDtypeStruct((B, L, C), BF16)
        out_specs = row_spec(C)
    return pl.pallas_call(
        functools.partial(_dft_kernel, mode, nb, len(xs)),
        out_shape=out_shape,
        grid=grid,
        in_specs=in_specs,
        out_specs=out_specs,
        scratch_shapes=[pltpu.VMEM((nb, ti, C), F32), pltpu.VMEM((nb, ti, C), F32)],
        compiler_params=_cparams(("arbitrary", "arbitrary", "arbitrary")),
        name="dft_" + mode,
    )(bc, bs, rc, rs, *xs, *extras)


FFT_R = 64
FFT_KF = 16
FFT_NK = FFT_R // FFT_KF


def _pack_pairs(x):
    return pltpu.bitcast(x.astype(BF16), jnp.uint32)


def _unpack_pairs(w):
    return pltpu.bitcast(w, BF16)


def _to_blocks(w):
    return jnp.swapaxes(w.reshape(FFT_R, FFT_KF, w.shape[-1]), 0, 1)


def _from_blocks(ws):
    c = ws[0].shape[-1]
    return jnp.swapaxes(jnp.stack(ws, axis=0), 0, 1).reshape(FFT_NK, FFT_KF, FFT_KF, c)


def _lead_in_kernel(g_ref, x_ref, o_ref):
    g = g_ref[...]
    xt = jnp.swapaxes(x_ref[...], 0, 1)
    for j in range(FFT_KF):
        o_ref[j] = _pack_pairs(jnp.dot(g, xt[j].astype(BF16), preferred_element_type=F32))


def _lead_in(g, x, name):
    nbx, _, _, c = x.shape
    m2 = g.shape[0] // 2
    return pl.pallas_call(
        _lead_in_kernel,
        out_shape=jax.ShapeDtypeStruct((nbx, FFT_NK, FFT_KF, m2, c), jnp.uint32),
        grid=(nbx, FFT_R // FFT_KF),
        in_specs=[pl.BlockSpec(g.shape, lambda b, k: (0, 0)),
                  pl.BlockSpec((None, FFT_R, FFT_KF, c), lambda b, k: (b, 0, k, 0))],
        out_specs=pl.BlockSpec((None, None, FFT_KF, m2, c), lambda b, k: (b, k, 0, 0, 0)),
        compiler_params=_cparams(("arbitrary", "arbitrary")),
        name=name,
    )(g, x)


def _lead_out_kernel(n_extra, g_ref, w_ref, *rest):
    g = g_ref[...]
    o_ref = rest[-1]
    ys = [jnp.dot(g, _unpack_pairs(w_ref[j]), preferred_element_type=F32) for j in range(FFT_KF)]
    y = jnp.swapaxes(jnp.stack(ys, axis=0), 0, 1)
    if n_extra:
        y = (y + rest[0][...]) * rest[1][...]
    o_ref[...] = y


def _lead_out(g, w, extras, name):
    nb, _, _, k2, c = w.shape
    blk = pl.BlockSpec((None, FFT_R, FFT_KF, c), lambda b, k: (b, 0, k, 0))
    return pl.pallas_call(
        functools.partial(_lead_out_kernel, len(extras)),
        out_shape=jax.ShapeDtypeStruct((nb, FFT_R, FFT_R, c), F32),
        grid=(nb, FFT_R // FFT_KF),
        in_specs=[pl.BlockSpec(g.shape, lambda b, k: (0, 0)),
                  pl.BlockSpec((None, None, FFT_KF, k2, c), lambda b, k: (b, k, 0, 0, 0))] + [blk] * len(extras),
        out_specs=blk,
        compiler_params=_cparams(("arbitrary", "arbitrary")),
        name=name,
    )(g, w, *extras)


def _interleave(a, b, axis):
    st = np.stack([a, b], axis=axis + 1)
    shape = list(a.shape)
    shape[axis] *= 2
    return st.reshape(shape)


def _hy2_tables():
    L = FFT_R * FFT_R
    n2 = 2 * L
    f1 = np.arange(2 * FFT_R, dtype=np.int64)
    s1 = np.arange(FFT_R, dtype=np.int64)
    th = np.pi * (((2 * f1[:, None] + 1) * s1[None, :]) % (4 * FFT_R)) / (2 * FFT_R)
    ga = _interleave(np.cos(th), -np.sin(th), 0)
    ma = _interleave(np.cos(th).T, -np.sin(th).T, 1) / L
    f2 = np.arange(FFT_R // 2, dtype=np.int64)
    s2 = np.arange(FFT_R, dtype=np.int64)
    idx = ((n2 // FFT_R) * 2 * f2[None, :, None] * s2[None, None, :]
           + (2 * f1[:, None, None] + 1) * s2[None, None, :]) % (2 * n2)
    al = np.pi * idx / n2
    c, s = np.cos(al), np.sin(al)
    nmat = np.concatenate([_interleave(c, s, 2), _interleave(-s, c, 2)], axis=1)
    ct, st = np.transpose(c, (0, 2, 1)), np.transpose(s, (0, 2, 1))
    mmat = _interleave(np.concatenate([ct, -st], axis=2), np.concatenate([st, ct], axis=2), 1)
    bf = lambda a: jnp.asarray(a, F32).astype(BF16)
    return bf(ga), bf(nmat), bf(mmat), bf(ma)


def _hy_mid_kernel(a_ref, n_ref, m_ref, k_ref, e_ref):
    half = FFT_R // 2
    a = _to_blocks(a_ref[...])
    ts = [jnp.dot(n_ref[j], _unpack_pairs(a[j]), preferred_element_type=F32)
          for j in range(FFT_KF)]
    ys = []
    for j in range(FFT_KF):
        tr, ti = ts[j][:half], ts[j][half:]
        kr, ki = k_ref[j, 0], k_ref[j, 1]
        ys.append(jnp.concatenate([tr * kr - ti * ki, tr * ki + ti * kr], axis=0).astype(BF16))
    e_ref[...] = _from_blocks([_pack_pairs(jnp.dot(m_ref[j], ys[j], preferred_element_type=F32))
                               for j in range(FFT_KF)])


def _hy_mid(a, nmat, mmat, khat):
    nb = a.shape[0]
    c = a.shape[-1]
    nf1 = 2 * FFT_R
    blk = pl.BlockSpec((None, FFT_NK, FFT_KF, FFT_KF, c), lambda i, b: (b, 0, 0, i, 0))
    return pl.pallas_call(
        _hy_mid_kernel,
        out_shape=jax.ShapeDtypeStruct(a.shape, jnp.uint32),
        grid=(nf1 // FFT_KF, nb),
        in_specs=[blk,
                  pl.BlockSpec((FFT_KF, FFT_R, 2 * FFT_R), lambda i, b: (i, 0, 0)),
                  pl.BlockSpec((FFT_KF, 2 * FFT_R, FFT_R), lambda i, b: (i, 0, 0)),
                  pl.BlockSpec((FFT_KF, 2, FFT_R // 2, c), lambda i, b: (i, 0, 0, 0))],
        out_specs=blk,
        compiler_params=_cparams(("arbitrary", "arbitrary")),
        name="l0_hyena_mid",
    )(a, nmat, mmat, khat)


def _hy_kfilt_kernel(a_ref, n_ref, nrm_ref, k_ref):
    half = FFT_R // 2
    nrm = nrm_ref[...]
    af, ab = _to_blocks(a_ref[0]), _to_blocks(a_ref[1])
    for j in range(FFT_KF):
        tf = jnp.dot(n_ref[j], _unpack_pairs(af[j]), preferred_element_type=F32)
        tb = jnp.dot(n_ref[j], _unpack_pairs(ab[j]), preferred_element_type=F32)
        k_ref[j, 0] = (tf[:half] + tb[:half]) / nrm
        k_ref[j, 1] = (tf[half:] - tb[half:]) / nrm


def _hy_kfilt(a, nmat, nrm):
    c = a.shape[-1]
    nf1 = 2 * FFT_R
    return pl.pallas_call(
        _hy_kfilt_kernel,
        out_shape=jax.ShapeDtypeStruct((nf1, 2, FFT_R // 2, c), F32),
        grid=(nf1 // FFT_KF,),
        in_specs=[pl.BlockSpec((2, FFT_NK, FFT_KF, FFT_KF, c), lambda i: (0, 0, 0, i, 0)),
                  pl.BlockSpec((FFT_KF, FFT_R, 2 * FFT_R), lambda i: (i, 0, 0)),
                  _const_spec(nrm.shape)],
        out_specs=pl.BlockSpec((FFT_KF, 2, FFT_R // 2, c), lambda i: (i, 0, 0, 0)),
        compiler_params=_cparams(("arbitrary",)),
        name="l0_hyena_kfilt",
    )(a, nmat, nrm)


def _hyena_long(u, e, x0, hfilt, nrm, nb):
    L = FFT_R * FFT_R
    c = u.shape[-1]
    v4 = lambda a, n: a.reshape(n, FFT_R, FFT_R, c)
    ga, nmat, mmat, ma = _hy2_tables()
    khat = _hy_kfilt(_lead_in(ga, v4(hfilt, 2), "l0_hyena_fwd_a"), nmat, nrm)
    ee = _hy_mid(_lead_in(ga, v4(u, nb), "l0_hyena_fwd_a"), nmat, mmat, khat)
    y = _lead_out(ma, ee, [v4(e, nb), v4(x0, nb)], "l0_hyena_inv_a")
    return y.reshape(nb * L, c)


def _fn2_tables():
    L = FFT_R * FFT_R
    r = np.arange(FFT_R, dtype=np.int64)
    idx = (FFT_R * r[None, :, None] * r[None, None, :] + r[None, :, None] * r[:, None, None]) % L
    gm = 2.0 * np.pi * idx / L
    c, s = np.cos(gm), np.sin(gm)
    g1 = _interleave(np.concatenate([c, -s], axis=2), np.concatenate([-s, -c], axis=2), 1)
    dl = 2.0 * np.pi * ((r[:, None] * r[None, :]) % FFT_R) / FFT_R
    g2 = _interleave(np.cos(dl), np.sin(dl), 1) / math.sqrt(L * FN_GROUP_CH)
    bf = lambda a: jnp.asarray(a, F32).astype(BF16)
    return bf(g1), bf(g2)


def _fnet_s1_kernel(x_ref, m_ref, cs_ref, g1_ref, o_ref, zc_ref, zs_ref):
    xs = jnp.swapaxes(x_ref[...], 0, 1).reshape(FFT_KF * FFT_R, D_MODEL)
    m = m_ref[...]
    h = (_ln_plain(xs) * (1.0 + m[:, D_MODEL:2 * D_MODEL]) + m[:, 0:D_MODEL]).astype(BF16)
    cs = cs_ref[...]
    for g in range(D_MODEL // FN_GROUP_CH):
        a = g * FN_GROUP_CH
        z = jnp.dot(h[:, a:a + FN_GROUP_CH], cs, preferred_element_type=F32)
        zc_ref[:, a:a + FN_GROUP_CH] = z[:, :FN_GROUP_CH].astype(BF16)
        zs_ref[:, a:a + FN_GROUP_CH] = z[:, FN_GROUP_CH:].astype(BF16)
    ws = []
    for j in range(FFT_KF):
        r0 = j * FFT_R
        s = jnp.concatenate([zc_ref[r0:r0 + FFT_R, :], zs_ref[r0:r0 + FFT_R, :]], axis=0)
        ws.append(_pack_pairs(jnp.dot(g1_ref[j], s, preferred_element_type=F32)))
    o_ref[...] = _from_blocks(ws)


def _fnet_long(x, mods, mod_base, nb):
    L = FFT_R * FFT_R
    d = D_MODEL
    g1, g2 = _fn2_tables()
    cs = _group_dft_table()
    bb = pl.pallas_call(
        _fnet_s1_kernel,
        out_shape=jax.ShapeDtypeStruct((nb, FFT_NK, FFT_KF, FFT_R, d), jnp.uint32),
        grid=(nb, FFT_R // FFT_KF),
        in_specs=[pl.BlockSpec((None, FFT_R, FFT_KF, d), lambda b, k: (b, 0, k, 0)),
                  pl.BlockSpec((None, 1, 6 * d), lambda b, k: (mod_base + b, 0, 0)),
                  pl.BlockSpec(cs.shape, lambda b, k: (0, 0)),
                  pl.BlockSpec((FFT_KF, 2 * FFT_R, 2 * FFT_R), lambda b, k: (k, 0, 0))],
        out_specs=pl.BlockSpec((None, FFT_NK, FFT_KF, FFT_KF, d), lambda b, k: (b, 0, 0, k, 0)),
        scratch_shapes=[pltpu.VMEM((FFT_KF * FFT_R, d), BF16), pltpu.VMEM((FFT_KF * FFT_R, d), BF16)],
        compiler_params=_cparams(("arbitrary", "arbitrary")),
        name="l1_fnet_stage1",
    )(x.reshape(nb, FFT_R, FFT_R, d), mods, cs, g1)
    y = _lead_out(g2, bb, [], "l1_fnet_stage2")
    return y.reshape(nb * L, d)


def _group_dft_table():
    g = FN_GROUP_CH
    jk = (np.arange(g, dtype=np.int64)[:, None] * np.arange(g, dtype=np.int64)[None, :]) % g
    ang = 2.0 * np.pi * jk / g
    return jnp.asarray(np.concatenate([np.cos(ang), np.sin(ang)], axis=1), F32).astype(BF16)


def _fnet_front_kernel(x_ref, m_ref, cs_ref, zc_ref, zs_ref):
    m = m_ref[...]
    h = (_ln_plain(x_ref[...]) * (1.0 + m[:, D_MODEL:2 * D_MODEL]) + m[:, 0:D_MODEL]).astype(BF16)
    cs = cs_ref[...]
    for g in range(D_MODEL // FN_GROUP_CH):
        a = g * FN_GROUP_CH
        z = jnp.dot(h[:, a:a + FN_GROUP_CH], cs, preferred_element_type=F32)
        zc_ref[:, a:a + FN_GROUP_CH] = z[:, :FN_GROUP_CH].astype(BF16)
        zs_ref[:, a:a + FN_GROUP_CH] = z[:, FN_GROUP_CH:].astype(BF16)


def _fnet_front(x, mods, mod_base, tiles_per_mod):
    t = x.shape[0]
    tm = ROW_TILE
    cs = _group_dft_table()
    return pl.pallas_call(
        _fnet_front_kernel,
        out_shape=(jax.ShapeDtypeStruct((t, D_MODEL), BF16),) * 2,
        grid=(t // tm,),
        in_specs=[pl.BlockSpec((tm, D_MODEL), lambda i: (i, 0)),
                  _mod_spec(mod_base, tiles_per_mod),
                  _const_spec(cs.shape)],
        out_specs=(pl.BlockSpec((tm, D_MODEL), lambda i: (i, 0)),) * 2,
        compiler_params=_cparams(("arbitrary",)),
        name="l1_fnet_front",
    )(x, mods, cs)


def _post_kernel(n_a, *refs):
    x_ref, m_ref = refs[0], refs[1]
    a_refs = refs[2:2 + n_a]
    wo_refs = refs[2 + n_a:2 + 2 * n_a]
    g1_ref, b1_ref, w1_ref, w2_ref, g2_ref, b2_ref, o_ref = refs[2 + 2 * n_a:]
    m = m_ref[...]
    d = D_MODEL
    tm = x_ref.shape[0]
    halves = [(r, r + tm // POST_SPLIT) for r in range(0, tm, tm // POST_SPLIT)]
    outs = []
    for r0, r1 in halves:
        out = _bdot(a_refs[0][r0:r1, :], wo_refs[0][...])
        for a_ref, wo_ref in zip(a_refs[1:], wo_refs[1:]):
            out += _bdot(a_ref[r0:r1, :], wo_ref[...])
        outs.append(out)
    x1s, hs = [], []
    for (r0, r1), out in zip(halves, outs):
        x1 = _ln_plain(ALPHA * x_ref[r0:r1, :] + m[:, 2 * d:3 * d] * out) * g1_ref[...] + b1_ref[...]
        x1s.append(x1)
        hs.append((_ln_plain(x1) * (1.0 + m[:, 4 * d:5 * d]) + m[:, 3 * d:4 * d]).astype(BF16))
    accs = []
    n_c = D_FF // d

    def up(h, c):
        hc = jnp.maximum(jnp.dot(h, w1_ref[:, c * d:(c + 1) * d], preferred_element_type=F32), 0.0)
        return (hc * hc).astype(BF16)

    for h in hs:
        acc = None
        nxt = up(h, 0)
        for c in range(n_c):
            cur = nxt
            if c + 1 < n_c:
                nxt = up(h, c + 1)
            part = jnp.dot(cur, w2_ref[c * d:(c + 1) * d, :], preferred_element_type=F32)
            acc = part if acc is None else acc + part
        accs.append(acc)
    for (r0, r1), x1, acc in zip(halves, x1s, accs):
        o_ref[r0:r1, :] = _ln_plain(ALPHA * x1 + m[:, 5 * d:6 * d] * acc) * g2_ref[...] + b2_ref[...]


def _post(x, mods, mod_base, tiles_per_mod, a_list, wo_list, g1, b1, w1, w2, g2, b2):
    t = x.shape[0]
    tm = POST_TILE
    row = lambda c: pl.BlockSpec((tm, c), lambda i: (i, 0))
    once = lambda v: pl.BlockSpec(v.shape, lambda i: (0,) * v.ndim, pipeline_mode=pl.Buffered(1))
    in_specs = ([row(D_MODEL), _mod_spec(mod_base, tiles_per_mod * ROW_TILE // tm)]
                + [row(a.shape[1]) for a in a_list]
                + [once(w) for w in wo_list]
                + [once(v) for v in (g1, b1, w1, w2, g2, b2)])
    return pl.pallas_call(
        functools.partial(_post_kernel, len(a_list)),
        out_shape=jax.ShapeDtypeStruct((t, D_MODEL), F32),
        grid=(t // tm,),
        in_specs=in_specs,
        out_specs=row(D_MODEL),
        compiler_params=_cparams(("arbitrary",)),
        name="post_mlp",
    )(x, mods, *a_list, *wo_list, g1, b1, w1, w2, g2, b2)


def _rot_cols(w):
    parts = []
    for seg in range(2):
        o = seg * 32
        parts += [-w[:, o + 16:o + 32], w[:, o:o + 16]]
    return jnp.concatenate(parts, axis=1)


def _pad_cols(w, n):
    return jnp.pad(w, ((0, 0), (0, n - w.shape[1])))


def _block_diag2(w):
    z = jnp.zeros_like(w)
    return jnp.concatenate([jnp.concatenate([w, z], axis=1), jnp.concatenate([z, w], axis=1)], axis=0)


def _rope_tables(L):
    rows = L // GRID_W
    row = np.repeat(np.arange(rows, dtype=np.float64), GRID_W)
    col = np.tile(np.arange(GRID_W, dtype=np.float64), rows)
    half = QK_ROPE // 2
    inv = 1.0 / (ROPE_THETA ** (np.arange(0, half, 2, dtype=np.float64) / half))
    ar = row[:, None] * inv[None, :]
    ac = col[:, None] * inv[None, :]
    ang = np.concatenate([ar, ar, ac, ac], axis=1)
    cos = np.concatenate([np.cos(ang), np.ones_like(ang)], axis=1)
    sin = np.concatenate([np.sin(ang), np.zeros_like(ang)], axis=1)
    return jnp.asarray(cos, F32), jnp.asarray(sin, F32)


def kernel(x_prompt, x_sample, cache_l0_ckv, cache_l0_krope, c, c_ctx, l0_ada_w, l0_ada_b, l0_w_in, l0_conv_w, l0_conv_b, l0_hf_w1, l0_hf_b1, l0_hf_freq, l0_hf_w2, l0_hf_b2, l0_hf_w3, l0_hf_skip, l0_q_norm, l0_q_up, l0_kv_norm, l0_kv_up, l0_w_out, l0_ln1_g, l0_ln1_b, l0_mlp_w1, l0_mlp_w2, l0_ln2_g, l0_ln2_b, l1_ada_w, l1_ada_b, l1_w_out, l1_ln1_g, l1_ln1_b, l1_mlp_w1, l1_mlp_w2, l1_ln2_g, l1_ln2_b):
    nbc, lc, d = x_prompt.shape
    nbs, ls, _ = x_sample.shape
    past = cache_l0_ckv.shape[1]
    tm = ROW_TILE
    row1 = lambda v: v.reshape(1, -1)

    cond8 = jnp.concatenate([c_ctx[None, :], c, jnp.zeros((8 - 1 - nbs, d), F32)], axis=0)
    mods0 = _modulation(cond8, l0_ada_w, l0_ada_b)
    mods1 = _modulation(cond8, l1_ada_w, l1_ada_b)

    kpe_w = l0_w_in[:, 1920:1984]
    win = jnp.concatenate([l0_w_in[:, :1920], _pad_cols(kpe_w, LANE), _pad_cols(_rot_cols(kpe_w), LANE)],
                          axis=1).astype(BF16)
    dh = QK_NOPE + QK_ROPE
    q_nope = [l0_q_up[:, h * dh:h * dh + QK_NOPE] for h in range(MLA_HEADS)]
    q_pe = [l0_q_up[:, h * dh + QK_NOPE:(h + 1) * dh] for h in range(MLA_HEADS)]
    qup = jnp.concatenate(q_nope + [_pad_cols(w, LANE) for w in q_pe]
                          + [_pad_cols(_rot_cols(w), LANE) for w in q_pe], axis=1).astype(BF16)
    kvup = l0_kv_up.astype(BF16)
    front_w = (win, row1(l0_q_norm), qup, row1(l0_kv_norm), kvup, l0_conv_w, row1(l0_conv_b), row1(l0_hf_skip))
    w1p = jnp.pad(l0_hf_w1, ((0, LANE - l0_hf_w1.shape[0]), (0, 0)))
    two = lambda v: jnp.tile(row1(v), (1, 2))
    filt_w = (_block_diag2(w1p), two(l0_hf_b1), two(l0_hf_freq), _block_diag2(l0_hf_w2), two(l0_hf_b2),
              _block_diag2(l0_hf_w3).astype(BF16))
    wo0 = l0_w_out.astype(BF16)

    xc = x_prompt.reshape(nbc * lc, d)
    xs = x_sample.reshape(nbs * ls, d)
    groups = (
        dict(x=xc, nb=nbc, L=lc, mod_base=0, tiles_per_mod=nbc * lc // tm, dft_nb=4, tq=lc, hps=MLA_HEADS),
        dict(x=xs, nb=nbs, L=ls, mod_base=1, tiles_per_mod=ls // tm, dft_nb=nbs, tq=512, hps=1),
    )
    ones_tab = (jnp.concatenate([jnp.ones((tm, LANE), F32)], axis=0), jnp.zeros((tm, LANE), F32))

    outs = []
    ctx_ckv = ctx_krope = None
    for gi, g in enumerate(groups):
        nb, L = g["nb"], g["L"]
        tiles_per_seq = L // tm
        latent = gi == 1
        cos, sin = _rope_tables(L) if latent else ones_tab
        two_stage = L == FFT_R * FFT_R
        io_dtype = F32 if two_stage else BF16
        u, e, x0, q, k, vt, kvn, kpe = _front(g["x"], mods0, g["mod_base"], g["tiles_per_mod"], front_w,
                                              cos, sin, latent, tiles_per_seq, io_dtype)
        if latent:
            extra = _cache_kv(cache_l0_ckv.reshape(nbs * past, KV_LORA),
                              _pad_cols(cache_l0_krope.reshape(nbs * past, QK_ROPE), LANE), kvup)
        else:
            extra = None
            ctx_ckv = kvn.reshape(nb, L, KV_LORA)
            ctx_krope = kpe.reshape(nb, L, QK_ROPE)
        y_mla = _attention(q, k, vt, extra, nb, L, g["tq"], g["hps"])

        hfilt, hnorm = _filters(L, io_dtype, *filt_w)
        if two_stage:
            y_hy = _hyena_long(u, e, x0, hfilt, hnorm, nb)
        else:
            kre, kim = _dft("filt", "hy_fwd", [hfilt], [hnorm], 2)
            sh = (nb, L, HY_CH)
            yre, yim = _dft("fwdk", "hy_fwd", [u.reshape(sh)], [kre, kim], g["dft_nb"])
            y_hy = _dft("inv", "hy_inv", [yre, yim], [e.reshape(sh), x0.reshape(sh)], g["dft_nb"])
            y_hy = y_hy.reshape(nb * L, HY_CH)

        x1 = _post(g["x"], mods0, g["mod_base"], g["tiles_per_mod"], [y_hy, y_mla],
                   [wo0[:HY_CH], wo0[HY_CH:]], row1(l0_ln1_g), row1(l0_ln1_b),
                   l0_mlp_w1.astype(BF16), l0_mlp_w2.astype(BF16), row1(l0_ln2_g), row1(l0_ln2_b))

        if L == FFT_R * FFT_R:
            yf = _fnet_long(x1, mods1, g["mod_base"], nb)
        else:
            zc, zs = _fnet_front(x1, mods1, g["mod_base"], g["tiles_per_mod"])
            sh = (nb, L, d)
            yf = _dft("fnet", "fnet", [zc.reshape(sh), zs.reshape(sh)], [], min(g["dft_nb"], 2))
            yf = yf.reshape(nb * L, d)
        x2 = _post(x1, mods1, g["mod_base"], g["tiles_per_mod"], [yf],
                   [l1_w_out.astype(BF16)], row1(l1_ln1_g), row1(l1_ln1_b),
                   l1_mlp_w1.astype(BF16), l1_mlp_w2.astype(BF16), row1(l1_ln2_g), row1(l1_ln2_b))
        outs.append(x2.reshape(nb, L, d))

    return (outs[0], outs[1], ctx_ckv, ctx_krope)
```

```python
import functools
import math

import numpy as np
import jax
import jax.numpy as jnp
from jax import lax
from jax.experimental import pallas as pl
from jax.experimental.pallas import tpu as pltpu

F32 = jnp.float32
BF16 = jnp.bfloat16
HI = lax.Precision.HIGHEST

D_MODEL = 1024
DEPTH = 2
GRID_W = 64
HY_CH = 512
FILT_BANDS = 16
FILT_ORDER = 64
FAST_DECAY_PCT = 0.3
SLOW_DECAY_PCT = 1.5
DECAY_TARGET = 1e-2
MAX_DECAY = math.log(DECAY_TARGET) / FAST_DECAY_PCT
MIN_DECAY = math.log(DECAY_TARGET) / SLOW_DECAY_PCT
MLA_HEADS = 4
QK_NOPE = 128
QK_ROPE = 64
V_HEAD = 128
Q_LORA = 256
KV_LORA = 128
ROPE_THETA = 10000.0
FN_GROUP_CH = 128
D_FF = 4096
ALPHA = (2 * DEPTH) ** 0.25
LN_EPS = 1e-5
RMS_EPS = 1e-6

LANE = 128
ROW_TILE = 256
POST_TILE = 512
POST_SPLIT = 2
QK_PAD = 256
VT_ROWS = V_HEAD + 16
LOG2E = 1.4426950408889634
VMEM_LIMIT = 56 * 1024 * 1024


def _cparams(sem):
    return pltpu.CompilerParams(dimension_semantics=sem, vmem_limit_bytes=VMEM_LIMIT)


def _ln_plain(x):
    mu = jnp.mean(x, axis=-1, keepdims=True)
    xc = x - mu
    var = jnp.mean(xc * xc, axis=-1, keepdims=True)
    return xc * lax.rsqrt(var + LN_EPS)


def _rms(x, g):
    return x * lax.rsqrt(jnp.mean(x * x, axis=-1, keepdims=True) + RMS_EPS) * g


def _bdot(a, b):
    return jnp.dot(a.astype(BF16), b, preferred_element_type=F32)


def _vt_rows(v):
    ones = jnp.ones((VT_ROWS - V_HEAD, v.shape[0]), BF16)
    return jnp.concatenate([jnp.transpose(v).astype(BF16), ones], axis=0)


MOD_STREAMS = 3


def _mod_kernel(c_ref, *refs):
    w_refs, b_ref, o_ref = refs[:MOD_STREAMS], refs[MOD_STREAMS], refs[MOD_STREAMS + 1]
    c = c_ref[...]
    s = c / (1.0 + jnp.exp(-c))
    s_hi = s.astype(BF16)
    s_lo = (s - s_hi.astype(F32)).astype(BF16)
    s2 = jnp.concatenate([s_hi, s_lo], axis=0)
    nr = s.shape[0]
    tn = w_refs[0].shape[1]
    for k, w_ref in enumerate(w_refs):
        w = w_ref[...]
        w_hi = w.astype(BF16)
        w_lo = (w - w_hi.astype(F32)).astype(BF16)
        r1 = jnp.dot(s2, w_hi, preferred_element_type=F32)
        r2 = jnp.dot(s_hi, w_lo, preferred_element_type=F32)
        o_ref[:, k * tn:(k + 1) * tn] = r1[:nr] + r1[nr:] + r2 + b_ref[:, k * tn:(k + 1) * tn]


def _modulation(cond8, w, b):
    n = w.shape[1]
    tn = 512
    step = MOD_STREAMS * tn
    w_specs = [pl.BlockSpec((D_MODEL, tn), functools.partial(lambda j, k: (0, MOD_STREAMS * j + k), k=k))
               for k in range(MOD_STREAMS)]
    out = pl.pallas_call(
        _mod_kernel,
        out_shape=jax.ShapeDtypeStruct((8, n), F32),
        grid=(n // step,),
        in_specs=[pl.BlockSpec((8, D_MODEL), lambda j: (0, 0))] + w_specs
                 + [pl.BlockSpec((1, step), lambda j: (0, j))],
        out_specs=pl.BlockSpec((8, step), lambda j: (0, j)),
        compiler_params=_cparams(("arbitrary",)),
        name="modulation",
    )(cond8, *([w] * MOD_STREAMS), b.reshape(1, n))
    return out.reshape(8, 1, n)


def _mod_spec(mod_base, tiles_per_mod):
    return pl.BlockSpec((None, 1, 6 * D_MODEL), lambda i: (mod_base + i // tiles_per_mod, 0, 0))


def _const_spec(shape):
    nd = len(shape)
    return pl.BlockSpec(shape, lambda i: (0,) * nd)


HALO = 8


def _front_kernel(tiles_per_seq, x_ref, xp_ref, xn_ref, m_ref, win_ref, qn_ref, qup_ref, kvn_ref, kvup_ref,
                  cos_ref, sin_ref, cw_ref, cb_ref, skip_ref,
                  u_ref, e_ref, x0_ref, q_ref, k_ref, v_ref, kvn_out_ref, kpe_ref):
    i = pl.program_id(0)
    m = m_ref[...]
    tm = x_ref.shape[0]
    nh = 3 * HY_CH
    xe = jnp.concatenate([xp_ref[...], x_ref[...], xn_ref[...]], axis=0)
    he = _ln_plain(xe) * (1.0 + m[:, D_MODEL:2 * D_MODEL]) + m[:, 0:D_MODEL]

    zh = _bdot(he, win_ref[:, :nh])
    pos = i % tiles_per_seq
    rows = lax.broadcasted_iota(jnp.int32, (tm + 2 * HALO, 1), 0)
    inside = jnp.logical_and(jnp.logical_or(rows >= HALO, pos != 0),
                             jnp.logical_or(rows < tm + HALO, pos != tiles_per_seq - 1))
    zh = jnp.where(inside, zh, 0.0)
    cw = cw_ref[...]
    pz = (pltpu.roll(zh, 1, 0) * cw[0:1, :] + zh * cw[1:2, :]
          + pltpu.roll(zh, tm + 2 * HALO - 1, 0) * cw[2:3, :])[HALO:HALO + tm] + cb_ref[...]
    u = pz[:, 2 * HY_CH:] * pz[:, HY_CH:2 * HY_CH]
    u_ref[...] = u.astype(u_ref.dtype)
    e_ref[...] = u * skip_ref[...]
    x0_ref[...] = pz[:, :HY_CH]

    z = _bdot(he[HALO:HALO + tm], win_ref[:, nh:])
    q_c = z[:, 0:256]
    kv_c = z[:, 256:384]
    cos = cos_ref[...]
    sin = sin_ref[...]
    kpe = z[:, 384:512] * cos + z[:, 512:640] * sin
    kpe_ref[...] = kpe[:, :QK_ROPE]
    kpe_b = kpe.astype(BF16)
    q = _bdot(_rms(q_c, qn_ref[...]), qup_ref[...]) * (LOG2E / math.sqrt(QK_NOPE + QK_ROPE))
    kvn = _rms(kv_c, kvn_ref[...])
    kvn_out_ref[...] = kvn
    kv = _bdot(kvn, kvup_ref[...])
    for hd in range(MLA_HEADS):
        a = hd * LANE
        q_pe = (q[:, 512 + a:512 + a + LANE] * cos + q[:, 1024 + a:1024 + a + LANE] * sin).astype(BF16)
        q_ref[hd] = jnp.concatenate([q[:, a:a + LANE].astype(BF16), q_pe], axis=-1)
        k_ref[hd] = jnp.concatenate([kv[:, 2 * a:2 * a + LANE].astype(BF16), kpe_b], axis=-1)
        v_ref[hd] = _vt_rows(kv[:, 2 * a + LANE:2 * a + 2 * LANE])


def _front(x, mods, mod_base, tiles_per_mod, w, cos, sin, rope, tiles_per_seq, u_dtype):
    t = x.shape[0]
    tm = ROW_TILE
    win, qn, qup, kvn, kvup, conv_w, conv_b, skip = w
    if rope:
        tab_spec = pl.BlockSpec((tm, LANE), lambda i: (i % tiles_per_seq, 0))
    else:
        tab_spec = pl.BlockSpec((tm, LANE), lambda i: (0, 0))
    r8 = tm // HALO
    n8 = t // HALO
    hy_out = lambda dt: jax.ShapeDtypeStruct((t, HY_CH), dt)
    hy_spec = pl.BlockSpec((tm, HY_CH), lambda i: (i, 0))
    return pl.pallas_call(
        functools.partial(_front_kernel, tiles_per_seq),
        out_shape=(hy_out(u_dtype), hy_out(F32), hy_out(F32),
                   jax.ShapeDtypeStruct((MLA_HEADS, t, QK_PAD), BF16),
                   jax.ShapeDtypeStruct((MLA_HEADS, t, QK_PAD), BF16),
                   jax.ShapeDtypeStruct((MLA_HEADS, VT_ROWS, t), BF16),
                   jax.ShapeDtypeStruct((t, KV_LORA), F32),
                   jax.ShapeDtypeStruct((t, QK_ROPE), F32)),
        grid=(t // tm,),
        in_specs=[pl.BlockSpec((tm, D_MODEL), lambda i: (i, 0)),
                  pl.BlockSpec((HALO, D_MODEL), lambda i: (jnp.maximum(i * r8 - 1, 0), 0)),
                  pl.BlockSpec((HALO, D_MODEL), lambda i: (jnp.minimum((i + 1) * r8, n8 - 1), 0)),
                  _mod_spec(mod_base, tiles_per_mod),
                  _const_spec(win.shape), _const_spec(qn.shape), _const_spec(qup.shape),
                  _const_spec(kvn.shape), _const_spec(kvup.shape),
                  tab_spec, tab_spec,
                  _const_spec(conv_w.shape), _const_spec(conv_b.shape), _const_spec(skip.shape)],
        out_specs=(hy_spec, hy_spec, hy_spec,
                   pl.BlockSpec((MLA_HEADS, tm, QK_PAD), lambda i: (0, i, 0)),
                   pl.BlockSpec((MLA_HEADS, tm, QK_PAD), lambda i: (0, i, 0)),
                   pl.BlockSpec((MLA_HEADS, VT_ROWS, tm), lambda i: (0, 0, i)),
                   pl.BlockSpec((tm, KV_LORA), lambda i: (i, 0)),
                   pl.BlockSpec((tm, QK_ROPE), lambda i: (i, 0))),
        compiler_params=_cparams(("arbitrary",)),
        name="l0_front",
    )(x, x, x, mods, win, qn, qup, kvn, kvup, cos, sin, conv_w, conv_b, skip)


def _cache_kv_kernel(ckv_ref, kr_ref, kvup_ref, k_ref, v_ref):
    kv = _bdot(ckv_ref[...], kvup_ref[...])
    kr = kr_ref[...].astype(BF16)
    for hd in range(MLA_HEADS):
        a = 2 * hd * LANE
        k_ref[hd] = jnp.concatenate([kv[:, a:a + LANE].astype(BF16), kr], axis=-1)
        v_ref[hd] = _vt_rows(kv[:, a + LANE:a + 2 * LANE])


def _cache_kv(ckv, krope_pad, kvup):
    t = ckv.shape[0]
    return pl.pallas_call(
        _cache_kv_kernel,
        out_shape=(jax.ShapeDtypeStruct((MLA_HEADS, t, QK_PAD), BF16),
                   jax.ShapeDtypeStruct((MLA_HEADS, VT_ROWS, t), BF16)),
        name="l0_cache_kv",
    )(ckv, krope_pad, kvup)


def _col_reduce(x, op):
    rows, n = x.shape
    for g in (32, 8):
        if rows % (8 * g) == 0 and rows > 8 * g:
            x = op(x.reshape(rows // (8 * g), 8 * g, n), axis=0)
            rows = 8 * g
    return op(x, axis=0, keepdims=True)


def _attn_kernel(n_kv, hps, q_ref, *refs):
    k_refs, vt_refs = refs[:n_kv], refs[n_kv:2 * n_kv]
    o_ref, s_even, s_odd = refs[2 * n_kv:]
    i = pl.program_id(0)

    @pl.when(i == 0)
    def _():
        s_odd[...] = jnp.zeros_like(s_odd)

    def step(s_write, s_read):
        nt = (((1,), (1,)), ((), ()))
        for h in range(hps):
            q = q_ref[h]
            r0 = 0
            for k_ref in k_refs:
                lk = k_ref.shape[1]
                s_write[h, r0:r0 + lk, :] = lax.dot_general(k_ref[h], q, nt, preferred_element_type=F32)
                r0 += lk
        for h in range(hps):
            s = s_read[h]
            m = _col_reduce(s, jnp.max)
            pb = jnp.exp2(s - m).astype(BF16)
            acc = None
            r0 = 0
            for vt_ref in vt_refs:
                lk = vt_ref.shape[2]
                pv = jnp.dot(vt_ref[h], pb[r0:r0 + lk, :], preferred_element_type=F32)
                acc = pv if acc is None else acc + pv
                r0 += lk
            o_ref[:, h * V_HEAD:(h + 1) * V_HEAD] = jnp.transpose(
                acc[:V_HEAD] / acc[V_HEAD:V_HEAD + 1]).astype(o_ref.dtype)

    pl.when(i % 2 == 0)(lambda: step(s_even, s_odd))
    pl.when(i % 2 == 1)(lambda: step(s_odd, s_even))


def _attention(q, k, vt, extra, nb, lq, tq, hps):
    nq = lq // tq
    ng = MLA_HEADS // hps
    n_tiles = nb * ng * nq

    def where(t):
        bh = t // nq
        return bh // ng, bh % ng, t % nq

    def score_side(fn):
        return lambda i: fn(*where(jnp.minimum(i, n_tiles - 1)))

    def value_side(fn):
        return lambda i: fn(*where(jnp.maximum(i - 1, 0)))

    ks, vts = [k], [vt]
    if extra is not None:
        ks.append(extra[0])
        vts.append(extra[1])
    in_specs = [pl.BlockSpec((hps, tq, QK_PAD), score_side(lambda b, h, j: (h, b * nq + j, 0)))]
    in_specs += [pl.BlockSpec((hps, a.shape[1] // nb, QK_PAD), score_side(lambda b, h, j: (h, b, 0))) for a in ks]
    in_specs += [pl.BlockSpec((hps, VT_ROWS, a.shape[2] // nb), value_side(lambda b, h, j: (h, 0, b))) for a in vts]
    lk_total = sum(a.shape[1] // nb for a in ks)
    return pl.pallas_call(
        functools.partial(_attn_kernel, len(ks), hps),
        out_shape=jax.ShapeDtypeStruct((nb * lq, MLA_HEADS * V_HEAD), BF16),
        grid=(n_tiles + 1,),
        in_specs=in_specs,
        out_specs=pl.BlockSpec((tq, hps * V_HEAD), value_side(lambda b, h, j: (b * nq + j, h))),
        scratch_shapes=[pltpu.VMEM((hps, lk_total, tq), F32), pltpu.VMEM((hps, lk_total, tq), F32)],
        compiler_params=_cparams(("arbitrary",)),
        name="l0_attention",
    )(q, *ks, *vts)


def _filter_kernel(z_ref, w1_ref, b1_ref, fr_ref, w2_ref, b2_ref, w3_ref, dl_ref, h_ref, norm_ref):
    i = pl.program_id(0)
    z = z_ref[...]
    tl = z.shape[0]
    fr = fr_ref[...]
    z2 = jnp.concatenate([z[:tl // 2], z[tl // 2:]], axis=1)
    h = jnp.sin(fr * (jnp.dot(z2, w1_ref[...], precision=HI, preferred_element_type=F32) + b1_ref[...]))
    h = jnp.sin(fr * (jnp.dot(h, w2_ref[...], precision=HI, preferred_element_type=F32) + b2_ref[...]))
    h = _bdot(h, w3_ref[...])
    h = jnp.concatenate([h[:, :2 * HY_CH], h[:, 2 * HY_CH:]], axis=0)
    decay = jnp.exp(-(z[:, 0:1] * dl_ref[...]))
    hf = h[:, :HY_CH] * decay
    hb = h[:, HY_CH:] * decay
    part = jnp.sum(jnp.abs(hf) + jnp.abs(hb), axis=0, keepdims=True)

    @pl.when(i == 0)
    def _():
        norm_ref[...] = part

    @pl.when(i > 0)
    def _():
        norm_ref[...] += part

    rows = lax.broadcasted_iota(jnp.int32, hb.shape, 0) + i * tl
    h_ref[0] = hf.astype(h_ref.dtype)
    h_ref[1] = jnp.where(rows == 0, 0.0, hb).astype(h_ref.dtype)


def _filter_embedding(L):
    t = np.linspace(0.0, 1.0, L)[:, None]
    w_ang = 2.0 * np.pi * np.arange(L) / L
    bands = np.linspace(1e-4, FILT_BANDS - 1, FILT_BANDS)
    ang = w_ang[:, None] * bands[None, :]
    z = np.zeros((L, LANE), np.float64)
    z[:, 0:1] = t
    z[:, 1:1 + FILT_BANDS] = np.cos(ang)
    z[:, 1 + FILT_BANDS:1 + 2 * FILT_BANDS] = -np.sin(ang)
    return jnp.asarray(z, F32)


def _filters(L, h_dtype, w1p, b1, fr, w2, b2, w3):
    tl = min(L, 512)
    z = _filter_embedding(L)
    deltas = jnp.asarray(np.abs(np.linspace(MIN_DECAY, MAX_DECAY, HY_CH))[None, :], F32)
    return pl.pallas_call(
        _filter_kernel,
        out_shape=(jax.ShapeDtypeStruct((2, L, HY_CH), h_dtype), jax.ShapeDtypeStruct((1, HY_CH), F32)),
        grid=(L // tl,),
        in_specs=[pl.BlockSpec((tl, LANE), lambda i: (i, 0)),
                  _const_spec(w1p.shape), _const_spec(b1.shape), _const_spec(fr.shape),
                  _const_spec(w2.shape), _const_spec(b2.shape), _const_spec(w3.shape),
                  _const_spec(deltas.shape)],
        out_specs=(pl.BlockSpec((2, tl, HY_CH), lambda i: (0, i, 0)),
                   pl.BlockSpec((1, HY_CH), lambda i: (0, 0))),
        compiler_params=_cparams(("arbitrary",)),
        name="l0_hyena_filters",
    )(z, w1p, b1, fr, w2, b2, w3, deltas)


def _dft_tables(kind, L, ti):
    ni = L // ti
    i = np.arange(ti, dtype=np.int64)[:, None]
    big = (np.arange(ni, dtype=np.int64) * ti)[:, None]
    c = np.arange(L, dtype=np.int64)[None, :]
    if kind == "hy_fwd":
        period = 4 * L
        base_idx = (2 * i + 1) * c
        r_idx = 2 * big * c
        scale = 1.0
    elif kind == "hy_inv":
        period = 4 * L
        base_idx = (2 * c + 1) * i
        r_idx = (2 * c + 1) * big
        scale = 1.0 / L
    else:
        period = L
        base_idx = i * c
        r_idx = big * c
        scale = 1.0 / math.sqrt(L * FN_GROUP_CH)
    ab = 2.0 * np.pi * (base_idx % period) / period
    ar = 2.0 * np.pi * (r_idx % period) / period
    return (jnp.asarray(np.cos(ab), F32), jnp.asarray(np.sin(ab), F32),
            jnp.asarray(scale * np.cos(ar), F32).reshape(ni, 1, L),
            jnp.asarray(scale * np.sin(ar), F32).reshape(ni, 1, L))


def _dft_kernel(mode, nb, n_x, *refs):
    bc_ref, bs_ref, rc_ref, rs_ref = refs[:4]
    x_refs = refs[4:4 + n_x]
    rest = refs[4 + n_x:]
    p_ref, q_ref = rest[-2], rest[-1]
    j = pl.program_id(2)
    nj = pl.num_programs(2)
    tj = x_refs[0].shape[1]
    if bc_ref.shape[1] == tj:
        bc, bs, rc, rs = bc_ref[...], bs_ref[...], rc_ref[...], rs_ref[...]
    else:
        off = pl.multiple_of(j * tj, tj)
        bc, bs = bc_ref[:, pl.ds(off, tj)], bs_ref[:, pl.ds(off, tj)]
        rc, rs = rc_ref[:, pl.ds(off, tj)], rs_ref[:, pl.ds(off, tj)]
    tc = (bc * rc - bs * rs).astype(BF16)
    ts = (bs * rc + bc * rs).astype(BF16)
    x1_ref = x_refs[0]
    x2_ref = x_refs[-1]

    pq = [(jnp.dot(tc, x1_ref[b], preferred_element_type=F32),
           jnp.dot(ts, x2_ref[b], preferred_element_type=F32)) for b in range(nb)]

    @pl.when(j == 0)
    def _():
        for b in range(nb):
            p_ref[b] = pq[b][0]
            q_ref[b] = pq[b][1]

    @pl.when(j > 0)
    def _():
        for b in range(nb):
            p_ref[b] += pq[b][0]
            q_ref[b] += pq[b][1]

    @pl.when(j == nj - 1)
    def _():
        if mode == "filt":
            nrm = rest[0][...]
            kre_ref, kim_ref = rest[1], rest[2]
            kre_ref[...] = (p_ref[0] + p_ref[1]) / nrm
            kim_ref[...] = (q_ref[1] - q_ref[0]) / nrm
        elif mode == "fwdk":
            kre, kim = rest[0][...], rest[1][...]
            yre_ref, yim_ref = rest[2], rest[3]
            for b in range(nb):
                pp, qq = p_ref[b], q_ref[b]
                yre_ref[b] = (pp * kre + qq * kim).astype(BF16)
                yim_ref[b] = (pp * kim - qq * kre).astype(BF16)
        elif mode == "inv":
            e_ref, x0_ref, o_ref = rest[0], rest[1], rest[2]
            for b in range(nb):
                o_ref[b] = ((p_ref[b] - q_ref[b] + e_ref[b]) * x0_ref[b]).astype(BF16)
        else:
            o_ref = rest[0]
            for b in range(nb):
                o_ref[b] = (p_ref[b] - q_ref[b]).astype(BF16)


def _dft(mode, kind, xs, extras, nb):
    B, L, C = xs[0].shape
    ti = min(L, 256)
    tj = min(L, 512)
    bc, bs, rc, rs = _dft_tables(kind, L, ti)
    grid = (B // nb, L // ti, L // tj)
    x_spec = pl.BlockSpec((nb, tj, C), lambda g, i, j: (g, j, 0))
    row_spec = lambda c, dt=None: pl.BlockSpec((nb, ti, c), lambda g, i, j: (g, i, 0))
    in_specs = [pl.BlockSpec((ti, L), lambda g, i, j: (0, 0)),
                pl.BlockSpec((ti, L), lambda g, i, j: (0, 0)),
                pl.BlockSpec((None, 1, L), lambda g, i, j: (i, 0, 0)),
                pl.BlockSpec((None, 1, L), lambda g, i, j: (i, 0, 0))] + [x_spec] * len(xs)
    if mode == "filt":
        in_specs += [pl.BlockSpec((1, HY_CH), lambda g, i, j: (0, 0))]
        out_shape = (jax.ShapeDtypeStruct((L, HY_CH), F32),) * 2
        out_specs = (pl.BlockSpec((ti, HY_CH), lambda g, i, j: (i, 0)),) * 2
    elif mode == "fwdk":
        in_specs += [pl.BlockSpec((ti, HY_CH), lambda g, i, j: (i, 0))] * 2
        out_shape = (jax.ShapeDtypeStruct((B, L, C), BF16),) * 2
        out_specs = (row_spec(C),) * 2
    elif mode == "inv":
        in_specs += [row_spec(C)] * 2
        out_shape = jax.ShapeDtypeStruct((B, L, C), BF16)
        out_specs = row_spec(C)
    else:
        out_shape = jax.ShapeDtypeStruct((B, L, C), BF16)
        out_specs = row_spec(C)
    return pl.pallas_call(
        functools.partial(_dft_kernel, mode, nb, len(xs)),
        out_shape=out_shape,
        grid=grid,
        in_specs=in_specs,
        out_specs=out_specs,
        scratch_shapes=[pltpu.VMEM((nb, ti, C), F32), pltpu.VMEM((nb, ti, C), F32)],
        compiler_params=_cparams(("arbitrary", "arbitrary", "arbitrary")),
        name="dft_" + mode,
    )(bc, bs, rc, rs, *xs, *extras)


FFT_R = 64
HYENA_KF = 32
FNET_KF = 16


def _pack_pairs(x):
    return pltpu.bitcast(x.astype(BF16), jnp.uint32)


def _unpack_pairs(w):
    return pltpu.bitcast(w, BF16)


def _to_blocks(w):
    return jnp.swapaxes(w.reshape(FFT_R, w.shape[-2], w.shape[-1]), 0, 1)


def _from_blocks(ws):
    kf, c = len(ws), ws[0].shape[-1]
    return jnp.swapaxes(jnp.stack(ws, axis=0), 0, 1).reshape(FFT_R // kf, kf, kf, c)


def _lead_in_kernel(g_ref, x_ref, o_ref):
    g = g_ref[...]
    xt = jnp.swapaxes(x_ref[...], 0, 1)
    for j in range(x_ref.shape[1]):
        o_ref[j] = _pack_pairs(jnp.dot(g, xt[j].astype(BF16), preferred_element_type=F32))


def _lead_in(g, x, kf, name):
    nbx, _, _, c = x.shape
    m2 = g.shape[0] // 2
    return pl.pallas_call(
        _lead_in_kernel,
        out_shape=jax.ShapeDtypeStruct((nbx, FFT_R // kf, kf, m2, c), jnp.uint32),
        grid=(nbx, FFT_R // kf),
        in_specs=[pl.BlockSpec(g.shape, lambda b, k: (0, 0)),
                  pl.BlockSpec((None, FFT_R, kf, c), lambda b, k: (b, 0, k, 0))],
        out_specs=pl.BlockSpec((None, None, kf, m2, c), lambda b, k: (b, k, 0, 0, 0)),
        compiler_params=_cparams(("arbitrary", "arbitrary")),
        name=name,
    )(g, x)


def _lead_out_kernel(n_extra, g_ref, w_ref, *rest):
    g = g_ref[...]
    o_ref = rest[-1]
    ys = [jnp.dot(g, _unpack_pairs(w_ref[j]), preferred_element_type=F32) for j in range(w_ref.shape[0])]
    y = jnp.swapaxes(jnp.stack(ys, axis=0), 0, 1)
    if n_extra:
        y = (y + rest[0][...]) * rest[1][...]
    o_ref[...] = y


def _lead_out(g, w, extras, name):
    nb, nk, kf, k2, c = w.shape
    blk = pl.BlockSpec((None, FFT_R, kf, c), lambda b, k: (b, 0, k, 0))
    return pl.pallas_call(
        functools.partial(_lead_out_kernel, len(extras)),
        out_shape=jax.ShapeDtypeStruct((nb, FFT_R, FFT_R, c), F32),
        grid=(nb, nk),
        in_specs=[pl.BlockSpec(g.shape, lambda b, k: (0, 0)),
                  pl.BlockSpec((None, None, kf, k2, c), lambda b, k: (b, k, 0, 0, 0))] + [blk] * len(extras),
        out_specs=blk,
        compiler_params=_cparams(("arbitrary", "arbitrary")),
        name=name,
    )(g, w, *extras)


def _interleave(a, b, axis):
    st = np.stack([a, b], axis=axis + 1)
    shape = list(a.shape)
    shape[axis] *= 2
    return st.reshape(shape)


def _hy2_tables():
    L = FFT_R * FFT_R
    n2 = 2 * L
    f1 = np.arange(2 * FFT_R, dtype=np.int64)
    s1 = np.arange(FFT_R, dtype=np.int64)
    th = np.pi * (((2 * f1[:, None] + 1) * s1[None, :]) % (4 * FFT_R)) / (2 * FFT_R)
    ga = _interleave(np.cos(th), -np.sin(th), 0)
    ma = _interleave(np.cos(th).T, -np.sin(th).T, 1) / L
    f2 = np.arange(FFT_R // 2, dtype=np.int64)
    s2 = np.arange(FFT_R, dtype=np.int64)
    idx = ((n2 // FFT_R) * 2 * f2[None, :, None] * s2[None, None, :]
           + (2 * f1[:, None, None] + 1) * s2[None, None, :]) % (2 * n2)
    al = np.pi * idx / n2
    c, s = np.cos(al), np.sin(al)
    nmat = np.concatenate([_interleave(c, s, 2), _interleave(-s, c, 2)], axis=1)
    ct, st = np.transpose(c, (0, 2, 1)), np.transpose(s, (0, 2, 1))
    mmat = _interleave(np.concatenate([ct, -st], axis=2), np.concatenate([st, ct], axis=2), 1)
    bf = lambda a: jnp.asarray(a, F32).astype(BF16)
    return bf(ga), bf(nmat), bf(mmat), bf(ma)


def _hy_mid_kernel(a_ref, n_ref, m_ref, k_ref, e_ref):
    half = FFT_R // 2
    kf = a_ref.shape[-2]
    a = _to_blocks(a_ref[...])
    ts = [jnp.dot(n_ref[j], _unpack_pairs(a[j]), preferred_element_type=F32)
          for j in range(kf)]
    ys = []
    for j in range(kf):
        tr, ti = ts[j][:half], ts[j][half:]
        kr, ki = k_ref[j, 0], k_ref[j, 1]
        ys.append(jnp.concatenate([tr * kr - ti * ki, tr * ki + ti * kr], axis=0).astype(BF16))
    e_ref[...] = _from_blocks([_pack_pairs(jnp.dot(m_ref[j], ys[j], preferred_element_type=F32))
                               for j in range(kf)])


def _hy_mid(a, nmat, mmat, khat):
    nb, nk, kf, _, c = a.shape
    nf1 = 2 * FFT_R
    blk = pl.BlockSpec((None, nk, kf, kf, c), lambda i, b: (b, 0, 0, i, 0))
    return pl.pallas_call(
        _hy_mid_kernel,
        out_shape=jax.ShapeDtypeStruct(a.shape, jnp.uint32),
        grid=(nf1 // kf, nb),
        in_specs=[blk,
                  pl.BlockSpec((kf, FFT_R, 2 * FFT_R), lambda i, b: (i, 0, 0)),
                  pl.BlockSpec((kf, 2 * FFT_R, FFT_R), lambda i, b: (i, 0, 0)),
                  pl.BlockSpec((kf, 2, FFT_R // 2, c), lambda i, b: (i, 0, 0, 0))],
        out_specs=blk,
        compiler_params=_cparams(("arbitrary", "arbitrary")),
        name="l0_hyena_mid",
    )(a, nmat, mmat, khat)


def _hy_kfilt_kernel(a_ref, n_ref, nrm_ref, k_ref):
    half = FFT_R // 2
    nrm = nrm_ref[...]
    af, ab = _to_blocks(a_ref[0]), _to_blocks(a_ref[1])
    for j in range(a_ref.shape[-2]):
        tf = jnp.dot(n_ref[j], _unpack_pairs(af[j]), preferred_element_type=F32)
        tb = jnp.dot(n_ref[j], _unpack_pairs(ab[j]), preferred_element_type=F32)
        k_ref[j, 0] = (tf[:half] + tb[:half]) / nrm
        k_ref[j, 1] = (tf[half:] - tb[half:]) / nrm


def _hy_kfilt(a, nmat, nrm):
    _, nk, kf, _, c = a.shape
    nf1 = 2 * FFT_R
    return pl.pallas_call(
        _hy_kfilt_kernel,
        out_shape=jax.ShapeDtypeStruct((nf1, 2, FFT_R // 2, c), F32),
        grid=(nf1 // kf,),
        in_specs=[pl.BlockSpec((2, nk, kf, kf, c), lambda i: (0, 0, 0, i, 0)),
                  pl.BlockSpec((kf, FFT_R, 2 * FFT_R), lambda i: (i, 0, 0)),
                  _const_spec(nrm.shape)],
        out_specs=pl.BlockSpec((kf, 2, FFT_R // 2, c), lambda i: (i, 0, 0, 0)),
        compiler_params=_cparams(("arbitrary",)),
        name="l0_hyena_kfilt",
    )(a, nmat, nrm)


def _hyena_long(u, e, x0, hfilt, nrm, nb):
    L = FFT_R * FFT_R
    c = u.shape[-1]
    v4 = lambda a, n: a.reshape(n, FFT_R, FFT_R, c)
    ga, nmat, mmat, ma = _hy2_tables()
    khat = _hy_kfilt(_lead_in(ga, v4(hfilt, 2), HYENA_KF, "l0_hyena_fwd_a"), nmat, nrm)
    ee = _hy_mid(_lead_in(ga, v4(u, nb), HYENA_KF, "l0_hyena_fwd_a"), nmat, mmat, khat)
    y = _lead_out(ma, ee, [v4(e, nb), v4(x0, nb)], "l0_hyena_inv_a")
    return y.reshape(nb * L, c)


def _fn2_tables():
    L = FFT_R * FFT_R
    r = np.arange(FFT_R, dtype=np.int64)
    idx = (FFT_R * r[None, :, None] * r[None, None, :] + r[None, :, None] * r[:, None, None]) % L
    gm = 2.0 * np.pi * idx / L
    c, s = np.cos(gm), np.sin(gm)
    g1 = _interleave(np.concatenate([c, -s], axis=2), np.concatenate([-s, -c], axis=2), 1)
    dl = 2.0 * np.pi * ((r[:, None] * r[None, :]) % FFT_R) / FFT_R
    g2 = _interleave(np.cos(dl), np.sin(dl), 1) / math.sqrt(L * FN_GROUP_CH)
    bf = lambda a: jnp.asarray(a, F32).astype(BF16)
    return bf(g1), bf(g2)


def _fnet_s1_kernel(x_ref, m_ref, cs_ref, g1_ref, o_ref, zc_ref, zs_ref):
    xs = jnp.swapaxes(x_ref[...], 0, 1).reshape(FNET_KF * FFT_R, D_MODEL)
    m = m_ref[...]
    h = (_ln_plain(xs) * (1.0 + m[:, D_MODEL:2 * D_MODEL]) + m[:, 0:D_MODEL]).astype(BF16)
    cs = cs_ref[...]
    for g in range(D_MODEL // FN_GROUP_CH):
        a = g * FN_GROUP_CH
        z = jnp.dot(h[:, a:a + FN_GROUP_CH], cs, preferred_element_type=F32)
        zc_ref[:, a:a + FN_GROUP_CH] = z[:, :FN_GROUP_CH].astype(BF16)
        zs_ref[:, a:a + FN_GROUP_CH] = z[:, FN_GROUP_CH:].astype(BF16)
    ws = []
    for j in range(FNET_KF):
        r0 = j * FFT_R
        s = jnp.concatenate([zc_ref[r0:r0 + FFT_R, :], zs_ref[r0:r0 + FFT_R, :]], axis=0)
        ws.append(_pack_pairs(jnp.dot(g1_ref[j], s, preferred_element_type=F32)))
    o_ref[...] = _from_blocks(ws)


def _fnet_long(x, mods, mod_base, nb):
    L = FFT_R * FFT_R
    d = D_MODEL
    g1, g2 = _fn2_tables()
    cs = _group_dft_table()
    kf = FNET_KF
    bb = pl.pallas_call(
        _fnet_s1_kernel,
        out_shape=jax.ShapeDtypeStruct((nb, FFT_R // kf, kf, FFT_R, d), jnp.uint32),
        grid=(nb, FFT_R // kf),
        in_specs=[pl.BlockSpec((None, FFT_R, kf, d), lambda b, k: (b, 0, k, 0)),
                  pl.BlockSpec((None, 1, 6 * d), lambda b, k: (mod_base + b, 0, 0)),
                  pl.BlockSpec(cs.shape, lambda b, k: (0, 0)),
                  pl.BlockSpec((kf, 2 * FFT_R, 2 * FFT_R), lambda b, k: (k, 0, 0))],
        out_specs=pl.BlockSpec((None, FFT_R // kf, kf, kf, d), lambda b, k: (b, 0, 0, k, 0)),
        scratch_shapes=[pltpu.VMEM((kf * FFT_R, d), BF16), pltpu.VMEM((kf * FFT_R, d), BF16)],
        compiler_params=_cparams(("arbitrary", "arbitrary")),
        name="l1_fnet_stage1",
    )(x.reshape(nb, FFT_R, FFT_R, d), mods, cs, g1)
    y = _lead_out(g2, bb, [], "l1_fnet_stage2")
    return y.reshape(nb * L, d)


def _group_dft_table():
    g = FN_GROUP_CH
    jk = (np.arange(g, dtype=np.int64)[:, None] * np.arange(g, dtype=np.int64)[None, :]) % g
    ang = 2.0 * np.pi * jk / g
    return jnp.asarray(np.concatenate([np.cos(ang), np.sin(ang)], axis=1), F32).astype(BF16)


def _fnet_front_kernel(x_ref, m_ref, cs_ref, zc_ref, zs_ref):
    m = m_ref[...]
    h = (_ln_plain(x_ref[...]) * (1.0 + m[:, D_MODEL:2 * D_MODEL]) + m[:, 0:D_MODEL]).astype(BF16)
    cs = cs_ref[...]
    for g in range(D_MODEL // FN_GROUP_CH):
        a = g * FN_GROUP_CH
        z = jnp.dot(h[:, a:a + FN_GROUP_CH], cs, preferred_element_type=F32)
        zc_ref[:, a:a + FN_GROUP_CH] = z[:, :FN_GROUP_CH].astype(BF16)
        zs_ref[:, a:a + FN_GROUP_CH] = z[:, FN_GROUP_CH:].astype(BF16)


def _fnet_front(x, mods, mod_base, tiles_per_mod):
    t = x.shape[0]
    tm = ROW_TILE
    cs = _group_dft_table()
    return pl.pallas_call(
        _fnet_front_kernel,
        out_shape=(jax.ShapeDtypeStruct((t, D_MODEL), BF16),) * 2,
        grid=(t // tm,),
        in_specs=[pl.BlockSpec((tm, D_MODEL), lambda i: (i, 0)),
                  _mod_spec(mod_base, tiles_per_mod),
                  _const_spec(cs.shape)],
        out_specs=(pl.BlockSpec((tm, D_MODEL), lambda i: (i, 0)),) * 2,
        compiler_params=_cparams(("arbitrary",)),
        name="l1_fnet_front",
    )(x, mods, cs)


def _post_kernel(n_a, *refs):
    x_ref, m_ref = refs[0], refs[1]
    a_refs = refs[2:2 + n_a]
    wo_refs = refs[2 + n_a:2 + 2 * n_a]
    g1_ref, b1_ref, w1_ref, w2_ref, g2_ref, b2_ref, o_ref = refs[2 + 2 * n_a:]
    m = m_ref[...]
    d = D_MODEL
    tm = x_ref.shape[0]
    halves = [(r, r + tm // POST_SPLIT) for r in range(0, tm, tm // POST_SPLIT)]
    outs = []
    for r0, r1 in halves:
        out = _bdot(a_refs[0][r0:r1, :], wo_refs[0][...])
        for a_ref, wo_ref in zip(a_refs[1:], wo_refs[1:]):
            out += _bdot(a_ref[r0:r1, :], wo_ref[...])
        outs.append(out)
    x1s, hs = [], []
    for (r0, r1), out in zip(halves, outs):
        x1 = _ln_plain(ALPHA * x_ref[r0:r1, :] + m[:, 2 * d:3 * d] * out) * g1_ref[...] + b1_ref[...]
        x1s.append(x1)
        hs.append((_ln_plain(x1) * (1.0 + m[:, 4 * d:5 * d]) + m[:, 3 * d:4 * d]).astype(BF16))
    accs = []
    n_c = D_FF // d

    def up(h, c):
        hc = jnp.maximum(jnp.dot(h, w1_ref[:, c * d:(c + 1) * d], preferred_element_type=F32), 0.0)
        return (hc * hc).astype(BF16)

    for h in hs:
        acc = None
        nxt = up(h, 0)
        for c in range(n_c):
            cur = nxt
            if c + 1 < n_c:
                nxt = up(h, c + 1)
            part = jnp.dot(cur, w2_ref[c * d:(c + 1) * d, :], preferred_element_type=F32)
            acc = part if acc is None else acc + part
        accs.append(acc)
    for (r0, r1), x1, acc in zip(halves, x1s, accs):
        o_ref[r0:r1, :] = _ln_plain(ALPHA * x1 + m[:, 5 * d:6 * d] * acc) * g2_ref[...] + b2_ref[...]


def _post(x, mods, mod_base, tiles_per_mod, a_list, wo_list, g1, b1, w1, w2, g2, b2):
    t = x.shape[0]
    tm = POST_TILE
    row = lambda c: pl.BlockSpec((tm, c), lambda i: (i, 0))
    once = lambda v: pl.BlockSpec(v.shape, lambda i: (0,) * v.ndim, pipeline_mode=pl.Buffered(1))
    in_specs = ([row(D_MODEL), _mod_spec(mod_base, tiles_per_mod * ROW_TILE // tm)]
                + [row(a.shape[1]) for a in a_list]
                + [once(w) for w in wo_list]
                + [once(v) for v in (g1, b1, w1, w2, g2, b2)])
    return pl.pallas_call(
        functools.partial(_post_kernel, len(a_list)),
        out_shape=jax.ShapeDtypeStruct((t, D_MODEL), F32),
        grid=(t // tm,),
        in_specs=in_specs,
        out_specs=row(D_MODEL),
        compiler_params=_cparams(("arbitrary",)),
        name="post_mlp",
    )(x, mods, *a_list, *wo_list, g1, b1, w1, w2, g2, b2)


def _rot_cols(w):
    parts = []
    for seg in range(2):
        o = seg * 32
        parts += [-w[:, o + 16:o + 32], w[:, o:o + 16]]
    return jnp.concatenate(parts, axis=1)


def _pad_cols(w, n):
    return jnp.pad(w, ((0, 0), (0, n - w.shape[1])))


def _block_diag2(w):
    z = jnp.zeros_like(w)
    return jnp.concatenate([jnp.concatenate([w, z], axis=1), jnp.concatenate([z, w], axis=1)], axis=0)


def _rope_tables(L):
    rows = L // GRID_W
    row = np.repeat(np.arange(rows, dtype=np.float64), GRID_W)
    col = np.tile(np.arange(GRID_W, dtype=np.float64), rows)
    half = QK_ROPE // 2
    inv = 1.0 / (ROPE_THETA ** (np.arange(0, half, 2, dtype=np.float64) / half))
    ar = row[:, None] * inv[None, :]
    ac = col[:, None] * inv[None, :]
    ang = np.concatenate([ar, ar, ac, ac], axis=1)
    cos = np.concatenate([np.cos(ang), np.ones_like(ang)], axis=1)
    sin = np.concatenate([np.sin(ang), np.zeros_like(ang)], axis=1)
    return jnp.asarray(cos, F32), jnp.asarray(sin, F32)


def kernel(x_prompt, x_sample, cache_l0_ckv, cache_l0_krope, c, c_ctx, l0_ada_w, l0_ada_b, l0_w_in, l0_conv_w, l0_conv_b, l0_hf_w1, l0_hf_b1, l0_hf_freq, l0_hf_w2, l0_hf_b2, l0_hf_w3, l0_hf_skip, l0_q_norm, l0_q_up, l0_kv_norm, l0_kv_up, l0_w_out, l0_ln1_g, l0_ln1_b, l0_mlp_w1, l0_mlp_w2, l0_ln2_g, l0_ln2_b, l1_ada_w, l1_ada_b, l1_w_out, l1_ln1_g, l1_ln1_b, l1_mlp_w1, l1_mlp_w2, l1_ln2_g, l1_ln2_b):
    nbc, lc, d = x_prompt.shape
    nbs, ls, _ = x_sample.shape
    past = cache_l0_ckv.shape[1]
    tm = ROW_TILE
    row1 = lambda v: v.reshape(1, -1)

    cond8 = jnp.concatenate([c_ctx[None, :], c, jnp.zeros((8 - 1 - nbs, d), F32)], axis=0)
    mods0 = _modulation(cond8, l0_ada_w, l0_ada_b)
    mods1 = _modulation(cond8, l1_ada_w, l1_ada_b)

    kpe_w = l0_w_in[:, 1920:1984]
    win = jnp.concatenate([l0_w_in[:, :1920], _pad_cols(kpe_w, LANE), _pad_cols(_rot_cols(kpe_w), LANE)],
                          axis=1).astype(BF16)
    dh = QK_NOPE + QK_ROPE
    q_nope = [l0_q_up[:, h * dh:h * dh + QK_NOPE] for h in range(MLA_HEADS)]
    q_pe = [l0_q_up[:, h * dh + QK_NOPE:(h + 1) * dh] for h in range(MLA_HEADS)]
    qup = jnp.concatenate(q_nope + [_pad_cols(w, LANE) for w in q_pe]
                          + [_pad_cols(_rot_cols(w), LANE) for w in q_pe], axis=1).astype(BF16)
    kvup = l0_kv_up.astype(BF16)
    front_w = (win, row1(l0_q_norm), qup, row1(l0_kv_norm), kvup, l0_conv_w, row1(l0_conv_b), row1(l0_hf_skip))
    w1p = jnp.pad(l0_hf_w1, ((0, LANE - l0_hf_w1.shape[0]), (0, 0)))
    two = lambda v: jnp.tile(row1(v), (1, 2))
    filt_w = (_block_diag2(w1p), two(l0_hf_b1), two(l0_hf_freq), _block_diag2(l0_hf_w2), two(l0_hf_b2),
              _block_diag2(l0_hf_w3).astype(BF16))
    wo0 = l0_w_out.astype(BF16)

    xc = x_prompt.reshape(nbc * lc, d)
    xs = x_sample.reshape(nbs * ls, d)
    groups = (
        dict(x=xc, nb=nbc, L=lc, mod_base=0, tiles_per_mod=nbc * lc // tm, dft_nb=4, tq=lc, hps=MLA_HEADS),
        dict(x=xs, nb=nbs, L=ls, mod_base=1, tiles_per_mod=ls // tm, dft_nb=nbs, tq=512, hps=1),
    )
    ones_tab = (jnp.concatenate([jnp.ones((tm, LANE), F32)], axis=0), jnp.zeros((tm, LANE), F32))

    outs = []
    ctx_ckv = ctx_krope = None
    for gi, g in enumerate(groups):
        nb, L = g["nb"], g["L"]
        tiles_per_seq = L // tm
        latent = gi == 1
        cos, sin = _rope_tables(L) if latent else ones_tab
        two_stage = L == FFT_R * FFT_R
        io_dtype = F32 if two_stage else BF16
        u, e, x0, q, k, vt, kvn, kpe = _front(g["x"], mods0, g["mod_base"], g["tiles_per_mod"], front_w,
                                              cos, sin, latent, tiles_per_seq, io_dtype)
        if latent:
            extra = _cache_kv(cache_l0_ckv.reshape(nbs * past, KV_LORA),
                              _pad_cols(cache_l0_krope.reshape(nbs * past, QK_ROPE), LANE), kvup)
        else:
            extra = None
            ctx_ckv = kvn.reshape(nb, L, KV_LORA)
            ctx_krope = kpe.reshape(nb, L, QK_ROPE)
        y_mla = _attention(q, k, vt, extra, nb, L, g["tq"], g["hps"])

        hfilt, hnorm = _filters(L, io_dtype, *filt_w)
        if two_stage:
            y_hy = _hyena_long(u, e, x0, hfilt, hnorm, nb)
        else:
            kre, kim = _dft("filt", "hy_fwd", [hfilt], [hnorm], 2)
            sh = (nb, L, HY_CH)
            yre, yim = _dft("fwdk", "hy_fwd", [u.reshape(sh)], [kre, kim], g["dft_nb"])
            y_hy = _dft("inv", "hy_inv", [yre, yim], [e.reshape(sh), x0.reshape(sh)], g["dft_nb"])
            y_hy = y_hy.reshape(nb * L, HY_CH)

        x1 = _post(g["x"], mods0, g["mod_base"], g["tiles_per_mod"], [y_hy, y_mla],
                   [wo0[:HY_CH], wo0[HY_CH:]], row1(l0_ln1_g), row1(l0_ln1_b),
                   l0_mlp_w1.astype(BF16), l0_mlp_w2.astype(BF16), row1(l0_ln2_g), row1(l0_ln2_b))

        if L == FFT_R * FFT_R:
            yf = _fnet_long(x1, mods1, g["mod_base"], nb)
        else:
            zc, zs = _fnet_front(x1, mods1, g["mod_base"], g["tiles_per_mod"])
            sh = (nb, L, d)
            yf = _dft("fnet", "fnet", [zc.reshape(sh), zs.reshape(sh)], [], min(g["dft_nb"], 2))
            yf = yf.reshape(nb * L, d)
        x2 = _post(x1, mods1, g["mod_base"], g["tiles_per_mod"], [yf],
                   [l1_w_out.astype(BF16)], row1(l1_ln1_g), row1(l1_ln1_b),
                   l1_mlp_w1.astype(BF16), l1_mlp_w2.astype(BF16), row1(l1_ln2_g), row1(l1_ln2_b))
        outs.append(x2.reshape(nb, L, d))

    return (outs[0], outs[1], ctx_ckv, ctx_krope)
```

```python
import functools
import math

import numpy as np
import jax
import jax.numpy as jnp
from jax import lax
from jax.experimental import pallas as pl
from jax.experimental.pallas import tpu as pltpu

F32 = jnp.float32
BF16 = jnp.bfloat16
HI = lax.Precision.HIGHEST

D_MODEL = 1024
DEPTH = 2
GRID_W = 64
HY_CH = 512
FILT_BANDS = 16
FILT_ORDER = 64
FAST_DECAY_PCT = 0.3
SLOW_DECAY_PCT = 1.5
DECAY_TARGET = 1e-2
MAX_DECAY = math.log(DECAY_TARGET) / FAST_DECAY_PCT
MIN_DECAY = math.log(DECAY_TARGET) / SLOW_DECAY_PCT
MLA_HEADS = 4
QK_NOPE = 128
QK_ROPE = 64
V_HEAD = 128
Q_LORA = 256
KV_LORA = 128
ROPE_THETA = 10000.0
FN_GROUP_CH = 128
D_FF = 4096
ALPHA = (2 * DEPTH) ** 0.25
LN_EPS = 1e-5
RMS_EPS = 1e-6

LANE = 128
ROW_TILE = 256
FRONT_TILE = 512
POST_TILE = 512
POST_SPLIT = 2
QK_PAD = 256
VT_ROWS = V_HEAD + 16
LOG2E = 1.4426950408889634
VMEM_LIMIT = 56 * 1024 * 1024


def _cparams(sem):
    return pltpu.CompilerParams(dimension_semantics=sem, vmem_limit_bytes=VMEM_LIMIT)


def _ln_plain(x):
    mu = jnp.mean(x, axis=-1, keepdims=True)
    xc = x - mu
    var = jnp.mean(xc * xc, axis=-1, keepdims=True)
    return xc * lax.rsqrt(var + LN_EPS)


def _rms(x, g):
    return x * lax.rsqrt(jnp.mean(x * x, axis=-1, keepdims=True) + RMS_EPS) * g


def _bdot(a, b):
    return jnp.dot(a.astype(BF16), b, preferred_element_type=F32)


def _vt_rows(v):
    ones = jnp.ones((VT_ROWS - V_HEAD, v.shape[0]), BF16)
    return jnp.concatenate([jnp.transpose(v).astype(BF16), ones], axis=0)


MOD_STREAMS = 3


def _mod_kernel(c_ref, *refs):
    w_refs, b_ref, o_ref = refs[:MOD_STREAMS], refs[MOD_STREAMS], refs[MOD_STREAMS + 1]
    c = c_ref[...]
    s = c / (1.0 + jnp.exp(-c))
    s_hi = s.astype(BF16)
    s_lo = (s - s_hi.astype(F32)).astype(BF16)
    s2 = jnp.concatenate([s_hi, s_lo], axis=0)
    nr = s.shape[0]
    tn = w_refs[0].shape[1]
    for k, w_ref in enumerate(w_refs):
        w = w_ref[...]
        w_hi = w.astype(BF16)
        w_lo = (w - w_hi.astype(F32)).astype(BF16)
        r1 = jnp.dot(s2, w_hi, preferred_element_type=F32)
        r2 = jnp.dot(s_hi, w_lo, preferred_element_type=F32)
        o_ref[:, k * tn:(k + 1) * tn] = r1[:nr] + r1[nr:] + r2 + b_ref[:, k * tn:(k + 1) * tn]


def _modulation(cond8, w, b):
    n = w.shape[1]
    tn = 512
    step = MOD_STREAMS * tn
    w_specs = [pl.BlockSpec((D_MODEL, tn), functools.partial(lambda j, k: (0, MOD_STREAMS * j + k), k=k))
               for k in range(MOD_STREAMS)]
    out = pl.pallas_call(
        _mod_kernel,
        out_shape=jax.ShapeDtypeStruct((8, n), F32),
        grid=(n // step,),
        in_specs=[pl.BlockSpec((8, D_MODEL), lambda j: (0, 0))] + w_specs
                 + [pl.BlockSpec((1, step), lambda j: (0, j))],
        out_specs=pl.BlockSpec((8, step), lambda j: (0, j)),
        compiler_params=_cparams(("arbitrary",)),
        name="modulation",
    )(cond8, *([w] * MOD_STREAMS), b.reshape(1, n))
    return out.reshape(8, 1, n)


def _mod_spec(mod_base, tiles_per_mod):
    return pl.BlockSpec((None, 1, 6 * D_MODEL), lambda i: (mod_base + i // tiles_per_mod, 0, 0))


def _const_spec(shape):
    nd = len(shape)
    return pl.BlockSpec(shape, lambda i: (0,) * nd)


HALO = 8


def _front_kernel(tiles_per_seq, x_ref, xp_ref, xn_ref, m_ref, win_ref, qn_ref, qup_ref, kvn_ref, kvup_ref,
                  cos_ref, sin_ref, cw_ref, cb_ref, skip_ref,
                  u_ref, e_ref, x0_ref, q_ref, k_ref, v_ref, kvn_out_ref, kpe_ref):
    i = pl.program_id(0)
    m = m_ref[...]
    tm = x_ref.shape[0]
    nh = 3 * HY_CH
    xe = jnp.concatenate([xp_ref[...], x_ref[...], xn_ref[...]], axis=0)
    he = _ln_plain(xe) * (1.0 + m[:, D_MODEL:2 * D_MODEL]) + m[:, 0:D_MODEL]

    z = _bdot(he[HALO:HALO + tm], win_ref[:, nh:])
    zh = _bdot(he, win_ref[:, :nh])
    q_c = z[:, 0:256]
    kv_c = z[:, 256:384]
    cos = cos_ref[...]
    sin = sin_ref[...]
    kpe = z[:, 384:512] * cos + z[:, 512:640] * sin
    kpe_ref[...] = kpe[:, :QK_ROPE]
    kpe_b = kpe.astype(BF16)
    q = _bdot(_rms(q_c, qn_ref[...]), qup_ref[...]) * (LOG2E / math.sqrt(QK_NOPE + QK_ROPE))
    kvn = _rms(kv_c, kvn_ref[...])
    kvn_out_ref[...] = kvn
    kv = _bdot(kvn, kvup_ref[...])

    pos = i % tiles_per_seq
    rows = lax.broadcasted_iota(jnp.int32, (tm + 2 * HALO, 1), 0)
    inside = jnp.logical_and(jnp.logical_or(rows >= HALO, pos != 0),
                             jnp.logical_or(rows < tm + HALO, pos != tiles_per_seq - 1))
    zh = jnp.where(inside, zh, 0.0)
    cw = cw_ref[...]
    pz = (pltpu.roll(zh, 1, 0) * cw[0:1, :] + zh * cw[1:2, :]
          + pltpu.roll(zh, tm + 2 * HALO - 1, 0) * cw[2:3, :])[HALO:HALO + tm] + cb_ref[...]
    u = pz[:, 2 * HY_CH:] * pz[:, HY_CH:2 * HY_CH]
    u_ref[...] = u.astype(u_ref.dtype)
    e_ref[...] = u * skip_ref[...]
    x0_ref[...] = pz[:, :HY_CH]

    for hd in range(MLA_HEADS):
        a = hd * LANE
        q_pe = (q[:, 512 + a:512 + a + LANE] * cos + q[:, 1024 + a:1024 + a + LANE] * sin).astype(BF16)
        q_ref[hd] = jnp.concatenate([q[:, a:a + LANE].astype(BF16), q_pe], axis=-1)
        k_ref[hd] = jnp.concatenate([kv[:, 2 * a:2 * a + LANE].astype(BF16), kpe_b], axis=-1)
        v_ref[hd] = _vt_rows(kv[:, 2 * a + LANE:2 * a + 2 * LANE])


def _front(x, mods, mod_base, rows_per_mod, w, cos, sin, rope, seq_len, tm, u_dtype):
    t = x.shape[0]
    tiles_per_seq = seq_len // tm
    tiles_per_mod = rows_per_mod // tm
    win, qn, qup, kvn, kvup, conv_w, conv_b, skip = w
    if rope:
        tab_spec = pl.BlockSpec((tm, LANE), lambda i: (i % tiles_per_seq, 0))
    else:
        tab_spec = pl.BlockSpec((tm, LANE), lambda i: (0, 0))
    r8 = tm // HALO
    n8 = t // HALO
    hy_out = lambda dt: jax.ShapeDtypeStruct((t, HY_CH), dt)
    hy_spec = pl.BlockSpec((tm, HY_CH), lambda i: (i, 0))
    return pl.pallas_call(
        functools.partial(_front_kernel, tiles_per_seq),
        out_shape=(hy_out(u_dtype), hy_out(F32), hy_out(F32),
                   jax.ShapeDtypeStruct((MLA_HEADS, t, QK_PAD), BF16),
                   jax.ShapeDtypeStruct((MLA_HEADS, t, QK_PAD), BF16),
                   jax.ShapeDtypeStruct((MLA_HEADS, VT_ROWS, t), BF16),
                   jax.ShapeDtypeStruct((t, KV_LORA), F32),
                   jax.ShapeDtypeStruct((t, QK_ROPE), F32)),
        grid=(t // tm,),
        in_specs=[pl.BlockSpec((tm, D_MODEL), lambda i: (i, 0)),
                  pl.BlockSpec((HALO, D_MODEL), lambda i: (jnp.maximum(i * r8 - 1, 0), 0)),
                  pl.BlockSpec((HALO, D_MODEL), lambda i: (jnp.minimum((i + 1) * r8, n8 - 1), 0)),
                  _mod_spec(mod_base, tiles_per_mod),
                  _const_spec(win.shape), _const_spec(qn.shape), _const_spec(qup.shape),
                  _const_spec(kvn.shape), _const_spec(kvup.shape),
                  tab_spec, tab_spec,
                  _const_spec(conv_w.shape), _const_spec(conv_b.shape), _const_spec(skip.shape)],
        out_specs=(hy_spec, hy_spec, hy_spec,
                   pl.BlockSpec((MLA_HEADS, tm, QK_PAD), lambda i: (0, i, 0)),
                   pl.BlockSpec((MLA_HEADS, tm, QK_PAD), lambda i: (0, i, 0)),
                   pl.BlockSpec((MLA_HEADS, VT_ROWS, tm), lambda i: (0, 0, i)),
                   pl.BlockSpec((tm, KV_LORA), lambda i: (i, 0)),
                   pl.BlockSpec((tm, QK_ROPE), lambda i: (i, 0))),
        compiler_params=_cparams(("arbitrary",)),
        name="l0_front",
    )(x, x, x, mods, win, qn, qup, kvn, kvup, cos, sin, conv_w, conv_b, skip)


def _cache_kv_kernel(ckv_ref, kr_ref, kvup_ref, k_ref, v_ref):
    kv = _bdot(ckv_ref[...], kvup_ref[...])
    kr = kr_ref[...].astype(BF16)
    for hd in range(MLA_HEADS):
        a = 2 * hd * LANE
        k_ref[hd] = jnp.concatenate([kv[:, a:a + LANE].astype(BF16), kr], axis=-1)
        v_ref[hd] = _vt_rows(kv[:, a + LANE:a + 2 * LANE])


def _cache_kv(ckv, krope_pad, kvup):
    t = ckv.shape[0]
    return pl.pallas_call(
        _cache_kv_kernel,
        out_shape=(jax.ShapeDtypeStruct((MLA_HEADS, t, QK_PAD), BF16),
                   jax.ShapeDtypeStruct((MLA_HEADS, VT_ROWS, t), BF16)),
        name="l0_cache_kv",
    )(ckv, krope_pad, kvup)


def _col_reduce(x, op):
    rows, n = x.shape
    for g in (32, 8):
        if rows % (8 * g) == 0 and rows > 8 * g:
            x = op(x.reshape(rows // (8 * g), 8 * g, n), axis=0)
            rows = 8 * g
    return op(x, axis=0, keepdims=True)


def _attn_kernel(n_kv, hps, q_ref, *refs):
    k_refs, vt_refs = refs[:n_kv], refs[n_kv:2 * n_kv]
    o_ref, s_even, s_odd = refs[2 * n_kv:]
    i = pl.program_id(0)

    @pl.when(i == 0)
    def _():
        s_odd[...] = jnp.zeros_like(s_odd)

    def step(s_write, s_read):
        nt = (((1,), (1,)), ((), ()))
        for h in range(hps):
            q = q_ref[h]
            r0 = 0
            for k_ref in k_refs:
                lk = k_ref.shape[1]
                s_write[h, r0:r0 + lk, :] = lax.dot_general(k_ref[h], q, nt, preferred_element_type=F32)
                r0 += lk
        for h in range(hps):
            s = s_read[h]
            m = _col_reduce(s, jnp.max)
            pb = jnp.exp2(s - m).astype(BF16)
            acc = None
            r0 = 0
            for vt_ref in vt_refs:
                lk = vt_ref.shape[2]
                pv = jnp.dot(vt_ref[h], pb[r0:r0 + lk, :], preferred_element_type=F32)
                acc = pv if acc is None else acc + pv
                r0 += lk
            o_ref[:, h * V_HEAD:(h + 1) * V_HEAD] = jnp.transpose(
                acc[:V_HEAD] / acc[V_HEAD:V_HEAD + 1]).astype(o_ref.dtype)

    pl.when(i % 2 == 0)(lambda: step(s_even, s_odd))
    pl.when(i % 2 == 1)(lambda: step(s_odd, s_even))


def _attention(q, k, vt, extra, nb, lq, tq, hps):
    nq = lq // tq
    ng = MLA_HEADS // hps
    n_tiles = nb * ng * nq

    def where(t):
        bh = t // nq
        return bh // ng, bh % ng, t % nq

    def score_side(fn):
        return lambda i: fn(*where(jnp.minimum(i, n_tiles - 1)))

    def value_side(fn):
        return lambda i: fn(*where(jnp.maximum(i - 1, 0)))

    ks, vts = [k], [vt]
    if extra is not None:
        ks.append(extra[0])
        vts.append(extra[1])
    in_specs = [pl.BlockSpec((hps, tq, QK_PAD), score_side(lambda b, h, j: (h, b * nq + j, 0)))]
    in_specs += [pl.BlockSpec((hps, a.shape[1] // nb, QK_PAD), score_side(lambda b, h, j: (h, b, 0))) for a in ks]
    in_specs += [pl.BlockSpec((hps, VT_ROWS, a.shape[2] // nb), value_side(lambda b, h, j: (h, 0, b))) for a in vts]
    lk_total = sum(a.shape[1] // nb for a in ks)
    return pl.pallas_call(
        functools.partial(_attn_kernel, len(ks), hps),
        out_shape=jax.ShapeDtypeStruct((nb * lq, MLA_HEADS * V_HEAD), BF16),
        grid=(n_tiles + 1,),
        in_specs=in_specs,
        out_specs=pl.BlockSpec((tq, hps * V_HEAD), value_side(lambda b, h, j: (b * nq + j, h))),
        scratch_shapes=[pltpu.VMEM((hps, lk_total, tq), F32), pltpu.VMEM((hps, lk_total, tq), F32)],
        compiler_params=_cparams(("arbitrary",)),
        name="l0_attention",
    )(q, *ks, *vts)


def _filter_kernel(z_ref, w1_ref, b1_ref, fr_ref, w2_ref, b2_ref, w3_ref, dl_ref, h_ref, norm_ref):
    i = pl.program_id(0)
    z = z_ref[...]
    tl = z.shape[0]
    fr = fr_ref[...]
    z2 = jnp.concatenate([z[:tl // 2], z[tl // 2:]], axis=1)
    h = jnp.sin(fr * (jnp.dot(z2, w1_ref[...], precision=HI, preferred_element_type=F32) + b1_ref[...]))
    h = jnp.sin(fr * (jnp.dot(h, w2_ref[...], precision=HI, preferred_element_type=F32) + b2_ref[...]))
    h = _bdot(h, w3_ref[...])
    h = jnp.concatenate([h[:, :2 * HY_CH], h[:, 2 * HY_CH:]], axis=0)
    decay = jnp.exp(-(z[:, 0:1] * dl_ref[...]))
    hf = h[:, :HY_CH] * decay
    hb = h[:, HY_CH:] * decay
    part = jnp.sum(jnp.abs(hf) + jnp.abs(hb), axis=0, keepdims=True)

    @pl.when(i == 0)
    def _():
        norm_ref[...] = part

    @pl.when(i > 0)
    def _():
        norm_ref[...] += part

    rows = lax.broadcasted_iota(jnp.int32, hb.shape, 0) + i * tl
    h_ref[0] = hf.astype(h_ref.dtype)
    h_ref[1] = jnp.where(rows == 0, 0.0, hb).astype(h_ref.dtype)


def _filter_embedding(L):
    t = np.linspace(0.0, 1.0, L)[:, None]
    w_ang = 2.0 * np.pi * np.arange(L) / L
    bands = np.linspace(1e-4, FILT_BANDS - 1, FILT_BANDS)
    ang = w_ang[:, None] * bands[None, :]
    z = np.zeros((L, LANE), np.float64)
    z[:, 0:1] = t
    z[:, 1:1 + FILT_BANDS] = np.cos(ang)
    z[:, 1 + FILT_BANDS:1 + 2 * FILT_BANDS] = -np.sin(ang)
    return jnp.asarray(z, F32)


def _filters(L, h_dtype, w1p, b1, fr, w2, b2, w3):
    tl = min(L, 512)
    z = _filter_embedding(L)
    deltas = jnp.asarray(np.abs(np.linspace(MIN_DECAY, MAX_DECAY, HY_CH))[None, :], F32)
    return pl.pallas_call(
        _filter_kernel,
        out_shape=(jax.ShapeDtypeStruct((2, L, HY_CH), h_dtype), jax.ShapeDtypeStruct((1, HY_CH), F32)),
        grid=(L // tl,),
        in_specs=[pl.BlockSpec((tl, LANE), lambda i: (i, 0)),
                  _const_spec(w1p.shape), _const_spec(b1.shape), _const_spec(fr.shape),
                  _const_spec(w2.shape), _const_spec(b2.shape), _const_spec(w3.shape),
                  _const_spec(deltas.shape)],
        out_specs=(pl.BlockSpec((2, tl, HY_CH), lambda i: (0, i, 0)),
                   pl.BlockSpec((1, HY_CH), lambda i: (0, 0))),
        compiler_params=_cparams(("arbitrary",)),
        name="l0_hyena_filters",
    )(z, w1p, b1, fr, w2, b2, w3, deltas)


def _dft_tables(kind, L, ti):
    ni = L // ti
    i = np.arange(ti, dtype=np.int64)[:, None]
    big = (np.arange(ni, dtype=np.int64) * ti)[:, None]
    c = np.arange(L, dtype=np.int64)[None, :]
    if kind == "hy_fwd":
        period = 4 * L
        base_idx = (2 * i + 1) * c
        r_idx = 2 * big * c
        scale = 1.0
    elif kind == "hy_inv":
        period = 4 * L
        base_idx = (2 * c + 1) * i
        r_idx = (2 * c + 1) * big
        scale = 1.0 / L
    else:
        period = L
        base_idx = i * c
        r_idx = big * c
        scale = 1.0 / math.sqrt(L * FN_GROUP_CH)
    ab = 2.0 * np.pi * (base_idx % period) / period
    ar = 2.0 * np.pi * (r_idx % period) / period
    return (jnp.asarray(np.cos(ab), F32), jnp.asarray(np.sin(ab), F32),
            jnp.asarray(scale * np.cos(ar), F32).reshape(ni, 1, L),
            jnp.asarray(scale * np.sin(ar), F32).reshape(ni, 1, L))


def _dft_kernel(mode, nb, n_x, *refs):
    bc_ref, bs_ref, rc_ref, rs_ref = refs[:4]
    x_refs = refs[4:4 + n_x]
    rest = refs[4 + n_x:]
    p_ref, q_ref = rest[-2], rest[-1]
    j = pl.program_id(2)
    nj = pl.num_programs(2)
    tj = x_refs[0].shape[1]
    if bc_ref.shape[1] == tj:
        bc, bs, rc, rs = bc_ref[...], bs_ref[...], rc_ref[...], rs_ref[...]
    else:
        off = pl.multiple_of(j * tj, tj)
        bc, bs = bc_ref[:, pl.ds(off, tj)], bs_ref[:, pl.ds(off, tj)]
        rc, rs = rc_ref[:, pl.ds(off, tj)], rs_ref[:, pl.ds(off, tj)]
    tc = (bc * rc - bs * rs).astype(BF16)
    ts = (bs * rc + bc * rs).astype(BF16)
    x1_ref = x_refs[0]
    x2_ref = x_refs[-1]

    pq = [(jnp.dot(tc, x1_ref[b], preferred_element_type=F32),
           jnp.dot(ts, x2_ref[b], preferred_element_type=F32)) for b in range(nb)]

    @pl.when(j == 0)
    def _():
        for b in range(nb):
            p_ref[b] = pq[b][0]
            q_ref[b] = pq[b][1]

    @pl.when(j > 0)
    def _():
        for b in range(nb):
            p_ref[b] += pq[b][0]
            q_ref[b] += pq[b][1]

    @pl.when(j == nj - 1)
    def _():
        if mode == "filt":
            nrm = rest[0][...]
            kre_ref, kim_ref = rest[1], rest[2]
            kre_ref[...] = (p_ref[0] + p_ref[1]) / nrm
            kim_ref[...] = (q_ref[1] - q_ref[0]) / nrm
        elif mode == "fwdk":
            kre, kim = rest[0][...], rest[1][...]
            yre_ref, yim_ref = rest[2], rest[3]
            for b in range(nb):
                pp, qq = p_ref[b], q_ref[b]
                yre_ref[b] = (pp * kre + qq * kim).astype(BF16)
                yim_ref[b] = (pp * kim - qq * kre).astype(BF16)
        elif mode == "inv":
            e_ref, x0_ref, o_ref = rest[0], rest[1], rest[2]
            for b in range(nb):
                o_ref[b] = ((p_ref[b] - q_ref[b] + e_ref[b]) * x0_ref[b]).astype(BF16)
        else:
            o_ref = rest[0]
            for b in range(nb):
                o_ref[b] = (p_ref[b] - q_ref[b]).astype(BF16)


def _dft(mode, kind, xs, extras, nb):
    B, L, C = xs[0].shape
    ti = min(L, 256)
    tj = min(L, 512)
    bc, bs, rc, rs = _dft_tables(kind, L, ti)
    grid = (B // nb, L // ti, L // tj)
    x_spec = pl.BlockSpec((nb, tj, C), lambda g, i, j: (g, j, 0))
    row_spec = lambda c, dt=None: pl.BlockSpec((nb, ti, c), lambda g, i, j: (g, i, 0))
    in_specs = [pl.BlockSpec((ti, L), lambda g, i, j: (0, 0)),
                pl.BlockSpec((ti, L), lambda g, i, j: (0, 0)),
                pl.BlockSpec((None, 1, L), lambda g, i, j: (i, 0, 0)),
                pl.BlockSpec((None, 1, L), lambda g, i, j: (i, 0, 0))] + [x_spec] * len(xs)
    if mode == "filt":
        in_specs += [pl.BlockSpec((1, HY_CH), lambda g, i, j: (0, 0))]
        out_shape = (jax.ShapeDtypeStruct((L, HY_CH), F32),) * 2
        out_specs = (pl.BlockSpec((ti, HY_CH), lambda g, i, j: (i, 0)),) * 2
    elif mode == "fwdk":
        in_specs += [pl.BlockSpec((ti, HY_CH), lambda g, i, j: (i, 0))] * 2
        out_shape = (jax.ShapeDtypeStruct((B, L, C), BF16),) * 2
        out_specs = (row_spec(C),) * 2
    elif mode == "inv":
        in_specs += [row_spec(C)] * 2
        out_shape = jax.ShapeDtypeStruct((B, L, C), BF16)
        out_specs = row_spec(C)
    else:
        out_shape = jax.ShapeDtypeStruct((B, L, C), BF16)
        out_specs = row_spec(C)
    return pl.pallas_call(
        functools.partial(_dft_kernel, mode, nb, len(xs)),
        out_shape=out_shape,
        grid=grid,
        in_specs=in_specs,
        out_specs=out_specs,
        scratch_shapes=[pltpu.VMEM((nb, ti, C), F32), pltpu.VMEM((nb, ti, C), F32)],
        compiler_params=_cparams(("arbitrary", "arbitrary", "arbitrary")),
        name="dft_" + mode,
    )(bc, bs, rc, rs, *xs, *extras)


FFT_R = 64
HYENA_KF = 16
FNET_KF = 16


def _pack_pairs(x):
    return pltpu.bitcast(x.astype(BF16), jnp.uint32)


def _unpack_pairs(w):
    return pltpu.bitcast(w, BF16)


def _to_blocks(w):
    return jnp.swapaxes(w.reshape(FFT_R, w.shape[-2], w.shape[-1]), 0, 1)


def _from_blocks(ws):
    kf, c = len(ws), ws[0].shape[-1]
    return jnp.swapaxes(jnp.stack(ws, axis=0), 0, 1).reshape(FFT_R // kf, kf, kf, c)


def _lead_in_kernel(g_ref, x_ref, o_ref):
    g = g_ref[...]
    xt = jnp.swapaxes(x_ref[...], 0, 1)
    for j in range(x_ref.shape[1]):
        o_ref[j] = _pack_pairs(jnp.dot(g, xt[j].astype(BF16), preferred_element_type=F32))


def _lead_in(g, x, kf, name):
    nbx, _, _, c = x.shape
    m2 = g.shape[0] // 2
    return pl.pallas_call(
        _lead_in_kernel,
        out_shape=jax.ShapeDtypeStruct((nbx, FFT_R // kf, kf, m2, c), jnp.uint32),
        grid=(nbx, FFT_R // kf),
        in_specs=[pl.BlockSpec(g.shape, lambda b, k: (0, 0)),
                  pl.BlockSpec((None, FFT_R, kf, c), lambda b, k: (b, 0, k, 0))],
        out_specs=pl.BlockSpec((None, None, kf, m2, c), lambda b, k: (b, k, 0, 0, 0)),
        compiler_params=_cparams(("arbitrary", "arbitrary")),
        name=name,
    )(g, x)


def _lead_out_kernel(n_extra, g_ref, w_ref, *rest):
    g = g_ref[...]
    o_ref = rest[-1]
    ys = [jnp.dot(g, _unpack_pairs(w_ref[j]), preferred_element_type=F32) for j in range(w_ref.shape[0])]
    y = jnp.swapaxes(jnp.stack(ys, axis=0), 0, 1)
    if n_extra:
        y = (y + rest[0][...]) * rest[1][...]
    o_ref[...] = y


def _lead_out(g, w, extras, name):
    nb, nk, kf, k2, c = w.shape
    blk = pl.BlockSpec((None, FFT_R, kf, c), lambda b, k: (b, 0, k, 0))
    return pl.pallas_call(
        functools.partial(_lead_out_kernel, len(extras)),
        out_shape=jax.ShapeDtypeStruct((nb, FFT_R, FFT_R, c), F32),
        grid=(nb, nk),
        in_specs=[pl.BlockSpec(g.shape, lambda b, k: (0, 0)),
                  pl.BlockSpec((None, None, kf, k2, c), lambda b, k: (b, k, 0, 0, 0))] + [blk] * len(extras),
        out_specs=blk,
        compiler_params=_cparams(("arbitrary", "arbitrary")),
        name=name,
    )(g, w, *extras)


def _interleave(a, b, axis):
    st = np.stack([a, b], axis=axis + 1)
    shape = list(a.shape)
    shape[axis] *= 2
    return st.reshape(shape)


def _hy2_tables():
    L = FFT_R * FFT_R
    n2 = 2 * L
    f1 = np.arange(2 * FFT_R, dtype=np.int64)
    s1 = np.arange(FFT_R, dtype=np.int64)
    th = np.pi * (((2 * f1[:, None] + 1) * s1[None, :]) % (4 * FFT_R)) / (2 * FFT_R)
    ga = _interleave(np.cos(th), -np.sin(th), 0)
    ma = _interleave(np.cos(th).T, -np.sin(th).T, 1) / L
    f2 = np.arange(FFT_R // 2, dtype=np.int64)
    s2 = np.arange(FFT_R, dtype=np.int64)
    idx = ((n2 // FFT_R) * 2 * f2[None, :, None] * s2[None, None, :]
           + (2 * f1[:, None, None] + 1) * s2[None, None, :]) % (2 * n2)
    al = np.pi * idx / n2
    c, s = np.cos(al), np.sin(al)
    nmat = np.concatenate([_interleave(c, s, 2), _interleave(-s, c, 2)], axis=1)
    ct, st = np.transpose(c, (0, 2, 1)), np.transpose(s, (0, 2, 1))
    mmat = _interleave(np.concatenate([ct, -st], axis=2), np.concatenate([st, ct], axis=2), 1)
    bf = lambda a: jnp.asarray(a, F32).astype(BF16)
    return bf(ga), bf(nmat), bf(mmat), bf(ma)


def _hy_mid_kernel(a_ref, n_ref, m_ref, k_ref, e_ref):
    half = FFT_R // 2
    kf = a_ref.shape[-2]
    a = _to_blocks(a_ref[...])
    ts = [jnp.dot(n_ref[j], _unpack_pairs(a[j]), preferred_element_type=F32)
          for j in range(kf)]
    ys = []
    for j in range(kf):
        tr, ti = ts[j][:half], ts[j][half:]
        kr, ki = k_ref[j, 0], k_ref[j, 1]
        ys.append(jnp.concatenate([tr * kr - ti * ki, tr * ki + ti * kr], axis=0).astype(BF16))
    e_ref[...] = _from_blocks([_pack_pairs(jnp.dot(m_ref[j], ys[j], preferred_element_type=F32))
                               for j in range(kf)])


def _hy_mid(a, nmat, mmat, khat):
    nb, nk, kf, _, c = a.shape
    nf1 = 2 * FFT_R
    blk = pl.BlockSpec((None, nk, kf, kf, c), lambda i, b: (b, 0, 0, i, 0))
    return pl.pallas_call(
        _hy_mid_kernel,
        out_shape=jax.ShapeDtypeStruct(a.shape, jnp.uint32),
        grid=(nf1 // kf, nb),
        in_specs=[blk,
                  pl.BlockSpec((kf, FFT_R, 2 * FFT_R), lambda i, b: (i, 0, 0)),
                  pl.BlockSpec((kf, 2 * FFT_R, FFT_R), lambda i, b: (i, 0, 0)),
                  pl.BlockSpec((kf, 2, FFT_R // 2, c), lambda i, b: (i, 0, 0, 0))],
        out_specs=blk,
        compiler_params=_cparams(("arbitrary", "arbitrary")),
        name="l0_hyena_mid",
    )(a, nmat, mmat, khat)


def _hy_kfilt_kernel(a_ref, n_ref, nrm_ref, k_ref):
    half = FFT_R // 2
    nrm = nrm_ref[...]
    af, ab = _to_blocks(a_ref[0]), _to_blocks(a_ref[1])
    for j in range(a_ref.shape[-2]):
        tf = jnp.dot(n_ref[j], _unpack_pairs(af[j]), preferred_element_type=F32)
        tb = jnp.dot(n_ref[j], _unpack_pairs(ab[j]), preferred_element_type=F32)
        k_ref[j, 0] = (tf[:half] + tb[:half]) / nrm
        k_ref[j, 1] = (tf[half:] - tb[half:]) / nrm


def _hy_kfilt(a, nmat, nrm):
    _, nk, kf, _, c = a.shape
    nf1 = 2 * FFT_R
    return pl.pallas_call(
        _hy_kfilt_kernel,
        out_shape=jax.ShapeDtypeStruct((nf1, 2, FFT_R // 2, c), F32),
        grid=(nf1 // kf,),
        in_specs=[pl.BlockSpec((2, nk, kf, kf, c), lambda i: (0, 0, 0, i, 0)),
                  pl.BlockSpec((kf, FFT_R, 2 * FFT_R), lambda i: (i, 0, 0)),
                  _const_spec(nrm.shape)],
        out_specs=pl.BlockSpec((kf, 2, FFT_R // 2, c), lambda i: (i, 0, 0, 0)),
        compiler_params=_cparams(("arbitrary",)),
        name="l0_hyena_kfilt",
    )(a, nmat, nrm)


def _hyena_long(u, e, x0, hfilt, nrm, nb):
    L = FFT_R * FFT_R
    c = u.shape[-1]
    v4 = lambda a, n: a.reshape(n, FFT_R, FFT_R, c)
    ga, nmat, mmat, ma = _hy2_tables()
    khat = _hy_kfilt(_lead_in(ga, v4(hfilt, 2), HYENA_KF, "l0_hyena_fwd_a"), nmat, nrm)
    ee = _hy_mid(_lead_in(ga, v4(u, nb), HYENA_KF, "l0_hyena_fwd_a"), nmat, mmat, khat)
    y = _lead_out(ma, ee, [v4(e, nb), v4(x0, nb)], "l0_hyena_inv_a")
    return y.reshape(nb * L, c)


def _fn2_tables():
    L = FFT_R * FFT_R
    r = np.arange(FFT_R, dtype=np.int64)
    idx = (FFT_R * r[None, :, None] * r[None, None, :] + r[None, :, None] * r[:, None, None]) % L
    gm = 2.0 * np.pi * idx / L
    c, s = np.cos(gm), np.sin(gm)
    g1 = _interleave(np.concatenate([c, -s], axis=2), np.concatenate([-s, -c], axis=2), 1)
    dl = 2.0 * np.pi * ((r[:, None] * r[None, :]) % FFT_R) / FFT_R
    g2 = _interleave(np.cos(dl), np.sin(dl), 1) / math.sqrt(L * FN_GROUP_CH)
    bf = lambda a: jnp.asarray(a, F32).astype(BF16)
    return bf(g1), bf(g2)


def _fnet_s1_kernel(x_ref, m_ref, cs_ref, g1_ref, o_ref, zc_ref, zs_ref):
    xs = jnp.swapaxes(x_ref[...], 0, 1).reshape(FNET_KF * FFT_R, D_MODEL)
    m = m_ref[...]
    h = (_ln_plain(xs) * (1.0 + m[:, D_MODEL:2 * D_MODEL]) + m[:, 0:D_MODEL]).astype(BF16)
    cs = cs_ref[...]
    for g in range(D_MODEL // FN_GROUP_CH):
        a = g * FN_GROUP_CH
        z = jnp.dot(h[:, a:a + FN_GROUP_CH], cs, preferred_element_type=F32)
        zc_ref[:, a:a + FN_GROUP_CH] = z[:, :FN_GROUP_CH].astype(BF16)
        zs_ref[:, a:a + FN_GROUP_CH] = z[:, FN_GROUP_CH:].astype(BF16)
    ws = []
    for j in range(FNET_KF):
        r0 = j * FFT_R
        s = jnp.concatenate([zc_ref[r0:r0 + FFT_R, :], zs_ref[r0:r0 + FFT_R, :]], axis=0)
        ws.append(_pack_pairs(jnp.dot(g1_ref[j], s, preferred_element_type=F32)))
    o_ref[...] = _from_blocks(ws)


def _fnet_long(x, mods, mod_base, nb):
    L = FFT_R * FFT_R
    d = D_MODEL
    g1, g2 = _fn2_tables()
    cs = _group_dft_table()
    kf = FNET_KF
    bb = pl.pallas_call(
        _fnet_s1_kernel,
        out_shape=jax.ShapeDtypeStruct((nb, FFT_R // kf, kf, FFT_R, d), jnp.uint32),
        grid=(nb, FFT_R // kf),
        in_specs=[pl.BlockSpec((None, FFT_R, kf, d), lambda b, k: (b, 0, k, 0)),
                  pl.BlockSpec((None, 1, 6 * d), lambda b, k: (mod_base + b, 0, 0)),
                  pl.BlockSpec(cs.shape, lambda b, k: (0, 0)),
                  pl.BlockSpec((kf, 2 * FFT_R, 2 * FFT_R), lambda b, k: (k, 0, 0))],
        out_specs=pl.BlockSpec((None, FFT_R // kf, kf, kf, d), lambda b, k: (b, 0, 0, k, 0)),
        scratch_shapes=[pltpu.VMEM((kf * FFT_R, d), BF16), pltpu.VMEM((kf * FFT_R, d), BF16)],
        compiler_params=_cparams(("arbitrary", "arbitrary")),
        name="l1_fnet_stage1",
    )(x.reshape(nb, FFT_R, FFT_R, d), mods, cs, g1)
    y = _lead_out(g2, bb, [], "l1_fnet_stage2")
    return y.reshape(nb * L, d)


def _group_dft_table():
    g = FN_GROUP_CH
    jk = (np.arange(g, dtype=np.int64)[:, None] * np.arange(g, dtype=np.int64)[None, :]) % g
    ang = 2.0 * np.pi * jk / g
    return jnp.asarray(np.concatenate([np.cos(ang), np.sin(ang)], axis=1), F32).astype(BF16)


def _fnet_front_kernel(x_ref, m_ref, cs_ref, zc_ref, zs_ref):
    m = m_ref[...]
    h = (_ln_plain(x_ref[...]) * (1.0 + m[:, D_MODEL:2 * D_MODEL]) + m[:, 0:D_MODEL]).astype(BF16)
    cs = cs_ref[...]
    for g in range(D_MODEL // FN_GROUP_CH):
        a = g * FN_GROUP_CH
        z = jnp.dot(h[:, a:a + FN_GROUP_CH], cs, preferred_element_type=F32)
        zc_ref[:, a:a + FN_GROUP_CH] = z[:, :FN_GROUP_CH].astype(BF16)
        zs_ref[:, a:a + FN_GROUP_CH] = z[:, FN_GROUP_CH:].astype(BF16)


def _fnet_front(x, mods, mod_base, tiles_per_mod):
    t = x.shape[0]
    tm = ROW_TILE
    cs = _group_dft_table()
    return pl.pallas_call(
        _fnet_front_kernel,
        out_shape=(jax.ShapeDtypeStruct((t, D_MODEL), BF16),) * 2,
        grid=(t // tm,),
        in_specs=[pl.BlockSpec((tm, D_MODEL), lambda i: (i, 0)),
                  _mod_spec(mod_base, tiles_per_mod),
                  _const_spec(cs.shape)],
        out_specs=(pl.BlockSpec((tm, D_MODEL), lambda i: (i, 0)),) * 2,
        compiler_params=_cparams(("arbitrary",)),
        name="l1_fnet_front",
    )(x, mods, cs)


def _post_kernel(n_a, *refs):
    x_ref, m_ref = refs[0], refs[1]
    a_refs = refs[2:2 + n_a]
    wo_refs = refs[2 + n_a:2 + 2 * n_a]
    g1_ref, b1_ref, w1_ref, w2_ref, g2_ref, b2_ref, o_ref = refs[2 + 2 * n_a:]
    m = m_ref[...]
    d = D_MODEL
    tm = x_ref.shape[0]
    halves = [(r, r + tm // POST_SPLIT) for r in range(0, tm, tm // POST_SPLIT)]
    outs = []
    for r0, r1 in halves:
        out = _bdot(a_refs[0][r0:r1, :], wo_refs[0][...])
        for a_ref, wo_ref in zip(a_refs[1:], wo_refs[1:]):
            out += _bdot(a_ref[r0:r1, :], wo_ref[...])
        outs.append(out)
    x1s, hs = [], []
    for (r0, r1), out in zip(halves, outs):
        x1 = _ln_plain(ALPHA * x_ref[r0:r1, :] + m[:, 2 * d:3 * d] * out) * g1_ref[...] + b1_ref[...]
        x1s.append(x1)
        hs.append((_ln_plain(x1) * (1.0 + m[:, 4 * d:5 * d]) + m[:, 3 * d:4 * d]).astype(BF16))
    accs = []
    n_c = D_FF // d

    def up(h, c):
        hc = jnp.maximum(jnp.dot(h, w1_ref[:, c * d:(c + 1) * d], preferred_element_type=F32), 0.0)
        return (hc * hc).astype(BF16)

    for h in hs:
        acc = None
        nxt = up(h, 0)
        for c in range(n_c):
            cur = nxt
            if c + 1 < n_c:
                nxt = up(h, c + 1)
            part = jnp.dot(cur, w2_ref[c * d:(c + 1) * d, :], preferred_element_type=F32)
            acc = part if acc is None else acc + part
        accs.append(acc)
    for (r0, r1), x1, acc in zip(halves, x1s, accs):
        o_ref[r0:r1, :] = _ln_plain(ALPHA * x1 + m[:, 5 * d:6 * d] * acc) * g2_ref[...] + b2_ref[...]


def _post(x, mods, mod_base, tiles_per_mod, a_list, wo_list, g1, b1, w1, w2, g2, b2):
    t = x.shape[0]
    tm = POST_TILE
    row = lambda c: pl.BlockSpec((tm, c), lambda i: (i, 0))
    once = lambda v: pl.BlockSpec(v.shape, lambda i: (0,) * v.ndim, pipeline_mode=pl.Buffered(1))
    in_specs = ([row(D_MODEL), _mod_spec(mod_base, tiles_per_mod * ROW_TILE // tm)]
                + [row(a.shape[1]) for a in a_list]
                + [once(w) for w in wo_list]
                + [once(v) for v in (g1, b1, w1, w2, g2, b2)])
    return pl.pallas_call(
        functools.partial(_post_kernel, len(a_list)),
        out_shape=jax.ShapeDtypeStruct((t, D_MODEL), F32),
        grid=(t // tm,),
        in_specs=in_specs,
        out_specs=row(D_MODEL),
        compiler_params=_cparams(("arbitrary",)),
        name="post_mlp",
    )(x, mods, *a_list, *wo_list, g1, b1, w1, w2, g2, b2)


def _rot_cols(w):
    parts = []
    for seg in range(2):
        o = seg * 32
        parts += [-w[:, o + 16:o + 32], w[:, o:o + 16]]
    return jnp.concatenate(parts, axis=1)


def _pad_cols(w, n):
    return jnp.pad(w, ((0, 0), (0, n - w.shape[1])))


def _block_diag2(w):
    z = jnp.zeros_like(w)
    return jnp.concatenate([jnp.concatenate([w, z], axis=1), jnp.concatenate([z, w], axis=1)], axis=0)


def _rope_tables(L):
    rows = L // GRID_W
    row = np.repeat(np.arange(rows, dtype=np.float64), GRID_W)
    col = np.tile(np.arange(GRID_W, dtype=np.float64), rows)
    half = QK_ROPE // 2
    inv = 1.0 / (ROPE_THETA ** (np.arange(0, half, 2, dtype=np.float64) / half))
    ar = row[:, None] * inv[None, :]
    ac = col[:, None] * inv[None, :]
    ang = np.concatenate([ar, ar, ac, ac], axis=1)
    cos = np.concatenate([np.cos(ang), np.ones_like(ang)], axis=1)
    sin = np.concatenate([np.sin(ang), np.zeros_like(ang)], axis=1)
    return jnp.asarray(cos, F32), jnp.asarray(sin, F32)


def kernel(x_prompt, x_sample, cache_l0_ckv, cache_l0_krope, c, c_ctx, l0_ada_w, l0_ada_b, l0_w_in, l0_conv_w, l0_conv_b, l0_hf_w1, l0_hf_b1, l0_hf_freq, l0_hf_w2, l0_hf_b2, l0_hf_w3, l0_hf_skip, l0_q_norm, l0_q_up, l0_kv_norm, l0_kv_up, l0_w_out, l0_ln1_g, l0_ln1_b, l0_mlp_w1, l0_mlp_w2, l0_ln2_g, l0_ln2_b, l1_ada_w, l1_ada_b, l1_w_out, l1_ln1_g, l1_ln1_b, l1_mlp_w1, l1_mlp_w2, l1_ln2_g, l1_ln2_b):
    nbc, lc, d = x_prompt.shape
    nbs, ls, _ = x_sample.shape
    past = cache_l0_ckv.shape[1]
    tm = ROW_TILE
    row1 = lambda v: v.reshape(1, -1)

    cond8 = jnp.concatenate([c_ctx[None, :], c, jnp.zeros((8 - 1 - nbs, d), F32)], axis=0)
    mods0 = _modulation(cond8, l0_ada_w, l0_ada_b)
    mods1 = _modulation(cond8, l1_ada_w, l1_ada_b)

    kpe_w = l0_w_in[:, 1920:1984]
    win = jnp.concatenate([l0_w_in[:, :1920], _pad_cols(kpe_w, LANE), _pad_cols(_rot_cols(kpe_w), LANE)],
                          axis=1).astype(BF16)
    dh = QK_NOPE + QK_ROPE
    q_nope = [l0_q_up[:, h * dh:h * dh + QK_NOPE] for h in range(MLA_HEADS)]
    q_pe = [l0_q_up[:, h * dh + QK_NOPE:(h + 1) * dh] for h in range(MLA_HEADS)]
    qup = jnp.concatenate(q_nope + [_pad_cols(w, LANE) for w in q_pe]
                          + [_pad_cols(_rot_cols(w), LANE) for w in q_pe], axis=1).astype(BF16)
    kvup = l0_kv_up.astype(BF16)
    front_w = (win, row1(l0_q_norm), qup, row1(l0_kv_norm), kvup, l0_conv_w, row1(l0_conv_b), row1(l0_hf_skip))
    w1p = jnp.pad(l0_hf_w1, ((0, LANE - l0_hf_w1.shape[0]), (0, 0)))
    two = lambda v: jnp.tile(row1(v), (1, 2))
    filt_w = (_block_diag2(w1p), two(l0_hf_b1), two(l0_hf_freq), _block_diag2(l0_hf_w2), two(l0_hf_b2),
              _block_diag2(l0_hf_w3).astype(BF16))
    wo0 = l0_w_out.astype(BF16)

    xc = x_prompt.reshape(nbc * lc, d)
    xs = x_sample.reshape(nbs * ls, d)
    groups = (
        dict(x=xc, nb=nbc, L=lc, mod_base=0, tiles_per_mod=nbc * lc // tm, dft_nb=4, tq=lc, hps=MLA_HEADS),
        dict(x=xs, nb=nbs, L=ls, mod_base=1, tiles_per_mod=ls // tm, dft_nb=nbs, tq=512, hps=1),
    )
    ones_tab = (jnp.concatenate([jnp.ones((tm, LANE), F32)], axis=0), jnp.zeros((tm, LANE), F32))

    outs = []
    ctx_ckv = ctx_krope = None
    for gi, g in enumerate(groups):
        nb, L = g["nb"], g["L"]
        tiles_per_seq = L // tm
        latent = gi == 1
        cos, sin = _rope_tables(L) if latent else ones_tab
        two_stage = L == FFT_R * FFT_R
        io_dtype = F32 if two_stage else BF16
        u, e, x0, q, k, vt, kvn, kpe = _front(g["x"], mods0, g["mod_base"], g["tiles_per_mod"] * tm, front_w,
                                              cos, sin, latent, L, min(L, FRONT_TILE), io_dtype)
        if latent:
            extra = _cache_kv(cache_l0_ckv.reshape(nbs * past, KV_LORA),
                              _pad_cols(cache_l0_krope.reshape(nbs * past, QK_ROPE), LANE), kvup)
        else:
            extra = None
            ctx_ckv = kvn.reshape(nb, L, KV_LORA)
            ctx_krope = kpe.reshape(nb, L, QK_ROPE)
        y_mla = _attention(q, k, vt, extra, nb, L, g["tq"], g["hps"])

        hfilt, hnorm = _filters(L, io_dtype, *filt_w)
        if two_stage:
            y_hy = _hyena_long(u, e, x0, hfilt, hnorm, nb)
        else:
            kre, kim = _dft("filt", "hy_fwd", [hfilt], [hnorm], 2)
            sh = (nb, L, HY_CH)
            yre, yim = _dft("fwdk", "hy_fwd", [u.reshape(sh)], [kre, kim], g["dft_nb"])
            y_hy = _dft("inv", "hy_inv", [yre, yim], [e.reshape(sh), x0.reshape(sh)], g["dft_nb"])
            y_hy = y_hy.reshape(nb * L, HY_CH)

        x1 = _post(g["x"], mods0, g["mod_base"], g["tiles_per_mod"], [y_hy, y_mla],
                   [wo0[:HY_CH], wo0[HY_CH:]], row1(l0_ln1_g), row1(l0_ln1_b),
                   l0_mlp_w1.astype(BF16), l0_mlp_w2.astype(BF16), row1(l0_ln2_g), row1(l0_ln2_b))

        if L == FFT_R * FFT_R:
            yf = _fnet_long(x1, mods1, g["mod_base"], nb)
        else:
            zc, zs = _fnet_front(x1, mods1, g["mod_base"], g["tiles_per_mod"])
            sh = (nb, L, d)
            yf = _dft("fnet", "fnet", [zc.reshape(sh), zs.reshape(sh)], [], min(g["dft_nb"], 2))
            yf = yf.reshape(nb * L, d)
        x2 = _post(x1, mods1, g["mod_base"], g["tiles_per_mod"], [yf],
                   [l1_w_out.astype(BF16)], row1(l1_ln1_g), row1(l1_ln1_b),
                   l1_mlp_w1.astype(BF16), l1_mlp_w2.astype(BF16), row1(l1_ln2_g), row1(l1_ln2_b))
        outs.append(x2.reshape(nb, L, d))

    return (outs[0], outs[1], ctx_ckv, ctx_krope)
```

```python
import functools
import math

import numpy as np
import jax
import jax.numpy as jnp
from jax import lax
from jax.experimental import pallas as pl
from jax.experimental.pallas import tpu as pltpu

F32 = jnp.float32
BF16 = jnp.bfloat16
HI = lax.Precision.HIGHEST

D_MODEL = 1024
DEPTH = 2
GRID_W = 64
HY_CH = 512
FILT_BANDS = 16
FILT_ORDER = 64
FAST_DECAY_PCT = 0.3
SLOW_DECAY_PCT = 1.5
DECAY_TARGET = 1e-2
MAX_DECAY = math.log(DECAY_TARGET) / FAST_DECAY_PCT
MIN_DECAY = math.log(DECAY_TARGET) / SLOW_DECAY_PCT
MLA_HEADS = 4
QK_NOPE = 128
QK_ROPE = 64
V_HEAD = 128
Q_LORA = 256
KV_LORA = 128
ROPE_THETA = 10000.0
FN_GROUP_CH = 128
D_FF = 4096
ALPHA = (2 * DEPTH) ** 0.25
LN_EPS = 1e-5
RMS_EPS = 1e-6

LANE = 128
ROW_TILE = 256
FRONT_TILE = 512
POST_TILE = 512
POST_SPLIT = 2
QK_PAD = 256
VT_ROWS = V_HEAD + 16
LOG2E = 1.4426950408889634
VMEM_LIMIT = 56 * 1024 * 1024


def _cparams(sem):
    return pltpu.CompilerParams(dimension_semantics=sem, vmem_limit_bytes=VMEM_LIMIT)


def _ln_plain(x):
    mu = jnp.mean(x, axis=-1, keepdims=True)
    xc = x - mu
    var = jnp.mean(xc * xc, axis=-1, keepdims=True)
    return xc * lax.rsqrt(var + LN_EPS)


def _rms(x, g):
    return x * lax.rsqrt(jnp.mean(x * x, axis=-1, keepdims=True) + RMS_EPS) * g


def _bdot(a, b):
    return jnp.dot(a.astype(BF16), b, preferred_element_type=F32)


def _vt_rows(v):
    ones = jnp.ones((VT_ROWS - V_HEAD, v.shape[0]), BF16)
    return jnp.concatenate([jnp.transpose(v).astype(BF16), ones], axis=0)


MOD_STREAMS = 3


def _mod_kernel(c_ref, *refs):
    w_refs, b_ref, o_ref = refs[:MOD_STREAMS], refs[MOD_STREAMS], refs[MOD_STREAMS + 1]
    c = c_ref[...]
    s = c / (1.0 + jnp.exp(-c))
    s_hi = s.astype(BF16)
    s_lo = (s - s_hi.astype(F32)).astype(BF16)
    s2 = jnp.concatenate([s_hi, s_lo], axis=0)
    nr = s.shape[0]
    tn = w_refs[0].shape[1]
    for k, w_ref in enumerate(w_refs):
        w = w_ref[...]
        w_hi = w.astype(BF16)
        w_lo = (w - w_hi.astype(F32)).astype(BF16)
        r1 = jnp.dot(s2, w_hi, preferred_element_type=F32)
        r2 = jnp.dot(s_hi, w_lo, preferred_element_type=F32)
        o_ref[:, k * tn:(k + 1) * tn] = r1[:nr] + r1[nr:] + r2 + b_ref[:, k * tn:(k + 1) * tn]


def _modulation(cond8, w, b):
    n = w.shape[1]
    tn = 512
    step = MOD_STREAMS * tn
    w_specs = [pl.BlockSpec((D_MODEL, tn), functools.partial(lambda j, k: (0, MOD_STREAMS * j + k), k=k))
               for k in range(MOD_STREAMS)]
    out = pl.pallas_call(
        _mod_kernel,
        out_shape=jax.ShapeDtypeStruct((8, n), F32),
        grid=(n // step,),
        in_specs=[pl.BlockSpec((8, D_MODEL), lambda j: (0, 0))] + w_specs
                 + [pl.BlockSpec((1, step), lambda j: (0, j))],
        out_specs=pl.BlockSpec((8, step), lambda j: (0, j)),
        compiler_params=_cparams(("arbitrary",)),
        name="modulation",
    )(cond8, *([w] * MOD_STREAMS), b.reshape(1, n))
    return out.reshape(8, 1, n)


def _mod_spec(mod_base, tiles_per_mod):
    return pl.BlockSpec((None, 1, 6 * D_MODEL), lambda i: (mod_base + i // tiles_per_mod, 0, 0))


def _const_spec(shape):
    nd = len(shape)
    return pl.BlockSpec(shape, lambda i: (0,) * nd)


HALO = 8


def _front_kernel(tiles_per_seq, x_ref, xp_ref, xn_ref, m_ref, win_ref, qn_ref, qup_ref, kvn_ref, kvup_ref,
                  cos_ref, sin_ref, cw_ref, cb_ref, skip_ref,
                  u_ref, e_ref, x0_ref, q_ref, k_ref, v_ref, kvn_out_ref, kpe_ref):
    i = pl.program_id(0)
    m = m_ref[...]
    tm = x_ref.shape[0]
    nh = 3 * HY_CH
    xe = jnp.concatenate([xp_ref[...], x_ref[...], xn_ref[...]], axis=0)
    he = _ln_plain(xe) * (1.0 + m[:, D_MODEL:2 * D_MODEL]) + m[:, 0:D_MODEL]

    z = _bdot(he[HALO:HALO + tm], win_ref[:, nh:])
    zh = _bdot(he, win_ref[:, :nh])
    q_c = z[:, 0:256]
    kv_c = z[:, 256:384]
    cos = cos_ref[...]
    sin = sin_ref[...]
    kpe = z[:, 384:512] * cos + z[:, 512:640] * sin
    kpe_ref[...] = kpe[:, :QK_ROPE]
    kpe_b = kpe.astype(BF16)
    q = _bdot(_rms(q_c, qn_ref[...]), qup_ref[...]) * (LOG2E / math.sqrt(QK_NOPE + QK_ROPE))
    kvn = _rms(kv_c, kvn_ref[...])
    kvn_out_ref[...] = kvn
    kv = _bdot(kvn, kvup_ref[...])

    pos = i % tiles_per_seq
    rows = lax.broadcasted_iota(jnp.int32, (tm + 2 * HALO, 1), 0)
    inside = jnp.logical_and(jnp.logical_or(rows >= HALO, pos != 0),
                             jnp.logical_or(rows < tm + HALO, pos != tiles_per_seq - 1))
    zh = jnp.where(inside, zh, 0.0)
    cw = cw_ref[...]
    pz = (pltpu.roll(zh, 1, 0) * cw[0:1, :] + zh * cw[1:2, :]
          + pltpu.roll(zh, tm + 2 * HALO - 1, 0) * cw[2:3, :])[HALO:HALO + tm] + cb_ref[...]
    u = pz[:, 2 * HY_CH:] * pz[:, HY_CH:2 * HY_CH]
    u_ref[...] = u.astype(u_ref.dtype)
    e_ref[...] = u * skip_ref[...]
    x0_ref[...] = pz[:, :HY_CH]

    for hd in range(MLA_HEADS):
        a = hd * LANE
        q_pe = (q[:, 512 + a:512 + a + LANE] * cos + q[:, 1024 + a:1024 + a + LANE] * sin).astype(BF16)
        q_ref[hd] = jnp.concatenate([q[:, a:a + LANE].astype(BF16), q_pe], axis=-1)
        k_ref[hd] = jnp.concatenate([kv[:, 2 * a:2 * a + LANE].astype(BF16), kpe_b], axis=-1)
        v_ref[hd] = _vt_rows(kv[:, 2 * a + LANE:2 * a + 2 * LANE])


def _front(x, mods, mod_base, rows_per_mod, w, cos, sin, rope, seq_len, tm, u_dtype):
    t = x.shape[0]
    tiles_per_seq = seq_len // tm
    tiles_per_mod = rows_per_mod // tm
    win, qn, qup, kvn, kvup, conv_w, conv_b, skip = w
    if rope:
        tab_spec = pl.BlockSpec((tm, LANE), lambda i: (i % tiles_per_seq, 0))
    else:
        tab_spec = pl.BlockSpec((tm, LANE), lambda i: (0, 0))
    r8 = tm // HALO
    n8 = t // HALO
    hy_out = lambda dt: jax.ShapeDtypeStruct((t, HY_CH), dt)
    hy_spec = pl.BlockSpec((tm, HY_CH), lambda i: (i, 0))
    return pl.pallas_call(
        functools.partial(_front_kernel, tiles_per_seq),
        out_shape=(hy_out(u_dtype), hy_out(F32), hy_out(F32),
                   jax.ShapeDtypeStruct((MLA_HEADS, t, QK_PAD), BF16),
                   jax.ShapeDtypeStruct((MLA_HEADS, t, QK_PAD), BF16),
                   jax.ShapeDtypeStruct((MLA_HEADS, VT_ROWS, t), BF16),
                   jax.ShapeDtypeStruct((t, KV_LORA), F32),
                   jax.ShapeDtypeStruct((t, QK_ROPE), F32)),
        grid=(t // tm,),
        in_specs=[pl.BlockSpec((tm, D_MODEL), lambda i: (i, 0)),
                  pl.BlockSpec((HALO, D_MODEL), lambda i: (jnp.maximum(i * r8 - 1, 0), 0)),
                  pl.BlockSpec((HALO, D_MODEL), lambda i: (jnp.minimum((i + 1) * r8, n8 - 1), 0)),
                  _mod_spec(mod_base, tiles_per_mod),
                  _const_spec(win.shape), _const_spec(qn.shape), _const_spec(qup.shape),
                  _const_spec(kvn.shape), _const_spec(kvup.shape),
                  tab_spec, tab_spec,
                  _const_spec(conv_w.shape), _const_spec(conv_b.shape), _const_spec(skip.shape)],
        out_specs=(hy_spec, hy_spec, hy_spec,
                   pl.BlockSpec((MLA_HEADS, tm, QK_PAD), lambda i: (0, i, 0)),
                   pl.BlockSpec((MLA_HEADS, tm, QK_PAD), lambda i: (0, i, 0)),
                   pl.BlockSpec((MLA_HEADS, VT_ROWS, tm), lambda i: (0, 0, i)),
                   pl.BlockSpec((tm, KV_LORA), lambda i: (i, 0)),
                   pl.BlockSpec((tm, QK_ROPE), lambda i: (i, 0))),
        compiler_params=_cparams(("arbitrary",)),
        name="l0_front",
    )(x, x, x, mods, win, qn, qup, kvn, kvup, cos, sin, conv_w, conv_b, skip)


def _cache_kv_kernel(ckv_ref, kr_ref, kvup_ref, k_ref, v_ref):
    kv = _bdot(ckv_ref[...], kvup_ref[...])
    kr = kr_ref[...].astype(BF16)
    for hd in range(MLA_HEADS):
        a = 2 * hd * LANE
        k_ref[hd] = jnp.concatenate([kv[:, a:a + LANE].astype(BF16), kr], axis=-1)
        v_ref[hd] = _vt_rows(kv[:, a + LANE:a + 2 * LANE])


def _cache_kv(ckv, krope_pad, kvup):
    t = ckv.shape[0]
    return pl.pallas_call(
        _cache_kv_kernel,
        out_shape=(jax.ShapeDtypeStruct((MLA_HEADS, t, QK_PAD), BF16),
                   jax.ShapeDtypeStruct((MLA_HEADS, VT_ROWS, t), BF16)),
        name="l0_cache_kv",
    )(ckv, krope_pad, kvup)


def _col_reduce(x, op):
    rows, n = x.shape
    for g in (32, 8):
        if rows % (8 * g) == 0 and rows > 8 * g:
            x = op(x.reshape(rows // (8 * g), 8 * g, n), axis=0)
            rows = 8 * g
    return op(x, axis=0, keepdims=True)


def _attn_kernel(n_kv, hps, q_ref, *refs):
    k_refs, vt_refs = refs[:n_kv], refs[n_kv:2 * n_kv]
    o_ref, s_even, s_odd, m_even, m_odd = refs[2 * n_kv:]
    i = pl.program_id(0)

    @pl.when(i == 0)
    def _():
        s_odd[...] = jnp.zeros_like(s_odd)
        m_odd[...] = jnp.zeros_like(m_odd)

    def step(s_write, m_write, s_read, m_read):
        nt = (((1,), (1,)), ((), ()))
        for h in range(hps):
            q = q_ref[h]
            r0 = 0
            m = None
            for k_ref in k_refs:
                lk = k_ref.shape[1]
                sblk = lax.dot_general(k_ref[h], q, nt, preferred_element_type=F32)
                s_write[h, r0:r0 + lk, :] = sblk
                mc = _col_reduce(sblk, jnp.max)
                m = mc if m is None else jnp.maximum(m, mc)
                r0 += lk
            m_write[h] = m
        for h in range(hps):
            pb = jnp.exp2(s_read[h] - m_read[h]).astype(BF16)
            acc = None
            r0 = 0
            for vt_ref in vt_refs:
                lk = vt_ref.shape[2]
                pv = jnp.dot(vt_ref[h], pb[r0:r0 + lk, :], preferred_element_type=F32)
                acc = pv if acc is None else acc + pv
                r0 += lk
            o_ref[:, h * V_HEAD:(h + 1) * V_HEAD] = jnp.transpose(
                acc[:V_HEAD] / acc[V_HEAD:V_HEAD + 1]).astype(o_ref.dtype)

    pl.when(i % 2 == 0)(lambda: step(s_even, m_even, s_odd, m_odd))
    pl.when(i % 2 == 1)(lambda: step(s_odd, m_odd, s_even, m_even))


def _attention(q, k, vt, extra, nb, lq, tq, hps):
    nq = lq // tq
    ng = MLA_HEADS // hps
    n_tiles = nb * ng * nq

    def where(t):
        bh = t // nq
        return bh // ng, bh % ng, t % nq

    def score_side(fn):
        return lambda i: fn(*where(jnp.minimum(i, n_tiles - 1)))

    def value_side(fn):
        return lambda i: fn(*where(jnp.maximum(i - 1, 0)))

    ks, vts = [k], [vt]
    if extra is not None:
        ks.append(extra[0])
        vts.append(extra[1])
    in_specs = [pl.BlockSpec((hps, tq, QK_PAD), score_side(lambda b, h, j: (h, b * nq + j, 0)))]
    in_specs += [pl.BlockSpec((hps, a.shape[1] // nb, QK_PAD), score_side(lambda b, h, j: (h, b, 0))) for a in ks]
    in_specs += [pl.BlockSpec((hps, VT_ROWS, a.shape[2] // nb), value_side(lambda b, h, j: (h, 0, b))) for a in vts]
    lk_total = sum(a.shape[1] // nb for a in ks)
    return pl.pallas_call(
        functools.partial(_attn_kernel, len(ks), hps),
        out_shape=jax.ShapeDtypeStruct((nb * lq, MLA_HEADS * V_HEAD), BF16),
        grid=(n_tiles + 1,),
        in_specs=in_specs,
        out_specs=pl.BlockSpec((tq, hps * V_HEAD), value_side(lambda b, h, j: (b * nq + j, h))),
        scratch_shapes=[pltpu.VMEM((hps, lk_total, tq), F32), pltpu.VMEM((hps, lk_total, tq), F32),
                        pltpu.VMEM((hps, 1, tq), F32), pltpu.VMEM((hps, 1, tq), F32)],
        compiler_params=_cparams(("arbitrary",)),
        name="l0_attention",
    )(q, *ks, *vts)


def _filter_kernel(z_ref, w1_ref, b1_ref, fr_ref, w2_ref, b2_ref, w3_ref, dl_ref, h_ref, norm_ref):
    i = pl.program_id(0)
    z = z_ref[...]
    tl = z.shape[0]
    fr = fr_ref[...]
    z2 = jnp.concatenate([z[:tl // 2], z[tl // 2:]], axis=1)
    h = jnp.sin(fr * (jnp.dot(z2, w1_ref[...], precision=HI, preferred_element_type=F32) + b1_ref[...]))
    h = jnp.sin(fr * (jnp.dot(h, w2_ref[...], precision=HI, preferred_element_type=F32) + b2_ref[...]))
    h = _bdot(h, w3_ref[...])
    h = jnp.concatenate([h[:, :2 * HY_CH], h[:, 2 * HY_CH:]], axis=0)
    decay = jnp.exp(-(z[:, 0:1] * dl_ref[...]))
    hf = h[:, :HY_CH] * decay
    hb = h[:, HY_CH:] * decay
    part = jnp.sum(jnp.abs(hf) + jnp.abs(hb), axis=0, keepdims=True)

    @pl.when(i == 0)
    def _():
        norm_ref[...] = part

    @pl.when(i > 0)
    def _():
        norm_ref[...] += part

    rows = lax.broadcasted_iota(jnp.int32, hb.shape, 0) + i * tl
    h_ref[0] = hf.astype(h_ref.dtype)
    h_ref[1] = jnp.where(rows == 0, 0.0, hb).astype(h_ref.dtype)


def _filter_embedding(L):
    t = np.linspace(0.0, 1.0, L)[:, None]
    w_ang = 2.0 * np.pi * np.arange(L) / L
    bands = np.linspace(1e-4, FILT_BANDS - 1, FILT_BANDS)
    ang = w_ang[:, None] * bands[None, :]
    z = np.zeros((L, LANE), np.float64)
    z[:, 0:1] = t
    z[:, 1:1 + FILT_BANDS] = np.cos(ang)
    z[:, 1 + FILT_BANDS:1 + 2 * FILT_BANDS] = -np.sin(ang)
    return jnp.asarray(z, F32)


def _filters(L, h_dtype, w1p, b1, fr, w2, b2, w3):
    tl = min(L, 512)
    z = _filter_embedding(L)
    deltas = jnp.asarray(np.abs(np.linspace(MIN_DECAY, MAX_DECAY, HY_CH))[None, :], F32)
    return pl.pallas_call(
        _filter_kernel,
        out_shape=(jax.ShapeDtypeStruct((2, L, HY_CH), h_dtype), jax.ShapeDtypeStruct((1, HY_CH), F32)),
        grid=(L // tl,),
        in_specs=[pl.BlockSpec((tl, LANE), lambda i: (i, 0)),
                  _const_spec(w1p.shape), _const_spec(b1.shape), _const_spec(fr.shape),
                  _const_spec(w2.shape), _const_spec(b2.shape), _const_spec(w3.shape),
                  _const_spec(deltas.shape)],
        out_specs=(pl.BlockSpec((2, tl, HY_CH), lambda i: (0, i, 0)),
                   pl.BlockSpec((1, HY_CH), lambda i: (0, 0))),
        compiler_params=_cparams(("arbitrary",)),
        name="l0_hyena_filters",
    )(z, w1p, b1, fr, w2, b2, w3, deltas)


def _dft_tables(kind, L, ti):
    ni = L // ti
    i = np.arange(ti, dtype=np.int64)[:, None]
    big = (np.arange(ni, dtype=np.int64) * ti)[:, None]
    c = np.arange(L, dtype=np.int64)[None, :]
    if kind == "hy_fwd":
        period = 4 * L
        base_idx = (2 * i + 1) * c
        r_idx = 2 * big * c
        scale = 1.0
    elif kind == "hy_inv":
        period = 4 * L
        base_idx = (2 * c + 1) * i
        r_idx = (2 * c + 1) * big
        scale = 1.0 / L
    else:
        period = L
        base_idx = i * c
        r_idx = big * c
        scale = 1.0 / math.sqrt(L * FN_GROUP_CH)
    ab = 2.0 * np.pi * (base_idx % period) / period
    ar = 2.0 * np.pi * (r_idx % period) / period
    return (jnp.asarray(np.cos(ab), F32), jnp.asarray(np.sin(ab), F32),
            jnp.asarray(scale * np.cos(ar), F32).reshape(ni, 1, L),
            jnp.asarray(scale * np.sin(ar), F32).reshape(ni, 1, L))


def _dft_kernel(mode, nb, n_x, *refs):
    bc_ref, bs_ref, rc_ref, rs_ref = refs[:4]
    x_refs = refs[4:4 + n_x]
    rest = refs[4 + n_x:]
    p_ref, q_ref = rest[-2], rest[-1]
    j = pl.program_id(2)
    nj = pl.num_programs(2)
    tj = x_refs[0].shape[1]
    if bc_ref.shape[1] == tj:
        bc, bs, rc, rs = bc_ref[...], bs_ref[...], rc_ref[...], rs_ref[...]
    else:
        off = pl.multiple_of(j * tj, tj)
        bc, bs = bc_ref[:, pl.ds(off, tj)], bs_ref[:, pl.ds(off, tj)]
        rc, rs = rc_ref[:, pl.ds(off, tj)], rs_ref[:, pl.ds(off, tj)]
    tc = (bc * rc - bs * rs).astype(BF16)
    ts = (bs * rc + bc * rs).astype(BF16)
    x1_ref = x_refs[0]
    x2_ref = x_refs[-1]

    pq = [(jnp.dot(tc, x1_ref[b], preferred_element_type=F32),
           jnp.dot(ts, x2_ref[b], preferred_element_type=F32)) for b in range(nb)]

    @pl.when(j == 0)
    def _():
        for b in range(nb):
            p_ref[b] = pq[b][0]
            q_ref[b] = pq[b][1]

    @pl.when(j > 0)
    def _():
        for b in range(nb):
            p_ref[b] += pq[b][0]
            q_ref[b] += pq[b][1]

    @pl.when(j == nj - 1)
    def _():
        if mode == "filt":
            nrm = rest[0][...]
            kre_ref, kim_ref = rest[1], rest[2]
            kre_ref[...] = (p_ref[0] + p_ref[1]) / nrm
            kim_ref[...] = (q_ref[1] - q_ref[0]) / nrm
        elif mode == "fwdk":
            kre, kim = rest[0][...], rest[1][...]
            yre_ref, yim_ref = rest[2], rest[3]
            for b in range(nb):
                pp, qq = p_ref[b], q_ref[b]
                yre_ref[b] = (pp * kre + qq * kim).astype(BF16)
                yim_ref[b] = (pp * kim - qq * kre).astype(BF16)
        elif mode == "inv":
            e_ref, x0_ref, o_ref = rest[0], rest[1], rest[2]
            for b in range(nb):
                o_ref[b] = ((p_ref[b] - q_ref[b] + e_ref[b]) * x0_ref[b]).astype(BF16)
        else:
            o_ref = rest[0]
            for b in range(nb):
                o_ref[b] = (p_ref[b] - q_ref[b]).astype(BF16)


def _dft(mode, kind, xs, extras, nb):
    B, L, C = xs[0].shape
    ti = min(L, 256)
    tj = min(L, 512)
    bc, bs, rc, rs = _dft_tables(kind, L, ti)
    grid = (B // nb, L // ti, L // tj)
    x_spec = pl.BlockSpec((nb, tj, C), lambda g, i, j: (g, j, 0))
    row_spec = lambda c, dt=None: pl.BlockSpec((nb, ti, c), lambda g, i, j: (g, i, 0))
    in_specs = [pl.BlockSpec((ti, L), lambda g, i, j: (0, 0)),
                pl.BlockSpec((ti, L), lambda g, i, j: (0, 0)),
                pl.BlockSpec((None, 1, L), lambda g, i, j: (i, 0, 0)),
                pl.BlockSpec((None, 1, L), lambda g, i, j: (i, 0, 0))] + [x_spec] * len(xs)
    if mode == "filt":
        in_specs += [pl.BlockSpec((1, HY_CH), lambda g, i, j: (0, 0))]
        out_shape = (jax.ShapeDtypeStruct((L, HY_CH), F32),) * 2
        out_specs = (pl.BlockSpec((ti, HY_CH), lambda g, i, j: (i, 0)),) * 2
    elif mode == "fwdk":
        in_specs += [pl.BlockSpec((ti, HY_CH), lambda g, i, j: (i, 0))] * 2
        out_shape = (jax.ShapeDtypeStruct((B, L, C), BF16),) * 2
        out_specs = (row_spec(C),) * 2
    elif mode == "inv":
        in_specs += [row_spec(C)] * 2
        out_shape = jax.ShapeDtypeStruct((B, L, C), BF16)
        out_specs = row_spec(C)
    else:
        out_shape = jax.ShapeDtypeStruct((B, L, C), BF16)
        out_specs = row_spec(C)
    return pl.pallas_call(
        functools.partial(_dft_kernel, mode, nb, len(xs)),
        out_shape=out_shape,
        grid=grid,
        in_specs=in_specs,
        out_specs=out_specs,
        scratch_shapes=[pltpu.VMEM((nb, ti, C), F32), pltpu.VMEM((nb, ti, C), F32)],
        compiler_params=_cparams(("arbitrary", "arbitrary", "arbitrary")),
        name="dft_" + mode,
    )(bc, bs, rc, rs, *xs, *extras)


FFT_R = 64
HYENA_KF = 16
FNET_KF = 16


def _pack_pairs(x):
    return pltpu.bitcast(x.astype(BF16), jnp.uint32)


def _unpack_pairs(w):
    return pltpu.bitcast(w, BF16)


def _to_blocks(w):
    return jnp.swapaxes(w.reshape(FFT_R, w.shape[-2], w.shape[-1]), 0, 1)


def _from_blocks(ws):
    kf, c = len(ws), ws[0].shape[-1]
    return jnp.swapaxes(jnp.stack(ws, axis=0), 0, 1).reshape(FFT_R // kf, kf, kf, c)


def _lead_in_kernel(g_ref, x_ref, o_ref):
    g = g_ref[...]
    xt = jnp.swapaxes(x_ref[...], 0, 1)
    for j in range(x_ref.shape[1]):
        o_ref[j] = _pack_pairs(jnp.dot(g, xt[j].astype(BF16), preferred_element_type=F32))


def _lead_in(g, x, kf, name):
    nbx, _, _, c = x.shape
    m2 = g.shape[0] // 2
    return pl.pallas_call(
        _lead_in_kernel,
        out_shape=jax.ShapeDtypeStruct((nbx, FFT_R // kf, kf, m2, c), jnp.uint32),
        grid=(nbx, FFT_R // kf),
        in_specs=[pl.BlockSpec(g.shape, lambda b, k: (0, 0)),
                  pl.BlockSpec((None, FFT_R, kf, c), lambda b, k: (b, 0, k, 0))],
        out_specs=pl.BlockSpec((None, None, kf, m2, c), lambda b, k: (b, k, 0, 0, 0)),
        compiler_params=_cparams(("arbitrary", "arbitrary")),
        name=name,
    )(g, x)


def _lead_out_kernel(n_extra, g_ref, w_ref, *rest):
    g = g_ref[...]
    o_ref = rest[-1]
    ys = [jnp.dot(g, _unpack_pairs(w_ref[j]), preferred_element_type=F32) for j in range(w_ref.shape[0])]
    y = jnp.swapaxes(jnp.stack(ys, axis=0), 0, 1)
    if n_extra:
        y = (y + rest[0][...]) * rest[1][...]
    o_ref[...] = y


def _lead_out(g, w, extras, name):
    nb, nk, kf, k2, c = w.shape
    blk = pl.BlockSpec((None, FFT_R, kf, c), lambda b, k: (b, 0, k, 0))
    return pl.pallas_call(
        functools.partial(_lead_out_kernel, len(extras)),
        out_shape=jax.ShapeDtypeStruct((nb, FFT_R, FFT_R, c), F32),
        grid=(nb, nk),
        in_specs=[pl.BlockSpec(g.shape, lambda b, k: (0, 0)),
                  pl.BlockSpec((None, None, kf, k2, c), lambda b, k: (b, k, 0, 0, 0))] + [blk] * len(extras),
        out_specs=blk,
        compiler_params=_cparams(("arbitrary", "arbitrary")),
        name=name,
    )(g, w, *extras)


def _interleave(a, b, axis):
    st = np.stack([a, b], axis=axis + 1)
    shape = list(a.shape)
    shape[axis] *= 2
    return st.reshape(shape)


def _hy2_tables():
    L = FFT_R * FFT_R
    n2 = 2 * L
    f1 = np.arange(2 * FFT_R, dtype=np.int64)
    s1 = np.arange(FFT_R, dtype=np.int64)
    th = np.pi * (((2 * f1[:, None] + 1) * s1[None, :]) % (4 * FFT_R)) / (2 * FFT_R)
    ga = _interleave(np.cos(th), -np.sin(th), 0)
    ma = _interleave(np.cos(th).T, -np.sin(th).T, 1) / L
    f2 = np.arange(FFT_R // 2, dtype=np.int64)
    s2 = np.arange(FFT_R, dtype=np.int64)
    idx = ((n2 // FFT_R) * 2 * f2[None, :, None] * s2[None, None, :]
           + (2 * f1[:, None, None] + 1) * s2[None, None, :]) % (2 * n2)
    al = np.pi * idx / n2
    c, s = np.cos(al), np.sin(al)
    nmat = np.concatenate([_interleave(c, s, 2), _interleave(-s, c, 2)], axis=1)
    ct, st = np.transpose(c, (0, 2, 1)), np.transpose(s, (0, 2, 1))
    mmat = _interleave(np.concatenate([ct, -st], axis=2), np.concatenate([st, ct], axis=2), 1)
    bf = lambda a: jnp.asarray(a, F32).astype(BF16)
    return bf(ga), bf(nmat), bf(mmat), bf(ma)


def _hy_mid_kernel(a_ref, n_ref, m_ref, k_ref, e_ref):
    half = FFT_R // 2
    kf = a_ref.shape[-2]
    a = _to_blocks(a_ref[...])
    ts = [jnp.dot(n_ref[j], _unpack_pairs(a[j]), preferred_element_type=F32)
          for j in range(kf)]
    ys = []
    for j in range(kf):
        tr, ti = ts[j][:half], ts[j][half:]
        kr, ki = k_ref[j, 0], k_ref[j, 1]
        ys.append(jnp.concatenate([tr * kr - ti * ki, tr * ki + ti * kr], axis=0).astype(BF16))
    e_ref[...] = _from_blocks([_pack_pairs(jnp.dot(m_ref[j], ys[j], preferred_element_type=F32))
                               for j in range(kf)])


def _hy_mid(a, nmat, mmat, khat):
    nb, nk, kf, _, c = a.shape
    nf1 = 2 * FFT_R
    blk = pl.BlockSpec((None, nk, kf, kf, c), lambda i, b: (b, 0, 0, i, 0))
    return pl.pallas_call(
        _hy_mid_kernel,
        out_shape=jax.ShapeDtypeStruct(a.shape, jnp.uint32),
        grid=(nf1 // kf, nb),
        in_specs=[blk,
                  pl.BlockSpec((kf, FFT_R, 2 * FFT_R), lambda i, b: (i, 0, 0)),
                  pl.BlockSpec((kf, 2 * FFT_R, FFT_R), lambda i, b: (i, 0, 0)),
                  pl.BlockSpec((kf, 2, FFT_R // 2, c), lambda i, b: (i, 0, 0, 0))],
        out_specs=blk,
        compiler_params=_cparams(("arbitrary", "arbitrary")),
        name="l0_hyena_mid",
    )(a, nmat, mmat, khat)


def _hy_kfilt_kernel(a_ref, n_ref, nrm_ref, k_ref):
    half = FFT_R // 2
    nrm = nrm_ref[...]
    af, ab = _to_blocks(a_ref[0]), _to_blocks(a_ref[1])
    for j in range(a_ref.shape[-2]):
        tf = jnp.dot(n_ref[j], _unpack_pairs(af[j]), preferred_element_type=F32)
        tb = jnp.dot(n_ref[j], _unpack_pairs(ab[j]), preferred_element_type=F32)
        k_ref[j, 0] = (tf[:half] + tb[:half]) / nrm
        k_ref[j, 1] = (tf[half:] - tb[half:]) / nrm


def _hy_kfilt(a, nmat, nrm):
    _, nk, kf, _, c = a.shape
    nf1 = 2 * FFT_R
    return pl.pallas_call(
        _hy_kfilt_kernel,
        out_shape=jax.ShapeDtypeStruct((nf1, 2, FFT_R // 2, c), F32),
        grid=(nf1 // kf,),
        in_specs=[pl.BlockSpec((2, nk, kf, kf, c), lambda i: (0, 0, 0, i, 0)),
                  pl.BlockSpec((kf, FFT_R, 2 * FFT_R), lambda i: (i, 0, 0)),
                  _const_spec(nrm.shape)],
        out_specs=pl.BlockSpec((kf, 2, FFT_R // 2, c), lambda i: (i, 0, 0, 0)),
        compiler_params=_cparams(("arbitrary",)),
        name="l0_hyena_kfilt",
    )(a, nmat, nrm)


def _hyena_long(u, e, x0, hfilt, nrm, nb):
    L = FFT_R * FFT_R
    c = u.shape[-1]
    v4 = lambda a, n: a.reshape(n, FFT_R, FFT_R, c)
    ga, nmat, mmat, ma = _hy2_tables()
    khat = _hy_kfilt(_lead_in(ga, v4(hfilt, 2), HYENA_KF, "l0_hyena_fwd_a"), nmat, nrm)
    ee = _hy_mid(_lead_in(ga, v4(u, nb), HYENA_KF, "l0_hyena_fwd_a"), nmat, mmat, khat)
    y = _lead_out(ma, ee, [v4(e, nb), v4(x0, nb)], "l0_hyena_inv_a")
    return y.reshape(nb * L, c)


def _fn2_tables():
    L = FFT_R * FFT_R
    r = np.arange(FFT_R, dtype=np.int64)
    idx = (FFT_R * r[None, :, None] * r[None, None, :] + r[None, :, None] * r[:, None, None]) % L
    gm = 2.0 * np.pi * idx / L
    c, s = np.cos(gm), np.sin(gm)
    g1 = _interleave(np.concatenate([c, -s], axis=2), np.concatenate([-s, -c], axis=2), 1)
    dl = 2.0 * np.pi * ((r[:, None] * r[None, :]) % FFT_R) / FFT_R
    g2 = _interleave(np.cos(dl), np.sin(dl), 1) / math.sqrt(L * FN_GROUP_CH)
    bf = lambda a: jnp.asarray(a, F32).astype(BF16)
    return bf(g1), bf(g2)


def _fnet_s1_kernel(x_ref, m_ref, cs_ref, g1_ref, o_ref, zc_ref, zs_ref):
    xs = jnp.swapaxes(x_ref[...], 0, 1).reshape(FNET_KF * FFT_R, D_MODEL)
    m = m_ref[...]
    h = (_ln_plain(xs) * (1.0 + m[:, D_MODEL:2 * D_MODEL]) + m[:, 0:D_MODEL]).astype(BF16)
    cs = cs_ref[...]
    for g in range(D_MODEL // FN_GROUP_CH):
        a = g * FN_GROUP_CH
        z = jnp.dot(h[:, a:a + FN_GROUP_CH], cs, preferred_element_type=F32)
        zc_ref[:, a:a + FN_GROUP_CH] = z[:, :FN_GROUP_CH].astype(BF16)
        zs_ref[:, a:a + FN_GROUP_CH] = z[:, FN_GROUP_CH:].astype(BF16)
    ws = []
    for j in range(FNET_KF):
        r0 = j * FFT_R
        s = jnp.concatenate([zc_ref[r0:r0 + FFT_R, :], zs_ref[r0:r0 + FFT_R, :]], axis=0)
        ws.append(_pack_pairs(jnp.dot(g1_ref[j], s, preferred_element_type=F32)))
    o_ref[...] = _from_blocks(ws)


def _fnet_long(x, mods, mod_base, nb):
    L = FFT_R * FFT_R
    d = D_MODEL
    g1, g2 = _fn2_tables()
    cs = _group_dft_table()
    kf = FNET_KF
    bb = pl.pallas_call(
        _fnet_s1_kernel,
        out_shape=jax.ShapeDtypeStruct((nb, FFT_R // kf, kf, FFT_R, d), jnp.uint32),
        grid=(nb, FFT_R // kf),
        in_specs=[pl.BlockSpec((None, FFT_R, kf, d), lambda b, k: (b, 0, k, 0)),
                  pl.BlockSpec((None, 1, 6 * d), lambda b, k: (mod_base + b, 0, 0)),
                  pl.BlockSpec(cs.shape, lambda b, k: (0, 0)),
                  pl.BlockSpec((kf, 2 * FFT_R, 2 * FFT_R), lambda b, k: (k, 0, 0))],
        out_specs=pl.BlockSpec((None, FFT_R // kf, kf, kf, d), lambda b, k: (b, 0, 0, k, 0)),
        scratch_shapes=[pltpu.VMEM((kf * FFT_R, d), BF16), pltpu.VMEM((kf * FFT_R, d), BF16)],
        compiler_params=_cparams(("arbitrary", "arbitrary")),
        name="l1_fnet_stage1",
    )(x.reshape(nb, FFT_R, FFT_R, d), mods, cs, g1)
    y = _lead_out(g2, bb, [], "l1_fnet_stage2")
    return y.reshape(nb * L, d)


def _group_dft_table():
    g = FN_GROUP_CH
    jk = (np.arange(g, dtype=np.int64)[:, None] * np.arange(g, dtype=np.int64)[None, :]) % g
    ang = 2.0 * np.pi * jk / g
    return jnp.asarray(np.concatenate([np.cos(ang), np.sin(ang)], axis=1), F32).astype(BF16)


def _fnet_front_kernel(x_ref, m_ref, cs_ref, zc_ref, zs_ref):
    m = m_ref[...]
    h = (_ln_plain(x_ref[...]) * (1.0 + m[:, D_MODEL:2 * D_MODEL]) + m[:, 0:D_MODEL]).astype(BF16)
    cs = cs_ref[...]
    for g in range(D_MODEL // FN_GROUP_CH):
        a = g * FN_GROUP_CH
        z = jnp.dot(h[:, a:a + FN_GROUP_CH], cs, preferred_element_type=F32)
        zc_ref[:, a:a + FN_GROUP_CH] = z[:, :FN_GROUP_CH].astype(BF16)
        zs_ref[:, a:a + FN_GROUP_CH] = z[:, FN_GROUP_CH:].astype(BF16)


def _fnet_front(x, mods, mod_base, tiles_per_mod):
    t = x.shape[0]
    tm = ROW_TILE
    cs = _group_dft_table()
    return pl.pallas_call(
        _fnet_front_kernel,
        out_shape=(jax.ShapeDtypeStruct((t, D_MODEL), BF16),) * 2,
        grid=(t // tm,),
        in_specs=[pl.BlockSpec((tm, D_MODEL), lambda i: (i, 0)),
                  _mod_spec(mod_base, tiles_per_mod),
                  _const_spec(cs.shape)],
        out_specs=(pl.BlockSpec((tm, D_MODEL), lambda i: (i, 0)),) * 2,
        compiler_params=_cparams(("arbitrary",)),
        name="l1_fnet_front",
    )(x, mods, cs)


def _post_kernel(n_a, *refs):
    x_ref, m_ref = refs[0], refs[1]
    a_refs = refs[2:2 + n_a]
    wo_refs = refs[2 + n_a:2 + 2 * n_a]
    g1_ref, b1_ref, w1_ref, w2_ref, g2_ref, b2_ref, o_ref = refs[2 + 2 * n_a:]
    m = m_ref[...]
    d = D_MODEL
    tm = x_ref.shape[0]
    halves = [(r, r + tm // POST_SPLIT) for r in range(0, tm, tm // POST_SPLIT)]
    outs = []
    for r0, r1 in halves:
        out = _bdot(a_refs[0][r0:r1, :], wo_refs[0][...])
        for a_ref, wo_ref in zip(a_refs[1:], wo_refs[1:]):
            out += _bdot(a_ref[r0:r1, :], wo_ref[...])
        outs.append(out)
    x1s, hs = [], []
    for (r0, r1), out in zip(halves, outs):
        x1 = _ln_plain(ALPHA * x_ref[r0:r1, :] + m[:, 2 * d:3 * d] * out) * g1_ref[...] + b1_ref[...]
        x1s.append(x1)
        hs.append((_ln_plain(x1) * (1.0 + m[:, 4 * d:5 * d]) + m[:, 3 * d:4 * d]).astype(BF16))
    accs = []
    n_c = D_FF // d

    def up(h, c):
        hc = jnp.maximum(jnp.dot(h, w1_ref[:, c * d:(c + 1) * d], preferred_element_type=F32), 0.0)
        return (hc * hc).astype(BF16)

    for h in hs:
        acc = None
        nxt = up(h, 0)
        for c in range(n_c):
            cur = nxt
            if c + 1 < n_c:
                nxt = up(h, c + 1)
            part = jnp.dot(cur, w2_ref[c * d:(c + 1) * d, :], preferred_element_type=F32)
            acc = part if acc is None else acc + part
        accs.append(acc)
    for (r0, r1), x1, acc in zip(halves, x1s, accs):
        o_ref[r0:r1, :] = _ln_plain(ALPHA * x1 + m[:, 5 * d:6 * d] * acc) * g2_ref[...] + b2_ref[...]


def _post(x, mods, mod_base, tiles_per_mod, a_list, wo_list, g1, b1, w1, w2, g2, b2):
    t = x.shape[0]
    tm = POST_TILE
    row = lambda c: pl.BlockSpec((tm, c), lambda i: (i, 0))
    once = lambda v: pl.BlockSpec(v.shape, lambda i: (0,) * v.ndim, pipeline_mode=pl.Buffered(1))
    in_specs = ([row(D_MODEL), _mod_spec(mod_base, tiles_per_mod * ROW_TILE // tm)]
                + [row(a.shape[1]) for a in a_list]
                + [once(w) for w in wo_list]
                + [once(v) for v in (g1, b1, w1, w2, g2, b2)])
    return pl.pallas_call(
        functools.partial(_post_kernel, len(a_list)),
        out_shape=jax.ShapeDtypeStruct((t, D_MODEL), F32),
        grid=(t // tm,),
        in_specs=in_specs,
        out_specs=row(D_MODEL),
        compiler_params=_cparams(("arbitrary",)),
        name="post_mlp",
    )(x, mods, *a_list, *wo_list, g1, b1, w1, w2, g2, b2)


def _rot_cols(w):
    parts = []
    for seg in range(2):
        o = seg * 32
        parts += [-w[:, o + 16:o + 32], w[:, o:o + 16]]
    return jnp.concatenate(parts, axis=1)


def _pad_cols(w, n):
    return jnp.pad(w, ((0, 0), (0, n - w.shape[1])))


def _block_diag2(w):
    z = jnp.zeros_like(w)
    return jnp.concatenate([jnp.concatenate([w, z], axis=1), jnp.concatenate([z, w], axis=1)], axis=0)


def _rope_tables(L):
    rows = L // GRID_W
    row = np.repeat(np.arange(rows, dtype=np.float64), GRID_W)
    col = np.tile(np.arange(GRID_W, dtype=np.float64), rows)
    half = QK_ROPE // 2
    inv = 1.0 / (ROPE_THETA ** (np.arange(0, half, 2, dtype=np.float64) / half))
    ar = row[:, None] * inv[None, :]
    ac = col[:, None] * inv[None, :]
    ang = np.concatenate([ar, ar, ac, ac], axis=1)
    cos = np.concatenate([np.cos(ang), np.ones_like(ang)], axis=1)
    sin = np.concatenate([np.sin(ang), np.zeros_like(ang)], axis=1)
    return jnp.asarray(cos, F32), jnp.asarray(sin, F32)


def kernel(x_prompt, x_sample, cache_l0_ckv, cache_l0_krope, c, c_ctx, l0_ada_w, l0_ada_b, l0_w_in, l0_conv_w, l0_conv_b, l0_hf_w1, l0_hf_b1, l0_hf_freq, l0_hf_w2, l0_hf_b2, l0_hf_w3, l0_hf_skip, l0_q_norm, l0_q_up, l0_kv_norm, l0_kv_up, l0_w_out, l0_ln1_g, l0_ln1_b, l0_mlp_w1, l0_mlp_w2, l0_ln2_g, l0_ln2_b, l1_ada_w, l1_ada_b, l1_w_out, l1_ln1_g, l1_ln1_b, l1_mlp_w1, l1_mlp_w2, l1_ln2_g, l1_ln2_b):
    nbc, lc, d = x_prompt.shape
    nbs, ls, _ = x_sample.shape
    past = cache_l0_ckv.shape[1]
    tm = ROW_TILE
    row1 = lambda v: v.reshape(1, -1)

    cond8 = jnp.concatenate([c_ctx[None, :], c, jnp.zeros((8 - 1 - nbs, d), F32)], axis=0)
    mods0 = _modulation(cond8, l0_ada_w, l0_ada_b)
    mods1 = _modulation(cond8, l1_ada_w, l1_ada_b)

    kpe_w = l0_w_in[:, 1920:1984]
    win = jnp.concatenate([l0_w_in[:, :1920], _pad_cols(kpe_w, LANE), _pad_cols(_rot_cols(kpe_w), LANE)],
                          axis=1).astype(BF16)
    dh = QK_NOPE + QK_ROPE
    q_nope = [l0_q_up[:, h * dh:h * dh + QK_NOPE] for h in range(MLA_HEADS)]
    q_pe = [l0_q_up[:, h * dh + QK_NOPE:(h + 1) * dh] for h in range(MLA_HEADS)]
    qup = jnp.concatenate(q_nope + [_pad_cols(w, LANE) for w in q_pe]
                          + [_pad_cols(_rot_cols(w), LANE) for w in q_pe], axis=1).astype(BF16)
    kvup = l0_kv_up.astype(BF16)
    front_w = (win, row1(l0_q_norm), qup, row1(l0_kv_norm), kvup, l0_conv_w, row1(l0_conv_b), row1(l0_hf_skip))
    w1p = jnp.pad(l0_hf_w1, ((0, LANE - l0_hf_w1.shape[0]), (0, 0)))
    two = lambda v: jnp.tile(row1(v), (1, 2))
    filt_w = (_block_diag2(w1p), two(l0_hf_b1), two(l0_hf_freq), _block_diag2(l0_hf_w2), two(l0_hf_b2),
              _block_diag2(l0_hf_w3).astype(BF16))
    wo0 = l0_w_out.astype(BF16)

    xc = x_prompt.reshape(nbc * lc, d)
    xs = x_sample.reshape(nbs * ls, d)
    groups = (
        dict(x=xc, nb=nbc, L=lc, mod_base=0, tiles_per_mod=nbc * lc // tm, dft_nb=4, tq=lc, hps=MLA_HEADS),
        dict(x=xs, nb=nbs, L=ls, mod_base=1, tiles_per_mod=ls // tm, dft_nb=nbs, tq=512, hps=1),
    )
    ones_tab = (jnp.concatenate([jnp.ones((tm, LANE), F32)], axis=0), jnp.zeros((tm, LANE), F32))

    outs = []
    ctx_ckv = ctx_krope = None
    for gi, g in enumerate(groups):
        nb, L = g["nb"], g["L"]
        tiles_per_seq = L // tm
        latent = gi == 1
        cos, sin = _rope_tables(L) if latent else ones_tab
        two_stage = L == FFT_R * FFT_R
        io_dtype = F32 if two_stage else BF16
        u, e, x0, q, k, vt, kvn, kpe = _front(g["x"], mods0, g["mod_base"], g["tiles_per_mod"] * tm, front_w,
                                              cos, sin, latent, L, min(L, FRONT_TILE), io_dtype)
        if latent:
            extra = _cache_kv(cache_l0_ckv.reshape(nbs * past, KV_LORA),
                              _pad_cols(cache_l0_krope.reshape(nbs * past, QK_ROPE), LANE), kvup)
        else:
            extra = None
            ctx_ckv = kvn.reshape(nb, L, KV_LORA)
            ctx_krope = kpe.reshape(nb, L, QK_ROPE)
        y_mla = _attention(q, k, vt, extra, nb, L, g["tq"], g["hps"])

        hfilt, hnorm = _filters(L, io_dtype, *filt_w)
        if two_stage:
            y_hy = _hyena_long(u, e, x0, hfilt, hnorm, nb)
        else:
            kre, kim = _dft("filt", "hy_fwd", [hfilt], [hnorm], 2)
            sh = (nb, L, HY_CH)
            yre, yim = _dft("fwdk", "hy_fwd", [u.reshape(sh)], [kre, kim], g["dft_nb"])
            y_hy = _dft("inv", "hy_inv", [yre, yim], [e.reshape(sh), x0.reshape(sh)], g["dft_nb"])
            y_hy = y_hy.reshape(nb * L, HY_CH)

        x1 = _post(g["x"], mods0, g["mod_base"], g["tiles_per_mod"], [y_hy, y_mla],
                   [wo0[:HY_CH], wo0[HY_CH:]], row1(l0_ln1_g), row1(l0_ln1_b),
                   l0_mlp_w1.astype(BF16), l0_mlp_w2.astype(BF16), row1(l0_ln2_g), row1(l0_ln2_b))

        if L == FFT_R * FFT_R:
            yf = _fnet_long(x1, mods1, g["mod_base"], nb)
        else:
            zc, zs = _fnet_front(x1, mods1, g["mod_base"], g["tiles_per_mod"])
            sh = (nb, L, d)
            yf = _dft("fnet", "fnet", [zc.reshape(sh), zs.reshape(sh)], [], min(g["dft_nb"], 2))
            yf = yf.reshape(nb * L, d)
        x2 = _post(x1, mods1, g["mod_base"], g["tiles_per_mod"], [yf],
                   [l1_w_out.astype(BF16)], row1(l1_ln1_g), row1(l1_ln1_b),
                   l1_mlp_w1.astype(BF16), l1_mlp_w2.astype(BF16), row1(l1_ln2_g), row1(l1_ln2_b))
        outs.append(x2.reshape(nb, L, d))

    return (outs[0], outs[1], ctx_ckv, ctx_krope)
```

```python
import functools
import math

import numpy as np
import jax
import jax.numpy as jnp
from jax import lax
from jax.experimental import pallas as pl
from jax.experimental.pallas import tpu as pltpu

F32 = jnp.float32
BF16 = jnp.bfloat16
HI = lax.Precision.HIGHEST

D_MODEL = 1024
DEPTH = 2
GRID_W = 64
HY_CH = 512
FILT_BANDS = 16
FILT_ORDER = 64
FAST_DECAY_PCT = 0.3
SLOW_DECAY_PCT = 1.5
DECAY_TARGET = 1e-2
MAX_DECAY = math.log(DECAY_TARGET) / FAST_DECAY_PCT
MIN_DECAY = math.log(DECAY_TARGET) / SLOW_DECAY_PCT
MLA_HEADS = 4
QK_NOPE = 128
QK_ROPE = 64
V_HEAD = 128
Q_LORA = 256
KV_LORA = 128
ROPE_THETA = 10000.0
FN_GROUP_CH = 128
D_FF = 4096
ALPHA = (2 * DEPTH) ** 0.25
LN_EPS = 1e-5
RMS_EPS = 1e-6

LANE = 128
ROW_TILE = 256
FRONT_TILE = 512
POST_TILE = 512
POST_SPLIT = 2
QK_PAD = 256
VT_ROWS = V_HEAD + 16
LOG2E = 1.4426950408889634
VMEM_LIMIT = 56 * 1024 * 1024


def _cparams(sem):
    return pltpu.CompilerParams(dimension_semantics=sem, vmem_limit_bytes=VMEM_LIMIT)


def _ln_plain(x):
    mu = jnp.mean(x, axis=-1, keepdims=True)
    xc = x - mu
    var = jnp.mean(xc * xc, axis=-1, keepdims=True)
    return xc * lax.rsqrt(var + LN_EPS)


def _rms(x, g):
    return x * lax.rsqrt(jnp.mean(x * x, axis=-1, keepdims=True) + RMS_EPS) * g


def _bdot(a, b):
    return jnp.dot(a.astype(BF16), b, preferred_element_type=F32)


def _vt_rows(v):
    ones = jnp.ones((VT_ROWS - V_HEAD, v.shape[0]), BF16)
    return jnp.concatenate([jnp.transpose(v).astype(BF16), ones], axis=0)


MOD_STREAMS = 3


def _mod_kernel(c_ref, *refs):
    w_refs, b_ref, o_ref = refs[:MOD_STREAMS], refs[MOD_STREAMS], refs[MOD_STREAMS + 1]
    c = c_ref[...]
    s = c / (1.0 + jnp.exp(-c))
    s_hi = s.astype(BF16)
    s_lo = (s - s_hi.astype(F32)).astype(BF16)
    s2 = jnp.concatenate([s_hi, s_lo], axis=0)
    nr = s.shape[0]
    tn = w_refs[0].shape[1]
    for k, w_ref in enumerate(w_refs):
        w = w_ref[...]
        w_hi = w.astype(BF16)
        w_lo = (w - w_hi.astype(F32)).astype(BF16)
        r1 = jnp.dot(s2, w_hi, preferred_element_type=F32)
        r2 = jnp.dot(s_hi, w_lo, preferred_element_type=F32)
        o_ref[:, k * tn:(k + 1) * tn] = r1[:nr] + r1[nr:] + r2 + b_ref[:, k * tn:(k + 1) * tn]


def _modulation(cond8, w, b):
    n = w.shape[1]
    tn = 512
    step = MOD_STREAMS * tn
    w_specs = [pl.BlockSpec((D_MODEL, tn), functools.partial(lambda j, k: (0, MOD_STREAMS * j + k), k=k))
               for k in range(MOD_STREAMS)]
    out = pl.pallas_call(
        _mod_kernel,
        out_shape=jax.ShapeDtypeStruct((8, n), F32),
        grid=(n // step,),
        in_specs=[pl.BlockSpec((8, D_MODEL), lambda j: (0, 0))] + w_specs
                 + [pl.BlockSpec((1, step), lambda j: (0, j))],
        out_specs=pl.BlockSpec((8, step), lambda j: (0, j)),
        compiler_params=_cparams(("arbitrary",)),
        name="modulation",
    )(cond8, *([w] * MOD_STREAMS), b.reshape(1, n))
    return out.reshape(8, 1, n)


def _mod_spec(mod_base, tiles_per_mod):
    return pl.BlockSpec((None, 1, 6 * D_MODEL), lambda i: (mod_base + i // tiles_per_mod, 0, 0))


def _const_spec(shape):
    nd = len(shape)
    return pl.BlockSpec(shape, lambda i: (0,) * nd)


HALO = 8


def _front_kernel(seq_len, x_ref, xp_ref, xn_ref, m_ref, win_ref, qn_ref, qup_ref, kvn_ref, kvup_ref,
                  cos_ref, sin_ref, cw_ref, cb_ref, skip_ref,
                  u_ref, e_ref, x0_ref, q_ref, k_ref, v_ref, kvn_out_ref, kpe_ref):
    i = pl.program_id(0)
    m = m_ref[...]
    tm = x_ref.shape[0]
    nh = 3 * HY_CH
    xe = jnp.concatenate([xp_ref[...], x_ref[...], xn_ref[...]], axis=0)
    he = _ln_plain(xe) * (1.0 + m[:, D_MODEL:2 * D_MODEL]) + m[:, 0:D_MODEL]

    z = _bdot(he[HALO:HALO + tm], win_ref[:, nh:])
    zh = _bdot(he, win_ref[:, :nh])
    q_c = z[:, 0:256]
    kv_c = z[:, 256:384]
    cos = cos_ref[...]
    sin = sin_ref[...]
    kpe = z[:, 384:512] * cos + z[:, 512:640] * sin
    kpe_ref[...] = kpe[:, :QK_ROPE]
    kpe_b = kpe.astype(BF16)
    q = _bdot(_rms(q_c, qn_ref[...]), qup_ref[...]) * (LOG2E / math.sqrt(QK_NOPE + QK_ROPE))
    kvn = _rms(kv_c, kvn_ref[...])
    kvn_out_ref[...] = kvn
    kv = _bdot(kvn, kvup_ref[...])

    rows = lax.broadcasted_iota(jnp.int32, (tm + 2 * HALO, 1), 0)
    in_seq = (i * tm + rows + (seq_len - HALO)) & (seq_len - 1)
    prev_tap = jnp.where(in_seq == 0, 0.0, pltpu.roll(zh, 1, 0))
    next_tap = jnp.where(in_seq == seq_len - 1, 0.0, pltpu.roll(zh, tm + 2 * HALO - 1, 0))
    cw = cw_ref[...]
    pz = (prev_tap * cw[0:1, :] + zh * cw[1:2, :] + next_tap * cw[2:3, :])[HALO:HALO + tm] + cb_ref[...]
    u = pz[:, 2 * HY_CH:] * pz[:, HY_CH:2 * HY_CH]
    u_ref[...] = u.astype(u_ref.dtype)
    e_ref[...] = u * skip_ref[...]
    x0_ref[...] = pz[:, :HY_CH]

    for hd in range(MLA_HEADS):
        a = hd * LANE
        q_pe = (q[:, 512 + a:512 + a + LANE] * cos + q[:, 1024 + a:1024 + a + LANE] * sin).astype(BF16)
        q_ref[hd] = jnp.concatenate([q[:, a:a + LANE].astype(BF16), q_pe], axis=-1)
        k_ref[hd] = jnp.concatenate([kv[:, 2 * a:2 * a + LANE].astype(BF16), kpe_b], axis=-1)
        v_ref[hd] = _vt_rows(kv[:, 2 * a + LANE:2 * a + 2 * LANE])


def _front(x, mods, mod_base, rows_per_mod, w, cos, sin, rope, seq_len, tm, u_dtype):
    t = x.shape[0]
    tiles_per_seq = seq_len // tm
    tiles_per_mod = rows_per_mod // tm
    win, qn, qup, kvn, kvup, conv_w, conv_b, skip = w
    if rope:
        tab_spec = pl.BlockSpec((tm, LANE), lambda i: (i % tiles_per_seq, 0))
    else:
        tab_spec = pl.BlockSpec((tm, LANE), lambda i: (0, 0))
    r8 = tm // HALO
    n8 = t // HALO
    hy_out = lambda dt: jax.ShapeDtypeStruct((t, HY_CH), dt)
    hy_spec = pl.BlockSpec((tm, HY_CH), lambda i: (i, 0))
    return pl.pallas_call(
        functools.partial(_front_kernel, seq_len),
        out_shape=(hy_out(u_dtype), hy_out(F32), hy_out(F32),
                   jax.ShapeDtypeStruct((MLA_HEADS, t, QK_PAD), BF16),
                   jax.ShapeDtypeStruct((MLA_HEADS, t, QK_PAD), BF16),
                   jax.ShapeDtypeStruct((MLA_HEADS, VT_ROWS, t), BF16),
                   jax.ShapeDtypeStruct((t, KV_LORA), F32),
                   jax.ShapeDtypeStruct((t, QK_ROPE), F32)),
        grid=(t // tm,),
        in_specs=[pl.BlockSpec((tm, D_MODEL), lambda i: (i, 0)),
                  pl.BlockSpec((HALO, D_MODEL), lambda i: (jnp.maximum(i * r8 - 1, 0), 0)),
                  pl.BlockSpec((HALO, D_MODEL), lambda i: (jnp.minimum((i + 1) * r8, n8 - 1), 0)),
                  _mod_spec(mod_base, tiles_per_mod),
                  _const_spec(win.shape), _const_spec(qn.shape), _const_spec(qup.shape),
                  _const_spec(kvn.shape), _const_spec(kvup.shape),
                  tab_spec, tab_spec,
                  _const_spec(conv_w.shape), _const_spec(conv_b.shape), _const_spec(skip.shape)],
        out_specs=(hy_spec, hy_spec, hy_spec,
                   pl.BlockSpec((MLA_HEADS, tm, QK_PAD), lambda i: (0, i, 0)),
                   pl.BlockSpec((MLA_HEADS, tm, QK_PAD), lambda i: (0, i, 0)),
                   pl.BlockSpec((MLA_HEADS, VT_ROWS, tm), lambda i: (0, 0, i)),
                   pl.BlockSpec((tm, KV_LORA), lambda i: (i, 0)),
                   pl.BlockSpec((tm, QK_ROPE), lambda i: (i, 0))),
        compiler_params=_cparams(("arbitrary",)),
        name="l0_front",
    )(x, x, x, mods, win, qn, qup, kvn, kvup, cos, sin, conv_w, conv_b, skip)


def _cache_kv_kernel(ckv_ref, kr_ref, kvup_ref, k_ref, v_ref):
    kv = _bdot(ckv_ref[...], kvup_ref[...])
    kr = kr_ref[...].astype(BF16)
    for hd in range(MLA_HEADS):
        a = 2 * hd * LANE
        k_ref[hd] = jnp.concatenate([kv[:, a:a + LANE].astype(BF16), kr], axis=-1)
        v_ref[hd] = _vt_rows(kv[:, a + LANE:a + 2 * LANE])


def _cache_kv(ckv, krope_pad, kvup):
    t = ckv.shape[0]
    return pl.pallas_call(
        _cache_kv_kernel,
        out_shape=(jax.ShapeDtypeStruct((MLA_HEADS, t, QK_PAD), BF16),
                   jax.ShapeDtypeStruct((MLA_HEADS, VT_ROWS, t), BF16)),
        name="l0_cache_kv",
    )(ckv, krope_pad, kvup)


def _col_reduce(x, op):
    rows, n = x.shape
    for g in (32, 8):
        if rows % (8 * g) == 0 and rows > 8 * g:
            x = op(x.reshape(rows // (8 * g), 8 * g, n), axis=0)
            rows = 8 * g
    return op(x, axis=0, keepdims=True)


def _attn_kernel(n_kv, hps, q_ref, *refs):
    k_refs, vt_refs = refs[:n_kv], refs[n_kv:2 * n_kv]
    o_ref, s_even, s_odd, m_even, m_odd = refs[2 * n_kv:]
    i = pl.program_id(0)

    @pl.when(i == 0)
    def _():
        s_odd[...] = jnp.zeros_like(s_odd)
        m_odd[...] = jnp.zeros_like(m_odd)

    def step(s_write, m_write, s_read, m_read):
        nt = (((1,), (1,)), ((), ()))
        for h in range(hps):
            q = q_ref[h]
            r0 = 0
            m = None
            for k_ref in k_refs:
                lk = k_ref.shape[1]
                sblk = lax.dot_general(k_ref[h], q, nt, preferred_element_type=F32)
                s_write[h, r0:r0 + lk, :] = sblk
                mc = _col_reduce(sblk, jnp.max)
                m = mc if m is None else jnp.maximum(m, mc)
                r0 += lk
            m_write[h] = m
        for h in range(hps):
            pb = jnp.exp2(s_read[h] - m_read[h]).astype(BF16)
            acc = None
            r0 = 0
            for vt_ref in vt_refs:
                lk = vt_ref.shape[2]
                pv = jnp.dot(vt_ref[h], pb[r0:r0 + lk, :], preferred_element_type=F32)
                acc = pv if acc is None else acc + pv
                r0 += lk
            o_ref[:, h * V_HEAD:(h + 1) * V_HEAD] = jnp.transpose(
                acc[:V_HEAD] / acc[V_HEAD:V_HEAD + 1]).astype(o_ref.dtype)

    pl.when(i % 2 == 0)(lambda: step(s_even, m_even, s_odd, m_odd))
    pl.when(i % 2 == 1)(lambda: step(s_odd, m_odd, s_even, m_even))


def _attention(q, k, vt, extra, nb, lq, tq, hps):
    nq = lq // tq
    ng = MLA_HEADS // hps
    n_tiles = nb * ng * nq

    def where(t):
        bh = t // nq
        return bh // ng, bh % ng, t % nq

    def score_side(fn):
        return lambda i: fn(*where(jnp.minimum(i, n_tiles - 1)))

    def value_side(fn):
        return lambda i: fn(*where(jnp.maximum(i - 1, 0)))

    ks, vts = [k], [vt]
    if extra is not None:
        ks.append(extra[0])
        vts.append(extra[1])
    in_specs = [pl.BlockSpec((hps, tq, QK_PAD), score_side(lambda b, h, j: (h, b * nq + j, 0)))]
    in_specs += [pl.BlockSpec((hps, a.shape[1] // nb, QK_PAD), score_side(lambda b, h, j: (h, b, 0))) for a in ks]
    in_specs += [pl.BlockSpec((hps, VT_ROWS, a.shape[2] // nb), value_side(lambda b, h, j: (h, 0, b))) for a in vts]
    lk_total = sum(a.shape[1] // nb for a in ks)
    return pl.pallas_call(
        functools.partial(_attn_kernel, len(ks), hps),
        out_shape=jax.ShapeDtypeStruct((nb * lq, MLA_HEADS * V_HEAD), BF16),
        grid=(n_tiles + 1,),
        in_specs=in_specs,
        out_specs=pl.BlockSpec((tq, hps * V_HEAD), value_side(lambda b, h, j: (b * nq + j, h))),
        scratch_shapes=[pltpu.VMEM((hps, lk_total, tq), F32), pltpu.VMEM((hps, lk_total, tq), F32),
                        pltpu.VMEM((hps, 1, tq), F32), pltpu.VMEM((hps, 1, tq), F32)],
        compiler_params=_cparams(("arbitrary",)),
        name="l0_attention",
    )(q, *ks, *vts)


def _filter_kernel(z_ref, w1_ref, b1_ref, fr_ref, w2_ref, b2_ref, w3_ref, dl_ref, h_ref, norm_ref):
    i = pl.program_id(0)
    z = z_ref[...]
    tl = z.shape[0]
    fr = fr_ref[...]
    z2 = jnp.concatenate([z[:tl // 2], z[tl // 2:]], axis=1)
    h = jnp.sin(fr * (jnp.dot(z2, w1_ref[...], precision=HI, preferred_element_type=F32) + b1_ref[...]))
    h = jnp.sin(fr * (jnp.dot(h, w2_ref[...], precision=HI, preferred_element_type=F32) + b2_ref[...]))
    h = _bdot(h, w3_ref[...])
    h = jnp.concatenate([h[:, :2 * HY_CH], h[:, 2 * HY_CH:]], axis=0)
    decay = jnp.exp(-(z[:, 0:1] * dl_ref[...]))
    hf = h[:, :HY_CH] * decay
    hb = h[:, HY_CH:] * decay
    part = jnp.sum(jnp.abs(hf) + jnp.abs(hb), axis=0, keepdims=True)

    @pl.when(i == 0)
    def _():
        norm_ref[...] = part

    @pl.when(i > 0)
    def _():
        norm_ref[...] += part

    rows = lax.broadcasted_iota(jnp.int32, hb.shape, 0) + i * tl
    h_ref[0] = hf.astype(h_ref.dtype)
    h_ref[1] = jnp.where(rows == 0, 0.0, hb).astype(h_ref.dtype)


def _filter_embedding(L):
    t = np.linspace(0.0, 1.0, L)[:, None]
    w_ang = 2.0 * np.pi * np.arange(L) / L
    bands = np.linspace(1e-4, FILT_BANDS - 1, FILT_BANDS)
    ang = w_ang[:, None] * bands[None, :]
    z = np.zeros((L, LANE), np.float64)
    z[:, 0:1] = t
    z[:, 1:1 + FILT_BANDS] = np.cos(ang)
    z[:, 1 + FILT_BANDS:1 + 2 * FILT_BANDS] = -np.sin(ang)
    return jnp.asarray(z, F32)


def _filters(L, h_dtype, w1p, b1, fr, w2, b2, w3):
    tl = min(L, 512)
    z = _filter_embedding(L)
    deltas = jnp.asarray(np.abs(np.linspace(MIN_DECAY, MAX_DECAY, HY_CH))[None, :], F32)
    return pl.pallas_call(
        _filter_kernel,
        out_shape=(jax.ShapeDtypeStruct((2, L, HY_CH), h_dtype), jax.ShapeDtypeStruct((1, HY_CH), F32)),
        grid=(L // tl,),
        in_specs=[pl.BlockSpec((tl, LANE), lambda i: (i, 0)),
                  _const_spec(w1p.shape), _const_spec(b1.shape), _const_spec(fr.shape),
                  _const_spec(w2.shape), _const_spec(b2.shape), _const_spec(w3.shape),
                  _const_spec(deltas.shape)],
        out_specs=(pl.BlockSpec((2, tl, HY_CH), lambda i: (0, i, 0)),
                   pl.BlockSpec((1, HY_CH), lambda i: (0, 0))),
        compiler_params=_cparams(("arbitrary",)),
        name="l0_hyena_filters",
    )(z, w1p, b1, fr, w2, b2, w3, deltas)


def _dft_tables(kind, L, ti):
    ni = L // ti
    i = np.arange(ti, dtype=np.int64)[:, None]
    big = (np.arange(ni, dtype=np.int64) * ti)[:, None]
    c = np.arange(L, dtype=np.int64)[None, :]
    if kind == "hy_fwd":
        period = 4 * L
        base_idx = (2 * i + 1) * c
        r_idx = 2 * big * c
        scale = 1.0
    elif kind == "hy_inv":
        period = 4 * L
        base_idx = (2 * c + 1) * i
        r_idx = (2 * c + 1) * big
        scale = 1.0 / L
    else:
        period = L
        base_idx = i * c
        r_idx = big * c
        scale = 1.0 / math.sqrt(L * FN_GROUP_CH)
    ab = 2.0 * np.pi * (base_idx % period) / period
    ar = 2.0 * np.pi * (r_idx % period) / period
    return (jnp.asarray(np.cos(ab), F32), jnp.asarray(np.sin(ab), F32),
            jnp.asarray(scale * np.cos(ar), F32).reshape(ni, 1, L),
            jnp.asarray(scale * np.sin(ar), F32).reshape(ni, 1, L))


def _dft_kernel(mode, nb, n_x, *refs):
    bc_ref, bs_ref, rc_ref, rs_ref = refs[:4]
    x_refs = refs[4:4 + n_x]
    rest = refs[4 + n_x:]
    p_ref, q_ref = rest[-2], rest[-1]
    j = pl.program_id(2)
    nj = pl.num_programs(2)
    tj = x_refs[0].shape[1]
    if bc_ref.shape[1] == tj:
        bc, bs, rc, rs = bc_ref[...], bs_ref[...], rc_ref[...], rs_ref[...]
    else:
        off = pl.multiple_of(j * tj, tj)
        bc, bs = bc_ref[:, pl.ds(off, tj)], bs_ref[:, pl.ds(off, tj)]
        rc, rs = rc_ref[:, pl.ds(off, tj)], rs_ref[:, pl.ds(off, tj)]
    tc = (bc * rc - bs * rs).astype(BF16)
    ts = (bs * rc + bc * rs).astype(BF16)
    x1_ref = x_refs[0]
    x2_ref = x_refs[-1]

    pq = [(jnp.dot(tc, x1_ref[b], preferred_element_type=F32),
           jnp.dot(ts, x2_ref[b], preferred_element_type=F32)) for b in range(nb)]

    @pl.when(j == 0)
    def _():
        for b in range(nb):
            p_ref[b] = pq[b][0]
            q_ref[b] = pq[b][1]

    @pl.when(j > 0)
    def _():
        for b in range(nb):
            p_ref[b] += pq[b][0]
            q_ref[b] += pq[b][1]

    @pl.when(j == nj - 1)
    def _():
        if mode == "filt":
            nrm = rest[0][...]
            kre_ref, kim_ref = rest[1], rest[2]
            kre_ref[...] = (p_ref[0] + p_ref[1]) / nrm
            kim_ref[...] = (q_ref[1] - q_ref[0]) / nrm
        elif mode == "fwdk":
            kre, kim = rest[0][...], rest[1][...]
            yre_ref, yim_ref = rest[2], rest[3]
            for b in range(nb):
                pp, qq = p_ref[b], q_ref[b]
                yre_ref[b] = (pp * kre + qq * kim).astype(BF16)
                yim_ref[b] = (pp * kim - qq * kre).astype(BF16)
        elif mode == "inv":
            e_ref, x0_ref, o_ref = rest[0], rest[1], rest[2]
            for b in range(nb):
                o_ref[b] = ((p_ref[b] - q_ref[b] + e_ref[b]) * x0_ref[b]).astype(BF16)
        else:
            o_ref = rest[0]
            for b in range(nb):
                o_ref[b] = (p_ref[b] - q_ref[b]).astype(BF16)


def _dft(mode, kind, xs, extras, nb):
    B, L, C = xs[0].shape
    ti = min(L, 256)
    tj = min(L, 512)
    bc, bs, rc, rs = _dft_tables(kind, L, ti)
    grid = (B // nb, L // ti, L // tj)
    x_spec = pl.BlockSpec((nb, tj, C), lambda g, i, j: (g, j, 0))
    row_spec = lambda c, dt=None: pl.BlockSpec((nb, ti, c), lambda g, i, j: (g, i, 0))
    in_specs = [pl.BlockSpec((ti, L), lambda g, i, j: (0, 0)),
                pl.BlockSpec((ti, L), lambda g, i, j: (0, 0)),
                pl.BlockSpec((None, 1, L), lambda g, i, j: (i, 0, 0)),
                pl.BlockSpec((None, 1, L), lambda g, i, j: (i, 0, 0))] + [x_spec] * len(xs)
    if mode == "filt":
        in_specs += [pl.BlockSpec((1, HY_CH), lambda g, i, j: (0, 0))]
        out_shape = (jax.ShapeDtypeStruct((L, HY_CH), F32),) * 2
        out_specs = (pl.BlockSpec((ti, HY_CH), lambda g, i, j: (i, 0)),) * 2
    elif mode == "fwdk":
        in_specs += [pl.BlockSpec((ti, HY_CH), lambda g, i, j: (i, 0))] * 2
        out_shape = (jax.ShapeDtypeStruct((B, L, C), BF16),) * 2
        out_specs = (row_spec(C),) * 2
    elif mode == "inv":
        in_specs += [row_spec(C)] * 2
        out_shape = jax.ShapeDtypeStruct((B, L, C), BF16)
        out_specs = row_spec(C)
    else:
        out_shape = jax.ShapeDtypeStruct((B, L, C), BF16)
        out_specs = row_spec(C)
    return pl.pallas_call(
        functools.partial(_dft_kernel, mode, nb, len(xs)),
        out_shape=out_shape,
        grid=grid,
        in_specs=in_specs,
        out_specs=out_specs,
        scratch_shapes=[pltpu.VMEM((nb, ti, C), F32), pltpu.VMEM((nb, ti, C), F32)],
        compiler_params=_cparams(("arbitrary", "arbitrary", "arbitrary")),
        name="dft_" + mode,
    )(bc, bs, rc, rs, *xs, *extras)


FFT_R = 64
HYENA_KF = 16
FNET_KF = 16


def _pack_pairs(x):
    return pltpu.bitcast(x.astype(BF16), jnp.uint32)


def _unpack_pairs(w):
    return pltpu.bitcast(w, BF16)


def _to_blocks(w):
    return jnp.swapaxes(w.reshape(FFT_R, w.shape[-2], w.shape[-1]), 0, 1)


def _from_blocks(ws):
    kf, c = len(ws), ws[0].shape[-1]
    return jnp.swapaxes(jnp.stack(ws, axis=0), 0, 1).reshape(FFT_R // kf, kf, kf, c)


def _lead_in_kernel(g_ref, x_ref, o_ref):
    g = g_ref[...]
    xt = jnp.swapaxes(x_ref[...], 0, 1)
    for j in range(x_ref.shape[1]):
        o_ref[j] = _pack_pairs(jnp.dot(g, xt[j].astype(BF16), preferred_element_type=F32))


def _lead_in(g, x, kf, name):
    nbx, _, _, c = x.shape
    m2 = g.shape[0] // 2
    return pl.pallas_call(
        _lead_in_kernel,
        out_shape=jax.ShapeDtypeStruct((nbx, FFT_R // kf, kf, m2, c), jnp.uint32),
        grid=(nbx, FFT_R // kf),
        in_specs=[pl.BlockSpec(g.shape, lambda b, k: (0, 0)),
                  pl.BlockSpec((None, FFT_R, kf, c), lambda b, k: (b, 0, k, 0))],
        out_specs=pl.BlockSpec((None, None, kf, m2, c), lambda b, k: (b, k, 0, 0, 0)),
        compiler_params=_cparams(("arbitrary", "arbitrary")),
        name=name,
    )(g, x)


def _lead_out_kernel(n_extra, g_ref, w_ref, *rest):
    g = g_ref[...]
    o_ref = rest[-1]
    ys = [jnp.dot(g, _unpack_pairs(w_ref[j]), preferred_element_type=F32) for j in range(w_ref.shape[0])]
    y = jnp.swapaxes(jnp.stack(ys, axis=0), 0, 1)
    if n_extra:
        y = (y + rest[0][...]) * rest[1][...]
    o_ref[...] = y


def _lead_out(g, w, extras, name):
    nb, nk, kf, k2, c = w.shape
    blk = pl.BlockSpec((None, FFT_R, kf, c), lambda b, k: (b, 0, k, 0))
    return pl.pallas_call(
        functools.partial(_lead_out_kernel, len(extras)),
        out_shape=jax.ShapeDtypeStruct((nb, FFT_R, FFT_R, c), F32),
        grid=(nb, nk),
        in_specs=[pl.BlockSpec(g.shape, lambda b, k: (0, 0)),
                  pl.BlockSpec((None, None, kf, k2, c), lambda b, k: (b, k, 0, 0, 0))] + [blk] * len(extras),
        out_specs=blk,
        compiler_params=_cparams(("arbitrary", "arbitrary")),
        name=name,
    )(g, w, *extras)


def _interleave(a, b, axis):
    st = np.stack([a, b], axis=axis + 1)
    shape = list(a.shape)
    shape[axis] *= 2
    return st.reshape(shape)


def _hy2_tables():
    L = FFT_R * FFT_R
    n2 = 2 * L
    f1 = np.arange(2 * FFT_R, dtype=np.int64)
    s1 = np.arange(FFT_R, dtype=np.int64)
    th = np.pi * (((2 * f1[:, None] + 1) * s1[None, :]) % (4 * FFT_R)) / (2 * FFT_R)
    ga = _interleave(np.cos(th), -np.sin(th), 0)
    ma = _interleave(np.cos(th).T, -np.sin(th).T, 1) / L
    f2 = np.arange(FFT_R // 2, dtype=np.int64)
    s2 = np.arange(FFT_R, dtype=np.int64)
    idx = ((n2 // FFT_R) * 2 * f2[None, :, None] * s2[None, None, :]
           + (2 * f1[:, None, None] + 1) * s2[None, None, :]) % (2 * n2)
    al = np.pi * idx / n2
    c, s = np.cos(al), np.sin(al)
    nmat = np.concatenate([_interleave(c, s, 2), _interleave(-s, c, 2)], axis=1)
    ct, st = np.transpose(c, (0, 2, 1)), np.transpose(s, (0, 2, 1))
    mmat = _interleave(np.concatenate([ct, -st], axis=2), np.concatenate([st, ct], axis=2), 1)
    bf = lambda a: jnp.asarray(a, F32).astype(BF16)
    return bf(ga), bf(nmat), bf(mmat), bf(ma)


def _hy_mid_kernel(a_ref, n_ref, m_ref, k_ref, e_ref):
    half = FFT_R // 2
    kf = a_ref.shape[-2]
    a = _to_blocks(a_ref[...])
    ts = [jnp.dot(n_ref[j], _unpack_pairs(a[j]), preferred_element_type=F32)
          for j in range(kf)]
    ys = []
    for j in range(kf):
        tr, ti = ts[j][:half], ts[j][half:]
        kr, ki = k_ref[j, 0].astype(F32), k_ref[j, 1].astype(F32)
        ys.append(jnp.concatenate([tr * kr - ti * ki, tr * ki + ti * kr], axis=0).astype(BF16))
    e_ref[...] = _from_blocks([_pack_pairs(jnp.dot(m_ref[j], ys[j], preferred_element_type=F32))
                               for j in range(kf)])


def _hy_mid(a, nmat, mmat, khat):
    nb, nk, kf, _, c = a.shape
    nf1 = 2 * FFT_R
    blk = pl.BlockSpec((None, nk, kf, kf, c), lambda i, b: (b, 0, 0, i, 0))
    return pl.pallas_call(
        _hy_mid_kernel,
        out_shape=jax.ShapeDtypeStruct(a.shape, jnp.uint32),
        grid=(nf1 // kf, nb),
        in_specs=[blk,
                  pl.BlockSpec((kf, FFT_R, 2 * FFT_R), lambda i, b: (i, 0, 0)),
                  pl.BlockSpec((kf, 2 * FFT_R, FFT_R), lambda i, b: (i, 0, 0)),
                  pl.BlockSpec((kf, 2, FFT_R // 2, c), lambda i, b: (i, 0, 0, 0))],
        out_specs=blk,
        compiler_params=_cparams(("arbitrary", "arbitrary")),
        name="l0_hyena_mid",
    )(a, nmat, mmat, khat)


def _hy_kfilt_kernel(a_ref, n_ref, nrm_ref, k_ref):
    half = FFT_R // 2
    nrm = nrm_ref[...]
    af, ab = _to_blocks(a_ref[0]), _to_blocks(a_ref[1])
    for j in range(a_ref.shape[-2]):
        tf = jnp.dot(n_ref[j], _unpack_pairs(af[j]), preferred_element_type=F32)
        tb = jnp.dot(n_ref[j], _unpack_pairs(ab[j]), preferred_element_type=F32)
        k_ref[j, 0] = ((tf[:half] + tb[:half]) / nrm).astype(k_ref.dtype)
        k_ref[j, 1] = ((tf[half:] - tb[half:]) / nrm).astype(k_ref.dtype)


def _hy_kfilt(a, nmat, nrm):
    _, nk, kf, _, c = a.shape
    nf1 = 2 * FFT_R
    return pl.pallas_call(
        _hy_kfilt_kernel,
        out_shape=jax.ShapeDtypeStruct((nf1, 2, FFT_R // 2, c), BF16),
        grid=(nf1 // kf,),
        in_specs=[pl.BlockSpec((2, nk, kf, kf, c), lambda i: (0, 0, 0, i, 0)),
                  pl.BlockSpec((kf, FFT_R, 2 * FFT_R), lambda i: (i, 0, 0)),
                  _const_spec(nrm.shape)],
        out_specs=pl.BlockSpec((kf, 2, FFT_R // 2, c), lambda i: (i, 0, 0, 0)),
        compiler_params=_cparams(("arbitrary",)),
        name="l0_hyena_kfilt",
    )(a, nmat, nrm)


def _hyena_long(u, e, x0, hfilt, nrm, nb):
    L = FFT_R * FFT_R
    c = u.shape[-1]
    v4 = lambda a, n: a.reshape(n, FFT_R, FFT_R, c)
    ga, nmat, mmat, ma = _hy2_tables()
    khat = _hy_kfilt(_lead_in(ga, v4(hfilt, 2), HYENA_KF, "l0_hyena_fwd_a"), nmat, nrm)
    ee = _hy_mid(_lead_in(ga, v4(u, nb), HYENA_KF, "l0_hyena_fwd_a"), nmat, mmat, khat)
    y = _lead_out(ma, ee, [v4(e, nb), v4(x0, nb)], "l0_hyena_inv_a")
    return y.reshape(nb * L, c)


def _fn2_tables():
    L = FFT_R * FFT_R
    r = np.arange(FFT_R, dtype=np.int64)
    idx = (FFT_R * r[None, :, None] * r[None, None, :] + r[None, :, None] * r[:, None, None]) % L
    gm = 2.0 * np.pi * idx / L
    c, s = np.cos(gm), np.sin(gm)
    g1 = _interleave(np.concatenate([c, -s], axis=2), np.concatenate([-s, -c], axis=2), 1)
    dl = 2.0 * np.pi * ((r[:, None] * r[None, :]) % FFT_R) / FFT_R
    g2 = _interleave(np.cos(dl), np.sin(dl), 1) / math.sqrt(L * FN_GROUP_CH)
    bf = lambda a: jnp.asarray(a, F32).astype(BF16)
    return bf(g1), bf(g2)


def _fnet_s1_kernel(x_ref, m_ref, cs_ref, g1_ref, o_ref, zc_ref, zs_ref):
    xs = jnp.swapaxes(x_ref[...], 0, 1).reshape(FNET_KF * FFT_R, D_MODEL)
    m = m_ref[...]
    h = (_ln_plain(xs) * (1.0 + m[:, D_MODEL:2 * D_MODEL]) + m[:, 0:D_MODEL]).astype(BF16)
    cs = cs_ref[...]
    for g in range(D_MODEL // FN_GROUP_CH):
        a = g * FN_GROUP_CH
        z = jnp.dot(h[:, a:a + FN_GROUP_CH], cs, preferred_element_type=F32)
        zc_ref[:, a:a + FN_GROUP_CH] = z[:, :FN_GROUP_CH].astype(BF16)
        zs_ref[:, a:a + FN_GROUP_CH] = z[:, FN_GROUP_CH:].astype(BF16)
    ws = []
    for j in range(FNET_KF):
        r0 = j * FFT_R
        s = jnp.concatenate([zc_ref[r0:r0 + FFT_R, :], zs_ref[r0:r0 + FFT_R, :]], axis=0)
        ws.append(_pack_pairs(jnp.dot(g1_ref[j], s, preferred_element_type=F32)))
    o_ref[...] = _from_blocks(ws)


def _fnet_long(x, mods, mod_base, nb):
    L = FFT_R * FFT_R
    d = D_MODEL
    g1, g2 = _fn2_tables()
    cs = _group_dft_table()
    kf = FNET_KF
    bb = pl.pallas_call(
        _fnet_s1_kernel,
        out_shape=jax.ShapeDtypeStruct((nb, FFT_R // kf, kf, FFT_R, d), jnp.uint32),
        grid=(nb, FFT_R // kf),
        in_specs=[pl.BlockSpec((None, FFT_R, kf, d), lambda b, k: (b, 0, k, 0)),
                  pl.BlockSpec((None, 1, 6 * d), lambda b, k: (mod_base + b, 0, 0)),
                  pl.BlockSpec(cs.shape, lambda b, k: (0, 0)),
                  pl.BlockSpec((kf, 2 * FFT_R, 2 * FFT_R), lambda b, k: (k, 0, 0))],
        out_specs=pl.BlockSpec((None, FFT_R // kf, kf, kf, d), lambda b, k: (b, 0, 0, k, 0)),
        scratch_shapes=[pltpu.VMEM((kf * FFT_R, d), BF16), pltpu.VMEM((kf * FFT_R, d), BF16)],
        compiler_params=_cparams(("arbitrary", "arbitrary")),
        name="l1_fnet_stage1",
    )(x.reshape(nb, FFT_R, FFT_R, d), mods, cs, g1)
    y = _lead_out(g2, bb, [], "l1_fnet_stage2")
    return y.reshape(nb * L, d)


def _group_dft_table():
    g = FN_GROUP_CH
    jk = (np.arange(g, dtype=np.int64)[:, None] * np.arange(g, dtype=np.int64)[None, :]) % g
    ang = 2.0 * np.pi * jk / g
    return jnp.asarray(np.concatenate([np.cos(ang), np.sin(ang)], axis=1), F32).astype(BF16)


def _fnet_front_kernel(x_ref, m_ref, cs_ref, zc_ref, zs_ref):
    m = m_ref[...]
    h = (_ln_plain(x_ref[...]) * (1.0 + m[:, D_MODEL:2 * D_MODEL]) + m[:, 0:D_MODEL]).astype(BF16)
    cs = cs_ref[...]
    for g in range(D_MODEL // FN_GROUP_CH):
        a = g * FN_GROUP_CH
        z = jnp.dot(h[:, a:a + FN_GROUP_CH], cs, preferred_element_type=F32)
        zc_ref[:, a:a + FN_GROUP_CH] = z[:, :FN_GROUP_CH].astype(BF16)
        zs_ref[:, a:a + FN_GROUP_CH] = z[:, FN_GROUP_CH:].astype(BF16)


def _fnet_front(x, mods, mod_base, tiles_per_mod):
    t = x.shape[0]
    tm = 2 * ROW_TILE
    tiles_per_mod = max(tiles_per_mod // 2, 1)
    cs = _group_dft_table()
    return pl.pallas_call(
        _fnet_front_kernel,
        out_shape=(jax.ShapeDtypeStruct((t, D_MODEL), BF16),) * 2,
        grid=(t // tm,),
        in_specs=[pl.BlockSpec((tm, D_MODEL), lambda i: (i, 0)),
                  _mod_spec(mod_base, tiles_per_mod),
                  _const_spec(cs.shape)],
        out_specs=(pl.BlockSpec((tm, D_MODEL), lambda i: (i, 0)),) * 2,
        compiler_params=_cparams(("arbitrary",)),
        name="l1_fnet_front",
    )(x, mods, cs)


def _post_kernel(n_a, *refs):
    x_ref, m_ref = refs[0], refs[1]
    a_refs = refs[2:2 + n_a]
    wo_refs = refs[2 + n_a:2 + 2 * n_a]
    g1_ref, b1_ref, w1_ref, w2_ref, g2_ref, b2_ref, o_ref = refs[2 + 2 * n_a:]
    m = m_ref[...]
    d = D_MODEL
    tm = x_ref.shape[0]
    halves = [(r, r + tm // POST_SPLIT) for r in range(0, tm, tm // POST_SPLIT)]
    outs = []
    for r0, r1 in halves:
        out = _bdot(a_refs[0][r0:r1, :], wo_refs[0][...])
        for a_ref, wo_ref in zip(a_refs[1:], wo_refs[1:]):
            out += _bdot(a_ref[r0:r1, :], wo_ref[...])
        outs.append(out)
    x1s, hs = [], []
    for (r0, r1), out in zip(halves, outs):
        x1 = _ln_plain(ALPHA * x_ref[r0:r1, :] + m[:, 2 * d:3 * d] * out) * g1_ref[...] + b1_ref[...]
        x1s.append(x1)
        hs.append((_ln_plain(x1) * (1.0 + m[:, 4 * d:5 * d]) + m[:, 3 * d:4 * d]).astype(BF16))
    accs = []
    n_c = D_FF // d

    def up(h, c):
        hc = jnp.maximum(jnp.dot(h, w1_ref[:, c * d:(c + 1) * d], preferred_element_type=F32), 0.0)
        return (hc * hc).astype(BF16)

    for h in hs:
        acc = None
        nxt = up(h, 0)
        for c in range(n_c):
            cur = nxt
            if c + 1 < n_c:
                nxt = up(h, c + 1)
            part = jnp.dot(cur, w2_ref[c * d:(c + 1) * d, :], preferred_element_type=F32)
            acc = part if acc is None else acc + part
        accs.append(acc)
    for (r0, r1), x1, acc in zip(halves, x1s, accs):
        o_ref[r0:r1, :] = _ln_plain(ALPHA * x1 + m[:, 5 * d:6 * d] * acc) * g2_ref[...] + b2_ref[...]


def _post(x, mods, mod_base, tiles_per_mod, a_list, wo_list, g1, b1, w1, w2, g2, b2):
    t = x.shape[0]
    tm = POST_TILE
    row = lambda c: pl.BlockSpec((tm, c), lambda i: (i, 0))
    once = lambda v: pl.BlockSpec(v.shape, lambda i: (0,) * v.ndim, pipeline_mode=pl.Buffered(1))
    in_specs = ([row(D_MODEL), _mod_spec(mod_base, tiles_per_mod * ROW_TILE // tm)]
                + [row(a.shape[1]) for a in a_list]
                + [once(w) for w in wo_list]
                + [once(v) for v in (g1, b1, w1, w2, g2, b2)])
    return pl.pallas_call(
        functools.partial(_post_kernel, len(a_list)),
        out_shape=jax.ShapeDtypeStruct((t, D_MODEL), F32),
        grid=(t // tm,),
        in_specs=in_specs,
        out_specs=row(D_MODEL),
        compiler_params=_cparams(("arbitrary",)),
        name="post_mlp",
    )(x, mods, *a_list, *wo_list, g1, b1, w1, w2, g2, b2)


def _rot_cols(w):
    parts = []
    for seg in range(2):
        o = seg * 32
        parts += [-w[:, o + 16:o + 32], w[:, o:o + 16]]
    return jnp.concatenate(parts, axis=1)


def _pad_cols(w, n):
    return jnp.pad(w, ((0, 0), (0, n - w.shape[1])))


def _block_diag2(w):
    z = jnp.zeros_like(w)
    return jnp.concatenate([jnp.concatenate([w, z], axis=1), jnp.concatenate([z, w], axis=1)], axis=0)


def _rope_tables(L):
    rows = L // GRID_W
    row = np.repeat(np.arange(rows, dtype=np.float64), GRID_W)
    col = np.tile(np.arange(GRID_W, dtype=np.float64), rows)
    half = QK_ROPE // 2
    inv = 1.0 / (ROPE_THETA ** (np.arange(0, half, 2, dtype=np.float64) / half))
    ar = row[:, None] * inv[None, :]
    ac = col[:, None] * inv[None, :]
    ang = np.concatenate([ar, ar, ac, ac], axis=1)
    cos = np.concatenate([np.cos(ang), np.ones_like(ang)], axis=1)
    sin = np.concatenate([np.sin(ang), np.zeros_like(ang)], axis=1)
    return jnp.asarray(cos, F32), jnp.asarray(sin, F32)


def kernel(x_prompt, x_sample, cache_l0_ckv, cache_l0_krope, c, c_ctx, l0_ada_w, l0_ada_b, l0_w_in, l0_conv_w, l0_conv_b, l0_hf_w1, l0_hf_b1, l0_hf_freq, l0_hf_w2, l0_hf_b2, l0_hf_w3, l0_hf_skip, l0_q_norm, l0_q_up, l0_kv_norm, l0_kv_up, l0_w_out, l0_ln1_g, l0_ln1_b, l0_mlp_w1, l0_mlp_w2, l0_ln2_g, l0_ln2_b, l1_ada_w, l1_ada_b, l1_w_out, l1_ln1_g, l1_ln1_b, l1_mlp_w1, l1_mlp_w2, l1_ln2_g, l1_ln2_b):
    nbc, lc, d = x_prompt.shape
    nbs, ls, _ = x_sample.shape
    past = cache_l0_ckv.shape[1]
    tm = ROW_TILE
    row1 = lambda v: v.reshape(1, -1)

    cond8 = jnp.concatenate([c_ctx[None, :], c, jnp.zeros((8 - 1 - nbs, d), F32)], axis=0)
    mods0 = _modulation(cond8, l0_ada_w, l0_ada_b)
    mods1 = _modulation(cond8, l1_ada_w, l1_ada_b)

    kpe_w = l0_w_in[:, 1920:1984]
    win = jnp.concatenate([l0_w_in[:, :1920], _pad_cols(kpe_w, LANE), _pad_cols(_rot_cols(kpe_w), LANE)],
                          axis=1).astype(BF16)
    dh = QK_NOPE + QK_ROPE
    q_nope = [l0_q_up[:, h * dh:h * dh + QK_NOPE] for h in range(MLA_HEADS)]
    q_pe = [l0_q_up[:, h * dh + QK_NOPE:(h + 1) * dh] for h in range(MLA_HEADS)]
    qup = jnp.concatenate(q_nope + [_pad_cols(w, LANE) for w in q_pe]
                          + [_pad_cols(_rot_cols(w), LANE) for w in q_pe], axis=1).astype(BF16)
    kvup = l0_kv_up.astype(BF16)
    front_w = (win, row1(l0_q_norm), qup, row1(l0_kv_norm), kvup, l0_conv_w, row1(l0_conv_b), row1(l0_hf_skip))
    w1p = jnp.pad(l0_hf_w1, ((0, LANE - l0_hf_w1.shape[0]), (0, 0)))
    two = lambda v: jnp.tile(row1(v), (1, 2))
    filt_w = (_block_diag2(w1p), two(l0_hf_b1), two(l0_hf_freq), _block_diag2(l0_hf_w2), two(l0_hf_b2),
              _block_diag2(l0_hf_w3).astype(BF16))
    wo0 = l0_w_out.astype(BF16)

    xc = x_prompt.reshape(nbc * lc, d)
    xs = x_sample.reshape(nbs * ls, d)
    groups = (
        dict(x=xc, nb=nbc, L=lc, mod_base=0, tiles_per_mod=nbc * lc // tm, dft_nb=4, tq=lc, hps=MLA_HEADS),
        dict(x=xs, nb=nbs, L=ls, mod_base=1, tiles_per_mod=ls // tm, dft_nb=nbs, tq=512, hps=1),
    )
    ones_tab = (jnp.ones((FRONT_TILE, LANE), F32), jnp.zeros((FRONT_TILE, LANE), F32))

    outs = []
    ctx_ckv = ctx_krope = None
    for gi, g in enumerate(groups):
        nb, L = g["nb"], g["L"]
        tiles_per_seq = L // tm
        latent = gi == 1
        cos, sin = _rope_tables(L) if latent else ones_tab
        two_stage = L == FFT_R * FFT_R
        io_dtype = F32 if two_stage else BF16
        u, e, x0, q, k, vt, kvn, kpe = _front(g["x"], mods0, g["mod_base"], g["tiles_per_mod"] * tm, front_w,
                                              cos, sin, latent, L, FRONT_TILE, io_dtype)
        if latent:
            extra = _cache_kv(cache_l0_ckv.reshape(nbs * past, KV_LORA),
                              _pad_cols(cache_l0_krope.reshape(nbs * past, QK_ROPE), LANE), kvup)
        else:
            extra = None
            ctx_ckv = kvn.reshape(nb, L, KV_LORA)
            ctx_krope = kpe.reshape(nb, L, QK_ROPE)
        y_mla = _attention(q, k, vt, extra, nb, L, g["tq"], g["hps"])

        hfilt, hnorm = _filters(L, io_dtype, *filt_w)
        if two_stage:
            y_hy = _hyena_long(u, e, x0, hfilt, hnorm, nb)
        else:
            kre, kim = _dft("filt", "hy_fwd", [hfilt], [hnorm], 2)
            sh = (nb, L, HY_CH)
            yre, yim = _dft("fwdk", "hy_fwd", [u.reshape(sh)], [kre, kim], g["dft_nb"])
            y_hy = _dft("inv", "hy_inv", [yre, yim], [e.reshape(sh), x0.reshape(sh)], g["dft_nb"])
            y_hy = y_hy.reshape(nb * L, HY_CH)

        x1 = _post(g["x"], mods0, g["mod_base"], g["tiles_per_mod"], [y_hy, y_mla],
                   [wo0[:HY_CH], wo0[HY_CH:]], row1(l0_ln1_g), row1(l0_ln1_b),
                   l0_mlp_w1.astype(BF16), l0_mlp_w2.astype(BF16), row1(l0_ln2_g), row1(l0_ln2_b))

        if L == FFT_R * FFT_R:
            yf = _fnet_long(x1, mods1, g["mod_base"], nb)
        else:
            zc, zs = _fnet_front(x1, mods1, g["mod_base"], g["tiles_per_mod"])
            sh = (nb, L, d)
            yf = _dft("fnet", "fnet", [zc.reshape(sh), zs.reshape(sh)], [], min(g["dft_nb"], 2))
            yf = yf.reshape(nb * L, d)
        x2 = _post(x1, mods1, g["mod_base"], g["tiles_per_mod"], [yf],
                   [l1_w_out.astype(BF16)], row1(l1_ln1_g), row1(l1_ln1_b),
                   l1_mlp_w1.astype(BF16), l1_mlp_w2.astype(BF16), row1(l1_ln2_g), row1(l1_ln2_b))
        outs.append(x2.reshape(nb, L, d))

    return (outs[0], outs[1], ctx_ckv, ctx_krope)
```

```python
import functools
import math

import numpy as np
import jax
import jax.numpy as jnp
from jax import lax
from jax.experimental import pallas as pl
from jax.experimental.pallas import tpu as pltpu

F32 = jnp.float32
BF16 = jnp.bfloat16
HI = lax.Precision.HIGHEST

D_MODEL = 1024
DEPTH = 2
GRID_W = 64
HY_CH = 512
FILT_BANDS = 16
FILT_ORDER = 64
FAST_DECAY_PCT = 0.3
SLOW_DECAY_PCT = 1.5
DECAY_TARGET = 1e-2
MAX_DECAY = math.log(DECAY_TARGET) / FAST_DECAY_PCT
MIN_DECAY = math.log(DECAY_TARGET) / SLOW_DECAY_PCT
MLA_HEADS = 4
QK_NOPE = 128
QK_ROPE = 64
V_HEAD = 128
Q_LORA = 256
KV_LORA = 128
ROPE_THETA = 10000.0
FN_GROUP_CH = 128
D_FF = 4096
ALPHA = (2 * DEPTH) ** 0.25
LN_EPS = 1e-5
RMS_EPS = 1e-6

LANE = 128
ROW_TILE = 256
FRONT_TILE = 512
POST_TILE = 512
POST_SPLIT = 2
QK_PAD = 256
VT_ROWS = V_HEAD + 16
LOG2E = 1.4426950408889634
VMEM_LIMIT = 56 * 1024 * 1024


def _cparams(sem):
    return pltpu.CompilerParams(dimension_semantics=sem, vmem_limit_bytes=VMEM_LIMIT)


def _ln_plain(x):
    mu = jnp.mean(x, axis=-1, keepdims=True)
    xc = x - mu
    var = jnp.mean(xc * xc, axis=-1, keepdims=True)
    return xc * lax.rsqrt(var + LN_EPS)


def _rms(x, g):
    return x * lax.rsqrt(jnp.mean(x * x, axis=-1, keepdims=True) + RMS_EPS) * g


def _bdot(a, b):
    return jnp.dot(a.astype(BF16), b, preferred_element_type=F32)


def _vt_rows(v):
    ones = jnp.ones((VT_ROWS - V_HEAD, v.shape[0]), BF16)
    return jnp.concatenate([jnp.transpose(v).astype(BF16), ones], axis=0)


MOD_STREAMS = 3


def _mod_kernel(c_ref, *refs):
    w_refs, b_ref, o_ref = refs[:MOD_STREAMS], refs[MOD_STREAMS], refs[MOD_STREAMS + 1]
    c = c_ref[...]
    s = c / (1.0 + jnp.exp(-c))
    s_hi = s.astype(BF16)
    s_lo = (s - s_hi.astype(F32)).astype(BF16)
    s2 = jnp.concatenate([s_hi, s_lo], axis=0)
    nr = s.shape[0]
    tn = w_refs[0].shape[1]
    for k, w_ref in enumerate(w_refs):
        w = w_ref[...]
        w_hi = w.astype(BF16)
        w_lo = (w - w_hi.astype(F32)).astype(BF16)
        r1 = jnp.dot(s2, w_hi, preferred_element_type=F32)
        r2 = jnp.dot(s_hi, w_lo, preferred_element_type=F32)
        o_ref[:, k * tn:(k + 1) * tn] = r1[:nr] + r1[nr:] + r2 + b_ref[:, k * tn:(k + 1) * tn]


def _modulation(cond8, w, b):
    n = w.shape[1]
    tn = 512
    step = MOD_STREAMS * tn
    w_specs = [pl.BlockSpec((D_MODEL, tn), functools.partial(lambda j, k: (0, MOD_STREAMS * j + k), k=k))
               for k in range(MOD_STREAMS)]
    out = pl.pallas_call(
        _mod_kernel,
        out_shape=jax.ShapeDtypeStruct((8, n), F32),
        grid=(n // step,),
        in_specs=[pl.BlockSpec((8, D_MODEL), lambda j: (0, 0))] + w_specs
                 + [pl.BlockSpec((1, step), lambda j: (0, j))],
        out_specs=pl.BlockSpec((8, step), lambda j: (0, j)),
        compiler_params=_cparams(("arbitrary",)),
        name="modulation",
    )(cond8, *([w] * MOD_STREAMS), b.reshape(1, n))
    return out.reshape(8, 1, n)


def _mod_spec(mod_base, tiles_per_mod):
    return pl.BlockSpec((None, 1, 6 * D_MODEL), lambda i: (mod_base + i // tiles_per_mod, 0, 0))


def _const_spec(shape):
    nd = len(shape)
    return pl.BlockSpec(shape, lambda i: (0,) * nd)


HALO = 8


def _front_kernel(tiles_per_seq, x_ref, xp_ref, xn_ref, m_ref, win_ref, qn_ref, qup_ref, kvn_ref, kvup_ref,
                  cos_ref, sin_ref, cw_ref, cb_ref, skip_ref,
                  u_ref, e_ref, x0_ref, q_ref, k_ref, v_ref, kvn_out_ref, kpe_ref):
    i = pl.program_id(0)
    m = m_ref[...]
    tm = x_ref.shape[0]
    nh = 3 * HY_CH
    xe = jnp.concatenate([xp_ref[...], x_ref[...], xn_ref[...]], axis=0)
    he = _ln_plain(xe) * (1.0 + m[:, D_MODEL:2 * D_MODEL]) + m[:, 0:D_MODEL]

    z = _bdot(he[HALO:HALO + tm], win_ref[:, nh:])
    zh = _bdot(he, win_ref[:, :nh])
    q_c = z[:, 0:256]
    kv_c = z[:, 256:384]
    cos = cos_ref[...]
    sin = sin_ref[...]
    kpe = z[:, 384:512] * cos + z[:, 512:640] * sin
    kpe_ref[...] = kpe[:, :QK_ROPE]
    kpe_b = kpe.astype(BF16)
    q = _bdot(_rms(q_c, qn_ref[...]), qup_ref[...]) * (LOG2E / math.sqrt(QK_NOPE + QK_ROPE))
    kvn = _rms(kv_c, kvn_ref[...])
    kvn_out_ref[...] = kvn
    kv = _bdot(kvn, kvup_ref[...])

    pos = i % tiles_per_seq
    rows = lax.broadcasted_iota(jnp.int32, (tm + 2 * HALO, 1), 0)
    inside = jnp.logical_and(jnp.logical_or(rows >= HALO, pos != 0),
                             jnp.logical_or(rows < tm + HALO, pos != tiles_per_seq - 1))
    zh = jnp.where(inside, zh, 0.0)
    cw = cw_ref[...]
    pz = (pltpu.roll(zh, 1, 0) * cw[0:1, :] + zh * cw[1:2, :]
          + pltpu.roll(zh, tm + 2 * HALO - 1, 0) * cw[2:3, :])[HALO:HALO + tm] + cb_ref[...]
    u = pz[:, 2 * HY_CH:] * pz[:, HY_CH:2 * HY_CH]
    u_ref[...] = u.astype(u_ref.dtype)
    e_ref[...] = u * skip_ref[...]
    x0_ref[...] = pz[:, :HY_CH]

    for hd in range(MLA_HEADS):
        a = hd * LANE
        q_pe = (q[:, 512 + a:512 + a + LANE] * cos + q[:, 1024 + a:1024 + a + LANE] * sin).astype(BF16)
        q_ref[hd] = jnp.concatenate([q[:, a:a + LANE].astype(BF16), q_pe], axis=-1)
        k_ref[hd] = jnp.concatenate([kv[:, 2 * a:2 * a + LANE].astype(BF16), kpe_b], axis=-1)
        v_ref[hd] = _vt_rows(kv[:, 2 * a + LANE:2 * a + 2 * LANE])


def _front(x, mods, mod_base, rows_per_mod, w, cos, sin, rope, seq_len, tm, u_dtype):
    t = x.shape[0]
    tiles_per_seq = seq_len // tm
    tiles_per_mod = rows_per_mod // tm
    win, qn, qup, kvn, kvup, conv_w, conv_b, skip = w
    if rope:
        tab_spec = pl.BlockSpec((tm, LANE), lambda i: (i % tiles_per_seq, 0))
    else:
        tab_spec = pl.BlockSpec((tm, LANE), lambda i: (0, 0))
    r8 = tm // HALO
    n8 = t // HALO
    hy_out = lambda dt: jax.ShapeDtypeStruct((t, HY_CH), dt)
    hy_spec = pl.BlockSpec((tm, HY_CH), lambda i: (i, 0))
    return pl.pallas_call(
        functools.partial(_front_kernel, tiles_per_seq),
        out_shape=(hy_out(u_dtype), hy_out(F32), hy_out(F32),
                   jax.ShapeDtypeStruct((MLA_HEADS, t, QK_PAD), BF16),
                   jax.ShapeDtypeStruct((MLA_HEADS, t, QK_PAD), BF16),
                   jax.ShapeDtypeStruct((MLA_HEADS, VT_ROWS, t), BF16),
                   jax.ShapeDtypeStruct((t, KV_LORA), F32),
                   jax.ShapeDtypeStruct((t, QK_ROPE), F32)),
        grid=(t // tm,),
        in_specs=[pl.BlockSpec((tm, D_MODEL), lambda i: (i, 0)),
                  pl.BlockSpec((HALO, D_MODEL), lambda i: (jnp.maximum(i * r8 - 1, 0), 0)),
                  pl.BlockSpec((HALO, D_MODEL), lambda i: (jnp.minimum((i + 1) * r8, n8 - 1), 0)),
                  _mod_spec(mod_base, tiles_per_mod),
                  _const_spec(win.shape), _const_spec(qn.shape), _const_spec(qup.shape),
                  _const_spec(kvn.shape), _const_spec(kvup.shape),
                  tab_spec, tab_spec,
                  _const_spec(conv_w.shape), _const_spec(conv_b.shape), _const_spec(skip.shape)],
        out_specs=(hy_spec, hy_spec, hy_spec,
                   pl.BlockSpec((MLA_HEADS, tm, QK_PAD), lambda i: (0, i, 0)),
                   pl.BlockSpec((MLA_HEADS, tm, QK_PAD), lambda i: (0, i, 0)),
                   pl.BlockSpec((MLA_HEADS, VT_ROWS, tm), lambda i: (0, 0, i)),
                   pl.BlockSpec((tm, KV_LORA), lambda i: (i, 0)),
                   pl.BlockSpec((tm, QK_ROPE), lambda i: (i, 0))),
        compiler_params=_cparams(("arbitrary",)),
        name="l0_front",
    )(x, x, x, mods, win, qn, qup, kvn, kvup, cos, sin, conv_w, conv_b, skip)


def _cache_kv_kernel(ckv_ref, kr_ref, kvup_ref, k_ref, v_ref):
    kv = _bdot(ckv_ref[...], kvup_ref[...])
    kr = kr_ref[...].astype(BF16)
    for hd in range(MLA_HEADS):
        a = 2 * hd * LANE
        k_ref[hd] = jnp.concatenate([kv[:, a:a + LANE].astype(BF16), kr], axis=-1)
        v_ref[hd] = _vt_rows(kv[:, a + LANE:a + 2 * LANE])


def _cache_kv(ckv, krope_pad, kvup):
    t = ckv.shape[0]
    return pl.pallas_call(
        _cache_kv_kernel,
        out_shape=(jax.ShapeDtypeStruct((MLA_HEADS, t, QK_PAD), BF16),
                   jax.ShapeDtypeStruct((MLA_HEADS, VT_ROWS, t), BF16)),
        name="l0_cache_kv",
    )(ckv, krope_pad, kvup)


def _col_reduce(x, op):
    rows, n = x.shape
    for g in (32, 8):
        if rows % (8 * g) == 0 and rows > 8 * g:
            x = op(x.reshape(rows // (8 * g), 8 * g, n), axis=0)
            rows = 8 * g
    return op(x, axis=0, keepdims=True)


def _attn_kernel(n_kv, hps, q_ref, *refs):
    k_refs, vt_refs = refs[:n_kv], refs[n_kv:2 * n_kv]
    o_ref, s_even, s_odd, m_even, m_odd = refs[2 * n_kv:]
    i = pl.program_id(0)

    @pl.when(i == 0)
    def _():
        s_odd[...] = jnp.zeros_like(s_odd)
        m_odd[...] = jnp.zeros_like(m_odd)

    def step(s_write, m_write, s_read, m_read):
        nt = (((1,), (1,)), ((), ()))
        for h in range(hps):
            q = q_ref[h]
            r0 = 0
            m = None
            for k_ref in k_refs:
                lk = k_ref.shape[1]
                sblk = lax.dot_general(k_ref[h], q, nt, preferred_element_type=F32)
                s_write[h, r0:r0 + lk, :] = sblk
                mc = _col_reduce(sblk, jnp.max)
                m = mc if m is None else jnp.maximum(m, mc)
                r0 += lk
            m_write[h] = m
        for h in range(hps):
            pb = jnp.exp2(s_read[h] - m_read[h]).astype(BF16)
            acc = None
            r0 = 0
            for vt_ref in vt_refs:
                lk = vt_ref.shape[2]
                pv = jnp.dot(vt_ref[h], pb[r0:r0 + lk, :], preferred_element_type=F32)
                acc = pv if acc is None else acc + pv
                r0 += lk
            o_ref[:, h * V_HEAD:(h + 1) * V_HEAD] = jnp.transpose(
                acc[:V_HEAD] / acc[V_HEAD:V_HEAD + 1]).astype(o_ref.dtype)

    pl.when(i % 2 == 0)(lambda: step(s_even, m_even, s_odd, m_odd))
    pl.when(i % 2 == 1)(lambda: step(s_odd, m_odd, s_even, m_even))


def _attention(q, k, vt, extra, nb, lq, tq, hps):
    nq = lq // tq
    ng = MLA_HEADS // hps
    n_tiles = nb * ng * nq

    def where(t):
        bh = t // nq
        return bh // ng, bh % ng, t % nq

    def score_side(fn):
        return lambda i: fn(*where(jnp.minimum(i, n_tiles - 1)))

    def value_side(fn):
        return lambda i: fn(*where(jnp.maximum(i - 1, 0)))

    ks, vts = [k], [vt]
    if extra is not None:
        ks.append(extra[0])
        vts.append(extra[1])
    in_specs = [pl.BlockSpec((hps, tq, QK_PAD), score_side(lambda b, h, j: (h, b * nq + j, 0)))]
    in_specs += [pl.BlockSpec((hps, a.shape[1] // nb, QK_PAD), score_side(lambda b, h, j: (h, b, 0))) for a in ks]
    in_specs += [pl.BlockSpec((hps, VT_ROWS, a.shape[2] // nb), value_side(lambda b, h, j: (h, 0, b))) for a in vts]
    lk_total = sum(a.shape[1] // nb for a in ks)
    return pl.pallas_call(
        functools.partial(_attn_kernel, len(ks), hps),
        out_shape=jax.ShapeDtypeStruct((nb * lq, MLA_HEADS * V_HEAD), BF16),
        grid=(n_tiles + 1,),
        in_specs=in_specs,
        out_specs=pl.BlockSpec((tq, hps * V_HEAD), value_side(lambda b, h, j: (b * nq + j, h))),
        scratch_shapes=[pltpu.VMEM((hps, lk_total, tq), F32), pltpu.VMEM((hps, lk_total, tq), F32),
                        pltpu.VMEM((hps, 1, tq), F32), pltpu.VMEM((hps, 1, tq), F32)],
        compiler_params=_cparams(("arbitrary",)),
        name="l0_attention",
    )(q, *ks, *vts)


def _filter_kernel(z_ref, w1_ref, b1_ref, fr_ref, w2_ref, b2_ref, w3_ref, dl_ref, h_ref, norm_ref):
    i = pl.program_id(0)
    z = z_ref[...]
    tl = z.shape[0]
    fr = fr_ref[...]
    z2 = jnp.concatenate([z[:tl // 2], z[tl // 2:]], axis=1)
    h = jnp.sin(fr * (jnp.dot(z2, w1_ref[...], precision=HI, preferred_element_type=F32) + b1_ref[...]))
    h = jnp.sin(fr * (jnp.dot(h, w2_ref[...], precision=HI, preferred_element_type=F32) + b2_ref[...]))
    h = _bdot(h, w3_ref[...])
    h = jnp.concatenate([h[:, :2 * HY_CH], h[:, 2 * HY_CH:]], axis=0)
    decay = jnp.exp(-(z[:, 0:1] * dl_ref[...]))
    hf = h[:, :HY_CH] * decay
    hb = h[:, HY_CH:] * decay
    part = jnp.sum(jnp.abs(hf) + jnp.abs(hb), axis=0, keepdims=True)

    @pl.when(i == 0)
    def _():
        norm_ref[...] = part

    @pl.when(i > 0)
    def _():
        norm_ref[...] += part

    rows = lax.broadcasted_iota(jnp.int32, hb.shape, 0) + i * tl
    h_ref[0] = hf.astype(h_ref.dtype)
    h_ref[1] = jnp.where(rows == 0, 0.0, hb).astype(h_ref.dtype)


def _filter_embedding(L):
    t = np.linspace(0.0, 1.0, L)[:, None]
    w_ang = 2.0 * np.pi * np.arange(L) / L
    bands = np.linspace(1e-4, FILT_BANDS - 1, FILT_BANDS)
    ang = w_ang[:, None] * bands[None, :]
    z = np.zeros((L, LANE), np.float64)
    z[:, 0:1] = t
    z[:, 1:1 + FILT_BANDS] = np.cos(ang)
    z[:, 1 + FILT_BANDS:1 + 2 * FILT_BANDS] = -np.sin(ang)
    return jnp.asarray(z, F32)


def _filters(L, h_dtype, w1p, b1, fr, w2, b2, w3):
    tl = min(L, 512)
    z = _filter_embedding(L)
    deltas = jnp.asarray(np.abs(np.linspace(MIN_DECAY, MAX_DECAY, HY_CH))[None, :], F32)
    return pl.pallas_call(
        _filter_kernel,
        out_shape=(jax.ShapeDtypeStruct((2, L, HY_CH), h_dtype), jax.ShapeDtypeStruct((1, HY_CH), F32)),
        grid=(L // tl,),
        in_specs=[pl.BlockSpec((tl, LANE), lambda i: (i, 0)),
                  _const_spec(w1p.shape), _const_spec(b1.shape), _const_spec(fr.shape),
                  _const_spec(w2.shape), _const_spec(b2.shape), _const_spec(w3.shape),
                  _const_spec(deltas.shape)],
        out_specs=(pl.BlockSpec((2, tl, HY_CH), lambda i: (0, i, 0)),
                   pl.BlockSpec((1, HY_CH), lambda i: (0, 0))),
        compiler_params=_cparams(("arbitrary",)),
        name="l0_hyena_filters",
    )(z, w1p, b1, fr, w2, b2, w3, deltas)


def _dft_tables(kind, L, ti):
    ni = L // ti
    i = np.arange(ti, dtype=np.int64)[:, None]
    big = (np.arange(ni, dtype=np.int64) * ti)[:, None]
    c = np.arange(L, dtype=np.int64)[None, :]
    if kind == "hy_fwd":
        period = 4 * L
        base_idx = (2 * i + 1) * c
        r_idx = 2 * big * c
        scale = 1.0
    elif kind == "hy_inv":
        period = 4 * L
        base_idx = (2 * c + 1) * i
        r_idx = (2 * c + 1) * big
        scale = 1.0 / L
    else:
        period = L
        base_idx = i * c
        r_idx = big * c
        scale = 1.0 / math.sqrt(L * FN_GROUP_CH)
    ab = 2.0 * np.pi * (base_idx % period) / period
    ar = 2.0 * np.pi * (r_idx % period) / period
    return (jnp.asarray(np.cos(ab), F32), jnp.asarray(np.sin(ab), F32),
            jnp.asarray(scale * np.cos(ar), F32).reshape(ni, 1, L),
            jnp.asarray(scale * np.sin(ar), F32).reshape(ni, 1, L))


def _dft_kernel(mode, nb, n_x, *refs):
    bc_ref, bs_ref, rc_ref, rs_ref = refs[:4]
    x_refs = refs[4:4 + n_x]
    rest = refs[4 + n_x:]
    p_ref, q_ref = rest[-2], rest[-1]
    j = pl.program_id(2)
    nj = pl.num_programs(2)
    tj = x_refs[0].shape[1]
    if bc_ref.shape[1] == tj:
        bc, bs, rc, rs = bc_ref[...], bs_ref[...], rc_ref[...], rs_ref[...]
    else:
        off = pl.multiple_of(j * tj, tj)
        bc, bs = bc_ref[:, pl.ds(off, tj)], bs_ref[:, pl.ds(off, tj)]
        rc, rs = rc_ref[:, pl.ds(off, tj)], rs_ref[:, pl.ds(off, tj)]
    tc = (bc * rc - bs * rs).astype(BF16)
    ts = (bs * rc + bc * rs).astype(BF16)
    x1_ref = x_refs[0]
    x2_ref = x_refs[-1]

    pq = [(jnp.dot(tc, x1_ref[b], preferred_element_type=F32),
           jnp.dot(ts, x2_ref[b], preferred_element_type=F32)) for b in range(nb)]

    @pl.when(j == 0)
    def _():
        for b in range(nb):
            p_ref[b] = pq[b][0]
            q_ref[b] = pq[b][1]

    @pl.when(j > 0)
    def _():
        for b in range(nb):
            p_ref[b] += pq[b][0]
            q_ref[b] += pq[b][1]

    @pl.when(j == nj - 1)
    def _():
        if mode == "filt":
            nrm = rest[0][...]
            kre_ref, kim_ref = rest[1], rest[2]
            kre_ref[...] = (p_ref[0] + p_ref[1]) / nrm
            kim_ref[...] = (q_ref[1] - q_ref[0]) / nrm
        elif mode == "fwdk":
            kre, kim = rest[0][...], rest[1][...]
            yre_ref, yim_ref = rest[2], rest[3]
            for b in range(nb):
                pp, qq = p_ref[b], q_ref[b]
                yre_ref[b] = (pp * kre + qq * kim).astype(BF16)
                yim_ref[b] = (pp * kim - qq * kre).astype(BF16)
        elif mode == "inv":
            e_ref, x0_ref, o_ref = rest[0], rest[1], rest[2]
            for b in range(nb):
                o_ref[b] = ((p_ref[b] - q_ref[b] + e_ref[b]) * x0_ref[b]).astype(BF16)
        else:
            o_ref = rest[0]
            for b in range(nb):
                o_ref[b] = (p_ref[b] - q_ref[b]).astype(BF16)


def _dft(mode, kind, xs, extras, nb):
    B, L, C = xs[0].shape
    ti = min(L, 256)
    tj = min(L, 512)
    bc, bs, rc, rs = _dft_tables(kind, L, ti)
    grid = (B // nb, L // ti, L // tj)
    x_spec = pl.BlockSpec((nb, tj, C), lambda g, i, j: (g, j, 0))
    row_spec = lambda c, dt=None: pl.BlockSpec((nb, ti, c), lambda g, i, j: (g, i, 0))
    in_specs = [pl.BlockSpec((ti, L), lambda g, i, j: (0, 0)),
                pl.BlockSpec((ti, L), lambda g, i, j: (0, 0)),
                pl.BlockSpec((None, 1, L), lambda g, i, j: (i, 0, 0)),
                pl.BlockSpec((None, 1, L), lambda g, i, j: (i, 0, 0))] + [x_spec] * len(xs)
    if mode == "filt":
        in_specs += [pl.BlockSpec((1, HY_CH), lambda g, i, j: (0, 0))]
        out_shape = (jax.ShapeDtypeStruct((L, HY_CH), F32),) * 2
        out_specs = (pl.BlockSpec((ti, HY_CH), lambda g, i, j: (i, 0)),) * 2
    elif mode == "fwdk":
        in_specs += [pl.BlockSpec((ti, HY_CH), lambda g, i, j: (i, 0))] * 2
        out_shape = (jax.ShapeDtypeStruct((B, L, C), BF16),) * 2
        out_specs = (row_spec(C),) * 2
    elif mode == "inv":
        in_specs += [row_spec(C)] * 2
        out_shape = jax.ShapeDtypeStruct((B, L, C), BF16)
        out_specs = row_spec(C)
    else:
        out_shape = jax.ShapeDtypeStruct((B, L, C), BF16)
        out_specs = row_spec(C)
    return pl.pallas_call(
        functools.partial(_dft_kernel, mode, nb, len(xs)),
        out_shape=out_shape,
        grid=grid,
        in_specs=in_specs,
        out_specs=out_specs,
        scratch_shapes=[pltpu.VMEM((nb, ti, C), F32), pltpu.VMEM((nb, ti, C), F32)],
        compiler_params=_cparams(("arbitrary", "arbitrary", "arbitrary")),
        name="dft_" + mode,
    )(bc, bs, rc, rs, *xs, *extras)


FFT_R = 64
HYENA_KF = 16
FNET_KF = 16


def _pack_pairs(x):
    return pltpu.bitcast(x.astype(BF16), jnp.uint32)


def _unpack_pairs(w):
    return pltpu.bitcast(w, BF16)


def _to_blocks(w):
    return jnp.swapaxes(w.reshape(FFT_R, w.shape[-2], w.shape[-1]), 0, 1)


def _from_blocks(ws):
    kf, c = len(ws), ws[0].shape[-1]
    return jnp.swapaxes(jnp.stack(ws, axis=0), 0, 1).reshape(FFT_R // kf, kf, kf, c)


def _lead_in_kernel(g_ref, x_ref, o_ref):
    g = g_ref[...]
    xt = jnp.swapaxes(x_ref[...], 0, 1)
    for j in range(x_ref.shape[1]):
        o_ref[j] = _pack_pairs(jnp.dot(g, xt[j].astype(BF16), preferred_element_type=F32))


def _lead_in(g, x, kf, name):
    nbx, _, _, c = x.shape
    m2 = g.shape[0] // 2
    return pl.pallas_call(
        _lead_in_kernel,
        out_shape=jax.ShapeDtypeStruct((nbx, FFT_R // kf, kf, m2, c), jnp.uint32),
        grid=(nbx, FFT_R // kf),
        in_specs=[pl.BlockSpec(g.shape, lambda b, k: (0, 0)),
                  pl.BlockSpec((None, FFT_R, kf, c), lambda b, k: (b, 0, k, 0))],
        out_specs=pl.BlockSpec((None, None, kf, m2, c), lambda b, k: (b, k, 0, 0, 0)),
        compiler_params=_cparams(("arbitrary", "arbitrary")),
        name=name,
    )(g, x)


def _lead_out_kernel(n_extra, g_ref, w_ref, *rest):
    g = g_ref[...]
    o_ref = rest[-1]
    ys = [jnp.dot(g, _unpack_pairs(w_ref[j]), preferred_element_type=F32) for j in range(w_ref.shape[0])]
    y = jnp.swapaxes(jnp.stack(ys, axis=0), 0, 1)
    if n_extra:
        y = (y + rest[0][...]) * rest[1][...]
    o_ref[...] = y


def _lead_out(g, w, extras, name):
    nb, nk, kf, k2, c = w.shape
    blk = pl.BlockSpec((None, FFT_R, kf, c), lambda b, k: (b, 0, k, 0))
    return pl.pallas_call(
        functools.partial(_lead_out_kernel, len(extras)),
        out_shape=jax.ShapeDtypeStruct((nb, FFT_R, FFT_R, c), F32),
        grid=(nb, nk),
        in_specs=[pl.BlockSpec(g.shape, lambda b, k: (0, 0)),
                  pl.BlockSpec((None, None, kf, k2, c), lambda b, k: (b, k, 0, 0, 0))] + [blk] * len(extras),
        out_specs=blk,
        compiler_params=_cparams(("arbitrary", "arbitrary")),
        name=name,
    )(g, w, *extras)


def _interleave(a, b, axis):
    st = np.stack([a, b], axis=axis + 1)
    shape = list(a.shape)
    shape[axis] *= 2
    return st.reshape(shape)


def _hy2_tables():
    L = FFT_R * FFT_R
    n2 = 2 * L
    f1 = np.arange(2 * FFT_R, dtype=np.int64)
    s1 = np.arange(FFT_R, dtype=np.int64)
    th = np.pi * (((2 * f1[:, None] + 1) * s1[None, :]) % (4 * FFT_R)) / (2 * FFT_R)
    ga = _interleave(np.cos(th), -np.sin(th), 0)
    ma = _interleave(np.cos(th).T, -np.sin(th).T, 1) / L
    f2 = np.arange(FFT_R // 2, dtype=np.int64)
    s2 = np.arange(FFT_R, dtype=np.int64)
    idx = ((n2 // FFT_R) * 2 * f2[None, :, None] * s2[None, None, :]
           + (2 * f1[:, None, None] + 1) * s2[None, None, :]) % (2 * n2)
    al = np.pi * idx / n2
    c, s = np.cos(al), np.sin(al)
    nmat = np.concatenate([_interleave(c, s, 2), _interleave(-s, c, 2)], axis=1)
    ct, st = np.transpose(c, (0, 2, 1)), np.transpose(s, (0, 2, 1))
    mmat = _interleave(np.concatenate([ct, -st], axis=2), np.concatenate([st, ct], axis=2), 1)
    bf = lambda a: jnp.asarray(a, F32).astype(BF16)
    return bf(ga), bf(nmat), bf(mmat), bf(ma)


def _hy_mid_kernel(a_ref, n_ref, m_ref, k_ref, e_ref):
    half = FFT_R // 2
    kf = a_ref.shape[-2]
    a = _to_blocks(a_ref[...])
    ts = [jnp.dot(n_ref[j], _unpack_pairs(a[j]), preferred_element_type=F32)
          for j in range(kf)]
    ys = []
    for j in range(kf):
        tr, ti = ts[j][:half], ts[j][half:]
        kr, ki = k_ref[j, 0].astype(F32), k_ref[j, 1].astype(F32)
        ys.append(jnp.concatenate([tr * kr - ti * ki, tr * ki + ti * kr], axis=0).astype(BF16))
    e_ref[...] = _from_blocks([_pack_pairs(jnp.dot(m_ref[j], ys[j], preferred_element_type=F32))
                               for j in range(kf)])


def _hy_mid(a, nmat, mmat, khat):
    nb, nk, kf, _, c = a.shape
    nf1 = 2 * FFT_R
    blk = pl.BlockSpec((None, nk, kf, kf, c), lambda i, b: (b, 0, 0, i, 0))
    return pl.pallas_call(
        _hy_mid_kernel,
        out_shape=jax.ShapeDtypeStruct(a.shape, jnp.uint32),
        grid=(nf1 // kf, nb),
        in_specs=[blk,
                  pl.BlockSpec((kf, FFT_R, 2 * FFT_R), lambda i, b: (i, 0, 0)),
                  pl.BlockSpec((kf, 2 * FFT_R, FFT_R), lambda i, b: (i, 0, 0)),
                  pl.BlockSpec((kf, 2, FFT_R // 2, c), lambda i, b: (i, 0, 0, 0))],
        out_specs=blk,
        compiler_params=_cparams(("arbitrary", "arbitrary")),
        name="l0_hyena_mid",
    )(a, nmat, mmat, khat)


def _hy_kfilt_kernel(a_ref, n_ref, nrm_ref, k_ref):
    half = FFT_R // 2
    nrm = nrm_ref[...]
    af, ab = _to_blocks(a_ref[0]), _to_blocks(a_ref[1])
    for j in range(a_ref.shape[-2]):
        tf = jnp.dot(n_ref[j], _unpack_pairs(af[j]), preferred_element_type=F32)
        tb = jnp.dot(n_ref[j], _unpack_pairs(ab[j]), preferred_element_type=F32)
        k_ref[j, 0] = ((tf[:half] + tb[:half]) / nrm).astype(k_ref.dtype)
        k_ref[j, 1] = ((tf[half:] - tb[half:]) / nrm).astype(k_ref.dtype)


def _hy_kfilt(a, nmat, nrm):
    _, nk, kf, _, c = a.shape
    nf1 = 2 * FFT_R
    return pl.pallas_call(
        _hy_kfilt_kernel,
        out_shape=jax.ShapeDtypeStruct((nf1, 2, FFT_R // 2, c), BF16),
        grid=(nf1 // kf,),
        in_specs=[pl.BlockSpec((2, nk, kf, kf, c), lambda i: (0, 0, 0, i, 0)),
                  pl.BlockSpec((kf, FFT_R, 2 * FFT_R), lambda i: (i, 0, 0)),
                  _const_spec(nrm.shape)],
        out_specs=pl.BlockSpec((kf, 2, FFT_R // 2, c), lambda i: (i, 0, 0, 0)),
        compiler_params=_cparams(("arbitrary",)),
        name="l0_hyena_kfilt",
    )(a, nmat, nrm)


def _hyena_long(u, e, x0, hfilt, nrm, nb):
    L = FFT_R * FFT_R
    c = u.shape[-1]
    v4 = lambda a, n: a.reshape(n, FFT_R, FFT_R, c)
    ga, nmat, mmat, ma = _hy2_tables()
    khat = _hy_kfilt(_lead_in(ga, v4(hfilt, 2), HYENA_KF, "l0_hyena_fwd_a"), nmat, nrm)
    ee = _hy_mid(_lead_in(ga, v4(u, nb), HYENA_KF, "l0_hyena_fwd_a"), nmat, mmat, khat)
    y = _lead_out(ma, ee, [v4(e, nb), v4(x0, nb)], "l0_hyena_inv_a")
    return y.reshape(nb * L, c)


def _fn2_tables():
    L = FFT_R * FFT_R
    r = np.arange(FFT_R, dtype=np.int64)
    idx = (FFT_R * r[None, :, None] * r[None, None, :] + r[None, :, None] * r[:, None, None]) % L
    gm = 2.0 * np.pi * idx / L
    c, s = np.cos(gm), np.sin(gm)
    g1 = _interleave(np.concatenate([c, -s], axis=2), np.concatenate([-s, -c], axis=2), 1)
    dl = 2.0 * np.pi * ((r[:, None] * r[None, :]) % FFT_R) / FFT_R
    g2 = _interleave(np.cos(dl), np.sin(dl), 1) / math.sqrt(L * FN_GROUP_CH)
    bf = lambda a: jnp.asarray(a, F32).astype(BF16)
    return bf(g1), bf(g2)


def _fnet_s1_kernel(x_ref, m_ref, cs_ref, g1_ref, o_ref, zc_ref, zs_ref):
    xs = jnp.swapaxes(x_ref[...], 0, 1).reshape(FNET_KF * FFT_R, D_MODEL)
    m = m_ref[...]
    h = (_ln_plain(xs) * (1.0 + m[:, D_MODEL:2 * D_MODEL]) + m[:, 0:D_MODEL]).astype(BF16)
    cs = cs_ref[...]
    for g in range(D_MODEL // FN_GROUP_CH):
        a = g * FN_GROUP_CH
        z = jnp.dot(h[:, a:a + FN_GROUP_CH], cs, preferred_element_type=F32)
        zc_ref[:, a:a + FN_GROUP_CH] = z[:, :FN_GROUP_CH].astype(BF16)
        zs_ref[:, a:a + FN_GROUP_CH] = z[:, FN_GROUP_CH:].astype(BF16)
    ws = []
    for j in range(FNET_KF):
        r0 = j * FFT_R
        s = jnp.concatenate([zc_ref[r0:r0 + FFT_R, :], zs_ref[r0:r0 + FFT_R, :]], axis=0)
        ws.append(_pack_pairs(jnp.dot(g1_ref[j], s, preferred_element_type=F32)))
    o_ref[...] = _from_blocks(ws)


def _fnet_long(x, mods, mod_base, nb):
    L = FFT_R * FFT_R
    d = D_MODEL
    g1, g2 = _fn2_tables()
    cs = _group_dft_table()
    kf = FNET_KF
    bb = pl.pallas_call(
        _fnet_s1_kernel,
        out_shape=jax.ShapeDtypeStruct((nb, FFT_R // kf, kf, FFT_R, d), jnp.uint32),
        grid=(nb, FFT_R // kf),
        in_specs=[pl.BlockSpec((None, FFT_R, kf, d), lambda b, k: (b, 0, k, 0)),
                  pl.BlockSpec((None, 1, 6 * d), lambda b, k: (mod_base + b, 0, 0)),
                  pl.BlockSpec(cs.shape, lambda b, k: (0, 0)),
                  pl.BlockSpec((kf, 2 * FFT_R, 2 * FFT_R), lambda b, k: (k, 0, 0))],
        out_specs=pl.BlockSpec((None, FFT_R // kf, kf, kf, d), lambda b, k: (b, 0, 0, k, 0)),
        scratch_shapes=[pltpu.VMEM((kf * FFT_R, d), BF16), pltpu.VMEM((kf * FFT_R, d), BF16)],
        compiler_params=_cparams(("arbitrary", "arbitrary")),
        name="l1_fnet_stage1",
    )(x.reshape(nb, FFT_R, FFT_R, d), mods, cs, g1)
    y = _lead_out(g2, bb, [], "l1_fnet_stage2")
    return y.reshape(nb * L, d)


def _group_dft_table():
    g = FN_GROUP_CH
    jk = (np.arange(g, dtype=np.int64)[:, None] * np.arange(g, dtype=np.int64)[None, :]) % g
    ang = 2.0 * np.pi * jk / g
    return jnp.asarray(np.concatenate([np.cos(ang), np.sin(ang)], axis=1), F32).astype(BF16)


def _fnet_front_kernel(x_ref, m_ref, cs_ref, zc_ref, zs_ref):
    m = m_ref[...]
    h = (_ln_plain(x_ref[...]) * (1.0 + m[:, D_MODEL:2 * D_MODEL]) + m[:, 0:D_MODEL]).astype(BF16)
    cs = cs_ref[...]
    for g in range(D_MODEL // FN_GROUP_CH):
        a = g * FN_GROUP_CH
        z = jnp.dot(h[:, a:a + FN_GROUP_CH], cs, preferred_element_type=F32)
        zc_ref[:, a:a + FN_GROUP_CH] = z[:, :FN_GROUP_CH].astype(BF16)
        zs_ref[:, a:a + FN_GROUP_CH] = z[:, FN_GROUP_CH:].astype(BF16)


def _fnet_front(x, mods, mod_base, tiles_per_mod):
    t = x.shape[0]
    tm = 2 * ROW_TILE
    tiles_per_mod = max(tiles_per_mod // 2, 1)
    cs = _group_dft_table()
    return pl.pallas_call(
        _fnet_front_kernel,
        out_shape=(jax.ShapeDtypeStruct((t, D_MODEL), BF16),) * 2,
        grid=(t // tm,),
        in_specs=[pl.BlockSpec((tm, D_MODEL), lambda i: (i, 0)),
                  _mod_spec(mod_base, tiles_per_mod),
                  _const_spec(cs.shape)],
        out_specs=(pl.BlockSpec((tm, D_MODEL), lambda i: (i, 0)),) * 2,
        compiler_params=_cparams(("arbitrary",)),
        name="l1_fnet_front",
    )(x, mods, cs)


def _post_kernel(n_a, *refs):
    x_ref, m_ref = refs[0], refs[1]
    a_refs = refs[2:2 + n_a]
    wo_refs = refs[2 + n_a:2 + 2 * n_a]
    g1_ref, b1_ref, w1_ref, w2_ref, g2_ref, b2_ref, o_ref = refs[2 + 2 * n_a:]
    m = m_ref[...]
    d = D_MODEL
    tm = x_ref.shape[0]
    halves = [(r, r + tm // POST_SPLIT) for r in range(0, tm, tm // POST_SPLIT)]
    outs = []
    for r0, r1 in halves:
        out = _bdot(a_refs[0][r0:r1, :], wo_refs[0][...])
        for a_ref, wo_ref in zip(a_refs[1:], wo_refs[1:]):
            out += _bdot(a_ref[r0:r1, :], wo_ref[...])
        outs.append(out)
    x1s, hs = [], []
    for (r0, r1), out in zip(halves, outs):
        x1 = _ln_plain(ALPHA * x_ref[r0:r1, :] + m[:, 2 * d:3 * d] * out) * g1_ref[...] + b1_ref[...]
        x1s.append(x1)
        hs.append((_ln_plain(x1) * (1.0 + m[:, 4 * d:5 * d]) + m[:, 3 * d:4 * d]).astype(BF16))
    accs = []
    n_c = D_FF // d

    def up(h, c):
        hc = jnp.maximum(jnp.dot(h, w1_ref[:, c * d:(c + 1) * d], preferred_element_type=F32), 0.0)
        return (hc * hc).astype(BF16)

    for h in hs:
        acc = None
        nxt = up(h, 0)
        for c in range(n_c):
            cur = nxt
            if c + 1 < n_c:
                nxt = up(h, c + 1)
            part = jnp.dot(cur, w2_ref[c * d:(c + 1) * d, :], preferred_element_type=F32)
            acc = part if acc is None else acc + part
        accs.append(acc)
    for (r0, r1), x1, acc in zip(halves, x1s, accs):
        o_ref[r0:r1, :] = _ln_plain(ALPHA * x1 + m[:, 5 * d:6 * d] * acc) * g2_ref[...] + b2_ref[...]


def _post(x, mods, mod_base, tiles_per_mod, a_list, wo_list, g1, b1, w1, w2, g2, b2):
    t = x.shape[0]
    tm = POST_TILE
    row = lambda c: pl.BlockSpec((tm, c), lambda i: (i, 0))
    once = lambda v: pl.BlockSpec(v.shape, lambda i: (0,) * v.ndim, pipeline_mode=pl.Buffered(1))
    in_specs = ([row(D_MODEL), _mod_spec(mod_base, tiles_per_mod * ROW_TILE // tm)]
                + [row(a.shape[1]) for a in a_list]
                + [once(w) for w in wo_list]
                + [once(v) for v in (g1, b1, w1, w2, g2, b2)])
    return pl.pallas_call(
        functools.partial(_post_kernel, len(a_list)),
        out_shape=jax.ShapeDtypeStruct((t, D_MODEL), F32),
        grid=(t // tm,),
        in_specs=in_specs,
        out_specs=row(D_MODEL),
        compiler_params=_cparams(("arbitrary",)),
        name="post_mlp",
    )(x, mods, *a_list, *wo_list, g1, b1, w1, w2, g2, b2)


def _rot_cols(w):
    parts = []
    for seg in range(2):
        o = seg * 32
        parts += [-w[:, o + 16:o + 32], w[:, o:o + 16]]
    return jnp.concatenate(parts, axis=1)


def _pad_cols(w, n):
    return jnp.pad(w, ((0, 0), (0, n - w.shape[1])))


def _block_diag2(w):
    z = jnp.zeros_like(w)
    return jnp.concatenate([jnp.concatenate([w, z], axis=1), jnp.concatenate([z, w], axis=1)], axis=0)


def _rope_tables(L):
    rows = L // GRID_W
    row = np.repeat(np.arange(rows, dtype=np.float64), GRID_W)
    col = np.tile(np.arange(GRID_W, dtype=np.float64), rows)
    half = QK_ROPE // 2
    inv = 1.0 / (ROPE_THETA ** (np.arange(0, half, 2, dtype=np.float64) / half))
    ar = row[:, None] * inv[None, :]
    ac = col[:, None] * inv[None, :]
    ang = np.concatenate([ar, ar, ac, ac], axis=1)
    cos = np.concatenate([np.cos(ang), np.ones_like(ang)], axis=1)
    sin = np.concatenate([np.sin(ang), np.zeros_like(ang)], axis=1)
    return jnp.asarray(cos, F32), jnp.asarray(sin, F32)


def kernel(x_prompt, x_sample, cache_l0_ckv, cache_l0_krope, c, c_ctx, l0_ada_w, l0_ada_b, l0_w_in, l0_conv_w, l0_conv_b, l0_hf_w1, l0_hf_b1, l0_hf_freq, l0_hf_w2, l0_hf_b2, l0_hf_w3, l0_hf_skip, l0_q_norm, l0_q_up, l0_kv_norm, l0_kv_up, l0_w_out, l0_ln1_g, l0_ln1_b, l0_mlp_w1, l0_mlp_w2, l0_ln2_g, l0_ln2_b, l1_ada_w, l1_ada_b, l1_w_out, l1_ln1_g, l1_ln1_b, l1_mlp_w1, l1_mlp_w2, l1_ln2_g, l1_ln2_b):
    nbc, lc, d = x_prompt.shape
    nbs, ls, _ = x_sample.shape
    past = cache_l0_ckv.shape[1]
    tm = ROW_TILE
    row1 = lambda v: v.reshape(1, -1)

    cond8 = jnp.concatenate([c_ctx[None, :], c, jnp.zeros((8 - 1 - nbs, d), F32)], axis=0)
    mods0 = _modulation(cond8, l0_ada_w, l0_ada_b)
    mods1 = _modulation(cond8, l1_ada_w, l1_ada_b)

    kpe_w = l0_w_in[:, 1920:1984]
    win = jnp.concatenate([l0_w_in[:, :1920], _pad_cols(kpe_w, LANE), _pad_cols(_rot_cols(kpe_w), LANE)],
                          axis=1).astype(BF16)
    dh = QK_NOPE + QK_ROPE
    q_nope = [l0_q_up[:, h * dh:h * dh + QK_NOPE] for h in range(MLA_HEADS)]
    q_pe = [l0_q_up[:, h * dh + QK_NOPE:(h + 1) * dh] for h in range(MLA_HEADS)]
    qup = jnp.concatenate(q_nope + [_pad_cols(w, LANE) for w in q_pe]
                          + [_pad_cols(_rot_cols(w), LANE) for w in q_pe], axis=1).astype(BF16)
    kvup = l0_kv_up.astype(BF16)
    front_w = (win, row1(l0_q_norm), qup, row1(l0_kv_norm), kvup, l0_conv_w, row1(l0_conv_b), row1(l0_hf_skip))
    w1p = jnp.pad(l0_hf_w1, ((0, LANE - l0_hf_w1.shape[0]), (0, 0)))
    two = lambda v: jnp.tile(row1(v), (1, 2))
    filt_w = (_block_diag2(w1p), two(l0_hf_b1), two(l0_hf_freq), _block_diag2(l0_hf_w2), two(l0_hf_b2),
              _block_diag2(l0_hf_w3).astype(BF16))
    wo0 = l0_w_out.astype(BF16)

    xc = x_prompt.reshape(nbc * lc, d)
    xs = x_sample.reshape(nbs * ls, d)
    groups = (
        dict(x=xc, nb=nbc, L=lc, mod_base=0, tiles_per_mod=nbc * lc // tm, dft_nb=8, tq=lc, hps=MLA_HEADS),
        dict(x=xs, nb=nbs, L=ls, mod_base=1, tiles_per_mod=ls // tm, dft_nb=nbs, tq=512, hps=1),
    )
    ones_tab = (jnp.ones((FRONT_TILE, LANE), F32), jnp.zeros((FRONT_TILE, LANE), F32))

    outs = []
    ctx_ckv = ctx_krope = None
    for gi, g in enumerate(groups):
        nb, L = g["nb"], g["L"]
        tiles_per_seq = L // tm
        latent = gi == 1
        cos, sin = _rope_tables(L) if latent else ones_tab
        two_stage = L == FFT_R * FFT_R
        io_dtype = F32 if two_stage else BF16
        u, e, x0, q, k, vt, kvn, kpe = _front(g["x"], mods0, g["mod_base"], g["tiles_per_mod"] * tm, front_w,
                                              cos, sin, latent, L, min(L, FRONT_TILE), io_dtype)
        if latent:
            extra = _cache_kv(cache_l0_ckv.reshape(nbs * past, KV_LORA),
                              _pad_cols(cache_l0_krope.reshape(nbs * past, QK_ROPE), LANE), kvup)
        else:
            extra = None
            ctx_ckv = kvn.reshape(nb, L, KV_LORA)
            ctx_krope = kpe.reshape(nb, L, QK_ROPE)
        y_mla = _attention(q, k, vt, extra, nb, L, g["tq"], g["hps"])

        hfilt, hnorm = _filters(L, io_dtype, *filt_w)
        if two_stage:
            y_hy = _hyena_long(u, e, x0, hfilt, hnorm, nb)
        else:
            kre, kim = _dft("filt", "hy_fwd", [hfilt], [hnorm], 2)
            sh = (nb, L, HY_CH)
            yre, yim = _dft("fwdk", "hy_fwd", [u.reshape(sh)], [kre, kim], g["dft_nb"])
            y_hy = _dft("inv", "hy_inv", [yre, yim], [e.reshape(sh), x0.reshape(sh)], g["dft_nb"])
            y_hy = y_hy.reshape(nb * L, HY_CH)

        x1 = _post(g["x"], mods0, g["mod_base"], g["tiles_per_mod"], [y_hy, y_mla],
                   [wo0[:HY_CH], wo0[HY_CH:]], row1(l0_ln1_g), row1(l0_ln1_b),
                   l0_mlp_w1.astype(BF16), l0_mlp_w2.astype(BF16), row1(l0_ln2_g), row1(l0_ln2_b))

        if L == FFT_R * FFT_R:
            yf = _fnet_long(x1, mods1, g["mod_base"], nb)
        else:
            zc, zs = _fnet_front(x1, mods1, g["mod_base"], g["tiles_per_mod"])
            sh = (nb, L, d)
            yf = _dft("fnet", "fnet", [zc.reshape(sh), zs.reshape(sh)], [], g["dft_nb"] // 2)
            yf = yf.reshape(nb * L, d)
        x2 = _post(x1, mods1, g["mod_base"], g["tiles_per_mod"], [yf],
                   [l1_w_out.astype(BF16)], row1(l1_ln1_g), row1(l1_ln1_b),
                   l1_mlp_w1.astype(BF16), l1_mlp_w2.astype(BF16), row1(l1_ln2_g), row1(l1_ln2_b))
        outs.append(x2.reshape(nb, L, d))

    return (outs[0], outs[1], ctx_ckv, ctx_krope)
```

```python
import functools
import math

import numpy as np
import jax
import jax.numpy as jnp
from jax import lax
from jax.experimental import pallas as pl
from jax.experimental.pallas import tpu as pltpu

F32 = jnp.float32
BF16 = jnp.bfloat16
HI = lax.Precision.HIGHEST

D_MODEL = 1024
DEPTH = 2
GRID_W = 64
HY_CH = 512
FILT_BANDS = 16
FILT_ORDER = 64
FAST_DECAY_PCT = 0.3
SLOW_DECAY_PCT = 1.5
DECAY_TARGET = 1e-2
MAX_DECAY = math.log(DECAY_TARGET) / FAST_DECAY_PCT
MIN_DECAY = math.log(DECAY_TARGET) / SLOW_DECAY_PCT
MLA_HEADS = 4
QK_NOPE = 128
QK_ROPE = 64
V_HEAD = 128
Q_LORA = 256
KV_LORA = 128
ROPE_THETA = 10000.0
FN_GROUP_CH = 128
D_FF = 4096
ALPHA = (2 * DEPTH) ** 0.25
LN_EPS = 1e-5
RMS_EPS = 1e-6

LANE = 128
ROW_TILE = 256
FRONT_TILE = 512
POST_TILE = 512
POST_SPLIT = 2
QK_PAD = 256
VT_ROWS = V_HEAD + 16
LOG2E = 1.4426950408889634
VMEM_LIMIT = 56 * 1024 * 1024


def _cparams(sem):
    return pltpu.CompilerParams(dimension_semantics=sem, vmem_limit_bytes=VMEM_LIMIT)


def _ln_plain(x):
    mu = jnp.mean(x, axis=-1, keepdims=True)
    xc = x - mu
    var = jnp.mean(xc * xc, axis=-1, keepdims=True)
    return xc * lax.rsqrt(var + LN_EPS)


def _rms(x, g):
    return x * lax.rsqrt(jnp.mean(x * x, axis=-1, keepdims=True) + RMS_EPS) * g


def _bdot(a, b):
    return jnp.dot(a.astype(BF16), b, preferred_element_type=F32)


def _vt_rows(v):
    ones = jnp.ones((VT_ROWS - V_HEAD, v.shape[0]), BF16)
    return jnp.concatenate([jnp.transpose(v).astype(BF16), ones], axis=0)


def _mod_kernel(c_ref, w_ref, b_ref, o_ref):
    c = c_ref[...]
    s = c / (1.0 + jnp.exp(-c))
    s_hi = s.astype(BF16)
    s_lo = (s - s_hi.astype(F32)).astype(BF16)
    s2 = jnp.concatenate([s_hi, s_lo], axis=0)
    nr = s.shape[0]
    w = w_ref[...]
    w_hi = w.astype(BF16)
    w_lo = (w - w_hi.astype(F32)).astype(BF16)
    r1 = jnp.dot(s2, w_hi, preferred_element_type=F32)
    r2 = jnp.dot(s_hi, w_lo, preferred_element_type=F32)
    o_ref[...] = r1[:nr] + r1[nr:] + r2 + b_ref[...]


def _modulation(cond8, w, b):
    n = w.shape[1]
    tn = 1536
    out = pl.pallas_call(
        _mod_kernel,
        out_shape=jax.ShapeDtypeStruct((8, n), F32),
        grid=(n // tn,),
        in_specs=[pl.BlockSpec((8, D_MODEL), lambda j: (0, 0)),
                  pl.BlockSpec((D_MODEL, tn), lambda j: (0, j)),
                  pl.BlockSpec((1, tn), lambda j: (0, j))],
        out_specs=pl.BlockSpec((8, tn), lambda j: (0, j)),
        compiler_params=_cparams(("arbitrary",)),
        name="modulation",
    )(cond8, w, b.reshape(1, n))
    return out.reshape(8, 1, n)


def _mod_spec(mod_base, tiles_per_mod):
    return pl.BlockSpec((None, 1, 6 * D_MODEL), lambda i: (mod_base + i // tiles_per_mod, 0, 0))


def _const_spec(shape):
    nd = len(shape)
    return pl.BlockSpec(shape, lambda i: (0,) * nd)


HALO = 8


def _front_kernel(tiles_per_seq, x_ref, xp_ref, xn_ref, m_ref, win_ref, qn_ref, qup_ref, kvn_ref, kvup_ref,
                  cos_ref, sin_ref, cw_ref, cb_ref,
                  u_ref, x0_ref, q_ref, k_ref, v_ref, kvn_out_ref, kpe_ref):
    i = pl.program_id(0)
    m = m_ref[...]
    tm = x_ref.shape[0]
    nh = 3 * HY_CH
    xe = jnp.concatenate([xp_ref[...], x_ref[...], xn_ref[...]], axis=0)
    he = _ln_plain(xe) * (1.0 + m[:, D_MODEL:2 * D_MODEL]) + m[:, 0:D_MODEL]

    z = _bdot(he[HALO:HALO + tm], win_ref[:, nh:])
    zh = _bdot(he, win_ref[:, :nh])
    q_c = z[:, 0:256]
    kv_c = z[:, 256:384]
    cos = cos_ref[...]
    sin = sin_ref[...]
    kpe = z[:, 384:512] * cos + z[:, 512:640] * sin
    kpe_ref[...] = kpe[:, :QK_ROPE]
    kpe_b = kpe.astype(BF16)
    q = _bdot(_rms(q_c, qn_ref[...]), qup_ref[...]) * (LOG2E / math.sqrt(QK_NOPE + QK_ROPE))
    kvn = _rms(kv_c, kvn_ref[...])
    kvn_out_ref[...] = kvn
    kv = _bdot(kvn, kvup_ref[...])

    pos = i % tiles_per_seq
    rows = lax.broadcasted_iota(jnp.int32, (tm + 2 * HALO, 1), 0)
    inside = jnp.logical_and(jnp.logical_or(rows >= HALO, pos != 0),
                             jnp.logical_or(rows < tm + HALO, pos != tiles_per_seq - 1))
    zh = jnp.where(inside, zh, 0.0)
    cw = cw_ref[...]
    pz = (pltpu.roll(zh, 1, 0) * cw[0:1, :] + zh * cw[1:2, :]
          + pltpu.roll(zh, tm + 2 * HALO - 1, 0) * cw[2:3, :])[HALO:HALO + tm] + cb_ref[...]
    u_ref[...] = pz[:, 2 * HY_CH:] * pz[:, HY_CH:2 * HY_CH]
    x0_ref[...] = pz[:, :HY_CH]

    for hd in range(MLA_HEADS):
        a = hd * LANE
        q_pe = (q[:, 512 + a:512 + a + LANE] * cos + q[:, 1024 + a:1024 + a + LANE] * sin).astype(BF16)
        q_ref[hd] = jnp.concatenate([q[:, a:a + LANE].astype(BF16), q_pe], axis=-1)
        k_ref[hd] = jnp.concatenate([kv[:, 2 * a:2 * a + LANE].astype(BF16), kpe_b], axis=-1)
        v_ref[hd] = _vt_rows(kv[:, 2 * a + LANE:2 * a + 2 * LANE])


def _front(x, mods, mod_base, rows_per_mod, w, cos, sin, rope, seq_len, tm):
    t = x.shape[0]
    tiles_per_seq = seq_len // tm
    tiles_per_mod = rows_per_mod // tm
    win, qn, qup, kvn, kvup, conv_w, conv_b = w
    if rope:
        tab_spec = pl.BlockSpec((tm, LANE), lambda i: (i % tiles_per_seq, 0))
    else:
        tab_spec = pl.BlockSpec((tm, LANE), lambda i: (0, 0))
    r8 = tm // HALO
    n8 = t // HALO
    hy_out = lambda dt: jax.ShapeDtypeStruct((t, HY_CH), dt)
    hy_spec = pl.BlockSpec((tm, HY_CH), lambda i: (i, 0))
    return pl.pallas_call(
        functools.partial(_front_kernel, tiles_per_seq),
        out_shape=(hy_out(F32), hy_out(F32),
                   jax.ShapeDtypeStruct((MLA_HEADS, t, QK_PAD), BF16),
                   jax.ShapeDtypeStruct((MLA_HEADS, t, QK_PAD), BF16),
                   jax.ShapeDtypeStruct((MLA_HEADS, VT_ROWS, t), BF16),
                   jax.ShapeDtypeStruct((t, KV_LORA), F32),
                   jax.ShapeDtypeStruct((t, QK_ROPE), F32)),
        grid=(t // tm,),
        in_specs=[pl.BlockSpec((tm, D_MODEL), lambda i: (i, 0)),
                  pl.BlockSpec((HALO, D_MODEL), lambda i: (jnp.maximum(i * r8 - 1, 0), 0)),
                  pl.BlockSpec((HALO, D_MODEL), lambda i: (jnp.minimum((i + 1) * r8, n8 - 1), 0)),
                  _mod_spec(mod_base, tiles_per_mod),
                  _const_spec(win.shape), _const_spec(qn.shape), _const_spec(qup.shape),
                  _const_spec(kvn.shape), _const_spec(kvup.shape),
                  tab_spec, tab_spec,
                  _const_spec(conv_w.shape), _const_spec(conv_b.shape)],
        out_specs=(hy_spec, hy_spec,
                   pl.BlockSpec((MLA_HEADS, tm, QK_PAD), lambda i: (0, i, 0)),
                   pl.BlockSpec((MLA_HEADS, tm, QK_PAD), lambda i: (0, i, 0)),
                   pl.BlockSpec((MLA_HEADS, VT_ROWS, tm), lambda i: (0, 0, i)),
                   pl.BlockSpec((tm, KV_LORA), lambda i: (i, 0)),
                   pl.BlockSpec((tm, QK_ROPE), lambda i: (i, 0))),
        compiler_params=_cparams(("arbitrary",)),
        name="l0_front",
    )(x, x, x, mods, win, qn, qup, kvn, kvup, cos, sin, conv_w, conv_b)


def _cache_kv_kernel(ckv_ref, kr_ref, kvup_ref, k_ref, v_ref):
    kv = _bdot(ckv_ref[...], kvup_ref[...])
    kr = kr_ref[...].astype(BF16)
    for hd in range(MLA_HEADS):
        a = 2 * hd * LANE
        k_ref[hd] = jnp.concatenate([kv[:, a:a + LANE].astype(BF16), kr], axis=-1)
        v_ref[hd] = _vt_rows(kv[:, a + LANE:a + 2 * LANE])


def _cache_kv(ckv, krope_pad, kvup):
    t = ckv.shape[0]
    return pl.pallas_call(
        _cache_kv_kernel,
        out_shape=(jax.ShapeDtypeStruct((MLA_HEADS, t, QK_PAD), BF16),
                   jax.ShapeDtypeStruct((MLA_HEADS, VT_ROWS, t), BF16)),
        name="l0_cache_kv",
    )(ckv, krope_pad, kvup)


def _col_reduce(x, op):
    rows, n = x.shape
    for g in (32, 8):
        if rows % (8 * g) == 0 and rows > 8 * g:
            x = op(x.reshape(rows // (8 * g), 8 * g, n), axis=0)
            rows = 8 * g
    return op(x, axis=0, keepdims=True)


def _attn_kernel(n_kv, hps, q_ref, *refs):
    k_refs, vt_refs = refs[:n_kv], refs[n_kv:2 * n_kv]
    o_ref, s_even, s_odd, m_even, m_odd = refs[2 * n_kv:]
    i = pl.program_id(0)

    @pl.when(i == 0)
    def _():
        s_odd[...] = jnp.zeros_like(s_odd)
        m_odd[...] = jnp.zeros_like(m_odd)

    def step(s_write, m_write, s_read, m_read):
        nt = (((1,), (1,)), ((), ()))
        for h in range(hps):
            q = q_ref[h]
            r0 = 0
            m = None
            for k_ref in k_refs:
                lk = k_ref.shape[1]
                sblk = lax.dot_general(k_ref[h], q, nt, preferred_element_type=F32)
                s_write[h, r0:r0 + lk, :] = sblk
                mc = _col_reduce(sblk, jnp.max)
                m = mc if m is None else jnp.maximum(m, mc)
                r0 += lk
            m_write[h] = m
        for h in range(hps):
            pb = jnp.exp2(s_read[h] - m_read[h]).astype(BF16)
            acc = None
            r0 = 0
            for vt_ref in vt_refs:
                lk = vt_ref.shape[2]
                pv = jnp.dot(vt_ref[h], pb[r0:r0 + lk, :], preferred_element_type=F32)
                acc = pv if acc is None else acc + pv
                r0 += lk
            o_ref[:, h * V_HEAD:(h + 1) * V_HEAD] = jnp.transpose(
                acc[:V_HEAD] / acc[V_HEAD:V_HEAD + 1]).astype(o_ref.dtype)

    pl.when(i % 2 == 0)(lambda: step(s_even, m_even, s_odd, m_odd))
    pl.when(i % 2 == 1)(lambda: step(s_odd, m_odd, s_even, m_even))


def _attention(q, k, vt, extra, nb, lq, tq, hps):
    nq = lq // tq
    ng = MLA_HEADS // hps
    n_tiles = nb * ng * nq

    def where(t):
        bh = t // nq
        return bh // ng, bh % ng, t % nq

    def score_side(fn):
        return lambda i: fn(*where(jnp.minimum(i, n_tiles - 1)))

    def value_side(fn):
        return lambda i: fn(*where(jnp.maximum(i - 1, 0)))

    ks, vts = [k], [vt]
    if extra is not None:
        ks.append(extra[0])
        vts.append(extra[1])
    in_specs = [pl.BlockSpec((hps, tq, QK_PAD), score_side(lambda b, h, j: (h, b * nq + j, 0)))]
    in_specs += [pl.BlockSpec((hps, a.shape[1] // nb, QK_PAD), score_side(lambda b, h, j: (h, b, 0))) for a in ks]
    in_specs += [pl.BlockSpec((hps, VT_ROWS, a.shape[2] // nb), value_side(lambda b, h, j: (h, 0, b))) for a in vts]
    lk_total = sum(a.shape[1] // nb for a in ks)
    return pl.pallas_call(
        functools.partial(_attn_kernel, len(ks), hps),
        out_shape=jax.ShapeDtypeStruct((nb * lq, MLA_HEADS * V_HEAD), BF16),
        grid=(n_tiles + 1,),
        in_specs=in_specs,
        out_specs=pl.BlockSpec((tq, hps * V_HEAD), value_side(lambda b, h, j: (b * nq + j, h))),
        scratch_shapes=[pltpu.VMEM((hps, lk_total, tq), F32), pltpu.VMEM((hps, lk_total, tq), F32),
                        pltpu.VMEM((hps, 1, tq), F32), pltpu.VMEM((hps, 1, tq), F32)],
        compiler_params=_cparams(("arbitrary",)),
        name="l0_attention",
    )(q, *ks, *vts)


def _filter_kernel(z_ref, w1_ref, b1_ref, fr_ref, w2_ref, b2_ref, w3_ref, dl_ref, h_ref, norm_ref):
    i = pl.program_id(0)
    z = z_ref[...]
    tl = z.shape[0]
    fr = fr_ref[...]
    z2 = jnp.concatenate([z[:tl // 2], z[tl // 2:]], axis=1)
    h = jnp.sin(fr * (jnp.dot(z2, w1_ref[...], precision=HI, preferred_element_type=F32) + b1_ref[...]))
    h = jnp.sin(fr * (jnp.dot(h, w2_ref[...], precision=HI, preferred_element_type=F32) + b2_ref[...]))
    h = _bdot(h, w3_ref[...])
    h = jnp.concatenate([h[:, :2 * HY_CH], h[:, 2 * HY_CH:]], axis=0)
    decay = jnp.exp(-(z[:, 0:1] * dl_ref[...]))
    hf = h[:, :HY_CH] * decay
    hb = h[:, HY_CH:] * decay
    part = jnp.sum(jnp.abs(hf) + jnp.abs(hb), axis=0, keepdims=True)

    @pl.when(i == 0)
    def _():
        norm_ref[...] = part

    @pl.when(i > 0)
    def _():
        norm_ref[...] += part

    rows = lax.broadcasted_iota(jnp.int32, hb.shape, 0) + i * tl
    h_ref[0] = hf.astype(h_ref.dtype)
    h_ref[1] = jnp.where(rows == 0, 0.0, hb).astype(h_ref.dtype)


def _filter_embedding(L):
    t = np.linspace(0.0, 1.0, L)[:, None]
    w_ang = 2.0 * np.pi * np.arange(L) / L
    bands = np.linspace(1e-4, FILT_BANDS - 1, FILT_BANDS)
    ang = w_ang[:, None] * bands[None, :]
    z = np.zeros((L, LANE), np.float64)
    z[:, 0:1] = t
    z[:, 1:1 + FILT_BANDS] = np.cos(ang)
    z[:, 1 + FILT_BANDS:1 + 2 * FILT_BANDS] = -np.sin(ang)
    return jnp.asarray(z, F32)


def _filters(L, w1p, b1, fr, w2, b2, w3):
    tl = min(L, 512)
    z = _filter_embedding(L)
    deltas = jnp.asarray(np.abs(np.linspace(MIN_DECAY, MAX_DECAY, HY_CH))[None, :], F32)
    return pl.pallas_call(
        _filter_kernel,
        out_shape=(jax.ShapeDtypeStruct((2, L, HY_CH), F32), jax.ShapeDtypeStruct((1, HY_CH), F32)),
        grid=(L // tl,),
        in_specs=[pl.BlockSpec((tl, LANE), lambda i: (i, 0)),
                  _const_spec(w1p.shape), _const_spec(b1.shape), _const_spec(fr.shape),
                  _const_spec(w2.shape), _const_spec(b2.shape), _const_spec(w3.shape),
                  _const_spec(deltas.shape)],
        out_specs=(pl.BlockSpec((2, tl, HY_CH), lambda i: (0, i, 0)),
                   pl.BlockSpec((1, HY_CH), lambda i: (0, 0))),
        compiler_params=_cparams(("arbitrary",)),
        name="l0_hyena_filters",
    )(z, w1p, b1, fr, w2, b2, w3, deltas)


def _dft_tables(kind, L, ti):
    ni = L // ti
    i = np.arange(ti, dtype=np.int64)[:, None]
    big = (np.arange(ni, dtype=np.int64) * ti)[:, None]
    c = np.arange(L, dtype=np.int64)[None, :]
    if kind == "hy_fwd":
        period = 4 * L
        base_idx = (2 * i + 1) * c
        r_idx = 2 * big * c
        scale = 1.0
    elif kind == "hy_inv":
        period = 4 * L
        base_idx = (2 * c + 1) * i
        r_idx = (2 * c + 1) * big
        scale = 1.0 / L
    else:
        period = L
        base_idx = i * c
        r_idx = big * c
        scale = 1.0 / math.sqrt(L * FN_GROUP_CH)
    ab = 2.0 * np.pi * (base_idx % period) / period
    ar = 2.0 * np.pi * (r_idx % period) / period
    return (jnp.asarray(np.cos(ab), F32), jnp.asarray(np.sin(ab), F32),
            jnp.asarray(scale * np.cos(ar), F32).reshape(ni, 1, L),
            jnp.asarray(scale * np.sin(ar), F32).reshape(ni, 1, L))


def _dft_kernel(mode, nb, n_x, *refs):
    bc_ref, bs_ref, rc_ref, rs_ref = refs[:4]
    x_refs = refs[4:4 + n_x]
    rest = refs[4 + n_x:]
    p_ref, q_ref = rest[-2], rest[-1]
    j = pl.program_id(2)
    nj = pl.num_programs(2)
    tj = x_refs[0].shape[1]
    if bc_ref.shape[1] == tj:
        bc, bs, rc, rs = bc_ref[...], bs_ref[...], rc_ref[...], rs_ref[...]
    else:
        off = pl.multiple_of(j * tj, tj)
        bc, bs = bc_ref[:, pl.ds(off, tj)], bs_ref[:, pl.ds(off, tj)]
        rc, rs = rc_ref[:, pl.ds(off, tj)], rs_ref[:, pl.ds(off, tj)]
    tc = (bc * rc - bs * rs).astype(BF16)
    ts = (bs * rc + bc * rs).astype(BF16)
    x1_ref = x_refs[0]
    x2_ref = x_refs[-1]

    pq = [(jnp.dot(tc, x1_ref[b].astype(BF16), preferred_element_type=F32),
           jnp.dot(ts, x2_ref[b].astype(BF16), preferred_element_type=F32)) for b in range(nb)]

    @pl.when(j == 0)
    def _():
        for b in range(nb):
            p_ref[b] = pq[b][0]
            q_ref[b] = pq[b][1]

    @pl.when(j > 0)
    def _():
        for b in range(nb):
            p_ref[b] += pq[b][0]
            q_ref[b] += pq[b][1]

    @pl.when(j == nj - 1)
    def _():
        if mode == "filt":
            nrm = rest[0][...]
            kre_ref, kim_ref = rest[1], rest[2]
            kre_ref[...] = (p_ref[0] + p_ref[1]) / nrm
            kim_ref[...] = (q_ref[1] - q_ref[0]) / nrm
        elif mode == "fwdk":
            kre, kim = rest[0][...], rest[1][...]
            yre_ref, yim_ref = rest[2], rest[3]
            for b in range(nb):
                pp, qq = p_ref[b], q_ref[b]
                yre_ref[b] = (pp * kre + qq * kim).astype(BF16)
                yim_ref[b] = (pp * kim - qq * kre).astype(BF16)
        elif mode == "inv":
            u_ref, x0_ref, skip_ref, o_ref = rest[0], rest[1], rest[2], rest[3]
            skip = skip_ref[...]
            for b in range(nb):
                o_ref[b] = ((p_ref[b] - q_ref[b] + u_ref[b] * skip) * x0_ref[b]).astype(BF16)
        else:
            o_ref = rest[0]
            for b in range(nb):
                o_ref[b] = (p_ref[b] - q_ref[b]).astype(BF16)


def _dft(mode, kind, xs, extras, nb):
    B, L, C = xs[0].shape
    ti = min(L, 256)
    tj = min(L, 512)
    bc, bs, rc, rs = _dft_tables(kind, L, ti)
    grid = (B // nb, L // ti, L // tj)
    x_spec = pl.BlockSpec((nb, tj, C), lambda g, i, j: (g, j, 0))
    row_spec = lambda c, dt=None: pl.BlockSpec((nb, ti, c), lambda g, i, j: (g, i, 0))
    in_specs = [pl.BlockSpec((ti, L), lambda g, i, j: (0, 0)),
                pl.BlockSpec((ti, L), lambda g, i, j: (0, 0)),
                pl.BlockSpec((None, 1, L), lambda g, i, j: (i, 0, 0)),
                pl.BlockSpec((None, 1, L), lambda g, i, j: (i, 0, 0))] + [x_spec] * len(xs)
    if mode == "filt":
        in_specs += [pl.BlockSpec((1, HY_CH), lambda g, i, j: (0, 0))]
        out_shape = (jax.ShapeDtypeStruct((L, HY_CH), F32),) * 2
        out_specs = (pl.BlockSpec((ti, HY_CH), lambda g, i, j: (i, 0)),) * 2
    elif mode == "fwdk":
        in_specs += [pl.BlockSpec((ti, HY_CH), lambda g, i, j: (i, 0))] * 2
        out_shape = (jax.ShapeDtypeStruct((B, L, C), BF16),) * 2
        out_specs = (row_spec(C),) * 2
    elif mode == "inv":
        in_specs += [row_spec(C)] * 2 + [pl.BlockSpec((1, C), lambda g, i, j: (0, 0))]
        out_shape = jax.ShapeDtypeStruct((B, L, C), BF16)
        out_specs = row_spec(C)
    else:
        out_shape = jax.ShapeDtypeStruct((B, L, C), BF16)
        out_specs = row_spec(C)
    return pl.pallas_call(
        functools.partial(_dft_kernel, mode, nb, len(xs)),
        out_shape=out_shape,
        grid=grid,
        in_specs=in_specs,
        out_specs=out_specs,
        scratch_shapes=[pltpu.VMEM((nb, ti, C), F32), pltpu.VMEM((nb, ti, C), F32)],
        compiler_params=_cparams(("arbitrary", "arbitrary", "arbitrary")),
        name="dft_" + mode,
    )(bc, bs, rc, rs, *xs, *extras)


FFT_R = 64
HYENA_KF = 16
FNET_KF = 16


def _pack_pairs(x):
    return pltpu.bitcast(x.astype(BF16), jnp.uint32)


def _unpack_pairs(w):
    return pltpu.bitcast(w, BF16)


def _to_blocks(w):
    return jnp.swapaxes(w.reshape(FFT_R, w.shape[-2], w.shape[-1]), 0, 1)


def _from_blocks(ws):
    kf, c = len(ws), ws[0].shape[-1]
    return jnp.swapaxes(jnp.stack(ws, axis=0), 0, 1).reshape(FFT_R // kf, kf, kf, c)


def _lead_in_kernel(g_ref, x_ref, o_ref):
    g = g_ref[...]
    xt = jnp.swapaxes(x_ref[...], 0, 1)
    for j in range(x_ref.shape[1]):
        o_ref[j] = _pack_pairs(jnp.dot(g, xt[j].astype(BF16), preferred_element_type=F32))


def _lead_in(g, x, kf, name):
    nbx, _, _, c = x.shape
    m2 = g.shape[0] // 2
    return pl.pallas_call(
        _lead_in_kernel,
        out_shape=jax.ShapeDtypeStruct((nbx, FFT_R // kf, kf, m2, c), jnp.uint32),
        grid=(nbx, FFT_R // kf),
        in_specs=[pl.BlockSpec(g.shape, lambda b, k: (0, 0)),
                  pl.BlockSpec((None, FFT_R, kf, c), lambda b, k: (b, 0, k, 0))],
        out_specs=pl.BlockSpec((None, None, kf, m2, c), lambda b, k: (b, k, 0, 0, 0)),
        compiler_params=_cparams(("arbitrary", "arbitrary")),
        name=name,
    )(g, x)


def _lead_out_kernel(n_extra, g_ref, w_ref, *rest):
    g = g_ref[...]
    o_ref = rest[-1]
    ys = [jnp.dot(g, _unpack_pairs(w_ref[j]), preferred_element_type=F32) for j in range(w_ref.shape[0])]
    y = jnp.swapaxes(jnp.stack(ys, axis=0), 0, 1)
    if n_extra:
        y = (y + rest[0][...] * rest[2][...]) * rest[1][...]
    o_ref[...] = y


def _lead_out(g, w, epilogue, name):
    nb, nk, kf, k2, c = w.shape
    blk = pl.BlockSpec((None, FFT_R, kf, c), lambda b, k: (b, 0, k, 0))
    extra_specs = [blk, blk, pl.BlockSpec((1, c), lambda b, k: (0, 0))] if epilogue else []
    return pl.pallas_call(
        functools.partial(_lead_out_kernel, len(epilogue)),
        out_shape=jax.ShapeDtypeStruct((nb, FFT_R, FFT_R, c), F32),
        grid=(nb, nk),
        in_specs=[pl.BlockSpec(g.shape, lambda b, k: (0, 0)),
                  pl.BlockSpec((None, None, kf, k2, c), lambda b, k: (b, k, 0, 0, 0))] + extra_specs,
        out_specs=blk,
        compiler_params=_cparams(("arbitrary", "arbitrary")),
        name=name,
    )(g, w, *epilogue)


def _interleave(a, b, axis):
    st = np.stack([a, b], axis=axis + 1)
    shape = list(a.shape)
    shape[axis] *= 2
    return st.reshape(shape)


def _hy2_tables():
    L = FFT_R * FFT_R
    n2 = 2 * L
    f1 = np.arange(2 * FFT_R, dtype=np.int64)
    s1 = np.arange(FFT_R, dtype=np.int64)
    th = np.pi * (((2 * f1[:, None] + 1) * s1[None, :]) % (4 * FFT_R)) / (2 * FFT_R)
    ga = _interleave(np.cos(th), -np.sin(th), 0)
    ma = _interleave(np.cos(th).T, -np.sin(th).T, 1) / L
    f2 = np.arange(FFT_R // 2, dtype=np.int64)
    s2 = np.arange(FFT_R, dtype=np.int64)
    idx = ((n2 // FFT_R) * 2 * f2[None, :, None] * s2[None, None, :]
           + (2 * f1[:, None, None] + 1) * s2[None, None, :]) % (2 * n2)
    al = np.pi * idx / n2
    c, s = np.cos(al), np.sin(al)
    nmat = np.concatenate([_interleave(c, s, 2), _interleave(-s, c, 2)], axis=1)
    ct, st = np.transpose(c, (0, 2, 1)), np.transpose(s, (0, 2, 1))
    mmat = _interleave(np.concatenate([ct, -st], axis=2), np.concatenate([st, ct], axis=2), 1)
    bf = lambda a: jnp.asarray(a, F32).astype(BF16)
    return bf(ga), bf(nmat), bf(mmat), bf(ma)


def _hy_mid_kernel(a_ref, n_ref, m_ref, k_ref, e_ref):
    half = FFT_R // 2
    kf = a_ref.shape[-2]
    a = _to_blocks(a_ref[...])
    ts = [jnp.dot(n_ref[j], _unpack_pairs(a[j]), preferred_element_type=F32)
          for j in range(kf)]
    ys = []
    for j in range(kf):
        tr, ti = ts[j][:half], ts[j][half:]
        kr, ki = k_ref[j, 0].astype(F32), k_ref[j, 1].astype(F32)
        ys.append(jnp.concatenate([tr * kr - ti * ki, tr * ki + ti * kr], axis=0).astype(BF16))
    e_ref[...] = _from_blocks([_pack_pairs(jnp.dot(m_ref[j], ys[j], preferred_element_type=F32))
                               for j in range(kf)])


def _hy_mid(a, nmat, mmat, khat):
    nb, nk, kf, _, c = a.shape
    nf1 = 2 * FFT_R
    blk = pl.BlockSpec((None, nk, kf, kf, c), lambda i, b: (b, 0, 0, i, 0))
    return pl.pallas_call(
        _hy_mid_kernel,
        out_shape=jax.ShapeDtypeStruct(a.shape, jnp.uint32),
        grid=(nf1 // kf, nb),
        in_specs=[blk,
                  pl.BlockSpec((kf, FFT_R, 2 * FFT_R), lambda i, b: (i, 0, 0)),
                  pl.BlockSpec((kf, 2 * FFT_R, FFT_R), lambda i, b: (i, 0, 0)),
                  pl.BlockSpec((kf, 2, FFT_R // 2, c), lambda i, b: (i, 0, 0, 0))],
        out_specs=blk,
        compiler_params=_cparams(("arbitrary", "arbitrary")),
        name="l0_hyena_mid",
    )(a, nmat, mmat, khat)


def _hy_kfilt_kernel(a_ref, n_ref, nrm_ref, k_ref):
    half = FFT_R // 2
    nrm = nrm_ref[...]
    af, ab = _to_blocks(a_ref[0]), _to_blocks(a_ref[1])
    for j in range(a_ref.shape[-2]):
        tf = jnp.dot(n_ref[j], _unpack_pairs(af[j]), preferred_element_type=F32)
        tb = jnp.dot(n_ref[j], _unpack_pairs(ab[j]), preferred_element_type=F32)
        k_ref[j, 0] = ((tf[:half] + tb[:half]) / nrm).astype(k_ref.dtype)
        k_ref[j, 1] = ((tf[half:] - tb[half:]) / nrm).astype(k_ref.dtype)


def _hy_kfilt(a, nmat, nrm):
    _, nk, kf, _, c = a.shape
    nf1 = 2 * FFT_R
    return pl.pallas_call(
        _hy_kfilt_kernel,
        out_shape=jax.ShapeDtypeStruct((nf1, 2, FFT_R // 2, c), BF16),
        grid=(nf1 // kf,),
        in_specs=[pl.BlockSpec((2, nk, kf, kf, c), lambda i: (0, 0, 0, i, 0)),
                  pl.BlockSpec((kf, FFT_R, 2 * FFT_R), lambda i: (i, 0, 0)),
                  _const_spec(nrm.shape)],
        out_specs=pl.BlockSpec((kf, 2, FFT_R // 2, c), lambda i: (i, 0, 0, 0)),
        compiler_params=_cparams(("arbitrary",)),
        name="l0_hyena_kfilt",
    )(a, nmat, nrm)


def _hyena_long(u, x0, skip, hfilt, nrm, nb):
    L = FFT_R * FFT_R
    c = u.shape[-1]
    v4 = lambda a, n: a.reshape(n, FFT_R, FFT_R, c)
    ga, nmat, mmat, ma = _hy2_tables()
    khat = _hy_kfilt(_lead_in(ga, v4(hfilt, 2), HYENA_KF, "l0_hyena_fwd_a"), nmat, nrm)
    ee = _hy_mid(_lead_in(ga, v4(u, nb), HYENA_KF, "l0_hyena_fwd_a"), nmat, mmat, khat)
    y = _lead_out(ma, ee, (v4(u, nb), v4(x0, nb), skip), "l0_hyena_inv_a")
    return y.reshape(nb * L, c)


def _fn2_tables():
    L = FFT_R * FFT_R
    r = np.arange(FFT_R, dtype=np.int64)
    idx = (FFT_R * r[None, :, None] * r[None, None, :] + r[None, :, None] * r[:, None, None]) % L
    gm = 2.0 * np.pi * idx / L
    c, s = np.cos(gm), np.sin(gm)
    g1 = _interleave(np.concatenate([c, -s], axis=2), np.concatenate([-s, -c], axis=2), 1)
    dl = 2.0 * np.pi * ((r[:, None] * r[None, :]) % FFT_R) / FFT_R
    g2 = _interleave(np.cos(dl), np.sin(dl), 1) / math.sqrt(L * FN_GROUP_CH)
    bf = lambda a: jnp.asarray(a, F32).astype(BF16)
    return bf(g1), bf(g2)


def _fnet_s1_kernel(x_ref, m_ref, cs_ref, g1_ref, o_ref, zc_ref, zs_ref):
    xs = jnp.swapaxes(x_ref[...], 0, 1).reshape(FNET_KF * FFT_R, D_MODEL)
    m = m_ref[...]
    h = (_ln_plain(xs) * (1.0 + m[:, D_MODEL:2 * D_MODEL]) + m[:, 0:D_MODEL]).astype(BF16)
    cs = cs_ref[...]
    for g in range(D_MODEL // FN_GROUP_CH):
        a = g * FN_GROUP_CH
        z = jnp.dot(h[:, a:a + FN_GROUP_CH], cs, preferred_element_type=F32)
        zc_ref[:, a:a + FN_GROUP_CH] = z[:, :FN_GROUP_CH].astype(BF16)
        zs_ref[:, a:a + FN_GROUP_CH] = z[:, FN_GROUP_CH:].astype(BF16)
    ws = []
    for j in range(FNET_KF):
        r0 = j * FFT_R
        s = jnp.concatenate([zc_ref[r0:r0 + FFT_R, :], zs_ref[r0:r0 + FFT_R, :]], axis=0)
        ws.append(_pack_pairs(jnp.dot(g1_ref[j], s, preferred_element_type=F32)))
    o_ref[...] = _from_blocks(ws)


def _fnet_long(x, mods, mod_base, nb):
    L = FFT_R * FFT_R
    d = D_MODEL
    g1, g2 = _fn2_tables()
    cs = _group_dft_table()
    kf = FNET_KF
    bb = pl.pallas_call(
        _fnet_s1_kernel,
        out_shape=jax.ShapeDtypeStruct((nb, FFT_R // kf, kf, FFT_R, d), jnp.uint32),
        grid=(nb, FFT_R // kf),
        in_specs=[pl.BlockSpec((None, FFT_R, kf, d), lambda b, k: (b, 0, k, 0)),
                  pl.BlockSpec((None, 1, 6 * d), lambda b, k: (mod_base + b, 0, 0)),
                  pl.BlockSpec(cs.shape, lambda b, k: (0, 0)),
                  pl.BlockSpec((kf, 2 * FFT_R, 2 * FFT_R), lambda b, k: (k, 0, 0))],
        out_specs=pl.BlockSpec((None, FFT_R // kf, kf, kf, d), lambda b, k: (b, 0, 0, k, 0)),
        scratch_shapes=[pltpu.VMEM((kf * FFT_R, d), BF16), pltpu.VMEM((kf * FFT_R, d), BF16)],
        compiler_params=_cparams(("arbitrary", "arbitrary")),
        name="l1_fnet_stage1",
    )(x.reshape(nb, FFT_R, FFT_R, d), mods, cs, g1)
    y = _lead_out(g2, bb, [], "l1_fnet_stage2")
    return y.reshape(nb * L, d)


def _group_dft_table():
    g = FN_GROUP_CH
    jk = (np.arange(g, dtype=np.int64)[:, None] * np.arange(g, dtype=np.int64)[None, :]) % g
    ang = 2.0 * np.pi * jk / g
    return jnp.asarray(np.concatenate([np.cos(ang), np.sin(ang)], axis=1), F32).astype(BF16)


def _fnet_front_kernel(x_ref, m_ref, cs_ref, zc_ref, zs_ref):
    m = m_ref[...]
    h = (_ln_plain(x_ref[...]) * (1.0 + m[:, D_MODEL:2 * D_MODEL]) + m[:, 0:D_MODEL]).astype(BF16)
    cs = cs_ref[...]
    for g in range(D_MODEL // FN_GROUP_CH):
        a = g * FN_GROUP_CH
        z = jnp.dot(h[:, a:a + FN_GROUP_CH], cs, preferred_element_type=F32)
        zc_ref[:, a:a + FN_GROUP_CH] = z[:, :FN_GROUP_CH].astype(BF16)
        zs_ref[:, a:a + FN_GROUP_CH] = z[:, FN_GROUP_CH:].astype(BF16)


def _fnet_front(x, mods, mod_base, tiles_per_mod):
    t = x.shape[0]
    tm = 2 * ROW_TILE
    tiles_per_mod = max(tiles_per_mod // 2, 1)
    cs = _group_dft_table()
    return pl.pallas_call(
        _fnet_front_kernel,
        out_shape=(jax.ShapeDtypeStruct((t, D_MODEL), BF16),) * 2,
        grid=(t // tm,),
        in_specs=[pl.BlockSpec((tm, D_MODEL), lambda i: (i, 0)),
                  _mod_spec(mod_base, tiles_per_mod),
                  _const_spec(cs.shape)],
        out_specs=(pl.BlockSpec((tm, D_MODEL), lambda i: (i, 0)),) * 2,
        compiler_params=_cparams(("arbitrary",)),
        name="l1_fnet_front",
    )(x, mods, cs)


def _post_kernel(n_a, *refs):
    x_ref, m_ref = refs[0], refs[1]
    a_refs = refs[2:2 + n_a]
    wo_refs = refs[2 + n_a:2 + 2 * n_a]
    g1_ref, b1_ref, w1_ref, w2_ref, g2_ref, b2_ref, o_ref = refs[2 + 2 * n_a:]
    m = m_ref[...]
    d = D_MODEL
    tm = x_ref.shape[0]
    halves = [(r, r + tm // POST_SPLIT) for r in range(0, tm, tm // POST_SPLIT)]
    outs = []
    for r0, r1 in halves:
        out = _bdot(a_refs[0][r0:r1, :], wo_refs[0][...])
        for a_ref, wo_ref in zip(a_refs[1:], wo_refs[1:]):
            out += _bdot(a_ref[r0:r1, :], wo_ref[...])
        outs.append(out)
    x1s, hs = [], []
    for (r0, r1), out in zip(halves, outs):
        x1 = _ln_plain(ALPHA * x_ref[r0:r1, :] + m[:, 2 * d:3 * d] * out) * g1_ref[...] + b1_ref[...]
        x1s.append(x1)
        hs.append((_ln_plain(x1) * (1.0 + m[:, 4 * d:5 * d]) + m[:, 3 * d:4 * d]).astype(BF16))
    accs = []
    n_c = D_FF // d

    def up(h, c):
        hc = jnp.maximum(jnp.dot(h, w1_ref[:, c * d:(c + 1) * d], preferred_element_type=F32), 0.0)
        return (hc * hc).astype(BF16)

    for h in hs:
        acc = None
        nxt = up(h, 0)
        for c in range(n_c):
            cur = nxt
            if c + 1 < n_c:
                nxt = up(h, c + 1)
            part = jnp.dot(cur, w2_ref[c * d:(c + 1) * d, :], preferred_element_type=F32)
            acc = part if acc is None else acc + part
        accs.append(acc)
    for (r0, r1), x1, acc in zip(halves, x1s, accs):
        o_ref[r0:r1, :] = _ln_plain(ALPHA * x1 + m[:, 5 * d:6 * d] * acc) * g2_ref[...] + b2_ref[...]


def _post(x, mods, mod_base, tiles_per_mod, a_list, wo_list, g1, b1, w1, w2, g2, b2):
    t = x.shape[0]
    tm = POST_TILE
    row = lambda c: pl.BlockSpec((tm, c), lambda i: (i, 0))
    once = lambda v: pl.BlockSpec(v.shape, lambda i: (0,) * v.ndim, pipeline_mode=pl.Buffered(1))
    in_specs = ([row(D_MODEL), _mod_spec(mod_base, tiles_per_mod * ROW_TILE // tm)]
                + [row(a.shape[1]) for a in a_list]
                + [once(w) for w in wo_list]
                + [once(v) for v in (g1, b1, w1, w2, g2, b2)])
    return pl.pallas_call(
        functools.partial(_post_kernel, len(a_list)),
        out_shape=jax.ShapeDtypeStruct((t, D_MODEL), F32),
        grid=(t // tm,),
        in_specs=in_specs,
        out_specs=row(D_MODEL),
        compiler_params=_cparams(("arbitrary",)),
        name="post_mlp",
    )(x, mods, *a_list, *wo_list, g1, b1, w1, w2, g2, b2)


def _rot_cols(w):
    parts = []
    for seg in range(2):
        o = seg * 32
        parts += [-w[:, o + 16:o + 32], w[:, o:o + 16]]
    return jnp.concatenate(parts, axis=1)


def _pad_cols(w, n):
    return jnp.pad(w, ((0, 0), (0, n - w.shape[1])))


def _block_diag2(w):
    z = jnp.zeros_like(w)
    return jnp.concatenate([jnp.concatenate([w, z], axis=1), jnp.concatenate([z, w], axis=1)], axis=0)


def _rope_tables(L):
    rows = L // GRID_W
    row = np.repeat(np.arange(rows, dtype=np.float64), GRID_W)
    col = np.tile(np.arange(GRID_W, dtype=np.float64), rows)
    half = QK_ROPE // 2
    inv = 1.0 / (ROPE_THETA ** (np.arange(0, half, 2, dtype=np.float64) / half))
    ar = row[:, None] * inv[None, :]
    ac = col[:, None] * inv[None, :]
    ang = np.concatenate([ar, ar, ac, ac], axis=1)
    cos = np.concatenate([np.cos(ang), np.ones_like(ang)], axis=1)
    sin = np.concatenate([np.sin(ang), np.zeros_like(ang)], axis=1)
    return jnp.asarray(cos, F32), jnp.asarray(sin, F32)


def kernel(x_prompt, x_sample, cache_l0_ckv, cache_l0_krope, c, c_ctx, l0_ada_w, l0_ada_b, l0_w_in, l0_conv_w, l0_conv_b, l0_hf_w1, l0_hf_b1, l0_hf_freq, l0_hf_w2, l0_hf_b2, l0_hf_w3, l0_hf_skip, l0_q_norm, l0_q_up, l0_kv_norm, l0_kv_up, l0_w_out, l0_ln1_g, l0_ln1_b, l0_mlp_w1, l0_mlp_w2, l0_ln2_g, l0_ln2_b, l1_ada_w, l1_ada_b, l1_w_out, l1_ln1_g, l1_ln1_b, l1_mlp_w1, l1_mlp_w2, l1_ln2_g, l1_ln2_b):
    nbc, lc, d = x_prompt.shape
    nbs, ls, _ = x_sample.shape
    past = cache_l0_ckv.shape[1]
    tm = ROW_TILE
    row1 = lambda v: v.reshape(1, -1)

    cond8 = jnp.concatenate([c_ctx[None, :], c, jnp.zeros((8 - 1 - nbs, d), F32)], axis=0)
    mods0 = _modulation(cond8, l0_ada_w, l0_ada_b)
    mods1 = _modulation(cond8, l1_ada_w, l1_ada_b)

    kpe_w = l0_w_in[:, 1920:1984]
    win = jnp.concatenate([l0_w_in[:, :1920], _pad_cols(kpe_w, LANE), _pad_cols(_rot_cols(kpe_w), LANE)],
                          axis=1).astype(BF16)
    dh = QK_NOPE + QK_ROPE
    q_nope = [l0_q_up[:, h * dh:h * dh + QK_NOPE] for h in range(MLA_HEADS)]
    q_pe = [l0_q_up[:, h * dh + QK_NOPE:(h + 1) * dh] for h in range(MLA_HEADS)]
    qup = jnp.concatenate(q_nope + [_pad_cols(w, LANE) for w in q_pe]
                          + [_pad_cols(_rot_cols(w), LANE) for w in q_pe], axis=1).astype(BF16)
    kvup = l0_kv_up.astype(BF16)
    front_w = (win, row1(l0_q_norm), qup, row1(l0_kv_norm), kvup, l0_conv_w, row1(l0_conv_b))
    skip = row1(l0_hf_skip)
    w1p = jnp.pad(l0_hf_w1, ((0, LANE - l0_hf_w1.shape[0]), (0, 0)))
    two = lambda v: jnp.tile(row1(v), (1, 2))
    filt_w = (_block_diag2(w1p), two(l0_hf_b1), two(l0_hf_freq), _block_diag2(l0_hf_w2), two(l0_hf_b2),
              _block_diag2(l0_hf_w3).astype(BF16))
    wo0 = l0_w_out.astype(BF16)

    xc = x_prompt.reshape(nbc * lc, d)
    xs = x_sample.reshape(nbs * ls, d)
    groups = (
        dict(x=xc, nb=nbc, L=lc, mod_base=0, tiles_per_mod=nbc * lc // tm, dft_nb=8, tq=lc, hps=MLA_HEADS),
        dict(x=xs, nb=nbs, L=ls, mod_base=1, tiles_per_mod=ls // tm, dft_nb=nbs, tq=512, hps=1),
    )
    ones_tab = (jnp.ones((FRONT_TILE, LANE), F32), jnp.zeros((FRONT_TILE, LANE), F32))

    outs = []
    ctx_ckv = ctx_krope = None
    for gi, g in enumerate(groups):
        nb, L = g["nb"], g["L"]
        latent = gi == 1
        cos, sin = _rope_tables(L) if latent else ones_tab
        two_stage = L == FFT_R * FFT_R
        u, x0, q, k, vt, kvn, kpe = _front(g["x"], mods0, g["mod_base"], g["tiles_per_mod"] * tm, front_w,
                                           cos, sin, latent, L, min(L, FRONT_TILE))
        if latent:
            extra = _cache_kv(cache_l0_ckv.reshape(nbs * past, KV_LORA),
                              _pad_cols(cache_l0_krope.reshape(nbs * past, QK_ROPE), LANE), kvup)
        else:
            extra = None
            ctx_ckv = kvn.reshape(nb, L, KV_LORA)
            ctx_krope = kpe.reshape(nb, L, QK_ROPE)
        y_mla = _attention(q, k, vt, extra, nb, L, g["tq"], g["hps"])

        hfilt, hnorm = _filters(L, *filt_w)
        if two_stage:
            y_hy = _hyena_long(u, x0, skip, hfilt, hnorm, nb)
        else:
            kre, kim = _dft("filt", "hy_fwd", [hfilt], [hnorm], 2)
            sh = (nb, L, HY_CH)
            yre, yim = _dft("fwdk", "hy_fwd", [u.reshape(sh)], [kre, kim], g["dft_nb"])
            y_hy = _dft("inv", "hy_inv", [yre, yim], [u.reshape(sh), x0.reshape(sh), skip], g["dft_nb"])
            y_hy = y_hy.reshape(nb * L, HY_CH)

        x1 = _post(g["x"], mods0, g["mod_base"], g["tiles_per_mod"], [y_hy, y_mla],
                   [wo0[:HY_CH], wo0[HY_CH:]], row1(l0_ln1_g), row1(l0_ln1_b),
                   l0_mlp_w1.astype(BF16), l0_mlp_w2.astype(BF16), row1(l0_ln2_g), row1(l0_ln2_b))

        if L == FFT_R * FFT_R:
            yf = _fnet_long(x1, mods1, g["mod_base"], nb)
        else:
            zc, zs = _fnet_front(x1, mods1, g["mod_base"], g["tiles_per_mod"])
            sh = (nb, L, d)
            yf = _dft("fnet", "fnet", [zc.reshape(sh), zs.reshape(sh)], [], g["dft_nb"] // 2)
            yf = yf.reshape(nb * L, d)
        x2 = _post(x1, mods1, g["mod_base"], g["tiles_per_mod"], [yf],
                   [l1_w_out.astype(BF16)], row1(l1_ln1_g), row1(l1_ln1_b),
                   l1_mlp_w1.astype(BF16), l1_mlp_w2.astype(BF16), row1(l1_ln2_g), row1(l1_ln2_b))
        outs.append(x2.reshape(nb, L, d))

    return (outs[0], outs[1], ctx_ckv, ctx_krope)
```

```python
import functools
import math

import numpy as np
import jax
import jax.numpy as jnp
from jax import lax
from jax.experimental import pallas as pl
from jax.experimental.pallas import tpu as pltpu

F32 = jnp.float32
BF16 = jnp.bfloat16
HI = lax.Precision.HIGHEST

D_MODEL = 1024
DEPTH = 2
GRID_W = 64
HY_CH = 512
FILT_BANDS = 16
FILT_ORDER = 64
FAST_DECAY_PCT = 0.3
SLOW_DECAY_PCT = 1.5
DECAY_TARGET = 1e-2
MAX_DECAY = math.log(DECAY_TARGET) / FAST_DECAY_PCT
MIN_DECAY = math.log(DECAY_TARGET) / SLOW_DECAY_PCT
MLA_HEADS = 4
QK_NOPE = 128
QK_ROPE = 64
V_HEAD = 128
Q_LORA = 256
KV_LORA = 128
ROPE_THETA = 10000.0
FN_GROUP_CH = 128
D_FF = 4096
ALPHA = (2 * DEPTH) ** 0.25
LN_EPS = 1e-5
RMS_EPS = 1e-6

LANE = 128
ROW_TILE = 256
FRONT_TILE = 512
POST_TILE = 512
POST_SPLIT = 2
QK_PAD = 256
VT_ROWS = V_HEAD + 16
LOG2E = 1.4426950408889634
VMEM_LIMIT = 56 * 1024 * 1024


def _cparams(sem):
    return pltpu.CompilerParams(dimension_semantics=sem, vmem_limit_bytes=VMEM_LIMIT)


def _ln_plain(x):
    mu = jnp.mean(x, axis=-1, keepdims=True)
    xc = x - mu
    var = jnp.mean(xc * xc, axis=-1, keepdims=True)
    return xc * lax.rsqrt(var + LN_EPS)


def _rms(x, g):
    return x * lax.rsqrt(jnp.mean(x * x, axis=-1, keepdims=True) + RMS_EPS) * g


def _bdot(a, b):
    return jnp.dot(a.astype(BF16), b, preferred_element_type=F32)


def _vt_rows(v):
    ones = jnp.ones((VT_ROWS - V_HEAD, v.shape[0]), BF16)
    return jnp.concatenate([jnp.transpose(v).astype(BF16), ones], axis=0)


def _mod_kernel(c_ref, w_ref, b_ref, o_ref):
    c = c_ref[...]
    s = c / (1.0 + jnp.exp(-c))
    s_hi = s.astype(BF16)
    s_lo = (s - s_hi.astype(F32)).astype(BF16)
    s2 = jnp.concatenate([s_hi, s_lo], axis=0)
    nr = s.shape[0]
    w = w_ref[...]
    w_hi = w.astype(BF16)
    w_lo = (w - w_hi.astype(F32)).astype(BF16)
    r1 = jnp.dot(s2, w_hi, preferred_element_type=F32)
    r2 = jnp.dot(s_hi, w_lo, preferred_element_type=F32)
    o_ref[...] = r1[:nr] + r1[nr:] + r2 + b_ref[...]


def _modulation(cond8, w, b):
    n = w.shape[1]
    tn = 1536
    out = pl.pallas_call(
        _mod_kernel,
        out_shape=jax.ShapeDtypeStruct((8, n), F32),
        grid=(n // tn,),
        in_specs=[pl.BlockSpec((8, D_MODEL), lambda j: (0, 0)),
                  pl.BlockSpec((D_MODEL, tn), lambda j: (0, j)),
                  pl.BlockSpec((1, tn), lambda j: (0, j))],
        out_specs=pl.BlockSpec((8, tn), lambda j: (0, j)),
        compiler_params=_cparams(("arbitrary",)),
        name="modulation",
    )(cond8, w, b.reshape(1, n))
    return out.reshape(8, 1, n)


def _mod_spec(mod_base, tiles_per_mod):
    return pl.BlockSpec((None, 1, 6 * D_MODEL), lambda i: (mod_base + i // tiles_per_mod, 0, 0))


def _const_spec(shape):
    nd = len(shape)
    return pl.BlockSpec(shape, lambda i: (0,) * nd)


HALO = 8


def _front_kernel(tiles_per_seq, x_ref, xp_ref, xn_ref, m_ref, win_ref, qn_ref, qup_ref, kvn_ref, kvup_ref,
                  cos_ref, sin_ref, cw_ref, cb_ref,
                  u_ref, x0_ref, q_ref, k_ref, v_ref, kvn_out_ref, kpe_ref):
    i = pl.program_id(0)
    m = m_ref[...]
    tm = x_ref.shape[0]
    nh = 3 * HY_CH
    xe = jnp.concatenate([xp_ref[...], x_ref[...], xn_ref[...]], axis=0)
    he = _ln_plain(xe) * (1.0 + m[:, D_MODEL:2 * D_MODEL]) + m[:, 0:D_MODEL]

    z = _bdot(he[HALO:HALO + tm], win_ref[:, nh:])
    zh = _bdot(he, win_ref[:, :nh])
    q_c = z[:, 0:256]
    kv_c = z[:, 256:384]
    cos = cos_ref[...]
    sin = sin_ref[...]
    kpe = z[:, 384:512] * cos + z[:, 512:640] * sin
    kpe_ref[...] = kpe[:, :QK_ROPE]
    kpe_b = kpe.astype(BF16)
    q = _bdot(_rms(q_c, qn_ref[...]), qup_ref[...]) * (LOG2E / math.sqrt(QK_NOPE + QK_ROPE))
    kvn = _rms(kv_c, kvn_ref[...])
    kvn_out_ref[...] = kvn
    kv = _bdot(kvn, kvup_ref[...])

    pos = i % tiles_per_seq
    rows = lax.broadcasted_iota(jnp.int32, (tm + 2 * HALO, 1), 0)
    inside = jnp.logical_and(jnp.logical_or(rows >= HALO, pos != 0),
                             jnp.logical_or(rows < tm + HALO, pos != tiles_per_seq - 1))
    zh = jnp.where(inside, zh, 0.0)
    cw = cw_ref[...]
    pz = (pltpu.roll(zh, 1, 0) * cw[0:1, :] + zh * cw[1:2, :]
          + pltpu.roll(zh, tm + 2 * HALO - 1, 0) * cw[2:3, :])[HALO:HALO + tm] + cb_ref[...]
    u_ref[...] = pz[:, 2 * HY_CH:] * pz[:, HY_CH:2 * HY_CH]
    x0_ref[...] = pz[:, :HY_CH].astype(x0_ref.dtype)

    for hd in range(MLA_HEADS):
        a = hd * LANE
        q_pe = (q[:, 512 + a:512 + a + LANE] * cos + q[:, 1024 + a:1024 + a + LANE] * sin).astype(BF16)
        q_ref[hd] = jnp.concatenate([q[:, a:a + LANE].astype(BF16), q_pe], axis=-1)
        k_ref[hd] = jnp.concatenate([kv[:, 2 * a:2 * a + LANE].astype(BF16), kpe_b], axis=-1)
        v_ref[hd] = _vt_rows(kv[:, 2 * a + LANE:2 * a + 2 * LANE])


def _front(x, mods, mod_base, rows_per_mod, w, cos, sin, rope, seq_len, tm):
    t = x.shape[0]
    tiles_per_seq = seq_len // tm
    tiles_per_mod = rows_per_mod // tm
    win, qn, qup, kvn, kvup, conv_w, conv_b = w
    if rope:
        tab_spec = pl.BlockSpec((tm, LANE), lambda i: (i % tiles_per_seq, 0))
    else:
        tab_spec = pl.BlockSpec((tm, LANE), lambda i: (0, 0))
    r8 = tm // HALO
    n8 = t // HALO
    hy_out = lambda dt: jax.ShapeDtypeStruct((t, HY_CH), dt)
    hy_spec = pl.BlockSpec((tm, HY_CH), lambda i: (i, 0))
    return pl.pallas_call(
        functools.partial(_front_kernel, tiles_per_seq),
        out_shape=(hy_out(F32), hy_out(BF16),
                   jax.ShapeDtypeStruct((MLA_HEADS, t, QK_PAD), BF16),
                   jax.ShapeDtypeStruct((MLA_HEADS, t, QK_PAD), BF16),
                   jax.ShapeDtypeStruct((MLA_HEADS, VT_ROWS, t), BF16),
                   jax.ShapeDtypeStruct((t, KV_LORA), F32),
                   jax.ShapeDtypeStruct((t, QK_ROPE), F32)),
        grid=(t // tm,),
        in_specs=[pl.BlockSpec((tm, D_MODEL), lambda i: (i, 0)),
                  pl.BlockSpec((HALO, D_MODEL), lambda i: (jnp.maximum(i * r8 - 1, 0), 0)),
                  pl.BlockSpec((HALO, D_MODEL), lambda i: (jnp.minimum((i + 1) * r8, n8 - 1), 0)),
                  _mod_spec(mod_base, tiles_per_mod),
                  _const_spec(win.shape), _const_spec(qn.shape), _const_spec(qup.shape),
                  _const_spec(kvn.shape), _const_spec(kvup.shape),
                  tab_spec, tab_spec,
                  _const_spec(conv_w.shape), _const_spec(conv_b.shape)],
        out_specs=(hy_spec, hy_spec,
                   pl.BlockSpec((MLA_HEADS, tm, QK_PAD), lambda i: (0, i, 0)),
                   pl.BlockSpec((MLA_HEADS, tm, QK_PAD), lambda i: (0, i, 0)),
                   pl.BlockSpec((MLA_HEADS, VT_ROWS, tm), lambda i: (0, 0, i)),
                   pl.BlockSpec((tm, KV_LORA), lambda i: (i, 0)),
                   pl.BlockSpec((tm, QK_ROPE), lambda i: (i, 0))),
        compiler_params=_cparams(("arbitrary",)),
        name="l0_front",
    )(x, x, x, mods, win, qn, qup, kvn, kvup, cos, sin, conv_w, conv_b)


def _cache_kv_kernel(ckv_ref, kr_ref, kvup_ref, k_ref, v_ref):
    kv = _bdot(ckv_ref[...], kvup_ref[...])
    kr = kr_ref[...].astype(BF16)
    for hd in range(MLA_HEADS):
        a = 2 * hd * LANE
        k_ref[hd] = jnp.concatenate([kv[:, a:a + LANE].astype(BF16), kr], axis=-1)
        v_ref[hd] = _vt_rows(kv[:, a + LANE:a + 2 * LANE])


def _cache_kv(ckv, krope_pad, kvup):
    t = ckv.shape[0]
    return pl.pallas_call(
        _cache_kv_kernel,
        out_shape=(jax.ShapeDtypeStruct((MLA_HEADS, t, QK_PAD), BF16),
                   jax.ShapeDtypeStruct((MLA_HEADS, VT_ROWS, t), BF16)),
        name="l0_cache_kv",
    )(ckv, krope_pad, kvup)


def _col_reduce(x, op):
    rows, n = x.shape
    for g in (32, 8):
        if rows % (8 * g) == 0 and rows > 8 * g:
            x = op(x.reshape(rows // (8 * g), 8 * g, n), axis=0)
            rows = 8 * g
    return op(x, axis=0, keepdims=True)


def _attn_kernel(n_kv, hps, q_ref, *refs):
    k_refs, vt_refs = refs[:n_kv], refs[n_kv:2 * n_kv]
    o_ref, s_even, s_odd, m_even, m_odd = refs[2 * n_kv:]
    i = pl.program_id(0)

    @pl.when(i == 0)
    def _():
        s_odd[...] = jnp.zeros_like(s_odd)
        m_odd[...] = jnp.zeros_like(m_odd)

    def step(s_write, m_write, s_read, m_read):
        nt = (((1,), (1,)), ((), ()))
        for h in range(hps):
            q = q_ref[h]
            r0 = 0
            m = None
            for k_ref in k_refs:
                lk = k_ref.shape[1]
                sblk = lax.dot_general(k_ref[h], q, nt, preferred_element_type=F32)
                s_write[h, r0:r0 + lk, :] = sblk
                mc = _col_reduce(sblk, jnp.max)
                m = mc if m is None else jnp.maximum(m, mc)
                r0 += lk
            m_write[h] = m
        for h in range(hps):
            pb = jnp.exp2(s_read[h] - m_read[h]).astype(BF16)
            acc = None
            r0 = 0
            for vt_ref in vt_refs:
                lk = vt_ref.shape[2]
                pv = jnp.dot(vt_ref[h], pb[r0:r0 + lk, :], preferred_element_type=F32)
                acc = pv if acc is None else acc + pv
                r0 += lk
            o_ref[:, h * V_HEAD:(h + 1) * V_HEAD] = jnp.transpose(
                acc[:V_HEAD] / acc[V_HEAD:V_HEAD + 1]).astype(o_ref.dtype)

    pl.when(i % 2 == 0)(lambda: step(s_even, m_even, s_odd, m_odd))
    pl.when(i % 2 == 1)(lambda: step(s_odd, m_odd, s_even, m_even))


def _attention(q, k, vt, extra, nb, lq, tq, hps):
    nq = lq // tq
    ng = MLA_HEADS // hps
    n_tiles = nb * ng * nq

    def where(t):
        bh = t // nq
        return bh // ng, bh % ng, t % nq

    def score_side(fn):
        return lambda i: fn(*where(jnp.minimum(i, n_tiles - 1)))

    def value_side(fn):
        return lambda i: fn(*where(jnp.maximum(i - 1, 0)))

    ks, vts = [k], [vt]
    if extra is not None:
        ks.append(extra[0])
        vts.append(extra[1])
    in_specs = [pl.BlockSpec((hps, tq, QK_PAD), score_side(lambda b, h, j: (h, b * nq + j, 0)))]
    in_specs += [pl.BlockSpec((hps, a.shape[1] // nb, QK_PAD), score_side(lambda b, h, j: (h, b, 0))) for a in ks]
    in_specs += [pl.BlockSpec((hps, VT_ROWS, a.shape[2] // nb), value_side(lambda b, h, j: (h, 0, b))) for a in vts]
    lk_total = sum(a.shape[1] // nb for a in ks)
    return pl.pallas_call(
        functools.partial(_attn_kernel, len(ks), hps),
        out_shape=jax.ShapeDtypeStruct((nb * lq, MLA_HEADS * V_HEAD), BF16),
        grid=(n_tiles + 1,),
        in_specs=in_specs,
        out_specs=pl.BlockSpec((tq, hps * V_HEAD), value_side(lambda b, h, j: (b * nq + j, h))),
        scratch_shapes=[pltpu.VMEM((hps, lk_total, tq), F32), pltpu.VMEM((hps, lk_total, tq), F32),
                        pltpu.VMEM((hps, 1, tq), F32), pltpu.VMEM((hps, 1, tq), F32)],
        compiler_params=_cparams(("arbitrary",)),
        name="l0_attention",
    )(q, *ks, *vts)


def _filter_kernel(z_ref, w1_ref, b1_ref, fr_ref, w2_ref, b2_ref, w3_ref, dl_ref, h_ref, norm_ref):
    i = pl.program_id(0)
    z = z_ref[...]
    tl = z.shape[0]
    fr = fr_ref[...]
    z2 = jnp.concatenate([z[:tl // 2], z[tl // 2:]], axis=1)
    h = jnp.sin(fr * (jnp.dot(z2, w1_ref[...], precision=HI, preferred_element_type=F32) + b1_ref[...]))
    h = jnp.sin(fr * (jnp.dot(h, w2_ref[...], precision=HI, preferred_element_type=F32) + b2_ref[...]))
    h = _bdot(h, w3_ref[...])
    h = jnp.concatenate([h[:, :2 * HY_CH], h[:, 2 * HY_CH:]], axis=0)
    decay = jnp.exp(-(z[:, 0:1] * dl_ref[...]))
    hf = h[:, :HY_CH] * decay
    hb = h[:, HY_CH:] * decay
    part = jnp.sum(jnp.abs(hf) + jnp.abs(hb), axis=0, keepdims=True)

    @pl.when(i == 0)
    def _():
        norm_ref[...] = part

    @pl.when(i > 0)
    def _():
        norm_ref[...] += part

    rows = lax.broadcasted_iota(jnp.int32, hb.shape, 0) + i * tl
    h_ref[0] = hf.astype(h_ref.dtype)
    h_ref[1] = jnp.where(rows == 0, 0.0, hb).astype(h_ref.dtype)


def _filter_embedding(L):
    t = np.linspace(0.0, 1.0, L)[:, None]
    w_ang = 2.0 * np.pi * np.arange(L) / L
    bands = np.linspace(1e-4, FILT_BANDS - 1, FILT_BANDS)
    ang = w_ang[:, None] * bands[None, :]
    z = np.zeros((L, LANE), np.float64)
    z[:, 0:1] = t
    z[:, 1:1 + FILT_BANDS] = np.cos(ang)
    z[:, 1 + FILT_BANDS:1 + 2 * FILT_BANDS] = -np.sin(ang)
    return jnp.asarray(z, F32)


def _filters(L, w1p, b1, fr, w2, b2, w3):
    tl = min(L, 512)
    z = _filter_embedding(L)
    deltas = jnp.asarray(np.abs(np.linspace(MIN_DECAY, MAX_DECAY, HY_CH))[None, :], F32)
    return pl.pallas_call(
        _filter_kernel,
        out_shape=(jax.ShapeDtypeStruct((2, L, HY_CH), F32), jax.ShapeDtypeStruct((1, HY_CH), F32)),
        grid=(L // tl,),
        in_specs=[pl.BlockSpec((tl, LANE), lambda i: (i, 0)),
                  _const_spec(w1p.shape), _const_spec(b1.shape), _const_spec(fr.shape),
                  _const_spec(w2.shape), _const_spec(b2.shape), _const_spec(w3.shape),
                  _const_spec(deltas.shape)],
        out_specs=(pl.BlockSpec((2, tl, HY_CH), lambda i: (0, i, 0)),
                   pl.BlockSpec((1, HY_CH), lambda i: (0, 0))),
        compiler_params=_cparams(("arbitrary",)),
        name="l0_hyena_filters",
    )(z, w1p, b1, fr, w2, b2, w3, deltas)


def _dft_tables(kind, L, ti):
    ni = L // ti
    i = np.arange(ti, dtype=np.int64)[:, None]
    big = (np.arange(ni, dtype=np.int64) * ti)[:, None]
    c = np.arange(L, dtype=np.int64)[None, :]
    if kind == "hy_fwd":
        period = 4 * L
        base_idx = (2 * i + 1) * c
        r_idx = 2 * big * c
        scale = 1.0
    elif kind == "hy_inv":
        period = 4 * L
        base_idx = (2 * c + 1) * i
        r_idx = (2 * c + 1) * big
        scale = 1.0 / L
    else:
        period = L
        base_idx = i * c
        r_idx = big * c
        scale = 1.0 / math.sqrt(L * FN_GROUP_CH)
    ab = 2.0 * np.pi * (base_idx % period) / period
    ar = 2.0 * np.pi * (r_idx % period) / period
    return (jnp.asarray(np.cos(ab), F32), jnp.asarray(np.sin(ab), F32),
            jnp.asarray(scale * np.cos(ar), F32).reshape(ni, 1, L),
            jnp.asarray(scale * np.sin(ar), F32).reshape(ni, 1, L))


def _dft_kernel(mode, nb, n_x, *refs):
    bc_ref, bs_ref, rc_ref, rs_ref = refs[:4]
    x_refs = refs[4:4 + n_x]
    rest = refs[4 + n_x:]
    p_ref, q_ref = rest[-2], rest[-1]
    j = pl.program_id(2)
    nj = pl.num_programs(2)
    tj = x_refs[0].shape[1]
    if bc_ref.shape[1] == tj:
        bc, bs, rc, rs = bc_ref[...], bs_ref[...], rc_ref[...], rs_ref[...]
    else:
        off = pl.multiple_of(j * tj, tj)
        bc, bs = bc_ref[:, pl.ds(off, tj)], bs_ref[:, pl.ds(off, tj)]
        rc, rs = rc_ref[:, pl.ds(off, tj)], rs_ref[:, pl.ds(off, tj)]
    tc = (bc * rc - bs * rs).astype(BF16)
    ts = (bs * rc + bc * rs).astype(BF16)
    x1_ref = x_refs[0]
    x2_ref = x_refs[-1]

    pq = [(jnp.dot(tc, x1_ref[b].astype(BF16), preferred_element_type=F32),
           jnp.dot(ts, x2_ref[b].astype(BF16), preferred_element_type=F32)) for b in range(nb)]

    @pl.when(j == 0)
    def _():
        for b in range(nb):
            p_ref[b] = pq[b][0]
            q_ref[b] = pq[b][1]

    @pl.when(j > 0)
    def _():
        for b in range(nb):
            p_ref[b] += pq[b][0]
            q_ref[b] += pq[b][1]

    @pl.when(j == nj - 1)
    def _():
        if mode == "filt":
            nrm = rest[0][...]
            kre_ref, kim_ref = rest[1], rest[2]
            kre_ref[...] = (p_ref[0] + p_ref[1]) / nrm
            kim_ref[...] = (q_ref[1] - q_ref[0]) / nrm
        elif mode == "fwdk":
            kre, kim = rest[0][...], rest[1][...]
            yre_ref, yim_ref = rest[2], rest[3]
            for b in range(nb):
                pp, qq = p_ref[b], q_ref[b]
                yre_ref[b] = (pp * kre + qq * kim).astype(BF16)
                yim_ref[b] = (pp * kim - qq * kre).astype(BF16)
        elif mode == "inv":
            u_ref, x0_ref, skip_ref, o_ref = rest[0], rest[1], rest[2], rest[3]
            skip = skip_ref[...]
            for b in range(nb):
                o_ref[b] = ((p_ref[b] - q_ref[b] + u_ref[b] * skip) * x0_ref[b].astype(F32)).astype(BF16)
        else:
            o_ref = rest[0]
            for b in range(nb):
                o_ref[b] = (p_ref[b] - q_ref[b]).astype(BF16)


def _dft(mode, kind, xs, extras, nb):
    B, L, C = xs[0].shape
    ti = min(L, 256)
    tj = min(L, 512)
    bc, bs, rc, rs = _dft_tables(kind, L, ti)
    grid = (B // nb, L // ti, L // tj)
    x_spec = pl.BlockSpec((nb, tj, C), lambda g, i, j: (g, j, 0))
    row_spec = lambda c, dt=None: pl.BlockSpec((nb, ti, c), lambda g, i, j: (g, i, 0))
    in_specs = [pl.BlockSpec((ti, L), lambda g, i, j: (0, 0)),
                pl.BlockSpec((ti, L), lambda g, i, j: (0, 0)),
                pl.BlockSpec((None, 1, L), lambda g, i, j: (i, 0, 0)),
                pl.BlockSpec((None, 1, L), lambda g, i, j: (i, 0, 0))] + [x_spec] * len(xs)
    if mode == "filt":
        in_specs += [pl.BlockSpec((1, HY_CH), lambda g, i, j: (0, 0))]
        out_shape = (jax.ShapeDtypeStruct((L, HY_CH), F32),) * 2
        out_specs = (pl.BlockSpec((ti, HY_CH), lambda g, i, j: (i, 0)),) * 2
    elif mode == "fwdk":
        in_specs += [pl.BlockSpec((ti, HY_CH), lambda g, i, j: (i, 0))] * 2
        out_shape = (jax.ShapeDtypeStruct((B, L, C), BF16),) * 2
        out_specs = (row_spec(C),) * 2
    elif mode == "inv":
        in_specs += [row_spec(C)] * 2 + [pl.BlockSpec((1, C), lambda g, i, j: (0, 0))]
        out_shape = jax.ShapeDtypeStruct((B, L, C), BF16)
        out_specs = row_spec(C)
    else:
        out_shape = jax.ShapeDtypeStruct((B, L, C), BF16)
        out_specs = row_spec(C)
    return pl.pallas_call(
        functools.partial(_dft_kernel, mode, nb, len(xs)),
        out_shape=out_shape,
        grid=grid,
        in_specs=in_specs,
        out_specs=out_specs,
        scratch_shapes=[pltpu.VMEM((nb, ti, C), F32), pltpu.VMEM((nb, ti, C), F32)],
        compiler_params=_cparams(("arbitrary", "arbitrary", "arbitrary")),
        name="dft_" + mode,
    )(bc, bs, rc, rs, *xs, *extras)


FFT_R = 64
HYENA_KF = 16
FNET_KF = 16


def _pack_pairs(x):
    return pltpu.bitcast(x.astype(BF16), jnp.uint32)


def _unpack_pairs(w):
    return pltpu.bitcast(w, BF16)


def _to_blocks(w):
    return jnp.swapaxes(w.reshape(FFT_R, w.shape[-2], w.shape[-1]), 0, 1)


def _from_blocks(ws):
    kf, c = len(ws), ws[0].shape[-1]
    return jnp.swapaxes(jnp.stack(ws, axis=0), 0, 1).reshape(FFT_R // kf, kf, kf, c)


def _lead_in_kernel(g_ref, x_ref, o_ref):
    g = g_ref[...]
    xt = jnp.swapaxes(x_ref[...], 0, 1)
    for j in range(x_ref.shape[1]):
        o_ref[j] = _pack_pairs(jnp.dot(g, xt[j].astype(BF16), preferred_element_type=F32))


def _lead_in(g, x, kf, name):
    nbx, _, _, c = x.shape
    m2 = g.shape[0] // 2
    return pl.pallas_call(
        _lead_in_kernel,
        out_shape=jax.ShapeDtypeStruct((nbx, FFT_R // kf, kf, m2, c), jnp.uint32),
        grid=(nbx, FFT_R // kf),
        in_specs=[pl.BlockSpec(g.shape, lambda b, k: (0, 0)),
                  pl.BlockSpec((None, FFT_R, kf, c), lambda b, k: (b, 0, k, 0))],
        out_specs=pl.BlockSpec((None, None, kf, m2, c), lambda b, k: (b, k, 0, 0, 0)),
        compiler_params=_cparams(("arbitrary", "arbitrary")),
        name=name,
    )(g, x)


def _lead_out_kernel(n_extra, g_ref, w_ref, *rest):
    g = g_ref[...]
    o_ref = rest[-1]
    ys = [jnp.dot(g, _unpack_pairs(w_ref[j]), preferred_element_type=F32) for j in range(w_ref.shape[0])]
    y = jnp.swapaxes(jnp.stack(ys, axis=0), 0, 1)
    if n_extra:
        y = (y + rest[0][...] * rest[2][...]) * rest[1][...].astype(F32)
    o_ref[...] = y.astype(o_ref.dtype)


def _lead_out(g, w, epilogue, name):
    nb, nk, kf, k2, c = w.shape
    blk = pl.BlockSpec((None, FFT_R, kf, c), lambda b, k: (b, 0, k, 0))
    extra_specs = [blk, blk, pl.BlockSpec((1, c), lambda b, k: (0, 0))] if epilogue else []
    return pl.pallas_call(
        functools.partial(_lead_out_kernel, len(epilogue)),
        out_shape=jax.ShapeDtypeStruct((nb, FFT_R, FFT_R, c), BF16),
        grid=(nb, nk),
        in_specs=[pl.BlockSpec(g.shape, lambda b, k: (0, 0)),
                  pl.BlockSpec((None, None, kf, k2, c), lambda b, k: (b, k, 0, 0, 0))] + extra_specs,
        out_specs=blk,
        compiler_params=_cparams(("arbitrary", "arbitrary")),
        name=name,
    )(g, w, *epilogue)


def _interleave(a, b, axis):
    st = np.stack([a, b], axis=axis + 1)
    shape = list(a.shape)
    shape[axis] *= 2
    return st.reshape(shape)


def _hy2_tables():
    L = FFT_R * FFT_R
    n2 = 2 * L
    f1 = np.arange(2 * FFT_R, dtype=np.int64)
    s1 = np.arange(FFT_R, dtype=np.int64)
    th = np.pi * (((2 * f1[:, None] + 1) * s1[None, :]) % (4 * FFT_R)) / (2 * FFT_R)
    ga = _interleave(np.cos(th), -np.sin(th), 0)
    ma = _interleave(np.cos(th).T, -np.sin(th).T, 1) / L
    f2 = np.arange(FFT_R // 2, dtype=np.int64)
    s2 = np.arange(FFT_R, dtype=np.int64)
    idx = ((n2 // FFT_R) * 2 * f2[None, :, None] * s2[None, None, :]
           + (2 * f1[:, None, None] + 1) * s2[None, None, :]) % (2 * n2)
    al = np.pi * idx / n2
    c, s = np.cos(al), np.sin(al)
    nmat = np.concatenate([_interleave(c, s, 2), _interleave(-s, c, 2)], axis=1)
    ct, st = np.transpose(c, (0, 2, 1)), np.transpose(s, (0, 2, 1))
    mmat = _interleave(np.concatenate([ct, -st], axis=2), np.concatenate([st, ct], axis=2), 1)
    bf = lambda a: jnp.asarray(a, F32).astype(BF16)
    return bf(ga), bf(nmat), bf(mmat), bf(ma)


def _hy_mid_kernel(a_ref, n_ref, m_ref, k_ref, e_ref):
    half = FFT_R // 2
    kf = a_ref.shape[-2]
    a = _to_blocks(a_ref[...])
    ts = [jnp.dot(n_ref[j], _unpack_pairs(a[j]), preferred_element_type=F32)
          for j in range(kf)]
    ys = []
    for j in range(kf):
        tr, ti = ts[j][:half], ts[j][half:]
        kr, ki = k_ref[j, 0].astype(F32), k_ref[j, 1].astype(F32)
        ys.append(jnp.concatenate([tr * kr - ti * ki, tr * ki + ti * kr], axis=0).astype(BF16))
    e_ref[...] = _from_blocks([_pack_pairs(jnp.dot(m_ref[j], ys[j], preferred_element_type=F32))
                               for j in range(kf)])


def _hy_mid(a, nmat, mmat, khat):
    nb, nk, kf, _, c = a.shape
    nf1 = 2 * FFT_R
    blk = pl.BlockSpec((None, nk, kf, kf, c), lambda i, b: (b, 0, 0, i, 0))
    return pl.pallas_call(
        _hy_mid_kernel,
        out_shape=jax.ShapeDtypeStruct(a.shape, jnp.uint32),
        grid=(nf1 // kf, nb),
        in_specs=[blk,
                  pl.BlockSpec((kf, FFT_R, 2 * FFT_R), lambda i, b: (i, 0, 0)),
                  pl.BlockSpec((kf, 2 * FFT_R, FFT_R), lambda i, b: (i, 0, 0)),
                  pl.BlockSpec((kf, 2, FFT_R // 2, c), lambda i, b: (i, 0, 0, 0))],
        out_specs=blk,
        compiler_params=_cparams(("arbitrary", "arbitrary")),
        name="l0_hyena_mid",
    )(a, nmat, mmat, khat)


def _hy_kfilt_kernel(a_ref, n_ref, nrm_ref, k_ref):
    half = FFT_R // 2
    nrm = nrm_ref[...]
    af, ab = _to_blocks(a_ref[0]), _to_blocks(a_ref[1])
    for j in range(a_ref.shape[-2]):
        tf = jnp.dot(n_ref[j], _unpack_pairs(af[j]), preferred_element_type=F32)
        tb = jnp.dot(n_ref[j], _unpack_pairs(ab[j]), preferred_element_type=F32)
        k_ref[j, 0] = ((tf[:half] + tb[:half]) / nrm).astype(k_ref.dtype)
        k_ref[j, 1] = ((tf[half:] - tb[half:]) / nrm).astype(k_ref.dtype)


def _hy_kfilt(a, nmat, nrm):
    _, nk, kf, _, c = a.shape
    nf1 = 2 * FFT_R
    return pl.pallas_call(
        _hy_kfilt_kernel,
        out_shape=jax.ShapeDtypeStruct((nf1, 2, FFT_R // 2, c), BF16),
        grid=(nf1 // kf,),
        in_specs=[pl.BlockSpec((2, nk, kf, kf, c), lambda i: (0, 0, 0, i, 0)),
                  pl.BlockSpec((kf, FFT_R, 2 * FFT_R), lambda i: (i, 0, 0)),
                  _const_spec(nrm.shape)],
        out_specs=pl.BlockSpec((kf, 2, FFT_R // 2, c), lambda i: (i, 0, 0, 0)),
        compiler_params=_cparams(("arbitrary",)),
        name="l0_hyena_kfilt",
    )(a, nmat, nrm)


def _hyena_long(u, x0, skip, hfilt, nrm, nb):
    L = FFT_R * FFT_R
    c = u.shape[-1]
    v4 = lambda a, n: a.reshape(n, FFT_R, FFT_R, c)
    ga, nmat, mmat, ma = _hy2_tables()
    khat = _hy_kfilt(_lead_in(ga, v4(hfilt, 2), HYENA_KF, "l0_hyena_fwd_a"), nmat, nrm)
    ee = _hy_mid(_lead_in(ga, v4(u, nb), HYENA_KF, "l0_hyena_fwd_a"), nmat, mmat, khat)
    y = _lead_out(ma, ee, (v4(u, nb), v4(x0, nb), skip), "l0_hyena_inv_a")
    return y.reshape(nb * L, c)


def _fn2_tables():
    L = FFT_R * FFT_R
    r = np.arange(FFT_R, dtype=np.int64)
    idx = (FFT_R * r[None, :, None] * r[None, None, :] + r[None, :, None] * r[:, None, None]) % L
    gm = 2.0 * np.pi * idx / L
    c, s = np.cos(gm), np.sin(gm)
    g1 = _interleave(np.concatenate([c, -s], axis=2), np.concatenate([-s, -c], axis=2), 1)
    dl = 2.0 * np.pi * ((r[:, None] * r[None, :]) % FFT_R) / FFT_R
    g2 = _interleave(np.cos(dl), np.sin(dl), 1) / math.sqrt(L * FN_GROUP_CH)
    bf = lambda a: jnp.asarray(a, F32).astype(BF16)
    return bf(g1), bf(g2)


def _fnet_s1_kernel(x_ref, m_ref, cs_ref, g1_ref, o_ref, zc_ref, zs_ref):
    xs = jnp.swapaxes(x_ref[...], 0, 1).reshape(FNET_KF * FFT_R, D_MODEL)
    m = m_ref[...]
    h = (_ln_plain(xs) * (1.0 + m[:, D_MODEL:2 * D_MODEL]) + m[:, 0:D_MODEL]).astype(BF16)
    cs = cs_ref[...]
    for g in range(D_MODEL // FN_GROUP_CH):
        a = g * FN_GROUP_CH
        z = jnp.dot(h[:, a:a + FN_GROUP_CH], cs, preferred_element_type=F32)
        zc_ref[:, a:a + FN_GROUP_CH] = z[:, :FN_GROUP_CH].astype(BF16)
        zs_ref[:, a:a + FN_GROUP_CH] = z[:, FN_GROUP_CH:].astype(BF16)
    ws = []
    for j in range(FNET_KF):
        r0 = j * FFT_R
        s = jnp.concatenate([zc_ref[r0:r0 + FFT_R, :], zs_ref[r0:r0 + FFT_R, :]], axis=0)
        ws.append(_pack_pairs(jnp.dot(g1_ref[j], s, preferred_element_type=F32)))
    o_ref[...] = _from_blocks(ws)


def _fnet_long(x, mods, mod_base, nb):
    L = FFT_R * FFT_R
    d = D_MODEL
    g1, g2 = _fn2_tables()
    cs = _group_dft_table()
    kf = FNET_KF
    bb = pl.pallas_call(
        _fnet_s1_kernel,
        out_shape=jax.ShapeDtypeStruct((nb, FFT_R // kf, kf, FFT_R, d), jnp.uint32),
        grid=(nb, FFT_R // kf),
        in_specs=[pl.BlockSpec((None, FFT_R, kf, d), lambda b, k: (b, 0, k, 0)),
                  pl.BlockSpec((None, 1, 6 * d), lambda b, k: (mod_base + b, 0, 0)),
                  pl.BlockSpec(cs.shape, lambda b, k: (0, 0)),
                  pl.BlockSpec((kf, 2 * FFT_R, 2 * FFT_R), lambda b, k: (k, 0, 0))],
        out_specs=pl.BlockSpec((None, FFT_R // kf, kf, kf, d), lambda b, k: (b, 0, 0, k, 0)),
        scratch_shapes=[pltpu.VMEM((kf * FFT_R, d), BF16), pltpu.VMEM((kf * FFT_R, d), BF16)],
        compiler_params=_cparams(("arbitrary", "arbitrary")),
        name="l1_fnet_stage1",
    )(x.reshape(nb, FFT_R, FFT_R, d), mods, cs, g1)
    y = _lead_out(g2, bb, [], "l1_fnet_stage2")
    return y.reshape(nb * L, d)


def _group_dft_table():
    g = FN_GROUP_CH
    jk = (np.arange(g, dtype=np.int64)[:, None] * np.arange(g, dtype=np.int64)[None, :]) % g
    ang = 2.0 * np.pi * jk / g
    return jnp.asarray(np.concatenate([np.cos(ang), np.sin(ang)], axis=1), F32).astype(BF16)


def _fnet_front_kernel(x_ref, m_ref, cs_ref, zc_ref, zs_ref):
    m = m_ref[...]
    h = (_ln_plain(x_ref[...]) * (1.0 + m[:, D_MODEL:2 * D_MODEL]) + m[:, 0:D_MODEL]).astype(BF16)
    cs = cs_ref[...]
    for g in range(D_MODEL // FN_GROUP_CH):
        a = g * FN_GROUP_CH
        z = jnp.dot(h[:, a:a + FN_GROUP_CH], cs, preferred_element_type=F32)
        zc_ref[:, a:a + FN_GROUP_CH] = z[:, :FN_GROUP_CH].astype(BF16)
        zs_ref[:, a:a + FN_GROUP_CH] = z[:, FN_GROUP_CH:].astype(BF16)


def _fnet_front(x, mods, mod_base, tiles_per_mod):
    t = x.shape[0]
    tm = 2 * ROW_TILE
    tiles_per_mod = max(tiles_per_mod // 2, 1)
    cs = _group_dft_table()
    return pl.pallas_call(
        _fnet_front_kernel,
        out_shape=(jax.ShapeDtypeStruct((t, D_MODEL), BF16),) * 2,
        grid=(t // tm,),
        in_specs=[pl.BlockSpec((tm, D_MODEL), lambda i: (i, 0)),
                  _mod_spec(mod_base, tiles_per_mod),
                  _const_spec(cs.shape)],
        out_specs=(pl.BlockSpec((tm, D_MODEL), lambda i: (i, 0)),) * 2,
        compiler_params=_cparams(("arbitrary",)),
        name="l1_fnet_front",
    )(x, mods, cs)


def _post_kernel(n_a, *refs):
    x_ref, m_ref = refs[0], refs[1]
    a_refs = refs[2:2 + n_a]
    wo_refs = refs[2 + n_a:2 + 2 * n_a]
    g1_ref, b1_ref, w1_ref, w2_ref, g2_ref, b2_ref, o_ref = refs[2 + 2 * n_a:]
    m = m_ref[...]
    d = D_MODEL
    tm = x_ref.shape[0]
    halves = [(r, r + tm // POST_SPLIT) for r in range(0, tm, tm // POST_SPLIT)]
    outs = []
    for r0, r1 in halves:
        out = _bdot(a_refs[0][r0:r1, :], wo_refs[0][...])
        for a_ref, wo_ref in zip(a_refs[1:], wo_refs[1:]):
            out += _bdot(a_ref[r0:r1, :], wo_ref[...])
        outs.append(out)
    x1s, hs = [], []
    for (r0, r1), out in zip(halves, outs):
        x1 = _ln_plain(ALPHA * x_ref[r0:r1, :] + m[:, 2 * d:3 * d] * out) * g1_ref[...] + b1_ref[...]
        x1s.append(x1)
        hs.append((_ln_plain(x1) * (1.0 + m[:, 4 * d:5 * d]) + m[:, 3 * d:4 * d]).astype(BF16))
    accs = []
    n_c = D_FF // d

    def up(h, c):
        hc = jnp.maximum(jnp.dot(h, w1_ref[:, c * d:(c + 1) * d], preferred_element_type=F32), 0.0)
        return (hc * hc).astype(BF16)

    for h in hs:
        acc = None
        nxt = up(h, 0)
        for c in range(n_c):
            cur = nxt
            if c + 1 < n_c:
                nxt = up(h, c + 1)
            part = jnp.dot(cur, w2_ref[c * d:(c + 1) * d, :], preferred_element_type=F32)
            acc = part if acc is None else acc + part
        accs.append(acc)
    for (r0, r1), x1, acc in zip(halves, x1s, accs):
        o_ref[r0:r1, :] = _ln_plain(ALPHA * x1 + m[:, 5 * d:6 * d] * acc) * g2_ref[...] + b2_ref[...]


def _post(x, mods, mod_base, tiles_per_mod, a_list, wo_list, g1, b1, w1, w2, g2, b2):
    t = x.shape[0]
    tm = POST_TILE
    row = lambda c: pl.BlockSpec((tm, c), lambda i: (i, 0))
    once = lambda v: pl.BlockSpec(v.shape, lambda i: (0,) * v.ndim, pipeline_mode=pl.Buffered(1))
    in_specs = ([row(D_MODEL), _mod_spec(mod_base, tiles_per_mod * ROW_TILE // tm)]
                + [row(a.shape[1]) for a in a_list]
                + [once(w) for w in wo_list]
                + [once(v) for v in (g1, b1, w1, w2, g2, b2)])
    return pl.pallas_call(
        functools.partial(_post_kernel, len(a_list)),
        out_shape=jax.ShapeDtypeStruct((t, D_MODEL), F32),
        grid=(t // tm,),
        in_specs=in_specs,
        out_specs=row(D_MODEL),
        compiler_params=_cparams(("arbitrary",)),
        name="post_mlp",
    )(x, mods, *a_list, *wo_list, g1, b1, w1, w2, g2, b2)


def _rot_cols(w):
    parts = []
    for seg in range(2):
        o = seg * 32
        parts += [-w[:, o + 16:o + 32], w[:, o:o + 16]]
    return jnp.concatenate(parts, axis=1)


def _pad_cols(w, n):
    return jnp.pad(w, ((0, 0), (0, n - w.shape[1])))


def _block_diag2(w):
    z = jnp.zeros_like(w)
    return jnp.concatenate([jnp.concatenate([w, z], axis=1), jnp.concatenate([z, w], axis=1)], axis=0)


def _rope_tables(L):
    rows = L // GRID_W
    row = np.repeat(np.arange(rows, dtype=np.float64), GRID_W)
    col = np.tile(np.arange(GRID_W, dtype=np.float64), rows)
    half = QK_ROPE // 2
    inv = 1.0 / (ROPE_THETA ** (np.arange(0, half, 2, dtype=np.float64) / half))
    ar = row[:, None] * inv[None, :]
    ac = col[:, None] * inv[None, :]
    ang = np.concatenate([ar, ar, ac, ac], axis=1)
    cos = np.concatenate([np.cos(ang), np.ones_like(ang)], axis=1)
    sin = np.concatenate([np.sin(ang), np.zeros_like(ang)], axis=1)
    return jnp.asarray(cos, F32), jnp.asarray(sin, F32)


def kernel(x_prompt, x_sample, cache_l0_ckv, cache_l0_krope, c, c_ctx, l0_ada_w, l0_ada_b, l0_w_in, l0_conv_w, l0_conv_b, l0_hf_w1, l0_hf_b1, l0_hf_freq, l0_hf_w2, l0_hf_b2, l0_hf_w3, l0_hf_skip, l0_q_norm, l0_q_up, l0_kv_norm, l0_kv_up, l0_w_out, l0_ln1_g, l0_ln1_b, l0_mlp_w1, l0_mlp_w2, l0_ln2_g, l0_ln2_b, l1_ada_w, l1_ada_b, l1_w_out, l1_ln1_g, l1_ln1_b, l1_mlp_w1, l1_mlp_w2, l1_ln2_g, l1_ln2_b):
    nbc, lc, d = x_prompt.shape
    nbs, ls, _ = x_sample.shape
    past = cache_l0_ckv.shape[1]
    tm = ROW_TILE
    row1 = lambda v: v.reshape(1, -1)

    cond8 = jnp.concatenate([c_ctx[None, :], c, jnp.zeros((8 - 1 - nbs, d), F32)], axis=0)
    mods0 = _modulation(cond8, l0_ada_w, l0_ada_b)
    mods1 = _modulation(cond8, l1_ada_w, l1_ada_b)

    kpe_w = l0_w_in[:, 1920:1984]
    win = jnp.concatenate([l0_w_in[:, :1920], _pad_cols(kpe_w, LANE), _pad_cols(_rot_cols(kpe_w), LANE)],
                          axis=1).astype(BF16)
    dh = QK_NOPE + QK_ROPE
    q_nope = [l0_q_up[:, h * dh:h * dh + QK_NOPE] for h in range(MLA_HEADS)]
    q_pe = [l0_q_up[:, h * dh + QK_NOPE:(h + 1) * dh] for h in range(MLA_HEADS)]
    qup = jnp.concatenate(q_nope + [_pad_cols(w, LANE) for w in q_pe]
                          + [_pad_cols(_rot_cols(w), LANE) for w in q_pe], axis=1).astype(BF16)
    kvup = l0_kv_up.astype(BF16)
    front_w = (win, row1(l0_q_norm), qup, row1(l0_kv_norm), kvup, l0_conv_w, row1(l0_conv_b))
    skip = row1(l0_hf_skip)
    w1p = jnp.pad(l0_hf_w1, ((0, LANE - l0_hf_w1.shape[0]), (0, 0)))
    two = lambda v: jnp.tile(row1(v), (1, 2))
    filt_w = (_block_diag2(w1p), two(l0_hf_b1), two(l0_hf_freq), _block_diag2(l0_hf_w2), two(l0_hf_b2),
              _block_diag2(l0_hf_w3).astype(BF16))
    wo0 = l0_w_out.astype(BF16)

    xc = x_prompt.reshape(nbc * lc, d)
    xs = x_sample.reshape(nbs * ls, d)
    groups = (
        dict(x=xc, nb=nbc, L=lc, mod_base=0, tiles_per_mod=nbc * lc // tm, dft_nb=8, tq=lc, hps=MLA_HEADS),
        dict(x=xs, nb=nbs, L=ls, mod_base=1, tiles_per_mod=ls // tm, dft_nb=nbs, tq=512, hps=1),
    )
    ones_tab = (jnp.ones((FRONT_TILE, LANE), F32), jnp.zeros((FRONT_TILE, LANE), F32))

    outs = []
    ctx_ckv = ctx_krope = None
    for gi, g in enumerate(groups):
        nb, L = g["nb"], g["L"]
        latent = gi == 1
        cos, sin = _rope_tables(L) if latent else ones_tab
        two_stage = L == FFT_R * FFT_R
        u, x0, q, k, vt, kvn, kpe = _front(g["x"], mods0, g["mod_base"], g["tiles_per_mod"] * tm, front_w,
                                           cos, sin, latent, L, min(L, FRONT_TILE))
        if latent:
            extra = _cache_kv(cache_l0_ckv.reshape(nbs * past, KV_LORA),
                              _pad_cols(cache_l0_krope.reshape(nbs * past, QK_ROPE), LANE), kvup)
        else:
            extra = None
            ctx_ckv = kvn.reshape(nb, L, KV_LORA)
            ctx_krope = kpe.reshape(nb, L, QK_ROPE)
        y_mla = _attention(q, k, vt, extra, nb, L, g["tq"], g["hps"])

        hfilt, hnorm = _filters(L, *filt_w)
        if two_stage:
            y_hy = _hyena_long(u, x0, skip, hfilt, hnorm, nb)
        else:
            kre, kim = _dft("filt", "hy_fwd", [hfilt], [hnorm], 2)
            sh = (nb, L, HY_CH)
            yre, yim = _dft("fwdk", "hy_fwd", [u.reshape(sh)], [kre, kim], g["dft_nb"])
            y_hy = _dft("inv", "hy_inv", [yre, yim], [u.reshape(sh), x0.reshape(sh), skip], g["dft_nb"])
            y_hy = y_hy.reshape(nb * L, HY_CH)

        x1 = _post(g["x"], mods0, g["mod_base"], g["tiles_per_mod"], [y_hy, y_mla],
                   [wo0[:HY_CH], wo0[HY_CH:]], row1(l0_ln1_g), row1(l0_ln1_b),
                   l0_mlp_w1.astype(BF16), l0_mlp_w2.astype(BF16), row1(l0_ln2_g), row1(l0_ln2_b))

        if L == FFT_R * FFT_R:
            yf = _fnet_long(x1, mods1, g["mod_base"], nb)
        else:
            zc, zs = _fnet_front(x1, mods1, g["mod_base"], g["tiles_per_mod"])
            sh = (nb, L, d)
            yf = _dft("fnet", "fnet", [zc.reshape(sh), zs.reshape(sh)], [], g["dft_nb"] // 2)
            yf = yf.reshape(nb * L, d)
        x2 = _post(x1, mods1, g["mod_base"], g["tiles_per_mod"], [yf],
                   [l1_w_out.astype(BF16)], row1(l1_ln1_g), row1(l1_ln1_b),
                   l1_mlp_w1.astype(BF16), l1_mlp_w2.astype(BF16), row1(l1_ln2_g), row1(l1_ln2_b))
        outs.append(x2.reshape(nb, L, d))

    return (outs[0], outs[1], ctx_ckv, ctx_krope)
```

```python
import functools
import math

import numpy as np
import jax
import jax.numpy as jnp
from jax import lax
from jax.experimental import pallas as pl
from jax.experimental.pallas import tpu as pltpu

F32 = jnp.float32
BF16 = jnp.bfloat16
HI = lax.Precision.HIGHEST

D_MODEL = 1024
DEPTH = 2
GRID_W = 64
HY_CH = 512
FILT_BANDS = 16
FILT_ORDER = 64
FAST_DECAY_PCT = 0.3
SLOW_DECAY_PCT = 1.5
DECAY_TARGET = 1e-2
MAX_DECAY = math.log(DECAY_TARGET) / FAST_DECAY_PCT
MIN_DECAY = math.log(DECAY_TARGET) / SLOW_DECAY_PCT
MLA_HEADS = 4
QK_NOPE = 128
QK_ROPE = 64
V_HEAD = 128
Q_LORA = 256
KV_LORA = 128
ROPE_THETA = 10000.0
FN_GROUP_CH = 128
D_FF = 4096
ALPHA = (2 * DEPTH) ** 0.25
LN_EPS = 1e-5
RMS_EPS = 1e-6

LANE = 128
ROW_TILE = 256
FRONT_TILE = 512
POST_TILE = 512
POST_SPLIT = 2
QK_PAD = 256
VT_ROWS = V_HEAD + 16
LOG2E = 1.4426950408889634
VMEM_LIMIT = 56 * 1024 * 1024


def _cparams(sem):
    return pltpu.CompilerParams(dimension_semantics=sem, vmem_limit_bytes=VMEM_LIMIT)


def _ln_plain(x):
    mu = jnp.mean(x, axis=-1, keepdims=True)
    xc = x - mu
    var = jnp.mean(xc * xc, axis=-1, keepdims=True)
    return xc * lax.rsqrt(var + LN_EPS)


def _rms(x, g):
    return x * lax.rsqrt(jnp.mean(x * x, axis=-1, keepdims=True) + RMS_EPS) * g


def _bdot(a, b):
    return jnp.dot(a.astype(BF16), b, preferred_element_type=F32)


def _vt_rows(v):
    ones = jnp.ones((VT_ROWS - V_HEAD, v.shape[0]), BF16)
    return jnp.concatenate([jnp.transpose(v).astype(BF16), ones], axis=0)


def _mod_kernel(c_ref, w_ref, b_ref, o_ref):
    c = c_ref[...]
    s = c / (1.0 + jnp.exp(-c))
    s_hi = s.astype(BF16)
    s_lo = (s - s_hi.astype(F32)).astype(BF16)
    s2 = jnp.concatenate([s_hi, s_lo], axis=0)
    nr = s.shape[0]
    w = w_ref[...]
    w_hi = w.astype(BF16)
    w_lo = (w - w_hi.astype(F32)).astype(BF16)
    r1 = jnp.dot(s2, w_hi, preferred_element_type=F32)
    r2 = jnp.dot(s_hi, w_lo, preferred_element_type=F32)
    o_ref[...] = r1[:nr] + r1[nr:] + r2 + b_ref[...]


def _modulation(cond8, w, b):
    n = w.shape[1]
    tn = 768
    out = pl.pallas_call(
        _mod_kernel,
        out_shape=jax.ShapeDtypeStruct((8, n), F32),
        grid=(n // tn,),
        in_specs=[pl.BlockSpec((8, D_MODEL), lambda j: (0, 0)),
                  pl.BlockSpec((D_MODEL, tn), lambda j: (0, j)),
                  pl.BlockSpec((1, tn), lambda j: (0, j))],
        out_specs=pl.BlockSpec((8, tn), lambda j: (0, j)),
        compiler_params=_cparams(("arbitrary",)),
        name="modulation",
    )(cond8, w, b.reshape(1, n))
    return out.reshape(8, 1, n)


def _mod_spec(mod_base, tiles_per_mod):
    return pl.BlockSpec((None, 1, 6 * D_MODEL), lambda i: (mod_base + i // tiles_per_mod, 0, 0))


def _const_spec(shape):
    nd = len(shape)
    return pl.BlockSpec(shape, lambda i: (0,) * nd)


HALO = 8


def _front_kernel(tiles_per_seq, x_ref, xp_ref, xn_ref, m_ref, win_ref, qn_ref, qup_ref, kvn_ref, kvup_ref,
                  cos_ref, sin_ref, cw_ref, cb_ref,
                  u_ref, x0_ref, q_ref, k_ref, v_ref, kvn_out_ref, kpe_ref):
    i = pl.program_id(0)
    m = m_ref[...]
    tm = x_ref.shape[0]
    nh = 3 * HY_CH
    xe = jnp.concatenate([xp_ref[...], x_ref[...], xn_ref[...]], axis=0)
    he = _ln_plain(xe) * (1.0 + m[:, D_MODEL:2 * D_MODEL]) + m[:, 0:D_MODEL]

    z = _bdot(he[HALO:HALO + tm], win_ref[:, nh:])
    zh = _bdot(he, win_ref[:, :nh])
    q_c = z[:, 0:256]
    kv_c = z[:, 256:384]
    cos = cos_ref[...]
    sin = sin_ref[...]
    kpe = z[:, 384:512] * cos + z[:, 512:640] * sin
    kpe_ref[...] = kpe[:, :QK_ROPE]
    kpe_b = kpe.astype(BF16)
    q = _bdot(_rms(q_c, qn_ref[...]), qup_ref[...]) * (LOG2E / math.sqrt(QK_NOPE + QK_ROPE))
    kvn = _rms(kv_c, kvn_ref[...])
    kvn_out_ref[...] = kvn
    kv = _bdot(kvn, kvup_ref[...])

    pos = i % tiles_per_seq
    rows = lax.broadcasted_iota(jnp.int32, (tm + 2 * HALO, 1), 0)
    inside = jnp.logical_and(jnp.logical_or(rows >= HALO, pos != 0),
                             jnp.logical_or(rows < tm + HALO, pos != tiles_per_seq - 1))
    zh = jnp.where(inside, zh, 0.0)
    cw = cw_ref[...]
    pz = (pltpu.roll(zh, 1, 0) * cw[0:1, :] + zh * cw[1:2, :]
          + pltpu.roll(zh, tm + 2 * HALO - 1, 0) * cw[2:3, :])[HALO:HALO + tm] + cb_ref[...]
    u_ref[...] = (pz[:, 2 * HY_CH:] * pz[:, HY_CH:2 * HY_CH]).astype(u_ref.dtype)
    x0_ref[...] = pz[:, :HY_CH].astype(x0_ref.dtype)

    for hd in range(MLA_HEADS):
        a = hd * LANE
        q_pe = (q[:, 512 + a:512 + a + LANE] * cos + q[:, 1024 + a:1024 + a + LANE] * sin).astype(BF16)
        q_ref[hd] = jnp.concatenate([q[:, a:a + LANE].astype(BF16), q_pe], axis=-1)
        k_ref[hd] = jnp.concatenate([kv[:, 2 * a:2 * a + LANE].astype(BF16), kpe_b], axis=-1)
        v_ref[hd] = _vt_rows(kv[:, 2 * a + LANE:2 * a + 2 * LANE])


def _front(x, mods, mod_base, rows_per_mod, w, cos, sin, rope, seq_len, tm):
    t = x.shape[0]
    tiles_per_seq = seq_len // tm
    tiles_per_mod = rows_per_mod // tm
    win, qn, qup, kvn, kvup, conv_w, conv_b = w
    if rope:
        tab_spec = pl.BlockSpec((tm, LANE), lambda i: (i % tiles_per_seq, 0))
    else:
        tab_spec = pl.BlockSpec((tm, LANE), lambda i: (0, 0))
    r8 = tm // HALO
    n8 = t // HALO
    hy_out = lambda dt: jax.ShapeDtypeStruct((t, HY_CH), dt)
    hy_spec = pl.BlockSpec((tm, HY_CH), lambda i: (i, 0))
    return pl.pallas_call(
        functools.partial(_front_kernel, tiles_per_seq),
        out_shape=(hy_out(BF16), hy_out(BF16),
                   jax.ShapeDtypeStruct((MLA_HEADS, t, QK_PAD), BF16),
                   jax.ShapeDtypeStruct((MLA_HEADS, t, QK_PAD), BF16),
                   jax.ShapeDtypeStruct((MLA_HEADS, VT_ROWS, t), BF16),
                   jax.ShapeDtypeStruct((t, KV_LORA), F32),
                   jax.ShapeDtypeStruct((t, QK_ROPE), F32)),
        grid=(t // tm,),
        in_specs=[pl.BlockSpec((tm, D_MODEL), lambda i: (i, 0)),
                  pl.BlockSpec((HALO, D_MODEL), lambda i: (jnp.maximum(i * r8 - 1, 0), 0)),
                  pl.BlockSpec((HALO, D_MODEL), lambda i: (jnp.minimum((i + 1) * r8, n8 - 1), 0)),
                  _mod_spec(mod_base, tiles_per_mod),
                  _const_spec(win.shape), _const_spec(qn.shape), _const_spec(qup.shape),
                  _const_spec(kvn.shape), _const_spec(kvup.shape),
                  tab_spec, tab_spec,
                  _const_spec(conv_w.shape), _const_spec(conv_b.shape)],
        out_specs=(hy_spec, hy_spec,
                   pl.BlockSpec((MLA_HEADS, tm, QK_PAD), lambda i: (0, i, 0)),
                   pl.BlockSpec((MLA_HEADS, tm, QK_PAD), lambda i: (0, i, 0)),
                   pl.BlockSpec((MLA_HEADS, VT_ROWS, tm), lambda i: (0, 0, i)),
                   pl.BlockSpec((tm, KV_LORA), lambda i: (i, 0)),
                   pl.BlockSpec((tm, QK_ROPE), lambda i: (i, 0))),
        compiler_params=_cparams(("arbitrary",)),
        name="l0_front",
    )(x, x, x, mods, win, qn, qup, kvn, kvup, cos, sin, conv_w, conv_b)


def _cache_kv_kernel(ckv_ref, kr_ref, kvup_ref, k_ref, v_ref):
    kv = _bdot(ckv_ref[...], kvup_ref[...])
    kr = kr_ref[...].astype(BF16)
    for hd in range(MLA_HEADS):
        a = 2 * hd * LANE
        k_ref[hd] = jnp.concatenate([kv[:, a:a + LANE].astype(BF16), kr], axis=-1)
        v_ref[hd] = _vt_rows(kv[:, a + LANE:a + 2 * LANE])


def _cache_kv(ckv, krope_pad, kvup):
    t = ckv.shape[0]
    return pl.pallas_call(
        _cache_kv_kernel,
        out_shape=(jax.ShapeDtypeStruct((MLA_HEADS, t, QK_PAD), BF16),
                   jax.ShapeDtypeStruct((MLA_HEADS, VT_ROWS, t), BF16)),
        name="l0_cache_kv",
    )(ckv, krope_pad, kvup)


def _col_reduce(x, op):
    rows, n = x.shape
    for g in (32, 8):
        if rows % (8 * g) == 0 and rows > 8 * g:
            x = op(x.reshape(rows // (8 * g), 8 * g, n), axis=0)
            rows = 8 * g
    return op(x, axis=0, keepdims=True)


PV_CHUNK = 128


def _attn_kernel(n_kv, hps, q_ref, *refs):
    k_refs, vt_refs = refs[:n_kv], refs[n_kv:2 * n_kv]
    o_ref, s_even, s_odd, m_even, m_odd = refs[2 * n_kv:]
    i = pl.program_id(0)

    @pl.when(i == 0)
    def _():
        s_odd[...] = jnp.zeros_like(s_odd)
        m_odd[...] = jnp.zeros_like(m_odd)

    def step(s_write, m_write, s_read, m_read):
        nt = (((1,), (1,)), ((), ()))
        for h in range(hps):
            q = q_ref[h]
            r0 = 0
            m = None
            for k_ref in k_refs:
                lk = k_ref.shape[1]
                sblk = lax.dot_general(k_ref[h], q, nt, preferred_element_type=F32)
                s_write[h, r0:r0 + lk, :] = sblk
                mc = _col_reduce(sblk, jnp.max)
                m = mc if m is None else jnp.maximum(m, mc)
                r0 += lk
            m_write[h] = m
        for h in range(hps):
            m = m_read[h]
            acc = None
            r0 = 0
            for vt_ref in vt_refs:
                lk = vt_ref.shape[2]
                for c0 in range(0, lk, PV_CHUNK):
                    c1 = min(c0 + PV_CHUNK, lk)
                    pb = jnp.exp2(s_read[h, r0 + c0:r0 + c1, :] - m).astype(BF16)
                    pv = jnp.dot(vt_ref[h, :, c0:c1], pb, preferred_element_type=F32)
                    acc = pv if acc is None else acc + pv
                r0 += lk
            o_ref[:, h * V_HEAD:(h + 1) * V_HEAD] = jnp.transpose(
                acc[:V_HEAD] / acc[V_HEAD:V_HEAD + 1]).astype(o_ref.dtype)

    pl.when(i % 2 == 0)(lambda: step(s_even, m_even, s_odd, m_odd))
    pl.when(i % 2 == 1)(lambda: step(s_odd, m_odd, s_even, m_even))


def _attention(q, k, vt, extra, nb, lq, tq, hps):
    nq = lq // tq
    ng = MLA_HEADS // hps
    n_tiles = nb * ng * nq

    def where(t):
        bh = t // nq
        return bh // ng, bh % ng, t % nq

    def score_side(fn):
        return lambda i: fn(*where(jnp.minimum(i, n_tiles - 1)))

    def value_side(fn):
        return lambda i: fn(*where(jnp.maximum(i - 1, 0)))

    ks, vts = [k], [vt]
    if extra is not None:
        ks.append(extra[0])
        vts.append(extra[1])
    in_specs = [pl.BlockSpec((hps, tq, QK_PAD), score_side(lambda b, h, j: (h, b * nq + j, 0)))]
    in_specs += [pl.BlockSpec((hps, a.shape[1] // nb, QK_PAD), score_side(lambda b, h, j: (h, b, 0))) for a in ks]
    in_specs += [pl.BlockSpec((hps, VT_ROWS, a.shape[2] // nb), value_side(lambda b, h, j: (h, 0, b))) for a in vts]
    lk_total = sum(a.shape[1] // nb for a in ks)
    return pl.pallas_call(
        functools.partial(_attn_kernel, len(ks), hps),
        out_shape=jax.ShapeDtypeStruct((nb * lq, MLA_HEADS * V_HEAD), BF16),
        grid=(n_tiles + 1,),
        in_specs=in_specs,
        out_specs=pl.BlockSpec((tq, hps * V_HEAD), value_side(lambda b, h, j: (b * nq + j, h))),
        scratch_shapes=[pltpu.VMEM((hps, lk_total, tq), F32), pltpu.VMEM((hps, lk_total, tq), F32),
                        pltpu.VMEM((hps, 1, tq), F32), pltpu.VMEM((hps, 1, tq), F32)],
        compiler_params=_cparams(("arbitrary",)),
        name="l0_attention",
    )(q, *ks, *vts)


def _filter_kernel(z_ref, w1_ref, b1_ref, fr_ref, w2_ref, b2_ref, w3_ref, dl_ref, h_ref, norm_ref):
    i = pl.program_id(0)
    z = z_ref[...]
    tl = z.shape[0]
    fr = fr_ref[...]
    z2 = jnp.concatenate([z[:tl // 2], z[tl // 2:]], axis=1)
    h = jnp.sin(fr * (jnp.dot(z2, w1_ref[...], precision=HI, preferred_element_type=F32) + b1_ref[...]))
    h = jnp.sin(fr * (jnp.dot(h, w2_ref[...], precision=HI, preferred_element_type=F32) + b2_ref[...]))
    h = _bdot(h, w3_ref[...])
    h = jnp.concatenate([h[:, :2 * HY_CH], h[:, 2 * HY_CH:]], axis=0)
    decay = jnp.exp(-(z[:, 0:1] * dl_ref[...]))
    hf = h[:, :HY_CH] * decay
    hb = h[:, HY_CH:] * decay
    part = jnp.sum(jnp.abs(hf) + jnp.abs(hb), axis=0, keepdims=True)

    @pl.when(i == 0)
    def _():
        norm_ref[...] = part

    @pl.when(i > 0)
    def _():
        norm_ref[...] += part

    rows = lax.broadcasted_iota(jnp.int32, hb.shape, 0) + i * tl
    h_ref[0] = hf.astype(h_ref.dtype)
    h_ref[1] = jnp.where(rows == 0, 0.0, hb).astype(h_ref.dtype)


def _filter_embedding(L):
    t = np.linspace(0.0, 1.0, L)[:, None]
    w_ang = 2.0 * np.pi * np.arange(L) / L
    bands = np.linspace(1e-4, FILT_BANDS - 1, FILT_BANDS)
    ang = w_ang[:, None] * bands[None, :]
    z = np.zeros((L, LANE), np.float64)
    z[:, 0:1] = t
    z[:, 1:1 + FILT_BANDS] = np.cos(ang)
    z[:, 1 + FILT_BANDS:1 + 2 * FILT_BANDS] = -np.sin(ang)
    return jnp.asarray(z, F32)


def _filters(L, w1p, b1, fr, w2, b2, w3):
    tl = min(L, 1024)
    z = _filter_embedding(L)
    deltas = jnp.asarray(np.abs(np.linspace(MIN_DECAY, MAX_DECAY, HY_CH))[None, :], F32)
    return pl.pallas_call(
        _filter_kernel,
        out_shape=(jax.ShapeDtypeStruct((2, L, HY_CH), BF16), jax.ShapeDtypeStruct((1, HY_CH), F32)),
        grid=(L // tl,),
        in_specs=[pl.BlockSpec((tl, LANE), lambda i: (i, 0)),
                  _const_spec(w1p.shape), _const_spec(b1.shape), _const_spec(fr.shape),
                  _const_spec(w2.shape), _const_spec(b2.shape), _const_spec(w3.shape),
                  _const_spec(deltas.shape)],
        out_specs=(pl.BlockSpec((2, tl, HY_CH), lambda i: (0, i, 0)),
                   pl.BlockSpec((1, HY_CH), lambda i: (0, 0))),
        compiler_params=_cparams(("arbitrary",)),
        name="l0_hyena_filters",
    )(z, w1p, b1, fr, w2, b2, w3, deltas)


def _dft_tables(kind, L, ti):
    ni = L // ti
    i = np.arange(ti, dtype=np.int64)[:, None]
    big = (np.arange(ni, dtype=np.int64) * ti)[:, None]
    c = np.arange(L, dtype=np.int64)[None, :]
    if kind == "hy_fwd":
        period = 4 * L
        base_idx = (2 * i + 1) * c
        r_idx = 2 * big * c
        scale = 1.0
    elif kind == "hy_inv":
        period = 4 * L
        base_idx = (2 * c + 1) * i
        r_idx = (2 * c + 1) * big
        scale = 1.0 / L
    else:
        period = L
        base_idx = i * c
        r_idx = big * c
        scale = 1.0 / math.sqrt(L * FN_GROUP_CH)
    ab = 2.0 * np.pi * (base_idx % period) / period
    ar = 2.0 * np.pi * (r_idx % period) / period
    return (jnp.asarray(np.cos(ab), F32), jnp.asarray(np.sin(ab), F32),
            jnp.asarray(scale * np.cos(ar), F32).reshape(ni, 1, L),
            jnp.asarray(scale * np.sin(ar), F32).reshape(ni, 1, L))


def _dft_kernel(mode, nb, n_x, *refs):
    bc_ref, bs_ref, rc_ref, rs_ref = refs[:4]
    x_refs = refs[4:4 + n_x]
    rest = refs[4 + n_x:]
    p_ref, q_ref = rest[-2], rest[-1]
    j = pl.program_id(2)
    nj = pl.num_programs(2)
    tj = x_refs[0].shape[1]
    if bc_ref.shape[1] == tj:
        bc, bs, rc, rs = bc_ref[...], bs_ref[...], rc_ref[...], rs_ref[...]
    else:
        off = pl.multiple_of(j * tj, tj)
        bc, bs = bc_ref[:, pl.ds(off, tj)], bs_ref[:, pl.ds(off, tj)]
        rc, rs = rc_ref[:, pl.ds(off, tj)], rs_ref[:, pl.ds(off, tj)]
    tc = (bc * rc - bs * rs).astype(BF16)
    ts = (bs * rc + bc * rs).astype(BF16)
    x1_ref = x_refs[0]
    x2_ref = x_refs[-1]

    pq = [(jnp.dot(tc, x1_ref[b].astype(BF16), preferred_element_type=F32),
           jnp.dot(ts, x2_ref[b].astype(BF16), preferred_element_type=F32)) for b in range(nb)]

    @pl.when(j == 0)
    def _():
        for b in range(nb):
            p_ref[b] = pq[b][0]
            q_ref[b] = pq[b][1]

    @pl.when(j > 0)
    def _():
        for b in range(nb):
            p_ref[b] += pq[b][0]
            q_ref[b] += pq[b][1]

    @pl.when(j == nj - 1)
    def _():
        if mode == "filt":
            nrm = rest[0][...]
            kre_ref, kim_ref = rest[1], rest[2]
            kre_ref[...] = (p_ref[0] + p_ref[1]) / nrm
            kim_ref[...] = (q_ref[1] - q_ref[0]) / nrm
        elif mode == "fwdk":
            kre, kim = rest[0][...], rest[1][...]
            yre_ref, yim_ref = rest[2], rest[3]
            for b in range(nb):
                pp, qq = p_ref[b], q_ref[b]
                yre_ref[b] = (pp * kre + qq * kim).astype(BF16)
                yim_ref[b] = (pp * kim - qq * kre).astype(BF16)
        elif mode == "inv":
            u_ref, x0_ref, skip_ref, o_ref = rest[0], rest[1], rest[2], rest[3]
            skip = skip_ref[...]
            for b in range(nb):
                o_ref[b] = ((p_ref[b] - q_ref[b] + u_ref[b] * skip) * x0_ref[b].astype(F32)).astype(BF16)
        else:
            o_ref = rest[0]
            for b in range(nb):
                o_ref[b] = (p_ref[b] - q_ref[b]).astype(BF16)


def _dft(mode, kind, xs, extras, nb):
    B, L, C = xs[0].shape
    ti = min(L, 256)
    tj = min(L, 512)
    bc, bs, rc, rs = _dft_tables(kind, L, ti)
    grid = (B // nb, L // ti, L // tj)
    x_spec = pl.BlockSpec((nb, tj, C), lambda g, i, j: (g, j, 0))
    row_spec = lambda c, dt=None: pl.BlockSpec((nb, ti, c), lambda g, i, j: (g, i, 0))
    in_specs = [pl.BlockSpec((ti, L), lambda g, i, j: (0, 0)),
                pl.BlockSpec((ti, L), lambda g, i, j: (0, 0)),
                pl.BlockSpec((None, 1, L), lambda g, i, j: (i, 0, 0)),
                pl.BlockSpec((None, 1, L), lambda g, i, j: (i, 0, 0))] + [x_spec] * len(xs)
    if mode == "filt":
        in_specs += [pl.BlockSpec((1, HY_CH), lambda g, i, j: (0, 0))]
        out_shape = (jax.ShapeDtypeStruct((L, HY_CH), F32),) * 2
        out_specs = (pl.BlockSpec((ti, HY_CH), lambda g, i, j: (i, 0)),) * 2
    elif mode == "fwdk":
        in_specs += [pl.BlockSpec((ti, HY_CH), lambda g, i, j: (i, 0))] * 2
        out_shape = (jax.ShapeDtypeStruct((B, L, C), BF16),) * 2
        out_specs = (row_spec(C),) * 2
    elif mode == "inv":
        in_specs += [row_spec(C)] * 2 + [pl.BlockSpec((1, C), lambda g, i, j: (0, 0))]
        out_shape = jax.ShapeDtypeStruct((B, L, C), BF16)
        out_specs = row_spec(C)
    else:
        out_shape = jax.ShapeDtypeStruct((B, L, C), BF16)
        out_specs = row_spec(C)
    return pl.pallas_call(
        functools.partial(_dft_kernel, mode, nb, len(xs)),
        out_shape=out_shape,
        grid=grid,
        in_specs=in_specs,
        out_specs=out_specs,
        scratch_shapes=[pltpu.VMEM((nb, ti, C), F32), pltpu.VMEM((nb, ti, C), F32)],
        compiler_params=_cparams(("arbitrary", "arbitrary", "arbitrary")),
        name="dft_" + mode,
    )(bc, bs, rc, rs, *xs, *extras)


FFT_R = 64
HYENA_KF = 16
FNET_KF = 16


def _pack_pairs(x):
    return pltpu.bitcast(x.astype(BF16), jnp.uint32)


def _unpack_pairs(w):
    return pltpu.bitcast(w, BF16)


def _to_blocks(w):
    return jnp.swapaxes(w.reshape(FFT_R, w.shape[-2], w.shape[-1]), 0, 1)


def _from_blocks(ws):
    kf, c = len(ws), ws[0].shape[-1]
    return jnp.swapaxes(jnp.stack(ws, axis=0), 0, 1).reshape(FFT_R // kf, kf, kf, c)


def _lead_in_kernel(g_ref, x_ref, o_ref):
    g = g_ref[...]
    xt = jnp.swapaxes(x_ref[...], 0, 1)
    for j in range(x_ref.shape[1]):
        o_ref[j] = _pack_pairs(jnp.dot(g, xt[j].astype(BF16), preferred_element_type=F32))


def _lead_in(g, x, kf, name):
    nbx, _, _, c = x.shape
    m2 = g.shape[0] // 2
    return pl.pallas_call(
        _lead_in_kernel,
        out_shape=jax.ShapeDtypeStruct((nbx, FFT_R // kf, kf, m2, c), jnp.uint32),
        grid=(nbx, FFT_R // kf),
        in_specs=[pl.BlockSpec(g.shape, lambda b, k: (0, 0)),
                  pl.BlockSpec((None, FFT_R, kf, c), lambda b, k: (b, 0, k, 0))],
        out_specs=pl.BlockSpec((None, None, kf, m2, c), lambda b, k: (b, k, 0, 0, 0)),
        compiler_params=_cparams(("arbitrary", "arbitrary")),
        name=name,
    )(g, x)


def _lead_out_kernel(n_extra, g_ref, w_ref, *rest):
    g = g_ref[...]
    o_ref = rest[-1]
    ys = [jnp.dot(g, _unpack_pairs(w_ref[j]), preferred_element_type=F32) for j in range(w_ref.shape[0])]
    y = jnp.swapaxes(jnp.stack(ys, axis=0), 0, 1)
    if n_extra:
        y = (y + rest[0][...] * rest[2][...]) * rest[1][...].astype(F32)
    o_ref[...] = y.astype(o_ref.dtype)


def _lead_out(g, w, epilogue, name):
    nb, nk, kf, k2, c = w.shape
    blk = pl.BlockSpec((None, FFT_R, kf, c), lambda b, k: (b, 0, k, 0))
    extra_specs = [blk, blk, pl.BlockSpec((1, c), lambda b, k: (0, 0))] if epilogue else []
    return pl.pallas_call(
        functools.partial(_lead_out_kernel, len(epilogue)),
        out_shape=jax.ShapeDtypeStruct((nb, FFT_R, FFT_R, c), BF16),
        grid=(nb, nk),
        in_specs=[pl.BlockSpec(g.shape, lambda b, k: (0, 0)),
                  pl.BlockSpec((None, None, kf, k2, c), lambda b, k: (b, k, 0, 0, 0))] + extra_specs,
        out_specs=blk,
        compiler_params=_cparams(("arbitrary", "arbitrary")),
        name=name,
    )(g, w, *epilogue)


def _interleave(a, b, axis):
    st = np.stack([a, b], axis=axis + 1)
    shape = list(a.shape)
    shape[axis] *= 2
    return st.reshape(shape)


def _hy2_tables():
    L = FFT_R * FFT_R
    n2 = 2 * L
    f1 = np.arange(2 * FFT_R, dtype=np.int64)
    s1 = np.arange(FFT_R, dtype=np.int64)
    th = np.pi * (((2 * f1[:, None] + 1) * s1[None, :]) % (4 * FFT_R)) / (2 * FFT_R)
    ga = _interleave(np.cos(th), -np.sin(th), 0)
    ma = _interleave(np.cos(th).T, -np.sin(th).T, 1) / L
    f2 = np.arange(FFT_R // 2, dtype=np.int64)
    s2 = np.arange(FFT_R, dtype=np.int64)
    idx = ((n2 // FFT_R) * 2 * f2[None, :, None] * s2[None, None, :]
           + (2 * f1[:, None, None] + 1) * s2[None, None, :]) % (2 * n2)
    al = np.pi * idx / n2
    c, s = np.cos(al), np.sin(al)
    nmat = np.concatenate([_interleave(c, s, 2), _interleave(-s, c, 2)], axis=1)
    ct, st = np.transpose(c, (0, 2, 1)), np.transpose(s, (0, 2, 1))
    mmat = _interleave(np.concatenate([ct, -st], axis=2), np.concatenate([st, ct], axis=2), 1)
    bf = lambda a: jnp.asarray(a, F32).astype(BF16)
    return bf(ga), bf(nmat), bf(mmat), bf(ma)


def _hy_mid_kernel(a_ref, n_ref, m_ref, k_ref, e_ref):
    half = FFT_R // 2
    kf = a_ref.shape[-2]
    a = _to_blocks(a_ref[...])
    ts = [jnp.dot(n_ref[j], _unpack_pairs(a[j]), preferred_element_type=F32)
          for j in range(kf)]
    ys = []
    for j in range(kf):
        tr, ti = ts[j][:half], ts[j][half:]
        kr, ki = k_ref[j, 0].astype(F32), k_ref[j, 1].astype(F32)
        ys.append(jnp.concatenate([tr * kr - ti * ki, tr * ki + ti * kr], axis=0).astype(BF16))
    e_ref[...] = _from_blocks([_pack_pairs(jnp.dot(m_ref[j], ys[j], preferred_element_type=F32))
                               for j in range(kf)])


def _hy_mid(a, nmat, mmat, khat):
    nb, nk, kf, _, c = a.shape
    nf1 = 2 * FFT_R
    blk = pl.BlockSpec((None, nk, kf, kf, c), lambda i, b: (b, 0, 0, i, 0))
    return pl.pallas_call(
        _hy_mid_kernel,
        out_shape=jax.ShapeDtypeStruct(a.shape, jnp.uint32),
        grid=(nf1 // kf, nb),
        in_specs=[blk,
                  pl.BlockSpec((kf, FFT_R, 2 * FFT_R), lambda i, b: (i, 0, 0)),
                  pl.BlockSpec((kf, 2 * FFT_R, FFT_R), lambda i, b: (i, 0, 0)),
                  pl.BlockSpec((kf, 2, FFT_R // 2, c), lambda i, b: (i, 0, 0, 0))],
        out_specs=blk,
        compiler_params=_cparams(("arbitrary", "arbitrary")),
        name="l0_hyena_mid",
    )(a, nmat, mmat, khat)


def _hy_kfilt_kernel(a_ref, n_ref, nrm_ref, k_ref):
    half = FFT_R // 2
    nrm = nrm_ref[...]
    af, ab = _to_blocks(a_ref[0]), _to_blocks(a_ref[1])
    for j in range(a_ref.shape[-2]):
        tf = jnp.dot(n_ref[j], _unpack_pairs(af[j]), preferred_element_type=F32)
        tb = jnp.dot(n_ref[j], _unpack_pairs(ab[j]), preferred_element_type=F32)
        k_ref[j, 0] = ((tf[:half] + tb[:half]) / nrm).astype(k_ref.dtype)
        k_ref[j, 1] = ((tf[half:] - tb[half:]) / nrm).astype(k_ref.dtype)


def _hy_kfilt(a, nmat, nrm):
    _, nk, kf, _, c = a.shape
    nf1 = 2 * FFT_R
    return pl.pallas_call(
        _hy_kfilt_kernel,
        out_shape=jax.ShapeDtypeStruct((nf1, 2, FFT_R // 2, c), BF16),
        grid=(nf1 // kf,),
        in_specs=[pl.BlockSpec((2, nk, kf, kf, c), lambda i: (0, 0, 0, i, 0)),
                  pl.BlockSpec((kf, FFT_R, 2 * FFT_R), lambda i: (i, 0, 0)),
                  _const_spec(nrm.shape)],
        out_specs=pl.BlockSpec((kf, 2, FFT_R // 2, c), lambda i: (i, 0, 0, 0)),
        compiler_params=_cparams(("arbitrary",)),
        name="l0_hyena_kfilt",
    )(a, nmat, nrm)


def _hyena_long(u, x0, skip, hfilt, nrm, nb):
    L = FFT_R * FFT_R
    c = u.shape[-1]
    v4 = lambda a, n: a.reshape(n, FFT_R, FFT_R, c)
    ga, nmat, mmat, ma = _hy2_tables()
    khat = _hy_kfilt(_lead_in(ga, v4(hfilt, 2), HYENA_KF, "l0_hyena_fwd_a"), nmat, nrm)
    ee = _hy_mid(_lead_in(ga, v4(u, nb), HYENA_KF, "l0_hyena_fwd_a"), nmat, mmat, khat)
    y = _lead_out(ma, ee, (v4(u, nb), v4(x0, nb), skip), "l0_hyena_inv_a")
    return y.reshape(nb * L, c)


def _fn2_tables():
    L = FFT_R * FFT_R
    r = np.arange(FFT_R, dtype=np.int64)
    idx = (FFT_R * r[None, :, None] * r[None, None, :] + r[None, :, None] * r[:, None, None]) % L
    gm = 2.0 * np.pi * idx / L
    c, s = np.cos(gm), np.sin(gm)
    g1 = _interleave(np.concatenate([c, -s], axis=2), np.concatenate([-s, -c], axis=2), 1)
    dl = 2.0 * np.pi * ((r[:, None] * r[None, :]) % FFT_R) / FFT_R
    g2 = _interleave(np.cos(dl), np.sin(dl), 1) / math.sqrt(L * FN_GROUP_CH)
    bf = lambda a: jnp.asarray(a, F32).astype(BF16)
    return bf(g1), bf(g2)


def _fnet_s1_kernel(x_ref, m_ref, cs_ref, g1_ref, o_ref, zc_ref, zs_ref):
    xs = jnp.swapaxes(x_ref[...], 0, 1).reshape(FNET_KF * FFT_R, D_MODEL)
    m = m_ref[...]
    h = (_ln_plain(xs) * (1.0 + m[:, D_MODEL:2 * D_MODEL]) + m[:, 0:D_MODEL]).astype(BF16)
    cs = cs_ref[...]
    for g in range(D_MODEL // FN_GROUP_CH):
        a = g * FN_GROUP_CH
        z = jnp.dot(h[:, a:a + FN_GROUP_CH], cs, preferred_element_type=F32)
        zc_ref[:, a:a + FN_GROUP_CH] = z[:, :FN_GROUP_CH].astype(BF16)
        zs_ref[:, a:a + FN_GROUP_CH] = z[:, FN_GROUP_CH:].astype(BF16)
    ws = []
    for j in range(FNET_KF):
        r0 = j * FFT_R
        s = jnp.concatenate([zc_ref[r0:r0 + FFT_R, :], zs_ref[r0:r0 + FFT_R, :]], axis=0)
        ws.append(_pack_pairs(jnp.dot(g1_ref[j], s, preferred_element_type=F32)))
    o_ref[...] = _from_blocks(ws)


def _fnet_long(x, mods, mod_base, nb):
    L = FFT_R * FFT_R
    d = D_MODEL
    g1, g2 = _fn2_tables()
    cs = _group_dft_table()
    kf = FNET_KF
    bb = pl.pallas_call(
        _fnet_s1_kernel,
        out_shape=jax.ShapeDtypeStruct((nb, FFT_R // kf, kf, FFT_R, d), jnp.uint32),
        grid=(nb, FFT_R // kf),
        in_specs=[pl.BlockSpec((None, FFT_R, kf, d), lambda b, k: (b, 0, k, 0)),
                  pl.BlockSpec((None, 1, 6 * d), lambda b, k: (mod_base + b, 0, 0)),
                  pl.BlockSpec(cs.shape, lambda b, k: (0, 0)),
                  pl.BlockSpec((kf, 2 * FFT_R, 2 * FFT_R), lambda b, k: (k, 0, 0))],
        out_specs=pl.BlockSpec((None, FFT_R // kf, kf, kf, d), lambda b, k: (b, 0, 0, k, 0)),
        scratch_shapes=[pltpu.VMEM((kf * FFT_R, d), BF16), pltpu.VMEM((kf * FFT_R, d), BF16)],
        compiler_params=_cparams(("arbitrary", "arbitrary")),
        name="l1_fnet_stage1",
    )(x.reshape(nb, FFT_R, FFT_R, d), mods, cs, g1)
    y = _lead_out(g2, bb, [], "l1_fnet_stage2")
    return y.reshape(nb * L, d)


def _group_dft_table():
    g = FN_GROUP_CH
    jk = (np.arange(g, dtype=np.int64)[:, None] * np.arange(g, dtype=np.int64)[None, :]) % g
    ang = 2.0 * np.pi * jk / g
    return jnp.asarray(np.concatenate([np.cos(ang), np.sin(ang)], axis=1), F32).astype(BF16)


def _fnet_front_kernel(x_ref, m_ref, cs_ref, zc_ref, zs_ref):
    m = m_ref[...]
    h = (_ln_plain(x_ref[...]) * (1.0 + m[:, D_MODEL:2 * D_MODEL]) + m[:, 0:D_MODEL]).astype(BF16)
    cs = cs_ref[...]
    for g in range(D_MODEL // FN_GROUP_CH):
        a = g * FN_GROUP_CH
        z = jnp.dot(h[:, a:a + FN_GROUP_CH], cs, preferred_element_type=F32)
        zc_ref[:, a:a + FN_GROUP_CH] = z[:, :FN_GROUP_CH].astype(BF16)
        zs_ref[:, a:a + FN_GROUP_CH] = z[:, FN_GROUP_CH:].astype(BF16)


def _fnet_front(x, mods, mod_base, tiles_per_mod):
    t = x.shape[0]
    tm = 2 * ROW_TILE
    tiles_per_mod = max(tiles_per_mod // 2, 1)
    cs = _group_dft_table()
    return pl.pallas_call(
        _fnet_front_kernel,
        out_shape=(jax.ShapeDtypeStruct((t, D_MODEL), BF16),) * 2,
        grid=(t // tm,),
        in_specs=[pl.BlockSpec((tm, D_MODEL), lambda i: (i, 0)),
                  _mod_spec(mod_base, tiles_per_mod),
                  _const_spec(cs.shape)],
        out_specs=(pl.BlockSpec((tm, D_MODEL), lambda i: (i, 0)),) * 2,
        compiler_params=_cparams(("arbitrary",)),
        name="l1_fnet_front",
    )(x, mods, cs)


def _post_kernel(n_a, *refs):
    x_ref, m_ref = refs[0], refs[1]
    a_refs = refs[2:2 + n_a]
    wo_refs = refs[2 + n_a:2 + 2 * n_a]
    g1_ref, b1_ref, w1_ref, w2_ref, g2_ref, b2_ref, o_ref = refs[2 + 2 * n_a:]
    m = m_ref[...]
    d = D_MODEL
    tm = x_ref.shape[0]
    halves = [(r, r + tm // POST_SPLIT) for r in range(0, tm, tm // POST_SPLIT)]
    outs = []
    for r0, r1 in halves:
        out = _bdot(a_refs[0][r0:r1, :], wo_refs[0][...])
        for a_ref, wo_ref in zip(a_refs[1:], wo_refs[1:]):
            out += _bdot(a_ref[r0:r1, :], wo_ref[...])
        outs.append(out)
    x1s, hs = [], []
    for (r0, r1), out in zip(halves, outs):
        x1 = _ln_plain(ALPHA * x_ref[r0:r1, :] + m[:, 2 * d:3 * d] * out) * g1_ref[...] + b1_ref[...]
        x1s.append(x1)
        hs.append((_ln_plain(x1) * (1.0 + m[:, 4 * d:5 * d]) + m[:, 3 * d:4 * d]).astype(BF16))
    accs = []
    n_c = D_FF // d

    def up(h, c):
        hc = jnp.maximum(jnp.dot(h, w1_ref[:, c * d:(c + 1) * d], preferred_element_type=F32), 0.0)
        return (hc * hc).astype(BF16)

    for h in hs:
        acc = None
        nxt = up(h, 0)
        for c in range(n_c):
            cur = nxt
            if c + 1 < n_c:
                nxt = up(h, c + 1)
            part = jnp.dot(cur, w2_ref[c * d:(c + 1) * d, :], preferred_element_type=F32)
            acc = part if acc is None else acc + part
        accs.append(acc)
    for (r0, r1), x1, acc in zip(halves, x1s, accs):
        o_ref[r0:r1, :] = _ln_plain(ALPHA * x1 + m[:, 5 * d:6 * d] * acc) * g2_ref[...] + b2_ref[...]


def _post(x, mods, mod_base, tiles_per_mod, a_list, wo_list, g1, b1, w1, w2, g2, b2):
    t = x.shape[0]
    tm = POST_TILE
    row = lambda c: pl.BlockSpec((tm, c), lambda i: (i, 0))
    once = lambda v: pl.BlockSpec(v.shape, lambda i: (0,) * v.ndim, pipeline_mode=pl.Buffered(1))
    in_specs = ([row(D_MODEL), _mod_spec(mod_base, tiles_per_mod * ROW_TILE // tm)]
                + [row(a.shape[1]) for a in a_list]
                + [once(w) for w in wo_list]
                + [once(v) for v in (g1, b1, w1, w2, g2, b2)])
    return pl.pallas_call(
        functools.partial(_post_kernel, len(a_list)),
        out_shape=jax.ShapeDtypeStruct((t, D_MODEL), F32),
        grid=(t // tm,),
        in_specs=in_specs,
        out_specs=row(D_MODEL),
        compiler_params=_cparams(("arbitrary",)),
        name="post_mlp",
    )(x, mods, *a_list, *wo_list, g1, b1, w1, w2, g2, b2)


def _rot_cols(w):
    parts = []
    for seg in range(2):
        o = seg * 32
        parts += [-w[:, o + 16:o + 32], w[:, o:o + 16]]
    return jnp.concatenate(parts, axis=1)


def _pad_cols(w, n):
    return jnp.pad(w, ((0, 0), (0, n - w.shape[1])))


def _block_diag2(w):
    z = jnp.zeros_like(w)
    return jnp.concatenate([jnp.concatenate([w, z], axis=1), jnp.concatenate([z, w], axis=1)], axis=0)


def _rope_tables(L):
    rows = L // GRID_W
    row = np.repeat(np.arange(rows, dtype=np.float64), GRID_W)
    col = np.tile(np.arange(GRID_W, dtype=np.float64), rows)
    half = QK_ROPE // 2
    inv = 1.0 / (ROPE_THETA ** (np.arange(0, half, 2, dtype=np.float64) / half))
    ar = row[:, None] * inv[None, :]
    ac = col[:, None] * inv[None, :]
    ang = np.concatenate([ar, ar, ac, ac], axis=1)
    cos = np.concatenate([np.cos(ang), np.ones_like(ang)], axis=1)
    sin = np.concatenate([np.sin(ang), np.zeros_like(ang)], axis=1)
    return jnp.asarray(cos, F32), jnp.asarray(sin, F32)


def kernel(x_prompt, x_sample, cache_l0_ckv, cache_l0_krope, c, c_ctx, l0_ada_w, l0_ada_b, l0_w_in, l0_conv_w, l0_conv_b, l0_hf_w1, l0_hf_b1, l0_hf_freq, l0_hf_w2, l0_hf_b2, l0_hf_w3, l0_hf_skip, l0_q_norm, l0_q_up, l0_kv_norm, l0_kv_up, l0_w_out, l0_ln1_g, l0_ln1_b, l0_mlp_w1, l0_mlp_w2, l0_ln2_g, l0_ln2_b, l1_ada_w, l1_ada_b, l1_w_out, l1_ln1_g, l1_ln1_b, l1_mlp_w1, l1_mlp_w2, l1_ln2_g, l1_ln2_b):
    nbc, lc, d = x_prompt.shape
    nbs, ls, _ = x_sample.shape
    past = cache_l0_ckv.shape[1]
    tm = ROW_TILE
    row1 = lambda v: v.reshape(1, -1)

    cond8 = jnp.concatenate([c_ctx[None, :], c, jnp.zeros((8 - 1 - nbs, d), F32)], axis=0)
    mods0 = _modulation(cond8, l0_ada_w, l0_ada_b)
    mods1 = _modulation(cond8, l1_ada_w, l1_ada_b)

    kpe_w = l0_w_in[:, 1920:1984]
    win = jnp.concatenate([l0_w_in[:, :1920], _pad_cols(kpe_w, LANE), _pad_cols(_rot_cols(kpe_w), LANE)],
                          axis=1).astype(BF16)
    dh = QK_NOPE + QK_ROPE
    q_nope = [l0_q_up[:, h * dh:h * dh + QK_NOPE] for h in range(MLA_HEADS)]
    q_pe = [l0_q_up[:, h * dh + QK_NOPE:(h + 1) * dh] for h in range(MLA_HEADS)]
    qup = jnp.concatenate(q_nope + [_pad_cols(w, LANE) for w in q_pe]
                          + [_pad_cols(_rot_cols(w), LANE) for w in q_pe], axis=1).astype(BF16)
    kvup = l0_kv_up.astype(BF16)
    front_w = (win, row1(l0_q_norm), qup, row1(l0_kv_norm), kvup, l0_conv_w, row1(l0_conv_b))
    skip = row1(l0_hf_skip)
    w1p = jnp.pad(l0_hf_w1, ((0, LANE - l0_hf_w1.shape[0]), (0, 0)))
    two = lambda v: jnp.tile(row1(v), (1, 2))
    filt_w = (_block_diag2(w1p), two(l0_hf_b1), two(l0_hf_freq), _block_diag2(l0_hf_w2), two(l0_hf_b2),
              _block_diag2(l0_hf_w3).astype(BF16))
    wo0 = l0_w_out.astype(BF16)

    xc = x_prompt.reshape(nbc * lc, d)
    xs = x_sample.reshape(nbs * ls, d)
    groups = (
        dict(x=xc, nb=nbc, L=lc, mod_base=0, tiles_per_mod=nbc * lc // tm, dft_nb=8, tq=lc, hps=MLA_HEADS),
        dict(x=xs, nb=nbs, L=ls, mod_base=1, tiles_per_mod=ls // tm, dft_nb=nbs, tq=512, hps=1),
    )
    ones_tab = (jnp.ones((FRONT_TILE, LANE), F32), jnp.zeros((FRONT_TILE, LANE), F32))

    outs = []
    ctx_ckv = ctx_krope = None
    for gi, g in enumerate(groups):
        nb, L = g["nb"], g["L"]
        latent = gi == 1
        cos, sin = _rope_tables(L) if latent else ones_tab
        two_stage = L == FFT_R * FFT_R
        u, x0, q, k, vt, kvn, kpe = _front(g["x"], mods0, g["mod_base"], g["tiles_per_mod"] * tm, front_w,
                                           cos, sin, latent, L, min(L, FRONT_TILE))
        if latent:
            extra = _cache_kv(cache_l0_ckv.reshape(nbs * past, KV_LORA),
                              _pad_cols(cache_l0_krope.reshape(nbs * past, QK_ROPE), LANE), kvup)
        else:
            extra = None
            ctx_ckv = kvn.reshape(nb, L, KV_LORA)
            ctx_krope = kpe.reshape(nb, L, QK_ROPE)
        y_mla = _attention(q, k, vt, extra, nb, L, g["tq"], g["hps"])

        hfilt, hnorm = _filters(L, *filt_w)
        if two_stage:
            y_hy = _hyena_long(u, x0, skip, hfilt, hnorm, nb)
        else:
            kre, kim = _dft("filt", "hy_fwd", [hfilt], [hnorm], 2)
            sh = (nb, L, HY_CH)
            yre, yim = _dft("fwdk", "hy_fwd", [u.reshape(sh)], [kre, kim], g["dft_nb"])
            y_hy = _dft("inv", "hy_inv", [yre, yim], [u.reshape(sh), x0.reshape(sh), skip], g["dft_nb"])
            y_hy = y_hy.reshape(nb * L, HY_CH)

        x1 = _post(g["x"], mods0, g["mod_base"], g["tiles_per_mod"], [y_hy, y_mla],
                   [wo0[:HY_CH], wo0[HY_CH:]], row1(l0_ln1_g), row1(l0_ln1_b),
                   l0_mlp_w1.astype(BF16), l0_mlp_w2.astype(BF16), row1(l0_ln2_g), row1(l0_ln2_b))

        if L == FFT_R * FFT_R:
            yf = _fnet_long(x1, mods1, g["mod_base"], nb)
        else:
            zc, zs = _fnet_front(x1, mods1, g["mod_base"], g["tiles_per_mod"])
            sh = (nb, L, d)
            yf = _dft("fnet", "fnet", [zc.reshape(sh), zs.reshape(sh)], [], g["dft_nb"] // 2)
            yf = yf.reshape(nb * L, d)
        x2 = _post(x1, mods1, g["mod_base"], g["tiles_per_mod"], [yf],
                   [l1_w_out.astype(BF16)], row1(l1_ln1_g), row1(l1_ln1_b),
                   l1_mlp_w1.astype(BF16), l1_mlp_w2.astype(BF16), row1(l1_ln2_g), row1(l1_ln2_b))
        outs.append(x2.reshape(nb, L, d))

    return (outs[0], outs[1], ctx_ckv, ctx_krope)
```

```python
import functools
import math

import numpy as np
import jax
import jax.numpy as jnp
from jax import lax
from jax.experimental import pallas as pl
from jax.experimental.pallas import tpu as pltpu

F32 = jnp.float32
BF16 = jnp.bfloat16
HI = lax.Precision.HIGHEST

D_MODEL = 1024
DEPTH = 2
GRID_W = 64
HY_CH = 512
FILT_BANDS = 16
FILT_ORDER = 64
FAST_DECAY_PCT = 0.3
SLOW_DECAY_PCT = 1.5
DECAY_TARGET = 1e-2
MAX_DECAY = math.log(DECAY_TARGET) / FAST_DECAY_PCT
MIN_DECAY = math.log(DECAY_TARGET) / SLOW_DECAY_PCT
MLA_HEADS = 4
QK_NOPE = 128
QK_ROPE = 64
V_HEAD = 128
Q_LORA = 256
KV_LORA = 128
ROPE_THETA = 10000.0
FN_GROUP_CH = 128
D_FF = 4096
ALPHA = (2 * DEPTH) ** 0.25
LN_EPS = 1e-5
RMS_EPS = 1e-6

LANE = 128
ROW_TILE = 256
FRONT_TILE = 512
POST_TILE = 512
POST_SPLIT = 2
QK_PAD = 256
VT_ROWS = V_HEAD + 16
LOG2E = 1.4426950408889634
VMEM_LIMIT = 56 * 1024 * 1024


def _cparams(sem):
    return pltpu.CompilerParams(dimension_semantics=sem, vmem_limit_bytes=VMEM_LIMIT)


def _ln_plain(x):
    mu = jnp.mean(x, axis=-1, keepdims=True)
    xc = x - mu
    var = jnp.mean(xc * xc, axis=-1, keepdims=True)
    return xc * lax.rsqrt(var + LN_EPS)


def _rms(x, g):
    return x * lax.rsqrt(jnp.mean(x * x, axis=-1, keepdims=True) + RMS_EPS) * g


def _bdot(a, b):
    return jnp.dot(a.astype(BF16), b, preferred_element_type=F32)


def _vt_rows(v):
    ones = jnp.ones((VT_ROWS - V_HEAD, v.shape[0]), BF16)
    return jnp.concatenate([jnp.transpose(v).astype(BF16), ones], axis=0)


def _mod_kernel(c_ref, w_ref, b_ref, o_ref):
    c = c_ref[...]
    s = c / (1.0 + jnp.exp(-c))
    s_hi = s.astype(BF16)
    s_lo = (s - s_hi.astype(F32)).astype(BF16)
    s2 = jnp.concatenate([s_hi, s_lo], axis=0)
    nr = s.shape[0]
    w = w_ref[...]
    w_hi = w.astype(BF16)
    w_lo = (w - w_hi.astype(F32)).astype(BF16)
    r1 = jnp.dot(s2, w_hi, preferred_element_type=F32)
    r2 = jnp.dot(s_hi, w_lo, preferred_element_type=F32)
    o_ref[...] = r1[:nr] + r1[nr:] + r2 + b_ref[...]


def _modulation(cond8, w, b):
    n = w.shape[1]
    tn = 768
    out = pl.pallas_call(
        _mod_kernel,
        out_shape=jax.ShapeDtypeStruct((8, n), F32),
        grid=(n // tn,),
        in_specs=[pl.BlockSpec((8, D_MODEL), lambda j: (0, 0)),
                  pl.BlockSpec((D_MODEL, tn), lambda j: (0, j)),
                  pl.BlockSpec((1, tn), lambda j: (0, j))],
        out_specs=pl.BlockSpec((8, tn), lambda j: (0, j)),
        compiler_params=_cparams(("arbitrary",)),
        name="modulation",
    )(cond8, w, b.reshape(1, n))
    return out.reshape(8, 1, n)


def _mod_spec(mod_base, tiles_per_mod):
    return pl.BlockSpec((None, 1, 6 * D_MODEL), lambda i: (mod_base + i // tiles_per_mod, 0, 0))


def _const_spec(shape):
    nd = len(shape)
    return pl.BlockSpec(shape, lambda i: (0,) * nd)


HALO = 8


def _front_kernel(tiles_per_seq, x_ref, xp_ref, xn_ref, m_ref, win_ref, qn_ref, qup_ref, kvn_ref, kvup_ref,
                  cos_ref, sin_ref, cw_ref, cb_ref,
                  u_ref, x0_ref, q_ref, k_ref, v_ref, kvn_out_ref, kpe_ref):
    i = pl.program_id(0)
    m = m_ref[...]
    tm = x_ref.shape[0]
    nh = 3 * HY_CH
    xe = jnp.concatenate([xp_ref[...], x_ref[...], xn_ref[...]], axis=0)
    he = _ln_plain(xe) * (1.0 + m[:, D_MODEL:2 * D_MODEL]) + m[:, 0:D_MODEL]

    z = _bdot(he[HALO:HALO + tm], win_ref[:, nh:])
    zh = _bdot(he, win_ref[:, :nh])
    q_c = z[:, 0:256]
    kv_c = z[:, 256:384]
    cos = cos_ref[...]
    sin = sin_ref[...]
    kpe = z[:, 384:512] * cos + z[:, 512:640] * sin
    kpe_ref[...] = kpe[:, :QK_ROPE]
    kpe_b = kpe.astype(BF16)
    q = _bdot(_rms(q_c, qn_ref[...]), qup_ref[...]) * (LOG2E / math.sqrt(QK_NOPE + QK_ROPE))
    kvn = _rms(kv_c, kvn_ref[...])
    kvn_out_ref[...] = kvn
    kv = _bdot(kvn, kvup_ref[...])

    pos = i % tiles_per_seq
    rows = lax.broadcasted_iota(jnp.int32, (tm + 2 * HALO, 1), 0)
    inside = jnp.logical_and(jnp.logical_or(rows >= HALO, pos != 0),
                             jnp.logical_or(rows < tm + HALO, pos != tiles_per_seq - 1))
    zh = jnp.where(inside, zh, 0.0)
    cw = cw_ref[...]
    pz = (pltpu.roll(zh, 1, 0) * cw[0:1, :] + zh * cw[1:2, :]
          + pltpu.roll(zh, tm + 2 * HALO - 1, 0) * cw[2:3, :])[HALO:HALO + tm] + cb_ref[...]
    u_ref[...] = (pz[:, 2 * HY_CH:] * pz[:, HY_CH:2 * HY_CH]).astype(u_ref.dtype)
    x0_ref[...] = pz[:, :HY_CH].astype(x0_ref.dtype)

    for hd in range(MLA_HEADS):
        a = hd * LANE
        q_pe = (q[:, 512 + a:512 + a + LANE] * cos + q[:, 1024 + a:1024 + a + LANE] * sin).astype(BF16)
        q_ref[hd] = jnp.concatenate([q[:, a:a + LANE].astype(BF16), q_pe], axis=-1)
        k_ref[hd] = jnp.concatenate([kv[:, 2 * a:2 * a + LANE].astype(BF16), kpe_b], axis=-1)
        v_ref[hd] = _vt_rows(kv[:, 2 * a + LANE:2 * a + 2 * LANE])


def _front(x, mods, mod_base, rows_per_mod, w, cos, sin, rope, seq_len, tm):
    t = x.shape[0]
    tiles_per_seq = seq_len // tm
    tiles_per_mod = rows_per_mod // tm
    win, qn, qup, kvn, kvup, conv_w, conv_b = w
    if rope:
        tab_spec = pl.BlockSpec((tm, LANE), lambda i: (i % tiles_per_seq, 0))
    else:
        tab_spec = pl.BlockSpec((tm, LANE), lambda i: (0, 0))
    r8 = tm // HALO
    n8 = t // HALO
    hy_out = lambda dt: jax.ShapeDtypeStruct((t, HY_CH), dt)
    hy_spec = pl.BlockSpec((tm, HY_CH), lambda i: (i, 0))
    return pl.pallas_call(
        functools.partial(_front_kernel, tiles_per_seq),
        out_shape=(hy_out(BF16), hy_out(BF16),
                   jax.ShapeDtypeStruct((MLA_HEADS, t, QK_PAD), BF16),
                   jax.ShapeDtypeStruct((MLA_HEADS, t, QK_PAD), BF16),
                   jax.ShapeDtypeStruct((MLA_HEADS, VT_ROWS, t), BF16),
                   jax.ShapeDtypeStruct((t, KV_LORA), F32),
                   jax.ShapeDtypeStruct((t, QK_ROPE), F32)),
        grid=(t // tm,),
        in_specs=[pl.BlockSpec((tm, D_MODEL), lambda i: (i, 0)),
                  pl.BlockSpec((HALO, D_MODEL), lambda i: (jnp.maximum(i * r8 - 1, 0), 0)),
                  pl.BlockSpec((HALO, D_MODEL), lambda i: (jnp.minimum((i + 1) * r8, n8 - 1), 0)),
                  _mod_spec(mod_base, tiles_per_mod),
                  _const_spec(win.shape), _const_spec(qn.shape), _const_spec(qup.shape),
                  _const_spec(kvn.shape), _const_spec(kvup.shape),
                  tab_spec, tab_spec,
                  _const_spec(conv_w.shape), _const_spec(conv_b.shape)],
        out_specs=(hy_spec, hy_spec,
                   pl.BlockSpec((MLA_HEADS, tm, QK_PAD), lambda i: (0, i, 0)),
                   pl.BlockSpec((MLA_HEADS, tm, QK_PAD), lambda i: (0, i, 0)),
                   pl.BlockSpec((MLA_HEADS, VT_ROWS, tm), lambda i: (0, 0, i)),
                   pl.BlockSpec((tm, KV_LORA), lambda i: (i, 0)),
                   pl.BlockSpec((tm, QK_ROPE), lambda i: (i, 0))),
        compiler_params=_cparams(("arbitrary",)),
        name="l0_front",
    )(x, x, x, mods, win, qn, qup, kvn, kvup, cos, sin, conv_w, conv_b)


def _cache_kv_kernel(ckv_ref, kr_ref, kvup_ref, k_ref, v_ref):
    kv = _bdot(ckv_ref[...], kvup_ref[...])
    kr = kr_ref[...].astype(BF16)
    for hd in range(MLA_HEADS):
        a = 2 * hd * LANE
        k_ref[hd] = jnp.concatenate([kv[:, a:a + LANE].astype(BF16), kr], axis=-1)
        v_ref[hd] = _vt_rows(kv[:, a + LANE:a + 2 * LANE])


def _cache_kv(ckv, krope_pad, kvup):
    t = ckv.shape[0]
    return pl.pallas_call(
        _cache_kv_kernel,
        out_shape=(jax.ShapeDtypeStruct((MLA_HEADS, t, QK_PAD), BF16),
                   jax.ShapeDtypeStruct((MLA_HEADS, VT_ROWS, t), BF16)),
        name="l0_cache_kv",
    )(ckv, krope_pad, kvup)


def _col_reduce(x, op):
    rows, n = x.shape
    for g in (32, 8):
        if rows % (8 * g) == 0 and rows > 8 * g:
            x = op(x.reshape(rows // (8 * g), 8 * g, n), axis=0)
            rows = 8 * g
    return op(x, axis=0, keepdims=True)


PV_CHUNK = 128


def _attn_kernel(n_kv, hps, q_ref, *refs):
    k_refs, vt_refs = refs[:n_kv], refs[n_kv:2 * n_kv]
    o_ref, s_even, s_odd, m_even, m_odd = refs[2 * n_kv:]
    i = pl.program_id(0)

    @pl.when(i == 0)
    def _():
        s_odd[...] = jnp.zeros_like(s_odd)
        m_odd[...] = jnp.zeros_like(m_odd)

    def step(s_write, m_write, s_read, m_read):
        nt = (((1,), (1,)), ((), ()))
        for h in range(hps):
            q = q_ref[h]
            r0 = 0
            m = None
            for k_ref in k_refs:
                lk = k_ref.shape[1]
                sblk = lax.dot_general(k_ref[h], q, nt, preferred_element_type=F32)
                s_write[h, r0:r0 + lk, :] = sblk
                mc = _col_reduce(sblk, jnp.max)
                m = mc if m is None else jnp.maximum(m, mc)
                r0 += lk
            m_write[h] = m
        for h in range(hps):
            m = m_read[h]
            acc = None
            r0 = 0
            for vt_ref in vt_refs:
                lk = vt_ref.shape[2]
                for c0 in range(0, lk, PV_CHUNK):
                    c1 = min(c0 + PV_CHUNK, lk)
                    pb = jnp.exp2(s_read[h, r0 + c0:r0 + c1, :] - m).astype(BF16)
                    pv = jnp.dot(vt_ref[h, :, c0:c1], pb, preferred_element_type=F32)
                    acc = pv if acc is None else acc + pv
                r0 += lk
            o_ref[:, h * V_HEAD:(h + 1) * V_HEAD] = jnp.transpose(
                acc[:V_HEAD] / acc[V_HEAD:V_HEAD + 1]).astype(o_ref.dtype)

    pl.when(i % 2 == 0)(lambda: step(s_even, m_even, s_odd, m_odd))
    pl.when(i % 2 == 1)(lambda: step(s_odd, m_odd, s_even, m_even))


def _attention(q, k, vt, extra, nb, lq, tq, hps):
    nq = lq // tq
    ng = MLA_HEADS // hps
    n_tiles = nb * ng * nq

    def where(t):
        bh = t // nq
        return bh // ng, bh % ng, t % nq

    def score_side(fn):
        return lambda i: fn(*where(jnp.minimum(i, n_tiles - 1)))

    def value_side(fn):
        return lambda i: fn(*where(jnp.maximum(i - 1, 0)))

    ks, vts = [k], [vt]
    if extra is not None:
        ks.append(extra[0])
        vts.append(extra[1])
    in_specs = [pl.BlockSpec((hps, tq, QK_PAD), score_side(lambda b, h, j: (h, b * nq + j, 0)))]
    in_specs += [pl.BlockSpec((hps, a.shape[1] // nb, QK_PAD), score_side(lambda b, h, j: (h, b, 0))) for a in ks]
    in_specs += [pl.BlockSpec((hps, VT_ROWS, a.shape[2] // nb), value_side(lambda b, h, j: (h, 0, b))) for a in vts]
    lk_total = sum(a.shape[1] // nb for a in ks)
    return pl.pallas_call(
        functools.partial(_attn_kernel, len(ks), hps),
        out_shape=jax.ShapeDtypeStruct((nb * lq, MLA_HEADS * V_HEAD), BF16),
        grid=(n_tiles + 1,),
        in_specs=in_specs,
        out_specs=pl.BlockSpec((tq, hps * V_HEAD), value_side(lambda b, h, j: (b * nq + j, h))),
        scratch_shapes=[pltpu.VMEM((hps, lk_total, tq), F32), pltpu.VMEM((hps, lk_total, tq), F32),
                        pltpu.VMEM((hps, 1, tq), F32), pltpu.VMEM((hps, 1, tq), F32)],
        compiler_params=_cparams(("arbitrary",)),
        name="l0_attention",
    )(q, *ks, *vts)


def _filter_kernel(z_ref, w1_ref, b1_ref, fr_ref, w2_ref, b2_ref, w3_ref, dl_ref, h_ref, norm_ref):
    i = pl.program_id(0)
    z = z_ref[...]
    tl = z.shape[0]
    fr = fr_ref[...]
    z2 = jnp.concatenate([z[:tl // 2], z[tl // 2:]], axis=1)
    h = jnp.sin(fr * (jnp.dot(z2, w1_ref[...], precision=HI, preferred_element_type=F32) + b1_ref[...]))
    h = jnp.sin(fr * (jnp.dot(h, w2_ref[...], precision=HI, preferred_element_type=F32) + b2_ref[...]))
    h = _bdot(h, w3_ref[...])
    h = jnp.concatenate([h[:, :2 * HY_CH], h[:, 2 * HY_CH:]], axis=0)
    decay = jnp.exp(-(z[:, 0:1] * dl_ref[...]))
    hf = h[:, :HY_CH] * decay
    hb = h[:, HY_CH:] * decay
    part = jnp.sum(jnp.abs(hf) + jnp.abs(hb), axis=0, keepdims=True)

    @pl.when(i == 0)
    def _():
        norm_ref[...] = part

    @pl.when(i > 0)
    def _():
        norm_ref[...] += part

    rows = lax.broadcasted_iota(jnp.int32, hb.shape, 0) + i * tl
    h_ref[0] = hf.astype(h_ref.dtype)
    h_ref[1] = jnp.where(rows == 0, 0.0, hb).astype(h_ref.dtype)


def _filter_embedding(L):
    t = np.linspace(0.0, 1.0, L)[:, None]
    w_ang = 2.0 * np.pi * np.arange(L) / L
    bands = np.linspace(1e-4, FILT_BANDS - 1, FILT_BANDS)
    ang = w_ang[:, None] * bands[None, :]
    z = np.zeros((L, LANE), np.float64)
    z[:, 0:1] = t
    z[:, 1:1 + FILT_BANDS] = np.cos(ang)
    z[:, 1 + FILT_BANDS:1 + 2 * FILT_BANDS] = -np.sin(ang)
    return jnp.asarray(z, F32)


def _filters(L, w1p, b1, fr, w2, b2, w3):
    tl = min(L, 1024)
    z = _filter_embedding(L)
    deltas = jnp.asarray(np.abs(np.linspace(MIN_DECAY, MAX_DECAY, HY_CH))[None, :], F32)
    return pl.pallas_call(
        _filter_kernel,
        out_shape=(jax.ShapeDtypeStruct((2, L, HY_CH), BF16), jax.ShapeDtypeStruct((1, HY_CH), F32)),
        grid=(L // tl,),
        in_specs=[pl.BlockSpec((tl, LANE), lambda i: (i, 0)),
                  _const_spec(w1p.shape), _const_spec(b1.shape), _const_spec(fr.shape),
                  _const_spec(w2.shape), _const_spec(b2.shape), _const_spec(w3.shape),
                  _const_spec(deltas.shape)],
        out_specs=(pl.BlockSpec((2, tl, HY_CH), lambda i: (0, i, 0)),
                   pl.BlockSpec((1, HY_CH), lambda i: (0, 0))),
        compiler_params=_cparams(("arbitrary",)),
        name="l0_hyena_filters",
    )(z, w1p, b1, fr, w2, b2, w3, deltas)


def _dft_tables(kind, L, ti):
    ni = L // ti
    i = np.arange(ti, dtype=np.int64)[:, None]
    big = (np.arange(ni, dtype=np.int64) * ti)[:, None]
    c = np.arange(L, dtype=np.int64)[None, :]
    if kind == "hy_fwd":
        period = 4 * L
        base_idx = (2 * i + 1) * c
        r_idx = 2 * big * c
        scale = 1.0
    elif kind == "hy_inv":
        period = 4 * L
        base_idx = (2 * c + 1) * i
        r_idx = (2 * c + 1) * big
        scale = 1.0 / L
    else:
        period = L
        base_idx = i * c
        r_idx = big * c
        scale = 1.0 / math.sqrt(L * FN_GROUP_CH)
    ab = 2.0 * np.pi * (base_idx % period) / period
    ar = 2.0 * np.pi * (r_idx % period) / period
    return (jnp.asarray(np.cos(ab), F32), jnp.asarray(np.sin(ab), F32),
            jnp.asarray(scale * np.cos(ar), F32).reshape(ni, 1, L),
            jnp.asarray(scale * np.sin(ar), F32).reshape(ni, 1, L))


def _dft_kernel(mode, nb, n_x, *refs):
    bc_ref, bs_ref, rc_ref, rs_ref = refs[:4]
    x_refs = refs[4:4 + n_x]
    rest = refs[4 + n_x:]
    p_ref, q_ref = rest[-2], rest[-1]
    j = pl.program_id(2)
    nj = pl.num_programs(2)
    tj = x_refs[0].shape[1]
    if bc_ref.shape[1] == tj:
        bc, bs, rc, rs = bc_ref[...], bs_ref[...], rc_ref[...], rs_ref[...]
    else:
        off = pl.multiple_of(j * tj, tj)
        bc, bs = bc_ref[:, pl.ds(off, tj)], bs_ref[:, pl.ds(off, tj)]
        rc, rs = rc_ref[:, pl.ds(off, tj)], rs_ref[:, pl.ds(off, tj)]
    tc = (bc * rc - bs * rs).astype(BF16)
    ts = (bs * rc + bc * rs).astype(BF16)
    x1_ref = x_refs[0]
    x2_ref = x_refs[-1]

    pq = [(jnp.dot(tc, x1_ref[b].astype(BF16), preferred_element_type=F32),
           jnp.dot(ts, x2_ref[b].astype(BF16), preferred_element_type=F32)) for b in range(nb)]

    @pl.when(j == 0)
    def _():
        for b in range(nb):
            p_ref[b] = pq[b][0]
            q_ref[b] = pq[b][1]

    @pl.when(j > 0)
    def _():
        for b in range(nb):
            p_ref[b] += pq[b][0]
            q_ref[b] += pq[b][1]

    @pl.when(j == nj - 1)
    def _():
        if mode == "filt":
            nrm = rest[0][...]
            kre_ref, kim_ref = rest[1], rest[2]
            kre_ref[...] = (p_ref[0] + p_ref[1]) / nrm
            kim_ref[...] = (q_ref[1] - q_ref[0]) / nrm
        elif mode == "fwdk":
            kre, kim = rest[0][...], rest[1][...]
            yre_ref, yim_ref = rest[2], rest[3]
            for b in range(nb):
                pp, qq = p_ref[b], q_ref[b]
                yre_ref[b] = (pp * kre + qq * kim).astype(BF16)
                yim_ref[b] = (pp * kim - qq * kre).astype(BF16)
        elif mode == "inv":
            u_ref, x0_ref, skip_ref, o_ref = rest[0], rest[1], rest[2], rest[3]
            skip = skip_ref[...]
            for b in range(nb):
                o_ref[b] = ((p_ref[b] - q_ref[b] + u_ref[b] * skip) * x0_ref[b].astype(F32)).astype(BF16)
        else:
            o_ref = rest[0]
            for b in range(nb):
                o_ref[b] = (p_ref[b] - q_ref[b]).astype(BF16)


def _dft(mode, kind, xs, extras, nb):
    B, L, C = xs[0].shape
    ti = min(L, 256)
    tj = min(L, 512)
    bc, bs, rc, rs = _dft_tables(kind, L, ti)
    grid = (B // nb, L // ti, L // tj)
    x_spec = pl.BlockSpec((nb, tj, C), lambda g, i, j: (g, j, 0))
    row_spec = lambda c, dt=None: pl.BlockSpec((nb, ti, c), lambda g, i, j: (g, i, 0))
    in_specs = [pl.BlockSpec((ti, L), lambda g, i, j: (0, 0)),
                pl.BlockSpec((ti, L), lambda g, i, j: (0, 0)),
                pl.BlockSpec((None, 1, L), lambda g, i, j: (i, 0, 0)),
                pl.BlockSpec((None, 1, L), lambda g, i, j: (i, 0, 0))] + [x_spec] * len(xs)
    if mode == "filt":
        in_specs += [pl.BlockSpec((1, HY_CH), lambda g, i, j: (0, 0))]
        out_shape = (jax.ShapeDtypeStruct((L, HY_CH), F32),) * 2
        out_specs = (pl.BlockSpec((ti, HY_CH), lambda g, i, j: (i, 0)),) * 2
    elif mode == "fwdk":
        in_specs += [pl.BlockSpec((ti, HY_CH), lambda g, i, j: (i, 0))] * 2
        out_shape = (jax.ShapeDtypeStruct((B, L, C), BF16),) * 2
        out_specs = (row_spec(C),) * 2
    elif mode == "inv":
        in_specs += [row_spec(C)] * 2 + [pl.BlockSpec((1, C), lambda g, i, j: (0, 0))]
        out_shape = jax.ShapeDtypeStruct((B, L, C), BF16)
        out_specs = row_spec(C)
    else:
        out_shape = jax.ShapeDtypeStruct((B, L, C), BF16)
        out_specs = row_spec(C)
    return pl.pallas_call(
        functools.partial(_dft_kernel, mode, nb, len(xs)),
        out_shape=out_shape,
        grid=grid,
        in_specs=in_specs,
        out_specs=out_specs,
        scratch_shapes=[pltpu.VMEM((nb, ti, C), F32), pltpu.VMEM((nb, ti, C), F32)],
        compiler_params=_cparams(("arbitrary", "arbitrary", "arbitrary")),
        name="dft_" + mode,
    )(bc, bs, rc, rs, *xs, *extras)


FFT_R = 64
HYENA_KF = 16
FNET_KF = 16


def _pack_pairs(x):
    return pltpu.bitcast(x.astype(BF16), jnp.uint32)


def _unpack_pairs(w):
    return pltpu.bitcast(w, BF16)


def _to_blocks(w):
    return jnp.swapaxes(w.reshape(FFT_R, w.shape[-2], w.shape[-1]), 0, 1)


def _from_blocks(ws):
    kf, c = len(ws), ws[0].shape[-1]
    return jnp.swapaxes(jnp.stack(ws, axis=0), 0, 1).reshape(FFT_R // kf, kf, kf, c)


def _lead_in_kernel(g_ref, x_ref, o_ref):
    g = g_ref[...]
    xt = jnp.swapaxes(x_ref[...], 0, 1)
    for j in range(x_ref.shape[1]):
        o_ref[j] = _pack_pairs(jnp.dot(g, xt[j].astype(BF16), preferred_element_type=F32))


def _lead_in(g, x, kf, name):
    nbx, _, _, c = x.shape
    m2 = g.shape[0] // 2
    return pl.pallas_call(
        _lead_in_kernel,
        out_shape=jax.ShapeDtypeStruct((nbx, FFT_R // kf, kf, m2, c), jnp.uint32),
        grid=(nbx, FFT_R // kf),
        in_specs=[pl.BlockSpec(g.shape, lambda b, k: (0, 0)),
                  pl.BlockSpec((None, FFT_R, kf, c), lambda b, k: (b, 0, k, 0))],
        out_specs=pl.BlockSpec((None, None, kf, m2, c), lambda b, k: (b, k, 0, 0, 0)),
        compiler_params=_cparams(("arbitrary", "arbitrary")),
        name=name,
    )(g, x)


def _lead_out_kernel(n_extra, g_ref, w_ref, *rest):
    g = g_ref[...]
    o_ref = rest[-1]
    ys = [jnp.dot(g, _unpack_pairs(w_ref[j]), preferred_element_type=F32) for j in range(w_ref.shape[0])]
    y = jnp.swapaxes(jnp.stack(ys, axis=0), 0, 1)
    if n_extra:
        y = (y + rest[0][...] * rest[2][...]) * rest[1][...].astype(F32)
    o_ref[...] = y.astype(o_ref.dtype)


def _lead_out(g, w, epilogue, name):
    nb, nk, kf, k2, c = w.shape
    blk = pl.BlockSpec((None, FFT_R, kf, c), lambda b, k: (b, 0, k, 0))
    extra_specs = [blk, blk, pl.BlockSpec((1, c), lambda b, k: (0, 0))] if epilogue else []
    return pl.pallas_call(
        functools.partial(_lead_out_kernel, len(epilogue)),
        out_shape=jax.ShapeDtypeStruct((nb, FFT_R, FFT_R, c), BF16),
        grid=(nb, nk),
        in_specs=[pl.BlockSpec(g.shape, lambda b, k: (0, 0)),
                  pl.BlockSpec((None, None, kf, k2, c), lambda b, k: (b, k, 0, 0, 0))] + extra_specs,
        out_specs=blk,
        compiler_params=_cparams(("arbitrary", "arbitrary")),
        name=name,
    )(g, w, *epilogue)


def _interleave(a, b, axis):
    st = np.stack([a, b], axis=axis + 1)
    shape = list(a.shape)
    shape[axis] *= 2
    return st.reshape(shape)


def _hy2_tables():
    L = FFT_R * FFT_R
    n2 = 2 * L
    f1 = np.arange(2 * FFT_R, dtype=np.int64)
    s1 = np.arange(FFT_R, dtype=np.int64)
    th = np.pi * (((2 * f1[:, None] + 1) * s1[None, :]) % (4 * FFT_R)) / (2 * FFT_R)
    ga = _interleave(np.cos(th), -np.sin(th), 0)
    ma = _interleave(np.cos(th).T, -np.sin(th).T, 1) / L
    f2 = np.arange(FFT_R // 2, dtype=np.int64)
    s2 = np.arange(FFT_R, dtype=np.int64)
    idx = ((n2 // FFT_R) * 2 * f2[None, :, None] * s2[None, None, :]
           + (2 * f1[:, None, None] + 1) * s2[None, None, :]) % (2 * n2)
    al = np.pi * idx / n2
    c, s = np.cos(al), np.sin(al)
    nmat = np.concatenate([_interleave(c, s, 2), _interleave(-s, c, 2)], axis=1)
    ct, st = np.transpose(c, (0, 2, 1)), np.transpose(s, (0, 2, 1))
    mmat = _interleave(np.concatenate([ct, -st], axis=2), np.concatenate([st, ct], axis=2), 1)
    bf = lambda a: jnp.asarray(a, F32).astype(BF16)
    return bf(ga), bf(nmat), bf(mmat), bf(ma)


def _hy_mid_kernel(a_ref, n_ref, m_ref, k_ref, e_ref):
    half = FFT_R // 2
    kf = a_ref.shape[-2]
    a = _to_blocks(a_ref[...])
    ts = [jnp.dot(n_ref[j], _unpack_pairs(a[j]), preferred_element_type=F32)
          for j in range(kf)]
    ys = []
    for j in range(kf):
        tr, ti = ts[j][:half], ts[j][half:]
        kr, ki = k_ref[j, 0].astype(F32), k_ref[j, 1].astype(F32)
        ys.append(jnp.concatenate([tr * kr - ti * ki, tr * ki + ti * kr], axis=0).astype(BF16))
    e_ref[...] = _from_blocks([_pack_pairs(jnp.dot(m_ref[j], ys[j], preferred_element_type=F32))
                               for j in range(kf)])


def _hy_mid(a, nmat, mmat, khat):
    nb, nk, kf, _, c = a.shape
    nf1 = 2 * FFT_R
    blk = pl.BlockSpec((None, nk, kf, kf, c), lambda i, b: (b, 0, 0, i, 0))
    return pl.pallas_call(
        _hy_mid_kernel,
        out_shape=jax.ShapeDtypeStruct(a.shape, jnp.uint32),
        grid=(nf1 // kf, nb),
        in_specs=[blk,
                  pl.BlockSpec((kf, FFT_R, 2 * FFT_R), lambda i, b: (i, 0, 0)),
                  pl.BlockSpec((kf, 2 * FFT_R, FFT_R), lambda i, b: (i, 0, 0)),
                  pl.BlockSpec((kf, 2, FFT_R // 2, c), lambda i, b: (i, 0, 0, 0))],
        out_specs=blk,
        compiler_params=_cparams(("arbitrary", "arbitrary")),
        name="l0_hyena_mid",
    )(a, nmat, mmat, khat)


def _hy_kfilt_kernel(a_ref, n_ref, nrm_ref, k_ref):
    half = FFT_R // 2
    nrm = nrm_ref[...]
    af, ab = _to_blocks(a_ref[0]), _to_blocks(a_ref[1])
    for j in range(a_ref.shape[-2]):
        tf = jnp.dot(n_ref[j], _unpack_pairs(af[j]), preferred_element_type=F32)
        tb = jnp.dot(n_ref[j], _unpack_pairs(ab[j]), preferred_element_type=F32)
        k_ref[j, 0] = ((tf[:half] + tb[:half]) / nrm).astype(k_ref.dtype)
        k_ref[j, 1] = ((tf[half:] - tb[half:]) / nrm).astype(k_ref.dtype)


def _hy_kfilt(a, nmat, nrm):
    _, nk, kf, _, c = a.shape
    nf1 = 2 * FFT_R
    return pl.pallas_call(
        _hy_kfilt_kernel,
        out_shape=jax.ShapeDtypeStruct((nf1, 2, FFT_R // 2, c), BF16),
        grid=(nf1 // kf,),
        in_specs=[pl.BlockSpec((2, nk, kf, kf, c), lambda i: (0, 0, 0, i, 0)),
                  pl.BlockSpec((kf, FFT_R, 2 * FFT_R), lambda i: (i, 0, 0)),
                  _const_spec(nrm.shape)],
        out_specs=pl.BlockSpec((kf, 2, FFT_R // 2, c), lambda i: (i, 0, 0, 0)),
        compiler_params=_cparams(("arbitrary",)),
        name="l0_hyena_kfilt",
    )(a, nmat, nrm)


def _hyena_long(u, x0, skip, hfilt, nrm, nb):
    L = FFT_R * FFT_R
    c = u.shape[-1]
    v4 = lambda a, n: a.reshape(n, FFT_R, FFT_R, c)
    ga, nmat, mmat, ma = _hy2_tables()
    khat = _hy_kfilt(_lead_in(ga, v4(hfilt, 2), HYENA_KF, "l0_hyena_fwd_a"), nmat, nrm)
    ee = _hy_mid(_lead_in(ga, v4(u, nb), HYENA_KF, "l0_hyena_fwd_a"), nmat, mmat, khat)
    y = _lead_out(ma, ee, (v4(u, nb), v4(x0, nb), skip), "l0_hyena_inv_a")
    return y.reshape(nb * L, c)


def _fn2_tables():
    L = FFT_R * FFT_R
    r = np.arange(FFT_R, dtype=np.int64)
    idx = (FFT_R * r[None, :, None] * r[None, None, :] + r[None, :, None] * r[:, None, None]) % L
    gm = 2.0 * np.pi * idx / L
    c, s = np.cos(gm), np.sin(gm)
    g1 = _interleave(np.concatenate([c, -s], axis=2), np.concatenate([-s, -c], axis=2), 1)
    dl = 2.0 * np.pi * ((r[:, None] * r[None, :]) % FFT_R) / FFT_R
    g2 = _interleave(np.cos(dl), np.sin(dl), 1) / math.sqrt(L * FN_GROUP_CH)
    bf = lambda a: jnp.asarray(a, F32).astype(BF16)
    return bf(g1), bf(g2)


def _fnet_s1_kernel(x_ref, m_ref, cs_ref, g1_ref, o_ref, zc_ref, zs_ref):
    xs = jnp.swapaxes(x_ref[...], 0, 1).reshape(FNET_KF * FFT_R, D_MODEL)
    m = m_ref[...]
    h = (_ln_plain(xs) * (1.0 + m[:, D_MODEL:2 * D_MODEL]) + m[:, 0:D_MODEL]).astype(BF16)
    cs = cs_ref[...]
    for g in range(D_MODEL // FN_GROUP_CH):
        a = g * FN_GROUP_CH
        z = jnp.dot(h[:, a:a + FN_GROUP_CH], cs, preferred_element_type=F32)
        zc_ref[:, a:a + FN_GROUP_CH] = z[:, :FN_GROUP_CH].astype(BF16)
        zs_ref[:, a:a + FN_GROUP_CH] = z[:, FN_GROUP_CH:].astype(BF16)
    ws = []
    for j in range(FNET_KF):
        r0 = j * FFT_R
        s = jnp.concatenate([zc_ref[r0:r0 + FFT_R, :], zs_ref[r0:r0 + FFT_R, :]], axis=0)
        ws.append(_pack_pairs(jnp.dot(g1_ref[j], s, preferred_element_type=F32)))
    o_ref[...] = _from_blocks(ws)


def _fnet_long(x, mods, mod_base, nb):
    L = FFT_R * FFT_R
    d = D_MODEL
    g1, g2 = _fn2_tables()
    cs = _group_dft_table()
    kf = FNET_KF
    bb = pl.pallas_call(
        _fnet_s1_kernel,
        out_shape=jax.ShapeDtypeStruct((nb, FFT_R // kf, kf, FFT_R, d), jnp.uint32),
        grid=(nb, FFT_R // kf),
        in_specs=[pl.BlockSpec((None, FFT_R, kf, d), lambda b, k: (b, 0, k, 0)),
                  pl.BlockSpec((None, 1, 6 * d), lambda b, k: (mod_base + b, 0, 0)),
                  pl.BlockSpec(cs.shape, lambda b, k: (0, 0)),
                  pl.BlockSpec((kf, 2 * FFT_R, 2 * FFT_R), lambda b, k: (k, 0, 0))],
        out_specs=pl.BlockSpec((None, FFT_R // kf, kf, kf, d), lambda b, k: (b, 0, 0, k, 0)),
        scratch_shapes=[pltpu.VMEM((kf * FFT_R, d), BF16), pltpu.VMEM((kf * FFT_R, d), BF16)],
        compiler_params=_cparams(("arbitrary", "arbitrary")),
        name="l1_fnet_stage1",
    )(x.reshape(nb, FFT_R, FFT_R, d), mods, cs, g1)
    y = _lead_out(g2, bb, [], "l1_fnet_stage2")
    return y.reshape(nb * L, d)


def _group_dft_table():
    g = FN_GROUP_CH
    jk = (np.arange(g, dtype=np.int64)[:, None] * np.arange(g, dtype=np.int64)[None, :]) % g
    ang = 2.0 * np.pi * jk / g
    return jnp.asarray(np.concatenate([np.cos(ang), np.sin(ang)], axis=1), F32).astype(BF16)


def _fnet_front_kernel(x_ref, m_ref, cs_ref, zc_ref, zs_ref):
    m = m_ref[...]
    h = (_ln_plain(x_ref[...]) * (1.0 + m[:, D_MODEL:2 * D_MODEL]) + m[:, 0:D_MODEL]).astype(BF16)
    cs = cs_ref[...]
    for g in range(D_MODEL // FN_GROUP_CH):
        a = g * FN_GROUP_CH
        z = jnp.dot(h[:, a:a + FN_GROUP_CH], cs, preferred_element_type=F32)
        zc_ref[:, a:a + FN_GROUP_CH] = z[:, :FN_GROUP_CH].astype(BF16)
        zs_ref[:, a:a + FN_GROUP_CH] = z[:, FN_GROUP_CH:].astype(BF16)


def _fnet_front(x, mods, mod_base, tiles_per_mod):
    t = x.shape[0]
    tm = 2 * ROW_TILE
    tiles_per_mod = max(tiles_per_mod // 2, 1)
    cs = _group_dft_table()
    return pl.pallas_call(
        _fnet_front_kernel,
        out_shape=(jax.ShapeDtypeStruct((t, D_MODEL), BF16),) * 2,
        grid=(t // tm,),
        in_specs=[pl.BlockSpec((tm, D_MODEL), lambda i: (i, 0)),
                  _mod_spec(mod_base, tiles_per_mod),
                  _const_spec(cs.shape)],
        out_specs=(pl.BlockSpec((tm, D_MODEL), lambda i: (i, 0)),) * 2,
        compiler_params=_cparams(("arbitrary",)),
        name="l1_fnet_front",
    )(x, mods, cs)


def _post_kernel(n_a, *refs):
    x_ref, m_ref = refs[0], refs[1]
    a_refs = refs[2:2 + n_a]
    wo_refs = refs[2 + n_a:2 + 2 * n_a]
    g1_ref, b1_ref, w1_ref, w2_ref, g2_ref, b2_ref, o_ref = refs[2 + 2 * n_a:]
    m = m_ref[...]
    d = D_MODEL
    tm = x_ref.shape[0]
    halves = [(r, r + tm // POST_SPLIT) for r in range(0, tm, tm // POST_SPLIT)]
    outs = []
    for r0, r1 in halves:
        out = _bdot(a_refs[0][r0:r1, :], wo_refs[0][...])
        for a_ref, wo_ref in zip(a_refs[1:], wo_refs[1:]):
            out += _bdot(a_ref[r0:r1, :], wo_ref[...])
        outs.append(out)
    x1s, hs = [], []
    for (r0, r1), out in zip(halves, outs):
        x1 = _ln_plain(ALPHA * x_ref[r0:r1, :] + m[:, 2 * d:3 * d] * out) * g1_ref[...] + b1_ref[...]
        x1s.append(x1)
        hs.append((_ln_plain(x1) * (1.0 + m[:, 4 * d:5 * d]) + m[:, 3 * d:4 * d]).astype(BF16))
    accs = []
    n_c = D_FF // d

    def up(h, c):
        hc = jnp.maximum(jnp.dot(h, w1_ref[:, c * d:(c + 1) * d], preferred_element_type=F32), 0.0)
        return (hc * hc).astype(BF16)

    for h in hs:
        acc = None
        nxt = up(h, 0)
        for c in range(n_c):
            cur = nxt
            if c + 1 < n_c:
                nxt = up(h, c + 1)
            part = jnp.dot(cur, w2_ref[c * d:(c + 1) * d, :], preferred_element_type=F32)
            acc = part if acc is None else acc + part
        accs.append(acc)
    for (r0, r1), x1, acc in zip(halves, x1s, accs):
        o_ref[r0:r1, :] = _ln_plain(ALPHA * x1 + m[:, 5 * d:6 * d] * acc) * g2_ref[...] + b2_ref[...]


def _post(x, mods, mod_base, tiles_per_mod, a_list, wo_list, g1, b1, w1, w2, g2, b2):
    t = x.shape[0]
    tm = POST_TILE
    row = lambda c: pl.BlockSpec((tm, c), lambda i: (i, 0))
    once = lambda v: pl.BlockSpec(v.shape, lambda i: (0,) * v.ndim, pipeline_mode=pl.Buffered(1))
    in_specs = ([row(D_MODEL), _mod_spec(mod_base, tiles_per_mod * ROW_TILE // tm)]
                + [row(a.shape[1]) for a in a_list]
                + [once(w) for w in wo_list]
                + [once(v) for v in (g1, b1, w1, w2, g2, b2)])
    return pl.pallas_call(
        functools.partial(_post_kernel, len(a_list)),
        out_shape=jax.ShapeDtypeStruct((t, D_MODEL), F32),
        grid=(t // tm,),
        in_specs=in_specs,
        out_specs=row(D_MODEL),
        compiler_params=_cparams(("arbitrary",)),
        name="post_mlp",
    )(x, mods, *a_list, *wo_list, g1, b1, w1, w2, g2, b2)


def _rot_cols(w):
    parts = []
    for seg in range(2):
        o = seg * 32
        parts += [-w[:, o + 16:o + 32], w[:, o:o + 16]]
    return jnp.concatenate(parts, axis=1)


def _pad_cols(w, n):
    return jnp.pad(w, ((0, 0), (0, n - w.shape[1])))


def _block_diag2(w):
    z = jnp.zeros_like(w)
    return jnp.concatenate([jnp.concatenate([w, z], axis=1), jnp.concatenate([z, w], axis=1)], axis=0)


def _rope_tables(L):
    rows = L // GRID_W
    row = np.repeat(np.arange(rows, dtype=np.float64), GRID_W)
    col = np.tile(np.arange(GRID_W, dtype=np.float64), rows)
    half = QK_ROPE // 2
    inv = 1.0 / (ROPE_THETA ** (np.arange(0, half, 2, dtype=np.float64) / half))
    ar = row[:, None] * inv[None, :]
    ac = col[:, None] * inv[None, :]
    ang = np.concatenate([ar, ar, ac, ac], axis=1)
    cos = np.concatenate([np.cos(ang), np.ones_like(ang)], axis=1)
    sin = np.concatenate([np.sin(ang), np.zeros_like(ang)], axis=1)
    return jnp.asarray(cos, F32), jnp.asarray(sin, F32)


def kernel(x_prompt, x_sample, cache_l0_ckv, cache_l0_krope, c, c_ctx, l0_ada_w, l0_ada_b, l0_w_in, l0_conv_w, l0_conv_b, l0_hf_w1, l0_hf_b1, l0_hf_freq, l0_hf_w2, l0_hf_b2, l0_hf_w3, l0_hf_skip, l0_q_norm, l0_q_up, l0_kv_norm, l0_kv_up, l0_w_out, l0_ln1_g, l0_ln1_b, l0_mlp_w1, l0_mlp_w2, l0_ln2_g, l0_ln2_b, l1_ada_w, l1_ada_b, l1_w_out, l1_ln1_g, l1_ln1_b, l1_mlp_w1, l1_mlp_w2, l1_ln2_g, l1_ln2_b):
    nbc, lc, d = x_prompt.shape
    nbs, ls, _ = x_sample.shape
    past = cache_l0_ckv.shape[1]
    tm = ROW_TILE
    row1 = lambda v: v.reshape(1, -1)

    cond8 = jnp.concatenate([c_ctx[None, :], c, jnp.zeros((8 - 1 - nbs, d), F32)], axis=0)
    mods0 = _modulation(cond8, l0_ada_w, l0_ada_b)
    mods1 = _modulation(cond8, l1_ada_w, l1_ada_b)

    kpe_w = l0_w_in[:, 1920:1984]
    win = jnp.concatenate([l0_w_in[:, :1920], _pad_cols(kpe_w, LANE), _pad_cols(_rot_cols(kpe_w), LANE)],
                          axis=1).astype(BF16)
    dh = QK_NOPE + QK_ROPE
    q_nope = [l0_q_up[:, h * dh:h * dh + QK_NOPE] for h in range(MLA_HEADS)]
    q_pe = [l0_q_up[:, h * dh + QK_NOPE:(h + 1) * dh] for h in range(MLA_HEADS)]
    qup = jnp.concatenate(q_nope + [_pad_cols(w, LANE) for w in q_pe]
                          + [_pad_cols(_rot_cols(w), LANE) for w in q_pe], axis=1).astype(BF16)
    kvup = l0_kv_up.astype(BF16)
    front_w = (win, row1(l0_q_norm), qup, row1(l0_kv_norm), kvup, l0_conv_w, row1(l0_conv_b))
    skip = row1(l0_hf_skip)
    w1p = jnp.pad(l0_hf_w1, ((0, LANE - l0_hf_w1.shape[0]), (0, 0)))
    two = lambda v: jnp.tile(row1(v), (1, 2))
    filt_w = (_block_diag2(w1p), two(l0_hf_b1), two(l0_hf_freq), _block_diag2(l0_hf_w2), two(l0_hf_b2),
              _block_diag2(l0_hf_w3).astype(BF16))
    wo0 = l0_w_out.astype(BF16)

    xc = x_prompt.reshape(nbc * lc, d)
    xs = x_sample.reshape(nbs * ls, d)
    groups = (
        dict(x=xc, nb=nbc, L=lc, mod_base=0, tiles_per_mod=nbc * lc // tm, dft_nb=8, tq=lc, hps=MLA_HEADS),
        dict(x=xs, nb=nbs, L=ls, mod_base=1, tiles_per_mod=ls // tm, dft_nb=nbs, tq=1024, hps=1),
    )
    ones_tab = (jnp.ones((FRONT_TILE, LANE), F32), jnp.zeros((FRONT_TILE, LANE), F32))

    outs = []
    ctx_ckv = ctx_krope = None
    for gi, g in enumerate(groups):
        nb, L = g["nb"], g["L"]
        latent = gi == 1
        cos, sin = _rope_tables(L) if latent else ones_tab
        two_stage = L == FFT_R * FFT_R
        u, x0, q, k, vt, kvn, kpe = _front(g["x"], mods0, g["mod_base"], g["tiles_per_mod"] * tm, front_w,
                                           cos, sin, latent, L, min(L, FRONT_TILE))
        if latent:
            extra = _cache_kv(cache_l0_ckv.reshape(nbs * past, KV_LORA),
                              _pad_cols(cache_l0_krope.reshape(nbs * past, QK_ROPE), LANE), kvup)
        else:
            extra = None
            ctx_ckv = kvn.reshape(nb, L, KV_LORA)
            ctx_krope = kpe.reshape(nb, L, QK_ROPE)
        y_mla = _attention(q, k, vt, extra, nb, L, g["tq"], g["hps"])

        hfilt, hnorm = _filters(L, *filt_w)
        if two_stage:
            y_hy = _hyena_long(u, x0, skip, hfilt, hnorm, nb)
        else:
            kre, kim = _dft("filt", "hy_fwd", [hfilt], [hnorm], 2)
            sh = (nb, L, HY_CH)
            yre, yim = _dft("fwdk", "hy_fwd", [u.reshape(sh)], [kre, kim], g["dft_nb"])
            y_hy = _dft("inv", "hy_inv", [yre, yim], [u.reshape(sh), x0.reshape(sh), skip], g["dft_nb"])
            y_hy = y_hy.reshape(nb * L, HY_CH)

        x1 = _post(g["x"], mods0, g["mod_base"], g["tiles_per_mod"], [y_hy, y_mla],
                   [wo0[:HY_CH], wo0[HY_CH:]], row1(l0_ln1_g), row1(l0_ln1_b),
                   l0_mlp_w1.astype(BF16), l0_mlp_w2.astype(BF16), row1(l0_ln2_g), row1(l0_ln2_b))

        if L == FFT_R * FFT_R:
            yf = _fnet_long(x1, mods1, g["mod_base"], nb)
        else:
            zc, zs = _fnet_front(x1, mods1, g["mod_base"], g["tiles_per_mod"])
            sh = (nb, L, d)
            yf = _dft("fnet", "fnet", [zc.reshape(sh), zs.reshape(sh)], [], g["dft_nb"] // 2)
            yf = yf.reshape(nb * L, d)
        x2 = _post(x1, mods1, g["mod_base"], g["tiles_per_mod"], [yf],
                   [l1_w_out.astype(BF16)], row1(l1_ln1_g), row1(l1_ln1_b),
                   l1_mlp_w1.astype(BF16), l1_mlp_w2.astype(BF16), row1(l1_ln2_g), row1(l1_ln2_b))
        outs.append(x2.reshape(nb, L, d))

    return (outs[0], outs[1], ctx_ckv, ctx_krope)
```

```python
import functools
import math

import numpy as np
import jax
import jax.numpy as jnp
from jax import lax
from jax.experimental import pallas as pl
from jax.experimental.pallas import tpu as pltpu

F32 = jnp.float32
BF16 = jnp.bfloat16
HI = lax.Precision.HIGHEST

D_MODEL = 1024
DEPTH = 2
GRID_W = 64
HY_CH = 512
FILT_BANDS = 16
FILT_ORDER = 64
FAST_DECAY_PCT = 0.3
SLOW_DECAY_PCT = 1.5
DECAY_TARGET = 1e-2
MAX_DECAY = math.log(DECAY_TARGET) / FAST_DECAY_PCT
MIN_DECAY = math.log(DECAY_TARGET) / SLOW_DECAY_PCT
MLA_HEADS = 4
QK_NOPE = 128
QK_ROPE = 64
V_HEAD = 128
Q_LORA = 256
KV_LORA = 128
ROPE_THETA = 10000.0
FN_GROUP_CH = 128
D_FF = 4096
ALPHA = (2 * DEPTH) ** 0.25
LN_EPS = 1e-5
RMS_EPS = 1e-6

LANE = 128
ROW_TILE = 256
FRONT_TILE = 1024
POST_TILE = 512
POST_SPLIT = 2
QK_PAD = 256
VT_ROWS = V_HEAD + 16
LOG2E = 1.4426950408889634
VMEM_LIMIT = 56 * 1024 * 1024


def _cparams(sem):
    return pltpu.CompilerParams(dimension_semantics=sem, vmem_limit_bytes=VMEM_LIMIT)


def _ln_plain(x):
    mu = jnp.mean(x, axis=-1, keepdims=True)
    xc = x - mu
    var = jnp.mean(xc * xc, axis=-1, keepdims=True)
    return xc * lax.rsqrt(var + LN_EPS)


def _rms(x, g):
    return x * lax.rsqrt(jnp.mean(x * x, axis=-1, keepdims=True) + RMS_EPS) * g


def _bdot(a, b):
    return jnp.dot(a.astype(BF16), b, preferred_element_type=F32)


def _vt_rows(v):
    ones = jnp.ones((VT_ROWS - V_HEAD, v.shape[0]), BF16)
    return jnp.concatenate([jnp.transpose(v).astype(BF16), ones], axis=0)


def _mod_kernel(c_ref, w_ref, b_ref, o_ref):
    c = c_ref[...]
    s = c / (1.0 + jnp.exp(-c))
    s_hi = s.astype(BF16)
    s_lo = (s - s_hi.astype(F32)).astype(BF16)
    s2 = jnp.concatenate([s_hi, s_lo], axis=0)
    nr = s.shape[0]
    w = w_ref[...]
    w_hi = w.astype(BF16)
    w_lo = (w - w_hi.astype(F32)).astype(BF16)
    r1 = jnp.dot(s2, w_hi, preferred_element_type=F32)
    r2 = jnp.dot(s_hi, w_lo, preferred_element_type=F32)
    o_ref[...] = r1[:nr] + r1[nr:] + r2 + b_ref[...]


def _modulation(cond8, w, b):
    n = w.shape[1]
    tn = 768
    out = pl.pallas_call(
        _mod_kernel,
        out_shape=jax.ShapeDtypeStruct((8, n), F32),
        grid=(n // tn,),
        in_specs=[pl.BlockSpec((8, D_MODEL), lambda j: (0, 0)),
                  pl.BlockSpec((D_MODEL, tn), lambda j: (0, j)),
                  pl.BlockSpec((1, tn), lambda j: (0, j))],
        out_specs=pl.BlockSpec((8, tn), lambda j: (0, j)),
        compiler_params=_cparams(("arbitrary",)),
        name="modulation",
    )(cond8, w, b.reshape(1, n))
    return out.reshape(8, 1, n)


def _mod_spec(mod_base, tiles_per_mod):
    return pl.BlockSpec((None, 1, 6 * D_MODEL), lambda i: (mod_base + i // tiles_per_mod, 0, 0))


def _const_spec(shape):
    nd = len(shape)
    return pl.BlockSpec(shape, lambda i: (0,) * nd)


HALO = 8


def _front_kernel(tiles_per_seq, x_ref, xp_ref, xn_ref, m_ref, win_ref, qn_ref, qup_ref, kvn_ref, kvup_ref,
                  cos_ref, sin_ref, cw_ref, cb_ref,
                  u_ref, x0_ref, q_ref, k_ref, v_ref, kvn_out_ref, kpe_ref):
    i = pl.program_id(0)
    m = m_ref[...]
    tm = x_ref.shape[0]
    nh = 3 * HY_CH
    xe = jnp.concatenate([xp_ref[...], x_ref[...], xn_ref[...]], axis=0)
    he = _ln_plain(xe) * (1.0 + m[:, D_MODEL:2 * D_MODEL]) + m[:, 0:D_MODEL]

    z = _bdot(he[HALO:HALO + tm], win_ref[:, nh:])
    zh = _bdot(he, win_ref[:, :nh])
    q_c = z[:, 0:256]
    kv_c = z[:, 256:384]
    cos = cos_ref[...]
    sin = sin_ref[...]
    kpe = z[:, 384:512] * cos + z[:, 512:640] * sin
    kpe_ref[...] = kpe[:, :QK_ROPE]
    kpe_b = kpe.astype(BF16)
    q = _bdot(_rms(q_c, qn_ref[...]), qup_ref[...]) * (LOG2E / math.sqrt(QK_NOPE + QK_ROPE))
    kvn = _rms(kv_c, kvn_ref[...])
    kvn_out_ref[...] = kvn
    kv = _bdot(kvn, kvup_ref[...])

    pos = i % tiles_per_seq
    rows = lax.broadcasted_iota(jnp.int32, (tm + 2 * HALO, 1), 0)
    inside = jnp.logical_and(jnp.logical_or(rows >= HALO, pos != 0),
                             jnp.logical_or(rows < tm + HALO, pos != tiles_per_seq - 1))
    zh = jnp.where(inside, zh, 0.0)
    cw = cw_ref[...]
    pz = (pltpu.roll(zh, 1, 0) * cw[0:1, :] + zh * cw[1:2, :]
          + pltpu.roll(zh, tm + 2 * HALO - 1, 0) * cw[2:3, :])[HALO:HALO + tm] + cb_ref[...]
    u_ref[...] = (pz[:, 2 * HY_CH:] * pz[:, HY_CH:2 * HY_CH]).astype(u_ref.dtype)
    x0_ref[...] = pz[:, :HY_CH].astype(x0_ref.dtype)

    for hd in range(MLA_HEADS):
        a = hd * LANE
        q_pe = (q[:, 512 + a:512 + a + LANE] * cos + q[:, 1024 + a:1024 + a + LANE] * sin).astype(BF16)
        q_ref[hd] = jnp.concatenate([q[:, a:a + LANE].astype(BF16), q_pe], axis=-1)
        k_ref[hd] = jnp.concatenate([kv[:, 2 * a:2 * a + LANE].astype(BF16), kpe_b], axis=-1)
        v_ref[hd] = _vt_rows(kv[:, 2 * a + LANE:2 * a + 2 * LANE])


def _front(x, mods, mod_base, rows_per_mod, w, cos, sin, rope, seq_len, tm):
    t = x.shape[0]
    tiles_per_seq = seq_len // tm
    tiles_per_mod = rows_per_mod // tm
    win, qn, qup, kvn, kvup, conv_w, conv_b = w
    if rope:
        tab_spec = pl.BlockSpec((tm, LANE), lambda i: (i % tiles_per_seq, 0))
    else:
        tab_spec = pl.BlockSpec((tm, LANE), lambda i: (0, 0))
    r8 = tm // HALO
    n8 = t // HALO
    hy_out = lambda dt: jax.ShapeDtypeStruct((t, HY_CH), dt)
    hy_spec = pl.BlockSpec((tm, HY_CH), lambda i: (i, 0))
    return pl.pallas_call(
        functools.partial(_front_kernel, tiles_per_seq),
        out_shape=(hy_out(BF16), hy_out(BF16),
                   jax.ShapeDtypeStruct((MLA_HEADS, t, QK_PAD), BF16),
                   jax.ShapeDtypeStruct((MLA_HEADS, t, QK_PAD), BF16),
                   jax.ShapeDtypeStruct((MLA_HEADS, VT_ROWS, t), BF16),
                   jax.ShapeDtypeStruct((t, KV_LORA), F32),
                   jax.ShapeDtypeStruct((t, QK_ROPE), F32)),
        grid=(t // tm,),
        in_specs=[pl.BlockSpec((tm, D_MODEL), lambda i: (i, 0)),
                  pl.BlockSpec((HALO, D_MODEL), lambda i: (jnp.maximum(i * r8 - 1, 0), 0)),
                  pl.BlockSpec((HALO, D_MODEL), lambda i: (jnp.minimum((i + 1) * r8, n8 - 1), 0)),
                  _mod_spec(mod_base, tiles_per_mod),
                  _const_spec(win.shape), _const_spec(qn.shape), _const_spec(qup.shape),
                  _const_spec(kvn.shape), _const_spec(kvup.shape),
                  tab_spec, tab_spec,
                  _const_spec(conv_w.shape), _const_spec(conv_b.shape)],
        out_specs=(hy_spec, hy_spec,
                   pl.BlockSpec((MLA_HEADS, tm, QK_PAD), lambda i: (0, i, 0)),
                   pl.BlockSpec((MLA_HEADS, tm, QK_PAD), lambda i: (0, i, 0)),
                   pl.BlockSpec((MLA_HEADS, VT_ROWS, tm), lambda i: (0, 0, i)),
                   pl.BlockSpec((tm, KV_LORA), lambda i: (i, 0)),
                   pl.BlockSpec((tm, QK_ROPE), lambda i: (i, 0))),
        compiler_params=_cparams(("arbitrary",)),
        name="l0_front",
    )(x, x, x, mods, win, qn, qup, kvn, kvup, cos, sin, conv_w, conv_b)


def _cache_kv_kernel(ckv_ref, kr_ref, kvup_ref, k_ref, v_ref):
    kv = _bdot(ckv_ref[...], kvup_ref[...])
    kr = kr_ref[...].astype(BF16)
    for hd in range(MLA_HEADS):
        a = 2 * hd * LANE
        k_ref[hd] = jnp.concatenate([kv[:, a:a + LANE].astype(BF16), kr], axis=-1)
        v_ref[hd] = _vt_rows(kv[:, a + LANE:a + 2 * LANE])


def _cache_kv(ckv, krope_pad, kvup):
    t = ckv.shape[0]
    return pl.pallas_call(
        _cache_kv_kernel,
        out_shape=(jax.ShapeDtypeStruct((MLA_HEADS, t, QK_PAD), BF16),
                   jax.ShapeDtypeStruct((MLA_HEADS, VT_ROWS, t), BF16)),
        name="l0_cache_kv",
    )(ckv, krope_pad, kvup)


def _col_reduce(x, op):
    rows, n = x.shape
    for g in (32, 8):
        if rows % (8 * g) == 0 and rows > 8 * g:
            x = op(x.reshape(rows // (8 * g), 8 * g, n), axis=0)
            rows = 8 * g
    return op(x, axis=0, keepdims=True)


PV_CHUNK = 128


def _attn_kernel(n_kv, hps, q_ref, *refs):
    k_refs, vt_refs = refs[:n_kv], refs[n_kv:2 * n_kv]
    o_ref, s_even, s_odd, m_even, m_odd = refs[2 * n_kv:]
    i = pl.program_id(0)

    @pl.when(i == 0)
    def _():
        s_odd[...] = jnp.zeros_like(s_odd)
        m_odd[...] = jnp.zeros_like(m_odd)

    def step(s_write, m_write, s_read, m_read):
        nt = (((1,), (1,)), ((), ()))
        for h in range(hps):
            q = q_ref[h]
            r0 = 0
            m = None
            for k_ref in k_refs:
                lk = k_ref.shape[1]
                sblk = lax.dot_general(k_ref[h], q, nt, preferred_element_type=F32)
                s_write[h, r0:r0 + lk, :] = sblk
                mc = _col_reduce(sblk, jnp.max)
                m = mc if m is None else jnp.maximum(m, mc)
                r0 += lk
            m_write[h] = m
        for h in range(hps):
            m = m_read[h]
            acc = None
            r0 = 0
            for vt_ref in vt_refs:
                lk = vt_ref.shape[2]
                for c0 in range(0, lk, PV_CHUNK):
                    c1 = min(c0 + PV_CHUNK, lk)
                    pb = jnp.exp2(s_read[h, r0 + c0:r0 + c1, :] - m).astype(BF16)
                    pv = jnp.dot(vt_ref[h, :, c0:c1], pb, preferred_element_type=F32)
                    acc = pv if acc is None else acc + pv
                r0 += lk
            o_ref[:, h * V_HEAD:(h + 1) * V_HEAD] = jnp.transpose(
                acc[:V_HEAD] / acc[V_HEAD:V_HEAD + 1]).astype(o_ref.dtype)

    pl.when(i % 2 == 0)(lambda: step(s_even, m_even, s_odd, m_odd))
    pl.when(i % 2 == 1)(lambda: step(s_odd, m_odd, s_even, m_even))


def _attention(q, k, vt, extra, nb, lq, tq, hps):
    nq = lq // tq
    ng = MLA_HEADS // hps
    n_tiles = nb * ng * nq

    def where(t):
        bh = t // nq
        return bh // ng, bh % ng, t % nq

    def score_side(fn):
        return lambda i: fn(*where(jnp.minimum(i, n_tiles - 1)))

    def value_side(fn):
        return lambda i: fn(*where(jnp.maximum(i - 1, 0)))

    ks, vts = [k], [vt]
    if extra is not None:
        ks.append(extra[0])
        vts.append(extra[1])
    in_specs = [pl.BlockSpec((hps, tq, QK_PAD), score_side(lambda b, h, j: (h, b * nq + j, 0)))]
    in_specs += [pl.BlockSpec((hps, a.shape[1] // nb, QK_PAD), score_side(lambda b, h, j: (h, b, 0))) for a in ks]
    in_specs += [pl.BlockSpec((hps, VT_ROWS, a.shape[2] // nb), value_side(lambda b, h, j: (h, 0, b))) for a in vts]
    lk_total = sum(a.shape[1] // nb for a in ks)
    return pl.pallas_call(
        functools.partial(_attn_kernel, len(ks), hps),
        out_shape=jax.ShapeDtypeStruct((nb * lq, MLA_HEADS * V_HEAD), BF16),
        grid=(n_tiles + 1,),
        in_specs=in_specs,
        out_specs=pl.BlockSpec((tq, hps * V_HEAD), value_side(lambda b, h, j: (b * nq + j, h))),
        scratch_shapes=[pltpu.VMEM((hps, lk_total, tq), F32), pltpu.VMEM((hps, lk_total, tq), F32),
                        pltpu.VMEM((hps, 1, tq), F32), pltpu.VMEM((hps, 1, tq), F32)],
        compiler_params=_cparams(("arbitrary",)),
        name="l0_attention",
    )(q, *ks, *vts)


def _filter_kernel(z_ref, w1_ref, b1_ref, fr_ref, w2_ref, b2_ref, w3_ref, dl_ref, h_ref, norm_ref):
    i = pl.program_id(0)
    z = z_ref[...]
    tl = z.shape[0]
    fr = fr_ref[...]
    z2 = jnp.concatenate([z[:tl // 2], z[tl // 2:]], axis=1)
    h = jnp.sin(fr * (jnp.dot(z2, w1_ref[...], precision=HI, preferred_element_type=F32) + b1_ref[...]))
    h = jnp.sin(fr * (jnp.dot(h, w2_ref[...], precision=HI, preferred_element_type=F32) + b2_ref[...]))
    h = _bdot(h, w3_ref[...])
    h = jnp.concatenate([h[:, :2 * HY_CH], h[:, 2 * HY_CH:]], axis=0)
    decay = jnp.exp(-(z[:, 0:1] * dl_ref[...]))
    hf = h[:, :HY_CH] * decay
    hb = h[:, HY_CH:] * decay
    part = jnp.sum(jnp.abs(hf) + jnp.abs(hb), axis=0, keepdims=True)

    @pl.when(i == 0)
    def _():
        norm_ref[...] = part

    @pl.when(i > 0)
    def _():
        norm_ref[...] += part

    rows = lax.broadcasted_iota(jnp.int32, hb.shape, 0) + i * tl
    h_ref[0] = hf.astype(h_ref.dtype)
    h_ref[1] = jnp.where(rows == 0, 0.0, hb).astype(h_ref.dtype)


def _filter_embedding(L):
    t = np.linspace(0.0, 1.0, L)[:, None]
    w_ang = 2.0 * np.pi * np.arange(L) / L
    bands = np.linspace(1e-4, FILT_BANDS - 1, FILT_BANDS)
    ang = w_ang[:, None] * bands[None, :]
    z = np.zeros((L, LANE), np.float64)
    z[:, 0:1] = t
    z[:, 1:1 + FILT_BANDS] = np.cos(ang)
    z[:, 1 + FILT_BANDS:1 + 2 * FILT_BANDS] = -np.sin(ang)
    return jnp.asarray(z, F32)


def _filters(L, w1p, b1, fr, w2, b2, w3):
    tl = min(L, 1024)
    z = _filter_embedding(L)
    deltas = jnp.asarray(np.abs(np.linspace(MIN_DECAY, MAX_DECAY, HY_CH))[None, :], F32)
    return pl.pallas_call(
        _filter_kernel,
        out_shape=(jax.ShapeDtypeStruct((2, L, HY_CH), BF16), jax.ShapeDtypeStruct((1, HY_CH), F32)),
        grid=(L // tl,),
        in_specs=[pl.BlockSpec((tl, LANE), lambda i: (i, 0)),
                  _const_spec(w1p.shape), _const_spec(b1.shape), _const_spec(fr.shape),
                  _const_spec(w2.shape), _const_spec(b2.shape), _const_spec(w3.shape),
                  _const_spec(deltas.shape)],
        out_specs=(pl.BlockSpec((2, tl, HY_CH), lambda i: (0, i, 0)),
                   pl.BlockSpec((1, HY_CH), lambda i: (0, 0))),
        compiler_params=_cparams(("arbitrary",)),
        name="l0_hyena_filters",
    )(z, w1p, b1, fr, w2, b2, w3, deltas)


def _dft_tables(kind, L, ti):
    ni = L // ti
    i = np.arange(ti, dtype=np.int64)[:, None]
    big = (np.arange(ni, dtype=np.int64) * ti)[:, None]
    c = np.arange(L, dtype=np.int64)[None, :]
    if kind == "hy_fwd":
        period = 4 * L
        base_idx = (2 * i + 1) * c
        r_idx = 2 * big * c
        scale = 1.0
    elif kind == "hy_inv":
        period = 4 * L
        base_idx = (2 * c + 1) * i
        r_idx = (2 * c + 1) * big
        scale = 1.0 / L
    else:
        period = L
        base_idx = i * c
        r_idx = big * c
        scale = 1.0 / math.sqrt(L * FN_GROUP_CH)
    ab = 2.0 * np.pi * (base_idx % period) / period
    ar = 2.0 * np.pi * (r_idx % period) / period
    return (jnp.asarray(np.cos(ab), F32), jnp.asarray(np.sin(ab), F32),
            jnp.asarray(scale * np.cos(ar), F32).reshape(ni, 1, L),
            jnp.asarray(scale * np.sin(ar), F32).reshape(ni, 1, L))


def _dft_kernel(mode, nb, n_x, *refs):
    bc_ref, bs_ref, rc_ref, rs_ref = refs[:4]
    x_refs = refs[4:4 + n_x]
    rest = refs[4 + n_x:]
    p_ref, q_ref = rest[-2], rest[-1]
    j = pl.program_id(2)
    nj = pl.num_programs(2)
    tj = x_refs[0].shape[1]
    if bc_ref.shape[1] == tj:
        bc, bs, rc, rs = bc_ref[...], bs_ref[...], rc_ref[...], rs_ref[...]
    else:
        off = pl.multiple_of(j * tj, tj)
        bc, bs = bc_ref[:, pl.ds(off, tj)], bs_ref[:, pl.ds(off, tj)]
        rc, rs = rc_ref[:, pl.ds(off, tj)], rs_ref[:, pl.ds(off, tj)]
    tc = (bc * rc - bs * rs).astype(BF16)
    ts = (bs * rc + bc * rs).astype(BF16)
    x1_ref = x_refs[0]
    x2_ref = x_refs[-1]

    pq = [(jnp.dot(tc, x1_ref[b].astype(BF16), preferred_element_type=F32),
           jnp.dot(ts, x2_ref[b].astype(BF16), preferred_element_type=F32)) for b in range(nb)]

    @pl.when(j == 0)
    def _():
        for b in range(nb):
            p_ref[b] = pq[b][0]
            q_ref[b] = pq[b][1]

    @pl.when(j > 0)
    def _():
        for b in range(nb):
            p_ref[b] += pq[b][0]
            q_ref[b] += pq[b][1]

    @pl.when(j == nj - 1)
    def _():
        if mode == "filt":
            nrm = rest[0][...]
            kre_ref, kim_ref = rest[1], rest[2]
            kre_ref[...] = (p_ref[0] + p_ref[1]) / nrm
            kim_ref[...] = (q_ref[1] - q_ref[0]) / nrm
        elif mode == "fwdk":
            kre, kim = rest[0][...], rest[1][...]
            yre_ref, yim_ref = rest[2], rest[3]
            for b in range(nb):
                pp, qq = p_ref[b], q_ref[b]
                yre_ref[b] = (pp * kre + qq * kim).astype(BF16)
                yim_ref[b] = (pp * kim - qq * kre).astype(BF16)
        elif mode == "inv":
            u_ref, x0_ref, skip_ref, o_ref = rest[0], rest[1], rest[2], rest[3]
            skip = skip_ref[...]
            for b in range(nb):
                o_ref[b] = ((p_ref[b] - q_ref[b] + u_ref[b] * skip) * x0_ref[b].astype(F32)).astype(BF16)
        else:
            o_ref = rest[0]
            for b in range(nb):
                o_ref[b] = (p_ref[b] - q_ref[b]).astype(BF16)


def _dft(mode, kind, xs, extras, nb):
    B, L, C = xs[0].shape
    ti = min(L, 256)
    tj = min(L, 512)
    bc, bs, rc, rs = _dft_tables(kind, L, ti)
    grid = (B // nb, L // ti, L // tj)
    x_spec = pl.BlockSpec((nb, tj, C), lambda g, i, j: (g, j, 0))
    row_spec = lambda c, dt=None: pl.BlockSpec((nb, ti, c), lambda g, i, j: (g, i, 0))
    in_specs = [pl.BlockSpec((ti, L), lambda g, i, j: (0, 0)),
                pl.BlockSpec((ti, L), lambda g, i, j: (0, 0)),
                pl.BlockSpec((None, 1, L), lambda g, i, j: (i, 0, 0)),
                pl.BlockSpec((None, 1, L), lambda g, i, j: (i, 0, 0))] + [x_spec] * len(xs)
    if mode == "filt":
        in_specs += [pl.BlockSpec((1, HY_CH), lambda g, i, j: (0, 0))]
        out_shape = (jax.ShapeDtypeStruct((L, HY_CH), F32),) * 2
        out_specs = (pl.BlockSpec((ti, HY_CH), lambda g, i, j: (i, 0)),) * 2
    elif mode == "fwdk":
        in_specs += [pl.BlockSpec((ti, HY_CH), lambda g, i, j: (i, 0))] * 2
        out_shape = (jax.ShapeDtypeStruct((B, L, C), BF16),) * 2
        out_specs = (row_spec(C),) * 2
    elif mode == "inv":
        in_specs += [row_spec(C)] * 2 + [pl.BlockSpec((1, C), lambda g, i, j: (0, 0))]
        out_shape = jax.ShapeDtypeStruct((B, L, C), BF16)
        out_specs = row_spec(C)
    else:
        out_shape = jax.ShapeDtypeStruct((B, L, C), BF16)
        out_specs = row_spec(C)
    return pl.pallas_call(
        functools.partial(_dft_kernel, mode, nb, len(xs)),
        out_shape=out_shape,
        grid=grid,
        in_specs=in_specs,
        out_specs=out_specs,
        scratch_shapes=[pltpu.VMEM((nb, ti, C), F32), pltpu.VMEM((nb, ti, C), F32)],
        compiler_params=_cparams(("arbitrary", "arbitrary", "arbitrary")),
        name="dft_" + mode,
    )(bc, bs, rc, rs, *xs, *extras)


FFT_R = 64
HYENA_KF = 16
FNET_KF = 16


def _pack_pairs(x):
    return pltpu.bitcast(x.astype(BF16), jnp.uint32)


def _unpack_pairs(w):
    return pltpu.bitcast(w, BF16)


def _to_blocks(w):
    return jnp.swapaxes(w.reshape(FFT_R, w.shape[-2], w.shape[-1]), 0, 1)


def _from_blocks(ws):
    kf, c = len(ws), ws[0].shape[-1]
    return jnp.swapaxes(jnp.stack(ws, axis=0), 0, 1).reshape(FFT_R // kf, kf, kf, c)


def _lead_in_kernel(g_ref, x_ref, o_ref):
    g = g_ref[...]
    xt = jnp.swapaxes(x_ref[...], 0, 1)
    for j in range(x_ref.shape[1]):
        o_ref[j] = _pack_pairs(jnp.dot(g, xt[j].astype(BF16), preferred_element_type=F32))


def _lead_in(g, x, kf, name):
    nbx, _, _, c = x.shape
    m2 = g.shape[0] // 2
    return pl.pallas_call(
        _lead_in_kernel,
        out_shape=jax.ShapeDtypeStruct((nbx, FFT_R // kf, kf, m2, c), jnp.uint32),
        grid=(nbx, FFT_R // kf),
        in_specs=[pl.BlockSpec(g.shape, lambda b, k: (0, 0)),
                  pl.BlockSpec((None, FFT_R, kf, c), lambda b, k: (b, 0, k, 0))],
        out_specs=pl.BlockSpec((None, None, kf, m2, c), lambda b, k: (b, k, 0, 0, 0)),
        compiler_params=_cparams(("arbitrary", "arbitrary")),
        name=name,
    )(g, x)


def _lead_out_kernel(n_extra, g_ref, w_ref, *rest):
    g = g_ref[...]
    o_ref = rest[-1]
    ys = [jnp.dot(g, _unpack_pairs(w_ref[j]), preferred_element_type=F32) for j in range(w_ref.shape[0])]
    y = jnp.swapaxes(jnp.stack(ys, axis=0), 0, 1)
    if n_extra:
        y = (y + rest[0][...] * rest[2][...]) * rest[1][...].astype(F32)
    o_ref[...] = y.astype(o_ref.dtype)


def _lead_out(g, w, epilogue, name):
    nb, nk, kf, k2, c = w.shape
    blk = pl.BlockSpec((None, FFT_R, kf, c), lambda b, k: (b, 0, k, 0))
    extra_specs = [blk, blk, pl.BlockSpec((1, c), lambda b, k: (0, 0))] if epilogue else []
    return pl.pallas_call(
        functools.partial(_lead_out_kernel, len(epilogue)),
        out_shape=jax.ShapeDtypeStruct((nb, FFT_R, FFT_R, c), BF16),
        grid=(nb, nk),
        in_specs=[pl.BlockSpec(g.shape, lambda b, k: (0, 0)),
                  pl.BlockSpec((None, None, kf, k2, c), lambda b, k: (b, k, 0, 0, 0))] + extra_specs,
        out_specs=blk,
        compiler_params=_cparams(("arbitrary", "arbitrary")),
        name=name,
    )(g, w, *epilogue)


def _interleave(a, b, axis):
    st = np.stack([a, b], axis=axis + 1)
    shape = list(a.shape)
    shape[axis] *= 2
    return st.reshape(shape)


def _hy2_tables():
    L = FFT_R * FFT_R
    n2 = 2 * L
    f1 = np.arange(2 * FFT_R, dtype=np.int64)
    s1 = np.arange(FFT_R, dtype=np.int64)
    th = np.pi * (((2 * f1[:, None] + 1) * s1[None, :]) % (4 * FFT_R)) / (2 * FFT_R)
    ga = _interleave(np.cos(th), -np.sin(th), 0)
    ma = _interleave(np.cos(th).T, -np.sin(th).T, 1) / L
    f2 = np.arange(FFT_R // 2, dtype=np.int64)
    s2 = np.arange(FFT_R, dtype=np.int64)
    idx = ((n2 // FFT_R) * 2 * f2[None, :, None] * s2[None, None, :]
           + (2 * f1[:, None, None] + 1) * s2[None, None, :]) % (2 * n2)
    al = np.pi * idx / n2
    c, s = np.cos(al), np.sin(al)
    nmat = np.concatenate([_interleave(c, s, 2), _interleave(-s, c, 2)], axis=1)
    ct, st = np.transpose(c, (0, 2, 1)), np.transpose(s, (0, 2, 1))
    mmat = _interleave(np.concatenate([ct, -st], axis=2), np.concatenate([st, ct], axis=2), 1)
    bf = lambda a: jnp.asarray(a, F32).astype(BF16)
    return bf(ga), bf(nmat), bf(mmat), bf(ma)


def _hy_mid_kernel(a_ref, n_ref, m_ref, k_ref, e_ref):
    half = FFT_R // 2
    kf = a_ref.shape[-2]
    a = _to_blocks(a_ref[...])
    ts = [jnp.dot(n_ref[j], _unpack_pairs(a[j]), preferred_element_type=F32)
          for j in range(kf)]
    ys = []
    for j in range(kf):
        tr, ti = ts[j][:half], ts[j][half:]
        kr, ki = k_ref[j, 0].astype(F32), k_ref[j, 1].astype(F32)
        ys.append(jnp.concatenate([tr * kr - ti * ki, tr * ki + ti * kr], axis=0).astype(BF16))
    e_ref[...] = _from_blocks([_pack_pairs(jnp.dot(m_ref[j], ys[j], preferred_element_type=F32))
                               for j in range(kf)])


def _hy_mid(a, nmat, mmat, khat):
    nb, nk, kf, _, c = a.shape
    nf1 = 2 * FFT_R
    blk = pl.BlockSpec((None, nk, kf, kf, c), lambda i, b: (b, 0, 0, i, 0))
    return pl.pallas_call(
        _hy_mid_kernel,
        out_shape=jax.ShapeDtypeStruct(a.shape, jnp.uint32),
        grid=(nf1 // kf, nb),
        in_specs=[blk,
                  pl.BlockSpec((kf, FFT_R, 2 * FFT_R), lambda i, b: (i, 0, 0)),
                  pl.BlockSpec((kf, 2 * FFT_R, FFT_R), lambda i, b: (i, 0, 0)),
                  pl.BlockSpec((kf, 2, FFT_R // 2, c), lambda i, b: (i, 0, 0, 0))],
        out_specs=blk,
        compiler_params=_cparams(("arbitrary", "arbitrary")),
        name="l0_hyena_mid",
    )(a, nmat, mmat, khat)


def _hy_kfilt_kernel(a_ref, n_ref, nrm_ref, k_ref):
    half = FFT_R // 2
    nrm = nrm_ref[...]
    af, ab = _to_blocks(a_ref[0]), _to_blocks(a_ref[1])
    for j in range(a_ref.shape[-2]):
        tf = jnp.dot(n_ref[j], _unpack_pairs(af[j]), preferred_element_type=F32)
        tb = jnp.dot(n_ref[j], _unpack_pairs(ab[j]), preferred_element_type=F32)
        k_ref[j, 0] = ((tf[:half] + tb[:half]) / nrm).astype(k_ref.dtype)
        k_ref[j, 1] = ((tf[half:] - tb[half:]) / nrm).astype(k_ref.dtype)


def _hy_kfilt(a, nmat, nrm):
    _, nk, kf, _, c = a.shape
    nf1 = 2 * FFT_R
    return pl.pallas_call(
        _hy_kfilt_kernel,
        out_shape=jax.ShapeDtypeStruct((nf1, 2, FFT_R // 2, c), BF16),
        grid=(nf1 // kf,),
        in_specs=[pl.BlockSpec((2, nk, kf, kf, c), lambda i: (0, 0, 0, i, 0)),
                  pl.BlockSpec((kf, FFT_R, 2 * FFT_R), lambda i: (i, 0, 0)),
                  _const_spec(nrm.shape)],
        out_specs=pl.BlockSpec((kf, 2, FFT_R // 2, c), lambda i: (i, 0, 0, 0)),
        compiler_params=_cparams(("arbitrary",)),
        name="l0_hyena_kfilt",
    )(a, nmat, nrm)


def _hyena_long(u, x0, skip, hfilt, nrm, nb):
    L = FFT_R * FFT_R
    c = u.shape[-1]
    v4 = lambda a, n: a.reshape(n, FFT_R, FFT_R, c)
    ga, nmat, mmat, ma = _hy2_tables()
    khat = _hy_kfilt(_lead_in(ga, v4(hfilt, 2), HYENA_KF, "l0_hyena_fwd_a"), nmat, nrm)
    ee = _hy_mid(_lead_in(ga, v4(u, nb), HYENA_KF, "l0_hyena_fwd_a"), nmat, mmat, khat)
    y = _lead_out(ma, ee, (v4(u, nb), v4(x0, nb), skip), "l0_hyena_inv_a")
    return y.reshape(nb * L, c)


def _fn2_tables():
    L = FFT_R * FFT_R
    r = np.arange(FFT_R, dtype=np.int64)
    idx = (FFT_R * r[None, :, None] * r[None, None, :] + r[None, :, None] * r[:, None, None]) % L
    gm = 2.0 * np.pi * idx / L
    c, s = np.cos(gm), np.sin(gm)
    g1 = _interleave(np.concatenate([c, -s], axis=2), np.concatenate([-s, -c], axis=2), 1)
    dl = 2.0 * np.pi * ((r[:, None] * r[None, :]) % FFT_R) / FFT_R
    g2 = _interleave(np.cos(dl), np.sin(dl), 1) / math.sqrt(L * FN_GROUP_CH)
    bf = lambda a: jnp.asarray(a, F32).astype(BF16)
    return bf(g1), bf(g2)


def _fnet_s1_kernel(x_ref, m_ref, cs_ref, g1_ref, o_ref, zc_ref, zs_ref):
    xs = jnp.swapaxes(x_ref[...], 0, 1).reshape(FNET_KF * FFT_R, D_MODEL)
    m = m_ref[...]
    h = (_ln_plain(xs) * (1.0 + m[:, D_MODEL:2 * D_MODEL]) + m[:, 0:D_MODEL]).astype(BF16)
    cs = cs_ref[...]
    for g in range(D_MODEL // FN_GROUP_CH):
        a = g * FN_GROUP_CH
        z = jnp.dot(h[:, a:a + FN_GROUP_CH], cs, preferred_element_type=F32)
        zc_ref[:, a:a + FN_GROUP_CH] = z[:, :FN_GROUP_CH].astype(BF16)
        zs_ref[:, a:a + FN_GROUP_CH] = z[:, FN_GROUP_CH:].astype(BF16)
    ws = []
    for j in range(FNET_KF):
        r0 = j * FFT_R
        s = jnp.concatenate([zc_ref[r0:r0 + FFT_R, :], zs_ref[r0:r0 + FFT_R, :]], axis=0)
        ws.append(_pack_pairs(jnp.dot(g1_ref[j], s, preferred_element_type=F32)))
    o_ref[...] = _from_blocks(ws)


def _fnet_long(x, mods, mod_base, nb):
    L = FFT_R * FFT_R
    d = D_MODEL
    g1, g2 = _fn2_tables()
    cs = _group_dft_table()
    kf = FNET_KF
    bb = pl.pallas_call(
        _fnet_s1_kernel,
        out_shape=jax.ShapeDtypeStruct((nb, FFT_R // kf, kf, FFT_R, d), jnp.uint32),
        grid=(nb, FFT_R // kf),
        in_specs=[pl.BlockSpec((None, FFT_R, kf, d), lambda b, k: (b, 0, k, 0)),
                  pl.BlockSpec((None, 1, 6 * d), lambda b, k: (mod_base + b, 0, 0)),
                  pl.BlockSpec(cs.shape, lambda b, k: (0, 0)),
                  pl.BlockSpec((kf, 2 * FFT_R, 2 * FFT_R), lambda b, k: (k, 0, 0))],
        out_specs=pl.BlockSpec((None, FFT_R // kf, kf, kf, d), lambda b, k: (b, 0, 0, k, 0)),
        scratch_shapes=[pltpu.VMEM((kf * FFT_R, d), BF16), pltpu.VMEM((kf * FFT_R, d), BF16)],
        compiler_params=_cparams(("arbitrary", "arbitrary")),
        name="l1_fnet_stage1",
    )(x.reshape(nb, FFT_R, FFT_R, d), mods, cs, g1)
    y = _lead_out(g2, bb, [], "l1_fnet_stage2")
    return y.reshape(nb * L, d)


def _group_dft_table():
    g = FN_GROUP_CH
    jk = (np.arange(g, dtype=np.int64)[:, None] * np.arange(g, dtype=np.int64)[None, :]) % g
    ang = 2.0 * np.pi * jk / g
    return jnp.asarray(np.concatenate([np.cos(ang), np.sin(ang)], axis=1), F32).astype(BF16)


def _fnet_front_kernel(x_ref, m_ref, cs_ref, zc_ref, zs_ref):
    m = m_ref[...]
    h = (_ln_plain(x_ref[...]) * (1.0 + m[:, D_MODEL:2 * D_MODEL]) + m[:, 0:D_MODEL]).astype(BF16)
    cs = cs_ref[...]
    for g in range(D_MODEL // FN_GROUP_CH):
        a = g * FN_GROUP_CH
        z = jnp.dot(h[:, a:a + FN_GROUP_CH], cs, preferred_element_type=F32)
        zc_ref[:, a:a + FN_GROUP_CH] = z[:, :FN_GROUP_CH].astype(BF16)
        zs_ref[:, a:a + FN_GROUP_CH] = z[:, FN_GROUP_CH:].astype(BF16)


def _fnet_front(x, mods, mod_base, tiles_per_mod):
    t = x.shape[0]
    tm = 2 * ROW_TILE
    tiles_per_mod = max(tiles_per_mod // 2, 1)
    cs = _group_dft_table()
    return pl.pallas_call(
        _fnet_front_kernel,
        out_shape=(jax.ShapeDtypeStruct((t, D_MODEL), BF16),) * 2,
        grid=(t // tm,),
        in_specs=[pl.BlockSpec((tm, D_MODEL), lambda i: (i, 0)),
                  _mod_spec(mod_base, tiles_per_mod),
                  _const_spec(cs.shape)],
        out_specs=(pl.BlockSpec((tm, D_MODEL), lambda i: (i, 0)),) * 2,
        compiler_params=_cparams(("arbitrary",)),
        name="l1_fnet_front",
    )(x, mods, cs)


def _post_kernel(n_a, *refs):
    x_ref, m_ref = refs[0], refs[1]
    a_refs = refs[2:2 + n_a]
    wo_refs = refs[2 + n_a:2 + 2 * n_a]
    g1_ref, b1_ref, w1_ref, w2_ref, g2_ref, b2_ref, o_ref = refs[2 + 2 * n_a:]
    m = m_ref[...]
    d = D_MODEL
    tm = x_ref.shape[0]
    halves = [(r, r + tm // POST_SPLIT) for r in range(0, tm, tm // POST_SPLIT)]
    outs = []
    for r0, r1 in halves:
        out = _bdot(a_refs[0][r0:r1, :], wo_refs[0][...])
        for a_ref, wo_ref in zip(a_refs[1:], wo_refs[1:]):
            out += _bdot(a_ref[r0:r1, :], wo_ref[...])
        outs.append(out)
    x1s, hs = [], []
    for (r0, r1), out in zip(halves, outs):
        x1 = _ln_plain(ALPHA * x_ref[r0:r1, :] + m[:, 2 * d:3 * d] * out) * g1_ref[...] + b1_ref[...]
        x1s.append(x1)
        hs.append((_ln_plain(x1) * (1.0 + m[:, 4 * d:5 * d]) + m[:, 3 * d:4 * d]).astype(BF16))
    accs = []
    n_c = D_FF // d

    def up(h, c):
        hc = jnp.maximum(jnp.dot(h, w1_ref[:, c * d:(c + 1) * d], preferred_element_type=F32), 0.0)
        return (hc * hc).astype(BF16)

    for h in hs:
        acc = None
        nxt = up(h, 0)
        for c in range(n_c):
            cur = nxt
            if c + 1 < n_c:
                nxt = up(h, c + 1)
            part = jnp.dot(cur, w2_ref[c * d:(c + 1) * d, :], preferred_element_type=F32)
            acc = part if acc is None else acc + part
        accs.append(acc)
    for (r0, r1), x1, acc in zip(halves, x1s, accs):
        o_ref[r0:r1, :] = _ln_plain(ALPHA * x1 + m[:, 5 * d:6 * d] * acc) * g2_ref[...] + b2_ref[...]


def _post(x, mods, mod_base, tiles_per_mod, a_list, wo_list, g1, b1, w1, w2, g2, b2):
    t = x.shape[0]
    tm = POST_TILE
    row = lambda c: pl.BlockSpec((tm, c), lambda i: (i, 0))
    once = lambda v: pl.BlockSpec(v.shape, lambda i: (0,) * v.ndim, pipeline_mode=pl.Buffered(1))
    in_specs = ([row(D_MODEL), _mod_spec(mod_base, tiles_per_mod * ROW_TILE // tm)]
                + [row(a.shape[1]) for a in a_list]
                + [once(w) for w in wo_list]
                + [once(v) for v in (g1, b1, w1, w2, g2, b2)])
    return pl.pallas_call(
        functools.partial(_post_kernel, len(a_list)),
        out_shape=jax.ShapeDtypeStruct((t, D_MODEL), F32),
        grid=(t // tm,),
        in_specs=in_specs,
        out_specs=row(D_MODEL),
        compiler_params=_cparams(("arbitrary",)),
        name="post_mlp",
    )(x, mods, *a_list, *wo_list, g1, b1, w1, w2, g2, b2)


def _rot_cols(w):
    parts = []
    for seg in range(2):
        o = seg * 32
        parts += [-w[:, o + 16:o + 32], w[:, o:o + 16]]
    return jnp.concatenate(parts, axis=1)


def _pad_cols(w, n):
    return jnp.pad(w, ((0, 0), (0, n - w.shape[1])))


def _block_diag2(w):
    z = jnp.zeros_like(w)
    return jnp.concatenate([jnp.concatenate([w, z], axis=1), jnp.concatenate([z, w], axis=1)], axis=0)


def _rope_tables(L):
    rows = L // GRID_W
    row = np.repeat(np.arange(rows, dtype=np.float64), GRID_W)
    col = np.tile(np.arange(GRID_W, dtype=np.float64), rows)
    half = QK_ROPE // 2
    inv = 1.0 / (ROPE_THETA ** (np.arange(0, half, 2, dtype=np.float64) / half))
    ar = row[:, None] * inv[None, :]
    ac = col[:, None] * inv[None, :]
    ang = np.concatenate([ar, ar, ac, ac], axis=1)
    cos = np.concatenate([np.cos(ang), np.ones_like(ang)], axis=1)
    sin = np.concatenate([np.sin(ang), np.zeros_like(ang)], axis=1)
    return jnp.asarray(cos, F32), jnp.asarray(sin, F32)


def kernel(x_prompt, x_sample, cache_l0_ckv, cache_l0_krope, c, c_ctx, l0_ada_w, l0_ada_b, l0_w_in, l0_conv_w, l0_conv_b, l0_hf_w1, l0_hf_b1, l0_hf_freq, l0_hf_w2, l0_hf_b2, l0_hf_w3, l0_hf_skip, l0_q_norm, l0_q_up, l0_kv_norm, l0_kv_up, l0_w_out, l0_ln1_g, l0_ln1_b, l0_mlp_w1, l0_mlp_w2, l0_ln2_g, l0_ln2_b, l1_ada_w, l1_ada_b, l1_w_out, l1_ln1_g, l1_ln1_b, l1_mlp_w1, l1_mlp_w2, l1_ln2_g, l1_ln2_b):
    nbc, lc, d = x_prompt.shape
    nbs, ls, _ = x_sample.shape
    past = cache_l0_ckv.shape[1]
    tm = ROW_TILE
    row1 = lambda v: v.reshape(1, -1)

    cond8 = jnp.concatenate([c_ctx[None, :], c, jnp.zeros((8 - 1 - nbs, d), F32)], axis=0)
    mods0 = _modulation(cond8, l0_ada_w, l0_ada_b)
    mods1 = _modulation(cond8, l1_ada_w, l1_ada_b)

    kpe_w = l0_w_in[:, 1920:1984]
    win = jnp.concatenate([l0_w_in[:, :1920], _pad_cols(kpe_w, LANE), _pad_cols(_rot_cols(kpe_w), LANE)],
                          axis=1).astype(BF16)
    dh = QK_NOPE + QK_ROPE
    q_nope = [l0_q_up[:, h * dh:h * dh + QK_NOPE] for h in range(MLA_HEADS)]
    q_pe = [l0_q_up[:, h * dh + QK_NOPE:(h + 1) * dh] for h in range(MLA_HEADS)]
    qup = jnp.concatenate(q_nope + [_pad_cols(w, LANE) for w in q_pe]
                          + [_pad_cols(_rot_cols(w), LANE) for w in q_pe], axis=1).astype(BF16)
    kvup = l0_kv_up.astype(BF16)
    front_w = (win, row1(l0_q_norm), qup, row1(l0_kv_norm), kvup, l0_conv_w, row1(l0_conv_b))
    skip = row1(l0_hf_skip)
    w1p = jnp.pad(l0_hf_w1, ((0, LANE - l0_hf_w1.shape[0]), (0, 0)))
    two = lambda v: jnp.tile(row1(v), (1, 2))
    filt_w = (_block_diag2(w1p), two(l0_hf_b1), two(l0_hf_freq), _block_diag2(l0_hf_w2), two(l0_hf_b2),
              _block_diag2(l0_hf_w3).astype(BF16))
    wo0 = l0_w_out.astype(BF16)

    xc = x_prompt.reshape(nbc * lc, d)
    xs = x_sample.reshape(nbs * ls, d)
    groups = (
        dict(x=xc, nb=nbc, L=lc, mod_base=0, tiles_per_mod=nbc * lc // tm, dft_nb=8, tq=lc, hps=MLA_HEADS),
        dict(x=xs, nb=nbs, L=ls, mod_base=1, tiles_per_mod=ls // tm, dft_nb=nbs, tq=1024, hps=1),
    )
    ones_tab = (jnp.ones((FRONT_TILE, LANE), F32), jnp.zeros((FRONT_TILE, LANE), F32))

    outs = []
    ctx_ckv = ctx_krope = None
    for gi, g in enumerate(groups):
        nb, L = g["nb"], g["L"]
        latent = gi == 1
        cos, sin = _rope_tables(L) if latent else ones_tab
        two_stage = L == FFT_R * FFT_R
        u, x0, q, k, vt, kvn, kpe = _front(g["x"], mods0, g["mod_base"], g["tiles_per_mod"] * tm, front_w,
                                           cos, sin, latent, L, min(L, FRONT_TILE))
        if latent:
            extra = _cache_kv(cache_l0_ckv.reshape(nbs * past, KV_LORA),
                              _pad_cols(cache_l0_krope.reshape(nbs * past, QK_ROPE), LANE), kvup)
        else:
            extra = None
            ctx_ckv = kvn.reshape(nb, L, KV_LORA)
            ctx_krope = kpe.reshape(nb, L, QK_ROPE)
        y_mla = _attention(q, k, vt, extra, nb, L, g["tq"], g["hps"])

        hfilt, hnorm = _filters(L, *filt_w)
        if two_stage:
            y_hy = _hyena_long(u, x0, skip, hfilt, hnorm, nb)
        else:
            kre, kim = _dft("filt", "hy_fwd", [hfilt], [hnorm], 2)
            sh = (nb, L, HY_CH)
            yre, yim = _dft("fwdk", "hy_fwd", [u.reshape(sh)], [kre, kim], g["dft_nb"])
            y_hy = _dft("inv", "hy_inv", [yre, yim], [u.reshape(sh), x0.reshape(sh), skip], g["dft_nb"])
            y_hy = y_hy.reshape(nb * L, HY_CH)

        x1 = _post(g["x"], mods0, g["mod_base"], g["tiles_per_mod"], [y_hy, y_mla],
                   [wo0[:HY_CH], wo0[HY_CH:]], row1(l0_ln1_g), row1(l0_ln1_b),
                   l0_mlp_w1.astype(BF16), l0_mlp_w2.astype(BF16), row1(l0_ln2_g), row1(l0_ln2_b))

        if L == FFT_R * FFT_R:
            yf = _fnet_long(x1, mods1, g["mod_base"], nb)
        else:
            zc, zs = _fnet_front(x1, mods1, g["mod_base"], g["tiles_per_mod"])
            sh = (nb, L, d)
            yf = _dft("fnet", "fnet", [zc.reshape(sh), zs.reshape(sh)], [], g["dft_nb"] // 2)
            yf = yf.reshape(nb * L, d)
        x2 = _post(x1, mods1, g["mod_base"], g["tiles_per_mod"], [yf],
                   [l1_w_out.astype(BF16)], row1(l1_ln1_g), row1(l1_ln1_b),
                   l1_mlp_w1.astype(BF16), l1_mlp_w2.astype(BF16), row1(l1_ln2_g), row1(l1_ln2_b))
        outs.append(x2.reshape(nb, L, d))

    return (outs[0], outs[1], ctx_ckv, ctx_krope)
```

```python
import functools
import math

import numpy as np
import jax
import jax.numpy as jnp
from jax import lax
from jax.experimental import pallas as pl
from jax.experimental.pallas import tpu as pltpu

F32 = jnp.float32
BF16 = jnp.bfloat16
HI = lax.Precision.HIGHEST

D_MODEL = 1024
DEPTH = 2
GRID_W = 64
HY_CH = 512
FILT_BANDS = 16
FILT_ORDER = 64
FAST_DECAY_PCT = 0.3
SLOW_DECAY_PCT = 1.5
DECAY_TARGET = 1e-2
MAX_DECAY = math.log(DECAY_TARGET) / FAST_DECAY_PCT
MIN_DECAY = math.log(DECAY_TARGET) / SLOW_DECAY_PCT
MLA_HEADS = 4
QK_NOPE = 128
QK_ROPE = 64
V_HEAD = 128
Q_LORA = 256
KV_LORA = 128
ROPE_THETA = 10000.0
FN_GROUP_CH = 128
D_FF = 4096
ALPHA = (2 * DEPTH) ** 0.25
LN_EPS = 1e-5
RMS_EPS = 1e-6

LANE = 128
ROW_TILE = 256
FRONT_TILE = 1024
POST_TILE = 512
POST_SPLIT = 2
QK_PAD = 256
VT_ROWS = V_HEAD + 16
LOG2E = 1.4426950408889634
VMEM_LIMIT = 56 * 1024 * 1024


def _cparams(sem):
    return pltpu.CompilerParams(dimension_semantics=sem, vmem_limit_bytes=VMEM_LIMIT)


def _ln_plain(x):
    mu = jnp.mean(x, axis=-1, keepdims=True)
    xc = x - mu
    var = jnp.mean(xc * xc, axis=-1, keepdims=True)
    return xc * lax.rsqrt(var + LN_EPS)


def _rms(x, g):
    return x * lax.rsqrt(jnp.mean(x * x, axis=-1, keepdims=True) + RMS_EPS) * g


def _bdot(a, b):
    return jnp.dot(a.astype(BF16), b, preferred_element_type=F32)


def _vt_rows(v):
    ones = jnp.ones((VT_ROWS - V_HEAD, v.shape[0]), BF16)
    return jnp.concatenate([jnp.transpose(v).astype(BF16), ones], axis=0)


def _mod_kernel(c_ref, w_ref, b_ref, o_ref):
    c = c_ref[...]
    s = c / (1.0 + jnp.exp(-c))
    s_hi = s.astype(BF16)
    s_lo = (s - s_hi.astype(F32)).astype(BF16)
    s2 = jnp.concatenate([s_hi, s_lo], axis=0)
    nr = s.shape[0]
    w = w_ref[...]
    w_hi = w.astype(BF16)
    w_lo = (w - w_hi.astype(F32)).astype(BF16)
    r1 = jnp.dot(s2, w_hi, preferred_element_type=F32)
    r2 = jnp.dot(s_hi, w_lo, preferred_element_type=F32)
    o_ref[...] = r1[:nr] + r1[nr:] + r2 + b_ref[...]


def _modulation(cond8, w, b):
    n = w.shape[1]
    tn = 768
    out = pl.pallas_call(
        _mod_kernel,
        out_shape=jax.ShapeDtypeStruct((8, n), F32),
        grid=(n // tn,),
        in_specs=[pl.BlockSpec((8, D_MODEL), lambda j: (0, 0)),
                  pl.BlockSpec((D_MODEL, tn), lambda j: (0, j)),
                  pl.BlockSpec((1, tn), lambda j: (0, j))],
        out_specs=pl.BlockSpec((8, tn), lambda j: (0, j)),
        compiler_params=_cparams(("arbitrary",)),
        name="modulation",
    )(cond8, w, b.reshape(1, n))
    return out.reshape(8, 1, n)


def _mod_spec(mod_base, tiles_per_mod):
    return pl.BlockSpec((None, 1, 6 * D_MODEL), lambda i: (mod_base + i // tiles_per_mod, 0, 0))


def _const_spec(shape):
    nd = len(shape)
    return pl.BlockSpec(shape, lambda i: (0,) * nd)


HALO = 8


def _front_kernel(tiles_per_seq, x_ref, xp_ref, xn_ref, m_ref, win_ref, qn_ref, qup_ref, kvn_ref, kvup_ref,
                  cos_ref, sin_ref, cw_ref, cb_ref,
                  u_ref, x0_ref, q_ref, k_ref, v_ref, kvn_out_ref, kpe_ref):
    i = pl.program_id(0)
    m = m_ref[...]
    tm = x_ref.shape[0]
    nh = 3 * HY_CH
    xe = jnp.concatenate([xp_ref[...], x_ref[...], xn_ref[...]], axis=0)
    he = _ln_plain(xe) * (1.0 + m[:, D_MODEL:2 * D_MODEL]) + m[:, 0:D_MODEL]

    z = _bdot(he[HALO:HALO + tm], win_ref[:, nh:])
    zh = _bdot(he, win_ref[:, :nh])
    q_c = z[:, 0:256]
    kv_c = z[:, 256:384]
    cos = cos_ref[...]
    sin = sin_ref[...]
    kpe = z[:, 384:512] * cos + z[:, 512:640] * sin
    kpe_ref[...] = kpe[:, :QK_ROPE]
    kpe_b = kpe.astype(BF16)
    q = _bdot(_rms(q_c, qn_ref[...]), qup_ref[...]) * (LOG2E / math.sqrt(QK_NOPE + QK_ROPE))
    kvn = _rms(kv_c, kvn_ref[...])
    kvn_out_ref[...] = kvn
    kv = _bdot(kvn, kvup_ref[...])

    pos = i % tiles_per_seq
    rows = lax.broadcasted_iota(jnp.int32, (tm + 2 * HALO, 1), 0)
    inside = jnp.logical_and(jnp.logical_or(rows >= HALO, pos != 0),
                             jnp.logical_or(rows < tm + HALO, pos != tiles_per_seq - 1))
    zh = jnp.where(inside, zh, 0.0)
    cw = cw_ref[...]
    pz = (pltpu.roll(zh, 1, 0) * cw[0:1, :] + zh * cw[1:2, :]
          + pltpu.roll(zh, tm + 2 * HALO - 1, 0) * cw[2:3, :])[HALO:HALO + tm] + cb_ref[...]
    u_ref[...] = (pz[:, 2 * HY_CH:] * pz[:, HY_CH:2 * HY_CH]).astype(u_ref.dtype)
    x0_ref[...] = pz[:, :HY_CH].astype(x0_ref.dtype)

    for hd in range(MLA_HEADS):
        a = hd * LANE
        q_pe = (q[:, 512 + a:512 + a + LANE] * cos + q[:, 1024 + a:1024 + a + LANE] * sin).astype(BF16)
        q_ref[hd] = jnp.concatenate([q[:, a:a + LANE].astype(BF16), q_pe], axis=-1)
        k_ref[hd] = jnp.concatenate([kv[:, 2 * a:2 * a + LANE].astype(BF16), kpe_b], axis=-1)
        v_ref[hd] = _vt_rows(kv[:, 2 * a + LANE:2 * a + 2 * LANE])


def _front(x, mods, mod_base, rows_per_mod, w, cos, sin, rope, seq_len, tm):
    t = x.shape[0]
    tiles_per_seq = seq_len // tm
    tiles_per_mod = rows_per_mod // tm
    win, qn, qup, kvn, kvup, conv_w, conv_b = w
    if rope:
        tab_spec = pl.BlockSpec((tm, LANE), lambda i: (i % tiles_per_seq, 0))
    else:
        tab_spec = pl.BlockSpec((tm, LANE), lambda i: (0, 0))
    r8 = tm // HALO
    n8 = t // HALO
    hy_out = lambda dt: jax.ShapeDtypeStruct((t, HY_CH), dt)
    hy_spec = pl.BlockSpec((tm, HY_CH), lambda i: (i, 0))
    return pl.pallas_call(
        functools.partial(_front_kernel, tiles_per_seq),
        out_shape=(hy_out(BF16), hy_out(BF16),
                   jax.ShapeDtypeStruct((MLA_HEADS, t, QK_PAD), BF16),
                   jax.ShapeDtypeStruct((MLA_HEADS, t, QK_PAD), BF16),
                   jax.ShapeDtypeStruct((MLA_HEADS, VT_ROWS, t), BF16),
                   jax.ShapeDtypeStruct((t, KV_LORA), F32),
                   jax.ShapeDtypeStruct((t, QK_ROPE), F32)),
        grid=(t // tm,),
        in_specs=[pl.BlockSpec((tm, D_MODEL), lambda i: (i, 0)),
                  pl.BlockSpec((HALO, D_MODEL), lambda i: (jnp.maximum(i * r8 - 1, 0), 0)),
                  pl.BlockSpec((HALO, D_MODEL), lambda i: (jnp.minimum((i + 1) * r8, n8 - 1), 0)),
                  _mod_spec(mod_base, tiles_per_mod),
                  _const_spec(win.shape), _const_spec(qn.shape), _const_spec(qup.shape),
                  _const_spec(kvn.shape), _const_spec(kvup.shape),
                  tab_spec, tab_spec,
                  _const_spec(conv_w.shape), _const_spec(conv_b.shape)],
        out_specs=(hy_spec, hy_spec,
                   pl.BlockSpec((MLA_HEADS, tm, QK_PAD), lambda i: (0, i, 0)),
                   pl.BlockSpec((MLA_HEADS, tm, QK_PAD), lambda i: (0, i, 0)),
                   pl.BlockSpec((MLA_HEADS, VT_ROWS, tm), lambda i: (0, 0, i)),
                   pl.BlockSpec((tm, KV_LORA), lambda i: (i, 0)),
                   pl.BlockSpec((tm, QK_ROPE), lambda i: (i, 0))),
        compiler_params=_cparams(("arbitrary",)),
        name="l0_front",
    )(x, x, x, mods, win, qn, qup, kvn, kvup, cos, sin, conv_w, conv_b)


def _cache_kv_kernel(ckv_ref, kr_ref, kvup_ref, k_ref, v_ref):
    kv = _bdot(ckv_ref[...], kvup_ref[...])
    kr = kr_ref[...].astype(BF16)
    for hd in range(MLA_HEADS):
        a = 2 * hd * LANE
        k_ref[hd] = jnp.concatenate([kv[:, a:a + LANE].astype(BF16), kr], axis=-1)
        v_ref[hd] = _vt_rows(kv[:, a + LANE:a + 2 * LANE])


def _cache_kv(ckv, krope_pad, kvup):
    t = ckv.shape[0]
    return pl.pallas_call(
        _cache_kv_kernel,
        out_shape=(jax.ShapeDtypeStruct((MLA_HEADS, t, QK_PAD), BF16),
                   jax.ShapeDtypeStruct((MLA_HEADS, VT_ROWS, t), BF16)),
        name="l0_cache_kv",
    )(ckv, krope_pad, kvup)


def _col_reduce(x, op):
    rows, n = x.shape
    for g in (32, 8):
        if rows % (8 * g) == 0 and rows > 8 * g:
            x = op(x.reshape(rows // (8 * g), 8 * g, n), axis=0)
            rows = 8 * g
    return op(x, axis=0, keepdims=True)


PV_CHUNK = 128


def _attn_kernel(n_kv, n_cast, hps, q_ref, *refs):
    k_refs, vt_refs = refs[:n_kv], refs[n_kv:2 * n_kv]
    w_refs = refs[2 * n_kv:2 * n_kv + n_cast]
    o_ref = refs[2 * n_kv + n_cast]
    wo_refs = refs[2 * n_kv + n_cast + 1:2 * n_kv + 2 * n_cast + 1]
    s_even, s_odd, m_even, m_odd = refs[2 * n_kv + 2 * n_cast + 1:]
    i = pl.program_id(0)

    for w_ref, wo_ref in zip(w_refs, wo_refs):
        wo_ref[...] = w_ref[...].astype(wo_ref.dtype)

    @pl.when(i == 0)
    def _():
        s_odd[...] = jnp.zeros_like(s_odd)
        m_odd[...] = jnp.zeros_like(m_odd)

    def step(s_write, m_write, s_read, m_read):
        nt = (((1,), (1,)), ((), ()))
        for h in range(hps):
            q = q_ref[h]
            r0 = 0
            m = None
            for k_ref in k_refs:
                lk = k_ref.shape[1]
                sblk = lax.dot_general(k_ref[h], q, nt, preferred_element_type=F32)
                s_write[h, r0:r0 + lk, :] = sblk
                mc = _col_reduce(sblk, jnp.max)
                m = mc if m is None else jnp.maximum(m, mc)
                r0 += lk
            m_write[h] = m
        for h in range(hps):
            m = m_read[h]
            acc = None
            r0 = 0
            for vt_ref in vt_refs:
                lk = vt_ref.shape[2]
                for c0 in range(0, lk, PV_CHUNK):
                    c1 = min(c0 + PV_CHUNK, lk)
                    pb = jnp.exp2(s_read[h, r0 + c0:r0 + c1, :] - m).astype(BF16)
                    pv = jnp.dot(vt_ref[h, :, c0:c1], pb, preferred_element_type=F32)
                    acc = pv if acc is None else acc + pv
                r0 += lk
            o_ref[:, h * V_HEAD:(h + 1) * V_HEAD] = jnp.transpose(
                acc[:V_HEAD] / acc[V_HEAD:V_HEAD + 1]).astype(o_ref.dtype)

    pl.when(i % 2 == 0)(lambda: step(s_even, m_even, s_odd, m_odd))
    pl.when(i % 2 == 1)(lambda: step(s_odd, m_odd, s_even, m_even))


def _attention(q, k, vt, extra, nb, lq, tq, hps, cast=()):
    nq = lq // tq
    ng = MLA_HEADS // hps
    n_tiles = nb * ng * nq

    def where(t):
        bh = t // nq
        return bh // ng, bh % ng, t % nq

    def score_side(fn):
        return lambda i: fn(*where(jnp.minimum(i, n_tiles - 1)))

    def value_side(fn):
        return lambda i: fn(*where(jnp.maximum(i - 1, 0)))

    ks, vts = [k], [vt]
    if extra is not None:
        ks.append(extra[0])
        vts.append(extra[1])
    in_specs = [pl.BlockSpec((hps, tq, QK_PAD), score_side(lambda b, h, j: (h, b * nq + j, 0)))]
    in_specs += [pl.BlockSpec((hps, a.shape[1] // nb, QK_PAD), score_side(lambda b, h, j: (h, b, 0))) for a in ks]
    in_specs += [pl.BlockSpec((hps, VT_ROWS, a.shape[2] // nb), value_side(lambda b, h, j: (h, 0, b))) for a in vts]
    lk_total = sum(a.shape[1] // nb for a in ks)
    cast_specs = [pl.BlockSpec((w.shape[0] // n_tiles, w.shape[1]), lambda i: (jnp.minimum(i, n_tiles - 1), 0))
                  for w in cast]
    in_specs += cast_specs
    res = pl.pallas_call(
        functools.partial(_attn_kernel, len(ks), len(cast), hps),
        out_shape=(jax.ShapeDtypeStruct((nb * lq, MLA_HEADS * V_HEAD), BF16),
                   *[jax.ShapeDtypeStruct(w.shape, BF16) for w in cast]),
        grid=(n_tiles + 1,),
        in_specs=in_specs,
        out_specs=(pl.BlockSpec((tq, hps * V_HEAD), value_side(lambda b, h, j: (b * nq + j, h))), *cast_specs),
        scratch_shapes=[pltpu.VMEM((hps, lk_total, tq), F32), pltpu.VMEM((hps, lk_total, tq), F32),
                        pltpu.VMEM((hps, 1, tq), F32), pltpu.VMEM((hps, 1, tq), F32)],
        compiler_params=_cparams(("arbitrary",)),
        name="l0_attention",
    )(q, *ks, *vts, *cast)
    return res[0], res[1:]


def _filter_kernel(z_ref, w1_ref, b1_ref, fr_ref, w2_ref, b2_ref, w3_ref, dl_ref, h_ref, norm_ref):
    i = pl.program_id(0)
    z = z_ref[...]
    tl = z.shape[0]
    fr = fr_ref[...]
    z2 = jnp.concatenate([z[:tl // 2], z[tl // 2:]], axis=1)
    h = jnp.sin(fr * (jnp.dot(z2, w1_ref[...], precision=HI, preferred_element_type=F32) + b1_ref[...]))
    h = jnp.sin(fr * (jnp.dot(h, w2_ref[...], precision=HI, preferred_element_type=F32) + b2_ref[...]))
    h = _bdot(h, w3_ref[...])
    h = jnp.concatenate([h[:, :2 * HY_CH], h[:, 2 * HY_CH:]], axis=0)
    decay = jnp.exp(-(z[:, 0:1] * dl_ref[...]))
    hf = h[:, :HY_CH] * decay
    hb = h[:, HY_CH:] * decay
    part = jnp.sum(jnp.abs(hf) + jnp.abs(hb), axis=0, keepdims=True)

    @pl.when(i == 0)
    def _():
        norm_ref[...] = part

    @pl.when(i > 0)
    def _():
        norm_ref[...] += part

    rows = lax.broadcasted_iota(jnp.int32, hb.shape, 0) + i * tl
    h_ref[0] = hf.astype(h_ref.dtype)
    h_ref[1] = jnp.where(rows == 0, 0.0, hb).astype(h_ref.dtype)


def _filter_embedding(L):
    t = np.linspace(0.0, 1.0, L)[:, None]
    w_ang = 2.0 * np.pi * np.arange(L) / L
    bands = np.linspace(1e-4, FILT_BANDS - 1, FILT_BANDS)
    ang = w_ang[:, None] * bands[None, :]
    z = np.zeros((L, LANE), np.float64)
    z[:, 0:1] = t
    z[:, 1:1 + FILT_BANDS] = np.cos(ang)
    z[:, 1 + FILT_BANDS:1 + 2 * FILT_BANDS] = -np.sin(ang)
    return jnp.asarray(z, F32)


def _filters(L, w1p, b1, fr, w2, b2, w3):
    tl = min(L, 1024)
    z = _filter_embedding(L)
    deltas = jnp.asarray(np.abs(np.linspace(MIN_DECAY, MAX_DECAY, HY_CH))[None, :], F32)
    return pl.pallas_call(
        _filter_kernel,
        out_shape=(jax.ShapeDtypeStruct((2, L, HY_CH), BF16), jax.ShapeDtypeStruct((1, HY_CH), F32)),
        grid=(L // tl,),
        in_specs=[pl.BlockSpec((tl, LANE), lambda i: (i, 0)),
                  _const_spec(w1p.shape), _const_spec(b1.shape), _const_spec(fr.shape),
                  _const_spec(w2.shape), _const_spec(b2.shape), _const_spec(w3.shape),
                  _const_spec(deltas.shape)],
        out_specs=(pl.BlockSpec((2, tl, HY_CH), lambda i: (0, i, 0)),
                   pl.BlockSpec((1, HY_CH), lambda i: (0, 0))),
        compiler_params=_cparams(("arbitrary",)),
        name="l0_hyena_filters",
    )(z, w1p, b1, fr, w2, b2, w3, deltas)


def _dft_tables(kind, L, ti):
    ni = L // ti
    i = np.arange(ti, dtype=np.int64)[:, None]
    big = (np.arange(ni, dtype=np.int64) * ti)[:, None]
    c = np.arange(L, dtype=np.int64)[None, :]
    if kind == "hy_fwd":
        period = 4 * L
        base_idx = (2 * i + 1) * c
        r_idx = 2 * big * c
        scale = 1.0
    elif kind == "hy_inv":
        period = 4 * L
        base_idx = (2 * c + 1) * i
        r_idx = (2 * c + 1) * big
        scale = 1.0 / L
    else:
        period = L
        base_idx = i * c
        r_idx = big * c
        scale = 1.0 / math.sqrt(L * FN_GROUP_CH)
    ab = 2.0 * np.pi * (base_idx % period) / period
    ar = 2.0 * np.pi * (r_idx % period) / period
    return (jnp.asarray(np.cos(ab), F32), jnp.asarray(np.sin(ab), F32),
            jnp.asarray(scale * np.cos(ar), F32).reshape(ni, 1, L),
            jnp.asarray(scale * np.sin(ar), F32).reshape(ni, 1, L))


def _dft_kernel(mode, nb, n_x, *refs):
    bc_ref, bs_ref, rc_ref, rs_ref = refs[:4]
    x_refs = refs[4:4 + n_x]
    rest = refs[4 + n_x:]
    p_ref, q_ref = rest[-2], rest[-1]
    j = pl.program_id(2)
    nj = pl.num_programs(2)
    tj = x_refs[0].shape[1]
    if bc_ref.shape[1] == tj:
        bc, bs, rc, rs = bc_ref[...], bs_ref[...], rc_ref[...], rs_ref[...]
    else:
        off = pl.multiple_of(j * tj, tj)
        bc, bs = bc_ref[:, pl.ds(off, tj)], bs_ref[:, pl.ds(off, tj)]
        rc, rs = rc_ref[:, pl.ds(off, tj)], rs_ref[:, pl.ds(off, tj)]
    tc = (bc * rc - bs * rs).astype(BF16)
    ts = (bs * rc + bc * rs).astype(BF16)
    x1_ref = x_refs[0]
    x2_ref = x_refs[-1]

    pq = [(jnp.dot(tc, x1_ref[b].astype(BF16), preferred_element_type=F32),
           jnp.dot(ts, x2_ref[b].astype(BF16), preferred_element_type=F32)) for b in range(nb)]

    @pl.when(j == 0)
    def _():
        for b in range(nb):
            p_ref[b] = pq[b][0]
            q_ref[b] = pq[b][1]

    @pl.when(j > 0)
    def _():
        for b in range(nb):
            p_ref[b] += pq[b][0]
            q_ref[b] += pq[b][1]

    @pl.when(j == nj - 1)
    def _():
        if mode == "filt":
            nrm = rest[0][...]
            kre_ref, kim_ref = rest[1], rest[2]
            kre_ref[...] = (p_ref[0] + p_ref[1]) / nrm
            kim_ref[...] = (q_ref[1] - q_ref[0]) / nrm
        elif mode == "fwdk":
            kre, kim = rest[0][...], rest[1][...]
            yre_ref, yim_ref = rest[2], rest[3]
            for b in range(nb):
                pp, qq = p_ref[b], q_ref[b]
                yre_ref[b] = (pp * kre + qq * kim).astype(BF16)
                yim_ref[b] = (pp * kim - qq * kre).astype(BF16)
        elif mode == "inv":
            u_ref, x0_ref, skip_ref, o_ref = rest[0], rest[1], rest[2], rest[3]
            skip = skip_ref[...]
            for b in range(nb):
                o_ref[b] = ((p_ref[b] - q_ref[b] + u_ref[b] * skip) * x0_ref[b].astype(F32)).astype(BF16)
        else:
            o_ref = rest[0]
            for b in range(nb):
                o_ref[b] = (p_ref[b] - q_ref[b]).astype(BF16)


def _dft(mode, kind, xs, extras, nb):
    B, L, C = xs[0].shape
    ti = min(L, 256)
    tj = min(L, 512)
    bc, bs, rc, rs = _dft_tables(kind, L, ti)
    grid = (B // nb, L // ti, L // tj)
    x_spec = pl.BlockSpec((nb, tj, C), lambda g, i, j: (g, j, 0))
    row_spec = lambda c, dt=None: pl.BlockSpec((nb, ti, c), lambda g, i, j: (g, i, 0))
    in_specs = [pl.BlockSpec((ti, L), lambda g, i, j: (0, 0)),
                pl.BlockSpec((ti, L), lambda g, i, j: (0, 0)),
                pl.BlockSpec((None, 1, L), lambda g, i, j: (i, 0, 0)),
                pl.BlockSpec((None, 1, L), lambda g, i, j: (i, 0, 0))] + [x_spec] * len(xs)
    if mode == "filt":
        in_specs += [pl.BlockSpec((1, HY_CH), lambda g, i, j: (0, 0))]
        out_shape = (jax.ShapeDtypeStruct((L, HY_CH), F32),) * 2
        out_specs = (pl.BlockSpec((ti, HY_CH), lambda g, i, j: (i, 0)),) * 2
    elif mode == "fwdk":
        in_specs += [pl.BlockSpec((ti, HY_CH), lambda g, i, j: (i, 0))] * 2
        out_shape = (jax.ShapeDtypeStruct((B, L, C), BF16),) * 2
        out_specs = (row_spec(C),) * 2
    elif mode == "inv":
        in_specs += [row_spec(C)] * 2 + [pl.BlockSpec((1, C), lambda g, i, j: (0, 0))]
        out_shape = jax.ShapeDtypeStruct((B, L, C), BF16)
        out_specs = row_spec(C)
    else:
        out_shape = jax.ShapeDtypeStruct((B, L, C), BF16)
        out_specs = row_spec(C)
    return pl.pallas_call(
        functools.partial(_dft_kernel, mode, nb, len(xs)),
        out_shape=out_shape,
        grid=grid,
        in_specs=in_specs,
        out_specs=out_specs,
        scratch_shapes=[pltpu.VMEM((nb, ti, C), F32), pltpu.VMEM((nb, ti, C), F32)],
        compiler_params=_cparams(("arbitrary", "arbitrary", "arbitrary")),
        name="dft_" + mode,
    )(bc, bs, rc, rs, *xs, *extras)


FFT_R = 64
HYENA_KF = 16
FNET_KF = 16


def _pack_pairs(x):
    return pltpu.bitcast(x.astype(BF16), jnp.uint32)


def _unpack_pairs(w):
    return pltpu.bitcast(w, BF16)


def _to_blocks(w):
    return jnp.swapaxes(w.reshape(FFT_R, w.shape[-2], w.shape[-1]), 0, 1)


def _from_blocks(ws):
    kf, c = len(ws), ws[0].shape[-1]
    return jnp.swapaxes(jnp.stack(ws, axis=0), 0, 1).reshape(FFT_R // kf, kf, kf, c)


def _lead_in_kernel(g_ref, x_ref, o_ref):
    g = g_ref[...]
    xt = jnp.swapaxes(x_ref[...], 0, 1)
    for j in range(x_ref.shape[1]):
        o_ref[j] = _pack_pairs(jnp.dot(g, xt[j].astype(BF16), preferred_element_type=F32))


def _lead_in(g, x, kf, name):
    nbx, _, _, c = x.shape
    m2 = g.shape[0] // 2
    return pl.pallas_call(
        _lead_in_kernel,
        out_shape=jax.ShapeDtypeStruct((nbx, FFT_R // kf, kf, m2, c), jnp.uint32),
        grid=(nbx, FFT_R // kf),
        in_specs=[pl.BlockSpec(g.shape, lambda b, k: (0, 0)),
                  pl.BlockSpec((None, FFT_R, kf, c), lambda b, k: (b, 0, k, 0))],
        out_specs=pl.BlockSpec((None, None, kf, m2, c), lambda b, k: (b, k, 0, 0, 0)),
        compiler_params=_cparams(("arbitrary", "arbitrary")),
        name=name,
    )(g, x)


def _lead_out_kernel(n_extra, g_ref, w_ref, *rest):
    g = g_ref[...]
    o_ref = rest[-1]
    ys = [jnp.dot(g, _unpack_pairs(w_ref[j]), preferred_element_type=F32) for j in range(w_ref.shape[0])]
    y = jnp.swapaxes(jnp.stack(ys, axis=0), 0, 1)
    if n_extra:
        y = (y + rest[0][...] * rest[2][...]) * rest[1][...].astype(F32)
    o_ref[...] = y.astype(o_ref.dtype)


def _lead_out(g, w, epilogue, name):
    nb, nk, kf, k2, c = w.shape
    blk = pl.BlockSpec((None, FFT_R, kf, c), lambda b, k: (b, 0, k, 0))
    extra_specs = [blk, blk, pl.BlockSpec((1, c), lambda b, k: (0, 0))] if epilogue else []
    return pl.pallas_call(
        functools.partial(_lead_out_kernel, len(epilogue)),
        out_shape=jax.ShapeDtypeStruct((nb, FFT_R, FFT_R, c), BF16),
        grid=(nb, nk),
        in_specs=[pl.BlockSpec(g.shape, lambda b, k: (0, 0)),
                  pl.BlockSpec((None, None, kf, k2, c), lambda b, k: (b, k, 0, 0, 0))] + extra_specs,
        out_specs=blk,
        compiler_params=_cparams(("arbitrary", "arbitrary")),
        name=name,
    )(g, w, *epilogue)


def _interleave(a, b, axis):
    st = np.stack([a, b], axis=axis + 1)
    shape = list(a.shape)
    shape[axis] *= 2
    return st.reshape(shape)


def _hy2_tables():
    L = FFT_R * FFT_R
    n2 = 2 * L
    f1 = np.arange(2 * FFT_R, dtype=np.int64)
    s1 = np.arange(FFT_R, dtype=np.int64)
    th = np.pi * (((2 * f1[:, None] + 1) * s1[None, :]) % (4 * FFT_R)) / (2 * FFT_R)
    ga = _interleave(np.cos(th), -np.sin(th), 0)
    ma = _interleave(np.cos(th).T, -np.sin(th).T, 1) / L
    f2 = np.arange(FFT_R // 2, dtype=np.int64)
    s2 = np.arange(FFT_R, dtype=np.int64)
    idx = ((n2 // FFT_R) * 2 * f2[None, :, None] * s2[None, None, :]
           + (2 * f1[:, None, None] + 1) * s2[None, None, :]) % (2 * n2)
    al = np.pi * idx / n2
    c, s = np.cos(al), np.sin(al)
    nmat = np.concatenate([_interleave(c, s, 2), _interleave(-s, c, 2)], axis=1)
    ct, st = np.transpose(c, (0, 2, 1)), np.transpose(s, (0, 2, 1))
    mmat = _interleave(np.concatenate([ct, -st], axis=2), np.concatenate([st, ct], axis=2), 1)
    bf = lambda a: jnp.asarray(a, F32).astype(BF16)
    return bf(ga), bf(nmat), bf(mmat), bf(ma)


def _hy_mid_kernel(a_ref, n_ref, m_ref, k_ref, e_ref):
    half = FFT_R // 2
    kf = a_ref.shape[-2]
    a = _to_blocks(a_ref[...])
    ts = [jnp.dot(n_ref[j], _unpack_pairs(a[j]), preferred_element_type=F32)
          for j in range(kf)]
    ys = []
    for j in range(kf):
        tr, ti = ts[j][:half], ts[j][half:]
        kr, ki = k_ref[j, 0].astype(F32), k_ref[j, 1].astype(F32)
        ys.append(jnp.concatenate([tr * kr - ti * ki, tr * ki + ti * kr], axis=0).astype(BF16))
    e_ref[...] = _from_blocks([_pack_pairs(jnp.dot(m_ref[j], ys[j], preferred_element_type=F32))
                               for j in range(kf)])


def _hy_mid(a, nmat, mmat, khat):
    nb, nk, kf, _, c = a.shape
    nf1 = 2 * FFT_R
    blk = pl.BlockSpec((None, nk, kf, kf, c), lambda i, b: (b, 0, 0, i, 0))
    return pl.pallas_call(
        _hy_mid_kernel,
        out_shape=jax.ShapeDtypeStruct(a.shape, jnp.uint32),
        grid=(nf1 // kf, nb),
        in_specs=[blk,
                  pl.BlockSpec((kf, FFT_R, 2 * FFT_R), lambda i, b: (i, 0, 0)),
                  pl.BlockSpec((kf, 2 * FFT_R, FFT_R), lambda i, b: (i, 0, 0)),
                  pl.BlockSpec((kf, 2, FFT_R // 2, c), lambda i, b: (i, 0, 0, 0))],
        out_specs=blk,
        compiler_params=_cparams(("arbitrary", "arbitrary")),
        name="l0_hyena_mid",
    )(a, nmat, mmat, khat)


def _hy_kfilt_kernel(a_ref, n_ref, nrm_ref, k_ref):
    half = FFT_R // 2
    nrm = nrm_ref[...]
    af, ab = _to_blocks(a_ref[0]), _to_blocks(a_ref[1])
    for j in range(a_ref.shape[-2]):
        tf = jnp.dot(n_ref[j], _unpack_pairs(af[j]), preferred_element_type=F32)
        tb = jnp.dot(n_ref[j], _unpack_pairs(ab[j]), preferred_element_type=F32)
        k_ref[j, 0] = ((tf[:half] + tb[:half]) / nrm).astype(k_ref.dtype)
        k_ref[j, 1] = ((tf[half:] - tb[half:]) / nrm).astype(k_ref.dtype)


def _hy_kfilt(a, nmat, nrm):
    _, nk, kf, _, c = a.shape
    nf1 = 2 * FFT_R
    return pl.pallas_call(
        _hy_kfilt_kernel,
        out_shape=jax.ShapeDtypeStruct((nf1, 2, FFT_R // 2, c), BF16),
        grid=(nf1 // kf,),
        in_specs=[pl.BlockSpec((2, nk, kf, kf, c), lambda i: (0, 0, 0, i, 0)),
                  pl.BlockSpec((kf, FFT_R, 2 * FFT_R), lambda i: (i, 0, 0)),
                  _const_spec(nrm.shape)],
        out_specs=pl.BlockSpec((kf, 2, FFT_R // 2, c), lambda i: (i, 0, 0, 0)),
        compiler_params=_cparams(("arbitrary",)),
        name="l0_hyena_kfilt",
    )(a, nmat, nrm)


def _hyena_long(u, x0, skip, hfilt, nrm, nb):
    L = FFT_R * FFT_R
    c = u.shape[-1]
    v4 = lambda a, n: a.reshape(n, FFT_R, FFT_R, c)
    ga, nmat, mmat, ma = _hy2_tables()
    khat = _hy_kfilt(_lead_in(ga, v4(hfilt, 2), HYENA_KF, "l0_hyena_fwd_a"), nmat, nrm)
    ee = _hy_mid(_lead_in(ga, v4(u, nb), HYENA_KF, "l0_hyena_fwd_a"), nmat, mmat, khat)
    y = _lead_out(ma, ee, (v4(u, nb), v4(x0, nb), skip), "l0_hyena_inv_a")
    return y.reshape(nb * L, c)


def _fn2_tables():
    L = FFT_R * FFT_R
    r = np.arange(FFT_R, dtype=np.int64)
    idx = (FFT_R * r[None, :, None] * r[None, None, :] + r[None, :, None] * r[:, None, None]) % L
    gm = 2.0 * np.pi * idx / L
    c, s = np.cos(gm), np.sin(gm)
    g1 = _interleave(np.concatenate([c, -s], axis=2), np.concatenate([-s, -c], axis=2), 1)
    dl = 2.0 * np.pi * ((r[:, None] * r[None, :]) % FFT_R) / FFT_R
    g2 = _interleave(np.cos(dl), np.sin(dl), 1) / math.sqrt(L * FN_GROUP_CH)
    bf = lambda a: jnp.asarray(a, F32).astype(BF16)
    return bf(g1), bf(g2)


def _fnet_s1_kernel(x_ref, m_ref, cs_ref, g1_ref, o_ref, zc_ref, zs_ref):
    xs = jnp.swapaxes(x_ref[...], 0, 1).reshape(FNET_KF * FFT_R, D_MODEL)
    m = m_ref[...]
    h = (_ln_plain(xs) * (1.0 + m[:, D_MODEL:2 * D_MODEL]) + m[:, 0:D_MODEL]).astype(BF16)
    cs = cs_ref[...]
    for g in range(D_MODEL // FN_GROUP_CH):
        a = g * FN_GROUP_CH
        z = jnp.dot(h[:, a:a + FN_GROUP_CH], cs, preferred_element_type=F32)
        zc_ref[:, a:a + FN_GROUP_CH] = z[:, :FN_GROUP_CH].astype(BF16)
        zs_ref[:, a:a + FN_GROUP_CH] = z[:, FN_GROUP_CH:].astype(BF16)
    ws = []
    for j in range(FNET_KF):
        r0 = j * FFT_R
        s = jnp.concatenate([zc_ref[r0:r0 + FFT_R, :], zs_ref[r0:r0 + FFT_R, :]], axis=0)
        ws.append(_pack_pairs(jnp.dot(g1_ref[j], s, preferred_element_type=F32)))
    o_ref[...] = _from_blocks(ws)


def _fnet_long(x, mods, mod_base, nb):
    L = FFT_R * FFT_R
    d = D_MODEL
    g1, g2 = _fn2_tables()
    cs = _group_dft_table()
    kf = FNET_KF
    bb = pl.pallas_call(
        _fnet_s1_kernel,
        out_shape=jax.ShapeDtypeStruct((nb, FFT_R // kf, kf, FFT_R, d), jnp.uint32),
        grid=(nb, FFT_R // kf),
        in_specs=[pl.BlockSpec((None, FFT_R, kf, d), lambda b, k: (b, 0, k, 0)),
                  pl.BlockSpec((None, 1, 6 * d), lambda b, k: (mod_base + b, 0, 0)),
                  pl.BlockSpec(cs.shape, lambda b, k: (0, 0)),
                  pl.BlockSpec((kf, 2 * FFT_R, 2 * FFT_R), lambda b, k: (k, 0, 0))],
        out_specs=pl.BlockSpec((None, FFT_R // kf, kf, kf, d), lambda b, k: (b, 0, 0, k, 0)),
        scratch_shapes=[pltpu.VMEM((kf * FFT_R, d), BF16), pltpu.VMEM((kf * FFT_R, d), BF16)],
        compiler_params=_cparams(("arbitrary", "arbitrary")),
        name="l1_fnet_stage1",
    )(x.reshape(nb, FFT_R, FFT_R, d), mods, cs, g1)
    y = _lead_out(g2, bb, [], "l1_fnet_stage2")
    return y.reshape(nb * L, d)


def _group_dft_table():
    g = FN_GROUP_CH
    jk = (np.arange(g, dtype=np.int64)[:, None] * np.arange(g, dtype=np.int64)[None, :]) % g
    ang = 2.0 * np.pi * jk / g
    return jnp.asarray(np.concatenate([np.cos(ang), np.sin(ang)], axis=1), F32).astype(BF16)


def _fnet_front_kernel(x_ref, m_ref, cs_ref, zc_ref, zs_ref):
    m = m_ref[...]
    h = (_ln_plain(x_ref[...]) * (1.0 + m[:, D_MODEL:2 * D_MODEL]) + m[:, 0:D_MODEL]).astype(BF16)
    cs = cs_ref[...]
    for g in range(D_MODEL // FN_GROUP_CH):
        a = g * FN_GROUP_CH
        z = jnp.dot(h[:, a:a + FN_GROUP_CH], cs, preferred_element_type=F32)
        zc_ref[:, a:a + FN_GROUP_CH] = z[:, :FN_GROUP_CH].astype(BF16)
        zs_ref[:, a:a + FN_GROUP_CH] = z[:, FN_GROUP_CH:].astype(BF16)


def _fnet_front(x, mods, mod_base, tiles_per_mod):
    t = x.shape[0]
    tm = 2 * ROW_TILE
    tiles_per_mod = max(tiles_per_mod // 2, 1)
    cs = _group_dft_table()
    return pl.pallas_call(
        _fnet_front_kernel,
        out_shape=(jax.ShapeDtypeStruct((t, D_MODEL), BF16),) * 2,
        grid=(t // tm,),
        in_specs=[pl.BlockSpec((tm, D_MODEL), lambda i: (i, 0)),
                  _mod_spec(mod_base, tiles_per_mod),
                  _const_spec(cs.shape)],
        out_specs=(pl.BlockSpec((tm, D_MODEL), lambda i: (i, 0)),) * 2,
        compiler_params=_cparams(("arbitrary",)),
        name="l1_fnet_front",
    )(x, mods, cs)


def _post_kernel(n_a, *refs):
    x_ref, m_ref = refs[0], refs[1]
    a_refs = refs[2:2 + n_a]
    wo_refs = refs[2 + n_a:2 + 2 * n_a]
    g1_ref, b1_ref, w1_ref, w2_ref, g2_ref, b2_ref, o_ref = refs[2 + 2 * n_a:]
    m = m_ref[...]
    d = D_MODEL
    tm = x_ref.shape[0]
    halves = [(r, r + tm // POST_SPLIT) for r in range(0, tm, tm // POST_SPLIT)]
    outs = []
    for r0, r1 in halves:
        out = _bdot(a_refs[0][r0:r1, :], wo_refs[0][...])
        for a_ref, wo_ref in zip(a_refs[1:], wo_refs[1:]):
            out += _bdot(a_ref[r0:r1, :], wo_ref[...])
        outs.append(out)
    x1s, hs = [], []
    for (r0, r1), out in zip(halves, outs):
        x1 = _ln_plain(ALPHA * x_ref[r0:r1, :] + m[:, 2 * d:3 * d] * out) * g1_ref[...] + b1_ref[...]
        x1s.append(x1)
        hs.append((_ln_plain(x1) * (1.0 + m[:, 4 * d:5 * d]) + m[:, 3 * d:4 * d]).astype(BF16))
    accs = []
    n_c = D_FF // d

    def up(h, c):
        hc = jnp.maximum(jnp.dot(h, w1_ref[:, c * d:(c + 1) * d], preferred_element_type=F32), 0.0)
        return (hc * hc).astype(BF16)

    for h in hs:
        acc = None
        nxt = up(h, 0)
        for c in range(n_c):
            cur = nxt
            if c + 1 < n_c:
                nxt = up(h, c + 1)
            part = jnp.dot(cur, w2_ref[c * d:(c + 1) * d, :], preferred_element_type=F32)
            acc = part if acc is None else acc + part
        accs.append(acc)
    for (r0, r1), x1, acc in zip(halves, x1s, accs):
        o_ref[r0:r1, :] = _ln_plain(ALPHA * x1 + m[:, 5 * d:6 * d] * acc) * g2_ref[...] + b2_ref[...]


def _post(x, mods, mod_base, tiles_per_mod, a_list, wo_list, g1, b1, w1, w2, g2, b2):
    t = x.shape[0]
    tm = POST_TILE
    row = lambda c: pl.BlockSpec((tm, c), lambda i: (i, 0))
    once = lambda v: pl.BlockSpec(v.shape, lambda i: (0,) * v.ndim, pipeline_mode=pl.Buffered(1))
    in_specs = ([row(D_MODEL), _mod_spec(mod_base, tiles_per_mod * ROW_TILE // tm)]
                + [row(a.shape[1]) for a in a_list]
                + [once(w) for w in wo_list]
                + [once(v) for v in (g1, b1, w1, w2, g2, b2)])
    return pl.pallas_call(
        functools.partial(_post_kernel, len(a_list)),
        out_shape=jax.ShapeDtypeStruct((t, D_MODEL), F32),
        grid=(t // tm,),
        in_specs=in_specs,
        out_specs=row(D_MODEL),
        compiler_params=_cparams(("arbitrary",)),
        name="post_mlp",
    )(x, mods, *a_list, *wo_list, g1, b1, w1, w2, g2, b2)


def _rot_cols(w):
    parts = []
    for seg in range(2):
        o = seg * 32
        parts += [-w[:, o + 16:o + 32], w[:, o:o + 16]]
    return jnp.concatenate(parts, axis=1)


def _pad_cols(w, n):
    return jnp.pad(w, ((0, 0), (0, n - w.shape[1])))


def _block_diag2(w):
    z = jnp.zeros_like(w)
    return jnp.concatenate([jnp.concatenate([w, z], axis=1), jnp.concatenate([z, w], axis=1)], axis=0)


def _rope_tables(L):
    rows = L // GRID_W
    row = np.repeat(np.arange(rows, dtype=np.float64), GRID_W)
    col = np.tile(np.arange(GRID_W, dtype=np.float64), rows)
    half = QK_ROPE // 2
    inv = 1.0 / (ROPE_THETA ** (np.arange(0, half, 2, dtype=np.float64) / half))
    ar = row[:, None] * inv[None, :]
    ac = col[:, None] * inv[None, :]
    ang = np.concatenate([ar, ar, ac, ac], axis=1)
    cos = np.concatenate([np.cos(ang), np.ones_like(ang)], axis=1)
    sin = np.concatenate([np.sin(ang), np.zeros_like(ang)], axis=1)
    return jnp.asarray(cos, F32), jnp.asarray(sin, F32)


def kernel(x_prompt, x_sample, cache_l0_ckv, cache_l0_krope, c, c_ctx, l0_ada_w, l0_ada_b, l0_w_in, l0_conv_w, l0_conv_b, l0_hf_w1, l0_hf_b1, l0_hf_freq, l0_hf_w2, l0_hf_b2, l0_hf_w3, l0_hf_skip, l0_q_norm, l0_q_up, l0_kv_norm, l0_kv_up, l0_w_out, l0_ln1_g, l0_ln1_b, l0_mlp_w1, l0_mlp_w2, l0_ln2_g, l0_ln2_b, l1_ada_w, l1_ada_b, l1_w_out, l1_ln1_g, l1_ln1_b, l1_mlp_w1, l1_mlp_w2, l1_ln2_g, l1_ln2_b):
    nbc, lc, d = x_prompt.shape
    nbs, ls, _ = x_sample.shape
    past = cache_l0_ckv.shape[1]
    tm = ROW_TILE
    row1 = lambda v: v.reshape(1, -1)

    cond8 = jnp.concatenate([c_ctx[None, :], c, jnp.zeros((8 - 1 - nbs, d), F32)], axis=0)
    mods0 = _modulation(cond8, l0_ada_w, l0_ada_b)
    mods1 = _modulation(cond8, l1_ada_w, l1_ada_b)

    kpe_w = l0_w_in[:, 1920:1984]
    win = jnp.concatenate([l0_w_in[:, :1920], _pad_cols(kpe_w, LANE), _pad_cols(_rot_cols(kpe_w), LANE)],
                          axis=1).astype(BF16)
    dh = QK_NOPE + QK_ROPE
    q_nope = [l0_q_up[:, h * dh:h * dh + QK_NOPE] for h in range(MLA_HEADS)]
    q_pe = [l0_q_up[:, h * dh + QK_NOPE:(h + 1) * dh] for h in range(MLA_HEADS)]
    qup = jnp.concatenate(q_nope + [_pad_cols(w, LANE) for w in q_pe]
                          + [_pad_cols(_rot_cols(w), LANE) for w in q_pe], axis=1).astype(BF16)
    kvup = l0_kv_up.astype(BF16)
    front_w = (win, row1(l0_q_norm), qup, row1(l0_kv_norm), kvup, l0_conv_w, row1(l0_conv_b))
    skip = row1(l0_hf_skip)
    w1p = jnp.pad(l0_hf_w1, ((0, LANE - l0_hf_w1.shape[0]), (0, 0)))
    two = lambda v: jnp.tile(row1(v), (1, 2))
    filt_w = (_block_diag2(w1p), two(l0_hf_b1), two(l0_hf_freq), _block_diag2(l0_hf_w2), two(l0_hf_b2),
              _block_diag2(l0_hf_w3).astype(BF16))
    wo0 = l0_w_out.astype(BF16)

    xc = x_prompt.reshape(nbc * lc, d)
    xs = x_sample.reshape(nbs * ls, d)
    groups = (
        dict(x=xc, nb=nbc, L=lc, mod_base=0, tiles_per_mod=nbc * lc // tm, dft_nb=8, tq=lc, hps=MLA_HEADS),
        dict(x=xs, nb=nbs, L=ls, mod_base=1, tiles_per_mod=ls // tm, dft_nb=nbs, tq=1024, hps=1),
    )
    ones_tab = (jnp.ones((FRONT_TILE, LANE), F32), jnp.zeros((FRONT_TILE, LANE), F32))

    outs = []
    ctx_ckv = ctx_krope = None
    mixed = []
    mlp_w = None
    for gi, g in enumerate(groups):
        nb, L = g["nb"], g["L"]
        latent = gi == 1
        cos, sin = _rope_tables(L) if latent else ones_tab
        u, x0, q, k, vt, kvn, kpe = _front(g["x"], mods0, g["mod_base"], g["tiles_per_mod"] * tm, front_w,
                                           cos, sin, latent, L, min(L, FRONT_TILE))
        if latent:
            extra = _cache_kv(cache_l0_ckv.reshape(nbs * past, KV_LORA),
                              _pad_cols(cache_l0_krope.reshape(nbs * past, QK_ROPE), LANE), kvup)
            y_mla, mlp_w = _attention(q, k, vt, extra, nb, L, g["tq"], g["hps"],
                                      (l0_mlp_w1, l0_mlp_w2, l1_mlp_w1, l1_mlp_w2))
        else:
            ctx_ckv = kvn.reshape(nb, L, KV_LORA)
            ctx_krope = kpe.reshape(nb, L, QK_ROPE)
            y_mla, _ = _attention(q, k, vt, None, nb, L, g["tq"], g["hps"])
        mixed.append((u, x0, y_mla))

    for g, (u, x0, y_mla) in zip(groups, mixed):
        nb, L = g["nb"], g["L"]
        two_stage = L == FFT_R * FFT_R
        hfilt, hnorm = _filters(L, *filt_w)
        if two_stage:
            y_hy = _hyena_long(u, x0, skip, hfilt, hnorm, nb)
        else:
            kre, kim = _dft("filt", "hy_fwd", [hfilt], [hnorm], 2)
            sh = (nb, L, HY_CH)
            yre, yim = _dft("fwdk", "hy_fwd", [u.reshape(sh)], [kre, kim], g["dft_nb"])
            y_hy = _dft("inv", "hy_inv", [yre, yim], [u.reshape(sh), x0.reshape(sh), skip], g["dft_nb"])
            y_hy = y_hy.reshape(nb * L, HY_CH)

        x1 = _post(g["x"], mods0, g["mod_base"], g["tiles_per_mod"], [y_hy, y_mla],
                   [wo0[:HY_CH], wo0[HY_CH:]], row1(l0_ln1_g), row1(l0_ln1_b),
                   mlp_w[0], mlp_w[1], row1(l0_ln2_g), row1(l0_ln2_b))

        if L == FFT_R * FFT_R:
            yf = _fnet_long(x1, mods1, g["mod_base"], nb)
        else:
            zc, zs = _fnet_front(x1, mods1, g["mod_base"], g["tiles_per_mod"])
            sh = (nb, L, d)
            yf = _dft("fnet", "fnet", [zc.reshape(sh), zs.reshape(sh)], [], g["dft_nb"] // 2)
            yf = yf.reshape(nb * L, d)
        x2 = _post(x1, mods1, g["mod_base"], g["tiles_per_mod"], [yf],
                   [l1_w_out.astype(BF16)], row1(l1_ln1_g), row1(l1_ln1_b),
                   mlp_w[2], mlp_w[3], row1(l1_ln2_g), row1(l1_ln2_b))
        outs.append(x2.reshape(nb, L, d))

    return (outs[0], outs[1], ctx_ckv, ctx_krope)
```

```python
import functools
import math

import numpy as np
import jax
import jax.numpy as jnp
from jax import lax
from jax.experimental import pallas as pl
from jax.experimental.pallas import tpu as pltpu

F32 = jnp.float32
BF16 = jnp.bfloat16
HI = lax.Precision.HIGHEST

D_MODEL = 1024
DEPTH = 2
GRID_W = 64
HY_CH = 512
FILT_BANDS = 16
FILT_ORDER = 64
FAST_DECAY_PCT = 0.3
SLOW_DECAY_PCT = 1.5
DECAY_TARGET = 1e-2
MAX_DECAY = math.log(DECAY_TARGET) / FAST_DECAY_PCT
MIN_DECAY = math.log(DECAY_TARGET) / SLOW_DECAY_PCT
MLA_HEADS = 4
QK_NOPE = 128
QK_ROPE = 64
V_HEAD = 128
Q_LORA = 256
KV_LORA = 128
ROPE_THETA = 10000.0
FN_GROUP_CH = 128
D_FF = 4096
ALPHA = (2 * DEPTH) ** 0.25
LN_EPS = 1e-5
RMS_EPS = 1e-6

LANE = 128
ROW_TILE = 256
FRONT_TILE = 1024
POST_TILE = 512
POST_SPLIT = 2
QK_PAD = 256
VT_ROWS = V_HEAD + 16
LOG2E = 1.4426950408889634
VMEM_LIMIT = 56 * 1024 * 1024


def _cparams(sem):
    return pltpu.CompilerParams(dimension_semantics=sem, vmem_limit_bytes=VMEM_LIMIT)


def _ln_plain(x):
    mu = jnp.mean(x, axis=-1, keepdims=True)
    xc = x - mu
    var = jnp.mean(xc * xc, axis=-1, keepdims=True)
    return xc * lax.rsqrt(var + LN_EPS)


def _rms(x, g):
    return x * lax.rsqrt(jnp.mean(x * x, axis=-1, keepdims=True) + RMS_EPS) * g


def _bdot(a, b):
    return jnp.dot(a.astype(BF16), b, preferred_element_type=F32)


def _vt_rows(v):
    ones = jnp.ones((VT_ROWS - V_HEAD, v.shape[0]), BF16)
    return jnp.concatenate([jnp.transpose(v).astype(BF16), ones], axis=0)


def _mod_kernel(c_ref, w_ref, b_ref, o_ref):
    c = c_ref[...]
    s = c / (1.0 + jnp.exp(-c))
    s_hi = s.astype(BF16)
    s_lo = (s - s_hi.astype(F32)).astype(BF16)
    s2 = jnp.concatenate([s_hi, s_lo], axis=0)
    nr = s.shape[0]
    w = w_ref[...]
    w_hi = w.astype(BF16)
    w_lo = (w - w_hi.astype(F32)).astype(BF16)
    r1 = jnp.dot(s2, w_hi, preferred_element_type=F32)
    r2 = jnp.dot(s_hi, w_lo, preferred_element_type=F32)
    part = r1[:nr] + r1[nr:] + r2
    j = pl.program_id(0)

    @pl.when(j == 0)
    def _():
        o_ref[:, 0, :] = part + b_ref[...]

    @pl.when(j > 0)
    def _():
        o_ref[:, 0, :] += part


def _modulation(cond8, w, b):
    n = w.shape[1]
    tk = LANE
    return pl.pallas_call(
        _mod_kernel,
        out_shape=jax.ShapeDtypeStruct((8, 1, n), F32),
        grid=(D_MODEL // tk,),
        in_specs=[pl.BlockSpec((8, tk), lambda j: (0, j)),
                  pl.BlockSpec((tk, n), lambda j: (j, 0)),
                  pl.BlockSpec((1, n), lambda j: (0, 0))],
        out_specs=pl.BlockSpec((8, 1, n), lambda j: (0, 0, 0)),
        compiler_params=_cparams(("arbitrary",)),
        name="modulation",
    )(cond8, w, b.reshape(1, n))


def _mod_spec(mod_base, tiles_per_mod):
    return pl.BlockSpec((None, 1, 6 * D_MODEL), lambda i: (mod_base + i // tiles_per_mod, 0, 0))


def _const_spec(shape):
    nd = len(shape)
    return pl.BlockSpec(shape, lambda i: (0,) * nd)


HALO = 8


def _front_kernel(seq_len, x_ref, xp_ref, xn_ref, m_ref, win_ref, qn_ref, qup_ref, kvn_ref, kvup_ref,
                  cos_ref, sin_ref, cw_ref, cb_ref,
                  u_ref, x0_ref, q_ref, k_ref, v_ref, kvn_out_ref, kpe_ref):
    i = pl.program_id(0)
    m = m_ref[...]
    tm = x_ref.shape[0]
    nh = 3 * HY_CH
    xe = jnp.concatenate([xp_ref[...], x_ref[...], xn_ref[...]], axis=0)
    he = _ln_plain(xe) * (1.0 + m[:, D_MODEL:2 * D_MODEL]) + m[:, 0:D_MODEL]

    z = _bdot(he[HALO:HALO + tm], win_ref[:, nh:])
    zh = _bdot(he, win_ref[:, :nh])
    q_c = z[:, 0:256]
    kv_c = z[:, 256:384]
    cos = cos_ref[...]
    sin = sin_ref[...]
    kpe = z[:, 384:512] * cos + z[:, 512:640] * sin
    kpe_ref[...] = kpe[:, :QK_ROPE]
    kpe_b = kpe.astype(BF16)
    q = _bdot(_rms(q_c, qn_ref[...]), qup_ref[...]) * (LOG2E / math.sqrt(QK_NOPE + QK_ROPE))
    kvn = _rms(kv_c, kvn_ref[...])
    kvn_out_ref[...] = kvn
    kv = _bdot(kvn, kvup_ref[...])

    rows = lax.broadcasted_iota(jnp.int32, (tm + 2 * HALO, 1), 0)
    if seq_len >= tm:
        tiles_per_seq = seq_len // tm
        pos = i % tiles_per_seq
        inside = jnp.logical_and(jnp.logical_or(rows >= HALO, pos != 0),
                                 jnp.logical_or(rows < tm + HALO, pos != tiles_per_seq - 1))
        zh = jnp.where(inside, zh, 0.0)
        prev = pltpu.roll(zh, 1, 0)
        nxt = pltpu.roll(zh, tm + 2 * HALO - 1, 0)
    else:
        at = (rows + (seq_len - HALO)) % seq_len
        prev = jnp.where(at == 0, 0.0, pltpu.roll(zh, 1, 0))
        nxt = jnp.where(at == seq_len - 1, 0.0, pltpu.roll(zh, tm + 2 * HALO - 1, 0))
    cw = cw_ref[...]
    pz = (prev * cw[0:1, :] + zh * cw[1:2, :] + nxt * cw[2:3, :])[HALO:HALO + tm] + cb_ref[...]
    u_ref[...] = (pz[:, 2 * HY_CH:] * pz[:, HY_CH:2 * HY_CH]).astype(u_ref.dtype)
    x0_ref[...] = pz[:, :HY_CH].astype(x0_ref.dtype)

    for hd in range(MLA_HEADS):
        a = hd * LANE
        q_pe = (q[:, 512 + a:512 + a + LANE] * cos + q[:, 1024 + a:1024 + a + LANE] * sin).astype(BF16)
        q_ref[hd] = jnp.concatenate([q[:, a:a + LANE].astype(BF16), q_pe], axis=-1)
        k_ref[hd] = jnp.concatenate([kv[:, 2 * a:2 * a + LANE].astype(BF16), kpe_b], axis=-1)
        v_ref[hd] = _vt_rows(kv[:, 2 * a + LANE:2 * a + 2 * LANE])


def _front(x, mods, mod_base, rows_per_mod, w, cos, sin, rope, seq_len, tm):
    t = x.shape[0]
    tiles_per_mod = rows_per_mod // tm
    win, qn, qup, kvn, kvup, conv_w, conv_b = w
    assert seq_len % tm == 0 or (tm % seq_len == 0 and not rope)
    if rope:
        tiles_per_seq = seq_len // tm
        tab_spec = pl.BlockSpec((tm, LANE), lambda i: (i % tiles_per_seq, 0))
    else:
        tab_spec = pl.BlockSpec((tm, LANE), lambda i: (0, 0))
    r8 = tm // HALO
    n8 = t // HALO
    hy_out = lambda dt: jax.ShapeDtypeStruct((t, HY_CH), dt)
    hy_spec = pl.BlockSpec((tm, HY_CH), lambda i: (i, 0))
    return pl.pallas_call(
        functools.partial(_front_kernel, seq_len),
        out_shape=(hy_out(BF16), hy_out(BF16),
                   jax.ShapeDtypeStruct((MLA_HEADS, t, QK_PAD), BF16),
                   jax.ShapeDtypeStruct((MLA_HEADS, t, QK_PAD), BF16),
                   jax.ShapeDtypeStruct((MLA_HEADS, VT_ROWS, t), BF16),
                   jax.ShapeDtypeStruct((t, KV_LORA), F32),
                   jax.ShapeDtypeStruct((t, QK_ROPE), F32)),
        grid=(t // tm,),
        in_specs=[pl.BlockSpec((tm, D_MODEL), lambda i: (i, 0)),
                  pl.BlockSpec((HALO, D_MODEL), lambda i: (jnp.maximum(i * r8 - 1, 0), 0)),
                  pl.BlockSpec((HALO, D_MODEL), lambda i: (jnp.minimum((i + 1) * r8, n8 - 1), 0)),
                  _mod_spec(mod_base, tiles_per_mod),
                  _const_spec(win.shape), _const_spec(qn.shape), _const_spec(qup.shape),
                  _const_spec(kvn.shape), _const_spec(kvup.shape),
                  tab_spec, tab_spec,
                  _const_spec(conv_w.shape), _const_spec(conv_b.shape)],
        out_specs=(hy_spec, hy_spec,
                   pl.BlockSpec((MLA_HEADS, tm, QK_PAD), lambda i: (0, i, 0)),
                   pl.BlockSpec((MLA_HEADS, tm, QK_PAD), lambda i: (0, i, 0)),
                   pl.BlockSpec((MLA_HEADS, VT_ROWS, tm), lambda i: (0, 0, i)),
                   pl.BlockSpec((tm, KV_LORA), lambda i: (i, 0)),
                   pl.BlockSpec((tm, QK_ROPE), lambda i: (i, 0))),
        compiler_params=_cparams(("arbitrary",)),
        name="l0_front",
    )(x, x, x, mods, win, qn, qup, kvn, kvup, cos, sin, conv_w, conv_b)


def _cache_kv_kernel(ckv_ref, kr_ref, kvup_ref, k_ref, v_ref):
    kv = _bdot(ckv_ref[...], kvup_ref[...])
    kr = kr_ref[...].astype(BF16)
    for hd in range(MLA_HEADS):
        a = 2 * hd * LANE
        k_ref[hd] = jnp.concatenate([kv[:, a:a + LANE].astype(BF16), kr], axis=-1)
        v_ref[hd] = _vt_rows(kv[:, a + LANE:a + 2 * LANE])


def _cache_kv(ckv, krope_pad, kvup):
    t = ckv.shape[0]
    return pl.pallas_call(
        _cache_kv_kernel,
        out_shape=(jax.ShapeDtypeStruct((MLA_HEADS, t, QK_PAD), BF16),
                   jax.ShapeDtypeStruct((MLA_HEADS, VT_ROWS, t), BF16)),
        name="l0_cache_kv",
    )(ckv, krope_pad, kvup)


def _col_reduce(x, op):
    rows, n = x.shape
    for g in (32, 8):
        if rows % (8 * g) == 0 and rows > 8 * g:
            x = op(x.reshape(rows // (8 * g), 8 * g, n), axis=0)
            rows = 8 * g
    return op(x, axis=0, keepdims=True)


PV_CHUNK = 128


def _attn_kernel(n_kv, n_cast, hps, q_ref, *refs):
    k_refs, vt_refs = refs[:n_kv], refs[n_kv:2 * n_kv]
    w_refs = refs[2 * n_kv:2 * n_kv + n_cast]
    o_ref = refs[2 * n_kv + n_cast]
    wo_refs = refs[2 * n_kv + n_cast + 1:2 * n_kv + 2 * n_cast + 1]
    s_even, s_odd, m_even, m_odd = refs[2 * n_kv + 2 * n_cast + 1:]
    i = pl.program_id(0)

    for w_ref, wo_ref in zip(w_refs, wo_refs):
        wo_ref[...] = w_ref[...].astype(wo_ref.dtype)

    @pl.when(i == 0)
    def _():
        s_odd[...] = jnp.zeros_like(s_odd)
        m_odd[...] = jnp.zeros_like(m_odd)

    def step(s_write, m_write, s_read, m_read):
        nt = (((1,), (1,)), ((), ()))
        for h in range(hps):
            q = q_ref[h]
            r0 = 0
            m = None
            for k_ref in k_refs:
                lk = k_ref.shape[1]
                sblk = lax.dot_general(k_ref[h], q, nt, preferred_element_type=F32)
                s_write[h, r0:r0 + lk, :] = sblk
                mc = _col_reduce(sblk, jnp.max)
                m = mc if m is None else jnp.maximum(m, mc)
                r0 += lk
            m_write[h] = m
        for h in range(hps):
            m = m_read[h]
            acc = None
            r0 = 0
            for vt_ref in vt_refs:
                lk = vt_ref.shape[2]
                for c0 in range(0, lk, PV_CHUNK):
                    c1 = min(c0 + PV_CHUNK, lk)
                    pb = jnp.exp2(s_read[h, r0 + c0:r0 + c1, :] - m).astype(BF16)
                    pv = jnp.dot(vt_ref[h, :, c0:c1], pb, preferred_element_type=F32)
                    acc = pv if acc is None else acc + pv
                r0 += lk
            o_ref[:, h * V_HEAD:(h + 1) * V_HEAD] = jnp.transpose(
                acc[:V_HEAD] / acc[V_HEAD:V_HEAD + 1]).astype(o_ref.dtype)

    pl.when(i % 2 == 0)(lambda: step(s_even, m_even, s_odd, m_odd))
    pl.when(i % 2 == 1)(lambda: step(s_odd, m_odd, s_even, m_even))


def _attention(q, k, vt, extra, nb, lq, tq, hps, cast=()):
    nq = lq // tq
    ng = MLA_HEADS // hps
    n_tiles = nb * ng * nq

    def where(t):
        bh = t // nq
        return bh // ng, bh % ng, t % nq

    def score_side(fn):
        return lambda i: fn(*where(jnp.minimum(i, n_tiles - 1)))

    def value_side(fn):
        return lambda i: fn(*where(jnp.maximum(i - 1, 0)))

    ks, vts = [k], [vt]
    if extra is not None:
        ks.append(extra[0])
        vts.append(extra[1])
    in_specs = [pl.BlockSpec((hps, tq, QK_PAD), score_side(lambda b, h, j: (h, b * nq + j, 0)))]
    in_specs += [pl.BlockSpec((hps, a.shape[1] // nb, QK_PAD), score_side(lambda b, h, j: (h, b, 0))) for a in ks]
    in_specs += [pl.BlockSpec((hps, VT_ROWS, a.shape[2] // nb), value_side(lambda b, h, j: (h, 0, b))) for a in vts]
    lk_total = sum(a.shape[1] // nb for a in ks)
    cast_specs = [pl.BlockSpec((w.shape[0] // n_tiles, w.shape[1]), lambda i: (jnp.minimum(i, n_tiles - 1), 0))
                  for w in cast]
    in_specs += cast_specs
    res = pl.pallas_call(
        functools.partial(_attn_kernel, len(ks), len(cast), hps),
        out_shape=(jax.ShapeDtypeStruct((nb * lq, MLA_HEADS * V_HEAD), BF16),
                   *[jax.ShapeDtypeStruct(w.shape, BF16) for w in cast]),
        grid=(n_tiles + 1,),
        in_specs=in_specs,
        out_specs=(pl.BlockSpec((tq, hps * V_HEAD), value_side(lambda b, h, j: (b * nq + j, h))), *cast_specs),
        scratch_shapes=[pltpu.VMEM((hps, lk_total, tq), F32), pltpu.VMEM((hps, lk_total, tq), F32),
                        pltpu.VMEM((hps, 1, tq), F32), pltpu.VMEM((hps, 1, tq), F32)],
        compiler_params=_cparams(("arbitrary",)),
        name="l0_attention",
    )(q, *ks, *vts, *cast)
    return res[0], res[1:]


def _filter_kernel(z_ref, w1_ref, b1_ref, fr_ref, w2_ref, b2_ref, w3_ref, dl_ref, h_ref, norm_ref):
    i = pl.program_id(0)
    z = z_ref[...]
    tl = z.shape[0]
    fr = fr_ref[...]
    z2 = jnp.concatenate([z[:tl // 2], z[tl // 2:]], axis=1)
    h = jnp.sin(fr * (jnp.dot(z2, w1_ref[...], precision=HI, preferred_element_type=F32) + b1_ref[...]))
    h = jnp.sin(fr * (jnp.dot(h, w2_ref[...], precision=HI, preferred_element_type=F32) + b2_ref[...]))
    h = _bdot(h, w3_ref[...])
    h = jnp.concatenate([h[:, :2 * HY_CH], h[:, 2 * HY_CH:]], axis=0)
    decay = jnp.exp(-(z[:, 0:1] * dl_ref[...]))
    hf = h[:, :HY_CH] * decay
    hb = h[:, HY_CH:] * decay
    part = jnp.sum(jnp.abs(hf) + jnp.abs(hb), axis=0, keepdims=True)

    @pl.when(i == 0)
    def _():
        norm_ref[...] = part

    @pl.when(i > 0)
    def _():
        norm_ref[...] += part

    rows = lax.broadcasted_iota(jnp.int32, hb.shape, 0) + i * tl
    h_ref[0] = hf.astype(h_ref.dtype)
    h_ref[1] = jnp.where(rows == 0, 0.0, hb).astype(h_ref.dtype)


def _filter_embedding(L):
    t = np.linspace(0.0, 1.0, L)[:, None]
    w_ang = 2.0 * np.pi * np.arange(L) / L
    bands = np.linspace(1e-4, FILT_BANDS - 1, FILT_BANDS)
    ang = w_ang[:, None] * bands[None, :]
    z = np.zeros((L, LANE), np.float64)
    z[:, 0:1] = t
    z[:, 1:1 + FILT_BANDS] = np.cos(ang)
    z[:, 1 + FILT_BANDS:1 + 2 * FILT_BANDS] = -np.sin(ang)
    return jnp.asarray(z, F32)


def _filters(L, w1p, b1, fr, w2, b2, w3):
    tl = min(L, 1024)
    z = _filter_embedding(L)
    deltas = jnp.asarray(np.abs(np.linspace(MIN_DECAY, MAX_DECAY, HY_CH))[None, :], F32)
    return pl.pallas_call(
        _filter_kernel,
        out_shape=(jax.ShapeDtypeStruct((2, L, HY_CH), BF16), jax.ShapeDtypeStruct((1, HY_CH), F32)),
        grid=(L // tl,),
        in_specs=[pl.BlockSpec((tl, LANE), lambda i: (i, 0)),
                  _const_spec(w1p.shape), _const_spec(b1.shape), _const_spec(fr.shape),
                  _const_spec(w2.shape), _const_spec(b2.shape), _const_spec(w3.shape),
                  _const_spec(deltas.shape)],
        out_specs=(pl.BlockSpec((2, tl, HY_CH), lambda i: (0, i, 0)),
                   pl.BlockSpec((1, HY_CH), lambda i: (0, 0))),
        compiler_params=_cparams(("arbitrary",)),
        name="l0_hyena_filters",
    )(z, w1p, b1, fr, w2, b2, w3, deltas)


def _dft_tables(kind, L, ti):
    ni = L // ti
    i = np.arange(ti, dtype=np.int64)[:, None]
    big = (np.arange(ni, dtype=np.int64) * ti)[:, None]
    c = np.arange(L, dtype=np.int64)[None, :]
    if kind == "hy_fwd":
        period = 4 * L
        base_idx = (2 * i + 1) * c
        r_idx = 2 * big * c
        scale = 1.0
    elif kind == "hy_inv":
        period = 4 * L
        base_idx = (2 * c + 1) * i
        r_idx = (2 * c + 1) * big
        scale = 1.0 / L
    else:
        period = L
        base_idx = i * c
        r_idx = big * c
        scale = 1.0 / math.sqrt(L * FN_GROUP_CH)
    ab = 2.0 * np.pi * (base_idx % period) / period
    ar = 2.0 * np.pi * (r_idx % period) / period
    return (jnp.asarray(np.cos(ab), F32), jnp.asarray(np.sin(ab), F32),
            jnp.asarray(scale * np.cos(ar), F32).reshape(ni, 1, L),
            jnp.asarray(scale * np.sin(ar), F32).reshape(ni, 1, L))


def _dft_kernel(mode, nb, n_x, *refs):
    bc_ref, bs_ref, rc_ref, rs_ref = refs[:4]
    x_refs = refs[4:4 + n_x]
    rest = refs[4 + n_x:]
    p_ref, q_ref = rest[-2], rest[-1]
    j = pl.program_id(2)
    nj = pl.num_programs(2)
    tj = x_refs[0].shape[1]
    if bc_ref.shape[1] == tj:
        bc, bs, rc, rs = bc_ref[...], bs_ref[...], rc_ref[...], rs_ref[...]
    else:
        off = pl.multiple_of(j * tj, tj)
        bc, bs = bc_ref[:, pl.ds(off, tj)], bs_ref[:, pl.ds(off, tj)]
        rc, rs = rc_ref[:, pl.ds(off, tj)], rs_ref[:, pl.ds(off, tj)]
    tc = (bc * rc - bs * rs).astype(BF16)
    ts = (bs * rc + bc * rs).astype(BF16)
    x1_ref = x_refs[0]
    x2_ref = x_refs[-1]

    pq = [(jnp.dot(tc, x1_ref[b].astype(BF16), preferred_element_type=F32),
           jnp.dot(ts, x2_ref[b].astype(BF16), preferred_element_type=F32)) for b in range(nb)]

    @pl.when(j == 0)
    def _():
        for b in range(nb):
            p_ref[b] = pq[b][0]
            q_ref[b] = pq[b][1]

    @pl.when(j > 0)
    def _():
        for b in range(nb):
            p_ref[b] += pq[b][0]
            q_ref[b] += pq[b][1]

    @pl.when(j == nj - 1)
    def _():
        if mode == "filt":
            nrm = rest[0][...]
            kre_ref, kim_ref = rest[1], rest[2]
            kre_ref[...] = (p_ref[0] + p_ref[1]) / nrm
            kim_ref[...] = (q_ref[1] - q_ref[0]) / nrm
        elif mode == "fwdk":
            kre, kim = rest[0][...], rest[1][...]
            yre_ref, yim_ref = rest[2], rest[3]
            for b in range(nb):
                pp, qq = p_ref[b], q_ref[b]
                yre_ref[b] = (pp * kre + qq * kim).astype(BF16)
                yim_ref[b] = (pp * kim - qq * kre).astype(BF16)
        elif mode == "inv":
            u_ref, x0_ref, skip_ref, o_ref = rest[0], rest[1], rest[2], rest[3]
            skip = skip_ref[...]
            for b in range(nb):
                o_ref[b] = ((p_ref[b] - q_ref[b] + u_ref[b] * skip) * x0_ref[b].astype(F32)).astype(BF16)
        else:
            o_ref = rest[0]
            for b in range(nb):
                o_ref[b] = (p_ref[b] - q_ref[b]).astype(BF16)


def _dft(mode, kind, xs, extras, nb):
    B, L, C = xs[0].shape
    ti = min(L, 256)
    tj = min(L, 512)
    bc, bs, rc, rs = _dft_tables(kind, L, ti)
    grid = (B // nb, L // ti, L // tj)
    x_spec = pl.BlockSpec((nb, tj, C), lambda g, i, j: (g, j, 0))
    row_spec = lambda c, dt=None: pl.BlockSpec((nb, ti, c), lambda g, i, j: (g, i, 0))
    in_specs = [pl.BlockSpec((ti, L), lambda g, i, j: (0, 0)),
                pl.BlockSpec((ti, L), lambda g, i, j: (0, 0)),
                pl.BlockSpec((None, 1, L), lambda g, i, j: (i, 0, 0)),
                pl.BlockSpec((None, 1, L), lambda g, i, j: (i, 0, 0))] + [x_spec] * len(xs)
    if mode == "filt":
        in_specs += [pl.BlockSpec((1, HY_CH), lambda g, i, j: (0, 0))]
        out_shape = (jax.ShapeDtypeStruct((L, HY_CH), F32),) * 2
        out_specs = (pl.BlockSpec((ti, HY_CH), lambda g, i, j: (i, 0)),) * 2
    elif mode == "fwdk":
        in_specs += [pl.BlockSpec((ti, HY_CH), lambda g, i, j: (i, 0))] * 2
        out_shape = (jax.ShapeDtypeStruct((B, L, C), BF16),) * 2
        out_specs = (row_spec(C),) * 2
    elif mode == "inv":
        in_specs += [row_spec(C)] * 2 + [pl.BlockSpec((1, C), lambda g, i, j: (0, 0))]
        out_shape = jax.ShapeDtypeStruct((B, L, C), BF16)
        out_specs = row_spec(C)
    else:
        out_shape = jax.ShapeDtypeStruct((B, L, C), BF16)
        out_specs = row_spec(C)
    return pl.pallas_call(
        functools.partial(_dft_kernel, mode, nb, len(xs)),
        out_shape=out_shape,
        grid=grid,
        in_specs=in_specs,
        out_specs=out_specs,
        scratch_shapes=[pltpu.VMEM((nb, ti, C), F32), pltpu.VMEM((nb, ti, C), F32)],
        compiler_params=_cparams(("arbitrary", "arbitrary", "arbitrary")),
        name="dft_" + mode,
    )(bc, bs, rc, rs, *xs, *extras)


FFT_R = 64
HYENA_KF = 16
FNET_KF = 16


def _pack_pairs(x):
    return pltpu.bitcast(x.astype(BF16), jnp.uint32)


def _unpack_pairs(w):
    return pltpu.bitcast(w, BF16)


def _to_blocks(w):
    return jnp.swapaxes(w.reshape(FFT_R, w.shape[-2], w.shape[-1]), 0, 1)


def _from_blocks(ws):
    kf, c = len(ws), ws[0].shape[-1]
    return jnp.swapaxes(jnp.stack(ws, axis=0), 0, 1).reshape(FFT_R // kf, kf, kf, c)


def _lead_in_kernel(g_ref, x_ref, o_ref):
    g = g_ref[...]
    xt = jnp.swapaxes(x_ref[...], 0, 1)
    for j in range(x_ref.shape[1]):
        o_ref[j] = _pack_pairs(jnp.dot(g, xt[j].astype(BF16), preferred_element_type=F32))


def _lead_in(g, x, kf, name):
    nbx, _, _, c = x.shape
    m2 = g.shape[0] // 2
    return pl.pallas_call(
        _lead_in_kernel,
        out_shape=jax.ShapeDtypeStruct((nbx, FFT_R // kf, kf, m2, c), jnp.uint32),
        grid=(nbx, FFT_R // kf),
        in_specs=[pl.BlockSpec(g.shape, lambda b, k: (0, 0)),
                  pl.BlockSpec((None, FFT_R, kf, c), lambda b, k: (b, 0, k, 0))],
        out_specs=pl.BlockSpec((None, None, kf, m2, c), lambda b, k: (b, k, 0, 0, 0)),
        compiler_params=_cparams(("arbitrary", "arbitrary")),
        name=name,
    )(g, x)


def _lead_out_kernel(n_extra, g_ref, w_ref, *rest):
    g = g_ref[...]
    o_ref = rest[-1]
    ys = [jnp.dot(g, _unpack_pairs(w_ref[j]), preferred_element_type=F32) for j in range(w_ref.shape[0])]
    y = jnp.swapaxes(jnp.stack(ys, axis=0), 0, 1)
    if n_extra:
        y = (y + rest[0][...] * rest[2][...]) * rest[1][...].astype(F32)
    o_ref[...] = y.astype(o_ref.dtype)


def _lead_out(g, w, epilogue, name):
    nb, nk, kf, k2, c = w.shape
    blk = pl.BlockSpec((None, FFT_R, kf, c), lambda b, k: (b, 0, k, 0))
    extra_specs = [blk, blk, pl.BlockSpec((1, c), lambda b, k: (0, 0))] if epilogue else []
    return pl.pallas_call(
        functools.partial(_lead_out_kernel, len(epilogue)),
        out_shape=jax.ShapeDtypeStruct((nb, FFT_R, FFT_R, c), BF16),
        grid=(nb, nk),
        in_specs=[pl.BlockSpec(g.shape, lambda b, k: (0, 0)),
                  pl.BlockSpec((None, None, kf, k2, c), lambda b, k: (b, k, 0, 0, 0))] + extra_specs,
        out_specs=blk,
        compiler_params=_cparams(("arbitrary", "arbitrary")),
        name=name,
    )(g, w, *epilogue)


def _interleave(a, b, axis):
    st = np.stack([a, b], axis=axis + 1)
    shape = list(a.shape)
    shape[axis] *= 2
    return st.reshape(shape)


def _hy2_tables():
    L = FFT_R * FFT_R
    n2 = 2 * L
    f1 = np.arange(2 * FFT_R, dtype=np.int64)
    s1 = np.arange(FFT_R, dtype=np.int64)
    th = np.pi * (((2 * f1[:, None] + 1) * s1[None, :]) % (4 * FFT_R)) / (2 * FFT_R)
    ga = _interleave(np.cos(th), -np.sin(th), 0)
    ma = _interleave(np.cos(th).T, -np.sin(th).T, 1) / L
    f2 = np.arange(FFT_R // 2, dtype=np.int64)
    s2 = np.arange(FFT_R, dtype=np.int64)
    idx = ((n2 // FFT_R) * 2 * f2[None, :, None] * s2[None, None, :]
           + (2 * f1[:, None, None] + 1) * s2[None, None, :]) % (2 * n2)
    al = np.pi * idx / n2
    c, s = np.cos(al), np.sin(al)
    nmat = np.concatenate([_interleave(c, s, 2), _interleave(-s, c, 2)], axis=1)
    ct, st = np.transpose(c, (0, 2, 1)), np.transpose(s, (0, 2, 1))
    mmat = _interleave(np.concatenate([ct, -st], axis=2), np.concatenate([st, ct], axis=2), 1)
    bf = lambda a: jnp.asarray(a, F32).astype(BF16)
    return bf(ga), bf(nmat), bf(mmat), bf(ma)


def _hy_mid_kernel(a_ref, n_ref, m_ref, k_ref, e_ref):
    half = FFT_R // 2
    kf = a_ref.shape[-2]
    a = _to_blocks(a_ref[...])
    ts = [jnp.dot(n_ref[j], _unpack_pairs(a[j]), preferred_element_type=F32)
          for j in range(kf)]
    ys = []
    for j in range(kf):
        tr, ti = ts[j][:half], ts[j][half:]
        kr, ki = k_ref[j, 0].astype(F32), k_ref[j, 1].astype(F32)
        ys.append(jnp.concatenate([tr * kr - ti * ki, tr * ki + ti * kr], axis=0).astype(BF16))
    e_ref[...] = _from_blocks([_pack_pairs(jnp.dot(m_ref[j], ys[j], preferred_element_type=F32))
                               for j in range(kf)])


def _hy_mid(a, nmat, mmat, khat):
    nb, nk, kf, _, c = a.shape
    nf1 = 2 * FFT_R
    blk = pl.BlockSpec((None, nk, kf, kf, c), lambda i, b: (b, 0, 0, i, 0))
    return pl.pallas_call(
        _hy_mid_kernel,
        out_shape=jax.ShapeDtypeStruct(a.shape, jnp.uint32),
        grid=(nf1 // kf, nb),
        in_specs=[blk,
                  pl.BlockSpec((kf, FFT_R, 2 * FFT_R), lambda i, b: (i, 0, 0)),
                  pl.BlockSpec((kf, 2 * FFT_R, FFT_R), lambda i, b: (i, 0, 0)),
                  pl.BlockSpec((kf, 2, FFT_R // 2, c), lambda i, b: (i, 0, 0, 0))],
        out_specs=blk,
        compiler_params=_cparams(("arbitrary", "arbitrary")),
        name="l0_hyena_mid",
    )(a, nmat, mmat, khat)


def _hy_kfilt_kernel(a_ref, n_ref, nrm_ref, k_ref):
    half = FFT_R // 2
    nrm = nrm_ref[...]
    af, ab = _to_blocks(a_ref[0]), _to_blocks(a_ref[1])
    for j in range(a_ref.shape[-2]):
        tf = jnp.dot(n_ref[j], _unpack_pairs(af[j]), preferred_element_type=F32)
        tb = jnp.dot(n_ref[j], _unpack_pairs(ab[j]), preferred_element_type=F32)
        k_ref[j, 0] = ((tf[:half] + tb[:half]) / nrm).astype(k_ref.dtype)
        k_ref[j, 1] = ((tf[half:] - tb[half:]) / nrm).astype(k_ref.dtype)


def _hy_kfilt(a, nmat, nrm):
    _, nk, kf, _, c = a.shape
    nf1 = 2 * FFT_R
    return pl.pallas_call(
        _hy_kfilt_kernel,
        out_shape=jax.ShapeDtypeStruct((nf1, 2, FFT_R // 2, c), BF16),
        grid=(nf1 // kf,),
        in_specs=[pl.BlockSpec((2, nk, kf, kf, c), lambda i: (0, 0, 0, i, 0)),
                  pl.BlockSpec((kf, FFT_R, 2 * FFT_R), lambda i: (i, 0, 0)),
                  _const_spec(nrm.shape)],
        out_specs=pl.BlockSpec((kf, 2, FFT_R // 2, c), lambda i: (i, 0, 0, 0)),
        compiler_params=_cparams(("arbitrary",)),
        name="l0_hyena_kfilt",
    )(a, nmat, nrm)


def _hyena_long(u, x0, skip, hfilt, nrm, nb):
    L = FFT_R * FFT_R
    c = u.shape[-1]
    v4 = lambda a, n: a.reshape(n, FFT_R, FFT_R, c)
    ga, nmat, mmat, ma = _hy2_tables()
    khat = _hy_kfilt(_lead_in(ga, v4(hfilt, 2), HYENA_KF, "l0_hyena_fwd_a"), nmat, nrm)
    ee = _hy_mid(_lead_in(ga, v4(u, nb), HYENA_KF, "l0_hyena_fwd_a"), nmat, mmat, khat)
    y = _lead_out(ma, ee, (v4(u, nb), v4(x0, nb), skip), "l0_hyena_inv_a")
    return y.reshape(nb * L, c)


def _fn2_tables():
    L = FFT_R * FFT_R
    r = np.arange(FFT_R, dtype=np.int64)
    idx = (FFT_R * r[None, :, None] * r[None, None, :] + r[None, :, None] * r[:, None, None]) % L
    gm = 2.0 * np.pi * idx / L
    c, s = np.cos(gm), np.sin(gm)
    g1 = _interleave(np.concatenate([c, -s], axis=2), np.concatenate([-s, -c], axis=2), 1)
    dl = 2.0 * np.pi * ((r[:, None] * r[None, :]) % FFT_R) / FFT_R
    g2 = _interleave(np.cos(dl), np.sin(dl), 1) / math.sqrt(L * FN_GROUP_CH)
    bf = lambda a: jnp.asarray(a, F32).astype(BF16)
    return bf(g1), bf(g2)


def _fnet_s1_kernel(x_ref, m_ref, cs_ref, g1_ref, o_ref, zc_ref, zs_ref):
    xs = jnp.swapaxes(x_ref[...], 0, 1).reshape(FNET_KF * FFT_R, D_MODEL)
    m = m_ref[...]
    h = (_ln_plain(xs) * (1.0 + m[:, D_MODEL:2 * D_MODEL]) + m[:, 0:D_MODEL]).astype(BF16)
    cs = cs_ref[...]
    for g in range(D_MODEL // FN_GROUP_CH):
        a = g * FN_GROUP_CH
        z = jnp.dot(h[:, a:a + FN_GROUP_CH], cs, preferred_element_type=F32)
        zc_ref[:, a:a + FN_GROUP_CH] = z[:, :FN_GROUP_CH].astype(BF16)
        zs_ref[:, a:a + FN_GROUP_CH] = z[:, FN_GROUP_CH:].astype(BF16)
    ws = []
    for j in range(FNET_KF):
        r0 = j * FFT_R
        s = jnp.concatenate([zc_ref[r0:r0 + FFT_R, :], zs_ref[r0:r0 + FFT_R, :]], axis=0)
        ws.append(_pack_pairs(jnp.dot(g1_ref[j], s, preferred_element_type=F32)))
    o_ref[...] = _from_blocks(ws)


def _fnet_long(x, mods, mod_base, nb):
    L = FFT_R * FFT_R
    d = D_MODEL
    g1, g2 = _fn2_tables()
    cs = _group_dft_table()
    kf = FNET_KF
    bb = pl.pallas_call(
        _fnet_s1_kernel,
        out_shape=jax.ShapeDtypeStruct((nb, FFT_R // kf, kf, FFT_R, d), jnp.uint32),
        grid=(nb, FFT_R // kf),
        in_specs=[pl.BlockSpec((None, FFT_R, kf, d), lambda b, k: (b, 0, k, 0)),
                  pl.BlockSpec((None, 1, 6 * d), lambda b, k: (mod_base + b, 0, 0)),
                  pl.BlockSpec(cs.shape, lambda b, k: (0, 0)),
                  pl.BlockSpec((kf, 2 * FFT_R, 2 * FFT_R), lambda b, k: (k, 0, 0))],
        out_specs=pl.BlockSpec((None, FFT_R // kf, kf, kf, d), lambda b, k: (b, 0, 0, k, 0)),
        scratch_shapes=[pltpu.VMEM((kf * FFT_R, d), BF16), pltpu.VMEM((kf * FFT_R, d), BF16)],
        compiler_params=_cparams(("arbitrary", "arbitrary")),
        name="l1_fnet_stage1",
    )(x.reshape(nb, FFT_R, FFT_R, d), mods, cs, g1)
    y = _lead_out(g2, bb, [], "l1_fnet_stage2")
    return y.reshape(nb * L, d)


def _group_dft_table():
    g = FN_GROUP_CH
    jk = (np.arange(g, dtype=np.int64)[:, None] * np.arange(g, dtype=np.int64)[None, :]) % g
    ang = 2.0 * np.pi * jk / g
    return jnp.asarray(np.concatenate([np.cos(ang), np.sin(ang)], axis=1), F32).astype(BF16)


def _fnet_front_kernel(x_ref, m_ref, cs_ref, zc_ref, zs_ref):
    m = m_ref[...]
    h = (_ln_plain(x_ref[...]) * (1.0 + m[:, D_MODEL:2 * D_MODEL]) + m[:, 0:D_MODEL]).astype(BF16)
    cs = cs_ref[...]
    for g in range(D_MODEL // FN_GROUP_CH):
        a = g * FN_GROUP_CH
        z = jnp.dot(h[:, a:a + FN_GROUP_CH], cs, preferred_element_type=F32)
        zc_ref[:, a:a + FN_GROUP_CH] = z[:, :FN_GROUP_CH].astype(BF16)
        zs_ref[:, a:a + FN_GROUP_CH] = z[:, FN_GROUP_CH:].astype(BF16)


def _fnet_front(x, mods, mod_base, tiles_per_mod):
    t = x.shape[0]
    tm = 2 * ROW_TILE
    tiles_per_mod = max(tiles_per_mod // 2, 1)
    cs = _group_dft_table()
    return pl.pallas_call(
        _fnet_front_kernel,
        out_shape=(jax.ShapeDtypeStruct((t, D_MODEL), BF16),) * 2,
        grid=(t // tm,),
        in_specs=[pl.BlockSpec((tm, D_MODEL), lambda i: (i, 0)),
                  _mod_spec(mod_base, tiles_per_mod),
                  _const_spec(cs.shape)],
        out_specs=(pl.BlockSpec((tm, D_MODEL), lambda i: (i, 0)),) * 2,
        compiler_params=_cparams(("arbitrary",)),
        name="l1_fnet_front",
    )(x, mods, cs)


def _post_kernel(n_a, *refs):
    x_ref, m_ref = refs[0], refs[1]
    a_refs = refs[2:2 + n_a]
    wo_refs = refs[2 + n_a:2 + 2 * n_a]
    g1_ref, b1_ref, w1_ref, w2_ref, g2_ref, b2_ref, o_ref = refs[2 + 2 * n_a:]
    m = m_ref[...]
    d = D_MODEL
    tm = x_ref.shape[0]
    halves = [(r, r + tm // POST_SPLIT) for r in range(0, tm, tm // POST_SPLIT)]
    outs = []
    for r0, r1 in halves:
        out = _bdot(a_refs[0][r0:r1, :], wo_refs[0][...])
        for a_ref, wo_ref in zip(a_refs[1:], wo_refs[1:]):
            out += _bdot(a_ref[r0:r1, :], wo_ref[...])
        outs.append(out)
    x1s, hs = [], []
    for (r0, r1), out in zip(halves, outs):
        x1 = _ln_plain(ALPHA * x_ref[r0:r1, :] + m[:, 2 * d:3 * d] * out) * g1_ref[...] + b1_ref[...]
        x1s.append(x1)
        hs.append((_ln_plain(x1) * (1.0 + m[:, 4 * d:5 * d]) + m[:, 3 * d:4 * d]).astype(BF16))
    accs = []
    n_c = D_FF // d

    def up(h, c):
        hc = jnp.maximum(jnp.dot(h, w1_ref[:, c * d:(c + 1) * d], preferred_element_type=F32), 0.0)
        return (hc * hc).astype(BF16)

    for h in hs:
        acc = None
        nxt = up(h, 0)
        for c in range(n_c):
            cur = nxt
            if c + 1 < n_c:
                nxt = up(h, c + 1)
            part = jnp.dot(cur, w2_ref[c * d:(c + 1) * d, :], preferred_element_type=F32)
            acc = part if acc is None else acc + part
        accs.append(acc)
    for (r0, r1), x1, acc in zip(halves, x1s, accs):
        o_ref[r0:r1, :] = _ln_plain(ALPHA * x1 + m[:, 5 * d:6 * d] * acc) * g2_ref[...] + b2_ref[...]


def _post(x, mods, mod_base, tiles_per_mod, a_list, wo_list, g1, b1, w1, w2, g2, b2):
    t = x.shape[0]
    tm = POST_TILE
    row = lambda c: pl.BlockSpec((tm, c), lambda i: (i, 0))
    once = lambda v: pl.BlockSpec(v.shape, lambda i: (0,) * v.ndim, pipeline_mode=pl.Buffered(1))
    in_specs = ([row(D_MODEL), _mod_spec(mod_base, tiles_per_mod * ROW_TILE // tm)]
                + [row(a.shape[1]) for a in a_list]
                + [once(w) for w in wo_list]
                + [once(v) for v in (g1, b1, w1, w2, g2, b2)])
    return pl.pallas_call(
        functools.partial(_post_kernel, len(a_list)),
        out_shape=jax.ShapeDtypeStruct((t, D_MODEL), F32),
        grid=(t // tm,),
        in_specs=in_specs,
        out_specs=row(D_MODEL),
        compiler_params=_cparams(("arbitrary",)),
        name="post_mlp",
    )(x, mods, *a_list, *wo_list, g1, b1, w1, w2, g2, b2)


def _rot_cols(w):
    parts = []
    for seg in range(2):
        o = seg * 32
        parts += [-w[:, o + 16:o + 32], w[:, o:o + 16]]
    return jnp.concatenate(parts, axis=1)


def _pad_cols(w, n):
    return jnp.pad(w, ((0, 0), (0, n - w.shape[1])))


def _block_diag2(w):
    z = jnp.zeros_like(w)
    return jnp.concatenate([jnp.concatenate([w, z], axis=1), jnp.concatenate([z, w], axis=1)], axis=0)


def _rope_tables(L):
    rows = L // GRID_W
    row = np.repeat(np.arange(rows, dtype=np.float64), GRID_W)
    col = np.tile(np.arange(GRID_W, dtype=np.float64), rows)
    half = QK_ROPE // 2
    inv = 1.0 / (ROPE_THETA ** (np.arange(0, half, 2, dtype=np.float64) / half))
    ar = row[:, None] * inv[None, :]
    ac = col[:, None] * inv[None, :]
    ang = np.concatenate([ar, ar, ac, ac], axis=1)
    cos = np.concatenate([np.cos(ang), np.ones_like(ang)], axis=1)
    sin = np.concatenate([np.sin(ang), np.zeros_like(ang)], axis=1)
    return jnp.asarray(cos, F32), jnp.asarray(sin, F32)


def kernel(x_prompt, x_sample, cache_l0_ckv, cache_l0_krope, c, c_ctx, l0_ada_w, l0_ada_b, l0_w_in, l0_conv_w, l0_conv_b, l0_hf_w1, l0_hf_b1, l0_hf_freq, l0_hf_w2, l0_hf_b2, l0_hf_w3, l0_hf_skip, l0_q_norm, l0_q_up, l0_kv_norm, l0_kv_up, l0_w_out, l0_ln1_g, l0_ln1_b, l0_mlp_w1, l0_mlp_w2, l0_ln2_g, l0_ln2_b, l1_ada_w, l1_ada_b, l1_w_out, l1_ln1_g, l1_ln1_b, l1_mlp_w1, l1_mlp_w2, l1_ln2_g, l1_ln2_b):
    nbc, lc, d = x_prompt.shape
    nbs, ls, _ = x_sample.shape
    past = cache_l0_ckv.shape[1]
    tm = ROW_TILE
    row1 = lambda v: v.reshape(1, -1)

    cond8 = jnp.concatenate([c_ctx[None, :], c, jnp.zeros((8 - 1 - nbs, d), F32)], axis=0)
    mods0 = _modulation(cond8, l0_ada_w, l0_ada_b)
    mods1 = _modulation(cond8, l1_ada_w, l1_ada_b)

    kpe_w = l0_w_in[:, 1920:1984]
    win = jnp.concatenate([l0_w_in[:, :1920], _pad_cols(kpe_w, LANE), _pad_cols(_rot_cols(kpe_w), LANE)],
                          axis=1).astype(BF16)
    dh = QK_NOPE + QK_ROPE
    q_nope = [l0_q_up[:, h * dh:h * dh + QK_NOPE] for h in range(MLA_HEADS)]
    q_pe = [l0_q_up[:, h * dh + QK_NOPE:(h + 1) * dh] for h in range(MLA_HEADS)]
    qup = jnp.concatenate(q_nope + [_pad_cols(w, LANE) for w in q_pe]
                          + [_pad_cols(_rot_cols(w), LANE) for w in q_pe], axis=1).astype(BF16)
    kvup = l0_kv_up.astype(BF16)
    front_w = (win, row1(l0_q_norm), qup, row1(l0_kv_norm), kvup, l0_conv_w, row1(l0_conv_b))
    skip = row1(l0_hf_skip)
    w1p = jnp.pad(l0_hf_w1, ((0, LANE - l0_hf_w1.shape[0]), (0, 0)))
    two = lambda v: jnp.tile(row1(v), (1, 2))
    filt_w = (_block_diag2(w1p), two(l0_hf_b1), two(l0_hf_freq), _block_diag2(l0_hf_w2), two(l0_hf_b2),
              _block_diag2(l0_hf_w3).astype(BF16))
    wo0 = l0_w_out.astype(BF16)

    xc = x_prompt.reshape(nbc * lc, d)
    xs = x_sample.reshape(nbs * ls, d)
    groups = (
        dict(x=xc, nb=nbc, L=lc, mod_base=0, tiles_per_mod=nbc * lc // tm, dft_nb=8, tq=lc, hps=MLA_HEADS),
        dict(x=xs, nb=nbs, L=ls, mod_base=1, tiles_per_mod=ls // tm, dft_nb=nbs, tq=1024, hps=1),
    )
    ones_tab = (jnp.ones((FRONT_TILE, LANE), F32), jnp.zeros((FRONT_TILE, LANE), F32))

    outs = []
    ctx_ckv = ctx_krope = None
    mixed = []
    mlp_w = None
    for gi, g in enumerate(groups):
        nb, L = g["nb"], g["L"]
        latent = gi == 1
        cos, sin = _rope_tables(L) if latent else ones_tab
        u, x0, q, k, vt, kvn, kpe = _front(g["x"], mods0, g["mod_base"], g["tiles_per_mod"] * tm, front_w,
                                           cos, sin, latent, L, min(L if latent else nb * L, FRONT_TILE))
        if latent:
            extra = _cache_kv(cache_l0_ckv.reshape(nbs * past, KV_LORA),
                              _pad_cols(cache_l0_krope.reshape(nbs * past, QK_ROPE), LANE), kvup)
            y_mla, mlp_w = _attention(q, k, vt, extra, nb, L, g["tq"], g["hps"],
                                      (l0_mlp_w1, l0_mlp_w2, l1_mlp_w1, l1_mlp_w2))
        else:
            ctx_ckv = kvn.reshape(nb, L, KV_LORA)
            ctx_krope = kpe.reshape(nb, L, QK_ROPE)
            y_mla, _ = _attention(q, k, vt, None, nb, L, g["tq"], g["hps"])
        mixed.append((u, x0, y_mla))

    for g, (u, x0, y_mla) in zip(groups, mixed):
        nb, L = g["nb"], g["L"]
        two_stage = L == FFT_R * FFT_R
        hfilt, hnorm = _filters(L, *filt_w)
        if two_stage:
            y_hy = _hyena_long(u, x0, skip, hfilt, hnorm, nb)
        else:
            kre, kim = _dft("filt", "hy_fwd", [hfilt], [hnorm], 2)
            sh = (nb, L, HY_CH)
            yre, yim = _dft("fwdk", "hy_fwd", [u.reshape(sh)], [kre, kim], g["dft_nb"])
            y_hy = _dft("inv", "hy_inv", [yre, yim], [u.reshape(sh), x0.reshape(sh), skip], g["dft_nb"])
            y_hy = y_hy.reshape(nb * L, HY_CH)

        x1 = _post(g["x"], mods0, g["mod_base"], g["tiles_per_mod"], [y_hy, y_mla],
                   [wo0[:HY_CH], wo0[HY_CH:]], row1(l0_ln1_g), row1(l0_ln1_b),
                   mlp_w[0], mlp_w[1], row1(l0_ln2_g), row1(l0_ln2_b))

        if L == FFT_R * FFT_R:
            yf = _fnet_long(x1, mods1, g["mod_base"], nb)
        else:
            zc, zs = _fnet_front(x1, mods1, g["mod_base"], g["tiles_per_mod"])
            sh = (nb, L, d)
            yf = _dft("fnet", "fnet", [zc.reshape(sh), zs.reshape(sh)], [], g["dft_nb"] // 2)
            yf = yf.reshape(nb * L, d)
        x2 = _post(x1, mods1, g["mod_base"], g["tiles_per_mod"], [yf],
                   [l1_w_out.astype(BF16)], row1(l1_ln1_g), row1(l1_ln1_b),
                   mlp_w[2], mlp_w[3], row1(l1_ln2_g), row1(l1_ln2_b))
        outs.append(x2.reshape(nb, L, d))

    return (outs[0], outs[1], ctx_ckv, ctx_krope)
```

```python
import functools
import math

import numpy as np
import jax
import jax.numpy as jnp
from jax import lax
from jax.experimental import pallas as pl
from jax.experimental.pallas import tpu as pltpu

F32 = jnp.float32
BF16 = jnp.bfloat16
HI = lax.Precision.HIGHEST

D_MODEL = 1024
DEPTH = 2
GRID_W = 64
HY_CH = 512
FILT_BANDS = 16
FILT_ORDER = 64
FAST_DECAY_PCT = 0.3
SLOW_DECAY_PCT = 1.5
DECAY_TARGET = 1e-2
MAX_DECAY = math.log(DECAY_TARGET) / FAST_DECAY_PCT
MIN_DECAY = math.log(DECAY_TARGET) / SLOW_DECAY_PCT
MLA_HEADS = 4
QK_NOPE = 128
QK_ROPE = 64
V_HEAD = 128
Q_LORA = 256
KV_LORA = 128
ROPE_THETA = 10000.0
FN_GROUP_CH = 128
D_FF = 4096
ALPHA = (2 * DEPTH) ** 0.25
LN_EPS = 1e-5
RMS_EPS = 1e-6

LANE = 128
ROW_TILE = 256
FRONT_TILE = 1024
POST_TILE = 512
POST_SPLIT = 2
QK_PAD = 256
VT_ROWS = V_HEAD + 16
LOG2E = 1.4426950408889634
VMEM_LIMIT = 56 * 1024 * 1024


def _cparams(sem):
    return pltpu.CompilerParams(dimension_semantics=sem, vmem_limit_bytes=VMEM_LIMIT)


def _ln_plain(x):
    mu = jnp.mean(x, axis=-1, keepdims=True)
    xc = x - mu
    var = jnp.mean(xc * xc, axis=-1, keepdims=True)
    return xc * lax.rsqrt(var + LN_EPS)


def _rms(x, g):
    return x * lax.rsqrt(jnp.mean(x * x, axis=-1, keepdims=True) + RMS_EPS) * g


def _bdot(a, b):
    return jnp.dot(a.astype(BF16), b, preferred_element_type=F32)


def _vt_rows(v):
    ones = jnp.ones((VT_ROWS - V_HEAD, v.shape[0]), BF16)
    return jnp.concatenate([jnp.transpose(v).astype(BF16), ones], axis=0)


def _mod_kernel(c_ref, w_ref, b_ref, o_ref):
    c = c_ref[...]
    s = c / (1.0 + jnp.exp(-c))
    s_hi = s.astype(BF16)
    s_lo = (s - s_hi.astype(F32)).astype(BF16)
    s2 = jnp.concatenate([s_hi, s_lo], axis=0)
    nr = s.shape[0]
    w = w_ref[...]
    w_hi = w.astype(BF16)
    w_lo = (w - w_hi.astype(F32)).astype(BF16)
    r1 = jnp.dot(s2, w_hi, preferred_element_type=F32)
    r2 = jnp.dot(s_hi, w_lo, preferred_element_type=F32)
    part = r1[:nr] + r1[nr:] + r2
    j = pl.program_id(0)

    @pl.when(j == 0)
    def _():
        o_ref[:, 0, :] = part + b_ref[...]

    @pl.when(j > 0)
    def _():
        o_ref[:, 0, :] += part


def _modulation(cond8, w, b):
    n = w.shape[1]
    tk = LANE
    return pl.pallas_call(
        _mod_kernel,
        out_shape=jax.ShapeDtypeStruct((8, 1, n), F32),
        grid=(D_MODEL // tk,),
        in_specs=[pl.BlockSpec((8, tk), lambda j: (0, j)),
                  pl.BlockSpec((tk, n), lambda j: (j, 0)),
                  pl.BlockSpec((1, n), lambda j: (0, 0))],
        out_specs=pl.BlockSpec((8, 1, n), lambda j: (0, 0, 0)),
        compiler_params=_cparams(("arbitrary",)),
        name="modulation",
    )(cond8, w, b.reshape(1, n))


def _mod_spec(mod_base, tiles_per_mod):
    return pl.BlockSpec((None, 1, 6 * D_MODEL), lambda i: (mod_base + i // tiles_per_mod, 0, 0))


def _const_spec(shape):
    nd = len(shape)
    return pl.BlockSpec(shape, lambda i: (0,) * nd)


HALO = 8


def _front_kernel(seq_len, x_ref, xp_ref, xn_ref, m_ref, win_ref, qn_ref, qup_ref, kvn_ref, kvup_ref,
                  cos_ref, sin_ref, cw_ref, cb_ref,
                  u_ref, x0_ref, q_ref, k_ref, v_ref, kvn_out_ref, kpe_ref):
    i = pl.program_id(0)
    m = m_ref[...]
    tm = x_ref.shape[0]
    nh = 3 * HY_CH
    xe = jnp.concatenate([xp_ref[...], x_ref[...], xn_ref[...]], axis=0)
    he = _ln_plain(xe) * (1.0 + m[:, D_MODEL:2 * D_MODEL]) + m[:, 0:D_MODEL]

    z = _bdot(he[HALO:HALO + tm], win_ref[:, nh:])
    zh = _bdot(he, win_ref[:, :nh])
    q_c = z[:, 0:256]
    kv_c = z[:, 256:384]
    cos = cos_ref[...]
    sin = sin_ref[...]
    kpe = z[:, 384:512] * cos + z[:, 512:640] * sin
    kpe_ref[...] = kpe[:, :QK_ROPE]
    kpe_b = kpe.astype(BF16)
    q = _bdot(_rms(q_c, qn_ref[...]), qup_ref[...]) * (LOG2E / math.sqrt(QK_NOPE + QK_ROPE))
    kvn = _rms(kv_c, kvn_ref[...])
    kvn_out_ref[...] = kvn
    kv = _bdot(kvn, kvup_ref[...])

    rows = lax.broadcasted_iota(jnp.int32, (tm + 2 * HALO, 1), 0)
    if seq_len >= tm:
        tiles_per_seq = seq_len // tm
        pos = i % tiles_per_seq
        inside = jnp.logical_and(jnp.logical_or(rows >= HALO, pos != 0),
                                 jnp.logical_or(rows < tm + HALO, pos != tiles_per_seq - 1))
        zh = jnp.where(inside, zh, 0.0)
        prev = pltpu.roll(zh, 1, 0)
        nxt = pltpu.roll(zh, tm + 2 * HALO - 1, 0)
    else:
        at = (rows + (seq_len - HALO)) % seq_len
        prev = jnp.where(at == 0, 0.0, pltpu.roll(zh, 1, 0))
        nxt = jnp.where(at == seq_len - 1, 0.0, pltpu.roll(zh, tm + 2 * HALO - 1, 0))
    cw = cw_ref[...]
    pz = (prev * cw[0:1, :] + zh * cw[1:2, :] + nxt * cw[2:3, :])[HALO:HALO + tm] + cb_ref[...]
    u_ref[...] = (pz[:, 2 * HY_CH:] * pz[:, HY_CH:2 * HY_CH]).astype(u_ref.dtype)
    x0_ref[...] = pz[:, :HY_CH].astype(x0_ref.dtype)

    for hd in range(MLA_HEADS):
        a = hd * LANE
        q_pe = (q[:, 512 + a:512 + a + LANE] * cos + q[:, 1024 + a:1024 + a + LANE] * sin).astype(BF16)
        q_ref[hd] = jnp.concatenate([q[:, a:a + LANE].astype(BF16), q_pe], axis=-1)
        k_ref[hd] = jnp.concatenate([kv[:, 2 * a:2 * a + LANE].astype(BF16), kpe_b], axis=-1)
        v_ref[hd] = _vt_rows(kv[:, 2 * a + LANE:2 * a + 2 * LANE])


def _front(x, mods, mod_base, rows_per_mod, w, cos, sin, rope, seq_len, tm):
    t = x.shape[0]
    tiles_per_mod = rows_per_mod // tm
    win, qn, qup, kvn, kvup, conv_w, conv_b = w
    assert seq_len % tm == 0 or (tm % seq_len == 0 and not rope)
    if rope:
        tiles_per_seq = seq_len // tm
        tab_spec = pl.BlockSpec((tm, LANE), lambda i: (i % tiles_per_seq, 0))
    else:
        tab_spec = pl.BlockSpec((tm, LANE), lambda i: (0, 0))
    r8 = tm // HALO
    n8 = t // HALO
    hy_out = lambda dt: jax.ShapeDtypeStruct((t, HY_CH), dt)
    hy_spec = pl.BlockSpec((tm, HY_CH), lambda i: (i, 0))
    return pl.pallas_call(
        functools.partial(_front_kernel, seq_len),
        out_shape=(hy_out(BF16), hy_out(BF16),
                   jax.ShapeDtypeStruct((MLA_HEADS, t, QK_PAD), BF16),
                   jax.ShapeDtypeStruct((MLA_HEADS, t, QK_PAD), BF16),
                   jax.ShapeDtypeStruct((MLA_HEADS, VT_ROWS, t), BF16),
                   jax.ShapeDtypeStruct((t, KV_LORA), F32),
                   jax.ShapeDtypeStruct((t, QK_ROPE), F32)),
        grid=(t // tm,),
        in_specs=[pl.BlockSpec((tm, D_MODEL), lambda i: (i, 0)),
                  pl.BlockSpec((HALO, D_MODEL), lambda i: (jnp.maximum(i * r8 - 1, 0), 0)),
                  pl.BlockSpec((HALO, D_MODEL), lambda i: (jnp.minimum((i + 1) * r8, n8 - 1), 0)),
                  _mod_spec(mod_base, tiles_per_mod),
                  _const_spec(win.shape), _const_spec(qn.shape), _const_spec(qup.shape),
                  _const_spec(kvn.shape), _const_spec(kvup.shape),
                  tab_spec, tab_spec,
                  _const_spec(conv_w.shape), _const_spec(conv_b.shape)],
        out_specs=(hy_spec, hy_spec,
                   pl.BlockSpec((MLA_HEADS, tm, QK_PAD), lambda i: (0, i, 0)),
                   pl.BlockSpec((MLA_HEADS, tm, QK_PAD), lambda i: (0, i, 0)),
                   pl.BlockSpec((MLA_HEADS, VT_ROWS, tm), lambda i: (0, 0, i)),
                   pl.BlockSpec((tm, KV_LORA), lambda i: (i, 0)),
                   pl.BlockSpec((tm, QK_ROPE), lambda i: (i, 0))),
        compiler_params=_cparams(("arbitrary",)),
        name="l0_front",
    )(x, x, x, mods, win, qn, qup, kvn, kvup, cos, sin, conv_w, conv_b)


def _cache_kv_kernel(ckv_ref, kr_ref, kvup_ref, k_ref, v_ref):
    kv = _bdot(ckv_ref[...], kvup_ref[...])
    kr = kr_ref[...].astype(BF16)
    for hd in range(MLA_HEADS):
        a = 2 * hd * LANE
        k_ref[hd] = jnp.concatenate([kv[:, a:a + LANE].astype(BF16), kr], axis=-1)
        v_ref[hd] = _vt_rows(kv[:, a + LANE:a + 2 * LANE])


def _cache_kv(ckv, krope_pad, kvup):
    t = ckv.shape[0]
    return pl.pallas_call(
        _cache_kv_kernel,
        out_shape=(jax.ShapeDtypeStruct((MLA_HEADS, t, QK_PAD), BF16),
                   jax.ShapeDtypeStruct((MLA_HEADS, VT_ROWS, t), BF16)),
        name="l0_cache_kv",
    )(ckv, krope_pad, kvup)


def _col_reduce(x, op):
    rows, n = x.shape
    for g in (32, 8):
        if rows % (8 * g) == 0 and rows > 8 * g:
            x = op(x.reshape(rows // (8 * g), 8 * g, n), axis=0)
            rows = 8 * g
    return op(x, axis=0, keepdims=True)


PV_CHUNK = 128


def _attn_kernel(n_kv, n_cast, hps, q_ref, *refs):
    k_refs, vt_refs = refs[:n_kv], refs[n_kv:2 * n_kv]
    w_refs = refs[2 * n_kv:2 * n_kv + n_cast]
    o_ref = refs[2 * n_kv + n_cast]
    wo_refs = refs[2 * n_kv + n_cast + 1:2 * n_kv + 2 * n_cast + 1]
    s_even, s_odd, m_even, m_odd = refs[2 * n_kv + 2 * n_cast + 1:]
    i = pl.program_id(0)

    for w_ref, wo_ref in zip(w_refs, wo_refs):
        wo_ref[...] = w_ref[...].astype(wo_ref.dtype)

    @pl.when(i == 0)
    def _():
        s_odd[...] = jnp.zeros_like(s_odd)
        m_odd[...] = jnp.zeros_like(m_odd)

    def step(s_write, m_write, s_read, m_read):
        nt = (((1,), (1,)), ((), ()))
        for h in range(hps):
            q = q_ref[h]
            r0 = 0
            m = None
            for k_ref in k_refs:
                lk = k_ref.shape[1]
                sblk = lax.dot_general(k_ref[h], q, nt, preferred_element_type=F32)
                s_write[h, r0:r0 + lk, :] = sblk
                mc = _col_reduce(sblk, jnp.max)
                m = mc if m is None else jnp.maximum(m, mc)
                r0 += lk
            m_write[h] = m
        for h in range(hps):
            m = m_read[h]
            acc = None
            r0 = 0
            for vt_ref in vt_refs:
                lk = vt_ref.shape[2]
                for c0 in range(0, lk, PV_CHUNK):
                    c1 = min(c0 + PV_CHUNK, lk)
                    pb = jnp.exp2(s_read[h, r0 + c0:r0 + c1, :] - m).astype(BF16)
                    pv = jnp.dot(vt_ref[h, :, c0:c1], pb, preferred_element_type=F32)
                    acc = pv if acc is None else acc + pv
                r0 += lk
            o_ref[:, h * V_HEAD:(h + 1) * V_HEAD] = jnp.transpose(
                acc[:V_HEAD] / acc[V_HEAD:V_HEAD + 1]).astype(o_ref.dtype)

    pl.when(i % 2 == 0)(lambda: step(s_even, m_even, s_odd, m_odd))
    pl.when(i % 2 == 1)(lambda: step(s_odd, m_odd, s_even, m_even))


def _attention(q, k, vt, extra, nb, lq, tq, hps, cast=()):
    nq = lq // tq
    ng = MLA_HEADS // hps
    n_tiles = nb * ng * nq

    def where(t):
        bh = t // nq
        return bh // ng, bh % ng, t % nq

    def score_side(fn):
        return lambda i: fn(*where(jnp.minimum(i, n_tiles - 1)))

    def value_side(fn):
        return lambda i: fn(*where(jnp.maximum(i - 1, 0)))

    ks, vts = [k], [vt]
    if extra is not None:
        ks.append(extra[0])
        vts.append(extra[1])
    in_specs = [pl.BlockSpec((hps, tq, QK_PAD), score_side(lambda b, h, j: (h, b * nq + j, 0)))]
    in_specs += [pl.BlockSpec((hps, a.shape[1] // nb, QK_PAD), score_side(lambda b, h, j: (h, b, 0))) for a in ks]
    in_specs += [pl.BlockSpec((hps, VT_ROWS, a.shape[2] // nb), value_side(lambda b, h, j: (h, 0, b))) for a in vts]
    lk_total = sum(a.shape[1] // nb for a in ks)
    cast_specs = [pl.BlockSpec((w.shape[0] // n_tiles, w.shape[1]), lambda i: (jnp.minimum(i, n_tiles - 1), 0))
                  for w in cast]
    in_specs += cast_specs
    res = pl.pallas_call(
        functools.partial(_attn_kernel, len(ks), len(cast), hps),
        out_shape=(jax.ShapeDtypeStruct((nb * lq, MLA_HEADS * V_HEAD), BF16),
                   *[jax.ShapeDtypeStruct(w.shape, BF16) for w in cast]),
        grid=(n_tiles + 1,),
        in_specs=in_specs,
        out_specs=(pl.BlockSpec((tq, hps * V_HEAD), value_side(lambda b, h, j: (b * nq + j, h))), *cast_specs),
        scratch_shapes=[pltpu.VMEM((hps, lk_total, tq), F32), pltpu.VMEM((hps, lk_total, tq), F32),
                        pltpu.VMEM((hps, 1, tq), F32), pltpu.VMEM((hps, 1, tq), F32)],
        compiler_params=_cparams(("arbitrary",)),
        name="l0_attention",
    )(q, *ks, *vts, *cast)
    return res[0], res[1:]


def _filter_kernel(z_ref, w1_ref, b1_ref, fr_ref, w2_ref, b2_ref, w3_ref, dl_ref, h_ref, norm_ref):
    i = pl.program_id(0)
    z = z_ref[...]
    tl = z.shape[0]
    fr = fr_ref[...]
    z2 = jnp.concatenate([z[:tl // 2], z[tl // 2:]], axis=1)
    h = jnp.sin(fr * (jnp.dot(z2, w1_ref[...], precision=HI, preferred_element_type=F32) + b1_ref[...]))
    h = jnp.sin(fr * (jnp.dot(h, w2_ref[...], precision=HI, preferred_element_type=F32) + b2_ref[...]))
    h = _bdot(h, w3_ref[...])
    h = jnp.concatenate([h[:, :2 * HY_CH], h[:, 2 * HY_CH:]], axis=0)
    decay = jnp.exp(-(z[:, 0:1] * dl_ref[...]))
    hf = h[:, :HY_CH] * decay
    hb = h[:, HY_CH:] * decay
    part = jnp.sum(jnp.abs(hf) + jnp.abs(hb), axis=0, keepdims=True)

    @pl.when(i == 0)
    def _():
        norm_ref[...] = part

    @pl.when(i > 0)
    def _():
        norm_ref[...] += part

    rows = lax.broadcasted_iota(jnp.int32, hb.shape, 0) + i * tl
    h_ref[0] = hf.astype(h_ref.dtype)
    h_ref[1] = jnp.where(rows == 0, 0.0, hb).astype(h_ref.dtype)


def _filter_embedding(L):
    t = np.linspace(0.0, 1.0, L)[:, None]
    w_ang = 2.0 * np.pi * np.arange(L) / L
    bands = np.linspace(1e-4, FILT_BANDS - 1, FILT_BANDS)
    ang = w_ang[:, None] * bands[None, :]
    z = np.zeros((L, LANE), np.float64)
    z[:, 0:1] = t
    z[:, 1:1 + FILT_BANDS] = np.cos(ang)
    z[:, 1 + FILT_BANDS:1 + 2 * FILT_BANDS] = -np.sin(ang)
    return jnp.asarray(z, F32)


def _filters(L, w1p, b1, fr, w2, b2, w3):
    tl = min(L, 1024)
    z = _filter_embedding(L)
    deltas = jnp.asarray(np.abs(np.linspace(MIN_DECAY, MAX_DECAY, HY_CH))[None, :], F32)
    return pl.pallas_call(
        _filter_kernel,
        out_shape=(jax.ShapeDtypeStruct((2, L, HY_CH), BF16), jax.ShapeDtypeStruct((1, HY_CH), F32)),
        grid=(L // tl,),
        in_specs=[pl.BlockSpec((tl, LANE), lambda i: (i, 0)),
                  _const_spec(w1p.shape), _const_spec(b1.shape), _const_spec(fr.shape),
                  _const_spec(w2.shape), _const_spec(b2.shape), _const_spec(w3.shape),
                  _const_spec(deltas.shape)],
        out_specs=(pl.BlockSpec((2, tl, HY_CH), lambda i: (0, i, 0)),
                   pl.BlockSpec((1, HY_CH), lambda i: (0, 0))),
        compiler_params=_cparams(("arbitrary",)),
        name="l0_hyena_filters",
    )(z, w1p, b1, fr, w2, b2, w3, deltas)


def _dft_tables(kind, L, ti):
    ni = L // ti
    i = np.arange(ti, dtype=np.int64)[:, None]
    big = (np.arange(ni, dtype=np.int64) * ti)[:, None]
    c = np.arange(L, dtype=np.int64)[None, :]
    if kind == "hy_fwd":
        period = 4 * L
        base_idx = (2 * i + 1) * c
        r_idx = 2 * big * c
        scale = 1.0
    elif kind == "hy_inv":
        period = 4 * L
        base_idx = (2 * c + 1) * i
        r_idx = (2 * c + 1) * big
        scale = 1.0 / L
    else:
        period = L
        base_idx = i * c
        r_idx = big * c
        scale = 1.0 / math.sqrt(L * FN_GROUP_CH)
    ab = 2.0 * np.pi * (base_idx % period) / period
    ar = 2.0 * np.pi * (r_idx % period) / period
    return (jnp.asarray(np.cos(ab), F32), jnp.asarray(np.sin(ab), F32),
            jnp.asarray(scale * np.cos(ar), F32).reshape(ni, 1, L),
            jnp.asarray(scale * np.sin(ar), F32).reshape(ni, 1, L))


def _dft_kernel(mode, nb, n_x, *refs):
    bc_ref, bs_ref, rc_ref, rs_ref = refs[:4]
    x_refs = refs[4:4 + n_x]
    rest = refs[4 + n_x:]
    p_ref, q_ref = rest[-2], rest[-1]
    j = pl.program_id(2)
    nj = pl.num_programs(2)
    tj = x_refs[0].shape[1]
    if bc_ref.shape[1] == tj:
        bc, bs, rc, rs = bc_ref[...], bs_ref[...], rc_ref[...], rs_ref[...]
    else:
        off = pl.multiple_of(j * tj, tj)
        bc, bs = bc_ref[:, pl.ds(off, tj)], bs_ref[:, pl.ds(off, tj)]
        rc, rs = rc_ref[:, pl.ds(off, tj)], rs_ref[:, pl.ds(off, tj)]
    tc = (bc * rc - bs * rs).astype(BF16)
    ts = (bs * rc + bc * rs).astype(BF16)
    x1_ref = x_refs[0]
    x2_ref = x_refs[-1]

    pq = [(jnp.dot(tc, x1_ref[b].astype(BF16), preferred_element_type=F32),
           jnp.dot(ts, x2_ref[b].astype(BF16), preferred_element_type=F32)) for b in range(nb)]

    @pl.when(j == 0)
    def _():
        for b in range(nb):
            p_ref[b] = pq[b][0]
            q_ref[b] = pq[b][1]

    @pl.when(j > 0)
    def _():
        for b in range(nb):
            p_ref[b] += pq[b][0]
            q_ref[b] += pq[b][1]

    @pl.when(j == nj - 1)
    def _():
        if mode == "filt":
            nrm = rest[0][...]
            kre_ref, kim_ref = rest[1], rest[2]
            kre_ref[...] = (p_ref[0] + p_ref[1]) / nrm
            kim_ref[...] = (q_ref[1] - q_ref[0]) / nrm
        elif mode == "fwdk":
            kre, kim = rest[0][...], rest[1][...]
            yre_ref, yim_ref = rest[2], rest[3]
            for b in range(nb):
                pp, qq = p_ref[b], q_ref[b]
                yre_ref[b] = (pp * kre + qq * kim).astype(BF16)
                yim_ref[b] = (pp * kim - qq * kre).astype(BF16)
        elif mode == "inv":
            u_ref, x0_ref, skip_ref, o_ref = rest[0], rest[1], rest[2], rest[3]
            skip = skip_ref[...]
            for b in range(nb):
                o_ref[b] = ((p_ref[b] - q_ref[b] + u_ref[b] * skip) * x0_ref[b].astype(F32)).astype(BF16)
        else:
            o_ref = rest[0]
            for b in range(nb):
                o_ref[b] = (p_ref[b] - q_ref[b]).astype(BF16)


def _dft(mode, kind, xs, extras, nb):
    B, L, C = xs[0].shape
    ti = min(L, 256)
    tj = min(L, 512)
    bc, bs, rc, rs = _dft_tables(kind, L, ti)
    grid = (B // nb, L // ti, L // tj)
    x_spec = pl.BlockSpec((nb, tj, C), lambda g, i, j: (g, j, 0))
    row_spec = lambda c, dt=None: pl.BlockSpec((nb, ti, c), lambda g, i, j: (g, i, 0))
    in_specs = [pl.BlockSpec((ti, L), lambda g, i, j: (0, 0)),
                pl.BlockSpec((ti, L), lambda g, i, j: (0, 0)),
                pl.BlockSpec((None, 1, L), lambda g, i, j: (i, 0, 0)),
                pl.BlockSpec((None, 1, L), lambda g, i, j: (i, 0, 0))] + [x_spec] * len(xs)
    if mode == "filt":
        in_specs += [pl.BlockSpec((1, HY_CH), lambda g, i, j: (0, 0))]
        out_shape = (jax.ShapeDtypeStruct((L, HY_CH), F32),) * 2
        out_specs = (pl.BlockSpec((ti, HY_CH), lambda g, i, j: (i, 0)),) * 2
    elif mode == "fwdk":
        in_specs += [pl.BlockSpec((ti, HY_CH), lambda g, i, j: (i, 0))] * 2
        out_shape = (jax.ShapeDtypeStruct((B, L, C), BF16),) * 2
        out_specs = (row_spec(C),) * 2
    elif mode == "inv":
        in_specs += [row_spec(C)] * 2 + [pl.BlockSpec((1, C), lambda g, i, j: (0, 0))]
        out_shape = jax.ShapeDtypeStruct((B, L, C), BF16)
        out_specs = row_spec(C)
    else:
        out_shape = jax.ShapeDtypeStruct((B, L, C), BF16)
        out_specs = row_spec(C)
    return pl.pallas_call(
        functools.partial(_dft_kernel, mode, nb, len(xs)),
        out_shape=out_shape,
        grid=grid,
        in_specs=in_specs,
        out_specs=out_specs,
        scratch_shapes=[pltpu.VMEM((nb, ti, C), F32), pltpu.VMEM((nb, ti, C), F32)],
        compiler_params=_cparams(("arbitrary", "arbitrary", "arbitrary")),
        name="dft_" + mode,
    )(bc, bs, rc, rs, *xs, *extras)


FFT_R = 64
HYENA_KF = 16
FNET_KF = 16


def _pack_pairs(x):
    return pltpu.bitcast(x.astype(BF16), jnp.uint32)


def _unpack_pairs(w):
    return pltpu.bitcast(w, BF16)


def _to_blocks(w):
    return jnp.swapaxes(w.reshape(FFT_R, w.shape[-2], w.shape[-1]), 0, 1)


def _from_blocks(ws):
    kf, c = len(ws), ws[0].shape[-1]
    return jnp.swapaxes(jnp.stack(ws, axis=0), 0, 1).reshape(FFT_R // kf, kf, kf, c)


def _lead_in_kernel(g_ref, x_ref, o_ref):
    g = g_ref[...]
    xt = jnp.swapaxes(x_ref[...], 0, 1)
    for j in range(x_ref.shape[1]):
        o_ref[j] = _pack_pairs(jnp.dot(g, xt[j].astype(BF16), preferred_element_type=F32))


def _lead_in(g, x, kf, name):
    nbx, _, _, c = x.shape
    m2 = g.shape[0] // 2
    return pl.pallas_call(
        _lead_in_kernel,
        out_shape=jax.ShapeDtypeStruct((nbx, FFT_R // kf, kf, m2, c), jnp.uint32),
        grid=(nbx, FFT_R // kf),
        in_specs=[pl.BlockSpec(g.shape, lambda b, k: (0, 0)),
                  pl.BlockSpec((None, FFT_R, kf, c), lambda b, k: (b, 0, k, 0))],
        out_specs=pl.BlockSpec((None, None, kf, m2, c), lambda b, k: (b, k, 0, 0, 0)),
        compiler_params=_cparams(("arbitrary", "arbitrary")),
        name=name,
    )(g, x)


def _lead_out_kernel(n_extra, g_ref, w_ref, *rest):
    g = g_ref[...]
    o_ref = rest[-1]
    ys = [jnp.dot(g, _unpack_pairs(w_ref[j]), preferred_element_type=F32) for j in range(w_ref.shape[0])]
    y = jnp.swapaxes(jnp.stack(ys, axis=0), 0, 1)
    if n_extra:
        y = (y + rest[0][...] * rest[2][...]) * rest[1][...].astype(F32)
    o_ref[...] = y.astype(o_ref.dtype)


def _lead_out(g, w, epilogue, name):
    nb, nk, kf, k2, c = w.shape
    blk = pl.BlockSpec((None, FFT_R, kf, c), lambda b, k: (b, 0, k, 0))
    extra_specs = [blk, blk, pl.BlockSpec((1, c), lambda b, k: (0, 0))] if epilogue else []
    return pl.pallas_call(
        functools.partial(_lead_out_kernel, len(epilogue)),
        out_shape=jax.ShapeDtypeStruct((nb, FFT_R, FFT_R, c), BF16),
        grid=(nb, nk),
        in_specs=[pl.BlockSpec(g.shape, lambda b, k: (0, 0)),
                  pl.BlockSpec((None, None, kf, k2, c), lambda b, k: (b, k, 0, 0, 0))] + extra_specs,
        out_specs=blk,
        compiler_params=_cparams(("arbitrary", "arbitrary")),
        name=name,
    )(g, w, *epilogue)


def _interleave(a, b, axis):
    st = np.stack([a, b], axis=axis + 1)
    shape = list(a.shape)
    shape[axis] *= 2
    return st.reshape(shape)


def _hy2_tables():
    L = FFT_R * FFT_R
    n2 = 2 * L
    f1 = np.arange(2 * FFT_R, dtype=np.int64)
    s1 = np.arange(FFT_R, dtype=np.int64)
    th = np.pi * (((2 * f1[:, None] + 1) * s1[None, :]) % (4 * FFT_R)) / (2 * FFT_R)
    ga = _interleave(np.cos(th), -np.sin(th), 0)
    ma = _interleave(np.cos(th).T, -np.sin(th).T, 1) / L
    f2 = np.arange(FFT_R // 2, dtype=np.int64)
    s2 = np.arange(FFT_R, dtype=np.int64)
    idx = ((n2 // FFT_R) * 2 * f2[None, :, None] * s2[None, None, :]
           + (2 * f1[:, None, None] + 1) * s2[None, None, :]) % (2 * n2)
    al = np.pi * idx / n2
    c, s = np.cos(al), np.sin(al)
    nmat = np.concatenate([_interleave(c, s, 2), _interleave(-s, c, 2)], axis=1)
    ct, st = np.transpose(c, (0, 2, 1)), np.transpose(s, (0, 2, 1))
    mmat = _interleave(np.concatenate([ct, -st], axis=2), np.concatenate([st, ct], axis=2), 1)
    bf = lambda a: jnp.asarray(a, F32).astype(BF16)
    return bf(ga), bf(nmat), bf(mmat), bf(ma)


def _hy_mid_kernel(a_ref, n_ref, m_ref, k_ref, e_ref):
    half = FFT_R // 2
    kf = a_ref.shape[-2]
    a = _to_blocks(a_ref[...])
    ts = [jnp.dot(n_ref[j], _unpack_pairs(a[j]), preferred_element_type=F32)
          for j in range(kf)]
    ys = []
    for j in range(kf):
        tr, ti = ts[j][:half], ts[j][half:]
        kr, ki = k_ref[j, 0].astype(F32), k_ref[j, 1].astype(F32)
        ys.append(jnp.concatenate([tr * kr - ti * ki, tr * ki + ti * kr], axis=0).astype(BF16))
    e_ref[...] = _from_blocks([_pack_pairs(jnp.dot(m_ref[j], ys[j], preferred_element_type=F32))
                               for j in range(kf)])


def _hy_mid(a, nmat, mmat, khat):
    nb, nk, kf, _, c = a.shape
    nf1 = 2 * FFT_R
    blk = pl.BlockSpec((None, nk, kf, kf, c), lambda i, b: (b, 0, 0, i, 0))
    return pl.pallas_call(
        _hy_mid_kernel,
        out_shape=jax.ShapeDtypeStruct(a.shape, jnp.uint32),
        grid=(nf1 // kf, nb),
        in_specs=[blk,
                  pl.BlockSpec((kf, FFT_R, 2 * FFT_R), lambda i, b: (i, 0, 0)),
                  pl.BlockSpec((kf, 2 * FFT_R, FFT_R), lambda i, b: (i, 0, 0)),
                  pl.BlockSpec((kf, 2, FFT_R // 2, c), lambda i, b: (i, 0, 0, 0))],
        out_specs=blk,
        compiler_params=_cparams(("arbitrary", "arbitrary")),
        name="l0_hyena_mid",
    )(a, nmat, mmat, khat)


def _hy_kfilt_kernel(a_ref, n_ref, nrm_ref, k_ref):
    half = FFT_R // 2
    nrm = nrm_ref[...]
    af, ab = _to_blocks(a_ref[0]), _to_blocks(a_ref[1])
    for j in range(a_ref.shape[-2]):
        tf = jnp.dot(n_ref[j], _unpack_pairs(af[j]), preferred_element_type=F32)
        tb = jnp.dot(n_ref[j], _unpack_pairs(ab[j]), preferred_element_type=F32)
        k_ref[j, 0] = ((tf[:half] + tb[:half]) / nrm).astype(k_ref.dtype)
        k_ref[j, 1] = ((tf[half:] - tb[half:]) / nrm).astype(k_ref.dtype)


def _hy_kfilt(a, nmat, nrm):
    _, nk, kf, _, c = a.shape
    nf1 = 2 * FFT_R
    return pl.pallas_call(
        _hy_kfilt_kernel,
        out_shape=jax.ShapeDtypeStruct((nf1, 2, FFT_R // 2, c), BF16),
        grid=(nf1 // kf,),
        in_specs=[pl.BlockSpec((2, nk, kf, kf, c), lambda i: (0, 0, 0, i, 0)),
                  pl.BlockSpec((kf, FFT_R, 2 * FFT_R), lambda i: (i, 0, 0)),
                  _const_spec(nrm.shape)],
        out_specs=pl.BlockSpec((kf, 2, FFT_R // 2, c), lambda i: (i, 0, 0, 0)),
        compiler_params=_cparams(("arbitrary",)),
        name="l0_hyena_kfilt",
    )(a, nmat, nrm)


def _hyena_long(u, x0, skip, hfilt, nrm, nb):
    L = FFT_R * FFT_R
    c = u.shape[-1]
    v4 = lambda a, n: a.reshape(n, FFT_R, FFT_R, c)
    ga, nmat, mmat, ma = _hy2_tables()
    khat = _hy_kfilt(_lead_in(ga, v4(hfilt, 2), HYENA_KF, "l0_hyena_fwd_a"), nmat, nrm)
    ee = _hy_mid(_lead_in(ga, v4(u, nb), HYENA_KF, "l0_hyena_fwd_a"), nmat, mmat, khat)
    y = _lead_out(ma, ee, (v4(u, nb), v4(x0, nb), skip), "l0_hyena_inv_a")
    return y.reshape(nb * L, c)


def _fn2_tables():
    L = FFT_R * FFT_R
    r = np.arange(FFT_R, dtype=np.int64)
    idx = (FFT_R * r[None, :, None] * r[None, None, :] + r[None, :, None] * r[:, None, None]) % L
    gm = 2.0 * np.pi * idx / L
    c, s = np.cos(gm), np.sin(gm)
    g1 = _interleave(np.concatenate([c, -s], axis=2), np.concatenate([-s, -c], axis=2), 1)
    dl = 2.0 * np.pi * ((r[:, None] * r[None, :]) % FFT_R) / FFT_R
    g2 = _interleave(np.cos(dl), np.sin(dl), 1) / math.sqrt(L * FN_GROUP_CH)
    bf = lambda a: jnp.asarray(a, F32).astype(BF16)
    return bf(g1), bf(g2)


def _fnet_s1_kernel(x_ref, m_ref, cs_ref, g1_ref, o_ref, zc_ref, zs_ref):
    xs = jnp.swapaxes(x_ref[...], 0, 1).reshape(FNET_KF * FFT_R, D_MODEL)
    m = m_ref[...]
    h = (_ln_plain(xs) * (1.0 + m[:, D_MODEL:2 * D_MODEL]) + m[:, 0:D_MODEL]).astype(BF16)
    cs = cs_ref[...]
    for g in range(D_MODEL // FN_GROUP_CH):
        a = g * FN_GROUP_CH
        z = jnp.dot(h[:, a:a + FN_GROUP_CH], cs, preferred_element_type=F32)
        zc_ref[:, a:a + FN_GROUP_CH] = z[:, :FN_GROUP_CH].astype(BF16)
        zs_ref[:, a:a + FN_GROUP_CH] = z[:, FN_GROUP_CH:].astype(BF16)
    ws = []
    for j in range(FNET_KF):
        r0 = j * FFT_R
        s = jnp.concatenate([zc_ref[r0:r0 + FFT_R, :], zs_ref[r0:r0 + FFT_R, :]], axis=0)
        ws.append(_pack_pairs(jnp.dot(g1_ref[j], s, preferred_element_type=F32)))
    o_ref[...] = _from_blocks(ws)


def _fnet_long(x, mods, mod_base, nb):
    L = FFT_R * FFT_R
    d = D_MODEL
    g1, g2 = _fn2_tables()
    cs = _group_dft_table()
    kf = FNET_KF
    bb = pl.pallas_call(
        _fnet_s1_kernel,
        out_shape=jax.ShapeDtypeStruct((nb, FFT_R // kf, kf, FFT_R, d), jnp.uint32),
        grid=(nb, FFT_R // kf),
        in_specs=[pl.BlockSpec((None, FFT_R, kf, d), lambda b, k: (b, 0, k, 0)),
                  pl.BlockSpec((None, 1, 6 * d), lambda b, k: (mod_base + b, 0, 0)),
                  pl.BlockSpec(cs.shape, lambda b, k: (0, 0)),
                  pl.BlockSpec((kf, 2 * FFT_R, 2 * FFT_R), lambda b, k: (k, 0, 0))],
        out_specs=pl.BlockSpec((None, FFT_R // kf, kf, kf, d), lambda b, k: (b, 0, 0, k, 0)),
        scratch_shapes=[pltpu.VMEM((kf * FFT_R, d), BF16), pltpu.VMEM((kf * FFT_R, d), BF16)],
        compiler_params=_cparams(("arbitrary", "arbitrary")),
        name="l1_fnet_stage1",
    )(x.reshape(nb, FFT_R, FFT_R, d), mods, cs, g1)
    y = _lead_out(g2, bb, [], "l1_fnet_stage2")
    return y.reshape(nb * L, d)


def _group_dft_table():
    g = FN_GROUP_CH
    jk = (np.arange(g, dtype=np.int64)[:, None] * np.arange(g, dtype=np.int64)[None, :]) % g
    ang = 2.0 * np.pi * jk / g
    return jnp.asarray(np.concatenate([np.cos(ang), np.sin(ang)], axis=1), F32).astype(BF16)


def _fnet_front_kernel(x_ref, m_ref, cs_ref, zc_ref, zs_ref):
    m = m_ref[...]
    h = (_ln_plain(x_ref[...]) * (1.0 + m[:, D_MODEL:2 * D_MODEL]) + m[:, 0:D_MODEL]).astype(BF16)
    cs = cs_ref[...]
    for g in range(D_MODEL // FN_GROUP_CH):
        a = g * FN_GROUP_CH
        z = jnp.dot(h[:, a:a + FN_GROUP_CH], cs, preferred_element_type=F32)
        zc_ref[:, a:a + FN_GROUP_CH] = z[:, :FN_GROUP_CH].astype(BF16)
        zs_ref[:, a:a + FN_GROUP_CH] = z[:, FN_GROUP_CH:].astype(BF16)


def _fnet_front(x, mods, mod_base, tiles_per_mod):
    t = x.shape[0]
    tm = 2 * ROW_TILE
    tiles_per_mod = max(tiles_per_mod // 2, 1)
    cs = _group_dft_table()
    return pl.pallas_call(
        _fnet_front_kernel,
        out_shape=(jax.ShapeDtypeStruct((t, D_MODEL), BF16),) * 2,
        grid=(t // tm,),
        in_specs=[pl.BlockSpec((tm, D_MODEL), lambda i: (i, 0)),
                  _mod_spec(mod_base, tiles_per_mod),
                  _const_spec(cs.shape)],
        out_specs=(pl.BlockSpec((tm, D_MODEL), lambda i: (i, 0)),) * 2,
        compiler_params=_cparams(("arbitrary",)),
        name="l1_fnet_front",
    )(x, mods, cs)


def _post_kernel(n_a, *refs):
    x_ref, m_ref = refs[0], refs[1]
    a_refs = refs[2:2 + n_a]
    wo_refs = refs[2 + n_a:2 + 2 * n_a]
    g1_ref, b1_ref, w1_ref, w2_ref, g2_ref, b2_ref, o_ref = refs[2 + 2 * n_a:]
    m = m_ref[...]
    d = D_MODEL
    tm = x_ref.shape[0]
    halves = [(r, r + tm // POST_SPLIT) for r in range(0, tm, tm // POST_SPLIT)]
    outs = []
    for r0, r1 in halves:
        out = _bdot(a_refs[0][r0:r1, :], wo_refs[0][...])
        for a_ref, wo_ref in zip(a_refs[1:], wo_refs[1:]):
            out += _bdot(a_ref[r0:r1, :], wo_ref[...])
        outs.append(out)
    x1s, hs = [], []
    for (r0, r1), out in zip(halves, outs):
        x1 = _ln_plain(ALPHA * x_ref[r0:r1, :] + m[:, 2 * d:3 * d] * out) * g1_ref[...] + b1_ref[...]
        x1s.append(x1)
        hs.append((_ln_plain(x1) * (1.0 + m[:, 4 * d:5 * d]) + m[:, 3 * d:4 * d]).astype(BF16))
    accs = []
    n_c = D_FF // d

    def up(h, c):
        hc = jnp.maximum(jnp.dot(h, w1_ref[:, c * d:(c + 1) * d], preferred_element_type=F32), 0.0)
        return (hc * hc).astype(BF16)

    for h in hs:
        acc = None
        nxt = up(h, 0)
        for c in range(n_c):
            cur = nxt
            if c + 1 < n_c:
                nxt = up(h, c + 1)
            part = jnp.dot(cur, w2_ref[c * d:(c + 1) * d, :], preferred_element_type=F32)
            acc = part if acc is None else acc + part
        accs.append(acc)
    for (r0, r1), x1, acc in zip(halves, x1s, accs):
        o_ref[r0:r1, :] = _ln_plain(ALPHA * x1 + m[:, 5 * d:6 * d] * acc) * g2_ref[...] + b2_ref[...]


def _post(x, mods, mod_base, tiles_per_mod, a_list, wo_list, g1, b1, w1, w2, g2, b2):
    t = x.shape[0]
    tm = POST_TILE
    row = lambda c: pl.BlockSpec((tm, c), lambda i: (i, 0))
    once = lambda v: pl.BlockSpec(v.shape, lambda i: (0,) * v.ndim, pipeline_mode=pl.Buffered(1))
    in_specs = ([row(D_MODEL), _mod_spec(mod_base, tiles_per_mod * ROW_TILE // tm)]
                + [row(a.shape[1]) for a in a_list]
                + [once(w) for w in wo_list]
                + [once(v) for v in (g1, b1, w1, w2, g2, b2)])
    return pl.pallas_call(
        functools.partial(_post_kernel, len(a_list)),
        out_shape=jax.ShapeDtypeStruct((t, D_MODEL), F32),
        grid=(t // tm,),
        in_specs=in_specs,
        out_specs=row(D_MODEL),
        compiler_params=_cparams(("arbitrary",)),
        name="post_mlp",
    )(x, mods, *a_list, *wo_list, g1, b1, w1, w2, g2, b2)


def _rot_cols(w):
    parts = []
    for seg in range(2):
        o = seg * 32
        parts += [-w[:, o + 16:o + 32], w[:, o:o + 16]]
    return jnp.concatenate(parts, axis=1)


def _pad_cols(w, n):
    return jnp.pad(w, ((0, 0), (0, n - w.shape[1])))


def _block_diag2(w):
    z = jnp.zeros_like(w)
    return jnp.concatenate([jnp.concatenate([w, z], axis=1), jnp.concatenate([z, w], axis=1)], axis=0)


def _rope_tables(L):
    rows = L // GRID_W
    row = np.repeat(np.arange(rows, dtype=np.float64), GRID_W)
    col = np.tile(np.arange(GRID_W, dtype=np.float64), rows)
    half = QK_ROPE // 2
    inv = 1.0 / (ROPE_THETA ** (np.arange(0, half, 2, dtype=np.float64) / half))
    ar = row[:, None] * inv[None, :]
    ac = col[:, None] * inv[None, :]
    ang = np.concatenate([ar, ar, ac, ac], axis=1)
    cos = np.concatenate([np.cos(ang), np.ones_like(ang)], axis=1)
    sin = np.concatenate([np.sin(ang), np.zeros_like(ang)], axis=1)
    return jnp.asarray(cos, F32), jnp.asarray(sin, F32)


def kernel(x_prompt, x_sample, cache_l0_ckv, cache_l0_krope, c, c_ctx, l0_ada_w, l0_ada_b, l0_w_in, l0_conv_w, l0_conv_b, l0_hf_w1, l0_hf_b1, l0_hf_freq, l0_hf_w2, l0_hf_b2, l0_hf_w3, l0_hf_skip, l0_q_norm, l0_q_up, l0_kv_norm, l0_kv_up, l0_w_out, l0_ln1_g, l0_ln1_b, l0_mlp_w1, l0_mlp_w2, l0_ln2_g, l0_ln2_b, l1_ada_w, l1_ada_b, l1_w_out, l1_ln1_g, l1_ln1_b, l1_mlp_w1, l1_mlp_w2, l1_ln2_g, l1_ln2_b):
    nbc, lc, d = x_prompt.shape
    nbs, ls, _ = x_sample.shape
    past = cache_l0_ckv.shape[1]
    tm = ROW_TILE
    row1 = lambda v: v.reshape(1, -1)

    cond8 = jnp.concatenate([c_ctx[None, :], c, jnp.zeros((8 - 1 - nbs, d), F32)], axis=0)
    mods0 = _modulation(cond8, l0_ada_w, l0_ada_b)
    mods1 = _modulation(cond8, l1_ada_w, l1_ada_b)

    kpe_w = l0_w_in[:, 1920:1984]
    win = jnp.concatenate([l0_w_in[:, :1920], _pad_cols(kpe_w, LANE), _pad_cols(_rot_cols(kpe_w), LANE)],
                          axis=1).astype(BF16)
    dh = QK_NOPE + QK_ROPE
    q_nope = [l0_q_up[:, h * dh:h * dh + QK_NOPE] for h in range(MLA_HEADS)]
    q_pe = [l0_q_up[:, h * dh + QK_NOPE:(h + 1) * dh] for h in range(MLA_HEADS)]
    qup = jnp.concatenate(q_nope + [_pad_cols(w, LANE) for w in q_pe]
                          + [_pad_cols(_rot_cols(w), LANE) for w in q_pe], axis=1).astype(BF16)
    kvup = l0_kv_up.astype(BF16)
    front_w = (win, row1(l0_q_norm), qup, row1(l0_kv_norm), kvup, l0_conv_w, row1(l0_conv_b))
    skip = row1(l0_hf_skip)
    w1p = jnp.pad(l0_hf_w1, ((0, LANE - l0_hf_w1.shape[0]), (0, 0)))
    two = lambda v: jnp.tile(row1(v), (1, 2))
    filt_w = (_block_diag2(w1p), two(l0_hf_b1), two(l0_hf_freq), _block_diag2(l0_hf_w2), two(l0_hf_b2),
              _block_diag2(l0_hf_w3).astype(BF16))

    xc = x_prompt.reshape(nbc * lc, d)
    xs = x_sample.reshape(nbs * ls, d)
    groups = (
        dict(x=xc, nb=nbc, L=lc, mod_base=0, tiles_per_mod=nbc * lc // tm, dft_nb=8, tq=lc, hps=MLA_HEADS),
        dict(x=xs, nb=nbs, L=ls, mod_base=1, tiles_per_mod=ls // tm, dft_nb=nbs, tq=1024, hps=1),
    )
    ones_tab = (jnp.ones((FRONT_TILE, LANE), F32), jnp.zeros((FRONT_TILE, LANE), F32))

    outs = []
    ctx_ckv = ctx_krope = None
    mixed = []
    mlp_w = None
    for gi, g in enumerate(groups):
        nb, L = g["nb"], g["L"]
        latent = gi == 1
        cos, sin = _rope_tables(L) if latent else ones_tab
        u, x0, q, k, vt, kvn, kpe = _front(g["x"], mods0, g["mod_base"], g["tiles_per_mod"] * tm, front_w,
                                           cos, sin, latent, L, min(L if latent else nb * L, FRONT_TILE))
        if latent:
            extra = _cache_kv(cache_l0_ckv.reshape(nbs * past, KV_LORA),
                              _pad_cols(cache_l0_krope.reshape(nbs * past, QK_ROPE), LANE), kvup)
            y_mla, mlp_w = _attention(q, k, vt, extra, nb, L, g["tq"], g["hps"],
                                      (l0_mlp_w1, l0_mlp_w2, l1_mlp_w1, l1_mlp_w2, l0_w_out, l1_w_out))
        else:
            ctx_ckv = kvn.reshape(nb, L, KV_LORA)
            ctx_krope = kpe.reshape(nb, L, QK_ROPE)
            y_mla, _ = _attention(q, k, vt, None, nb, L, g["tq"], g["hps"])
        mixed.append((u, x0, y_mla))

    for g, (u, x0, y_mla) in zip(groups, mixed):
        nb, L = g["nb"], g["L"]
        two_stage = L == FFT_R * FFT_R
        hfilt, hnorm = _filters(L, *filt_w)
        if two_stage:
            y_hy = _hyena_long(u, x0, skip, hfilt, hnorm, nb)
        else:
            kre, kim = _dft("filt", "hy_fwd", [hfilt], [hnorm], 2)
            sh = (nb, L, HY_CH)
            yre, yim = _dft("fwdk", "hy_fwd", [u.reshape(sh)], [kre, kim], g["dft_nb"])
            y_hy = _dft("inv", "hy_inv", [yre, yim], [u.reshape(sh), x0.reshape(sh), skip], g["dft_nb"])
            y_hy = y_hy.reshape(nb * L, HY_CH)

        x1 = _post(g["x"], mods0, g["mod_base"], g["tiles_per_mod"], [y_hy, y_mla],
                   [mlp_w[4][:HY_CH], mlp_w[4][HY_CH:]], row1(l0_ln1_g), row1(l0_ln1_b),
                   mlp_w[0], mlp_w[1], row1(l0_ln2_g), row1(l0_ln2_b))

        if L == FFT_R * FFT_R:
            yf = _fnet_long(x1, mods1, g["mod_base"], nb)
        else:
            zc, zs = _fnet_front(x1, mods1, g["mod_base"], g["tiles_per_mod"])
            sh = (nb, L, d)
            yf = _dft("fnet", "fnet", [zc.reshape(sh), zs.reshape(sh)], [], g["dft_nb"] // 2)
            yf = yf.reshape(nb * L, d)
        x2 = _post(x1, mods1, g["mod_base"], g["tiles_per_mod"], [yf],
                   [mlp_w[5]], row1(l1_ln1_g), row1(l1_ln1_b),
                   mlp_w[2], mlp_w[3], row1(l1_ln2_g), row1(l1_ln2_b))
        outs.append(x2.reshape(nb, L, d))

    return (outs[0], outs[1], ctx_ckv, ctx_krope)
```
